```python
import math
import jax, jax.numpy as jnp
from jax import lax
import numpy as np

D_MODEL = 1024
BATCH = 8
SEQ = 8192
DEPTH = 1

N_MEM = 256
D_LRU = D_MODEL // 2
D_CONF = D_MODEL // 2
D_MIX = D_LRU + D_CONF
LRU_HEADS = 8
LRU_HD = D_LRU // LRU_HEADS
LRU_CONV = 4
RG_C = 8.0
CONF_CONV = 31
CONF_GROUPS = 8
XA_HEADS = 4
XA_HD = D_MODEL // XA_HEADS
D_FF = 3 * D_MODEL
FFN_CONV = 3
EPS = 1e-6

kernel_name = "hybrid_rglru_conformer_xattn_convffn"


def rms_norm(x, g):
    xf = x.astype(jnp.float32)
    y = xf * lax.rsqrt(jnp.mean(xf * xf, axis=-1, keepdims=True) + EPS)
    return y.astype(x.dtype) * g


def layer_norm(x, g, b):
    xf = x.astype(jnp.float32)
    mu = jnp.mean(xf, axis=-1, keepdims=True)
    xc = xf - mu
    var = jnp.mean(xc * xc, axis=-1, keepdims=True)
    return (xc * lax.rsqrt(var + EPS)).astype(x.dtype) * g + b


def causal_dwconv(x, w, b):
    k = w.shape[0]
    y = lax.conv_general_dilated(
        x, w[:, None, :], window_strides=(1,), padding=[(k - 1, 0)],
        dimension_numbers=("NWC", "WIO", "NWC"), feature_group_count=x.shape[-1])
    return y + b


def rg_lru(x, w_a, b_a, w_x, b_x, lam):
    bsz, s, c = x.shape
    xh = x.reshape(bsz, s, LRU_HEADS, LRU_HD)
    r = jax.nn.sigmoid(jnp.einsum("bshi,hij->bshj", xh, w_a).reshape(bsz, s, c) + b_a)
    i = jax.nn.sigmoid(jnp.einsum("bshi,hij->bshj", xh, w_x).reshape(bsz, s, c) + b_x)
    log_a = -RG_C * r.astype(jnp.float32) * jax.nn.softplus(-lam.astype(jnp.float32))
    a = jnp.exp(log_a)
    mult = jnp.sqrt(-jnp.expm1(2.0 * log_a))
    u = mult * (i * x).astype(jnp.float32)

    def combine(left, right):
        a1, b1 = left
        a2, b2 = right
        return a1 * a2, a2 * b1 + b2

    _, h = lax.associative_scan(combine, (a, u), axis=1)
    return h.astype(x.dtype)


def _fwd_setup_inputs(seed: int = 0) -> dict:
    key = jax.random.key(seed)
    ks = iter(jax.random.split(key, 40))
    f32 = jnp.float32

    def nrm(shape, scale):
        return jax.random.normal(next(ks), shape, f32) * scale

    def gain(shape):
        return 1.0 + 0.01 * jax.random.normal(next(ks), shape, f32)

    L = DEPTH
    x = jax.random.normal(next(ks), (BATCH, SEQ, D_MODEL), f32)
    mem = jax.random.normal(next(ks), (BATCH, N_MEM, D_MODEL), f32)
    a_c = jax.random.uniform(next(ks), (L, D_LRU), f32, 0.9, 0.999)
    sig = a_c ** (1.0 / RG_C)
    lam = jnp.log(sig) - jnp.log1p(-sig)
    return {
        "x": x,
        "mem": mem,
        "mix_norm_g": gain((L, D_MODEL)),
        "w_in": nrm((L, D_MODEL, 2 * D_MIX), D_MODEL ** -0.5),
        "lru_conv_w": nrm((L, LRU_CONV, D_LRU), LRU_CONV ** -0.5),
        "lru_conv_b": nrm((L, D_LRU), 0.01),
        "lru_w_a": nrm((L, LRU_HEADS, LRU_HD, LRU_HD), LRU_HD ** -0.5),
        "lru_b_a": nrm((L, D_LRU), 0.01),
        "lru_w_x": nrm((L, LRU_HEADS, LRU_HD, LRU_HD), LRU_HD ** -0.5),
        "lru_b_x": nrm((L, D_LRU), 0.01),
        "lru_lambda": lam,
        "conf_conv_w": nrm((L, CONF_CONV, D_CONF), CONF_CONV ** -0.5),
        "conf_conv_b": nrm((L, D_CONF), 0.01),
        "conf_ln_g": gain((L, D_CONF)),
        "conf_ln_b": nrm((L, D_CONF), 0.01),
        "w_out": nrm((L, D_MIX, D_MODEL), D_MIX ** -0.5),
        "xa_norm_g": gain((L, D_MODEL)),
        "mem_norm_g": gain((L, D_MODEL)),
        "w_q": nrm((L, D_MODEL, D_MODEL), D_MODEL ** -0.5),
        "w_kv": nrm((L, D_MODEL, 2 * D_MODEL), D_MODEL ** -0.5),
        "w_o": nrm((L, D_MODEL, D_MODEL), D_MODEL ** -0.5),
        "ffn_norm_g": gain((L, D_MODEL)),
        "w_up": nrm((L, D_MODEL, 2 * D_FF), D_MODEL ** -0.5),
        "ffn_conv_w": nrm((L, FFN_CONV, D_FF), FFN_CONV ** -0.5),
        "ffn_conv_b": nrm((L, D_FF), 0.01),
        "w_down": nrm((L, D_FF, D_MODEL), D_FF ** -0.5),
        "final_norm_g": gain((D_MODEL,)),
    }


def _fwd_reference(x, mem, mix_norm_g, w_in, lru_conv_w, lru_conv_b, lru_w_a, lru_b_a,
              lru_w_x, lru_b_x, lru_lambda, conf_conv_w, conf_conv_b, conf_ln_g,
              conf_ln_b, w_out, xa_norm_g, mem_norm_g, w_q, w_kv, w_o, ffn_norm_g,
              w_up, ffn_conv_w, ffn_conv_b, w_down, final_norm_g):
    bsz, s, d = x.shape
    m_len = mem.shape[1]
    for l in range(DEPTH):
        h = rms_norm(x, mix_norm_g[l])
        z = h @ w_in[l]
        lru_x, lru_gate, conf_a, conf_b = jnp.split(
            z, [D_LRU, 2 * D_LRU, 2 * D_LRU + D_CONF], axis=-1)
        lru_x = causal_dwconv(lru_x, lru_conv_w[l], lru_conv_b[l])
        y_lru = rg_lru(lru_x, lru_w_a[l], lru_b_a[l], lru_w_x[l], lru_b_x[l],
                       lru_lambda[l]) * jax.nn.gelu(lru_gate, approximate=True)
        c = conf_a * jax.nn.sigmoid(conf_b)
        c = causal_dwconv(c, conf_conv_w[l], conf_conv_b[l])
        c = jax.nn.silu(layer_norm(c, conf_ln_g[l], conf_ln_b[l]))
        y = jnp.concatenate([y_lru, c], axis=-1) @ w_out[l]
        x = x + y

        h = rms_norm(x, xa_norm_g[l])
        m = rms_norm(mem, mem_norm_g[l])
        q = (h @ w_q[l]).reshape(bsz, s, XA_HEADS, XA_HD)
        kv = m @ w_kv[l]
        k, v = jnp.split(kv, 2, axis=-1)
        k = k.reshape(bsz, m_len, XA_HEADS, XA_HD)
        v = v.reshape(bsz, m_len, XA_HEADS, XA_HD)
        scores = jnp.einsum("bshd,bmhd->bhsm", q, k).astype(jnp.float32) * (XA_HD ** -0.5)
        p = jax.nn.softmax(scores, axis=-1).astype(v.dtype)
        o = jnp.einsum("bhsm,bmhd->bshd", p, v).reshape(bsz, s, d)
        x = x + o @ w_o[l]

        h = rms_norm(x, ffn_norm_g[l])
        gu = h @ w_up[l]
        g, u = jnp.split(gu, 2, axis=-1)
        g = causal_dwconv(g, ffn_conv_w[l], ffn_conv_b[l])
        x = x + (jax.nn.gelu(g, approximate=True) * u) @ w_down[l]
    return rms_norm(x, final_norm_g)


import jax as _jax
import jax.numpy as _jnp

TWIN_FORMAT = 'train_step'
FWD_PARAMS = ['x', 'mem', 'mix_norm_g', 'w_in', 'lru_conv_w', 'lru_conv_b', 'lru_w_a', 'lru_b_a', 'lru_w_x', 'lru_b_x', 'lru_lambda', 'conf_conv_w', 'conf_conv_b', 'conf_ln_g', 'conf_ln_b', 'w_out', 'xa_norm_g', 'mem_norm_g', 'w_q', 'w_kv', 'w_o', 'ffn_norm_g', 'w_up', 'ffn_conv_w', 'ffn_conv_b', 'w_down', 'final_norm_g']
TWIN_WEIGHTS = ['mix_norm_g', 'w_in', 'lru_conv_w', 'lru_conv_b', 'lru_w_a', 'lru_b_a', 'lru_w_x', 'lru_b_x', 'lru_lambda', 'conf_conv_w', 'conf_conv_b', 'conf_ln_g', 'conf_ln_b', 'w_out', 'xa_norm_g', 'mem_norm_g', 'w_q', 'w_kv', 'w_o', 'ffn_norm_g', 'w_up', 'ffn_conv_w', 'ffn_conv_b', 'w_down', 'final_norm_g']
TWIN_DIFF_INPUT = 'x'
TWIN_INPUTS = ['x', 'mem', 'mix_norm_g', 'w_in', 'lru_conv_w', 'lru_conv_b', 'lru_w_a', 'lru_b_a', 'lru_w_x', 'lru_b_x', 'lru_lambda', 'conf_conv_w', 'conf_conv_b', 'conf_ln_g', 'conf_ln_b', 'w_out', 'xa_norm_g', 'mem_norm_g', 'w_q', 'w_kv', 'w_o', 'ffn_norm_g', 'w_up', 'ffn_conv_w', 'ffn_conv_b', 'w_down', 'final_norm_g', 'loss_target', 'm_mix_norm_g', 'm_w_in', 'm_lru_conv_w', 'm_lru_conv_b', 'm_lru_w_a', 'm_lru_b_a', 'm_lru_w_x', 'm_lru_b_x', 'm_lru_lambda', 'm_conf_conv_w', 'm_conf_conv_b', 'm_conf_ln_g', 'm_conf_ln_b', 'm_w_out', 'm_xa_norm_g', 'm_mem_norm_g', 'm_w_q', 'm_w_kv', 'm_w_o', 'm_ffn_norm_g', 'm_w_up', 'm_ffn_conv_w', 'm_ffn_conv_b', 'm_w_down', 'm_final_norm_g', 'v_mix_norm_g', 'v_w_in', 'v_lru_conv_w', 'v_lru_conv_b', 'v_lru_w_a', 'v_lru_b_a', 'v_lru_w_x', 'v_lru_b_x', 'v_lru_lambda', 'v_conf_conv_w', 'v_conf_conv_b', 'v_conf_ln_g', 'v_conf_ln_b', 'v_w_out', 'v_xa_norm_g', 'v_mem_norm_g', 'v_w_q', 'v_w_kv', 'v_w_o', 'v_ffn_norm_g', 'v_w_up', 'v_ffn_conv_w', 'v_ffn_conv_b', 'v_w_down', 'v_final_norm_g']
TWIN_OUTPUTS = ['loss', 'grad_x', 'grad_mix_norm_g', 'grad_w_in', 'grad_lru_conv_w', 'grad_lru_conv_b', 'grad_lru_w_a', 'grad_lru_b_a', 'grad_lru_w_x', 'grad_lru_b_x', 'grad_lru_lambda', 'grad_conf_conv_w', 'grad_conf_conv_b', 'grad_conf_ln_g', 'grad_conf_ln_b', 'grad_w_out', 'grad_xa_norm_g', 'grad_mem_norm_g', 'grad_w_q', 'grad_w_kv', 'grad_w_o', 'grad_ffn_norm_g', 'grad_w_up', 'grad_ffn_conv_w', 'grad_ffn_conv_b', 'grad_w_down', 'grad_final_norm_g', 'delta_mix_norm_g', 'delta_w_in', 'delta_lru_conv_w', 'delta_lru_conv_b', 'delta_lru_w_a', 'delta_lru_b_a', 'delta_lru_w_x', 'delta_lru_b_x', 'delta_lru_lambda', 'delta_conf_conv_w', 'delta_conf_conv_b', 'delta_conf_ln_g', 'delta_conf_ln_b', 'delta_w_out', 'delta_xa_norm_g', 'delta_mem_norm_g', 'delta_w_q', 'delta_w_kv', 'delta_w_o', 'delta_ffn_norm_g', 'delta_w_up', 'delta_ffn_conv_w', 'delta_ffn_conv_b', 'delta_w_down', 'delta_final_norm_g', 'new_m_mix_norm_g', 'new_m_w_in', 'new_m_lru_conv_w', 'new_m_lru_conv_b', 'new_m_lru_w_a', 'new_m_lru_b_a', 'new_m_lru_w_x', 'new_m_lru_b_x', 'new_m_lru_lambda', 'new_m_conf_conv_w', 'new_m_conf_conv_b', 'new_m_conf_ln_g', 'new_m_conf_ln_b', 'new_m_w_out', 'new_m_xa_norm_g', 'new_m_mem_norm_g', 'new_m_w_q', 'new_m_w_kv', 'new_m_w_o', 'new_m_ffn_norm_g', 'new_m_w_up', 'new_m_ffn_conv_w', 'new_m_ffn_conv_b', 'new_m_w_down', 'new_m_final_norm_g', 'new_v_mix_norm_g', 'new_v_w_in', 'new_v_lru_conv_w', 'new_v_lru_conv_b', 'new_v_lru_w_a', 'new_v_lru_b_a', 'new_v_lru_w_x', 'new_v_lru_b_x', 'new_v_lru_lambda', 'new_v_conf_conv_w', 'new_v_conf_conv_b', 'new_v_conf_ln_g', 'new_v_conf_ln_b', 'new_v_w_out', 'new_v_xa_norm_g', 'new_v_mem_norm_g', 'new_v_w_q', 'new_v_w_kv', 'new_v_w_o', 'new_v_ffn_norm_g', 'new_v_w_up', 'new_v_ffn_conv_w', 'new_v_ffn_conv_b', 'new_v_w_down', 'new_v_final_norm_g']
TWIN_LEAF_KINDS = {'loss': 'loss', 'grad_x': 'grad_x', 'grad_mix_norm_g': 'grad_w', 'grad_w_in': 'grad_w', 'grad_lru_conv_w': 'grad_w', 'grad_lru_conv_b': 'grad_w', 'grad_lru_w_a': 'grad_w', 'grad_lru_b_a': 'grad_w', 'grad_lru_w_x': 'grad_w', 'grad_lru_b_x': 'grad_w', 'grad_lru_lambda': 'grad_w', 'grad_conf_conv_w': 'grad_w', 'grad_conf_conv_b': 'grad_w', 'grad_conf_ln_g': 'grad_w', 'grad_conf_ln_b': 'grad_w', 'grad_w_out': 'grad_w', 'grad_xa_norm_g': 'grad_w', 'grad_mem_norm_g': 'grad_w', 'grad_w_q': 'grad_w', 'grad_w_kv': 'grad_w', 'grad_w_o': 'grad_w', 'grad_ffn_norm_g': 'grad_w', 'grad_w_up': 'grad_w', 'grad_ffn_conv_w': 'grad_w', 'grad_ffn_conv_b': 'grad_w', 'grad_w_down': 'grad_w', 'grad_final_norm_g': 'grad_w', 'delta_mix_norm_g': 'delta_w', 'delta_w_in': 'delta_w', 'delta_lru_conv_w': 'delta_w', 'delta_lru_conv_b': 'delta_w', 'delta_lru_w_a': 'delta_w', 'delta_lru_b_a': 'delta_w', 'delta_lru_w_x': 'delta_w', 'delta_lru_b_x': 'delta_w', 'delta_lru_lambda': 'delta_w', 'delta_conf_conv_w': 'delta_w', 'delta_conf_conv_b': 'delta_w', 'delta_conf_ln_g': 'delta_w', 'delta_conf_ln_b': 'delta_w', 'delta_w_out': 'delta_w', 'delta_xa_norm_g': 'delta_w', 'delta_mem_norm_g': 'delta_w', 'delta_w_q': 'delta_w', 'delta_w_kv': 'delta_w', 'delta_w_o': 'delta_w', 'delta_ffn_norm_g': 'delta_w', 'delta_w_up': 'delta_w', 'delta_ffn_conv_w': 'delta_w', 'delta_ffn_conv_b': 'delta_w', 'delta_w_down': 'delta_w', 'delta_final_norm_g': 'delta_w', 'new_m_mix_norm_g': 'new_m', 'new_m_w_in': 'new_m', 'new_m_lru_conv_w': 'new_m', 'new_m_lru_conv_b': 'new_m', 'new_m_lru_w_a': 'new_m', 'new_m_lru_b_a': 'new_m', 'new_m_lru_w_x': 'new_m', 'new_m_lru_b_x': 'new_m', 'new_m_lru_lambda': 'new_m', 'new_m_conf_conv_w': 'new_m', 'new_m_conf_conv_b': 'new_m', 'new_m_conf_ln_g': 'new_m', 'new_m_conf_ln_b': 'new_m', 'new_m_w_out': 'new_m', 'new_m_xa_norm_g': 'new_m', 'new_m_mem_norm_g': 'new_m', 'new_m_w_q': 'new_m', 'new_m_w_kv': 'new_m', 'new_m_w_o': 'new_m', 'new_m_ffn_norm_g': 'new_m', 'new_m_w_up': 'new_m', 'new_m_ffn_conv_w': 'new_m', 'new_m_ffn_conv_b': 'new_m', 'new_m_w_down': 'new_m', 'new_m_final_norm_g': 'new_m', 'new_v_mix_norm_g': 'new_v', 'new_v_w_in': 'new_v', 'new_v_lru_conv_w': 'new_v', 'new_v_lru_conv_b': 'new_v', 'new_v_lru_w_a': 'new_v', 'new_v_lru_b_a': 'new_v', 'new_v_lru_w_x': 'new_v', 'new_v_lru_b_x': 'new_v', 'new_v_lru_lambda': 'new_v', 'new_v_conf_conv_w': 'new_v', 'new_v_conf_conv_b': 'new_v', 'new_v_conf_ln_g': 'new_v', 'new_v_conf_ln_b': 'new_v', 'new_v_w_out': 'new_v', 'new_v_xa_norm_g': 'new_v', 'new_v_mem_norm_g': 'new_v', 'new_v_w_q': 'new_v', 'new_v_w_kv': 'new_v', 'new_v_w_o': 'new_v', 'new_v_ffn_norm_g': 'new_v', 'new_v_w_up': 'new_v', 'new_v_ffn_conv_w': 'new_v', 'new_v_ffn_conv_b': 'new_v', 'new_v_w_down': 'new_v', 'new_v_final_norm_g': 'new_v'}


def _forward(args):
    return _fwd_reference(*[args[k] for k in FWD_PARAMS])


def _output_shape():
    def fwd():
        inp = _fwd_setup_inputs(0)
        return _fwd_reference(*[inp[k] for k in FWD_PARAMS])
    out = _jax.eval_shape(fwd)
    return out.shape, out.dtype

N_MICROBATCH = 1
ADAM_LR = 0.001
ADAM_B1 = 0.9
ADAM_B2 = 0.999
ADAM_EPS = 1e-08
ADAM_WD = 0.01
ADAM_STEP = 10
PER_EXAMPLE_BATCH_AXIS = {'x': 0, 'mem': 0, 'loss_target': 0}
SHARED_INPUTS = []
_WEIGHT_DTYPES = {'mix_norm_g': _jnp.float32, 'w_in': _jnp.float32, 'lru_conv_w': _jnp.float32, 'lru_conv_b': _jnp.float32, 'lru_w_a': _jnp.float32, 'lru_b_a': _jnp.float32, 'lru_w_x': _jnp.float32, 'lru_b_x': _jnp.float32, 'lru_lambda': _jnp.float32, 'conf_conv_w': _jnp.float32, 'conf_conv_b': _jnp.float32, 'conf_ln_g': _jnp.float32, 'conf_ln_b': _jnp.float32, 'w_out': _jnp.float32, 'xa_norm_g': _jnp.float32, 'mem_norm_g': _jnp.float32, 'w_q': _jnp.float32, 'w_kv': _jnp.float32, 'w_o': _jnp.float32, 'ffn_norm_g': _jnp.float32, 'w_up': _jnp.float32, 'ffn_conv_w': _jnp.float32, 'ffn_conv_b': _jnp.float32, 'w_down': _jnp.float32, 'final_norm_g': _jnp.float32}
MOMENT_SCALE = {'mix_norm_g': 1.901577e-01, 'w_in': 1.164546e-01, 'lru_conv_w': 1.319721e-01, 'lru_conv_b': 1.570601e+00, 'lru_w_a': 5.608336e-02, 'lru_b_a': 4.240765e-02, 'lru_w_x': 9.982212e-02, 'lru_b_x': 4.505609e-02, 'lru_lambda': 7.881789e-02, 'conf_conv_w': 1.546625e-01, 'conf_conv_b': 3.333958e-01, 'conf_ln_g': 1.888180e-01, 'conf_ln_b': 1.575796e-01, 'w_out': 1.372230e-01, 'xa_norm_g': 2.930094e-02, 'mem_norm_g': 4.146905e-02, 'w_q': 2.617063e-02, 'w_kv': 2.636946e-02, 'w_o': 2.675510e-02, 'ffn_norm_g': 1.990502e-01, 'w_up': 7.452333e-02, 'ffn_conv_w': 7.595399e-02, 'ffn_conv_b': 7.445393e-02, 'w_down': 1.269002e-01, 'final_norm_g': 6.401318e+01}


def _to_microbatches(a, axis):
    t = _jnp.moveaxis(a, axis, 0)
    t = t.reshape((N_MICROBATCH, t.shape[0] // N_MICROBATCH) + t.shape[1:])
    return _jnp.moveaxis(t, 1, axis + 1)


def setup_inputs(seed: int = 0) -> dict:
    inp = _fwd_setup_inputs(seed)
    key = _jax.random.fold_in(_jax.random.key(seed), 7919)
    shape, _ = _output_shape()
    out = dict(inp)
    out["loss_target"] = _jax.random.normal(_jax.random.fold_in(key, 0), shape, _jnp.float32)
    for i, name in enumerate(TWIN_WEIGHTS):
        w = inp[name].astype(_jnp.float32)
        if MOMENT_SCALE is None:
            s = _jnp.sqrt(_jnp.mean(_jnp.square(w)) + 1e-30)
        else:
            s = MOMENT_SCALE[name]
        km, kv = _jax.random.split(_jax.random.fold_in(key, i + 1))
        out[name] = w
        out["m_" + name] = s * _jax.random.normal(km, w.shape, _jnp.float32)
        out["v_" + name] = (s * s) * _jax.random.uniform(kv, w.shape, _jnp.float32, 0.5, 1.5)
    if N_MICROBATCH > 1:
        for name, axis in PER_EXAMPLE_BATCH_AXIS.items():
            out[name] = _to_microbatches(out[name], axis)
    return {'x': out['x'], 'mem': out['mem'], 'mix_norm_g': out['mix_norm_g'], 'w_in': out['w_in'], 'lru_conv_w': out['lru_conv_w'], 'lru_conv_b': out['lru_conv_b'], 'lru_w_a': out['lru_w_a'], 'lru_b_a': out['lru_b_a'], 'lru_w_x': out['lru_w_x'], 'lru_b_x': out['lru_b_x'], 'lru_lambda': out['lru_lambda'], 'conf_conv_w': out['conf_conv_w'], 'conf_conv_b': out['conf_conv_b'], 'conf_ln_g': out['conf_ln_g'], 'conf_ln_b': out['conf_ln_b'], 'w_out': out['w_out'], 'xa_norm_g': out['xa_norm_g'], 'mem_norm_g': out['mem_norm_g'], 'w_q': out['w_q'], 'w_kv': out['w_kv'], 'w_o': out['w_o'], 'ffn_norm_g': out['ffn_norm_g'], 'w_up': out['w_up'], 'ffn_conv_w': out['ffn_conv_w'], 'ffn_conv_b': out['ffn_conv_b'], 'w_down': out['w_down'], 'final_norm_g': out['final_norm_g'], 'loss_target': out['loss_target'], 'm_mix_norm_g': out['m_mix_norm_g'], 'm_w_in': out['m_w_in'], 'm_lru_conv_w': out['m_lru_conv_w'], 'm_lru_conv_b': out['m_lru_conv_b'], 'm_lru_w_a': out['m_lru_w_a'], 'm_lru_b_a': out['m_lru_b_a'], 'm_lru_w_x': out['m_lru_w_x'], 'm_lru_b_x': out['m_lru_b_x'], 'm_lru_lambda': out['m_lru_lambda'], 'm_conf_conv_w': out['m_conf_conv_w'], 'm_conf_conv_b': out['m_conf_conv_b'], 'm_conf_ln_g': out['m_conf_ln_g'], 'm_conf_ln_b': out['m_conf_ln_b'], 'm_w_out': out['m_w_out'], 'm_xa_norm_g': out['m_xa_norm_g'], 'm_mem_norm_g': out['m_mem_norm_g'], 'm_w_q': out['m_w_q'], 'm_w_kv': out['m_w_kv'], 'm_w_o': out['m_w_o'], 'm_ffn_norm_g': out['m_ffn_norm_g'], 'm_w_up': out['m_w_up'], 'm_ffn_conv_w': out['m_ffn_conv_w'], 'm_ffn_conv_b': out['m_ffn_conv_b'], 'm_w_down': out['m_w_down'], 'm_final_norm_g': out['m_final_norm_g'], 'v_mix_norm_g': out['v_mix_norm_g'], 'v_w_in': out['v_w_in'], 'v_lru_conv_w': out['v_lru_conv_w'], 'v_lru_conv_b': out['v_lru_conv_b'], 'v_lru_w_a': out['v_lru_w_a'], 'v_lru_b_a': out['v_lru_b_a'], 'v_lru_w_x': out['v_lru_w_x'], 'v_lru_b_x': out['v_lru_b_x'], 'v_lru_lambda': out['v_lru_lambda'], 'v_conf_conv_w': out['v_conf_conv_w'], 'v_conf_conv_b': out['v_conf_conv_b'], 'v_conf_ln_g': out['v_conf_ln_g'], 'v_conf_ln_b': out['v_conf_ln_b'], 'v_w_out': out['v_w_out'], 'v_xa_norm_g': out['v_xa_norm_g'], 'v_mem_norm_g': out['v_mem_norm_g'], 'v_w_q': out['v_w_q'], 'v_w_kv': out['v_w_kv'], 'v_w_o': out['v_w_o'], 'v_ffn_norm_g': out['v_ffn_norm_g'], 'v_w_up': out['v_w_up'], 'v_ffn_conv_w': out['v_ffn_conv_w'], 'v_ffn_conv_b': out['v_ffn_conv_b'], 'v_w_down': out['v_w_down'], 'v_final_norm_g': out['v_final_norm_g']}


def _loss(weights, diff, rest, loss_target):
    with _jax.named_scope("forward"):
        args = {**rest, TWIN_DIFF_INPUT: diff, **{k: w.astype(_WEIGHT_DTYPES[k]) for k, w in weights.items()}}
        y = _forward(args)
    with _jax.named_scope("loss_head"):
        err = _jnp.square(y.astype(_jnp.float32) - loss_target)
        return 0.5 * _jnp.sum(_jnp.mean(err, axis=-1)) if err.ndim else 0.5 * err


def _adamw(w, g, m, v):
    m = ADAM_B1 * m + (1.0 - ADAM_B1) * g
    v = ADAM_B2 * v + (1.0 - ADAM_B2) * _jnp.square(g)
    m_hat = m / (1.0 - ADAM_B1 ** ADAM_STEP)
    v_hat = v / (1.0 - ADAM_B2 ** ADAM_STEP)
    delta = -ADAM_LR * (m_hat / (_jnp.sqrt(v_hat) + ADAM_EPS) + ADAM_WD * w)
    return delta, m, v


def reference(x, mem, mix_norm_g, w_in, lru_conv_w, lru_conv_b, lru_w_a, lru_b_a, lru_w_x, lru_b_x, lru_lambda, conf_conv_w, conf_conv_b, conf_ln_g, conf_ln_b, w_out, xa_norm_g, mem_norm_g, w_q, w_kv, w_o, ffn_norm_g, w_up, ffn_conv_w, ffn_conv_b, w_down, final_norm_g, loss_target, m_mix_norm_g, m_w_in, m_lru_conv_w, m_lru_conv_b, m_lru_w_a, m_lru_b_a, m_lru_w_x, m_lru_b_x, m_lru_lambda, m_conf_conv_w, m_conf_conv_b, m_conf_ln_g, m_conf_ln_b, m_w_out, m_xa_norm_g, m_mem_norm_g, m_w_q, m_w_kv, m_w_o, m_ffn_norm_g, m_w_up, m_ffn_conv_w, m_ffn_conv_b, m_w_down, m_final_norm_g, v_mix_norm_g, v_w_in, v_lru_conv_w, v_lru_conv_b, v_lru_w_a, v_lru_b_a, v_lru_w_x, v_lru_b_x, v_lru_lambda, v_conf_conv_w, v_conf_conv_b, v_conf_ln_g, v_conf_ln_b, v_w_out, v_xa_norm_g, v_mem_norm_g, v_w_q, v_w_kv, v_w_o, v_ffn_norm_g, v_w_up, v_ffn_conv_w, v_ffn_conv_b, v_w_down, v_final_norm_g):
    given = dict(x=x, mem=mem, mix_norm_g=mix_norm_g, w_in=w_in, lru_conv_w=lru_conv_w, lru_conv_b=lru_conv_b, lru_w_a=lru_w_a, lru_b_a=lru_b_a, lru_w_x=lru_w_x, lru_b_x=lru_b_x, lru_lambda=lru_lambda, conf_conv_w=conf_conv_w, conf_conv_b=conf_conv_b, conf_ln_g=conf_ln_g, conf_ln_b=conf_ln_b, w_out=w_out, xa_norm_g=xa_norm_g, mem_norm_g=mem_norm_g, w_q=w_q, w_kv=w_kv, w_o=w_o, ffn_norm_g=ffn_norm_g, w_up=w_up, ffn_conv_w=ffn_conv_w, ffn_conv_b=ffn_conv_b, w_down=w_down, final_norm_g=final_norm_g, loss_target=loss_target, m_mix_norm_g=m_mix_norm_g, m_w_in=m_w_in, m_lru_conv_w=m_lru_conv_w, m_lru_conv_b=m_lru_conv_b, m_lru_w_a=m_lru_w_a, m_lru_b_a=m_lru_b_a, m_lru_w_x=m_lru_w_x, m_lru_b_x=m_lru_b_x, m_lru_lambda=m_lru_lambda, m_conf_conv_w=m_conf_conv_w, m_conf_conv_b=m_conf_conv_b, m_conf_ln_g=m_conf_ln_g, m_conf_ln_b=m_conf_ln_b, m_w_out=m_w_out, m_xa_norm_g=m_xa_norm_g, m_mem_norm_g=m_mem_norm_g, m_w_q=m_w_q, m_w_kv=m_w_kv, m_w_o=m_w_o, m_ffn_norm_g=m_ffn_norm_g, m_w_up=m_w_up, m_ffn_conv_w=m_ffn_conv_w, m_ffn_conv_b=m_ffn_conv_b, m_w_down=m_w_down, m_final_norm_g=m_final_norm_g, v_mix_norm_g=v_mix_norm_g, v_w_in=v_w_in, v_lru_conv_w=v_lru_conv_w, v_lru_conv_b=v_lru_conv_b, v_lru_w_a=v_lru_w_a, v_lru_b_a=v_lru_b_a, v_lru_w_x=v_lru_w_x, v_lru_b_x=v_lru_b_x, v_lru_lambda=v_lru_lambda, v_conf_conv_w=v_conf_conv_w, v_conf_conv_b=v_conf_conv_b, v_conf_ln_g=v_conf_ln_g, v_conf_ln_b=v_conf_ln_b, v_w_out=v_w_out, v_xa_norm_g=v_xa_norm_g, v_mem_norm_g=v_mem_norm_g, v_w_q=v_w_q, v_w_kv=v_w_kv, v_w_o=v_w_o, v_ffn_norm_g=v_ffn_norm_g, v_w_up=v_w_up, v_ffn_conv_w=v_ffn_conv_w, v_ffn_conv_b=v_ffn_conv_b, v_w_down=v_w_down, v_final_norm_g=v_final_norm_g)
    weights = {n: given[n] for n in TWIN_WEIGHTS}
    shared = {n: given[n] for n in SHARED_INPUTS}
    per_example = {n: given[n] for n in ['x', 'mem']}
    grad_fn = _jax.value_and_grad(_loss, argnums=(0, 1))

    def one_microbatch(ex, loss_target):
        ex = dict(ex)
        diff = ex.pop(TWIN_DIFF_INPUT)
        return grad_fn(weights, diff, {**shared, **ex}, loss_target)

    if N_MICROBATCH == 1:
        loss, (grad_w, grad_x) = one_microbatch(per_example, given["loss_target"])
    else:
        def body(carry, xs):
            loss_sum, grad_sum = carry
            l_k, (gw_k, gx_k) = one_microbatch(xs[0], xs[1])
            with _jax.named_scope("update"):
                return (loss_sum + l_k, _jax.tree.map(_jnp.add, grad_sum, gw_k)), gx_k

        init = (_jnp.zeros((), _jnp.float32), _jax.tree.map(_jnp.zeros_like, weights))
        (loss, grad_w), grad_x = _jax.lax.scan(body, init, (per_example, given["loss_target"]))
    with _jax.named_scope("update"):
        delta_w, new_m, new_v = {}, {}, {}
        for n in TWIN_WEIGHTS:
            delta_w[n], new_m[n], new_v[n] = _adamw(weights[n], grad_w[n], given["m_" + n], given["v_" + n])
    return (loss, grad_x, *[grad_w[n] for n in TWIN_WEIGHTS], *[delta_w[n] for n in TWIN_WEIGHTS],
            *[new_m[n] for n in TWIN_WEIGHTS], *[new_v[n] for n in TWIN_WEIGHTS])
```

```python
import functools

import jax
import jax.numpy as jnp
from jax import lax
from jax.experimental import pallas as pl
from jax.experimental.pallas import tpu as pltpu

_MXU = jnp.bfloat16
_XFER = jnp.bfloat16
_F32 = jnp.float32
_EPS = 1e-6
_NDEV = 8
_AXES = ("x", "y", "c")
_VMEM_LIMIT = 48 * 1024 * 1024

_D_LRU = 512
_XA_HEADS = 4
_RG_C = 8.0
_ADAM_LR, _ADAM_B1, _ADAM_B2, _ADAM_EPS, _ADAM_WD, _ADAM_STEP = 0.001, 0.9, 0.999, 1e-08, 0.01, 10

_MESH_ID = pl.DeviceIdType.MESH
_ANY = pl.BlockSpec(memory_space=pl.ANY)


def _cparams(*sem):
    return pltpu.CompilerParams(dimension_semantics=tuple(sem), vmem_limit_bytes=_VMEM_LIMIT)


def _sigmoid(v):
    return 1.0 / (1.0 + jnp.exp(-v))


_GELU_C = 0.7978845608028654
_GELU_K = 0.044715


def _gelu(v):
    t = jnp.tanh(_GELU_C * (v + _GELU_K * v * v * v))
    return 0.5 * v * (1.0 + t)


def _gelu_and_grad(v):
    v2 = v * v
    t = jnp.tanh(_GELU_C * (v + _GELU_K * v2 * v))
    g = 0.5 * v * (1.0 + t)
    dg = 0.5 * (1.0 + t) + 0.5 * v * (1.0 - t * t) * (_GELU_C * (1.0 + 3.0 * _GELU_K * v2))
    return g, dg


def _softplus(v):
    e = jnp.exp(-jnp.abs(v))
    log1p = jnp.where(e < 1e-2, e * (1.0 - e * (0.5 - e * (1.0 / 3.0))), jnp.log(1.0 + e))
    return jnp.maximum(v, 0.0) + log1p


def _neg_expm1(v):
    series = -v * (1.0 + v * (0.5 + v * ((1.0 / 6.0) + v * (1.0 / 24.0))))
    return jnp.where(v > -0.0625, series, 1.0 - jnp.exp(v))


def _dot(a, b, dims):
    return lax.dot_general(a.astype(_MXU), b.astype(_MXU), (dims, ((), ())), preferred_element_type=_F32)


_NN = ((1,), (0,))
_NT = ((1,), (1,))
_TN = ((0,), (0,))


def _scan_fwd(a, b, rows):
    n = a.shape[0]
    d = 1
    while d < n:
        keep = rows >= d
        b = jnp.where(keep, b + a * pltpu.roll(b, d, 0), b)
        a = jnp.where(keep, a * pltpu.roll(a, d, 0), a)
        d *= 2
    return a, b


def _scan_rev(a, b, rows):
    n = a.shape[0]
    d = 1
    while d < n:
        keep = rows < n - d
        b = jnp.where(keep, b + a * pltpu.roll(b, n - d, 0), b)
        a = jnp.where(keep, a * pltpu.roll(a, n - d, 0), a)
        d *= 2
    return a, b


def _colsum(v):
    return jnp.sum(v, axis=0, keepdims=True)


def _mm(a, b, *, dims, grid, a_spec, b_spec, out_shape, out_spec, acc_shape, name, add=None, add_spec=None):
    nred = grid[-1]
    red_axis = len(grid) - 1
    has_add = add is not None

    def body(*refs):
        if has_add:
            a_ref, b_ref, add_ref, o_ref, acc_ref = refs
        else:
            a_ref, b_ref, o_ref, acc_ref = refs
        p = _dot(a_ref[...], b_ref[...], dims)

        def finish(total):
            if has_add:
                total = total + add_ref[...]
            o_ref[...] = total.astype(o_ref.dtype)

        if nred == 1:
            finish(p)
        else:
            k = pl.program_id(red_axis)

            @pl.when(k == 0)
            def _():
                acc_ref[...] = p

            @pl.when(k > 0)
            def _():
                acc_ref[...] += p

            @pl.when(k == nred - 1)
            def _():
                finish(acc_ref[...])

    in_specs = [a_spec, b_spec] + ([add_spec] if has_add else [])
    args = (a, b) + ((add,) if has_add else ())
    sem = ("parallel",) * (len(grid) - 1) + ("arbitrary",)
    return pl.pallas_call(
        body, grid=grid, in_specs=in_specs, out_specs=out_spec, out_shape=out_shape,
        scratch_shapes=[pltpu.VMEM(acc_shape if nred > 1 else (8, 128), _F32)],
        compiler_params=_cparams(*sem), name=name,
    )(*args)


def _tile(m, cap):
    t = min(m, cap)
    assert m % t == 0
    return t


def _mm_nn_stacked(a, w, out_dtype, name):
    m, k = a.shape
    j, _, n = w.shape
    tm = _tile(m, 1024)
    return _mm(a, w, dims=_NN, grid=(m // tm, j, 1),
               a_spec=pl.BlockSpec((tm, k), lambda i, jj, r: (i, 0)),
               b_spec=pl.BlockSpec((None, k, n), lambda i, jj, r: (jj, 0, 0)),
               out_shape=jax.ShapeDtypeStruct((m, j * n), out_dtype),
               out_spec=pl.BlockSpec((tm, n), lambda i, jj, r: (i, jj)),
               acc_shape=(tm, n), name=name)


def _mm_nn_nat(a, w, add, out_dtype, name):
    m, kt = a.shape
    _, n = w.shape
    tm = _tile(m, 1024)
    tk = _tile(kt, 1024)
    return _mm(a, w, dims=_NN, grid=(m // tm, kt // tk),
               a_spec=pl.BlockSpec((tm, tk), lambda i, r: (i, r)),
               b_spec=pl.BlockSpec((tk, n), lambda i, r: (r, 0)),
               out_shape=jax.ShapeDtypeStruct((m, n), out_dtype),
               out_spec=pl.BlockSpec((tm, n), lambda i, r: (i, 0)),
               acc_shape=(tm, n), name=name,
               add=add, add_spec=pl.BlockSpec((tm, n), lambda i, r: (i, 0)))


def _mm_nt_stacked(dc, w, out_dtype, name):
    m = dc.shape[0]
    j, k, n = w.shape
    tm = _tile(m, 1024)
    return _mm(dc, w, dims=_NT, grid=(m // tm, j),
               a_spec=pl.BlockSpec((tm, n), lambda i, r: (i, r)),
               b_spec=pl.BlockSpec((None, k, n), lambda i, r: (r, 0, 0)),
               out_shape=jax.ShapeDtypeStruct((m, k), out_dtype),
               out_spec=pl.BlockSpec((tm, k), lambda i, r: (i, 0)),
               acc_shape=(tm, k), name=name)


def _mm_nt_nat(dc, w, out_dtype, name):
    m, n = dc.shape
    kt = w.shape[0]
    tm = _tile(m, 1024)
    tkb = _tile(kt, 1024)
    return _mm(dc, w, dims=_NT, grid=(m // tm, kt // tkb, 1),
               a_spec=pl.BlockSpec((tm, n), lambda i, kb, r: (i, 0)),
               b_spec=pl.BlockSpec((tkb, n), lambda i, kb, r: (kb, 0)),
               out_shape=jax.ShapeDtypeStruct((m, kt), out_dtype),
               out_spec=pl.BlockSpec((tm, tkb), lambda i, kb, r: (i, kb)),
               acc_shape=(tm, tkb), name=name)


def _mm_tn_stacked(a, dc, j, out_dtype, name):
    s, k = a.shape
    n = dc.shape[1] // j
    ts = _tile(s, 1024)
    return _mm(a, dc, dims=_TN, grid=(j, s // ts),
               a_spec=pl.BlockSpec((ts, k), lambda jj, r: (r, 0)),
               b_spec=pl.BlockSpec((ts, n), lambda jj, r: (r, jj)),
               out_shape=jax.ShapeDtypeStruct((j, k, n), out_dtype),
               out_spec=pl.BlockSpec((None, k, n), lambda jj, r: (jj, 0, 0)),
               acc_shape=(k, n), name=name)


def _mm_tn_nat(a, dc, out_dtype, name):
    s, kt = a.shape
    n = dc.shape[1]
    ts = _tile(s, 1024)
    tkb = _tile(kt, 512)
    return _mm(a, dc, dims=_TN, grid=(kt // tkb, s // ts),
               a_spec=pl.BlockSpec((ts, tkb), lambda kb, r: (r, kb)),
               b_spec=pl.BlockSpec((ts, n), lambda kb, r: (r, 0)),
               out_shape=jax.ShapeDtypeStruct((kt, n), out_dtype),
               out_spec=pl.BlockSpec((tkb, n), lambda kb, r: (kb, 0)),
               acc_shape=(tkb, n), name=name)


def _rms_fwd(x, g, name):
    s, d = x.shape
    t = _tile(s, 256)

    def body(x_ref, g_ref, h_ref):
        xv = x_ref[...]
        r = lax.rsqrt(jnp.mean(xv * xv, axis=-1, keepdims=True) + _EPS)
        h_ref[...] = (xv * r * g_ref[...]).astype(h_ref.dtype)

    return pl.pallas_call(
        body, grid=(s // t,),
        in_specs=[pl.BlockSpec((t, d), lambda i: (i, 0)), pl.BlockSpec((1, d), lambda i: (0, 0))],
        out_specs=pl.BlockSpec((t, d), lambda i: (i, 0)),
        out_shape=jax.ShapeDtypeStruct((s, d), _MXU),
        compiler_params=_cparams("parallel"), name=name,
    )(x, g)


def _rms_bwd(dh, x, g, dres, name):
    s, d = x.shape
    t = _tile(s, 256)
    has_res = dres is not None

    def body(*refs):
        if has_res:
            dh_ref, x_ref, g_ref, dres_ref, dx_ref, dxb_ref, dg_ref = refs
        else:
            dh_ref, x_ref, g_ref, dx_ref, dxb_ref, dg_ref = refs
        xv = x_ref[...]
        dhv = dh_ref[...]
        r = lax.rsqrt(jnp.mean(xv * xv, axis=-1, keepdims=True) + _EPS)
        xhat = xv * r
        dxh = dhv * g_ref[...]
        dx = r * (dxh - xhat * jnp.mean(dxh * xhat, axis=-1, keepdims=True))
        if has_res:
            dx = dx + dres_ref[...]
        dx_ref[...] = dx
        dxb_ref[...] = dx.astype(dxb_ref.dtype)

        @pl.when(pl.program_id(0) == 0)
        def _():
            dg_ref[...] = jnp.zeros_like(dg_ref)

        dg_ref[...] += _colsum(dhv * xhat)

    row = pl.BlockSpec((t, d), lambda i: (i, 0))
    vec = pl.BlockSpec((1, d), lambda i: (0, 0))
    in_specs = [row, row, vec] + ([row] if has_res else [])
    args = (dh, x, g) + ((dres,) if has_res else ())
    return pl.pallas_call(
        body, grid=(s // t,), in_specs=in_specs, out_specs=[row, row, vec],
        out_shape=[jax.ShapeDtypeStruct((s, d), _F32), jax.ShapeDtypeStruct((s, d), _MXU),
                   jax.ShapeDtypeStruct((1, d), _F32)],
        compiler_params=_cparams("arbitrary"), name=name,
    )(*args)


def _final_loss(x3, target, g, name):
    s, d = x3.shape
    t = _tile(s, 256)

    def body(x_ref, t_ref, g_ref, dx_ref, dxb_ref, l_ref, dg_ref):
        xv = x_ref[...]
        gv = g_ref[...]
        r = lax.rsqrt(jnp.mean(xv * xv, axis=-1, keepdims=True) + _EPS)
        xhat = xv * r
        err = xhat * gv - t_ref[...]
        dy = err * (1.0 / d)
        dxh = dy * gv
        dx = r * (dxh - xhat * jnp.mean(dxh * xhat, axis=-1, keepdims=True))
        dx_ref[...] = dx
        dxb_ref[...] = dx.astype(dxb_ref.dtype)

        @pl.when(pl.program_id(0) == 0)
        def _():
            l_ref[...] = jnp.zeros_like(l_ref)
            dg_ref[...] = jnp.zeros_like(dg_ref)

        l_ref[...] += _colsum(err * err)
        dg_ref[...] += _colsum(dy * xhat)

    row = pl.BlockSpec((t, d), lambda i: (i, 0))
    vec = pl.BlockSpec((1, d), lambda i: (0, 0))
    return pl.pallas_call(
        body, grid=(s // t,), in_specs=[row, row, vec], out_specs=[row, row, vec, vec],
        out_shape=[jax.ShapeDtypeStruct((s, d), _F32), jax.ShapeDtypeStruct((s, d), _MXU),
                   jax.ShapeDtypeStruct((1, d), _F32), jax.ShapeDtypeStruct((1, d), _F32)],
        compiler_params=_cparams("arbitrary"), name=name,
    )(x3, target, g)


_LRU_K = 4
_CONF_K = 31
_LRU_HALO = 8
_CONF_HALO = 32
_MIX_T = 256


def _lru_gates(lx, wab_ref, ba_ref, bx_ref, lam_ref):
    c = _D_LRU
    pre = _dot(lx, wab_ref[...], _NN)
    r = _sigmoid(pre[:, :c] + ba_ref[...])
    ig = _sigmoid(pre[:, c:] + bx_ref[...])
    sp = _softplus(-lam_ref[...])
    log_a = (-_RG_C) * r * sp
    a = jnp.exp(log_a)
    mult = jnp.sqrt(_neg_expm1(2.0 * log_a))
    return r, ig, sp, a, mult


def _causal_conv(ext_ref, halo, w_ref, b_ref, taps, t):
    acc = b_ref[...] + w_ref[0:1, :] * ext_ref[pl.ds(halo - (taps - 1), t), :]
    for k in range(1, taps):
        acc = acc + w_ref[k:k + 1, :] * ext_ref[pl.ds(halo - (taps - 1) + k, t), :]
    return acc


def _mixer_fwd(z, lcw, lcb, wab, ba, bx, lam, ccw, ccb, lng, lnb, name):
    s = z.shape[0]
    c = _D_LRU
    t = _tile(s, _MIX_T)
    nt = s // t

    def body(lx0_ref, lx0h_ref, gate_ref, ca_ref, cah_ref, cb_ref, cbh_ref,
             lcw_ref, lcb_ref, wab_ref, ba_ref, bx_ref, lam_ref, ccw_ref, ccb_ref, lng_ref, lnb_ref,
             ycat_ref, hs_ref, cc_ref, ext_ref, cge_ref, hc_ref):
        i = pl.program_id(0)
        first = i == 0
        rows = lax.broadcasted_iota(jnp.int32, (t, c), 0)

        ext_ref[0:_LRU_HALO, :] = jnp.where(first, 0.0, lx0h_ref[...])
        ext_ref[_LRU_HALO:_LRU_HALO + t, :] = lx0_ref[...]
        lx = _causal_conv(ext_ref, _LRU_HALO, lcw_ref, lcb_ref, _LRU_K, t)
        r, ig, sp, a, mult = _lru_gates(lx, wab_ref, ba_ref, bx_ref, lam_ref)
        u = mult * (ig * lx)
        a_cum, h_loc = _scan_fwd(a, u, rows)

        @pl.when(first)
        def _():
            hc_ref[...] = jnp.zeros_like(hc_ref)

        h = h_loc + a_cum * hc_ref[7:8, :]
        hs_ref[...] = h
        hc_ref[...] = hs_ref[pl.ds(t - 8, 8), :]
        ycat_ref[:, 0:c] = (h * _gelu(gate_ref[...])).astype(ycat_ref.dtype)

        cge_ref[0:_CONF_HALO, :] = jnp.where(first, 0.0, cah_ref[...] * _sigmoid(cbh_ref[...]))
        cge_ref[_CONF_HALO:_CONF_HALO + t, :] = ca_ref[...] * _sigmoid(cb_ref[...])
        cc = _causal_conv(cge_ref, _CONF_HALO, ccw_ref, ccb_ref, _CONF_K, t)
        cc_ref[...] = cc
        xc = cc - jnp.mean(cc, axis=-1, keepdims=True)
        rstd = lax.rsqrt(jnp.mean(xc * xc, axis=-1, keepdims=True) + _EPS)
        ln = xc * rstd * lng_ref[...] + lnb_ref[...]
        ycat_ref[:, c:2 * c] = (ln * _sigmoid(ln)).astype(ycat_ref.dtype)

    def col(j):
        return pl.BlockSpec((t, c), lambda i: (i, j))

    def halo(j, rows_):
        per = t // rows_
        return pl.BlockSpec((rows_, c), lambda i: (jnp.maximum(i * per - 1, 0), j))

    def full(arr):
        return pl.BlockSpec(arr.shape, lambda i: (0,) * arr.ndim)

    params = (lcw, lcb, wab, ba, bx, lam, ccw, ccb, lng, lnb)
    return pl.pallas_call(
        body, grid=(nt,),
        in_specs=[col(0), halo(0, _LRU_HALO), col(1), col(2), halo(2, _CONF_HALO), col(3), halo(3, _CONF_HALO)]
        + [full(p) for p in params],
        out_specs=[pl.BlockSpec((t, 2 * c), lambda i: (i, 0)), pl.BlockSpec((t, c), lambda i: (i, 0)),
                   pl.BlockSpec((t, c), lambda i: (i, 0))],
        out_shape=[jax.ShapeDtypeStruct((s, 2 * c), _MXU), jax.ShapeDtypeStruct((s, c), _F32),
                   jax.ShapeDtypeStruct((s, c), _F32)],
        scratch_shapes=[pltpu.VMEM((t + _LRU_HALO, c), _F32), pltpu.VMEM((t + _CONF_HALO, c), _F32),
                        pltpu.VMEM((8, c), _F32)],
        compiler_params=_cparams("arbitrary"), name=name,
    )(z, z, z, z, z, z, z, *params)


def _mixer_bwd(dycat, z, hs, cc, lcw, lcb, wab, ba, bx, lam, ccw, ccb, lng, lnb, name):
    s = z.shape[0]
    c = _D_LRU
    t = _tile(s, _MIX_T)
    nt = s // t

    def body(dyl_ref, dc_ref, lx0_ref, lx0h_ref, gate_ref, ca_ref, cah_ref, cb_ref, cbh_ref,
             hs_ref, hsh_ref, cc_ref,
             lcw_ref, lcb_ref, wab_ref, ba_ref, bx_ref, lam_ref, ccw_ref, ccb_ref, lng_ref, lnb_ref,
             dz_ref, dlcw_ref, dlcb_ref, dwab_ref, dba_ref, dbx_ref, dlam_ref, dccw_ref, dccb_ref, dlng_ref,
             dlnb_ref,
             ext_ref, up_ref, cge_ref, dce_ref, xc_ref, dlxc_ref, dccc_ref):
        i = pl.program_id(0)
        first_tile = i == nt - 1
        last_tile = i == 0
        rows = lax.broadcasted_iota(jnp.int32, (t, c), 0)

        @pl.when(last_tile)
        def _():
            for ref in (dlcw_ref, dlcb_ref, dwab_ref, dba_ref, dbx_ref, dlam_ref, dccw_ref, dccb_ref, dlng_ref,
                        dlnb_ref, xc_ref, dlxc_ref, dccc_ref):
                ref[...] = jnp.zeros_like(ref)

        ext_ref[0:_LRU_HALO, :] = jnp.where(first_tile, 0.0, lx0h_ref[...])
        ext_ref[_LRU_HALO:_LRU_HALO + t, :] = lx0_ref[...]
        lx = _causal_conv(ext_ref, _LRU_HALO, lcw_ref, lcb_ref, _LRU_K, t)
        r, ig, sp, a, mult = _lru_gates(lx, wab_ref, ba_ref, bx_ref, lam_ref)
        h = hs_ref[...]
        gl, dgl = _gelu_and_grad(gate_ref[...])
        dyl = dyl_ref[...]
        dz_ref[:, c:2 * c] = (dyl * h * dgl).astype(dz_ref.dtype)
        dh = dyl * gl

        up_ref[0:t, :] = a
        up_ref[t:t + 8, :] = jnp.ones((8, c), _F32)
        a_up = up_ref[pl.ds(1, t), :]
        a_cum, g_loc = _scan_rev(a_up, dh, rows)
        gt = g_loc + a_cum * xc_ref[0:1, :]
        xc_ref[...] = (a * gt)[0:8, :]

        up_ref[0:8, :] = jnp.where(first_tile, 0.0, hsh_ref[...])
        up_ref[8:8 + t, :] = h
        hprev = up_ref[pl.ds(7, t), :]

        da = gt * hprev
        dmult = gt * ig * lx
        dig = gt * mult * lx
        dlx = gt * mult * ig
        dlog_a = da * a - dmult * a * a / mult
        dpre_r = dlog_a * (-_RG_C) * sp * r * (1.0 - r)
        dpre_i = dig * ig * (1.0 - ig)
        dlam_ref[...] += _colsum(dlog_a * r) * (_RG_C * _sigmoid(-lam_ref[...]))
        dba_ref[...] += _colsum(dpre_r)
        dbx_ref[...] += _colsum(dpre_i)
        dpre = jnp.concatenate([dpre_r, dpre_i], axis=1).astype(_MXU)
        dlx = dlx + _dot(dpre, wab_ref[...], _NT)
        dwab_ref[...] += _dot(lx, dpre, _TN)

        dlcb_ref[...] += _colsum(dlx)
        up_ref[0:t, :] = dlx
        up_ref[t:t + 8, :] = dlxc_ref[...]
        dlxc_ref[...] = dlx[0:8, :]
        acc = lcw_ref[0:1, :] * up_ref[pl.ds(_LRU_K - 1, t), :]
        for k in range(1, _LRU_K):
            acc = acc + lcw_ref[k:k + 1, :] * up_ref[pl.ds(_LRU_K - 1 - k, t), :]
        dz_ref[:, 0:c] = acc.astype(dz_ref.dtype)
        for k in range(_LRU_K):
            dlcw_ref[k:k + 1, :] += _colsum(dlx * ext_ref[pl.ds(_LRU_HALO - (_LRU_K - 1) + k, t), :])

        sig_b = _sigmoid(cb_ref[...])
        ca = ca_ref[...]
        cge_ref[0:_CONF_HALO, :] = jnp.where(first_tile, 0.0, cah_ref[...] * _sigmoid(cbh_ref[...]))
        cge_ref[_CONF_HALO:_CONF_HALO + t, :] = ca * sig_b
        ccv = cc_ref[...]
        xcen = ccv - jnp.mean(ccv, axis=-1, keepdims=True)
        rstd = lax.rsqrt(jnp.mean(xcen * xcen, axis=-1, keepdims=True) + _EPS)
        xn = xcen * rstd
        ln = xn * lng_ref[...] + lnb_ref[...]
        sg = _sigmoid(ln)
        dln = dc_ref[...] * (sg * (1.0 + ln * (1.0 - sg)))
        dlng_ref[...] += _colsum(dln * xn)
        dlnb_ref[...] += _colsum(dln)
        dxn = dln * lng_ref[...]
        dcc = rstd * (dxn - jnp.mean(dxn, axis=-1, keepdims=True)
                      - xn * jnp.mean(dxn * xn, axis=-1, keepdims=True))
        dccb_ref[...] += _colsum(dcc)
        for k in range(_CONF_K):
            dccw_ref[k:k + 1, :] += _colsum(dcc * cge_ref[pl.ds(_CONF_HALO - (_CONF_K - 1) + k, t), :])
        dce_ref[0:t, :] = dcc
        dce_ref[t:t + _CONF_HALO, :] = dccc_ref[...]
        dccc_ref[...] = dcc[0:_CONF_HALO, :]
        dcg = ccw_ref[0:1, :] * dce_ref[pl.ds(_CONF_K - 1, t), :]
        for k in range(1, _CONF_K):
            dcg = dcg + ccw_ref[k:k + 1, :] * dce_ref[pl.ds(_CONF_K - 1 - k, t), :]
        dz_ref[:, 2 * c:3 * c] = (dcg * sig_b).astype(dz_ref.dtype)
        dz_ref[:, 3 * c:4 * c] = (dcg * ca * sig_b * (1.0 - sig_b)).astype(dz_ref.dtype)

    def col(j):
        return pl.BlockSpec((t, c), lambda i: (nt - 1 - i, j))

    def halo(j, rows_):
        per = t // rows_
        return pl.BlockSpec((rows_, c), lambda i: (jnp.maximum((nt - 1 - i) * per - 1, 0), j))

    def full(shape):
        return pl.BlockSpec(shape, lambda i: (0,) * len(shape))

    params = (lcw, lcb, wab, ba, bx, lam, ccw, ccb, lng, lnb)
    small = [(_LRU_K, c), (1, c), (c, 2 * c), (1, c), (1, c), (1, c), (_CONF_K, c), (1, c), (1, c), (1, c)]
    return pl.pallas_call(
        body, grid=(nt,),
        in_specs=[col(0), col(1),
                  col(0), halo(0, _LRU_HALO), col(1), col(2), halo(2, _CONF_HALO), col(3), halo(3, _CONF_HALO),
                  col(0), halo(0, 8), col(0)]
        + [full(p.shape) for p in params],
        out_specs=[pl.BlockSpec((t, 4 * c), lambda i: (nt - 1 - i, 0))] + [full(sh) for sh in small],
        out_shape=[jax.ShapeDtypeStruct((s, 4 * c), _MXU)] + [jax.ShapeDtypeStruct(sh, _F32) for sh in small],
        scratch_shapes=[pltpu.VMEM((t + _LRU_HALO, c), _F32), pltpu.VMEM((t + 8, c), _F32),
                        pltpu.VMEM((t + _CONF_HALO, c), _F32), pltpu.VMEM((t + _CONF_HALO, c), _F32),
                        pltpu.VMEM((8, c), _F32), pltpu.VMEM((8, c), _F32), pltpu.VMEM((_CONF_HALO, c), _F32)],
        compiler_params=_cparams("arbitrary"), name=name,
    )(dycat, dycat, z, z, z, z, z, z, z, hs, hs, cc, *params)


_ATT_T = 512


def _attn_probs(qh, kh, scale):
    sc = _dot(qh, kh, _NT) * scale
    e = jnp.exp(sc - jnp.max(sc, axis=-1, keepdims=True))
    return e / jnp.sum(e, axis=-1, keepdims=True)


def _attn_fwd(q, kv, name):
    s, d = q.shape
    nm = kv.shape[0]
    hd = d // _XA_HEADS
    t = _tile(s, _ATT_T)
    scale = hd ** -0.5

    def body(q_ref, k_ref, v_ref, o_ref):
        for hh in range(_XA_HEADS):
            sl = slice(hh * hd, (hh + 1) * hd)
            p = _attn_probs(q_ref[:, sl], k_ref[:, sl], scale)
            o_ref[:, sl] = _dot(p, v_ref[:, sl], _NN).astype(o_ref.dtype)

    return pl.pallas_call(
        body, grid=(s // t,),
        in_specs=[pl.BlockSpec((t, d), lambda i: (i, 0)), pl.BlockSpec((nm, d), lambda i: (0, 0)),
                  pl.BlockSpec((nm, d), lambda i: (0, 1))],
        out_specs=pl.BlockSpec((t, d), lambda i: (i, 0)),
        out_shape=jax.ShapeDtypeStruct((s, d), _MXU),
        compiler_params=_cparams("parallel"), name=name,
    )(q, kv, kv)


def _attn_bwd(q, kv, do, name):
    s, d = q.shape
    nm = kv.shape[0]
    hd = d // _XA_HEADS
    t = _tile(s, _ATT_T)
    scale = hd ** -0.5

    def body(q_ref, k_ref, v_ref, do_ref, dq_ref, dk_ref, dv_ref):
        @pl.when(pl.program_id(0) == 0)
        def _():
            dk_ref[...] = jnp.zeros_like(dk_ref)
            dv_ref[...] = jnp.zeros_like(dv_ref)

        for hh in range(_XA_HEADS):
            sl = slice(hh * hd, (hh + 1) * hd)
            qh = q_ref[:, sl]
            kh = k_ref[:, sl]
            doh = do_ref[:, sl]
            p = _attn_probs(qh, kh, scale)
            dp = _dot(doh, v_ref[:, sl], _NT)
            dv_ref[:, sl] += _dot(p, doh, _TN)
            ds = (p * (dp - jnp.sum(dp * p, axis=-1, keepdims=True)) * scale).astype(_MXU)
            dq_ref[:, sl] = _dot(ds, kh, _NN).astype(dq_ref.dtype)
            dk_ref[:, sl] += _dot(ds, qh, _TN)

    row = pl.BlockSpec((t, d), lambda i: (i, 0))
    return pl.pallas_call(
        body, grid=(s // t,),
        in_specs=[row, pl.BlockSpec((nm, d), lambda i: (0, 0)), pl.BlockSpec((nm, d), lambda i: (0, 1)), row],
        out_specs=[row, pl.BlockSpec((nm, d), lambda i: (0, 0)), pl.BlockSpec((nm, d), lambda i: (0, 0))],
        out_shape=[jax.ShapeDtypeStruct((s, d), _MXU), jax.ShapeDtypeStruct((nm, d), _F32),
                   jax.ShapeDtypeStruct((nm, d), _F32)],
        compiler_params=_cparams("arbitrary"), name=name,
    )(q, kv, kv, do)


_FFN_K = 3
_FFN_T = 128
_FFN_CB = 512


def _ffn_fwd(gu, fcw, fcb, name):
    s = gu.shape[0]
    f = gu.shape[1] // 2
    t = _tile(s, 256)
    cb = _tile(f, 768)
    ncb = f // cb

    def body(g0_ref, g0h_ref, u_ref, w_ref, b_ref, act_ref, ext_ref):
        first = pl.program_id(0) == 0
        ext_ref[0:8, :] = jnp.where(first, 0.0, g0h_ref[...])
        ext_ref[8:8 + t, :] = g0_ref[...]
        g = _causal_conv(ext_ref, 8, w_ref, b_ref, _FFN_K, t)
        act_ref[...] = (_gelu(g) * u_ref[...]).astype(act_ref.dtype)

    per = t // 8
    return pl.pallas_call(
        body, grid=(s // t, ncb),
        in_specs=[pl.BlockSpec((t, cb), lambda i, j: (i, j)),
                  pl.BlockSpec((8, cb), lambda i, j: (jnp.maximum(i * per - 1, 0), j)),
                  pl.BlockSpec((t, cb), lambda i, j: (i, j + ncb)),
                  pl.BlockSpec((_FFN_K, cb), lambda i, j: (0, j)),
                  pl.BlockSpec((1, cb), lambda i, j: (0, j))],
        out_specs=pl.BlockSpec((t, cb), lambda i, j: (i, j)),
        out_shape=jax.ShapeDtypeStruct((s, f), _MXU),
        scratch_shapes=[pltpu.VMEM((t + 8, cb), _F32)],
        compiler_params=_cparams("parallel", "parallel"), name=name,
    )(gu, gu, gu, fcw, fcb)


def _ffn_bwd(dact, gu, fcw, fcb, name):
    s = gu.shape[0]
    f = gu.shape[1] // 2
    t = _tile(s, _FFN_T)
    nt = s // t
    cb = _tile(f, _FFN_CB)

    def body(dact_ref, g0_ref, g0h_ref, u_ref, w_ref, b_ref, dgu_ref, dw_ref, db_ref, ext_ref, up_ref, car_ref):
        i = pl.program_id(0)
        first_tile = i == nt - 1

        @pl.when(i == 0)
        def _():
            dw_ref[...] = jnp.zeros_like(dw_ref)
            db_ref[...] = jnp.zeros_like(db_ref)
            car_ref[...] = jnp.zeros_like(car_ref)

        for j in range(f // cb):
            cs = slice(j * cb, (j + 1) * cb)
            ext_ref[0:8, :] = jnp.where(first_tile, 0.0, g0h_ref[:, cs])
            ext_ref[8:8 + t, :] = g0_ref[:, cs]
            g = _causal_conv(ext_ref, 8, w_ref.at[:, cs], b_ref.at[:, cs], _FFN_K, t)
            gl, dgl = _gelu_and_grad(g)
            da = dact_ref[:, cs]
            dgu_ref[:, f + j * cb:f + (j + 1) * cb] = (da * gl).astype(dgu_ref.dtype)
            dg = da * u_ref[:, cs] * dgl
            db_ref[:, cs] += _colsum(dg)
            for k in range(_FFN_K):
                dw_ref[k:k + 1, cs] += _colsum(dg * ext_ref[pl.ds(8 - (_FFN_K - 1) + k, t), :])
            up_ref[0:t, :] = dg
            up_ref[t:t + 8, :] = car_ref[:, cs]
            car_ref[:, cs] = dg[0:8, :]
            acc = w_ref[0:1, cs] * up_ref[pl.ds(_FFN_K - 1, t), :]
            for k in range(1, _FFN_K):
                acc = acc + w_ref[k:k + 1, cs] * up_ref[pl.ds(_FFN_K - 1 - k, t), :]
            dgu_ref[:, cs] = acc.astype(dgu_ref.dtype)

    per = t // 8
    return pl.pallas_call(
        body, grid=(nt,),
        in_specs=[pl.BlockSpec((t, f), lambda i: (nt - 1 - i, 0)),
                  pl.BlockSpec((t, f), lambda i: (nt - 1 - i, 0)),
                  pl.BlockSpec((8, f), lambda i: (jnp.maximum((nt - 1 - i) * per - 1, 0), 0)),
                  pl.BlockSpec((t, f), lambda i: (nt - 1 - i, 1)),
                  pl.BlockSpec((_FFN_K, f), lambda i: (0, 0)),
                  pl.BlockSpec((1, f), lambda i: (0, 0))],
        out_specs=[pl.BlockSpec((t, 2 * f), lambda i: (nt - 1 - i, 0)),
                   pl.BlockSpec((_FFN_K, f), lambda i: (0, 0)), pl.BlockSpec((1, f), lambda i: (0, 0))],
        out_shape=[jax.ShapeDtypeStruct((s, 2 * f), _MXU), jax.ShapeDtypeStruct((_FFN_K, f), _F32),
                   jax.ShapeDtypeStruct((1, f), _F32)],
        scratch_shapes=[pltpu.VMEM((t + 8, cb), _F32), pltpu.VMEM((t + 8, cb), _F32), pltpu.VMEM((8, f), _F32)],
        compiler_params=_cparams("arbitrary"), name=name,
    )(dact, gu, gu, gu, fcw, fcb)


def _mesh_pos():
    return lax.axis_index("x"), lax.axis_index("y"), lax.axis_index("c")


def _flip(v, bit):
    return 1 - v if bit else v


def _all_gather_many(xs, name):
    n = len(xs)

    def body(*refs):
        x_refs, out_refs = refs[:n], refs[n:2 * n]
        send_sems, recv_sems, local_sems = refs[2 * n:]
        x, y, c = _mesh_pos()
        me, sibling = (x, y, c), (x, y, 1 - c)
        chips = [(1 - x, y), (x, 1 - y), (1 - x, 1 - y)]

        def copy(a, k, block, to, src=None):
            slot = out_refs[a].at[4 * block[0] + 2 * block[1] + block[2]]
            return pltpu.make_async_remote_copy(
                src_ref=slot if src is None else src, dst_ref=slot,
                send_sem=send_sems.at[a * 7 + k], recv_sem=recv_sems.at[a * 7 + k],
                device_id=to, device_id_type=_MESH_ID)

        started = []
        mine = []
        for a in range(n):
            own = pltpu.make_async_copy(x_refs[a], out_refs[a].at[4 * x + 2 * y + c], local_sems.at[a])
            own.start()
            mine.append(own)
            first = [copy(a, 0, me, sibling, src=x_refs[a])]
            first += [copy(a, 1 + j, me, (*chip, c), src=x_refs[a]) for j, chip in enumerate(chips)]
            for cp in first:
                cp.start()
            started += first
        for a in range(n):
            for j, chip in enumerate(chips):
                copy(a, 1 + j, (*chip, c), me).wait_recv()
                fwd = copy(a, 4 + j, (*chip, c), sibling)
                fwd.start()
                started.append(fwd)
        for a in range(n):
            copy(a, 0, sibling, me).wait_recv()
            for j, chip in enumerate(chips):
                copy(a, 4 + j, (*chip, 1 - c), me).wait_recv()
        for cp in started:
            cp.wait_send()
        for own in mine:
            own.wait()

    return pl.pallas_call(
        body,
        out_shape=[jax.ShapeDtypeStruct((_NDEV,) + v.shape, v.dtype) for v in xs],
        in_specs=[_ANY] * n, out_specs=[_ANY] * n,
        scratch_shapes=[pltpu.SemaphoreType.DMA((7 * n,)), pltpu.SemaphoreType.DMA((7 * n,)),
                        pltpu.SemaphoreType.DMA((n,))],
        name=name,
    )(*xs)


def _all_to_all_many(gs, name):
    n = len(gs)

    def body(*refs):
        g_refs, r_refs = refs[:n], refs[n:2 * n]
        send_sems, recv_sems, local_sems = refs[2 * n:]
        x, y, c = _mesh_pos()
        me_idx = 4 * x + 2 * y + c

        def copy(a, k):
            peer = (_flip(x, k & 4), _flip(y, k & 2), _flip(c, k & 1))
            peer_idx = 4 * peer[0] + 2 * peer[1] + peer[2]
            return pltpu.make_async_remote_copy(
                src_ref=g_refs[a].at[peer_idx], dst_ref=r_refs[a].at[me_idx],
                send_sem=send_sems.at[a * 7 + k - 1], recv_sem=recv_sems.at[a * 7 + k - 1],
                device_id=peer, device_id_type=_MESH_ID)

        copies = [copy(a, k) for a in range(n) for k in range(1, _NDEV)]
        mine = [pltpu.make_async_copy(g_refs[a].at[me_idx], r_refs[a].at[me_idx], local_sems.at[a])
                for a in range(n)]
        for cp in copies + mine:
            cp.start()
        for cp in copies:
            cp.wait_recv()
        for cp in copies:
            cp.wait_send()
        for cp in mine:
            cp.wait()

    return pl.pallas_call(
        body,
        out_shape=[jax.ShapeDtypeStruct(v.shape, v.dtype) for v in gs],
        in_specs=[_ANY] * n, out_specs=[_ANY] * n,
        scratch_shapes=[pltpu.SemaphoreType.DMA((7 * n,)), pltpu.SemaphoreType.DMA((7 * n,)),
                        pltpu.SemaphoreType.DMA((n,))],
        name=name,
    )(*gs)


def _adamw_math(w, g, m, v):
    m = _ADAM_B1 * m + (1.0 - _ADAM_B1) * g
    v = _ADAM_B2 * v + (1.0 - _ADAM_B2) * (g * g)
    m_hat = m / (1.0 - _ADAM_B1 ** _ADAM_STEP)
    v_hat = v / (1.0 - _ADAM_B2 ** _ADAM_STEP)
    delta = -_ADAM_LR * (m_hat / (jnp.sqrt(v_hat) + _ADAM_EPS) + _ADAM_WD * w)
    return delta, m, v


def _sum_adamw(parts, w, m, v, name):
    r, c = w.shape
    tr = _tile(r, 128)

    def body(p_ref, w_ref, m_ref, v_ref, g_ref, d_ref, nm_ref, nv_ref):
        g = p_ref[0].astype(_F32)
        for j in range(1, _NDEV):
            g = g + p_ref[j].astype(_F32)
        delta, nm, nv = _adamw_math(w_ref[...], g, m_ref[...], v_ref[...])
        g_ref[...] = g
        d_ref[...] = delta
        nm_ref[...] = nm
        nv_ref[...] = nv

    blk = pl.BlockSpec((tr, c), lambda i: (i, 0))
    return pl.pallas_call(
        body, grid=(r // tr,),
        in_specs=[pl.BlockSpec((_NDEV, tr, c), lambda i: (0, i, 0)), blk, blk, blk],
        out_specs=[blk] * 4, out_shape=[jax.ShapeDtypeStruct((r, c), _F32)] * 4,
        compiler_params=_cparams("parallel"), name=name,
    )(parts, w, m, v)


def _sum8(parts, name):
    _, r, c = parts.shape

    def body(p_ref, o_ref):
        g = p_ref[0]
        for j in range(1, _NDEV):
            g = g + p_ref[j]
        o_ref[...] = g

    return pl.pallas_call(
        body, grid=(1,), in_specs=[pl.BlockSpec((_NDEV, r, c), lambda i: (0, 0, 0))],
        out_specs=pl.BlockSpec((r, c), lambda i: (0, 0)), out_shape=jax.ShapeDtypeStruct((r, c), _F32),
        compiler_params=_cparams("arbitrary"), name=name,
    )(parts)


def _adamw_flat(g, w, m, v, name):
    r, c = w.shape

    def body(g_ref, w_ref, m_ref, v_ref, d_ref, nm_ref, nv_ref):
        delta, nm, nv = _adamw_math(w_ref[...], g_ref[...], m_ref[...], v_ref[...])
        d_ref[...] = delta
        nm_ref[...] = nm
        nv_ref[...] = nv

    blk = pl.BlockSpec((r, c), lambda i: (0, 0))
    return pl.pallas_call(
        body, grid=(1,), in_specs=[blk] * 4, out_specs=[blk] * 3,
        out_shape=[jax.ShapeDtypeStruct((r, c), _F32)] * 3,
        compiler_params=_cparams("arbitrary"), name=name,
    )(g, w, m, v)


def _pack(arrs):
    flat = jnp.concatenate([a.reshape(-1).astype(_F32) for a in arrs])
    pad = (-flat.shape[0]) % 1024
    return jnp.pad(flat, (0, pad)).reshape(-1, 128)


def _unpack(flat2d, shapes):
    flat = flat2d.reshape(-1)
    out, off = [], 0
    for sh in shapes:
        size = 1
        for dim in sh:
            size *= dim
        out.append(flat[off:off + size].reshape(sh))
        off += size
    return out


def _block_diag(w):
    h, hd, _ = w.shape
    eye = jnp.eye(h, dtype=w.dtype)
    return (eye[:, None, :, None] * w[:, :, None, :]).reshape(h * hd, h * hd)


def _diag_blocks(full, h):
    hd = full.shape[0] // h
    return jnp.stack([full[i * hd:(i + 1) * hd, i * hd:(i + 1) * hd] for i in range(h)])


def _ungather_cols(g):
    j, k, n = g.shape
    return g.transpose(1, 0, 2).reshape(k, j * n)


def kernel(x, mem, mix_norm_g, w_in, lru_conv_w, lru_conv_b, lru_w_a, lru_b_a, lru_w_x, lru_b_x, lru_lambda, conf_conv_w, conf_conv_b, conf_ln_g, conf_ln_b, w_out, xa_norm_g, mem_norm_g, w_q, w_kv, w_o, ffn_norm_g, w_up, ffn_conv_w, ffn_conv_b, w_down, final_norm_g, loss_target, m_mix_norm_g, m_w_in, m_lru_conv_w, m_lru_conv_b, m_lru_w_a, m_lru_b_a, m_lru_w_x, m_lru_b_x, m_lru_lambda, m_conf_conv_w, m_conf_conv_b, m_conf_ln_g, m_conf_ln_b, m_w_out, m_xa_norm_g, m_mem_norm_g, m_w_q, m_w_kv, m_w_o, m_ffn_norm_g, m_w_up, m_ffn_conv_w, m_ffn_conv_b, m_w_down, m_final_norm_g, v_mix_norm_g, v_w_in, v_lru_conv_w, v_lru_conv_b, v_lru_w_a, v_lru_b_a, v_lru_w_x, v_lru_b_x, v_lru_lambda, v_conf_conv_w, v_conf_conv_b, v_conf_ln_g, v_conf_ln_b, v_w_out, v_xa_norm_g, v_mem_norm_g, v_w_q, v_w_kv, v_w_o, v_ffn_norm_g, v_w_up, v_ffn_conv_w, v_ffn_conv_b, v_w_down, v_final_norm_g):
    names = ["mix_norm_g", "w_in", "lru_conv_w", "lru_conv_b", "lru_w_a", "lru_b_a", "lru_w_x", "lru_b_x",
             "lru_lambda", "conf_conv_w", "conf_conv_b", "conf_ln_g", "conf_ln_b", "w_out", "xa_norm_g",
             "mem_norm_g", "w_q", "w_kv", "w_o", "ffn_norm_g", "w_up", "ffn_conv_w", "ffn_conv_b", "w_down",
             "final_norm_g"]
    loc = locals()
    W = {n: loc[n] for n in names}
    M = {n: loc["m_" + n] for n in names}
    V = {n: loc["v_" + n] for n in names}
    big = ["w_in", "w_out", "w_q", "w_kv", "w_o", "w_up", "w_down"]
    conv_sharded = ["lru_conv_w", "conf_conv_w", "ffn_conv_w"]

    xs = x[0]
    mems = mem[0]
    tgt = loss_target[0]
    me = 4 * lax.axis_index("x") + 2 * lax.axis_index("y") + lax.axis_index("c")

    conv_shapes = [W[n].shape[1:] for n in conv_sharded]
    conv_pack = _pack([W[n][0] for n in conv_sharded])
    gathered = _all_gather_many([W[n][0].astype(_XFER) for n in big] + [conv_pack], "gather_weights")
    g_in, g_out, g_q, g_kv, g_o, g_up, g_down, g_conv = gathered
    w_out_f = g_out.reshape(-1, g_out.shape[-1])
    w_q_f = g_q.reshape(-1, g_q.shape[-1])
    w_o_f = g_o.reshape(-1, g_o.shape[-1])
    w_down_f = g_down.reshape(-1, g_down.shape[-1])
    convs = [[] for _ in conv_sharded]
    for j in range(_NDEV):
        for idx, part in enumerate(_unpack(g_conv[j], conv_shapes)):
            convs[idx].append(part)
    lcw, ccw, fcw = [jnp.concatenate(parts, axis=-1) for parts in convs]

    wab = jnp.concatenate([_block_diag(lru_w_a[0]), _block_diag(lru_w_x[0])], axis=1).astype(_MXU)
    mixer_params = (lcw, lru_conv_b, wab, lru_b_a, lru_b_x, lru_lambda, ccw, conf_conv_b, conf_ln_g, conf_ln_b)

    h1 = _rms_fwd(xs, mix_norm_g, "rms1_fwd")
    z = _mm_nn_stacked(h1, g_in, _F32, "mm_in_fwd")
    ycat, hs, cc = _mixer_fwd(z, *mixer_params, "mixer_fwd")
    x1 = _mm_nn_nat(ycat, w_out_f, xs, _F32, "mm_out_fwd")

    h2 = _rms_fwd(x1, xa_norm_g, "rms2_fwd")
    mn = _rms_fwd(mems, mem_norm_g, "rmsm_fwd")
    q = _mm_nn_nat(h2, w_q_f, None, _MXU, "mm_q_fwd")
    kv = _mm_nn_stacked(mn, g_kv, _MXU, "mm_kv_fwd")
    o = _attn_fwd(q, kv, "attn_fwd")
    x2 = _mm_nn_nat(o, w_o_f, x1, _F32, "mm_o_fwd")

    h3 = _rms_fwd(x2, ffn_norm_g, "rms3_fwd")
    gu = _mm_nn_stacked(h3, g_up, _F32, "mm_up_fwd")
    act = _ffn_fwd(gu, fcw, ffn_conv_b, "ffn_fwd")
    x3 = _mm_nn_nat(act, w_down_f, x2, _F32, "mm_down_fwd")

    gfin = final_norm_g.reshape(1, -1)
    dx3, dx3b, lvec, dg_final = _final_loss(x3, tgt, gfin, "final_loss")
    loss_local = 0.5 * jnp.sum(lvec) / x3.shape[1]
    loss = lax.psum(loss_local, _AXES)

    dact = _mm_nt_nat(dx3b, w_down_f, _F32, "mm_down_dgrad")
    p_down = _mm_tn_nat(act, dx3b, _XFER, "mm_down_wgrad")
    dgu, dfcw, dfcb = _ffn_bwd(dact, gu, fcw, ffn_conv_b, "ffn_bwd")
    dh3 = _mm_nt_stacked(dgu, g_up, _F32, "mm_up_dgrad")
    p_up = _mm_tn_stacked(h3, dgu, _NDEV, _XFER, "mm_up_wgrad")
    dx2, dx2b, dg_ffn = _rms_bwd(dh3, x2, ffn_norm_g, dx3, "rms3_bwd")

    do = _mm_nt_nat(dx2b, w_o_f, _MXU, "mm_o_dgrad")
    p_o = _mm_tn_nat(o, dx2b, _XFER, "mm_o_wgrad")
    dq, dk, dv = _attn_bwd(q, kv, do, "attn_bwd")
    dh2 = _mm_nt_nat(dq, w_q_f, _F32, "mm_q_dgrad")
    p_q = _mm_tn_nat(h2, dq, _XFER, "mm_q_wgrad")
    dx1, dx1b, dg_xa = _rms_bwd(dh2, x1, xa_norm_g, dx2, "rms2_bwd")
    dkv = jnp.concatenate([dk, dv], axis=1).astype(_MXU)
    dmn = _mm_nt_stacked(dkv, g_kv, _F32, "mm_kv_dgrad")
    p_kv = _mm_tn_stacked(mn, dkv, _NDEV, _XFER, "mm_kv_wgrad")
    _, _, dg_mem = _rms_bwd(dmn, mems, mem_norm_g, None, "rmsm_bwd")

    dycat = _mm_nt_nat(dx1b, w_out_f, _F32, "mm_out_dgrad")
    p_out = _mm_tn_nat(ycat, dx1b, _XFER, "mm_out_wgrad")
    (dz, dlcw, dlcb, dwab, dba, dbx, dlam, dccw, dccb, dlng, dlnb) = _mixer_bwd(
        dycat, z, hs, cc, *mixer_params, "mixer_bwd")
    dh1 = _mm_nt_stacked(dz, g_in, _F32, "mm_in_dgrad")
    p_in = _mm_tn_stacked(h1, dz, _NDEV, _XFER, "mm_in_wgrad")
    grad_x, _, dg_mix = _rms_bwd(dh1, xs, mix_norm_g, dx1, "rms1_bwd")

    def rows8(p):
        return p.reshape(_NDEV, p.shape[0] // _NDEV, p.shape[1])

    parts = {"w_in": p_in, "w_out": rows8(p_out), "w_q": rows8(p_q), "w_kv": p_kv, "w_o": rows8(p_o),
             "w_up": p_up, "w_down": rows8(p_down)}
    c = _D_LRU
    heads = lru_w_a.shape[1]
    small_partial = {
        "mix_norm_g": dg_mix, "lru_conv_w": dlcw, "lru_conv_b": dlcb,
        "lru_w_a": _diag_blocks(dwab[:, :c], heads), "lru_b_a": dba,
        "lru_w_x": _diag_blocks(dwab[:, c:], heads), "lru_b_x": dbx, "lru_lambda": dlam,
        "conf_conv_w": dccw, "conf_conv_b": dccb, "conf_ln_g": dlng, "conf_ln_b": dlnb,
        "xa_norm_g": dg_xa, "mem_norm_g": dg_mem, "ffn_norm_g": dg_ffn,
        "ffn_conv_w": dfcw, "ffn_conv_b": dfcb, "final_norm_g": dg_final,
    }
    small = list(small_partial)
    small_full_shapes = [small_partial[n].shape for n in small]
    received = _all_to_all_many([parts[n] for n in big], "exchange_grads")
    small_all = _all_gather_many([_pack([small_partial[n] for n in small])], "gather_small_grads")[0]
    small_sum = _unpack(_sum8(small_all, "sum_small_grads"), small_full_shapes)

    grads, deltas, new_m, new_v = {}, {}, {}, {}
    for n, rec in zip(big, received):
        shp = W[n].shape
        w2, m2, v2 = (t.reshape(rec.shape[1:]) for t in (W[n], M[n], V[n]))
        outs = _sum_adamw(rec, w2, m2, v2, "adamw_" + n)
        grads[n], deltas[n], new_m[n], new_v[n] = (t.reshape(shp) for t in outs)

    small_g = []
    for n, g in zip(small, small_sum):
        if n in conv_sharded:
            width = W[n].shape[-1]
            g = lax.dynamic_slice_in_dim(g, me * width, width, axis=1)
        small_g.append(g.reshape(W[n].shape))
    small_shapes = [W[n].shape for n in small]
    sd, sm, sv = _adamw_flat(_pack(small_g), _pack([W[n] for n in small]), _pack([M[n] for n in small]),
                             _pack([V[n] for n in small]), "adamw_small")
    for n, g, d_, m_, v_ in zip(small, small_g, _unpack(sd, small_shapes), _unpack(sm, small_shapes),
                                _unpack(sv, small_shapes)):
        grads[n], deltas[n], new_m[n], new_v[n] = g, d_, m_, v_

    return (loss, grad_x[None], *[grads[n] for n in names], *[deltas[n] for n in names],
            *[new_m[n] for n in names], *[new_v[n] for n in names])
```

```python
import functools

import jax
import jax.numpy as jnp
from jax import lax
from jax.experimental import pallas as pl
from jax.experimental.pallas import tpu as pltpu

_MXU = jnp.bfloat16
_XFER = jnp.bfloat16
_F32 = jnp.float32
_EPS = 1e-6
_NDEV = 8
_AXES = ("x", "y", "c")
_VMEM_LIMIT = 48 * 1024 * 1024

_D_LRU = 512
_XA_HEADS = 4
_RG_C = 8.0
_ADAM_LR, _ADAM_B1, _ADAM_B2, _ADAM_EPS, _ADAM_WD, _ADAM_STEP = 0.001, 0.9, 0.999, 1e-08, 0.01, 10

_MESH_ID = pl.DeviceIdType.MESH
_ANY = pl.BlockSpec(memory_space=pl.ANY)


def _cparams(*sem):
    return pltpu.CompilerParams(dimension_semantics=tuple(sem), vmem_limit_bytes=_VMEM_LIMIT)


def _pcall(body, *, args, grid, in_specs, out_specs, out_shape, sem, name, scratch_shapes=(), comm=None):
    outs_l = list(out_shape) if isinstance(out_shape, (list, tuple)) else [out_shape]
    ospecs_l = list(out_specs) if isinstance(out_specs, (list, tuple)) else [out_specs]
    n_in, n_out, n_scr = len(args), len(outs_l), len(scratch_shapes)
    if comm is None:
        res = pl.pallas_call(
            body, grid=grid, in_specs=list(in_specs), out_specs=ospecs_l, out_shape=outs_l,
            scratch_shapes=list(scratch_shapes), compiler_params=_cparams(*sem), name=name)(*args)
        return list(res), []
    n_ci, n_co = len(comm.ins), len(comm.outs)

    def wrapped(*refs):
        ins, cins = refs[:n_in], refs[n_in:n_in + n_ci]
        o = n_in + n_ci
        outs, couts = refs[o:o + n_out], refs[o + n_out:o + n_out + n_co]
        s = o + n_out + n_co
        scr, cscr = refs[s:s + n_scr], refs[s + n_scr:]
        first = pl.program_id(0) == 0
        last = pl.program_id(0) == grid[0] - 1
        for ax in range(1, len(grid)):
            first = jnp.logical_and(first, pl.program_id(ax) == 0)
            last = jnp.logical_and(last, pl.program_id(ax) == grid[ax] - 1)

        @pl.when(first)
        def _():
            comm.start(cins, couts, cscr)

        body(*ins, *outs, *scr)

        @pl.when(last)
        def _():
            comm.finish(cins, couts, cscr)

    res = pl.pallas_call(
        wrapped, grid=grid, in_specs=list(in_specs) + [_ANY] * n_ci, out_specs=ospecs_l + [_ANY] * n_co,
        out_shape=outs_l + list(comm.outs), scratch_shapes=list(scratch_shapes) + list(comm.scratch),
        compiler_params=_cparams(*(("arbitrary",) * len(grid))), name=name)(*args, *comm.ins)
    return list(res[:n_out]), list(res[n_out:])


def _sigmoid(v):
    return 1.0 / (1.0 + jnp.exp(-v))


_GELU_C = 0.7978845608028654
_GELU_K = 0.044715


def _gelu(v):
    t = jnp.tanh(_GELU_C * (v + _GELU_K * v * v * v))
    return 0.5 * v * (1.0 + t)


def _gelu_and_grad(v):
    v2 = v * v
    t = jnp.tanh(_GELU_C * (v + _GELU_K * v2 * v))
    g = 0.5 * v * (1.0 + t)
    dg = 0.5 * (1.0 + t) + 0.5 * v * (1.0 - t * t) * (_GELU_C * (1.0 + 3.0 * _GELU_K * v2))
    return g, dg


def _softplus(v):
    e = jnp.exp(-jnp.abs(v))
    log1p = jnp.where(e < 1e-2, e * (1.0 - e * (0.5 - e * (1.0 / 3.0))), jnp.log(1.0 + e))
    return jnp.maximum(v, 0.0) + log1p


def _neg_expm1(v):
    series = -v * (1.0 + v * (0.5 + v * ((1.0 / 6.0) + v * (1.0 / 24.0))))
    return jnp.where(v > -0.0625, series, 1.0 - jnp.exp(v))


def _dot(a, b, dims):
    return lax.dot_general(a.astype(_MXU), b.astype(_MXU), (dims, ((), ())), preferred_element_type=_F32)


_NN = ((1,), (0,))
_NT = ((1,), (1,))
_TN = ((0,), (0,))


def _scan_fwd(a, b, rows):
    n = a.shape[0]
    d = 1
    while d < n:
        keep = rows >= d
        b = jnp.where(keep, b + a * pltpu.roll(b, d, 0), b)
        a = jnp.where(keep, a * pltpu.roll(a, d, 0), a)
        d *= 2
    return a, b


def _scan_rev(a, b, rows):
    n = a.shape[0]
    d = 1
    while d < n:
        keep = rows < n - d
        b = jnp.where(keep, b + a * pltpu.roll(b, n - d, 0), b)
        a = jnp.where(keep, a * pltpu.roll(a, n - d, 0), a)
        d *= 2
    return a, b


def _colsum(v):
    return jnp.sum(v, axis=0, keepdims=True)


def _mm(a, b, *, dims, grid, a_spec, b_spec, out_shape, out_spec, acc_shape, name, add=None, add_spec=None,
        comm=None):
    nred = grid[-1]
    red_axis = len(grid) - 1
    has_add = add is not None

    def body(*refs):
        if has_add:
            a_ref, b_ref, add_ref, o_ref, acc_ref = refs
        else:
            a_ref, b_ref, o_ref, acc_ref = refs
        p = _dot(a_ref[...], b_ref[...], dims)

        def finish(total):
            if has_add:
                total = total + add_ref[...]
            o_ref[...] = total.astype(o_ref.dtype)

        if nred == 1:
            finish(p)
        else:
            k = pl.program_id(red_axis)

            @pl.when(k == 0)
            def _():
                acc_ref[...] = p

            @pl.when(k > 0)
            def _():
                acc_ref[...] += p

            @pl.when(k == nred - 1)
            def _():
                finish(acc_ref[...])

    in_specs = [a_spec, b_spec] + ([add_spec] if has_add else [])
    args = (a, b) + ((add,) if has_add else ())
    sem = ("parallel",) * (len(grid) - 1) + ("arbitrary",)
    res, cres = _pcall(
        body, args=args, grid=grid, in_specs=in_specs, out_specs=out_spec, out_shape=out_shape,
        scratch_shapes=[pltpu.VMEM(acc_shape if nred > 1 else (8, 128), _F32)], sem=sem, name=name, comm=comm)
    return res[0] if comm is None else (res[0], cres)


def _tile(m, cap):
    t = min(m, cap)
    assert m % t == 0
    return t


def _mm_nn_stacked(a, w, out_dtype, name, comm=None):
    m, k = a.shape
    j, _, n = w.shape
    tm = _tile(m, 1024)
    return _mm(a, w, dims=_NN, grid=(m // tm, j, 1),
               a_spec=pl.BlockSpec((tm, k), lambda i, jj, r: (i, 0)),
               b_spec=pl.BlockSpec((None, k, n), lambda i, jj, r: (jj, 0, 0)),
               out_shape=jax.ShapeDtypeStruct((m, j * n), out_dtype),
               out_spec=pl.BlockSpec((tm, n), lambda i, jj, r: (i, jj)),
               acc_shape=(tm, n), name=name, comm=comm)


def _mm_nn_nat(a, w, add, out_dtype, name):
    m, kt = a.shape
    _, n = w.shape
    tm = _tile(m, 1024)
    tk = _tile(kt, 1024)
    return _mm(a, w, dims=_NN, grid=(m // tm, kt // tk),
               a_spec=pl.BlockSpec((tm, tk), lambda i, r: (i, r)),
               b_spec=pl.BlockSpec((tk, n), lambda i, r: (r, 0)),
               out_shape=jax.ShapeDtypeStruct((m, n), out_dtype),
               out_spec=pl.BlockSpec((tm, n), lambda i, r: (i, 0)),
               acc_shape=(tm, n), name=name,
               add=add, add_spec=pl.BlockSpec((tm, n), lambda i, r: (i, 0)))


def _mm_nt_stacked(dc, w, out_dtype, name, comm=None):
    m = dc.shape[0]
    j, k, n = w.shape
    tm = _tile(m, 1024)
    return _mm(dc, w, dims=_NT, grid=(m // tm, j),
               a_spec=pl.BlockSpec((tm, n), lambda i, r: (i, r)),
               b_spec=pl.BlockSpec((None, k, n), lambda i, r: (r, 0, 0)),
               out_shape=jax.ShapeDtypeStruct((m, k), out_dtype),
               out_spec=pl.BlockSpec((tm, k), lambda i, r: (i, 0)),
               acc_shape=(tm, k), name=name, comm=comm)


def _mm_nt_nat(dc, w, out_dtype, name):
    m, n = dc.shape
    kt = w.shape[0]
    tm = _tile(m, 1024)
    tkb = _tile(kt, 1024)
    return _mm(dc, w, dims=_NT, grid=(m // tm, kt // tkb, 1),
               a_spec=pl.BlockSpec((tm, n), lambda i, kb, r: (i, 0)),
               b_spec=pl.BlockSpec((tkb, n), lambda i, kb, r: (kb, 0)),
               out_shape=jax.ShapeDtypeStruct((m, kt), out_dtype),
               out_spec=pl.BlockSpec((tm, tkb), lambda i, kb, r: (i, kb)),
               acc_shape=(tm, tkb), name=name)


def _mm_tn_stacked(a, dc, j, out_dtype, name):
    s, k = a.shape
    n = dc.shape[1] // j
    ts = _tile(s, 1024)
    return _mm(a, dc, dims=_TN, grid=(j, s // ts),
               a_spec=pl.BlockSpec((ts, k), lambda jj, r: (r, 0)),
               b_spec=pl.BlockSpec((ts, n), lambda jj, r: (r, jj)),
               out_shape=jax.ShapeDtypeStruct((j, k, n), out_dtype),
               out_spec=pl.BlockSpec((None, k, n), lambda jj, r: (jj, 0, 0)),
               acc_shape=(k, n), name=name)


def _mm_tn_nat(a, dc, out_dtype, name):
    s, kt = a.shape
    n = dc.shape[1]
    ts = _tile(s, 1024)
    tkb = _tile(kt, 512)
    return _mm(a, dc, dims=_TN, grid=(kt // tkb, s // ts),
               a_spec=pl.BlockSpec((ts, tkb), lambda kb, r: (r, kb)),
               b_spec=pl.BlockSpec((ts, n), lambda kb, r: (r, 0)),
               out_shape=jax.ShapeDtypeStruct((kt, n), out_dtype),
               out_spec=pl.BlockSpec((tkb, n), lambda kb, r: (kb, 0)),
               acc_shape=(tkb, n), name=name)


def _rms_fwd(x, g, name, comm=None):
    s, d = x.shape
    t = _tile(s, 256)

    def body(x_ref, g_ref, h_ref):
        xv = x_ref[...]
        r = lax.rsqrt(jnp.mean(xv * xv, axis=-1, keepdims=True) + _EPS)
        h_ref[...] = (xv * r * g_ref[...]).astype(h_ref.dtype)

    res, cres = _pcall(
        body, args=(x, g), grid=(s // t,),
        in_specs=[pl.BlockSpec((t, d), lambda i: (i, 0)), pl.BlockSpec((1, d), lambda i: (0, 0))],
        out_specs=pl.BlockSpec((t, d), lambda i: (i, 0)),
        out_shape=jax.ShapeDtypeStruct((s, d), _MXU), sem=("parallel",), name=name, comm=comm)
    return res[0] if comm is None else (res[0], cres)


def _rms_bwd(dh, x, g, dres, name):
    s, d = x.shape
    t = _tile(s, 256)
    has_res = dres is not None

    def body(*refs):
        if has_res:
            dh_ref, x_ref, g_ref, dres_ref, dx_ref, dxb_ref, dg_ref = refs
        else:
            dh_ref, x_ref, g_ref, dx_ref, dxb_ref, dg_ref = refs
        xv = x_ref[...]
        dhv = dh_ref[...]
        r = lax.rsqrt(jnp.mean(xv * xv, axis=-1, keepdims=True) + _EPS)
        xhat = xv * r
        dxh = dhv * g_ref[...]
        dx = r * (dxh - xhat * jnp.mean(dxh * xhat, axis=-1, keepdims=True))
        if has_res:
            dx = dx + dres_ref[...]
        dx_ref[...] = dx
        dxb_ref[...] = dx.astype(dxb_ref.dtype)

        @pl.when(pl.program_id(0) == 0)
        def _():
            dg_ref[...] = jnp.zeros_like(dg_ref)

        dg_ref[...] += _colsum(dhv * xhat)

    row = pl.BlockSpec((t, d), lambda i: (i, 0))
    vec = pl.BlockSpec((1, d), lambda i: (0, 0))
    in_specs = [row, row, vec] + ([row] if has_res else [])
    args = (dh, x, g) + ((dres,) if has_res else ())
    return pl.pallas_call(
        body, grid=(s // t,), in_specs=in_specs, out_specs=[row, row, vec],
        out_shape=[jax.ShapeDtypeStruct((s, d), _F32), jax.ShapeDtypeStruct((s, d), _MXU),
                   jax.ShapeDtypeStruct((1, d), _F32)],
        compiler_params=_cparams("arbitrary"), name=name,
    )(*args)


def _final_loss(x3, target, g, name):
    s, d = x3.shape
    t = _tile(s, 256)

    def body(x_ref, t_ref, g_ref, dx_ref, dxb_ref, l_ref, dg_ref):
        xv = x_ref[...]
        gv = g_ref[...]
        r = lax.rsqrt(jnp.mean(xv * xv, axis=-1, keepdims=True) + _EPS)
        xhat = xv * r
        err = xhat * gv - t_ref[...]
        dy = err * (1.0 / d)
        dxh = dy * gv
        dx = r * (dxh - xhat * jnp.mean(dxh * xhat, axis=-1, keepdims=True))
        dx_ref[...] = dx
        dxb_ref[...] = dx.astype(dxb_ref.dtype)

        @pl.when(pl.program_id(0) == 0)
        def _():
            l_ref[...] = jnp.zeros_like(l_ref)
            dg_ref[...] = jnp.zeros_like(dg_ref)

        l_ref[...] += _colsum(err * err)
        dg_ref[...] += _colsum(dy * xhat)

    row = pl.BlockSpec((t, d), lambda i: (i, 0))
    vec = pl.BlockSpec((1, d), lambda i: (0, 0))
    return pl.pallas_call(
        body, grid=(s // t,), in_specs=[row, row, vec], out_specs=[row, row, vec, vec],
        out_shape=[jax.ShapeDtypeStruct((s, d), _F32), jax.ShapeDtypeStruct((s, d), _MXU),
                   jax.ShapeDtypeStruct((1, d), _F32), jax.ShapeDtypeStruct((1, d), _F32)],
        compiler_params=_cparams("arbitrary"), name=name,
    )(x3, target, g)


_LRU_K = 4
_CONF_K = 31
_LRU_HALO = 8
_CONF_HALO = 32
_MIX_T = 256


def _lru_gates(lx, wab_ref, ba_ref, bx_ref, lam_ref):
    c = _D_LRU
    pre = _dot(lx, wab_ref[...], _NN)
    r = _sigmoid(pre[:, :c] + ba_ref[...])
    ig = _sigmoid(pre[:, c:] + bx_ref[...])
    sp = _softplus(-lam_ref[...])
    log_a = (-_RG_C) * r * sp
    a = jnp.exp(log_a)
    mult = jnp.sqrt(_neg_expm1(2.0 * log_a))
    return r, ig, sp, a, mult


def _causal_conv(ext_ref, halo, w_ref, b_ref, taps, t):
    acc = b_ref[...] + w_ref[0:1, :] * ext_ref[pl.ds(halo - (taps - 1), t), :]
    for k in range(1, taps):
        acc = acc + w_ref[k:k + 1, :] * ext_ref[pl.ds(halo - (taps - 1) + k, t), :]
    return acc


def _mixer_fwd(z, lcw, lcb, wab, ba, bx, lam, ccw, ccb, lng, lnb, name, comm=None):
    s = z.shape[0]
    c = _D_LRU
    t = _tile(s, _MIX_T)
    nt = s // t

    def body(lx0_ref, lx0h_ref, gate_ref, ca_ref, cah_ref, cb_ref, cbh_ref,
             lcw_ref, lcb_ref, wab_ref, ba_ref, bx_ref, lam_ref, ccw_ref, ccb_ref, lng_ref, lnb_ref,
             ycat_ref, hs_ref, cc_ref, ext_ref, cge_ref, hc_ref):
        i = pl.program_id(0)
        first = i == 0
        rows = lax.broadcasted_iota(jnp.int32, (t, c), 0)

        ext_ref[0:_LRU_HALO, :] = jnp.where(first, 0.0, lx0h_ref[...])
        ext_ref[_LRU_HALO:_LRU_HALO + t, :] = lx0_ref[...]
        lx = _causal_conv(ext_ref, _LRU_HALO, lcw_ref, lcb_ref, _LRU_K, t)
        r, ig, sp, a, mult = _lru_gates(lx, wab_ref, ba_ref, bx_ref, lam_ref)
        u = mult * (ig * lx)
        a_cum, h_loc = _scan_fwd(a, u, rows)

        @pl.when(first)
        def _():
            hc_ref[...] = jnp.zeros_like(hc_ref)

        h = h_loc + a_cum * hc_ref[7:8, :]
        hs_ref[...] = h
        hc_ref[...] = hs_ref[pl.ds(t - 8, 8), :]
        ycat_ref[:, 0:c] = (h * _gelu(gate_ref[...])).astype(ycat_ref.dtype)

        cge_ref[0:_CONF_HALO, :] = jnp.where(first, 0.0, cah_ref[...] * _sigmoid(cbh_ref[...]))
        cge_ref[_CONF_HALO:_CONF_HALO + t, :] = ca_ref[...] * _sigmoid(cb_ref[...])
        cc = _causal_conv(cge_ref, _CONF_HALO, ccw_ref, ccb_ref, _CONF_K, t)
        cc_ref[...] = cc
        xc = cc - jnp.mean(cc, axis=-1, keepdims=True)
        rstd = lax.rsqrt(jnp.mean(xc * xc, axis=-1, keepdims=True) + _EPS)
        ln = xc * rstd * lng_ref[...] + lnb_ref[...]
        ycat_ref[:, c:2 * c] = (ln * _sigmoid(ln)).astype(ycat_ref.dtype)

    def col(j):
        return pl.BlockSpec((t, c), lambda i: (i, j))

    def halo(j, rows_):
        per = t // rows_
        return pl.BlockSpec((rows_, c), lambda i: (jnp.maximum(i * per - 1, 0), j))

    def full(arr):
        return pl.BlockSpec(arr.shape, lambda i: (0,) * arr.ndim)

    params = (lcw, lcb, wab, ba, bx, lam, ccw, ccb, lng, lnb)
    res, cres = _pcall(
        body, args=(z, z, z, z, z, z, z, *params), grid=(nt,),
        in_specs=[col(0), halo(0, _LRU_HALO), col(1), col(2), halo(2, _CONF_HALO), col(3), halo(3, _CONF_HALO)]
        + [full(p) for p in params],
        out_specs=[pl.BlockSpec((t, 2 * c), lambda i: (i, 0)), pl.BlockSpec((t, c), lambda i: (i, 0)),
                   pl.BlockSpec((t, c), lambda i: (i, 0))],
        out_shape=[jax.ShapeDtypeStruct((s, 2 * c), _MXU), jax.ShapeDtypeStruct((s, c), _F32),
                   jax.ShapeDtypeStruct((s, c), _F32)],
        scratch_shapes=[pltpu.VMEM((t + _LRU_HALO, c), _F32), pltpu.VMEM((t + _CONF_HALO, c), _F32),
                        pltpu.VMEM((8, c), _F32)],
        sem=("arbitrary",), name=name, comm=comm)
    return res, cres


def _mixer_bwd(dycat, z, hs, cc, lcw, lcb, wab, ba, bx, lam, ccw, ccb, lng, lnb, name, comm=None):
    s = z.shape[0]
    c = _D_LRU
    t = _tile(s, _MIX_T)
    nt = s // t

    def body(dyl_ref, dc_ref, lx0_ref, lx0h_ref, gate_ref, ca_ref, cah_ref, cb_ref, cbh_ref,
             hs_ref, hsh_ref, cc_ref,
             lcw_ref, lcb_ref, wab_ref, ba_ref, bx_ref, lam_ref, ccw_ref, ccb_ref, lng_ref, lnb_ref,
             dz_ref, dlcw_ref, dlcb_ref, dwab_ref, dba_ref, dbx_ref, dlam_ref, dccw_ref, dccb_ref, dlng_ref,
             dlnb_ref,
             ext_ref, up_ref, cge_ref, dce_ref, xc_ref, dlxc_ref, dccc_ref):
        i = pl.program_id(0)
        first_tile = i == nt - 1
        last_tile = i == 0
        rows = lax.broadcasted_iota(jnp.int32, (t, c), 0)

        @pl.when(last_tile)
        def _():
            for ref in (dlcw_ref, dlcb_ref, dwab_ref, dba_ref, dbx_ref, dlam_ref, dccw_ref, dccb_ref, dlng_ref,
                        dlnb_ref, xc_ref, dlxc_ref, dccc_ref):
                ref[...] = jnp.zeros_like(ref)

        ext_ref[0:_LRU_HALO, :] = jnp.where(first_tile, 0.0, lx0h_ref[...])
        ext_ref[_LRU_HALO:_LRU_HALO + t, :] = lx0_ref[...]
        lx = _causal_conv(ext_ref, _LRU_HALO, lcw_ref, lcb_ref, _LRU_K, t)
        r, ig, sp, a, mult = _lru_gates(lx, wab_ref, ba_ref, bx_ref, lam_ref)
        h = hs_ref[...]
        gl, dgl = _gelu_and_grad(gate_ref[...])
        dyl = dyl_ref[...]
        dz_ref[:, c:2 * c] = (dyl * h * dgl).astype(dz_ref.dtype)
        dh = dyl * gl

        up_ref[0:t, :] = a
        up_ref[t:t + 8, :] = jnp.ones((8, c), _F32)
        a_up = up_ref[pl.ds(1, t), :]
        a_cum, g_loc = _scan_rev(a_up, dh, rows)
        gt = g_loc + a_cum * xc_ref[0:1, :]
        xc_ref[...] = (a * gt)[0:8, :]

        up_ref[0:8, :] = jnp.where(first_tile, 0.0, hsh_ref[...])
        up_ref[8:8 + t, :] = h
        hprev = up_ref[pl.ds(7, t), :]

        da = gt * hprev
        dmult = gt * ig * lx
        dig = gt * mult * lx
        dlx = gt * mult * ig
        dlog_a = da * a - dmult * a * a / mult
        dpre_r = dlog_a * (-_RG_C) * sp * r * (1.0 - r)
        dpre_i = dig * ig * (1.0 - ig)
        dlam_ref[...] += _colsum(dlog_a * r) * (_RG_C * _sigmoid(-lam_ref[...]))
        dba_ref[...] += _colsum(dpre_r)
        dbx_ref[...] += _colsum(dpre_i)
        dpre = jnp.concatenate([dpre_r, dpre_i], axis=1).astype(_MXU)
        dlx = dlx + _dot(dpre, wab_ref[...], _NT)
        dwab_ref[...] += _dot(lx, dpre, _TN)

        dlcb_ref[...] += _colsum(dlx)
        up_ref[0:t, :] = dlx
        up_ref[t:t + 8, :] = dlxc_ref[...]
        dlxc_ref[...] = dlx[0:8, :]
        acc = lcw_ref[0:1, :] * up_ref[pl.ds(_LRU_K - 1, t), :]
        for k in range(1, _LRU_K):
            acc = acc + lcw_ref[k:k + 1, :] * up_ref[pl.ds(_LRU_K - 1 - k, t), :]
        dz_ref[:, 0:c] = acc.astype(dz_ref.dtype)
        for k in range(_LRU_K):
            dlcw_ref[k:k + 1, :] += _colsum(dlx * ext_ref[pl.ds(_LRU_HALO - (_LRU_K - 1) + k, t), :])

        sig_b = _sigmoid(cb_ref[...])
        ca = ca_ref[...]
        cge_ref[0:_CONF_HALO, :] = jnp.where(first_tile, 0.0, cah_ref[...] * _sigmoid(cbh_ref[...]))
        cge_ref[_CONF_HALO:_CONF_HALO + t, :] = ca * sig_b
        ccv = cc_ref[...]
        xcen = ccv - jnp.mean(ccv, axis=-1, keepdims=True)
        rstd = lax.rsqrt(jnp.mean(xcen * xcen, axis=-1, keepdims=True) + _EPS)
        xn = xcen * rstd
        ln = xn * lng_ref[...] + lnb_ref[...]
        sg = _sigmoid(ln)
        dln = dc_ref[...] * (sg * (1.0 + ln * (1.0 - sg)))
        dlng_ref[...] += _colsum(dln * xn)
        dlnb_ref[...] += _colsum(dln)
        dxn = dln * lng_ref[...]
        dcc = rstd * (dxn - jnp.mean(dxn, axis=-1, keepdims=True)
                      - xn * jnp.mean(dxn * xn, axis=-1, keepdims=True))
        dccb_ref[...] += _colsum(dcc)
        for k in range(_CONF_K):
            dccw_ref[k:k + 1, :] += _colsum(dcc * cge_ref[pl.ds(_CONF_HALO - (_CONF_K - 1) + k, t), :])
        dce_ref[0:t, :] = dcc
        dce_ref[t:t + _CONF_HALO, :] = dccc_ref[...]
        dccc_ref[...] = dcc[0:_CONF_HALO, :]
        dcg = ccw_ref[0:1, :] * dce_ref[pl.ds(_CONF_K - 1, t), :]
        for k in range(1, _CONF_K):
            dcg = dcg + ccw_ref[k:k + 1, :] * dce_ref[pl.ds(_CONF_K - 1 - k, t), :]
        dz_ref[:, 2 * c:3 * c] = (dcg * sig_b).astype(dz_ref.dtype)
        dz_ref[:, 3 * c:4 * c] = (dcg * ca * sig_b * (1.0 - sig_b)).astype(dz_ref.dtype)

    def col(j):
        return pl.BlockSpec((t, c), lambda i: (nt - 1 - i, j))

    def halo(j, rows_):
        per = t // rows_
        return pl.BlockSpec((rows_, c), lambda i: (jnp.maximum((nt - 1 - i) * per - 1, 0), j))

    def full(shape):
        return pl.BlockSpec(shape, lambda i: (0,) * len(shape))

    params = (lcw, lcb, wab, ba, bx, lam, ccw, ccb, lng, lnb)
    small = [(_LRU_K, c), (1, c), (c, 2 * c), (1, c), (1, c), (1, c), (_CONF_K, c), (1, c), (1, c), (1, c)]
    return _pcall(
        body, args=(dycat, dycat, z, z, z, z, z, z, z, hs, hs, cc, *params), grid=(nt,),
        in_specs=[col(0), col(1),
                  col(0), halo(0, _LRU_HALO), col(1), col(2), halo(2, _CONF_HALO), col(3), halo(3, _CONF_HALO),
                  col(0), halo(0, 8), col(0)]
        + [full(p.shape) for p in params],
        out_specs=[pl.BlockSpec((t, 4 * c), lambda i: (nt - 1 - i, 0))] + [full(sh) for sh in small],
        out_shape=[jax.ShapeDtypeStruct((s, 4 * c), _MXU)] + [jax.ShapeDtypeStruct(sh, _F32) for sh in small],
        scratch_shapes=[pltpu.VMEM((t + _LRU_HALO, c), _F32), pltpu.VMEM((t + 8, c), _F32),
                        pltpu.VMEM((t + _CONF_HALO, c), _F32), pltpu.VMEM((t + _CONF_HALO, c), _F32),
                        pltpu.VMEM((8, c), _F32), pltpu.VMEM((8, c), _F32), pltpu.VMEM((_CONF_HALO, c), _F32)],
        sem=("arbitrary",), name=name, comm=comm)


_ATT_T = 512


def _attn_probs(qh, kh, scale):
    sc = _dot(qh, kh, _NT) * scale
    e = jnp.exp(sc - jnp.max(sc, axis=-1, keepdims=True))
    return e / jnp.sum(e, axis=-1, keepdims=True)


def _attn_fwd(q, kv, name):
    s, d = q.shape
    nm = kv.shape[0]
    hd = d // _XA_HEADS
    t = _tile(s, _ATT_T)
    scale = hd ** -0.5

    def body(q_ref, k_ref, v_ref, o_ref):
        for hh in range(_XA_HEADS):
            sl = slice(hh * hd, (hh + 1) * hd)
            p = _attn_probs(q_ref[:, sl], k_ref[:, sl], scale)
            o_ref[:, sl] = _dot(p, v_ref[:, sl], _NN).astype(o_ref.dtype)

    return pl.pallas_call(
        body, grid=(s // t,),
        in_specs=[pl.BlockSpec((t, d), lambda i: (i, 0)), pl.BlockSpec((nm, d), lambda i: (0, 0)),
                  pl.BlockSpec((nm, d), lambda i: (0, 1))],
        out_specs=pl.BlockSpec((t, d), lambda i: (i, 0)),
        out_shape=jax.ShapeDtypeStruct((s, d), _MXU),
        compiler_params=_cparams("parallel"), name=name,
    )(q, kv, kv)


def _attn_bwd(q, kv, do, name):
    s, d = q.shape
    nm = kv.shape[0]
    hd = d // _XA_HEADS
    t = _tile(s, _ATT_T)
    scale = hd ** -0.5

    def body(q_ref, k_ref, v_ref, do_ref, dq_ref, dk_ref, dv_ref):
        @pl.when(pl.program_id(0) == 0)
        def _():
            dk_ref[...] = jnp.zeros_like(dk_ref)
            dv_ref[...] = jnp.zeros_like(dv_ref)

        for hh in range(_XA_HEADS):
            sl = slice(hh * hd, (hh + 1) * hd)
            qh = q_ref[:, sl]
            kh = k_ref[:, sl]
            doh = do_ref[:, sl]
            p = _attn_probs(qh, kh, scale)
            dp = _dot(doh, v_ref[:, sl], _NT)
            dv_ref[:, sl] += _dot(p, doh, _TN)
            ds = (p * (dp - jnp.sum(dp * p, axis=-1, keepdims=True)) * scale).astype(_MXU)
            dq_ref[:, sl] = _dot(ds, kh, _NN).astype(dq_ref.dtype)
            dk_ref[:, sl] += _dot(ds, qh, _TN)

    row = pl.BlockSpec((t, d), lambda i: (i, 0))
    return pl.pallas_call(
        body, grid=(s // t,),
        in_specs=[row, pl.BlockSpec((nm, d), lambda i: (0, 0)), pl.BlockSpec((nm, d), lambda i: (0, 1)), row],
        out_specs=[row, pl.BlockSpec((nm, d), lambda i: (0, 0)), pl.BlockSpec((nm, d), lambda i: (0, 0))],
        out_shape=[jax.ShapeDtypeStruct((s, d), _MXU), jax.ShapeDtypeStruct((nm, d), _F32),
                   jax.ShapeDtypeStruct((nm, d), _F32)],
        compiler_params=_cparams("arbitrary"), name=name,
    )(q, kv, kv, do)


_FFN_K = 3
_FFN_T = 128
_FFN_CB = 512


def _ffn_fwd(gu, fcw, fcb, name):
    s = gu.shape[0]
    f = gu.shape[1] // 2
    t = _tile(s, 256)
    cb = _tile(f, 768)
    ncb = f // cb

    def body(g0_ref, g0h_ref, u_ref, w_ref, b_ref, act_ref, ext_ref):
        first = pl.program_id(0) == 0
        ext_ref[0:8, :] = jnp.where(first, 0.0, g0h_ref[...])
        ext_ref[8:8 + t, :] = g0_ref[...]
        g = _causal_conv(ext_ref, 8, w_ref, b_ref, _FFN_K, t)
        act_ref[...] = (_gelu(g) * u_ref[...]).astype(act_ref.dtype)

    per = t // 8
    return pl.pallas_call(
        body, grid=(s // t, ncb),
        in_specs=[pl.BlockSpec((t, cb), lambda i, j: (i, j)),
                  pl.BlockSpec((8, cb), lambda i, j: (jnp.maximum(i * per - 1, 0), j)),
                  pl.BlockSpec((t, cb), lambda i, j: (i, j + ncb)),
                  pl.BlockSpec((_FFN_K, cb), lambda i, j: (0, j)),
                  pl.BlockSpec((1, cb), lambda i, j: (0, j))],
        out_specs=pl.BlockSpec((t, cb), lambda i, j: (i, j)),
        out_shape=jax.ShapeDtypeStruct((s, f), _MXU),
        scratch_shapes=[pltpu.VMEM((t + 8, cb), _F32)],
        compiler_params=_cparams("parallel", "parallel"), name=name,
    )(gu, gu, gu, fcw, fcb)


def _ffn_bwd(dact, gu, fcw, fcb, name, comm=None):
    s = gu.shape[0]
    f = gu.shape[1] // 2
    t = _tile(s, _FFN_T)
    nt = s // t
    cb = _tile(f, _FFN_CB)

    def body(dact_ref, g0_ref, g0h_ref, u_ref, w_ref, b_ref, dgu_ref, dw_ref, db_ref, ext_ref, up_ref, car_ref):
        i = pl.program_id(0)
        first_tile = i == nt - 1

        @pl.when(i == 0)
        def _():
            dw_ref[...] = jnp.zeros_like(dw_ref)
            db_ref[...] = jnp.zeros_like(db_ref)
            car_ref[...] = jnp.zeros_like(car_ref)

        for j in range(f // cb):
            cs = slice(j * cb, (j + 1) * cb)
            ext_ref[0:8, :] = jnp.where(first_tile, 0.0, g0h_ref[:, cs])
            ext_ref[8:8 + t, :] = g0_ref[:, cs]
            g = _causal_conv(ext_ref, 8, w_ref.at[:, cs], b_ref.at[:, cs], _FFN_K, t)
            gl, dgl = _gelu_and_grad(g)
            da = dact_ref[:, cs]
            dgu_ref[:, f + j * cb:f + (j + 1) * cb] = (da * gl).astype(dgu_ref.dtype)
            dg = da * u_ref[:, cs] * dgl
            db_ref[:, cs] += _colsum(dg)
            for k in range(_FFN_K):
                dw_ref[k:k + 1, cs] += _colsum(dg * ext_ref[pl.ds(8 - (_FFN_K - 1) + k, t), :])
            up_ref[0:t, :] = dg
            up_ref[t:t + 8, :] = car_ref[:, cs]
            car_ref[:, cs] = dg[0:8, :]
            acc = w_ref[0:1, cs] * up_ref[pl.ds(_FFN_K - 1, t), :]
            for k in range(1, _FFN_K):
                acc = acc + w_ref[k:k + 1, cs] * up_ref[pl.ds(_FFN_K - 1 - k, t), :]
            dgu_ref[:, cs] = acc.astype(dgu_ref.dtype)

    per = t // 8
    return _pcall(
        body, args=(dact, gu, gu, gu, fcw, fcb), grid=(nt,),
        in_specs=[pl.BlockSpec((t, f), lambda i: (nt - 1 - i, 0)),
                  pl.BlockSpec((t, f), lambda i: (nt - 1 - i, 0)),
                  pl.BlockSpec((8, f), lambda i: (jnp.maximum((nt - 1 - i) * per - 1, 0), 0)),
                  pl.BlockSpec((t, f), lambda i: (nt - 1 - i, 1)),
                  pl.BlockSpec((_FFN_K, f), lambda i: (0, 0)),
                  pl.BlockSpec((1, f), lambda i: (0, 0))],
        out_specs=[pl.BlockSpec((t, 2 * f), lambda i: (nt - 1 - i, 0)),
                   pl.BlockSpec((_FFN_K, f), lambda i: (0, 0)), pl.BlockSpec((1, f), lambda i: (0, 0))],
        out_shape=[jax.ShapeDtypeStruct((s, 2 * f), _MXU), jax.ShapeDtypeStruct((_FFN_K, f), _F32),
                   jax.ShapeDtypeStruct((1, f), _F32)],
        scratch_shapes=[pltpu.VMEM((t + 8, cb), _F32), pltpu.VMEM((t + 8, cb), _F32), pltpu.VMEM((8, f), _F32)],
        sem=("arbitrary",), name=name, comm=comm)


def _mesh_pos():
    return lax.axis_index("x"), lax.axis_index("y"), lax.axis_index("c")


def _flip(v, bit):
    return 1 - v if bit else v


def _sem_scratch(n):
    return [pltpu.SemaphoreType.DMA((7 * n,)), pltpu.SemaphoreType.DMA((7 * n,)), pltpu.SemaphoreType.DMA((n,))]


class _Gather:
    def __init__(self, xs):
        self.ins = list(xs)
        self.outs = [jax.ShapeDtypeStruct((_NDEV,) + v.shape, v.dtype) for v in xs]
        self.scratch = _sem_scratch(len(xs))

    def _plan(self, x_refs, out_refs, sems):
        send_sems, recv_sems, local_sems = sems
        x, y, c = _mesh_pos()
        me, sibling = (x, y, c), (x, y, 1 - c)
        chips = [(1 - x, y), (x, 1 - y), (1 - x, 1 - y)]

        def copy(a, k, block, to, src=None):
            slot = out_refs[a].at[4 * block[0] + 2 * block[1] + block[2]]
            return pltpu.make_async_remote_copy(
                src_ref=slot if src is None else src, dst_ref=slot,
                send_sem=send_sems.at[a * 7 + k], recv_sem=recv_sems.at[a * 7 + k],
                device_id=to, device_id_type=_MESH_ID)

        def own(a):
            return pltpu.make_async_copy(x_refs[a], out_refs[a].at[4 * x + 2 * y + c], local_sems.at[a])

        def first(a):
            return [copy(a, 0, me, sibling, src=x_refs[a])] + [
                copy(a, 1 + j, me, (*chip, c), src=x_refs[a]) for j, chip in enumerate(chips)]

        return me, sibling, chips, c, copy, own, first

    def start(self, x_refs, out_refs, sems):
        _, _, _, _, _, own, first = self._plan(x_refs, out_refs, sems)
        for a in range(len(self.ins)):
            own(a).start()
            for cp in first(a):
                cp.start()

    def finish(self, x_refs, out_refs, sems):
        me, sibling, chips, c, copy, own, first = self._plan(x_refs, out_refs, sems)
        n = len(self.ins)
        passed = []
        for a in range(n):
            for j, chip in enumerate(chips):
                copy(a, 1 + j, (*chip, c), me).wait_recv()
                fwd = copy(a, 4 + j, (*chip, c), sibling)
                fwd.start()
                passed.append(fwd)
        for a in range(n):
            copy(a, 0, sibling, me).wait_recv()
            for j, chip in enumerate(chips):
                copy(a, 4 + j, (*chip, 1 - c), me).wait_recv()
        for a in range(n):
            for cp in first(a):
                cp.wait_send()
        for cp in passed:
            cp.wait_send()
        for a in range(n):
            own(a).wait()


class _Exchange:
    def __init__(self, gs):
        self.ins = list(gs)
        self.outs = [jax.ShapeDtypeStruct(v.shape, v.dtype) for v in gs]
        self.scratch = _sem_scratch(len(gs))

    def _plan(self, g_refs, r_refs, sems):
        send_sems, recv_sems, local_sems = sems
        x, y, c = _mesh_pos()
        me_idx = 4 * x + 2 * y + c
        n = len(self.ins)

        def copy(a, k):
            peer = (_flip(x, k & 4), _flip(y, k & 2), _flip(c, k & 1))
            peer_idx = 4 * peer[0] + 2 * peer[1] + peer[2]
            return pltpu.make_async_remote_copy(
                src_ref=g_refs[a].at[peer_idx], dst_ref=r_refs[a].at[me_idx],
                send_sem=send_sems.at[a * 7 + k - 1], recv_sem=recv_sems.at[a * 7 + k - 1],
                device_id=peer, device_id_type=_MESH_ID)

        copies = [copy(a, k) for a in range(n) for k in range(1, _NDEV)]
        mine = [pltpu.make_async_copy(g_refs[a].at[me_idx], r_refs[a].at[me_idx], local_sems.at[a])
                for a in range(n)]
        return copies, mine

    def start(self, g_refs, r_refs, sems):
        copies, mine = self._plan(g_refs, r_refs, sems)
        for cp in copies + mine:
            cp.start()

    def finish(self, g_refs, r_refs, sems):
        copies, mine = self._plan(g_refs, r_refs, sems)
        for cp in copies:
            cp.wait_recv()
        for cp in copies:
            cp.wait_send()
        for cp in mine:
            cp.wait()


class _Both:
    def __init__(self, first, second):
        self.parts = (first, second)
        self.ins = first.ins + second.ins
        self.outs = first.outs + second.outs
        self.scratch = first.scratch + second.scratch

    def _split(self, ins, outs, sems):
        a, b = self.parts
        na, nb = len(a.ins), len(a.scratch)
        return (a, ins[:na], outs[:na], sems[:nb]), (b, ins[na:], outs[na:], sems[nb:])

    def start(self, ins, outs, sems):
        for part, i, o, s in self._split(ins, outs, sems):
            part.start(i, o, s)

    def finish(self, ins, outs, sems):
        for part, i, o, s in self._split(ins, outs, sems):
            part.finish(i, o, s)


def _comm_call(comm, name):
    def body(*refs):
        n_i, n_o = len(comm.ins), len(comm.outs)
        ins, outs, sems = refs[:n_i], refs[n_i:n_i + n_o], refs[n_i + n_o:]
        comm.start(ins, outs, sems)
        comm.finish(ins, outs, sems)

    return pl.pallas_call(
        body, out_shape=list(comm.outs), in_specs=[_ANY] * len(comm.ins), out_specs=[_ANY] * len(comm.outs),
        scratch_shapes=list(comm.scratch), name=name)(*comm.ins)


def _adamw_math(w, g, m, v):
    m = _ADAM_B1 * m + (1.0 - _ADAM_B1) * g
    v = _ADAM_B2 * v + (1.0 - _ADAM_B2) * (g * g)
    m_hat = m / (1.0 - _ADAM_B1 ** _ADAM_STEP)
    v_hat = v / (1.0 - _ADAM_B2 ** _ADAM_STEP)
    delta = -_ADAM_LR * (m_hat / (jnp.sqrt(v_hat) + _ADAM_EPS) + _ADAM_WD * w)
    return delta, m, v


def _sum_adamw(parts, w, m, v, name):
    r, c = w.shape
    tr = _tile(r, 128)

    def body(p_ref, w_ref, m_ref, v_ref, g_ref, d_ref, nm_ref, nv_ref):
        g = p_ref[0].astype(_F32)
        for j in range(1, _NDEV):
            g = g + p_ref[j].astype(_F32)
        delta, nm, nv = _adamw_math(w_ref[...], g, m_ref[...], v_ref[...])
        g_ref[...] = g
        d_ref[...] = delta
        nm_ref[...] = nm
        nv_ref[...] = nv

    blk = pl.BlockSpec((tr, c), lambda i: (i, 0))
    return pl.pallas_call(
        body, grid=(r // tr,),
        in_specs=[pl.BlockSpec((_NDEV, tr, c), lambda i: (0, i, 0)), blk, blk, blk],
        out_specs=[blk] * 4, out_shape=[jax.ShapeDtypeStruct((r, c), _F32)] * 4,
        compiler_params=_cparams("parallel"), name=name,
    )(parts, w, m, v)


def _sum8(parts, name):
    _, r, c = parts.shape

    def body(p_ref, o_ref):
        g = p_ref[0]
        for j in range(1, _NDEV):
            g = g + p_ref[j]
        o_ref[...] = g

    return pl.pallas_call(
        body, grid=(1,), in_specs=[pl.BlockSpec((_NDEV, r, c), lambda i: (0, 0, 0))],
        out_specs=pl.BlockSpec((r, c), lambda i: (0, 0)), out_shape=jax.ShapeDtypeStruct((r, c), _F32),
        compiler_params=_cparams("arbitrary"), name=name,
    )(parts)


def _adamw_flat(g, w, m, v, name):
    r, c = w.shape

    def body(g_ref, w_ref, m_ref, v_ref, d_ref, nm_ref, nv_ref):
        delta, nm, nv = _adamw_math(w_ref[...], g_ref[...], m_ref[...], v_ref[...])
        d_ref[...] = delta
        nm_ref[...] = nm
        nv_ref[...] = nv

    blk = pl.BlockSpec((r, c), lambda i: (0, 0))
    return pl.pallas_call(
        body, grid=(1,), in_specs=[blk] * 4, out_specs=[blk] * 3,
        out_shape=[jax.ShapeDtypeStruct((r, c), _F32)] * 3,
        compiler_params=_cparams("arbitrary"), name=name,
    )(g, w, m, v)


def _pack(arrs):
    flat = jnp.concatenate([a.reshape(-1).astype(_F32) for a in arrs])
    pad = (-flat.shape[0]) % 1024
    return jnp.pad(flat, (0, pad)).reshape(-1, 128)


def _unpack(flat2d, shapes):
    flat = flat2d.reshape(-1)
    out, off = [], 0
    for sh in shapes:
        size = 1
        for dim in sh:
            size *= dim
        out.append(flat[off:off + size].reshape(sh))
        off += size
    return out


def _block_diag(w):
    h, hd, _ = w.shape
    eye = jnp.eye(h, dtype=w.dtype)
    return (eye[:, None, :, None] * w[:, :, None, :]).reshape(h * hd, h * hd)


def _diag_blocks(full, h):
    hd = full.shape[0] // h
    return jnp.stack([full[i * hd:(i + 1) * hd, i * hd:(i + 1) * hd] for i in range(h)])


def _ungather_cols(g):
    j, k, n = g.shape
    return g.transpose(1, 0, 2).reshape(k, j * n)


def kernel(x, mem, mix_norm_g, w_in, lru_conv_w, lru_conv_b, lru_w_a, lru_b_a, lru_w_x, lru_b_x, lru_lambda, conf_conv_w, conf_conv_b, conf_ln_g, conf_ln_b, w_out, xa_norm_g, mem_norm_g, w_q, w_kv, w_o, ffn_norm_g, w_up, ffn_conv_w, ffn_conv_b, w_down, final_norm_g, loss_target, m_mix_norm_g, m_w_in, m_lru_conv_w, m_lru_conv_b, m_lru_w_a, m_lru_b_a, m_lru_w_x, m_lru_b_x, m_lru_lambda, m_conf_conv_w, m_conf_conv_b, m_conf_ln_g, m_conf_ln_b, m_w_out, m_xa_norm_g, m_mem_norm_g, m_w_q, m_w_kv, m_w_o, m_ffn_norm_g, m_w_up, m_ffn_conv_w, m_ffn_conv_b, m_w_down, m_final_norm_g, v_mix_norm_g, v_w_in, v_lru_conv_w, v_lru_conv_b, v_lru_w_a, v_lru_b_a, v_lru_w_x, v_lru_b_x, v_lru_lambda, v_conf_conv_w, v_conf_conv_b, v_conf_ln_g, v_conf_ln_b, v_w_out, v_xa_norm_g, v_mem_norm_g, v_w_q, v_w_kv, v_w_o, v_ffn_norm_g, v_w_up, v_ffn_conv_w, v_ffn_conv_b, v_w_down, v_final_norm_g):
    names = ["mix_norm_g", "w_in", "lru_conv_w", "lru_conv_b", "lru_w_a", "lru_b_a", "lru_w_x", "lru_b_x",
             "lru_lambda", "conf_conv_w", "conf_conv_b", "conf_ln_g", "conf_ln_b", "w_out", "xa_norm_g",
             "mem_norm_g", "w_q", "w_kv", "w_o", "ffn_norm_g", "w_up", "ffn_conv_w", "ffn_conv_b", "w_down",
             "final_norm_g"]
    loc = locals()
    W = {n: loc[n] for n in names}
    M = {n: loc["m_" + n] for n in names}
    V = {n: loc["v_" + n] for n in names}
    big = ["w_in", "w_out", "w_q", "w_kv", "w_o", "w_up", "w_down"]
    conv_sharded = ["lru_conv_w", "conf_conv_w", "ffn_conv_w"]

    xs = x[0]
    mems = mem[0]
    tgt = loss_target[0]
    me = 4 * lax.axis_index("x") + 2 * lax.axis_index("y") + lax.axis_index("c")

    conv_shapes = [W[n].shape[1:] for n in conv_sharded]
    conv_pack = _pack([W[n][0] for n in conv_sharded])
    shard = {n: W[n][0].astype(_XFER) for n in big}
    h1, (g_in, g_conv) = _rms_fwd(xs, mix_norm_g, "rms1_fwd", comm=_Gather([shard["w_in"], conv_pack]))
    convs = [[] for _ in conv_sharded]
    for j in range(_NDEV):
        for idx, part in enumerate(_unpack(g_conv[j], conv_shapes)):
            convs[idx].append(part)
    lcw, ccw, fcw = [jnp.concatenate(parts, axis=-1) for parts in convs]

    wab = jnp.concatenate([_block_diag(lru_w_a[0]), _block_diag(lru_w_x[0])], axis=1).astype(_MXU)
    mixer_params = (lcw, lru_conv_b, wab, lru_b_a, lru_b_x, lru_lambda, ccw, conf_conv_b, conf_ln_g, conf_ln_b)

    z, (g_out, g_q, g_kv, g_o) = _mm_nn_stacked(
        h1, g_in, _F32, "mm_in_fwd", comm=_Gather([shard[n] for n in ("w_out", "w_q", "w_kv", "w_o")]))
    (ycat, hs, cc), (g_up, g_down) = _mixer_fwd(
        z, *mixer_params, "mixer_fwd", comm=_Gather([shard["w_up"], shard["w_down"]]))
    w_out_f = g_out.reshape(-1, g_out.shape[-1])
    w_q_f = g_q.reshape(-1, g_q.shape[-1])
    w_o_f = g_o.reshape(-1, g_o.shape[-1])
    w_down_f = g_down.reshape(-1, g_down.shape[-1])
    x1 = _mm_nn_nat(ycat, w_out_f, xs, _F32, "mm_out_fwd")

    h2 = _rms_fwd(x1, xa_norm_g, "rms2_fwd")
    mn = _rms_fwd(mems, mem_norm_g, "rmsm_fwd")
    q = _mm_nn_nat(h2, w_q_f, None, _MXU, "mm_q_fwd")
    kv = _mm_nn_stacked(mn, g_kv, _MXU, "mm_kv_fwd")
    o = _attn_fwd(q, kv, "attn_fwd")
    x2 = _mm_nn_nat(o, w_o_f, x1, _F32, "mm_o_fwd")

    h3 = _rms_fwd(x2, ffn_norm_g, "rms3_fwd")
    gu = _mm_nn_stacked(h3, g_up, _F32, "mm_up_fwd")
    act = _ffn_fwd(gu, fcw, ffn_conv_b, "ffn_fwd")
    x3 = _mm_nn_nat(act, w_down_f, x2, _F32, "mm_down_fwd")

    gfin = final_norm_g.reshape(1, -1)
    dx3, dx3b, lvec, dg_final = _final_loss(x3, tgt, gfin, "final_loss")
    loss_local = 0.5 * jnp.sum(lvec) / x3.shape[1]
    loss = lax.psum(loss_local, _AXES)

    def rows8(p):
        return p.reshape(_NDEV, p.shape[0] // _NDEV, p.shape[1])

    dact = _mm_nt_nat(dx3b, w_down_f, _F32, "mm_down_dgrad")
    p_down = _mm_tn_nat(act, dx3b, _XFER, "mm_down_wgrad")
    (dgu, dfcw, dfcb), (r_down,) = _ffn_bwd(dact, gu, fcw, ffn_conv_b, "ffn_bwd", comm=_Exchange([rows8(p_down)]))
    dh3 = _mm_nt_stacked(dgu, g_up, _F32, "mm_up_dgrad")
    p_up = _mm_tn_stacked(h3, dgu, _NDEV, _XFER, "mm_up_wgrad")
    dx2, dx2b, dg_ffn = _rms_bwd(dh3, x2, ffn_norm_g, dx3, "rms3_bwd")

    do = _mm_nt_nat(dx2b, w_o_f, _MXU, "mm_o_dgrad")
    p_o = _mm_tn_nat(o, dx2b, _XFER, "mm_o_wgrad")
    dq, dk, dv = _attn_bwd(q, kv, do, "attn_bwd")
    dh2 = _mm_nt_nat(dq, w_q_f, _F32, "mm_q_dgrad")
    p_q = _mm_tn_nat(h2, dq, _XFER, "mm_q_wgrad")
    dx1, dx1b, dg_xa = _rms_bwd(dh2, x1, xa_norm_g, dx2, "rms2_bwd")
    dkv = jnp.concatenate([dk, dv], axis=1).astype(_MXU)
    dmn = _mm_nt_stacked(dkv, g_kv, _F32, "mm_kv_dgrad")
    p_kv = _mm_tn_stacked(mn, dkv, _NDEV, _XFER, "mm_kv_wgrad")
    _, _, dg_mem = _rms_bwd(dmn, mems, mem_norm_g, None, "rmsm_bwd")

    dycat = _mm_nt_nat(dx1b, w_out_f, _F32, "mm_out_dgrad")
    p_out = _mm_tn_nat(ycat, dx1b, _XFER, "mm_out_wgrad")
    ((dz, dlcw, dlcb, dwab, dba, dbx, dlam, dccw, dccb, dlng, dlnb), (r_up, r_o, r_q, r_kv, r_out)) = _mixer_bwd(
        dycat, z, hs, cc, *mixer_params, "mixer_bwd",
        comm=_Exchange([p_up, rows8(p_o), rows8(p_q), p_kv, rows8(p_out)]))
    p_in = _mm_tn_stacked(h1, dz, _NDEV, _XFER, "mm_in_wgrad")

    c = _D_LRU
    heads = lru_w_a.shape[1]
    small_partial = {
        "lru_conv_w": dlcw, "lru_conv_b": dlcb,
        "lru_w_a": _diag_blocks(dwab[:, :c], heads), "lru_b_a": dba,
        "lru_w_x": _diag_blocks(dwab[:, c:], heads), "lru_b_x": dbx, "lru_lambda": dlam,
        "conf_conv_w": dccw, "conf_conv_b": dccb, "conf_ln_g": dlng, "conf_ln_b": dlnb,
        "xa_norm_g": dg_xa, "mem_norm_g": dg_mem, "ffn_norm_g": dg_ffn,
        "ffn_conv_w": dfcw, "ffn_conv_b": dfcb, "final_norm_g": dg_final,
    }
    early = list(small_partial)
    early_shapes = [small_partial[n].shape for n in early]
    dh1, (r_in, early_all) = _mm_nt_stacked(
        dz, g_in, _F32, "mm_in_dgrad",
        comm=_Both(_Exchange([p_in]), _Gather([_pack([small_partial[n] for n in early])])))
    grad_x, _, dg_mix = _rms_bwd(dh1, xs, mix_norm_g, dx1, "rms1_bwd")
    (mix_all,) = _comm_call(_Gather([dg_mix]), "gather_mix_grad")
    small = early + ["mix_norm_g"]
    small_sum = _unpack(_sum8(early_all, "sum_small_grads"), early_shapes)
    small_sum.append(_sum8(mix_all.reshape(_NDEV, 8, -1), "sum_mix_grad").reshape(dg_mix.shape))
    received = {"w_in": r_in, "w_out": r_out, "w_q": r_q, "w_kv": r_kv, "w_o": r_o, "w_up": r_up,
                "w_down": r_down}

    grads, deltas, new_m, new_v = {}, {}, {}, {}
    for n, rec in ((n, received[n]) for n in big):
        shp = W[n].shape
        w2, m2, v2 = (t.reshape(rec.shape[1:]) for t in (W[n], M[n], V[n]))
        outs = _sum_adamw(rec, w2, m2, v2, "adamw_" + n)
        grads[n], deltas[n], new_m[n], new_v[n] = (t.reshape(shp) for t in outs)

    small_g = []
    for n, g in zip(small, small_sum):
        if n in conv_sharded:
            width = W[n].shape[-1]
            g = lax.dynamic_slice_in_dim(g, me * width, width, axis=1)
        small_g.append(g.reshape(W[n].shape))
    small_shapes = [W[n].shape for n in small]
    sd, sm, sv = _adamw_flat(_pack(small_g), _pack([W[n] for n in small]), _pack([M[n] for n in small]),
                             _pack([V[n] for n in small]), "adamw_small")
    for n, g, d_, m_, v_ in zip(small, small_g, _unpack(sd, small_shapes), _unpack(sm, small_shapes),
                                _unpack(sv, small_shapes)):
        grads[n], deltas[n], new_m[n], new_v[n] = g, d_, m_, v_

    return (loss, grad_x[None], *[grads[n] for n in names], *[deltas[n] for n in names],
            *[new_m[n] for n in names], *[new_v[n] for n in names])
```

```python
import functools

import jax
import jax.numpy as jnp
from jax import lax
from jax.experimental import pallas as pl
from jax.experimental.pallas import tpu as pltpu

_MXU = jnp.bfloat16
_XFER = jnp.bfloat16
_F32 = jnp.float32
_EPS = 1e-6
_NDEV = 8
_AXES = ("x", "y", "c")
_VMEM_LIMIT = 48 * 1024 * 1024

_D_LRU = 512
_XA_HEADS = 4
_RG_C = 8.0
_ADAM_LR, _ADAM_B1, _ADAM_B2, _ADAM_EPS, _ADAM_WD, _ADAM_STEP = 0.001, 0.9, 0.999, 1e-08, 0.01, 10

_MESH_ID = pl.DeviceIdType.MESH
_ANY = pl.BlockSpec(memory_space=pl.ANY)


def _cparams(*sem):
    return pltpu.CompilerParams(dimension_semantics=tuple(sem), vmem_limit_bytes=_VMEM_LIMIT)


def _pcall(body, *, args, grid, in_specs, out_specs, out_shape, sem, name, scratch_shapes=(), comm=None):
    outs_l = list(out_shape) if isinstance(out_shape, (list, tuple)) else [out_shape]
    ospecs_l = list(out_specs) if isinstance(out_specs, (list, tuple)) else [out_specs]
    n_in, n_out, n_scr = len(args), len(outs_l), len(scratch_shapes)
    if comm is None:
        res = pl.pallas_call(
            body, grid=grid, in_specs=list(in_specs), out_specs=ospecs_l, out_shape=outs_l,
            scratch_shapes=list(scratch_shapes), compiler_params=_cparams(*sem), name=name)(*args)
        return list(res), []
    n_ci, n_co = len(comm.ins), len(comm.outs)

    def wrapped(*refs):
        ins, cins = refs[:n_in], refs[n_in:n_in + n_ci]
        o = n_in + n_ci
        outs, couts = refs[o:o + n_out], refs[o + n_out:o + n_out + n_co]
        s = o + n_out + n_co
        scr, cscr = refs[s:s + n_scr], refs[s + n_scr:]
        first = pl.program_id(0) == 0
        last = pl.program_id(0) == grid[0] - 1
        for ax in range(1, len(grid)):
            first = jnp.logical_and(first, pl.program_id(ax) == 0)
            last = jnp.logical_and(last, pl.program_id(ax) == grid[ax] - 1)

        @pl.when(first)
        def _():
            comm.start(cins, couts, cscr)

        body(*ins, *outs, *scr)

        @pl.when(last)
        def _():
            comm.finish(cins, couts, cscr)

    res = pl.pallas_call(
        wrapped, grid=grid, in_specs=list(in_specs) + [_ANY] * n_ci, out_specs=ospecs_l + [_ANY] * n_co,
        out_shape=outs_l + list(comm.outs), scratch_shapes=list(scratch_shapes) + list(comm.scratch),
        compiler_params=_cparams(*(("arbitrary",) * len(grid))), name=name)(*args, *comm.ins)
    return list(res[:n_out]), list(res[n_out:])


def _sigmoid(v):
    return 1.0 / (1.0 + jnp.exp(-v))


_GELU_C = 0.7978845608028654
_GELU_K = 0.044715


def _gelu(v):
    t = jnp.tanh(_GELU_C * (v + _GELU_K * v * v * v))
    return 0.5 * v * (1.0 + t)


def _gelu_and_grad(v):
    v2 = v * v
    t = jnp.tanh(_GELU_C * (v + _GELU_K * v2 * v))
    g = 0.5 * v * (1.0 + t)
    dg = 0.5 * (1.0 + t) + 0.5 * v * (1.0 - t * t) * (_GELU_C * (1.0 + 3.0 * _GELU_K * v2))
    return g, dg


def _softplus(v):
    e = jnp.exp(-jnp.abs(v))
    log1p = jnp.where(e < 1e-2, e * (1.0 - e * (0.5 - e * (1.0 / 3.0))), jnp.log(1.0 + e))
    return jnp.maximum(v, 0.0) + log1p


def _neg_expm1(v):
    series = -v * (1.0 + v * (0.5 + v * ((1.0 / 6.0) + v * (1.0 / 24.0))))
    return jnp.where(v > -0.0625, series, 1.0 - jnp.exp(v))


def _dot(a, b, dims):
    return lax.dot_general(a.astype(_MXU), b.astype(_MXU), (dims, ((), ())), preferred_element_type=_F32)


_NN = ((1,), (0,))
_NT = ((1,), (1,))
_TN = ((0,), (0,))


def _scan_fwd(a, b, rows):
    n = a.shape[0]
    d = 1
    while d < n:
        keep = rows >= d
        b = jnp.where(keep, b + a * pltpu.roll(b, d, 0), b)
        a = jnp.where(keep, a * pltpu.roll(a, d, 0), a)
        d *= 2
    return a, b


def _scan_rev(a, b, rows):
    n = a.shape[0]
    d = 1
    while d < n:
        keep = rows < n - d
        b = jnp.where(keep, b + a * pltpu.roll(b, n - d, 0), b)
        a = jnp.where(keep, a * pltpu.roll(a, n - d, 0), a)
        d *= 2
    return a, b


def _colsum(v):
    return jnp.sum(v, axis=0, keepdims=True)


def _mm(a, b, *, dims, grid, a_spec, b_spec, outs, acc_shape, name, extra=(), epi=None, slabs=1, comm=None):
    nred = grid[-1]
    red_axis = len(grid) - 1
    n_ex, n_out = len(extra), len(outs)
    epi = _epi_store if epi is None else epi

    def body(*refs):
        a_ref, b_ref = refs[:2]
        ex, o_refs, acc_ref = refs[2:2 + n_ex], refs[2 + n_ex:2 + n_ex + n_out], refs[-1]
        if slabs == 1:
            p = _dot(a_ref[...], b_ref[...], dims)
        else:
            n = b_ref.shape[-1]
            p = _dot(a_ref[:, 0:n], b_ref[0], dims)
            for jj in range(1, slabs):
                p = p + _dot(a_ref[:, jj * n:(jj + 1) * n], b_ref[jj], dims)

        first_rows = pl.program_id(0) == 0
        if nred == 1:
            epi(p, ex, o_refs, first_rows)
        else:
            k = pl.program_id(red_axis)

            @pl.when(k == 0)
            def _():
                acc_ref[...] = p

            @pl.when(jnp.logical_and(k > 0, k < nred - 1))
            def _():
                acc_ref[...] += p

            @pl.when(k == nred - 1)
            def _():
                epi(acc_ref[...] + p, ex, o_refs, first_rows)

    sem = ("parallel",) * (len(grid) - 1) + ("arbitrary",)
    if any(o[0].shape[0] == 1 for o in outs):
        sem = ("arbitrary",) * len(grid)
    res, cres = _pcall(
        body, args=(a, b) + tuple(e[0] for e in extra), grid=grid,
        in_specs=[a_spec, b_spec] + [e[1] for e in extra],
        out_specs=[o[1] for o in outs], out_shape=[o[0] for o in outs],
        scratch_shapes=[pltpu.VMEM(acc_shape if nred > 1 else (8, 128), _F32)], sem=sem, name=name, comm=comm)
    res = res[0] if n_out == 1 else res
    return res if comm is None else (res, cres)


def _epi_store(total, ex, outs, first_rows):
    outs[0][...] = total.astype(outs[0].dtype)


def _epi_residual_rms(total, ex, outs, first_rows):
    res_ref, g_ref = ex
    xn = total + res_ref[...]
    outs[0][...] = xn
    r = lax.rsqrt(jnp.mean(xn * xn, axis=-1, keepdims=True) + _EPS)
    outs[1][...] = (xn * r * g_ref[...]).astype(outs[1].dtype)


def _epi_rms_bwd(total, ex, outs, first_rows):
    x_ref, g_ref, dres_ref = ex
    dg_ref = outs[-1]
    xv = x_ref[...]
    r = lax.rsqrt(jnp.mean(xv * xv, axis=-1, keepdims=True) + _EPS)
    xhat = xv * r
    dxh = total * g_ref[...]
    dx = dres_ref[...] + r * (dxh - xhat * jnp.mean(dxh * xhat, axis=-1, keepdims=True))
    for o_ref in outs[:-1]:
        o_ref[...] = dx.astype(o_ref.dtype)

    @pl.when(first_rows)
    def _():
        dg_ref[...] = jnp.zeros_like(dg_ref)

    dg_ref[...] += _colsum(total * xhat)


def _epi_final(total, ex, outs, first_rows):
    res_ref, t_ref, g_ref = ex
    dx_ref, dxb_ref, l_ref, dg_ref = outs
    xv = total + res_ref[...]
    gv = g_ref[...]
    d = xv.shape[-1]
    r = lax.rsqrt(jnp.mean(xv * xv, axis=-1, keepdims=True) + _EPS)
    xhat = xv * r
    err = xhat * gv - t_ref[...]
    dy = err * (1.0 / d)
    dxh = dy * gv
    dx = r * (dxh - xhat * jnp.mean(dxh * xhat, axis=-1, keepdims=True))
    dx_ref[...] = dx
    dxb_ref[...] = dx.astype(dxb_ref.dtype)

    @pl.when(first_rows)
    def _():
        l_ref[...] = jnp.zeros_like(l_ref)
        dg_ref[...] = jnp.zeros_like(dg_ref)

    l_ref[...] += _colsum(err * err)
    dg_ref[...] += _colsum(dy * xhat)


def _tile(m, cap):
    t = min(m, cap)
    assert m % t == 0
    return t


def _row_spec(tm, n):
    return pl.BlockSpec((tm, n), lambda i, *_: (i, 0))


def _vec_spec(n):
    return pl.BlockSpec((1, n), lambda *_: (0, 0))


def _row_io(m, n, tm, extra, outs):
    def spec(kind):
        return _row_spec(tm, n) if kind == "row" else _vec_spec(n)

    ex = [(arr, spec(kind)) for arr, kind in extra]
    os_ = [(jax.ShapeDtypeStruct((m, n) if kind == "row" else (1, n), dt), spec(kind)) for dt, kind in outs]
    return ex, os_


def _mm_nn_stacked(a, w, out_dtype, name, comm=None, tm=1024):
    m, k = a.shape
    j, _, n = w.shape
    tm = _tile(m, tm)
    return _mm(a, w, dims=_NN, grid=(m // tm, j, 1),
               a_spec=pl.BlockSpec((tm, k), lambda i, jj, r: (i, 0)),
               b_spec=pl.BlockSpec((None, k, n), lambda i, jj, r: (jj, 0, 0)),
               outs=[(jax.ShapeDtypeStruct((m, j * n), out_dtype), pl.BlockSpec((tm, n), lambda i, jj, r: (i, jj)))],
               acc_shape=(tm, n), name=name, comm=comm)


def _mm_nn_nat(a, w, name, *, outs, extra=(), epi=None, tm=1024):
    m, kt = a.shape
    _, n = w.shape
    tm = _tile(m, tm)
    tk = _tile(kt, 1024)
    ex, os_ = _row_io(m, n, tm, extra, outs)
    return _mm(a, w, dims=_NN, grid=(m // tm, kt // tk),
               a_spec=pl.BlockSpec((tm, tk), lambda i, r: (i, r)),
               b_spec=pl.BlockSpec((tk, n), lambda i, r: (r, 0)),
               outs=os_, extra=ex, epi=epi, acc_shape=(tm, n), name=name)


def _mm_nt_stacked(dc, w, name, *, outs, extra=(), epi=None, comm=None, tm=1024, slabs=1):
    m = dc.shape[0]
    j, k, n = w.shape
    tm = _tile(m, tm)
    assert j % slabs == 0
    ex, os_ = _row_io(m, k, tm, extra, outs)
    wblk = (None, k, n) if slabs == 1 else (slabs, k, n)
    return _mm(dc, w, dims=_NT, grid=(m // tm, j // slabs),
               a_spec=pl.BlockSpec((tm, slabs * n), lambda i, r: (i, r)),
               b_spec=pl.BlockSpec(wblk, lambda i, r: (r, 0, 0)),
               outs=os_, extra=ex, epi=epi, acc_shape=(tm, k), name=name, slabs=slabs, comm=comm)


def _mm_nt_nat(dc, w, name, *, outs, extra=(), epi=None, tm=1024):
    m, n = dc.shape
    kt = w.shape[0]
    tm = _tile(m, tm)
    tkb = _tile(kt, 1024)
    if kt == tkb:
        ex, os_ = _row_io(m, kt, tm, extra, outs)
    else:
        assert not extra and len(outs) == 1
        ex, os_ = [], [(jax.ShapeDtypeStruct((m, kt), outs[0][0]), pl.BlockSpec((tm, tkb), lambda i, kb, r: (i, kb)))]
    return _mm(dc, w, dims=_NT, grid=(m // tm, kt // tkb, 1),
               a_spec=pl.BlockSpec((tm, n), lambda i, kb, r: (i, 0)),
               b_spec=pl.BlockSpec((tkb, n), lambda i, kb, r: (kb, 0)),
               outs=os_, extra=ex, epi=epi, acc_shape=(tm, tkb), name=name)


def _mm_tn_stacked(a, dc, j, out_dtype, name, slabs=1, ts=1024):
    s, k = a.shape
    n = dc.shape[1] // j
    ts = _tile(s, ts)
    assert j % slabs == 0

    def epi(total, ex, outs, first_rows):
        for jj in range(slabs):
            outs[0][jj] = total[:, jj * n:(jj + 1) * n].astype(outs[0].dtype)

    return _mm(a, dc, dims=_TN, grid=(j // slabs, s // ts),
               a_spec=pl.BlockSpec((ts, k), lambda jj, r: (r, 0)),
               b_spec=pl.BlockSpec((ts, slabs * n), lambda jj, r: (r, jj)),
               outs=[(jax.ShapeDtypeStruct((j, k, n), out_dtype),
                      pl.BlockSpec((slabs, k, n), lambda jj, r: (jj, 0, 0)))],
               epi=epi, acc_shape=(k, slabs * n), name=name)


def _mm_tn_nat(a, dc, out_dtype, name, ts=1024):
    s, kt = a.shape
    n = dc.shape[1]
    ts = _tile(s, ts)
    tkb = _tile(kt, 512)
    return _mm(a, dc, dims=_TN, grid=(kt // tkb, s // ts),
               a_spec=pl.BlockSpec((ts, tkb), lambda kb, r: (r, kb)),
               b_spec=pl.BlockSpec((ts, n), lambda kb, r: (r, 0)),
               outs=[(jax.ShapeDtypeStruct((kt, n), out_dtype), pl.BlockSpec((tkb, n), lambda kb, r: (kb, 0)))],
               acc_shape=(tkb, n), name=name)


def _rms_fwd(x, g, name, comm=None):
    s, d = x.shape
    t = _tile(s, 256)

    def body(x_ref, g_ref, h_ref):
        xv = x_ref[...]
        r = lax.rsqrt(jnp.mean(xv * xv, axis=-1, keepdims=True) + _EPS)
        h_ref[...] = (xv * r * g_ref[...]).astype(h_ref.dtype)

    res, cres = _pcall(
        body, args=(x, g), grid=(s // t,),
        in_specs=[pl.BlockSpec((t, d), lambda i: (i, 0)), pl.BlockSpec((1, d), lambda i: (0, 0))],
        out_specs=pl.BlockSpec((t, d), lambda i: (i, 0)),
        out_shape=jax.ShapeDtypeStruct((s, d), _MXU), sem=("parallel",), name=name, comm=comm)
    return res[0] if comm is None else (res[0], cres)


def _rms_bwd(dh, x, g, dres, name):
    s, d = x.shape
    t = _tile(s, 256)
    has_res = dres is not None

    def body(*refs):
        if has_res:
            dh_ref, x_ref, g_ref, dres_ref, dx_ref, dxb_ref, dg_ref = refs
        else:
            dh_ref, x_ref, g_ref, dx_ref, dxb_ref, dg_ref = refs
        xv = x_ref[...]
        dhv = dh_ref[...]
        r = lax.rsqrt(jnp.mean(xv * xv, axis=-1, keepdims=True) + _EPS)
        xhat = xv * r
        dxh = dhv * g_ref[...]
        dx = r * (dxh - xhat * jnp.mean(dxh * xhat, axis=-1, keepdims=True))
        if has_res:
            dx = dx + dres_ref[...]
        dx_ref[...] = dx
        dxb_ref[...] = dx.astype(dxb_ref.dtype)

        @pl.when(pl.program_id(0) == 0)
        def _():
            dg_ref[...] = jnp.zeros_like(dg_ref)

        dg_ref[...] += _colsum(dhv * xhat)

    row = pl.BlockSpec((t, d), lambda i: (i, 0))
    vec = pl.BlockSpec((1, d), lambda i: (0, 0))
    in_specs = [row, row, vec] + ([row] if has_res else [])
    args = (dh, x, g) + ((dres,) if has_res else ())
    return pl.pallas_call(
        body, grid=(s // t,), in_specs=in_specs, out_specs=[row, row, vec],
        out_shape=[jax.ShapeDtypeStruct((s, d), _F32), jax.ShapeDtypeStruct((s, d), _MXU),
                   jax.ShapeDtypeStruct((1, d), _F32)],
        compiler_params=_cparams("arbitrary"), name=name,
    )(*args)


_LRU_K = 4
_CONF_K = 31
_LRU_HALO = 8
_CONF_HALO = 32
_MIX_T = 256


def _lru_gates(lx, wab_ref, ba_ref, bx_ref, lam_ref):
    c = _D_LRU
    pre = _dot(lx, wab_ref[...], _NN)
    r = _sigmoid(pre[:, :c] + ba_ref[...])
    ig = _sigmoid(pre[:, c:] + bx_ref[...])
    sp = _softplus(-lam_ref[...])
    log_a = (-_RG_C) * r * sp
    a = jnp.exp(log_a)
    mult = jnp.sqrt(_neg_expm1(2.0 * log_a))
    return r, ig, sp, a, mult


def _causal_conv(ext_ref, halo, w_ref, b_ref, taps, t):
    acc = b_ref[...] + w_ref[0:1, :] * ext_ref[pl.ds(halo - (taps - 1), t), :]
    for k in range(1, taps):
        acc = acc + w_ref[k:k + 1, :] * ext_ref[pl.ds(halo - (taps - 1) + k, t), :]
    return acc


def _mixer_fwd(z, lcw, lcb, wab, ba, bx, lam, ccw, ccb, lng, lnb, name, comm=None):
    s = z.shape[0]
    c = _D_LRU
    t = _tile(s, _MIX_T)
    nt = s // t

    def body(lx0_ref, lx0h_ref, gate_ref, ca_ref, cah_ref, cb_ref, cbh_ref,
             lcw_ref, lcb_ref, wab_ref, ba_ref, bx_ref, lam_ref, ccw_ref, ccb_ref, lng_ref, lnb_ref,
             ycat_ref, hs_ref, cc_ref, ext_ref, cge_ref, hc_ref):
        i = pl.program_id(0)
        first = i == 0
        rows = lax.broadcasted_iota(jnp.int32, (t, c), 0)

        ext_ref[0:_LRU_HALO, :] = jnp.where(first, 0.0, lx0h_ref[...])
        ext_ref[_LRU_HALO:_LRU_HALO + t, :] = lx0_ref[...]
        lx = _causal_conv(ext_ref, _LRU_HALO, lcw_ref, lcb_ref, _LRU_K, t)
        r, ig, sp, a, mult = _lru_gates(lx, wab_ref, ba_ref, bx_ref, lam_ref)
        u = mult * (ig * lx)
        a_cum, h_loc = _scan_fwd(a, u, rows)

        @pl.when(first)
        def _():
            hc_ref[...] = jnp.zeros_like(hc_ref)

        h = h_loc + a_cum * hc_ref[7:8, :]
        hs_ref[...] = h
        hc_ref[...] = hs_ref[pl.ds(t - 8, 8), :]
        ycat_ref[:, 0:c] = (h * _gelu(gate_ref[...])).astype(ycat_ref.dtype)

        cge_ref[0:_CONF_HALO, :] = jnp.where(first, 0.0, cah_ref[...] * _sigmoid(cbh_ref[...]))
        cge_ref[_CONF_HALO:_CONF_HALO + t, :] = ca_ref[...] * _sigmoid(cb_ref[...])
        cc = _causal_conv(cge_ref, _CONF_HALO, ccw_ref, ccb_ref, _CONF_K, t)
        cc_ref[...] = cc
        xc = cc - jnp.mean(cc, axis=-1, keepdims=True)
        rstd = lax.rsqrt(jnp.mean(xc * xc, axis=-1, keepdims=True) + _EPS)
        ln = xc * rstd * lng_ref[...] + lnb_ref[...]
        ycat_ref[:, c:2 * c] = (ln * _sigmoid(ln)).astype(ycat_ref.dtype)

    def col(j):
        return pl.BlockSpec((t, c), lambda i: (i, j))

    def halo(j, rows_):
        per = t // rows_
        return pl.BlockSpec((rows_, c), lambda i: (jnp.maximum(i * per - 1, 0), j))

    def full(arr):
        return pl.BlockSpec(arr.shape, lambda i: (0,) * arr.ndim)

    params = (lcw, lcb, wab, ba, bx, lam, ccw, ccb, lng, lnb)
    res, cres = _pcall(
        body, args=(z, z, z, z, z, z, z, *params), grid=(nt,),
        in_specs=[col(0), halo(0, _LRU_HALO), col(1), col(2), halo(2, _CONF_HALO), col(3), halo(3, _CONF_HALO)]
        + [full(p) for p in params],
        out_specs=[pl.BlockSpec((t, 2 * c), lambda i: (i, 0)), pl.BlockSpec((t, c), lambda i: (i, 0)),
                   pl.BlockSpec((t, c), lambda i: (i, 0))],
        out_shape=[jax.ShapeDtypeStruct((s, 2 * c), _MXU), jax.ShapeDtypeStruct((s, c), _F32),
                   jax.ShapeDtypeStruct((s, c), _F32)],
        scratch_shapes=[pltpu.VMEM((t + _LRU_HALO, c), _F32), pltpu.VMEM((t + _CONF_HALO, c), _F32),
                        pltpu.VMEM((8, c), _F32)],
        sem=("arbitrary",), name=name, comm=comm)
    return res, cres


def _mixer_bwd(dycat, z, hs, cc, lcw, lcb, wab, ba, bx, lam, ccw, ccb, lng, lnb, name, comm=None):
    s = z.shape[0]
    c = _D_LRU
    t = _tile(s, _MIX_T)
    nt = s // t

    def body(dyl_ref, dc_ref, lx0_ref, lx0h_ref, gate_ref, ca_ref, cah_ref, cb_ref, cbh_ref,
             hs_ref, hsh_ref, cc_ref,
             lcw_ref, lcb_ref, wab_ref, ba_ref, bx_ref, lam_ref, ccw_ref, ccb_ref, lng_ref, lnb_ref,
             dz_ref, dlcw_ref, dlcb_ref, dwab_ref, dba_ref, dbx_ref, dlam_ref, dccw_ref, dccb_ref, dlng_ref,
             dlnb_ref,
             ext_ref, up_ref, cge_ref, dce_ref, xc_ref, dlxc_ref, dccc_ref):
        i = pl.program_id(0)
        first_tile = i == nt - 1
        last_tile = i == 0
        rows = lax.broadcasted_iota(jnp.int32, (t, c), 0)

        @pl.when(last_tile)
        def _():
            for ref in (dlcw_ref, dlcb_ref, dwab_ref, dba_ref, dbx_ref, dlam_ref, dccw_ref, dccb_ref, dlng_ref,
                        dlnb_ref, xc_ref, dlxc_ref, dccc_ref):
                ref[...] = jnp.zeros_like(ref)

        ext_ref[0:_LRU_HALO, :] = jnp.where(first_tile, 0.0, lx0h_ref[...])
        ext_ref[_LRU_HALO:_LRU_HALO + t, :] = lx0_ref[...]
        lx = _causal_conv(ext_ref, _LRU_HALO, lcw_ref, lcb_ref, _LRU_K, t)
        r, ig, sp, a, mult = _lru_gates(lx, wab_ref, ba_ref, bx_ref, lam_ref)
        h = hs_ref[...]
        gl, dgl = _gelu_and_grad(gate_ref[...])
        dyl = dyl_ref[...]
        dz_ref[:, c:2 * c] = (dyl * h * dgl).astype(dz_ref.dtype)
        dh = dyl * gl

        up_ref[0:t, :] = a
        up_ref[t:t + 8, :] = jnp.ones((8, c), _F32)
        a_up = up_ref[pl.ds(1, t), :]
        a_cum, g_loc = _scan_rev(a_up, dh, rows)
        gt = g_loc + a_cum * xc_ref[0:1, :]
        xc_ref[...] = (a * gt)[0:8, :]

        up_ref[0:8, :] = jnp.where(first_tile, 0.0, hsh_ref[...])
        up_ref[8:8 + t, :] = h
        hprev = up_ref[pl.ds(7, t), :]

        da = gt * hprev
        dmult = gt * ig * lx
        dig = gt * mult * lx
        dlx = gt * mult * ig
        dlog_a = da * a - dmult * a * a / mult
        dpre_r = dlog_a * (-_RG_C) * sp * r * (1.0 - r)
        dpre_i = dig * ig * (1.0 - ig)
        dlam_ref[...] += _colsum(dlog_a * r) * (_RG_C * _sigmoid(-lam_ref[...]))
        dba_ref[...] += _colsum(dpre_r)
        dbx_ref[...] += _colsum(dpre_i)
        dpre = jnp.concatenate([dpre_r, dpre_i], axis=1).astype(_MXU)
        dlx = dlx + _dot(dpre, wab_ref[...], _NT)
        dwab_ref[...] += _dot(lx, dpre, _TN)

        dlcb_ref[...] += _colsum(dlx)
        up_ref[0:t, :] = dlx
        up_ref[t:t + 8, :] = dlxc_ref[...]
        dlxc_ref[...] = dlx[0:8, :]
        acc = lcw_ref[0:1, :] * up_ref[pl.ds(_LRU_K - 1, t), :]
        for k in range(1, _LRU_K):
            acc = acc + lcw_ref[k:k + 1, :] * up_ref[pl.ds(_LRU_K - 1 - k, t), :]
        dz_ref[:, 0:c] = acc.astype(dz_ref.dtype)
        for k in range(_LRU_K):
            dlcw_ref[k:k + 1, :] += _colsum(dlx * ext_ref[pl.ds(_LRU_HALO - (_LRU_K - 1) + k, t), :])

        sig_b = _sigmoid(cb_ref[...])
        ca = ca_ref[...]
        cge_ref[0:_CONF_HALO, :] = jnp.where(first_tile, 0.0, cah_ref[...] * _sigmoid(cbh_ref[...]))
        cge_ref[_CONF_HALO:_CONF_HALO + t, :] = ca * sig_b
        ccv = cc_ref[...]
        xcen = ccv - jnp.mean(ccv, axis=-1, keepdims=True)
        rstd = lax.rsqrt(jnp.mean(xcen * xcen, axis=-1, keepdims=True) + _EPS)
        xn = xcen * rstd
        ln = xn * lng_ref[...] + lnb_ref[...]
        sg = _sigmoid(ln)
        dln = dc_ref[...] * (sg * (1.0 + ln * (1.0 - sg)))
        dlng_ref[...] += _colsum(dln * xn)
        dlnb_ref[...] += _colsum(dln)
        dxn = dln * lng_ref[...]
        dcc = rstd * (dxn - jnp.mean(dxn, axis=-1, keepdims=True)
                      - xn * jnp.mean(dxn * xn, axis=-1, keepdims=True))
        dccb_ref[...] += _colsum(dcc)
        for k in range(_CONF_K):
            dccw_ref[k:k + 1, :] += _colsum(dcc * cge_ref[pl.ds(_CONF_HALO - (_CONF_K - 1) + k, t), :])
        dce_ref[0:t, :] = dcc
        dce_ref[t:t + _CONF_HALO, :] = dccc_ref[...]
        dccc_ref[...] = dcc[0:_CONF_HALO, :]
        dcg = ccw_ref[0:1, :] * dce_ref[pl.ds(_CONF_K - 1, t), :]
        for k in range(1, _CONF_K):
            dcg = dcg + ccw_ref[k:k + 1, :] * dce_ref[pl.ds(_CONF_K - 1 - k, t), :]
        dz_ref[:, 2 * c:3 * c] = (dcg * sig_b).astype(dz_ref.dtype)
        dz_ref[:, 3 * c:4 * c] = (dcg * ca * sig_b * (1.0 - sig_b)).astype(dz_ref.dtype)

    def col(j):
        return pl.BlockSpec((t, c), lambda i: (nt - 1 - i, j))

    def halo(j, rows_):
        per = t // rows_
        return pl.BlockSpec((rows_, c), lambda i: (jnp.maximum((nt - 1 - i) * per - 1, 0), j))

    def full(shape):
        return pl.BlockSpec(shape, lambda i: (0,) * len(shape))

    params = (lcw, lcb, wab, ba, bx, lam, ccw, ccb, lng, lnb)
    small = [(_LRU_K, c), (1, c), (c, 2 * c), (1, c), (1, c), (1, c), (_CONF_K, c), (1, c), (1, c), (1, c)]
    return _pcall(
        body, args=(dycat, dycat, z, z, z, z, z, z, z, hs, hs, cc, *params), grid=(nt,),
        in_specs=[col(0), col(1),
                  col(0), halo(0, _LRU_HALO), col(1), col(2), halo(2, _CONF_HALO), col(3), halo(3, _CONF_HALO),
                  col(0), halo(0, 8), col(0)]
        + [full(p.shape) for p in params],
        out_specs=[pl.BlockSpec((t, 4 * c), lambda i: (nt - 1 - i, 0))] + [full(sh) for sh in small],
        out_shape=[jax.ShapeDtypeStruct((s, 4 * c), _MXU)] + [jax.ShapeDtypeStruct(sh, _F32) for sh in small],
        scratch_shapes=[pltpu.VMEM((t + _LRU_HALO, c), _F32), pltpu.VMEM((t + 8, c), _F32),
                        pltpu.VMEM((t + _CONF_HALO, c), _F32), pltpu.VMEM((t + _CONF_HALO, c), _F32),
                        pltpu.VMEM((8, c), _F32), pltpu.VMEM((8, c), _F32), pltpu.VMEM((_CONF_HALO, c), _F32)],
        sem=("arbitrary",), name=name, comm=comm)


_ATT_T = 512


def _attn_probs(qh, kh, scale):
    sc = _dot(qh, kh, _NT) * scale
    e = jnp.exp(sc - jnp.max(sc, axis=-1, keepdims=True))
    return e / jnp.sum(e, axis=-1, keepdims=True)


def _attn_fwd(q, kv, name):
    s, d = q.shape
    nm = kv.shape[0]
    hd = d // _XA_HEADS
    t = _tile(s, _ATT_T)
    scale = hd ** -0.5

    def body(q_ref, k_ref, v_ref, o_ref):
        for hh in range(_XA_HEADS):
            sl = slice(hh * hd, (hh + 1) * hd)
            p = _attn_probs(q_ref[:, sl], k_ref[:, sl], scale)
            o_ref[:, sl] = _dot(p, v_ref[:, sl], _NN).astype(o_ref.dtype)

    return pl.pallas_call(
        body, grid=(s // t,),
        in_specs=[pl.BlockSpec((t, d), lambda i: (i, 0)), pl.BlockSpec((nm, d), lambda i: (0, 0)),
                  pl.BlockSpec((nm, d), lambda i: (0, 1))],
        out_specs=pl.BlockSpec((t, d), lambda i: (i, 0)),
        out_shape=jax.ShapeDtypeStruct((s, d), _MXU),
        compiler_params=_cparams("parallel"), name=name,
    )(q, kv, kv)


def _attn_bwd(q, kv, do, name):
    s, d = q.shape
    nm = kv.shape[0]
    hd = d // _XA_HEADS
    t = _tile(s, _ATT_T)
    scale = hd ** -0.5

    def body(q_ref, k_ref, v_ref, do_ref, dq_ref, dk_ref, dv_ref):
        @pl.when(pl.program_id(0) == 0)
        def _():
            dk_ref[...] = jnp.zeros_like(dk_ref)
            dv_ref[...] = jnp.zeros_like(dv_ref)

        for hh in range(_XA_HEADS):
            sl = slice(hh * hd, (hh + 1) * hd)
            qh = q_ref[:, sl]
            kh = k_ref[:, sl]
            doh = do_ref[:, sl]
            p = _attn_probs(qh, kh, scale)
            dp = _dot(doh, v_ref[:, sl], _NT)
            dv_ref[:, sl] += _dot(p, doh, _TN)
            ds = (p * (dp - jnp.sum(dp * p, axis=-1, keepdims=True)) * scale).astype(_MXU)
            dq_ref[:, sl] = _dot(ds, kh, _NN).astype(dq_ref.dtype)
            dk_ref[:, sl] += _dot(ds, qh, _TN)

    row = pl.BlockSpec((t, d), lambda i: (i, 0))
    return pl.pallas_call(
        body, grid=(s // t,),
        in_specs=[row, pl.BlockSpec((nm, d), lambda i: (0, 0)), pl.BlockSpec((nm, d), lambda i: (0, 1)), row],
        out_specs=[row, pl.BlockSpec((nm, d), lambda i: (0, 0)), pl.BlockSpec((nm, d), lambda i: (0, 0))],
        out_shape=[jax.ShapeDtypeStruct((s, d), _MXU), jax.ShapeDtypeStruct((nm, d), _F32),
                   jax.ShapeDtypeStruct((nm, d), _F32)],
        compiler_params=_cparams("arbitrary"), name=name,
    )(q, kv, kv, do)


_FFN_K = 3
_FFN_T = 256
_FFN_CB = 256
_FFN_R = 16


def _rows_before(ext, shift, n):
    h = ext.shape[0] - n
    return ext[h:h + n] if shift == 0 else pltpu.roll(ext, shift, 0)[h:h + n]


def _rows_after(ext, shift, n):
    return ext[0:n] if shift == 0 else pltpu.roll(ext, ext.shape[0] - shift, 0)[0:n]


def _ffn_fwd(gu, fcw, fcb, name):
    s = gu.shape[0]
    f = gu.shape[1] // 2
    t = _tile(s, 512)
    cb = _tile(f, 384)
    ncb = f // cb
    rr = _FFN_R

    def body(g0_ref, g0h_ref, u_ref, w_ref, b_ref, act_ref):
        first = pl.program_id(0) == 0
        w0, w1, w2, b = w_ref[0:1, :], w_ref[1:2, :], w_ref[2:3, :], b_ref[...]
        halo = jnp.where(first, 0.0, g0h_ref[...].astype(_F32))[rr - 8:rr]

        def chunk(r, prev):
            base = pl.multiple_of(r * rr, rr)
            cur = g0_ref[pl.ds(base, rr), :].astype(_F32)
            ext = jnp.concatenate([prev, cur], axis=0)
            g = b + w2 * cur + w1 * _rows_before(ext, 1, rr) + w0 * _rows_before(ext, 2, rr)
            u = u_ref[pl.ds(base, rr), :].astype(_F32)
            act_ref[pl.ds(base, rr), :] = (_gelu(g) * u).astype(act_ref.dtype)
            return cur[rr - 8:rr]

        lax.fori_loop(0, t // rr, chunk, halo, unroll=2)

    per = t // rr
    return pl.pallas_call(
        body, grid=(s // t, ncb),
        in_specs=[pl.BlockSpec((t, cb), lambda i, j: (i, j)),
                  pl.BlockSpec((rr, cb), lambda i, j: (jnp.maximum(i * per - 1, 0), j)),
                  pl.BlockSpec((t, cb), lambda i, j: (i, j + ncb)),
                  pl.BlockSpec((_FFN_K, cb), lambda i, j: (0, j)),
                  pl.BlockSpec((1, cb), lambda i, j: (0, j))],
        out_specs=pl.BlockSpec((t, cb), lambda i, j: (i, j)),
        out_shape=jax.ShapeDtypeStruct((s, f), _MXU),
        compiler_params=_cparams("parallel", "parallel"), name=name,
    )(gu, gu, gu, fcw, fcb)


def _ffn_bwd(dact, gu, fcw, fcb, name, comm=None):
    s = gu.shape[0]
    f = gu.shape[1] // 2
    t = _tile(s, _FFN_T)
    nt = s // t
    cb = _tile(f, _FFN_CB)
    rr = _FFN_R
    nchunk = t // rr

    def body(dact_ref, g0_ref, g0h_ref, u_ref, w_ref, b_ref, dgu_ref, dw_ref, db_ref, car_ref):
        i = pl.program_id(0)
        first_tile = i == nt - 1

        @pl.when(i == 0)
        def _():
            dw_ref[...] = jnp.zeros_like(dw_ref)
            db_ref[...] = jnp.zeros_like(db_ref)
            car_ref[...] = jnp.zeros_like(car_ref)

        for j in range(f // cb):
            cs = slice(j * cb, (j + 1) * cb)
            us = slice(f + j * cb, f + (j + 1) * cb)
            w0, w1, w2, b = w_ref[0:1, cs], w_ref[1:2, cs], w_ref[2:3, cs], b_ref[:, cs]
            halo = jnp.where(first_tile, 0.0, g0h_ref[:, cs].astype(_F32))[rr - 8:rr]

            def chunk(q, carry, cs=cs, us=us, w0=w0, w1=w1, w2=w2, b=b, halo=halo):
                dg_next, a0, a1, a2, ab = carry
                r = nchunk - 1 - q
                base = pl.multiple_of(r * rr, rr)
                before = pl.multiple_of(jnp.maximum(base - rr, 0), rr)
                cur = g0_ref[pl.ds(base, rr), cs].astype(_F32)
                prev = jnp.where(r == 0, halo, g0_ref[pl.ds(before, rr), cs].astype(_F32)[rr - 8:rr])
                ext = jnp.concatenate([prev, cur], axis=0)
                x1 = _rows_before(ext, 1, rr)
                x2 = _rows_before(ext, 2, rr)
                gl, dgl = _gelu_and_grad(b + w2 * cur + w1 * x1 + w0 * x2)
                da = dact_ref[pl.ds(base, rr), cs].astype(_F32)
                dgu_ref[pl.ds(base, rr), us] = (da * gl).astype(dgu_ref.dtype)
                dg = da * u_ref[pl.ds(base, rr), cs].astype(_F32) * dgl
                after = jnp.concatenate([dg, dg_next], axis=0)
                dg0 = w2 * dg + w1 * _rows_after(after, 1, rr) + w0 * _rows_after(after, 2, rr)
                dgu_ref[pl.ds(base, rr), cs] = dg0.astype(dgu_ref.dtype)
                return dg[0:8], a0 + dg * x2, a1 + dg * x1, a2 + dg * cur, ab + dg

            zero = jnp.zeros((rr, cb), _F32)
            dg_first, a0, a1, a2, ab = lax.fori_loop(0, nchunk, chunk, (car_ref[:, cs], zero, zero, zero, zero))
            car_ref[:, cs] = dg_first
            dw_ref[0:1, cs] += _colsum(a0)
            dw_ref[1:2, cs] += _colsum(a1)
            dw_ref[2:3, cs] += _colsum(a2)
            db_ref[:, cs] += _colsum(ab)

    per = t // rr
    return _pcall(
        body, args=(dact, gu, gu, gu, fcw, fcb), grid=(nt,),
        in_specs=[pl.BlockSpec((t, f), lambda i: (nt - 1 - i, 0)),
                  pl.BlockSpec((t, f), lambda i: (nt - 1 - i, 0)),
                  pl.BlockSpec((rr, f), lambda i: (jnp.maximum((nt - 1 - i) * per - 1, 0), 0)),
                  pl.BlockSpec((t, f), lambda i: (nt - 1 - i, 1)),
                  pl.BlockSpec((_FFN_K, f), lambda i: (0, 0)),
                  pl.BlockSpec((1, f), lambda i: (0, 0))],
        out_specs=[pl.BlockSpec((t, 2 * f), lambda i: (nt - 1 - i, 0)),
                   pl.BlockSpec((_FFN_K, f), lambda i: (0, 0)), pl.BlockSpec((1, f), lambda i: (0, 0))],
        out_shape=[jax.ShapeDtypeStruct((s, 2 * f), _MXU), jax.ShapeDtypeStruct((_FFN_K, f), _F32),
                   jax.ShapeDtypeStruct((1, f), _F32)],
        scratch_shapes=[pltpu.VMEM((8, f), _F32)],
        sem=("arbitrary",), name=name, comm=comm)


def _mesh_pos():
    return lax.axis_index("x"), lax.axis_index("y"), lax.axis_index("c")


def _flip(v, bit):
    return 1 - v if bit else v


def _sem_scratch(n):
    return [pltpu.SemaphoreType.DMA((7 * n,)), pltpu.SemaphoreType.DMA((7 * n,)), pltpu.SemaphoreType.DMA((n,))]


class _Gather:
    def __init__(self, xs):
        self.ins = list(xs)
        self.outs = [jax.ShapeDtypeStruct((_NDEV,) + v.shape, v.dtype) for v in xs]
        self.scratch = _sem_scratch(len(xs))

    def _plan(self, x_refs, out_refs, sems):
        send_sems, recv_sems, local_sems = sems
        x, y, c = _mesh_pos()
        me, sibling = (x, y, c), (x, y, 1 - c)
        chips = [(1 - x, y), (x, 1 - y), (1 - x, 1 - y)]

        def copy(a, k, block, to, src=None):
            slot = out_refs[a].at[4 * block[0] + 2 * block[1] + block[2]]
            return pltpu.make_async_remote_copy(
                src_ref=slot if src is None else src, dst_ref=slot,
                send_sem=send_sems.at[a * 7 + k], recv_sem=recv_sems.at[a * 7 + k],
                device_id=to, device_id_type=_MESH_ID)

        def own(a):
            return pltpu.make_async_copy(x_refs[a], out_refs[a].at[4 * x + 2 * y + c], local_sems.at[a])

        def first(a):
            return [copy(a, 0, me, sibling, src=x_refs[a])] + [
                copy(a, 1 + j, me, (*chip, c), src=x_refs[a]) for j, chip in enumerate(chips)]

        return me, sibling, chips, c, copy, own, first

    def start(self, x_refs, out_refs, sems):
        _, _, _, _, _, own, first = self._plan(x_refs, out_refs, sems)
        for a in range(len(self.ins)):
            own(a).start()
            for cp in first(a):
                cp.start()

    def finish(self, x_refs, out_refs, sems):
        me, sibling, chips, c, copy, own, first = self._plan(x_refs, out_refs, sems)
        n = len(self.ins)
        passed = []
        for a in range(n):
            for j, chip in enumerate(chips):
                copy(a, 1 + j, (*chip, c), me).wait_recv()
                fwd = copy(a, 4 + j, (*chip, c), sibling)
                fwd.start()
                passed.append(fwd)
        for a in range(n):
            copy(a, 0, sibling, me).wait_recv()
            for j, chip in enumerate(chips):
                copy(a, 4 + j, (*chip, 1 - c), me).wait_recv()
        for a in range(n):
            for cp in first(a):
                cp.wait_send()
        for cp in passed:
            cp.wait_send()
        for a in range(n):
            own(a).wait()


class _Exchange:
    def __init__(self, gs):
        self.ins = list(gs)
        self.outs = [jax.ShapeDtypeStruct(v.shape, v.dtype) for v in gs]
        self.scratch = _sem_scratch(len(gs))

    def _plan(self, g_refs, r_refs, sems):
        send_sems, recv_sems, local_sems = sems
        x, y, c = _mesh_pos()
        me_idx = 4 * x + 2 * y + c
        n = len(self.ins)

        def copy(a, k):
            peer = (_flip(x, k & 4), _flip(y, k & 2), _flip(c, k & 1))
            peer_idx = 4 * peer[0] + 2 * peer[1] + peer[2]
            return pltpu.make_async_remote_copy(
                src_ref=g_refs[a].at[peer_idx], dst_ref=r_refs[a].at[me_idx],
                send_sem=send_sems.at[a * 7 + k - 1], recv_sem=recv_sems.at[a * 7 + k - 1],
                device_id=peer, device_id_type=_MESH_ID)

        copies = [copy(a, k) for a in range(n) for k in range(1, _NDEV)]
        mine = [pltpu.make_async_copy(g_refs[a].at[me_idx], r_refs[a].at[me_idx], local_sems.at[a])
                for a in range(n)]
        return copies, mine

    def start(self, g_refs, r_refs, sems):
        copies, mine = self._plan(g_refs, r_refs, sems)
        for cp in copies + mine:
            cp.start()

    def finish(self, g_refs, r_refs, sems):
        copies, mine = self._plan(g_refs, r_refs, sems)
        for cp in copies:
            cp.wait_recv()
        for cp in copies:
            cp.wait_send()
        for cp in mine:
            cp.wait()


class _Both:
    def __init__(self, first, second):
        self.parts = (first, second)
        self.ins = first.ins + second.ins
        self.outs = first.outs + second.outs
        self.scratch = first.scratch + second.scratch

    def _split(self, ins, outs, sems):
        a, b = self.parts
        na, nb = len(a.ins), len(a.scratch)
        return (a, ins[:na], outs[:na], sems[:nb]), (b, ins[na:], outs[na:], sems[nb:])

    def start(self, ins, outs, sems):
        for part, i, o, s in self._split(ins, outs, sems):
            part.start(i, o, s)

    def finish(self, ins, outs, sems):
        for part, i, o, s in self._split(ins, outs, sems):
            part.finish(i, o, s)


def _comm_call(comm, name):
    def body(*refs):
        n_i, n_o = len(comm.ins), len(comm.outs)
        ins, outs, sems = refs[:n_i], refs[n_i:n_i + n_o], refs[n_i + n_o:]
        comm.start(ins, outs, sems)
        comm.finish(ins, outs, sems)

    return pl.pallas_call(
        body, out_shape=list(comm.outs), in_specs=[_ANY] * len(comm.ins), out_specs=[_ANY] * len(comm.outs),
        scratch_shapes=list(comm.scratch), name=name)(*comm.ins)


def _adamw_math(w, g, m, v):
    m = _ADAM_B1 * m + (1.0 - _ADAM_B1) * g
    v = _ADAM_B2 * v + (1.0 - _ADAM_B2) * (g * g)
    m_hat = m / (1.0 - _ADAM_B1 ** _ADAM_STEP)
    v_hat = v / (1.0 - _ADAM_B2 ** _ADAM_STEP)
    delta = -_ADAM_LR * (m_hat / (jnp.sqrt(v_hat) + _ADAM_EPS) + _ADAM_WD * w)
    return delta, m, v


def _sum_adamw(parts, w, m, v, name):
    r, c = w.shape
    tr = _tile(r, 128)

    def body(p_ref, w_ref, m_ref, v_ref, g_ref, d_ref, nm_ref, nv_ref):
        g = p_ref[0].astype(_F32)
        for j in range(1, _NDEV):
            g = g + p_ref[j].astype(_F32)
        delta, nm, nv = _adamw_math(w_ref[...], g, m_ref[...], v_ref[...])
        g_ref[...] = g
        d_ref[...] = delta
        nm_ref[...] = nm
        nv_ref[...] = nv

    blk = pl.BlockSpec((tr, c), lambda i: (i, 0))
    return pl.pallas_call(
        body, grid=(r // tr,),
        in_specs=[pl.BlockSpec((_NDEV, tr, c), lambda i: (0, i, 0)), blk, blk, blk],
        out_specs=[blk] * 4, out_shape=[jax.ShapeDtypeStruct((r, c), _F32)] * 4,
        compiler_params=_cparams("parallel"), name=name,
    )(parts, w, m, v)


def _sum8(parts, name):
    _, r, c = parts.shape

    def body(p_ref, o_ref):
        g = p_ref[0]
        for j in range(1, _NDEV):
            g = g + p_ref[j]
        o_ref[...] = g

    return pl.pallas_call(
        body, grid=(1,), in_specs=[pl.BlockSpec((_NDEV, r, c), lambda i: (0, 0, 0))],
        out_specs=pl.BlockSpec((r, c), lambda i: (0, 0)), out_shape=jax.ShapeDtypeStruct((r, c), _F32),
        compiler_params=_cparams("arbitrary"), name=name,
    )(parts)


def _adamw_flat(g, w, m, v, name):
    r, c = w.shape

    def body(g_ref, w_ref, m_ref, v_ref, d_ref, nm_ref, nv_ref):
        delta, nm, nv = _adamw_math(w_ref[...], g_ref[...], m_ref[...], v_ref[...])
        d_ref[...] = delta
        nm_ref[...] = nm
        nv_ref[...] = nv

    blk = pl.BlockSpec((r, c), lambda i: (0, 0))
    return pl.pallas_call(
        body, grid=(1,), in_specs=[blk] * 4, out_specs=[blk] * 3,
        out_shape=[jax.ShapeDtypeStruct((r, c), _F32)] * 3,
        compiler_params=_cparams("arbitrary"), name=name,
    )(g, w, m, v)


def _pack(arrs):
    flat = jnp.concatenate([a.reshape(-1).astype(_F32) for a in arrs])
    pad = (-flat.shape[0]) % 1024
    return jnp.pad(flat, (0, pad)).reshape(-1, 128)


def _unpack(flat2d, shapes):
    flat = flat2d.reshape(-1)
    out, off = [], 0
    for sh in shapes:
        size = 1
        for dim in sh:
            size *= dim
        out.append(flat[off:off + size].reshape(sh))
        off += size
    return out


def _block_diag(w):
    h, hd, _ = w.shape
    eye = jnp.eye(h, dtype=w.dtype)
    return (eye[:, None, :, None] * w[:, :, None, :]).reshape(h * hd, h * hd)


def _diag_blocks(full, h):
    hd = full.shape[0] // h
    return jnp.stack([full[i * hd:(i + 1) * hd, i * hd:(i + 1) * hd] for i in range(h)])


def _ungather_cols(g):
    j, k, n = g.shape
    return g.transpose(1, 0, 2).reshape(k, j * n)


def kernel(x, mem, mix_norm_g, w_in, lru_conv_w, lru_conv_b, lru_w_a, lru_b_a, lru_w_x, lru_b_x, lru_lambda, conf_conv_w, conf_conv_b, conf_ln_g, conf_ln_b, w_out, xa_norm_g, mem_norm_g, w_q, w_kv, w_o, ffn_norm_g, w_up, ffn_conv_w, ffn_conv_b, w_down, final_norm_g, loss_target, m_mix_norm_g, m_w_in, m_lru_conv_w, m_lru_conv_b, m_lru_w_a, m_lru_b_a, m_lru_w_x, m_lru_b_x, m_lru_lambda, m_conf_conv_w, m_conf_conv_b, m_conf_ln_g, m_conf_ln_b, m_w_out, m_xa_norm_g, m_mem_norm_g, m_w_q, m_w_kv, m_w_o, m_ffn_norm_g, m_w_up, m_ffn_conv_w, m_ffn_conv_b, m_w_down, m_final_norm_g, v_mix_norm_g, v_w_in, v_lru_conv_w, v_lru_conv_b, v_lru_w_a, v_lru_b_a, v_lru_w_x, v_lru_b_x, v_lru_lambda, v_conf_conv_w, v_conf_conv_b, v_conf_ln_g, v_conf_ln_b, v_w_out, v_xa_norm_g, v_mem_norm_g, v_w_q, v_w_kv, v_w_o, v_ffn_norm_g, v_w_up, v_ffn_conv_w, v_ffn_conv_b, v_w_down, v_final_norm_g):
    names = ["mix_norm_g", "w_in", "lru_conv_w", "lru_conv_b", "lru_w_a", "lru_b_a", "lru_w_x", "lru_b_x",
             "lru_lambda", "conf_conv_w", "conf_conv_b", "conf_ln_g", "conf_ln_b", "w_out", "xa_norm_g",
             "mem_norm_g", "w_q", "w_kv", "w_o", "ffn_norm_g", "w_up", "ffn_conv_w", "ffn_conv_b", "w_down",
             "final_norm_g"]
    loc = locals()
    W = {n: loc[n] for n in names}
    M = {n: loc["m_" + n] for n in names}
    V = {n: loc["v_" + n] for n in names}
    big = ["w_in", "w_out", "w_q", "w_kv", "w_o", "w_up", "w_down"]
    conv_sharded = ["lru_conv_w", "conf_conv_w", "ffn_conv_w"]

    xs = x[0]
    mems = mem[0]
    tgt = loss_target[0]
    me = 4 * lax.axis_index("x") + 2 * lax.axis_index("y") + lax.axis_index("c")

    conv_shapes = [W[n].shape[1:] for n in conv_sharded]
    conv_pack = _pack([W[n][0] for n in conv_sharded])
    shard = {n: W[n][0].astype(_XFER) for n in big}
    h1, (g_in, g_conv) = _rms_fwd(xs, mix_norm_g, "rms1_fwd", comm=_Gather([shard["w_in"], conv_pack]))
    convs = [[] for _ in conv_sharded]
    for j in range(_NDEV):
        for idx, part in enumerate(_unpack(g_conv[j], conv_shapes)):
            convs[idx].append(part)
    lcw, ccw, fcw = [jnp.concatenate(parts, axis=-1) for parts in convs]

    wab = jnp.concatenate([_block_diag(lru_w_a[0]), _block_diag(lru_w_x[0])], axis=1).astype(_MXU)
    mixer_params = (lcw, lru_conv_b, wab, lru_b_a, lru_b_x, lru_lambda, ccw, conf_conv_b, conf_ln_g, conf_ln_b)

    z, (g_out, g_q, g_kv, g_o) = _mm_nn_stacked(
        h1, g_in, _F32, "mm_in_fwd", comm=_Gather([shard[n] for n in ("w_out", "w_q", "w_kv", "w_o")]))
    (ycat, hs, cc), (g_up, g_down) = _mixer_fwd(
        z, *mixer_params, "mixer_fwd", comm=_Gather([shard["w_up"], shard["w_down"]]))
    w_out_f = g_out.reshape(-1, g_out.shape[-1])
    w_q_f = g_q.reshape(-1, g_q.shape[-1])
    w_o_f = g_o.reshape(-1, g_o.shape[-1])
    w_down_f = g_down.reshape(-1, g_down.shape[-1])
    row32, row16, vec32 = (_F32, "row"), (_MXU, "row"), (_F32, "vec")
    x1, h2 = _mm_nn_nat(ycat, w_out_f, "mm_out_fwd", epi=_epi_residual_rms,
                        extra=[(xs, "row"), (xa_norm_g, "vec")], outs=[row32, row16])
    mn = _rms_fwd(mems, mem_norm_g, "rmsm_fwd")
    q = _mm_nn_nat(h2, w_q_f, "mm_q_fwd", outs=[row16])
    kv = _mm_nn_stacked(mn, g_kv, _MXU, "mm_kv_fwd")
    o = _attn_fwd(q, kv, "attn_fwd")
    x2, h3 = _mm_nn_nat(o, w_o_f, "mm_o_fwd", epi=_epi_residual_rms,
                        extra=[(x1, "row"), (ffn_norm_g, "vec")], outs=[row32, row16])

    gu = _mm_nn_stacked(h3, g_up, _MXU, "mm_up_fwd", tm=2048)
    act = _ffn_fwd(gu, fcw, ffn_conv_b, "ffn_fwd")
    gfin = final_norm_g.reshape(1, -1)
    dx3, dx3b, lvec, dg_final = _mm_nn_nat(
        act, w_down_f, "mm_down_fwd", epi=_epi_final,
        extra=[(x2, "row"), (tgt, "row"), (gfin, "vec")], outs=[row32, row16, vec32, vec32], tm=512)
    loss_local = 0.5 * jnp.sum(lvec) / xs.shape[1]
    loss = lax.psum(loss_local, _AXES)

    def rows8(p):
        return p.reshape(_NDEV, p.shape[0] // _NDEV, p.shape[1])

    dact = _mm_nt_nat(dx3b, w_down_f, "mm_down_dgrad", outs=[row16])
    p_down = _mm_tn_nat(act, dx3b, _XFER, "mm_down_wgrad", ts=2048)
    (dgu, dfcw, dfcb), (r_down,) = _ffn_bwd(dact, gu, fcw, ffn_conv_b, "ffn_bwd", comm=_Exchange([rows8(p_down)]))
    dx2, dx2b, dg_ffn = _mm_nt_stacked(
        dgu, g_up, "mm_up_dgrad", epi=_epi_rms_bwd, extra=[(x2, "row"), (ffn_norm_g, "vec"), (dx3, "row")],
        outs=[row32, row16, vec32], tm=512, slabs=4)
    p_up = _mm_tn_stacked(h3, dgu, _NDEV, _XFER, "mm_up_wgrad", slabs=2)

    do = _mm_nt_nat(dx2b, w_o_f, "mm_o_dgrad", outs=[row16])
    p_o = _mm_tn_nat(o, dx2b, _XFER, "mm_o_wgrad", ts=2048)
    dq, dk, dv = _attn_bwd(q, kv, do, "attn_bwd")
    dx1, dx1b, dg_xa = _mm_nt_nat(
        dq, w_q_f, "mm_q_dgrad", epi=_epi_rms_bwd, extra=[(x1, "row"), (xa_norm_g, "vec"), (dx2, "row")],
        outs=[row32, row16, vec32], tm=512)
    p_q = _mm_tn_nat(h2, dq, _XFER, "mm_q_wgrad", ts=2048)
    dkv = jnp.concatenate([dk, dv], axis=1).astype(_MXU)
    dmn = _mm_nt_stacked(dkv, g_kv, "mm_kv_dgrad", outs=[row32], slabs=_NDEV)
    p_kv = _mm_tn_stacked(mn, dkv, _NDEV, _XFER, "mm_kv_wgrad", slabs=_NDEV)
    _, _, dg_mem = _rms_bwd(dmn, mems, mem_norm_g, None, "rmsm_bwd")

    dycat = _mm_nt_nat(dx1b, w_out_f, "mm_out_dgrad", outs=[row32])
    p_out = _mm_tn_nat(ycat, dx1b, _XFER, "mm_out_wgrad", ts=2048)
    ((dz, dlcw, dlcb, dwab, dba, dbx, dlam, dccw, dccb, dlng, dlnb), (r_up, r_o, r_q, r_kv, r_out)) = _mixer_bwd(
        dycat, z, hs, cc, *mixer_params, "mixer_bwd",
        comm=_Exchange([p_up, rows8(p_o), rows8(p_q), p_kv, rows8(p_out)]))
    p_in = _mm_tn_stacked(h1, dz, _NDEV, _XFER, "mm_in_wgrad", slabs=_NDEV)

    c = _D_LRU
    heads = lru_w_a.shape[1]
    small_partial = {
        "lru_conv_w": dlcw, "lru_conv_b": dlcb,
        "lru_w_a": _diag_blocks(dwab[:, :c], heads), "lru_b_a": dba,
        "lru_w_x": _diag_blocks(dwab[:, c:], heads), "lru_b_x": dbx, "lru_lambda": dlam,
        "conf_conv_w": dccw, "conf_conv_b": dccb, "conf_ln_g": dlng, "conf_ln_b": dlnb,
        "xa_norm_g": dg_xa, "mem_norm_g": dg_mem, "ffn_norm_g": dg_ffn,
        "ffn_conv_w": dfcw, "ffn_conv_b": dfcb, "final_norm_g": dg_final,
    }
    early = list(small_partial)
    early_shapes = [small_partial[n].shape for n in early]
    (grad_x, dg_mix), (r_in, early_all) = _mm_nt_stacked(
        dz, g_in, "mm_in_dgrad", epi=_epi_rms_bwd, extra=[(xs, "row"), (mix_norm_g, "vec"), (dx1, "row")],
        outs=[row32, vec32], tm=512, slabs=_NDEV,
        comm=_Both(_Exchange([p_in]), _Gather([_pack([small_partial[n] for n in early])])))
    (mix_all,) = _comm_call(_Gather([dg_mix]), "gather_mix_grad")
    small = early + ["mix_norm_g"]
    small_sum = _unpack(_sum8(early_all, "sum_small_grads"), early_shapes)
    small_sum.append(_sum8(mix_all.reshape(_NDEV, 8, -1), "sum_mix_grad").reshape(dg_mix.shape))
    received = {"w_in": r_in, "w_out": r_out, "w_q": r_q, "w_kv": r_kv, "w_o": r_o, "w_up": r_up,
                "w_down": r_down}

    grads, deltas, new_m, new_v = {}, {}, {}, {}
    for n, rec in ((n, received[n]) for n in big):
        shp = W[n].shape
        w2, m2, v2 = (t.reshape(rec.shape[1:]) for t in (W[n], M[n], V[n]))
        outs = _sum_adamw(rec, w2, m2, v2, "adamw_" + n)
        grads[n], deltas[n], new_m[n], new_v[n] = (t.reshape(shp) for t in outs)

    small_g = []
    for n, g in zip(small, small_sum):
        if n in conv_sharded:
            width = W[n].shape[-1]
            g = lax.dynamic_slice_in_dim(g, me * width, width, axis=1)
        small_g.append(g.reshape(W[n].shape))
    small_shapes = [W[n].shape for n in small]
    sd, sm, sv = _adamw_flat(_pack(small_g), _pack([W[n] for n in small]), _pack([M[n] for n in small]),
                             _pack([V[n] for n in small]), "adamw_small")
    for n, g, d_, m_, v_ in zip(small, small_g, _unpack(sd, small_shapes), _unpack(sm, small_shapes),
                                _unpack(sv, small_shapes)):
        grads[n], deltas[n], new_m[n], new_v[n] = g, d_, m_, v_

    return (loss, grad_x[None], *[grads[n] for n in names], *[deltas[n] for n in names],
            *[new_m[n] for n in names], *[new_v[n] for n in names])
```

```python
import functools

import jax
import jax.numpy as jnp
from jax import lax
from jax.experimental import pallas as pl
from jax.experimental.pallas import tpu as pltpu

_MXU = jnp.bfloat16
_XFER = jnp.bfloat16
_F32 = jnp.float32
_EPS = 1e-6
_NDEV = 8
_AXES = ("x", "y", "c")
_VMEM_LIMIT = 48 * 1024 * 1024

_D_LRU = 512
_XA_HEADS = 4
_RG_C = 8.0
_ADAM_LR, _ADAM_B1, _ADAM_B2, _ADAM_EPS, _ADAM_WD, _ADAM_STEP = 0.001, 0.9, 0.999, 1e-08, 0.01, 10

_MESH_ID = pl.DeviceIdType.MESH
_ANY = pl.BlockSpec(memory_space=pl.ANY)


def _cparams(*sem, vmem=_VMEM_LIMIT):
    return pltpu.CompilerParams(dimension_semantics=tuple(sem), vmem_limit_bytes=vmem)


def _pcall(body, *, args, grid, in_specs, out_specs, out_shape, sem, name, scratch_shapes=(), comm=None,
           vmem=_VMEM_LIMIT):
    outs_l = list(out_shape) if isinstance(out_shape, (list, tuple)) else [out_shape]
    ospecs_l = list(out_specs) if isinstance(out_specs, (list, tuple)) else [out_specs]
    n_in, n_out, n_scr = len(args), len(outs_l), len(scratch_shapes)
    if comm is None:
        res = pl.pallas_call(
            body, grid=grid, in_specs=list(in_specs), out_specs=ospecs_l, out_shape=outs_l,
            scratch_shapes=list(scratch_shapes), compiler_params=_cparams(*sem, vmem=vmem), name=name)(*args)
        return list(res), []
    n_ci, n_co = len(comm.ins), len(comm.outs)

    def wrapped(*refs):
        ins, cins = refs[:n_in], refs[n_in:n_in + n_ci]
        o = n_in + n_ci
        outs, couts = refs[o:o + n_out], refs[o + n_out:o + n_out + n_co]
        s = o + n_out + n_co
        scr, cscr = refs[s:s + n_scr], refs[s + n_scr:]
        first = pl.program_id(0) == 0
        last = pl.program_id(0) == grid[0] - 1
        for ax in range(1, len(grid)):
            first = jnp.logical_and(first, pl.program_id(ax) == 0)
            last = jnp.logical_and(last, pl.program_id(ax) == grid[ax] - 1)

        @pl.when(first)
        def _():
            comm.start(cins, couts, cscr)

        body(*ins, *outs, *scr)

        @pl.when(last)
        def _():
            comm.finish(cins, couts, cscr)

    res = pl.pallas_call(
        wrapped, grid=grid, in_specs=list(in_specs) + [_ANY] * n_ci, out_specs=ospecs_l + [_ANY] * n_co,
        out_shape=outs_l + list(comm.outs), scratch_shapes=list(scratch_shapes) + list(comm.scratch),
        compiler_params=_cparams(*(("arbitrary",) * len(grid)), vmem=vmem), name=name)(*args, *comm.ins)
    return list(res[:n_out]), list(res[n_out:])


def _sigmoid(v):
    return 1.0 / (1.0 + jnp.exp(-v))


_GELU_C = 0.7978845608028654
_GELU_K = 0.044715


def _gelu(v):
    t = jnp.tanh(_GELU_C * (v + _GELU_K * v * v * v))
    return 0.5 * v * (1.0 + t)


def _gelu_and_grad(v):
    v2 = v * v
    t = jnp.tanh(_GELU_C * (v + _GELU_K * v2 * v))
    g = 0.5 * v * (1.0 + t)
    dg = 0.5 * (1.0 + t) + 0.5 * v * (1.0 - t * t) * (_GELU_C * (1.0 + 3.0 * _GELU_K * v2))
    return g, dg


def _softplus(v):
    e = jnp.exp(-jnp.abs(v))
    log1p = jnp.where(e < 1e-2, e * (1.0 - e * (0.5 - e * (1.0 / 3.0))), jnp.log(1.0 + e))
    return jnp.maximum(v, 0.0) + log1p


def _neg_expm1(v):
    series = -v * (1.0 + v * (0.5 + v * ((1.0 / 6.0) + v * (1.0 / 24.0))))
    return jnp.where(v > -0.0625, series, 1.0 - jnp.exp(v))


def _dot(a, b, dims):
    return lax.dot_general(a.astype(_MXU), b.astype(_MXU), (dims, ((), ())), preferred_element_type=_F32)


_NN = ((1,), (0,))
_NT = ((1,), (1,))
_TN = ((0,), (0,))


def _scan_fwd(a, b, rows):
    n = a.shape[0]
    d = 1
    while d < n:
        keep = rows >= d
        b = jnp.where(keep, b + a * pltpu.roll(b, d, 0), b)
        a = jnp.where(keep, a * pltpu.roll(a, d, 0), a)
        d *= 2
    return a, b


def _scan_rev(a, b, rows):
    n = a.shape[0]
    d = 1
    while d < n:
        keep = rows < n - d
        b = jnp.where(keep, b + a * pltpu.roll(b, n - d, 0), b)
        a = jnp.where(keep, a * pltpu.roll(a, n - d, 0), a)
        d *= 2
    return a, b


def _colsum(v):
    return jnp.sum(v, axis=0, keepdims=True)


def _mm(a, b, *, dims, grid, a_spec, b_spec, outs, acc_shape, name, extra=(), epi=None, slabs=1, comm=None):
    nred = grid[-1]
    red_axis = len(grid) - 1
    n_ex, n_out = len(extra), len(outs)
    epi = _epi_store if epi is None else epi

    def body(*refs):
        a_ref, b_ref = refs[:2]
        ex, o_refs, acc_ref = refs[2:2 + n_ex], refs[2 + n_ex:2 + n_ex + n_out], refs[-1]
        if slabs == 1:
            p = _dot(a_ref[...], b_ref[...], dims)
        else:
            n = b_ref.shape[-1]
            p = _dot(a_ref[:, 0:n], b_ref[0], dims)
            for jj in range(1, slabs):
                p = p + _dot(a_ref[:, jj * n:(jj + 1) * n], b_ref[jj], dims)

        first_rows = pl.program_id(0) == 0
        if nred == 1:
            epi(p, ex, o_refs, first_rows)
        else:
            k = pl.program_id(red_axis)

            @pl.when(k == 0)
            def _():
                acc_ref[...] = p

            @pl.when(jnp.logical_and(k > 0, k < nred - 1))
            def _():
                acc_ref[...] += p

            @pl.when(k == nred - 1)
            def _():
                epi(acc_ref[...] + p, ex, o_refs, first_rows)

    sem = ("parallel",) * (len(grid) - 1) + ("arbitrary",)
    if any(o[0].shape[0] == 1 for o in outs):
        sem = ("arbitrary",) * len(grid)
    res, cres = _pcall(
        body, args=(a, b) + tuple(e[0] for e in extra), grid=grid,
        in_specs=[a_spec, b_spec] + [e[1] for e in extra],
        out_specs=[o[1] for o in outs], out_shape=[o[0] for o in outs],
        scratch_shapes=[pltpu.VMEM(acc_shape if nred > 1 else (8, 128), _F32)], sem=sem, name=name, comm=comm)
    res = res[0] if n_out == 1 else res
    return res if comm is None else (res, cres)


def _epi_store(total, ex, outs, first_rows):
    outs[0][...] = total.astype(outs[0].dtype)


def _epi_residual_rms(total, ex, outs, first_rows):
    res_ref, g_ref = ex
    xn = total + res_ref[...]
    outs[0][...] = xn
    r = lax.rsqrt(jnp.mean(xn * xn, axis=-1, keepdims=True) + _EPS)
    outs[1][...] = (xn * r * g_ref[...]).astype(outs[1].dtype)


def _epi_rms_bwd(total, ex, outs, first_rows):
    x_ref, g_ref, dres_ref = ex
    dg_ref = outs[-1]
    xv = x_ref[...]
    r = lax.rsqrt(jnp.mean(xv * xv, axis=-1, keepdims=True) + _EPS)
    xhat = xv * r
    dxh = total * g_ref[...]
    dx = dres_ref[...] + r * (dxh - xhat * jnp.mean(dxh * xhat, axis=-1, keepdims=True))
    for o_ref in outs[:-1]:
        o_ref[...] = dx.astype(o_ref.dtype)

    @pl.when(first_rows)
    def _():
        dg_ref[...] = jnp.zeros_like(dg_ref)

    dg_ref[...] += _colsum(total * xhat)


def _epi_final(total, ex, outs, first_rows):
    res_ref, t_ref, g_ref = ex
    dx_ref, dxb_ref, l_ref, dg_ref = outs
    xv = total + res_ref[...]
    gv = g_ref[...]
    d = xv.shape[-1]
    r = lax.rsqrt(jnp.mean(xv * xv, axis=-1, keepdims=True) + _EPS)
    xhat = xv * r
    err = xhat * gv - t_ref[...]
    dy = err * (1.0 / d)
    dxh = dy * gv
    dx = r * (dxh - xhat * jnp.mean(dxh * xhat, axis=-1, keepdims=True))
    dx_ref[...] = dx
    dxb_ref[...] = dx.astype(dxb_ref.dtype)

    @pl.when(first_rows)
    def _():
        l_ref[...] = jnp.zeros_like(l_ref)
        dg_ref[...] = jnp.zeros_like(dg_ref)

    l_ref[...] += _colsum(err * err)
    dg_ref[...] += _colsum(dy * xhat)


def _tile(m, cap):
    t = min(m, cap)
    assert m % t == 0
    return t


def _row_spec(tm, n):
    return pl.BlockSpec((tm, n), lambda i, *_: (i, 0))


def _vec_spec(n):
    return pl.BlockSpec((1, n), lambda *_: (0, 0))


def _row_io(m, n, tm, extra, outs):
    def spec(kind):
        return _row_spec(tm, n) if kind == "row" else _vec_spec(n)

    ex = [(arr, spec(kind)) for arr, kind in extra]
    os_ = [(jax.ShapeDtypeStruct((m, n) if kind == "row" else (1, n), dt), spec(kind)) for dt, kind in outs]
    return ex, os_


def _mm_nn_stacked(a, w, out_dtype, name, comm=None, tm=1024):
    m, k = a.shape
    j, _, n = w.shape
    tm = _tile(m, tm)
    return _mm(a, w, dims=_NN, grid=(m // tm, j, 1),
               a_spec=pl.BlockSpec((tm, k), lambda i, jj, r: (i, 0)),
               b_spec=pl.BlockSpec((None, k, n), lambda i, jj, r: (jj, 0, 0)),
               outs=[(jax.ShapeDtypeStruct((m, j * n), out_dtype), pl.BlockSpec((tm, n), lambda i, jj, r: (i, jj)))],
               acc_shape=(tm, n), name=name, comm=comm)


def _mm_nn_nat(a, w, name, *, outs, extra=(), epi=None, tm=1024):
    m, kt = a.shape
    _, n = w.shape
    tm = _tile(m, tm)
    tk = _tile(kt, 1024)
    ex, os_ = _row_io(m, n, tm, extra, outs)
    return _mm(a, w, dims=_NN, grid=(m // tm, kt // tk),
               a_spec=pl.BlockSpec((tm, tk), lambda i, r: (i, r)),
               b_spec=pl.BlockSpec((tk, n), lambda i, r: (r, 0)),
               outs=os_, extra=ex, epi=epi, acc_shape=(tm, n), name=name)


def _mm_nt_stacked(dc, w, name, *, outs, extra=(), epi=None, comm=None, tm=1024, slabs=1):
    m = dc.shape[0]
    j, k, n = w.shape
    tm = _tile(m, tm)
    assert j % slabs == 0
    ex, os_ = _row_io(m, k, tm, extra, outs)
    wblk = (None, k, n) if slabs == 1 else (slabs, k, n)
    return _mm(dc, w, dims=_NT, grid=(m // tm, j // slabs),
               a_spec=pl.BlockSpec((tm, slabs * n), lambda i, r: (i, r)),
               b_spec=pl.BlockSpec(wblk, lambda i, r: (r, 0, 0)),
               outs=os_, extra=ex, epi=epi, acc_shape=(tm, k), name=name, slabs=slabs, comm=comm)


def _mm_nt_nat(dc, w, name, *, outs, extra=(), epi=None, tm=1024):
    m, n = dc.shape
    kt = w.shape[0]
    tm = _tile(m, tm)
    tkb = _tile(kt, 1024)
    if kt == tkb:
        ex, os_ = _row_io(m, kt, tm, extra, outs)
    else:
        assert not extra and len(outs) == 1
        ex, os_ = [], [(jax.ShapeDtypeStruct((m, kt), outs[0][0]), pl.BlockSpec((tm, tkb), lambda i, kb, r: (i, kb)))]
    return _mm(dc, w, dims=_NT, grid=(m // tm, kt // tkb, 1),
               a_spec=pl.BlockSpec((tm, n), lambda i, kb, r: (i, 0)),
               b_spec=pl.BlockSpec((tkb, n), lambda i, kb, r: (kb, 0)),
               outs=os_, extra=ex, epi=epi, acc_shape=(tm, tkb), name=name)


def _mm_tn_stacked(a, dc, j, out_dtype, name, slabs=1, ts=1024):
    s, k = a.shape
    n = dc.shape[1] // j
    ts = _tile(s, ts)
    assert j % slabs == 0

    def epi(total, ex, outs, first_rows):
        for jj in range(slabs):
            outs[0][jj] = total[:, jj * n:(jj + 1) * n].astype(outs[0].dtype)

    return _mm(a, dc, dims=_TN, grid=(j // slabs, s // ts),
               a_spec=pl.BlockSpec((ts, k), lambda jj, r: (r, 0)),
               b_spec=pl.BlockSpec((ts, slabs * n), lambda jj, r: (r, jj)),
               outs=[(jax.ShapeDtypeStruct((j, k, n), out_dtype),
                      pl.BlockSpec((slabs, k, n), lambda jj, r: (jj, 0, 0)))],
               epi=epi, acc_shape=(k, slabs * n), name=name)


def _mm_tn_nat(a, dc, out_dtype, name, ts=1024):
    s, kt = a.shape
    n = dc.shape[1]
    ts = _tile(s, ts)
    tkb = _tile(kt, 512)
    return _mm(a, dc, dims=_TN, grid=(kt // tkb, s // ts),
               a_spec=pl.BlockSpec((ts, tkb), lambda kb, r: (r, kb)),
               b_spec=pl.BlockSpec((ts, n), lambda kb, r: (r, 0)),
               outs=[(jax.ShapeDtypeStruct((kt, n), out_dtype), pl.BlockSpec((tkb, n), lambda kb, r: (kb, 0)))],
               acc_shape=(tkb, n), name=name)


def _rms_fwd(x, g, name, comm=None):
    s, d = x.shape
    t = _tile(s, 256)

    def body(x_ref, g_ref, h_ref):
        xv = x_ref[...]
        r = lax.rsqrt(jnp.mean(xv * xv, axis=-1, keepdims=True) + _EPS)
        h_ref[...] = (xv * r * g_ref[...]).astype(h_ref.dtype)

    res, cres = _pcall(
        body, args=(x, g), grid=(s // t,),
        in_specs=[pl.BlockSpec((t, d), lambda i: (i, 0)), pl.BlockSpec((1, d), lambda i: (0, 0))],
        out_specs=pl.BlockSpec((t, d), lambda i: (i, 0)),
        out_shape=jax.ShapeDtypeStruct((s, d), _MXU), sem=("parallel",), name=name, comm=comm)
    return res[0] if comm is None else (res[0], cres)


def _rms_bwd(dh, x, g, dres, name):
    s, d = x.shape
    t = _tile(s, 256)
    has_res = dres is not None

    def body(*refs):
        if has_res:
            dh_ref, x_ref, g_ref, dres_ref, dx_ref, dxb_ref, dg_ref = refs
        else:
            dh_ref, x_ref, g_ref, dx_ref, dxb_ref, dg_ref = refs
        xv = x_ref[...]
        dhv = dh_ref[...]
        r = lax.rsqrt(jnp.mean(xv * xv, axis=-1, keepdims=True) + _EPS)
        xhat = xv * r
        dxh = dhv * g_ref[...]
        dx = r * (dxh - xhat * jnp.mean(dxh * xhat, axis=-1, keepdims=True))
        if has_res:
            dx = dx + dres_ref[...]
        dx_ref[...] = dx
        dxb_ref[...] = dx.astype(dxb_ref.dtype)

        @pl.when(pl.program_id(0) == 0)
        def _():
            dg_ref[...] = jnp.zeros_like(dg_ref)

        dg_ref[...] += _colsum(dhv * xhat)

    row = pl.BlockSpec((t, d), lambda i: (i, 0))
    vec = pl.BlockSpec((1, d), lambda i: (0, 0))
    in_specs = [row, row, vec] + ([row] if has_res else [])
    args = (dh, x, g) + ((dres,) if has_res else ())
    return pl.pallas_call(
        body, grid=(s // t,), in_specs=in_specs, out_specs=[row, row, vec],
        out_shape=[jax.ShapeDtypeStruct((s, d), _F32), jax.ShapeDtypeStruct((s, d), _MXU),
                   jax.ShapeDtypeStruct((1, d), _F32)],
        compiler_params=_cparams("arbitrary"), name=name,
    )(*args)


_LRU_K = 4
_CONF_K = 31
_LRU_HALO = 8
_CONF_HALO = 32
_MIX_T = 256


def _lru_gates(lx, wab_ref, ba_ref, bx_ref, lam_ref):
    c = _D_LRU
    pre = _dot(lx, wab_ref[...], _NN)
    r = _sigmoid(pre[:, :c] + ba_ref[...])
    ig = _sigmoid(pre[:, c:] + bx_ref[...])
    sp = _softplus(-lam_ref[...])
    log_a = (-_RG_C) * r * sp
    a = jnp.exp(log_a)
    mult = jnp.sqrt(_neg_expm1(2.0 * log_a))
    return r, ig, sp, a, mult


def _causal_conv(ext_ref, halo, w_ref, b_ref, taps, t):
    acc = b_ref[...] + w_ref[0:1, :] * ext_ref[pl.ds(halo - (taps - 1), t), :]
    for k in range(1, taps):
        acc = acc + w_ref[k:k + 1, :] * ext_ref[pl.ds(halo - (taps - 1) + k, t), :]
    return acc


def _mixer_fwd(z, lcw, lcb, wab, ba, bx, lam, ccw, ccb, lng, lnb, name, comm=None):
    s = z.shape[0]
    c = _D_LRU
    t = _tile(s, _MIX_T)
    nt = s // t

    def body(lx0_ref, lx0h_ref, gate_ref, ca_ref, cah_ref, cb_ref, cbh_ref,
             lcw_ref, lcb_ref, wab_ref, ba_ref, bx_ref, lam_ref, ccw_ref, ccb_ref, lng_ref, lnb_ref,
             ycat_ref, hs_ref, cc_ref, ext_ref, cge_ref, hc_ref):
        i = pl.program_id(0)
        first = i == 0
        rows = lax.broadcasted_iota(jnp.int32, (t, c), 0)

        ext_ref[0:_LRU_HALO, :] = jnp.where(first, 0.0, lx0h_ref[...])
        ext_ref[_LRU_HALO:_LRU_HALO + t, :] = lx0_ref[...]
        lx = _causal_conv(ext_ref, _LRU_HALO, lcw_ref, lcb_ref, _LRU_K, t)
        r, ig, sp, a, mult = _lru_gates(lx, wab_ref, ba_ref, bx_ref, lam_ref)
        u = mult * (ig * lx)
        a_cum, h_loc = _scan_fwd(a, u, rows)

        @pl.when(first)
        def _():
            hc_ref[...] = jnp.zeros_like(hc_ref)

        h = h_loc + a_cum * hc_ref[7:8, :]
        hs_ref[...] = h
        hc_ref[...] = hs_ref[pl.ds(t - 8, 8), :]
        ycat_ref[:, 0:c] = (h * _gelu(gate_ref[...])).astype(ycat_ref.dtype)

        cge_ref[0:_CONF_HALO, :] = jnp.where(first, 0.0, cah_ref[...] * _sigmoid(cbh_ref[...]))
        cge_ref[_CONF_HALO:_CONF_HALO + t, :] = ca_ref[...] * _sigmoid(cb_ref[...])
        cc = _causal_conv(cge_ref, _CONF_HALO, ccw_ref, ccb_ref, _CONF_K, t)
        cc_ref[...] = cc
        xc = cc - jnp.mean(cc, axis=-1, keepdims=True)
        rstd = lax.rsqrt(jnp.mean(xc * xc, axis=-1, keepdims=True) + _EPS)
        ln = xc * rstd * lng_ref[...] + lnb_ref[...]
        ycat_ref[:, c:2 * c] = (ln * _sigmoid(ln)).astype(ycat_ref.dtype)

    def col(j):
        return pl.BlockSpec((t, c), lambda i: (i, j))

    def halo(j, rows_):
        per = t // rows_
        return pl.BlockSpec((rows_, c), lambda i: (jnp.maximum(i * per - 1, 0), j))

    def full(arr):
        return pl.BlockSpec(arr.shape, lambda i: (0,) * arr.ndim)

    params = (lcw, lcb, wab, ba, bx, lam, ccw, ccb, lng, lnb)
    res, cres = _pcall(
        body, args=(z, z, z, z, z, z, z, *params), grid=(nt,),
        in_specs=[col(0), halo(0, _LRU_HALO), col(1), col(2), halo(2, _CONF_HALO), col(3), halo(3, _CONF_HALO)]
        + [full(p) for p in params],
        out_specs=[pl.BlockSpec((t, 2 * c), lambda i: (i, 0)), pl.BlockSpec((t, c), lambda i: (i, 0)),
                   pl.BlockSpec((t, c), lambda i: (i, 0))],
        out_shape=[jax.ShapeDtypeStruct((s, 2 * c), _MXU), jax.ShapeDtypeStruct((s, c), _F32),
                   jax.ShapeDtypeStruct((s, c), _F32)],
        scratch_shapes=[pltpu.VMEM((t + _LRU_HALO, c), _F32), pltpu.VMEM((t + _CONF_HALO, c), _F32),
                        pltpu.VMEM((8, c), _F32)],
        sem=("arbitrary",), name=name, comm=comm)
    return res, cres


def _mixer_bwd(dycat, z, hs, cc, lcw, lcb, wab, ba, bx, lam, ccw, ccb, lng, lnb, name, comm=None):
    s = z.shape[0]
    c = _D_LRU
    t = _tile(s, _MIX_T)
    nt = s // t

    def body(dyl_ref, dc_ref, lx0_ref, lx0h_ref, gate_ref, ca_ref, cah_ref, cb_ref, cbh_ref,
             hs_ref, hsh_ref, cc_ref,
             lcw_ref, lcb_ref, wab_ref, ba_ref, bx_ref, lam_ref, ccw_ref, ccb_ref, lng_ref, lnb_ref,
             dz_ref, dlcw_ref, dlcb_ref, dwab_ref, dba_ref, dbx_ref, dlam_ref, dccw_ref, dccb_ref, dlng_ref,
             dlnb_ref,
             ext_ref, up_ref, cge_ref, dce_ref, xc_ref, dlxc_ref, dccc_ref):
        i = pl.program_id(0)
        first_tile = i == nt - 1
        last_tile = i == 0
        rows = lax.broadcasted_iota(jnp.int32, (t, c), 0)

        @pl.when(last_tile)
        def _():
            for ref in (dlcw_ref, dlcb_ref, dwab_ref, dba_ref, dbx_ref, dlam_ref, dccw_ref, dccb_ref, dlng_ref,
                        dlnb_ref, xc_ref, dlxc_ref, dccc_ref):
                ref[...] = jnp.zeros_like(ref)

        ext_ref[0:_LRU_HALO, :] = jnp.where(first_tile, 0.0, lx0h_ref[...])
        ext_ref[_LRU_HALO:_LRU_HALO + t, :] = lx0_ref[...]
        lx = _causal_conv(ext_ref, _LRU_HALO, lcw_ref, lcb_ref, _LRU_K, t)
        r, ig, sp, a, mult = _lru_gates(lx, wab_ref, ba_ref, bx_ref, lam_ref)
        h = hs_ref[...]
        gl, dgl = _gelu_and_grad(gate_ref[...])
        dyl = dyl_ref[...]
        dz_ref[:, c:2 * c] = (dyl * h * dgl).astype(dz_ref.dtype)
        dh = dyl * gl

        up_ref[0:t, :] = a
        up_ref[t:t + 8, :] = jnp.ones((8, c), _F32)
        a_up = up_ref[pl.ds(1, t), :]
        a_cum, g_loc = _scan_rev(a_up, dh, rows)
        gt = g_loc + a_cum * xc_ref[0:1, :]
        xc_ref[...] = (a * gt)[0:8, :]

        up_ref[0:8, :] = jnp.where(first_tile, 0.0, hsh_ref[...])
        up_ref[8:8 + t, :] = h
        hprev = up_ref[pl.ds(7, t), :]

        da = gt * hprev
        dmult = gt * ig * lx
        dig = gt * mult * lx
        dlx = gt * mult * ig
        dlog_a = da * a - dmult * a * a / mult
        dpre_r = dlog_a * (-_RG_C) * sp * r * (1.0 - r)
        dpre_i = dig * ig * (1.0 - ig)
        dlam_ref[...] += _colsum(dlog_a * r) * (_RG_C * _sigmoid(-lam_ref[...]))
        dba_ref[...] += _colsum(dpre_r)
        dbx_ref[...] += _colsum(dpre_i)
        dpre = jnp.concatenate([dpre_r, dpre_i], axis=1).astype(_MXU)
        dlx = dlx + _dot(dpre, wab_ref[...], _NT)
        dwab_ref[...] += _dot(lx, dpre, _TN)

        dlcb_ref[...] += _colsum(dlx)
        up_ref[0:t, :] = dlx
        up_ref[t:t + 8, :] = dlxc_ref[...]
        dlxc_ref[...] = dlx[0:8, :]
        acc = lcw_ref[0:1, :] * up_ref[pl.ds(_LRU_K - 1, t), :]
        for k in range(1, _LRU_K):
            acc = acc + lcw_ref[k:k + 1, :] * up_ref[pl.ds(_LRU_K - 1 - k, t), :]
        dz_ref[:, 0:c] = acc.astype(dz_ref.dtype)
        for k in range(_LRU_K):
            dlcw_ref[k:k + 1, :] += _colsum(dlx * ext_ref[pl.ds(_LRU_HALO - (_LRU_K - 1) + k, t), :])

        sig_b = _sigmoid(cb_ref[...])
        ca = ca_ref[...]
        cge_ref[0:_CONF_HALO, :] = jnp.where(first_tile, 0.0, cah_ref[...] * _sigmoid(cbh_ref[...]))
        cge_ref[_CONF_HALO:_CONF_HALO + t, :] = ca * sig_b
        ccv = cc_ref[...]
        xcen = ccv - jnp.mean(ccv, axis=-1, keepdims=True)
        rstd = lax.rsqrt(jnp.mean(xcen * xcen, axis=-1, keepdims=True) + _EPS)
        xn = xcen * rstd
        ln = xn * lng_ref[...] + lnb_ref[...]
        sg = _sigmoid(ln)
        dln = dc_ref[...] * (sg * (1.0 + ln * (1.0 - sg)))
        dlng_ref[...] += _colsum(dln * xn)
        dlnb_ref[...] += _colsum(dln)
        dxn = dln * lng_ref[...]
        dcc = rstd * (dxn - jnp.mean(dxn, axis=-1, keepdims=True)
                      - xn * jnp.mean(dxn * xn, axis=-1, keepdims=True))
        dccb_ref[...] += _colsum(dcc)
        for k in range(_CONF_K):
            dccw_ref[k:k + 1, :] += _colsum(dcc * cge_ref[pl.ds(_CONF_HALO - (_CONF_K - 1) + k, t), :])
        dce_ref[0:t, :] = dcc
        dce_ref[t:t + _CONF_HALO, :] = dccc_ref[...]
        dccc_ref[...] = dcc[0:_CONF_HALO, :]
        dcg = ccw_ref[0:1, :] * dce_ref[pl.ds(_CONF_K - 1, t), :]
        for k in range(1, _CONF_K):
            dcg = dcg + ccw_ref[k:k + 1, :] * dce_ref[pl.ds(_CONF_K - 1 - k, t), :]
        dz_ref[:, 2 * c:3 * c] = (dcg * sig_b).astype(dz_ref.dtype)
        dz_ref[:, 3 * c:4 * c] = (dcg * ca * sig_b * (1.0 - sig_b)).astype(dz_ref.dtype)

    def col(j):
        return pl.BlockSpec((t, c), lambda i: (nt - 1 - i, j))

    def halo(j, rows_):
        per = t // rows_
        return pl.BlockSpec((rows_, c), lambda i: (jnp.maximum((nt - 1 - i) * per - 1, 0), j))

    def full(shape):
        return pl.BlockSpec(shape, lambda i: (0,) * len(shape))

    params = (lcw, lcb, wab, ba, bx, lam, ccw, ccb, lng, lnb)
    small = [(_LRU_K, c), (1, c), (c, 2 * c), (1, c), (1, c), (1, c), (_CONF_K, c), (1, c), (1, c), (1, c)]
    return _pcall(
        body, args=(dycat, dycat, z, z, z, z, z, z, z, hs, hs, cc, *params), grid=(nt,),
        in_specs=[col(0), col(1),
                  col(0), halo(0, _LRU_HALO), col(1), col(2), halo(2, _CONF_HALO), col(3), halo(3, _CONF_HALO),
                  col(0), halo(0, 8), col(0)]
        + [full(p.shape) for p in params],
        out_specs=[pl.BlockSpec((t, 4 * c), lambda i: (nt - 1 - i, 0))] + [full(sh) for sh in small],
        out_shape=[jax.ShapeDtypeStruct((s, 4 * c), _MXU)] + [jax.ShapeDtypeStruct(sh, _F32) for sh in small],
        scratch_shapes=[pltpu.VMEM((t + _LRU_HALO, c), _F32), pltpu.VMEM((t + 8, c), _F32),
                        pltpu.VMEM((t + _CONF_HALO, c), _F32), pltpu.VMEM((t + _CONF_HALO, c), _F32),
                        pltpu.VMEM((8, c), _F32), pltpu.VMEM((8, c), _F32), pltpu.VMEM((_CONF_HALO, c), _F32)],
        sem=("arbitrary",), name=name, comm=comm)


_ATT_T = 512


def _attn_probs(qh, kh, scale):
    sc = _dot(qh, kh, _NT) * scale
    e = jnp.exp(sc - jnp.max(sc, axis=-1, keepdims=True))
    return e / jnp.sum(e, axis=-1, keepdims=True)


def _attn_fwd(q, kv, name):
    s, d = q.shape
    nm = kv.shape[0]
    hd = d // _XA_HEADS
    t = _tile(s, _ATT_T)
    scale = hd ** -0.5

    def body(q_ref, k_ref, v_ref, o_ref):
        for hh in range(_XA_HEADS):
            sl = slice(hh * hd, (hh + 1) * hd)
            p = _attn_probs(q_ref[:, sl], k_ref[:, sl], scale)
            o_ref[:, sl] = _dot(p, v_ref[:, sl], _NN).astype(o_ref.dtype)

    return pl.pallas_call(
        body, grid=(s // t,),
        in_specs=[pl.BlockSpec((t, d), lambda i: (i, 0)), pl.BlockSpec((nm, d), lambda i: (0, 0)),
                  pl.BlockSpec((nm, d), lambda i: (0, 1))],
        out_specs=pl.BlockSpec((t, d), lambda i: (i, 0)),
        out_shape=jax.ShapeDtypeStruct((s, d), _MXU),
        compiler_params=_cparams("parallel"), name=name,
    )(q, kv, kv)


def _attn_bwd(q, kv, do, name):
    s, d = q.shape
    nm = kv.shape[0]
    hd = d // _XA_HEADS
    t = _tile(s, _ATT_T)
    scale = hd ** -0.5

    def body(q_ref, k_ref, v_ref, do_ref, dq_ref, dk_ref, dv_ref):
        @pl.when(pl.program_id(0) == 0)
        def _():
            dk_ref[...] = jnp.zeros_like(dk_ref)
            dv_ref[...] = jnp.zeros_like(dv_ref)

        for hh in range(_XA_HEADS):
            sl = slice(hh * hd, (hh + 1) * hd)
            qh = q_ref[:, sl]
            kh = k_ref[:, sl]
            doh = do_ref[:, sl]
            p = _attn_probs(qh, kh, scale)
            dp = _dot(doh, v_ref[:, sl], _NT)
            dv_ref[:, sl] += _dot(p, doh, _TN)
            ds = (p * (dp - jnp.sum(dp * p, axis=-1, keepdims=True)) * scale).astype(_MXU)
            dq_ref[:, sl] = _dot(ds, kh, _NN).astype(dq_ref.dtype)
            dk_ref[:, sl] += _dot(ds, qh, _TN)

    row = pl.BlockSpec((t, d), lambda i: (i, 0))
    return pl.pallas_call(
        body, grid=(s // t,),
        in_specs=[row, pl.BlockSpec((nm, d), lambda i: (0, 0)), pl.BlockSpec((nm, d), lambda i: (0, 1)), row],
        out_specs=[row, pl.BlockSpec((nm, d), lambda i: (0, 0)), pl.BlockSpec((nm, d), lambda i: (0, 0))],
        out_shape=[jax.ShapeDtypeStruct((s, d), _MXU), jax.ShapeDtypeStruct((nm, d), _F32),
                   jax.ShapeDtypeStruct((nm, d), _F32)],
        compiler_params=_cparams("arbitrary"), name=name,
    )(q, kv, kv, do)


_FFN_K = 3
_FFN_FUSED_T = 256
_VMEM_LIMIT_FUSED = 58 * 1024 * 1024


def _ffn_fused_fwd(h3, w_up, w_down, fcw, fcb, x2, target, gfin, name):
    s, d = h3.shape
    nblk, _, n = w_up.shape
    half = nblk // 2
    f = half * n
    t = _tile(s, _FFN_FUSED_T)

    def body(h_ref, wup_ref, wdown_ref, w_ref, b_ref, x2_ref, t_ref, g_ref,
             gu_ref, act_ref, dx_ref, dxb_ref, l_ref, dg_ref, ext0_ref, ext1_ref, halo_ref):
        i = pl.program_id(0)
        first = i == 0
        h = h_ref[...]
        total = None
        for j in range(half):
            cs = slice(j * n, (j + 1) * n)
            ext_ref = ext0_ref if j % 2 == 0 else ext1_ref
            g0 = _dot(h, wup_ref[j], _NN)
            u = _dot(h, wup_ref[half + j], _NN)
            gu_ref[:, cs] = g0.astype(gu_ref.dtype)
            gu_ref[:, f + j * n:f + (j + 1) * n] = u.astype(gu_ref.dtype)
            ext_ref[0:8, :] = jnp.where(first, 0.0, halo_ref[:, cs])
            ext_ref[8:8 + t, :] = g0
            halo_ref[:, cs] = g0[t - 8:t, :]
            g = _causal_conv(ext_ref, 8, w_ref.at[:, cs], b_ref.at[:, cs], _FFN_K, t)
            act = (_gelu(g) * u).astype(act_ref.dtype)
            act_ref[:, cs] = act
            p = _dot(act, wdown_ref[cs, :], _NN)
            total = p if total is None else total + p
        _epi_final(total, (x2_ref, t_ref, g_ref), (dx_ref, dxb_ref, l_ref, dg_ref), first)

    def const(shape):
        return pl.BlockSpec(shape, lambda i: (0,) * len(shape), pipeline_mode=pl.Buffered(1))

    row = pl.BlockSpec((t, d), lambda i: (i, 0))
    vec = pl.BlockSpec((1, d), lambda i: (0, 0))
    return pl.pallas_call(
        body, grid=(s // t,),
        in_specs=[row, const(w_up.shape), const(w_down.shape), const(fcw.shape), const(fcb.shape), row, row, vec],
        out_specs=[pl.BlockSpec((t, 2 * f), lambda i: (i, 0)), pl.BlockSpec((t, f), lambda i: (i, 0)),
                   row, row, vec, vec],
        out_shape=[jax.ShapeDtypeStruct((s, 2 * f), _MXU), jax.ShapeDtypeStruct((s, f), _MXU),
                   jax.ShapeDtypeStruct((s, d), _F32), jax.ShapeDtypeStruct((s, d), _MXU),
                   jax.ShapeDtypeStruct((1, d), _F32), jax.ShapeDtypeStruct((1, d), _F32)],
        scratch_shapes=[pltpu.VMEM((t + 8, n), _F32), pltpu.VMEM((t + 8, n), _F32), pltpu.VMEM((8, f), _F32)],
        compiler_params=_cparams("arbitrary", vmem=_VMEM_LIMIT_FUSED), name=name,
    )(h3, w_up, w_down, fcw, fcb, x2, target, gfin)


def _ffn_fused_bwd(dx3b, gu, w_down, w_up, fcw, fcb, x2, gnorm, dx3, name, comm=None):
    s, d = x2.shape
    nblk, _, n = w_up.shape
    half = nblk // 2
    f = half * n
    t = _tile(s, _FFN_FUSED_T)
    nt = s // t
    hrows = 16

    def body(dxb_ref, gu_ref, guh_ref, wdown_ref, wup_ref, w_ref, b_ref, x2_ref, g_ref, dx3_ref,
             dgu_ref, dx2_ref, dx2b_ref, dgn_ref, dw_ref, db_ref, ext0_ref, ext1_ref, up0_ref, up1_ref, car_ref):
        i = pl.program_id(0)
        first_tile = i == nt - 1
        last_tile = i == 0

        @pl.when(last_tile)
        def _():
            dw_ref[...] = jnp.zeros_like(dw_ref)
            db_ref[...] = jnp.zeros_like(db_ref)
            car_ref[...] = jnp.zeros_like(car_ref)

        dxb = dxb_ref[...]
        total = None
        for j in range(half):
            cs = slice(j * n, (j + 1) * n)
            us = slice(f + j * n, f + (j + 1) * n)
            ext_ref = ext0_ref if j % 2 == 0 else ext1_ref
            up_ref = up0_ref if j % 2 == 0 else up1_ref
            dact = _dot(dxb, wdown_ref[cs, :], _NT)
            ext_ref[0:8, :] = jnp.where(first_tile, 0.0, guh_ref[:, cs].astype(_F32)[hrows - 8:hrows])
            ext_ref[8:8 + t, :] = gu_ref[:, cs].astype(_F32)
            g = _causal_conv(ext_ref, 8, w_ref.at[:, cs], b_ref.at[:, cs], _FFN_K, t)
            gl, dgl = _gelu_and_grad(g)
            du = (dact * gl).astype(dgu_ref.dtype)
            dgu_ref[:, us] = du
            dg = dact * gu_ref[:, us].astype(_F32) * dgl
            db_ref[:, cs] += _colsum(dg)
            for k in range(_FFN_K):
                dw_ref[k:k + 1, cs] += _colsum(dg * ext_ref[pl.ds(8 - (_FFN_K - 1) + k, t), :])
            up_ref[0:t, :] = dg
            up_ref[t:t + 8, :] = car_ref[:, cs]
            car_ref[:, cs] = dg[0:8, :]
            dg0 = w_ref[0:1, cs] * up_ref[pl.ds(_FFN_K - 1, t), :]
            for k in range(1, _FFN_K):
                dg0 = dg0 + w_ref[k:k + 1, cs] * up_ref[pl.ds(_FFN_K - 1 - k, t), :]
            dg0 = dg0.astype(dgu_ref.dtype)
            dgu_ref[:, cs] = dg0
            p = _dot(dg0, wup_ref[j], _NT) + _dot(du, wup_ref[half + j], _NT)
            total = p if total is None else total + p
        _epi_rms_bwd(total, (x2_ref, g_ref, dx3_ref), (dx2_ref, dx2b_ref, dgn_ref), last_tile)

    def const(shape):
        return pl.BlockSpec(shape, lambda i: (0,) * len(shape), pipeline_mode=pl.Buffered(1))

    row = pl.BlockSpec((t, d), lambda i: (nt - 1 - i, 0))
    vec = pl.BlockSpec((1, d), lambda i: (0, 0))
    per = t // hrows
    return _pcall(
        body, args=(dx3b, gu, gu, w_down, w_up, fcw, fcb, x2, gnorm, dx3), grid=(nt,),
        in_specs=[row, pl.BlockSpec((t, 2 * f), lambda i: (nt - 1 - i, 0)),
                  pl.BlockSpec((hrows, 2 * f), lambda i: (jnp.maximum((nt - 1 - i) * per - 1, 0), 0)),
                  const(w_down.shape), const(w_up.shape), const(fcw.shape), const(fcb.shape), row, vec, row],
        out_specs=[pl.BlockSpec((t, 2 * f), lambda i: (nt - 1 - i, 0)), row, row, vec,
                   pl.BlockSpec((_FFN_K, f), lambda i: (0, 0)), pl.BlockSpec((1, f), lambda i: (0, 0))],
        out_shape=[jax.ShapeDtypeStruct((s, 2 * f), _MXU), jax.ShapeDtypeStruct((s, d), _F32),
                   jax.ShapeDtypeStruct((s, d), _MXU), jax.ShapeDtypeStruct((1, d), _F32),
                   jax.ShapeDtypeStruct((_FFN_K, f), _F32), jax.ShapeDtypeStruct((1, f), _F32)],
        scratch_shapes=[pltpu.VMEM((t + 8, n), _F32), pltpu.VMEM((t + 8, n), _F32),
                        pltpu.VMEM((t + 8, n), _F32), pltpu.VMEM((t + 8, n), _F32), pltpu.VMEM((8, f), _F32)],
        sem=("arbitrary",), name=name, comm=comm, vmem=_VMEM_LIMIT_FUSED)


def _mesh_pos():
    return lax.axis_index("x"), lax.axis_index("y"), lax.axis_index("c")


def _flip(v, bit):
    return 1 - v if bit else v


def _sem_scratch(n):
    return [pltpu.SemaphoreType.DMA((7 * n,)), pltpu.SemaphoreType.DMA((7 * n,)), pltpu.SemaphoreType.DMA((n,))]


class _Gather:
    def __init__(self, xs):
        self.ins = list(xs)
        self.outs = [jax.ShapeDtypeStruct((_NDEV,) + v.shape, v.dtype) for v in xs]
        self.scratch = _sem_scratch(len(xs))

    def _plan(self, x_refs, out_refs, sems):
        send_sems, recv_sems, local_sems = sems
        x, y, c = _mesh_pos()
        me, sibling = (x, y, c), (x, y, 1 - c)
        chips = [(1 - x, y), (x, 1 - y), (1 - x, 1 - y)]

        def copy(a, k, block, to, src=None):
            slot = out_refs[a].at[4 * block[0] + 2 * block[1] + block[2]]
            return pltpu.make_async_remote_copy(
                src_ref=slot if src is None else src, dst_ref=slot,
                send_sem=send_sems.at[a * 7 + k], recv_sem=recv_sems.at[a * 7 + k],
                device_id=to, device_id_type=_MESH_ID)

        def own(a):
            return pltpu.make_async_copy(x_refs[a], out_refs[a].at[4 * x + 2 * y + c], local_sems.at[a])

        def first(a):
            return [copy(a, 0, me, sibling, src=x_refs[a])] + [
                copy(a, 1 + j, me, (*chip, c), src=x_refs[a]) for j, chip in enumerate(chips)]

        return me, sibling, chips, c, copy, own, first

    def start(self, x_refs, out_refs, sems):
        _, _, _, _, _, own, first = self._plan(x_refs, out_refs, sems)
        for a in range(len(self.ins)):
            own(a).start()
            for cp in first(a):
                cp.start()

    def finish(self, x_refs, out_refs, sems):
        me, sibling, chips, c, copy, own, first = self._plan(x_refs, out_refs, sems)
        n = len(self.ins)
        passed = []
        for a in range(n):
            for j, chip in enumerate(chips):
                copy(a, 1 + j, (*chip, c), me).wait_recv()
                fwd = copy(a, 4 + j, (*chip, c), sibling)
                fwd.start()
                passed.append(fwd)
        for a in range(n):
            copy(a, 0, sibling, me).wait_recv()
            for j, chip in enumerate(chips):
                copy(a, 4 + j, (*chip, 1 - c), me).wait_recv()
        for a in range(n):
            for cp in first(a):
                cp.wait_send()
        for cp in passed:
            cp.wait_send()
        for a in range(n):
            own(a).wait()


class _Exchange:
    def __init__(self, gs):
        self.ins = list(gs)
        self.outs = [jax.ShapeDtypeStruct(v.shape, v.dtype) for v in gs]
        self.scratch = _sem_scratch(len(gs))

    def _plan(self, g_refs, r_refs, sems):
        send_sems, recv_sems, local_sems = sems
        x, y, c = _mesh_pos()
        me_idx = 4 * x + 2 * y + c
        n = len(self.ins)

        def copy(a, k):
            peer = (_flip(x, k & 4), _flip(y, k & 2), _flip(c, k & 1))
            peer_idx = 4 * peer[0] + 2 * peer[1] + peer[2]
            return pltpu.make_async_remote_copy(
                src_ref=g_refs[a].at[peer_idx], dst_ref=r_refs[a].at[me_idx],
                send_sem=send_sems.at[a * 7 + k - 1], recv_sem=recv_sems.at[a * 7 + k - 1],
                device_id=peer, device_id_type=_MESH_ID)

        copies = [copy(a, k) for a in range(n) for k in range(1, _NDEV)]
        mine = [pltpu.make_async_copy(g_refs[a].at[me_idx], r_refs[a].at[me_idx], local_sems.at[a])
                for a in range(n)]
        return copies, mine

    def start(self, g_refs, r_refs, sems):
        copies, mine = self._plan(g_refs, r_refs, sems)
        for cp in copies + mine:
            cp.start()

    def finish(self, g_refs, r_refs, sems):
        copies, mine = self._plan(g_refs, r_refs, sems)
        for cp in copies:
            cp.wait_recv()
        for cp in copies:
            cp.wait_send()
        for cp in mine:
            cp.wait()


class _Both:
    def __init__(self, first, second):
        self.parts = (first, second)
        self.ins = first.ins + second.ins
        self.outs = first.outs + second.outs
        self.scratch = first.scratch + second.scratch

    def _split(self, ins, outs, sems):
        a, b = self.parts
        na, nb = len(a.ins), len(a.scratch)
        return (a, ins[:na], outs[:na], sems[:nb]), (b, ins[na:], outs[na:], sems[nb:])

    def start(self, ins, outs, sems):
        for part, i, o, s in self._split(ins, outs, sems):
            part.start(i, o, s)

    def finish(self, ins, outs, sems):
        for part, i, o, s in self._split(ins, outs, sems):
            part.finish(i, o, s)


def _comm_call(comm, name):
    def body(*refs):
        n_i, n_o = len(comm.ins), len(comm.outs)
        ins, outs, sems = refs[:n_i], refs[n_i:n_i + n_o], refs[n_i + n_o:]
        comm.start(ins, outs, sems)
        comm.finish(ins, outs, sems)

    return pl.pallas_call(
        body, out_shape=list(comm.outs), in_specs=[_ANY] * len(comm.ins), out_specs=[_ANY] * len(comm.outs),
        scratch_shapes=list(comm.scratch), name=name)(*comm.ins)


def _adamw_math(w, g, m, v):
    m = _ADAM_B1 * m + (1.0 - _ADAM_B1) * g
    v = _ADAM_B2 * v + (1.0 - _ADAM_B2) * (g * g)
    m_hat = m / (1.0 - _ADAM_B1 ** _ADAM_STEP)
    v_hat = v / (1.0 - _ADAM_B2 ** _ADAM_STEP)
    delta = -_ADAM_LR * (m_hat / (jnp.sqrt(v_hat) + _ADAM_EPS) + _ADAM_WD * w)
    return delta, m, v


def _sum_adamw(parts, w, m, v, name):
    r, c = w.shape
    tr = _tile(r, 128)

    def body(p_ref, w_ref, m_ref, v_ref, g_ref, d_ref, nm_ref, nv_ref):
        g = p_ref[0].astype(_F32)
        for j in range(1, _NDEV):
            g = g + p_ref[j].astype(_F32)
        delta, nm, nv = _adamw_math(w_ref[...], g, m_ref[...], v_ref[...])
        g_ref[...] = g
        d_ref[...] = delta
        nm_ref[...] = nm
        nv_ref[...] = nv

    blk = pl.BlockSpec((tr, c), lambda i: (i, 0))
    return pl.pallas_call(
        body, grid=(r // tr,),
        in_specs=[pl.BlockSpec((_NDEV, tr, c), lambda i: (0, i, 0)), blk, blk, blk],
        out_specs=[blk] * 4, out_shape=[jax.ShapeDtypeStruct((r, c), _F32)] * 4,
        compiler_params=_cparams("parallel"), name=name,
    )(parts, w, m, v)


def _sum8(parts, name):
    _, r, c = parts.shape

    def body(p_ref, o_ref):
        g = p_ref[0]
        for j in range(1, _NDEV):
            g = g + p_ref[j]
        o_ref[...] = g

    return pl.pallas_call(
        body, grid=(1,), in_specs=[pl.BlockSpec((_NDEV, r, c), lambda i: (0, 0, 0))],
        out_specs=pl.BlockSpec((r, c), lambda i: (0, 0)), out_shape=jax.ShapeDtypeStruct((r, c), _F32),
        compiler_params=_cparams("arbitrary"), name=name,
    )(parts)


def _adamw_flat(g, w, m, v, name):
    r, c = w.shape

    def body(g_ref, w_ref, m_ref, v_ref, d_ref, nm_ref, nv_ref):
        delta, nm, nv = _adamw_math(w_ref[...], g_ref[...], m_ref[...], v_ref[...])
        d_ref[...] = delta
        nm_ref[...] = nm
        nv_ref[...] = nv

    blk = pl.BlockSpec((r, c), lambda i: (0, 0))
    return pl.pallas_call(
        body, grid=(1,), in_specs=[blk] * 4, out_specs=[blk] * 3,
        out_shape=[jax.ShapeDtypeStruct((r, c), _F32)] * 3,
        compiler_params=_cparams("arbitrary"), name=name,
    )(g, w, m, v)


def _pack(arrs):
    flat = jnp.concatenate([a.reshape(-1).astype(_F32) for a in arrs])
    pad = (-flat.shape[0]) % 1024
    return jnp.pad(flat, (0, pad)).reshape(-1, 128)


def _unpack(flat2d, shapes):
    flat = flat2d.reshape(-1)
    out, off = [], 0
    for sh in shapes:
        size = 1
        for dim in sh:
            size *= dim
        out.append(flat[off:off + size].reshape(sh))
        off += size
    return out


def _block_diag(w):
    h, hd, _ = w.shape
    eye = jnp.eye(h, dtype=w.dtype)
    return (eye[:, None, :, None] * w[:, :, None, :]).reshape(h * hd, h * hd)


def _diag_blocks(full, h):
    hd = full.shape[0] // h
    return jnp.stack([full[i * hd:(i + 1) * hd, i * hd:(i + 1) * hd] for i in range(h)])


def _ungather_cols(g):
    j, k, n = g.shape
    return g.transpose(1, 0, 2).reshape(k, j * n)


def kernel(x, mem, mix_norm_g, w_in, lru_conv_w, lru_conv_b, lru_w_a, lru_b_a, lru_w_x, lru_b_x, lru_lambda, conf_conv_w, conf_conv_b, conf_ln_g, conf_ln_b, w_out, xa_norm_g, mem_norm_g, w_q, w_kv, w_o, ffn_norm_g, w_up, ffn_conv_w, ffn_conv_b, w_down, final_norm_g, loss_target, m_mix_norm_g, m_w_in, m_lru_conv_w, m_lru_conv_b, m_lru_w_a, m_lru_b_a, m_lru_w_x, m_lru_b_x, m_lru_lambda, m_conf_conv_w, m_conf_conv_b, m_conf_ln_g, m_conf_ln_b, m_w_out, m_xa_norm_g, m_mem_norm_g, m_w_q, m_w_kv, m_w_o, m_ffn_norm_g, m_w_up, m_ffn_conv_w, m_ffn_conv_b, m_w_down, m_final_norm_g, v_mix_norm_g, v_w_in, v_lru_conv_w, v_lru_conv_b, v_lru_w_a, v_lru_b_a, v_lru_w_x, v_lru_b_x, v_lru_lambda, v_conf_conv_w, v_conf_conv_b, v_conf_ln_g, v_conf_ln_b, v_w_out, v_xa_norm_g, v_mem_norm_g, v_w_q, v_w_kv, v_w_o, v_ffn_norm_g, v_w_up, v_ffn_conv_w, v_ffn_conv_b, v_w_down, v_final_norm_g):
    names = ["mix_norm_g", "w_in", "lru_conv_w", "lru_conv_b", "lru_w_a", "lru_b_a", "lru_w_x", "lru_b_x",
             "lru_lambda", "conf_conv_w", "conf_conv_b", "conf_ln_g", "conf_ln_b", "w_out", "xa_norm_g",
             "mem_norm_g", "w_q", "w_kv", "w_o", "ffn_norm_g", "w_up", "ffn_conv_w", "ffn_conv_b", "w_down",
             "final_norm_g"]
    loc = locals()
    W = {n: loc[n] for n in names}
    M = {n: loc["m_" + n] for n in names}
    V = {n: loc["v_" + n] for n in names}
    big = ["w_in", "w_out", "w_q", "w_kv", "w_o", "w_up", "w_down"]
    conv_sharded = ["lru_conv_w", "conf_conv_w", "ffn_conv_w"]

    xs = x[0]
    mems = mem[0]
    tgt = loss_target[0]
    me = 4 * lax.axis_index("x") + 2 * lax.axis_index("y") + lax.axis_index("c")

    conv_shapes = [W[n].shape[1:] for n in conv_sharded]
    conv_pack = _pack([W[n][0] for n in conv_sharded])
    shard = {n: W[n][0].astype(_XFER) for n in big}
    h1, (g_in, g_conv) = _rms_fwd(xs, mix_norm_g, "rms1_fwd", comm=_Gather([shard["w_in"], conv_pack]))
    convs = [[] for _ in conv_sharded]
    for j in range(_NDEV):
        for idx, part in enumerate(_unpack(g_conv[j], conv_shapes)):
            convs[idx].append(part)
    lcw, ccw, fcw = [jnp.concatenate(parts, axis=-1) for parts in convs]

    wab = jnp.concatenate([_block_diag(lru_w_a[0]), _block_diag(lru_w_x[0])], axis=1).astype(_MXU)
    mixer_params = (lcw, lru_conv_b, wab, lru_b_a, lru_b_x, lru_lambda, ccw, conf_conv_b, conf_ln_g, conf_ln_b)

    z, (g_out, g_q, g_kv, g_o) = _mm_nn_stacked(
        h1, g_in, _F32, "mm_in_fwd", comm=_Gather([shard[n] for n in ("w_out", "w_q", "w_kv", "w_o")]))
    (ycat, hs, cc), (g_up, g_down) = _mixer_fwd(
        z, *mixer_params, "mixer_fwd", comm=_Gather([shard["w_up"], shard["w_down"]]))
    w_out_f = g_out.reshape(-1, g_out.shape[-1])
    w_q_f = g_q.reshape(-1, g_q.shape[-1])
    w_o_f = g_o.reshape(-1, g_o.shape[-1])
    w_down_f = g_down.reshape(-1, g_down.shape[-1])
    row32, row16, vec32 = (_F32, "row"), (_MXU, "row"), (_F32, "vec")
    x1, h2 = _mm_nn_nat(ycat, w_out_f, "mm_out_fwd", epi=_epi_residual_rms,
                        extra=[(xs, "row"), (xa_norm_g, "vec")], outs=[row32, row16])
    mn = _rms_fwd(mems, mem_norm_g, "rmsm_fwd")
    q = _mm_nn_nat(h2, w_q_f, "mm_q_fwd", outs=[row16])
    kv = _mm_nn_stacked(mn, g_kv, _MXU, "mm_kv_fwd")
    o = _attn_fwd(q, kv, "attn_fwd")
    x2, h3 = _mm_nn_nat(o, w_o_f, "mm_o_fwd", epi=_epi_residual_rms,
                        extra=[(x1, "row"), (ffn_norm_g, "vec")], outs=[row32, row16])

    gfin = final_norm_g.reshape(1, -1)
    gu, act, dx3, dx3b, lvec, dg_final = _ffn_fused_fwd(
        h3, g_up, w_down_f, fcw, ffn_conv_b, x2, tgt, gfin, "ffn_fwd")
    loss_local = 0.5 * jnp.sum(lvec) / xs.shape[1]
    loss = lax.psum(loss_local, _AXES)

    def rows8(p):
        return p.reshape(_NDEV, p.shape[0] // _NDEV, p.shape[1])

    p_down = _mm_tn_nat(act, dx3b, _XFER, "mm_down_wgrad", ts=2048)
    (dgu, dx2, dx2b, dg_ffn, dfcw, dfcb), (r_down,) = _ffn_fused_bwd(
        dx3b, gu, w_down_f, g_up, fcw, ffn_conv_b, x2, ffn_norm_g, dx3, "ffn_bwd",
        comm=_Exchange([rows8(p_down)]))
    p_up = _mm_tn_stacked(h3, dgu, _NDEV, _XFER, "mm_up_wgrad", slabs=2)

    do = _mm_nt_nat(dx2b, w_o_f, "mm_o_dgrad", outs=[row16])
    p_o = _mm_tn_nat(o, dx2b, _XFER, "mm_o_wgrad", ts=2048)
    dq, dk, dv = _attn_bwd(q, kv, do, "attn_bwd")
    dx1, dx1b, dg_xa = _mm_nt_nat(
        dq, w_q_f, "mm_q_dgrad", epi=_epi_rms_bwd, extra=[(x1, "row"), (xa_norm_g, "vec"), (dx2, "row")],
        outs=[row32, row16, vec32], tm=512)
    p_q = _mm_tn_nat(h2, dq, _XFER, "mm_q_wgrad", ts=2048)
    dkv = jnp.concatenate([dk, dv], axis=1).astype(_MXU)
    dmn = _mm_nt_stacked(dkv, g_kv, "mm_kv_dgrad", outs=[row32], slabs=_NDEV)
    p_kv = _mm_tn_stacked(mn, dkv, _NDEV, _XFER, "mm_kv_wgrad", slabs=_NDEV)
    _, _, dg_mem = _rms_bwd(dmn, mems, mem_norm_g, None, "rmsm_bwd")

    dycat = _mm_nt_nat(dx1b, w_out_f, "mm_out_dgrad", outs=[row32])
    p_out = _mm_tn_nat(ycat, dx1b, _XFER, "mm_out_wgrad", ts=2048)
    ((dz, dlcw, dlcb, dwab, dba, dbx, dlam, dccw, dccb, dlng, dlnb), (r_up, r_o, r_q, r_kv, r_out)) = _mixer_bwd(
        dycat, z, hs, cc, *mixer_params, "mixer_bwd",
        comm=_Exchange([p_up, rows8(p_o), rows8(p_q), p_kv, rows8(p_out)]))
    p_in = _mm_tn_stacked(h1, dz, _NDEV, _XFER, "mm_in_wgrad", slabs=_NDEV)

    c = _D_LRU
    heads = lru_w_a.shape[1]
    small_partial = {
        "lru_conv_w": dlcw, "lru_conv_b": dlcb,
        "lru_w_a": _diag_blocks(dwab[:, :c], heads), "lru_b_a": dba,
        "lru_w_x": _diag_blocks(dwab[:, c:], heads), "lru_b_x": dbx, "lru_lambda": dlam,
        "conf_conv_w": dccw, "conf_conv_b": dccb, "conf_ln_g": dlng, "conf_ln_b": dlnb,
        "xa_norm_g": dg_xa, "mem_norm_g": dg_mem, "ffn_norm_g": dg_ffn,
        "ffn_conv_w": dfcw, "ffn_conv_b": dfcb, "final_norm_g": dg_final,
    }
    early = list(small_partial)
    early_shapes = [small_partial[n].shape for n in early]
    (grad_x, dg_mix), (r_in, early_all) = _mm_nt_stacked(
        dz, g_in, "mm_in_dgrad", epi=_epi_rms_bwd, extra=[(xs, "row"), (mix_norm_g, "vec"), (dx1, "row")],
        outs=[row32, vec32], tm=512, slabs=_NDEV,
        comm=_Both(_Exchange([p_in]), _Gather([_pack([small_partial[n] for n in early])])))
    (mix_all,) = _comm_call(_Gather([dg_mix]), "gather_mix_grad")
    small = early + ["mix_norm_g"]
    small_sum = _unpack(_sum8(early_all, "sum_small_grads"), early_shapes)
    small_sum.append(_sum8(mix_all.reshape(_NDEV, 8, -1), "sum_mix_grad").reshape(dg_mix.shape))
    received = {"w_in": r_in, "w_out": r_out, "w_q": r_q, "w_kv": r_kv, "w_o": r_o, "w_up": r_up,
                "w_down": r_down}

    grads, deltas, new_m, new_v = {}, {}, {}, {}
    for n, rec in ((n, received[n]) for n in big):
        shp = W[n].shape
        w2, m2, v2 = (t.reshape(rec.shape[1:]) for t in (W[n], M[n], V[n]))
        outs = _sum_adamw(rec, w2, m2, v2, "adamw_" + n)
        grads[n], deltas[n], new_m[n], new_v[n] = (t.reshape(shp) for t in outs)

    small_g = []
    for n, g in zip(small, small_sum):
        if n in conv_sharded:
            width = W[n].shape[-1]
            g = lax.dynamic_slice_in_dim(g, me * width, width, axis=1)
        small_g.append(g.reshape(W[n].shape))
    small_shapes = [W[n].shape for n in small]
    sd, sm, sv = _adamw_flat(_pack(small_g), _pack([W[n] for n in small]), _pack([M[n] for n in small]),
                             _pack([V[n] for n in small]), "adamw_small")
    for n, g, d_, m_, v_ in zip(small, small_g, _unpack(sd, small_shapes), _unpack(sm, small_shapes),
                                _unpack(sv, small_shapes)):
        grads[n], deltas[n], new_m[n], new_v[n] = g, d_, m_, v_

    return (loss, grad_x[None], *[grads[n] for n in names], *[deltas[n] for n in names],
            *[new_m[n] for n in names], *[new_v[n] for n in names])
```

```python
import functools

import jax
import jax.numpy as jnp
from jax import lax
from jax.experimental import pallas as pl
from jax.experimental.pallas import tpu as pltpu

_MXU = jnp.bfloat16
_XFER = jnp.bfloat16
_F32 = jnp.float32
_EPS = 1e-6
_NDEV = 8
_AXES = ("x", "y", "c")
_VMEM_LIMIT = 48 * 1024 * 1024

_D_LRU = 512
_XA_HEADS = 4
_RG_C = 8.0
_ADAM_LR, _ADAM_B1, _ADAM_B2, _ADAM_EPS, _ADAM_WD, _ADAM_STEP = 0.001, 0.9, 0.999, 1e-08, 0.01, 10

_MESH_ID = pl.DeviceIdType.MESH
_ANY = pl.BlockSpec(memory_space=pl.ANY)


def _cparams(*sem, vmem=_VMEM_LIMIT):
    return pltpu.CompilerParams(dimension_semantics=tuple(sem), vmem_limit_bytes=vmem)


def _pcall(body, *, args, grid, in_specs, out_specs, out_shape, sem, name, scratch_shapes=(), comm=None,
           vmem=_VMEM_LIMIT):
    outs_l = list(out_shape) if isinstance(out_shape, (list, tuple)) else [out_shape]
    ospecs_l = list(out_specs) if isinstance(out_specs, (list, tuple)) else [out_specs]
    n_in, n_out, n_scr = len(args), len(outs_l), len(scratch_shapes)
    if comm is None:
        res = pl.pallas_call(
            body, grid=grid, in_specs=list(in_specs), out_specs=ospecs_l, out_shape=outs_l,
            scratch_shapes=list(scratch_shapes), compiler_params=_cparams(*sem, vmem=vmem), name=name)(*args)
        return list(res), []
    n_ci, n_co = len(comm.ins), len(comm.outs)

    def wrapped(*refs):
        ins, cins = refs[:n_in], refs[n_in:n_in + n_ci]
        o = n_in + n_ci
        outs, couts = refs[o:o + n_out], refs[o + n_out:o + n_out + n_co]
        s = o + n_out + n_co
        scr, cscr = refs[s:s + n_scr], refs[s + n_scr:]
        first = pl.program_id(0) == 0
        last = pl.program_id(0) == grid[0] - 1
        for ax in range(1, len(grid)):
            first = jnp.logical_and(first, pl.program_id(ax) == 0)
            last = jnp.logical_and(last, pl.program_id(ax) == grid[ax] - 1)

        @pl.when(first)
        def _():
            comm.start(cins, couts, cscr)

        body(*ins, *outs, *scr)

        @pl.when(last)
        def _():
            comm.finish(cins, couts, cscr)

    res = pl.pallas_call(
        wrapped, grid=grid, in_specs=list(in_specs) + [_ANY] * n_ci, out_specs=ospecs_l + [_ANY] * n_co,
        out_shape=outs_l + list(comm.outs), scratch_shapes=list(scratch_shapes) + list(comm.scratch),
        compiler_params=_cparams(*(("arbitrary",) * len(grid)), vmem=vmem), name=name)(*args, *comm.ins)
    return list(res[:n_out]), list(res[n_out:])


def _sigmoid(v):
    return 1.0 / (1.0 + jnp.exp(-v))


_GELU_C = 0.7978845608028654
_GELU_K = 0.044715


def _gelu(v):
    t = jnp.tanh(_GELU_C * (v + _GELU_K * v * v * v))
    return 0.5 * v * (1.0 + t)


def _gelu_and_grad(v):
    v2 = v * v
    t = jnp.tanh(_GELU_C * (v + _GELU_K * v2 * v))
    g = 0.5 * v * (1.0 + t)
    dg = 0.5 * (1.0 + t) + 0.5 * v * (1.0 - t * t) * (_GELU_C * (1.0 + 3.0 * _GELU_K * v2))
    return g, dg


def _softplus(v):
    e = jnp.exp(-jnp.abs(v))
    log1p = jnp.where(e < 1e-2, e * (1.0 - e * (0.5 - e * (1.0 / 3.0))), jnp.log(1.0 + e))
    return jnp.maximum(v, 0.0) + log1p


def _neg_expm1(v):
    series = -v * (1.0 + v * (0.5 + v * ((1.0 / 6.0) + v * (1.0 / 24.0))))
    return jnp.where(v > -0.0625, series, 1.0 - jnp.exp(v))


def _dot(a, b, dims):
    return lax.dot_general(a.astype(_MXU), b.astype(_MXU), (dims, ((), ())), preferred_element_type=_F32)


_NN = ((1,), (0,))
_NT = ((1,), (1,))
_TN = ((0,), (0,))


def _scan_fwd(a, b, rows):
    n = a.shape[0]
    d = 1
    while d < n:
        keep = rows >= d
        b = jnp.where(keep, b + a * pltpu.roll(b, d, 0), b)
        a = jnp.where(keep, a * pltpu.roll(a, d, 0), a)
        d *= 2
    return a, b


def _scan_rev(a, b, rows):
    n = a.shape[0]
    d = 1
    while d < n:
        keep = rows < n - d
        b = jnp.where(keep, b + a * pltpu.roll(b, n - d, 0), b)
        a = jnp.where(keep, a * pltpu.roll(a, n - d, 0), a)
        d *= 2
    return a, b


def _colsum(v):
    return jnp.sum(v, axis=0, keepdims=True)


def _mm(a, b, *, dims, grid, a_spec, b_spec, outs, acc_shape, name, extra=(), epi=None, slabs=1, comm=None):
    nred = grid[-1]
    red_axis = len(grid) - 1
    n_ex, n_out = len(extra), len(outs)
    epi = _epi_store if epi is None else epi

    def body(*refs):
        a_ref, b_ref = refs[:2]
        ex, o_refs, acc_ref = refs[2:2 + n_ex], refs[2 + n_ex:2 + n_ex + n_out], refs[-1]
        if slabs == 1:
            p = _dot(a_ref[...], b_ref[...], dims)
        else:
            n = b_ref.shape[-1]
            p = _dot(a_ref[:, 0:n], b_ref[0], dims)
            for jj in range(1, slabs):
                p = p + _dot(a_ref[:, jj * n:(jj + 1) * n], b_ref[jj], dims)

        first_rows = pl.program_id(0) == 0
        if nred == 1:
            epi(p, ex, o_refs, first_rows)
        else:
            k = pl.program_id(red_axis)

            @pl.when(k == 0)
            def _():
                acc_ref[...] = p

            @pl.when(jnp.logical_and(k > 0, k < nred - 1))
            def _():
                acc_ref[...] += p

            @pl.when(k == nred - 1)
            def _():
                epi(acc_ref[...] + p, ex, o_refs, first_rows)

    sem = ("parallel",) * (len(grid) - 1) + ("arbitrary",)
    if any(o[0].shape[0] == 1 for o in outs):
        sem = ("arbitrary",) * len(grid)
    res, cres = _pcall(
        body, args=(a, b) + tuple(e[0] for e in extra), grid=grid,
        in_specs=[a_spec, b_spec] + [e[1] for e in extra],
        out_specs=[o[1] for o in outs], out_shape=[o[0] for o in outs],
        scratch_shapes=[pltpu.VMEM(acc_shape if nred > 1 else (8, 128), _F32)], sem=sem, name=name, comm=comm)
    res = res[0] if n_out == 1 else res
    return res if comm is None else (res, cres)


def _epi_store(total, ex, outs, first_rows):
    outs[0][...] = total.astype(outs[0].dtype)


def _epi_residual_rms(total, ex, outs, first_rows):
    res_ref, g_ref = ex
    xn = total + res_ref[...]
    outs[0][...] = xn
    r = lax.rsqrt(jnp.mean(xn * xn, axis=-1, keepdims=True) + _EPS)
    outs[1][...] = (xn * r * g_ref[...]).astype(outs[1].dtype)


def _epi_rms_bwd(total, ex, outs, first_rows):
    x_ref, g_ref, dres_ref = ex
    dg_ref = outs[-1]
    xv = x_ref[...]
    r = lax.rsqrt(jnp.mean(xv * xv, axis=-1, keepdims=True) + _EPS)
    xhat = xv * r
    dxh = total * g_ref[...]
    dx = dres_ref[...] + r * (dxh - xhat * jnp.mean(dxh * xhat, axis=-1, keepdims=True))
    for o_ref in outs[:-1]:
        o_ref[...] = dx.astype(o_ref.dtype)

    @pl.when(first_rows)
    def _():
        dg_ref[...] = jnp.zeros_like(dg_ref)

    dg_ref[...] += _colsum(total * xhat)


def _epi_final(total, ex, outs, first_rows):
    res_ref, t_ref, g_ref = ex
    dx_ref, dxb_ref, l_ref, dg_ref = outs
    xv = total + res_ref[...]
    gv = g_ref[...]
    d = xv.shape[-1]
    r = lax.rsqrt(jnp.mean(xv * xv, axis=-1, keepdims=True) + _EPS)
    xhat = xv * r
    err = xhat * gv - t_ref[...]
    dy = err * (1.0 / d)
    dxh = dy * gv
    dx = r * (dxh - xhat * jnp.mean(dxh * xhat, axis=-1, keepdims=True))
    dx_ref[...] = dx
    dxb_ref[...] = dx.astype(dxb_ref.dtype)

    @pl.when(first_rows)
    def _():
        l_ref[...] = jnp.zeros_like(l_ref)
        dg_ref[...] = jnp.zeros_like(dg_ref)

    l_ref[...] += _colsum(err * err)
    dg_ref[...] += _colsum(dy * xhat)


def _tile(m, cap):
    t = min(m, cap)
    assert m % t == 0
    return t


def _row_spec(tm, n):
    return pl.BlockSpec((tm, n), lambda i, *_: (i, 0))


def _vec_spec(n):
    return pl.BlockSpec((1, n), lambda *_: (0, 0))


def _row_io(m, n, tm, extra, outs):
    def spec(kind):
        return _row_spec(tm, n) if kind == "row" else _vec_spec(n)

    ex = [(arr, spec(kind)) for arr, kind in extra]
    os_ = [(jax.ShapeDtypeStruct((m, n) if kind == "row" else (1, n), dt), spec(kind)) for dt, kind in outs]
    return ex, os_


def _mm_nn_stacked(a, w, out_dtype, name, comm=None, tm=1024):
    m, k = a.shape
    j, _, n = w.shape
    tm = _tile(m, tm)
    return _mm(a, w, dims=_NN, grid=(m // tm, j, 1),
               a_spec=pl.BlockSpec((tm, k), lambda i, jj, r: (i, 0)),
               b_spec=pl.BlockSpec((None, k, n), lambda i, jj, r: (jj, 0, 0)),
               outs=[(jax.ShapeDtypeStruct((m, j * n), out_dtype), pl.BlockSpec((tm, n), lambda i, jj, r: (i, jj)))],
               acc_shape=(tm, n), name=name, comm=comm)


def _mm_nn_nat(a, w, name, *, outs, extra=(), epi=None, tm=1024):
    m, kt = a.shape
    _, n = w.shape
    tm = _tile(m, tm)
    tk = _tile(kt, 1024)
    ex, os_ = _row_io(m, n, tm, extra, outs)
    return _mm(a, w, dims=_NN, grid=(m // tm, kt // tk),
               a_spec=pl.BlockSpec((tm, tk), lambda i, r: (i, r)),
               b_spec=pl.BlockSpec((tk, n), lambda i, r: (r, 0)),
               outs=os_, extra=ex, epi=epi, acc_shape=(tm, n), name=name)


def _mm_nt_stacked(dc, w, name, *, outs, extra=(), epi=None, comm=None, tm=1024, slabs=1):
    m = dc.shape[0]
    j, k, n = w.shape
    tm = _tile(m, tm)
    assert j % slabs == 0
    ex, os_ = _row_io(m, k, tm, extra, outs)
    wblk = (None, k, n) if slabs == 1 else (slabs, k, n)
    return _mm(dc, w, dims=_NT, grid=(m // tm, j // slabs),
               a_spec=pl.BlockSpec((tm, slabs * n), lambda i, r: (i, r)),
               b_spec=pl.BlockSpec(wblk, lambda i, r: (r, 0, 0)),
               outs=os_, extra=ex, epi=epi, acc_shape=(tm, k), name=name, slabs=slabs, comm=comm)


def _mm_nt_nat(dc, w, name, *, outs, extra=(), epi=None, tm=1024):
    m, n = dc.shape
    kt = w.shape[0]
    tm = _tile(m, tm)
    tkb = _tile(kt, 1024)
    if kt == tkb:
        ex, os_ = _row_io(m, kt, tm, extra, outs)
    else:
        assert not extra and len(outs) == 1
        ex, os_ = [], [(jax.ShapeDtypeStruct((m, kt), outs[0][0]), pl.BlockSpec((tm, tkb), lambda i, kb, r: (i, kb)))]
    return _mm(dc, w, dims=_NT, grid=(m // tm, kt // tkb, 1),
               a_spec=pl.BlockSpec((tm, n), lambda i, kb, r: (i, 0)),
               b_spec=pl.BlockSpec((tkb, n), lambda i, kb, r: (kb, 0)),
               outs=os_, extra=ex, epi=epi, acc_shape=(tm, tkb), name=name)


def _mm_tn_stacked(a, dc, j, out_dtype, name, slabs=1, ts=1024):
    s, k = a.shape
    n = dc.shape[1] // j
    ts = _tile(s, ts)
    assert j % slabs == 0

    def epi(total, ex, outs, first_rows):
        for jj in range(slabs):
            outs[0][jj] = total[:, jj * n:(jj + 1) * n].astype(outs[0].dtype)

    return _mm(a, dc, dims=_TN, grid=(j // slabs, s // ts),
               a_spec=pl.BlockSpec((ts, k), lambda jj, r: (r, 0)),
               b_spec=pl.BlockSpec((ts, slabs * n), lambda jj, r: (r, jj)),
               outs=[(jax.ShapeDtypeStruct((j, k, n), out_dtype),
                      pl.BlockSpec((slabs, k, n), lambda jj, r: (jj, 0, 0)))],
               epi=epi, acc_shape=(k, slabs * n), name=name)


def _mm_tn_nat(a, dc, out_dtype, name, ts=1024):
    s, kt = a.shape
    n = dc.shape[1]
    ts = _tile(s, ts)
    tkb = _tile(kt, 512)
    return _mm(a, dc, dims=_TN, grid=(kt // tkb, s // ts),
               a_spec=pl.BlockSpec((ts, tkb), lambda kb, r: (r, kb)),
               b_spec=pl.BlockSpec((ts, n), lambda kb, r: (r, 0)),
               outs=[(jax.ShapeDtypeStruct((kt, n), out_dtype), pl.BlockSpec((tkb, n), lambda kb, r: (kb, 0)))],
               acc_shape=(tkb, n), name=name)


def _rms_fwd(x, g, name, comm=None):
    s, d = x.shape
    t = _tile(s, 256)

    def body(x_ref, g_ref, h_ref):
        xv = x_ref[...]
        r = lax.rsqrt(jnp.mean(xv * xv, axis=-1, keepdims=True) + _EPS)
        h_ref[...] = (xv * r * g_ref[...]).astype(h_ref.dtype)

    res, cres = _pcall(
        body, args=(x, g), grid=(s // t,),
        in_specs=[pl.BlockSpec((t, d), lambda i: (i, 0)), pl.BlockSpec((1, d), lambda i: (0, 0))],
        out_specs=pl.BlockSpec((t, d), lambda i: (i, 0)),
        out_shape=jax.ShapeDtypeStruct((s, d), _MXU), sem=("parallel",), name=name, comm=comm)
    return res[0] if comm is None else (res[0], cres)


def _rms_bwd(dh, x, g, dres, name):
    s, d = x.shape
    t = _tile(s, 256)
    has_res = dres is not None

    def body(*refs):
        if has_res:
            dh_ref, x_ref, g_ref, dres_ref, dx_ref, dxb_ref, dg_ref = refs
        else:
            dh_ref, x_ref, g_ref, dx_ref, dxb_ref, dg_ref = refs
        xv = x_ref[...]
        dhv = dh_ref[...]
        r = lax.rsqrt(jnp.mean(xv * xv, axis=-1, keepdims=True) + _EPS)
        xhat = xv * r
        dxh = dhv * g_ref[...]
        dx = r * (dxh - xhat * jnp.mean(dxh * xhat, axis=-1, keepdims=True))
        if has_res:
            dx = dx + dres_ref[...]
        dx_ref[...] = dx
        dxb_ref[...] = dx.astype(dxb_ref.dtype)

        @pl.when(pl.program_id(0) == 0)
        def _():
            dg_ref[...] = jnp.zeros_like(dg_ref)

        dg_ref[...] += _colsum(dhv * xhat)

    row = pl.BlockSpec((t, d), lambda i: (i, 0))
    vec = pl.BlockSpec((1, d), lambda i: (0, 0))
    in_specs = [row, row, vec] + ([row] if has_res else [])
    args = (dh, x, g) + ((dres,) if has_res else ())
    return pl.pallas_call(
        body, grid=(s // t,), in_specs=in_specs, out_specs=[row, row, vec],
        out_shape=[jax.ShapeDtypeStruct((s, d), _F32), jax.ShapeDtypeStruct((s, d), _MXU),
                   jax.ShapeDtypeStruct((1, d), _F32)],
        compiler_params=_cparams("arbitrary"), name=name,
    )(*args)


_LRU_K = 4
_CONF_K = 31
_LRU_HALO = 8
_CONF_HALO = 32
_MIX_T = 256


def _lru_gates(lx, wab_ref, ba_ref, bx_ref, lam_ref):
    c = _D_LRU
    pre = _dot(lx, wab_ref[...], _NN)
    r = _sigmoid(pre[:, :c] + ba_ref[...])
    ig = _sigmoid(pre[:, c:] + bx_ref[...])
    sp = _softplus(-lam_ref[...])
    log_a = (-_RG_C) * r * sp
    a = jnp.exp(log_a)
    mult = jnp.sqrt(_neg_expm1(2.0 * log_a))
    return r, ig, sp, a, mult


def _causal_conv(ext_ref, halo, w_ref, b_ref, taps, t):
    acc = b_ref[...] + w_ref[0:1, :] * ext_ref[pl.ds(halo - (taps - 1), t), :]
    for k in range(1, taps):
        acc = acc + w_ref[k:k + 1, :] * ext_ref[pl.ds(halo - (taps - 1) + k, t), :]
    return acc


class _Windows:
    def __init__(self, ext_ref, shifted_ref, t):
        self.ext_ref, self.shifted_ref, self.t = ext_ref, shifted_ref, t
        rows = t + 24
        for r in range(1, 8):
            shifted_ref[r - 1, 0:rows, :] = ext_ref[pl.ds(r, rows), :]

    def __call__(self, off):
        q, r = divmod(off, 8)
        if r == 0:
            return self.ext_ref[pl.ds(8 * q, self.t), :]
        return self.shifted_ref[r - 1, pl.ds(8 * q, self.t), :]


def _mixer_fwd(z, lcw, lcb, wab, ba, bx, lam, ccw, ccb, lng, lnb, name, comm=None):
    s = z.shape[0]
    c = _D_LRU
    t = _tile(s, _MIX_T)
    nt = s // t

    def body(lx0_ref, lx0h_ref, gate_ref, ca_ref, cah_ref, cb_ref, cbh_ref,
             lcw_ref, lcb_ref, wab_ref, ba_ref, bx_ref, lam_ref, ccw_ref, ccb_ref, lng_ref, lnb_ref,
             ycat_ref, hs_ref, cc_ref, ext_ref, cge_ref, hc_ref, shifted_ref):
        i = pl.program_id(0)
        first = i == 0
        rows = lax.broadcasted_iota(jnp.int32, (t, c), 0)

        ext_ref[0:_LRU_HALO, :] = jnp.where(first, 0.0, lx0h_ref[...])
        ext_ref[_LRU_HALO:_LRU_HALO + t, :] = lx0_ref[...]
        lx = _causal_conv(ext_ref, _LRU_HALO, lcw_ref, lcb_ref, _LRU_K, t)
        r, ig, sp, a, mult = _lru_gates(lx, wab_ref, ba_ref, bx_ref, lam_ref)
        u = mult * (ig * lx)
        a_cum, h_loc = _scan_fwd(a, u, rows)

        @pl.when(first)
        def _():
            hc_ref[...] = jnp.zeros_like(hc_ref)

        h = h_loc + a_cum * hc_ref[7:8, :]
        hs_ref[...] = h
        hc_ref[...] = hs_ref[pl.ds(t - 8, 8), :]
        ycat_ref[:, 0:c] = (h * _gelu(gate_ref[...])).astype(ycat_ref.dtype)

        cge_ref[0:_CONF_HALO, :] = jnp.where(first, 0.0, cah_ref[...] * _sigmoid(cbh_ref[...]))
        cge_ref[_CONF_HALO:_CONF_HALO + t, :] = ca_ref[...] * _sigmoid(cb_ref[...])
        win = _Windows(cge_ref, shifted_ref, t)
        first_off = _CONF_HALO - (_CONF_K - 1)
        cc = ccb_ref[...] + ccw_ref[0:1, :] * win(first_off)
        for k in range(1, _CONF_K):
            cc = cc + ccw_ref[k:k + 1, :] * win(first_off + k)
        cc_ref[...] = cc
        xc = cc - jnp.mean(cc, axis=-1, keepdims=True)
        rstd = lax.rsqrt(jnp.mean(xc * xc, axis=-1, keepdims=True) + _EPS)
        ln = xc * rstd * lng_ref[...] + lnb_ref[...]
        ycat_ref[:, c:2 * c] = (ln * _sigmoid(ln)).astype(ycat_ref.dtype)

    def col(j):
        return pl.BlockSpec((t, c), lambda i: (i, j))

    def halo(j, rows_):
        per = t // rows_
        return pl.BlockSpec((rows_, c), lambda i: (jnp.maximum(i * per - 1, 0), j))

    def full(arr):
        return pl.BlockSpec(arr.shape, lambda i: (0,) * arr.ndim)

    params = (lcw, lcb, wab, ba, bx, lam, ccw, ccb, lng, lnb)
    res, cres = _pcall(
        body, args=(z, z, z, z, z, z, z, *params), grid=(nt,),
        in_specs=[col(0), halo(0, _LRU_HALO), col(1), col(2), halo(2, _CONF_HALO), col(3), halo(3, _CONF_HALO)]
        + [full(p) for p in params],
        out_specs=[pl.BlockSpec((t, 2 * c), lambda i: (i, 0)), pl.BlockSpec((t, c), lambda i: (i, 0)),
                   pl.BlockSpec((t, c), lambda i: (i, 0))],
        out_shape=[jax.ShapeDtypeStruct((s, 2 * c), _MXU), jax.ShapeDtypeStruct((s, c), _F32),
                   jax.ShapeDtypeStruct((s, c), _F32)],
        scratch_shapes=[pltpu.VMEM((t + _LRU_HALO, c), _F32), pltpu.VMEM((t + _CONF_HALO, c), _F32),
                        pltpu.VMEM((8, c), _F32), pltpu.VMEM((7, t + _CONF_HALO, c), _F32)],
        sem=("arbitrary",), name=name, comm=comm)
    return res, cres


def _mixer_bwd(dycat, z, hs, cc, lcw, lcb, wab, ba, bx, lam, ccw, ccb, lng, lnb, name, comm=None):
    s = z.shape[0]
    c = _D_LRU
    t = _tile(s, _MIX_T)
    nt = s // t

    def body(dyl_ref, dc_ref, lx0_ref, lx0h_ref, gate_ref, ca_ref, cah_ref, cb_ref, cbh_ref,
             hs_ref, hsh_ref, cc_ref,
             lcw_ref, lcb_ref, wab_ref, ba_ref, bx_ref, lam_ref, ccw_ref, ccb_ref, lng_ref, lnb_ref,
             dz_ref, dlcw_ref, dlcb_ref, dwab_ref, dba_ref, dbx_ref, dlam_ref, dccw_ref, dccb_ref, dlng_ref,
             dlnb_ref,
             ext_ref, up_ref, cge_ref, dce_ref, xc_ref, dlxc_ref, dccc_ref, shifted_ref):
        i = pl.program_id(0)
        first_tile = i == nt - 1
        last_tile = i == 0
        rows = lax.broadcasted_iota(jnp.int32, (t, c), 0)

        @pl.when(last_tile)
        def _():
            for ref in (dlcw_ref, dlcb_ref, dwab_ref, dba_ref, dbx_ref, dlam_ref, dccw_ref, dccb_ref, dlng_ref,
                        dlnb_ref, xc_ref, dlxc_ref, dccc_ref):
                ref[...] = jnp.zeros_like(ref)

        ext_ref[0:_LRU_HALO, :] = jnp.where(first_tile, 0.0, lx0h_ref[...])
        ext_ref[_LRU_HALO:_LRU_HALO + t, :] = lx0_ref[...]
        lx = _causal_conv(ext_ref, _LRU_HALO, lcw_ref, lcb_ref, _LRU_K, t)
        r, ig, sp, a, mult = _lru_gates(lx, wab_ref, ba_ref, bx_ref, lam_ref)
        h = hs_ref[...]
        gl, dgl = _gelu_and_grad(gate_ref[...])
        dyl = dyl_ref[...]
        dz_ref[:, c:2 * c] = (dyl * h * dgl).astype(dz_ref.dtype)
        dh = dyl * gl

        up_ref[0:t, :] = a
        up_ref[t:t + 8, :] = jnp.ones((8, c), _F32)
        a_up = up_ref[pl.ds(1, t), :]
        a_cum, g_loc = _scan_rev(a_up, dh, rows)
        gt = g_loc + a_cum * xc_ref[0:1, :]
        xc_ref[...] = (a * gt)[0:8, :]

        up_ref[0:8, :] = jnp.where(first_tile, 0.0, hsh_ref[...])
        up_ref[8:8 + t, :] = h
        hprev = up_ref[pl.ds(7, t), :]

        da = gt * hprev
        dmult = gt * ig * lx
        dig = gt * mult * lx
        dlx = gt * mult * ig
        dlog_a = da * a - dmult * a * a / mult
        dpre_r = dlog_a * (-_RG_C) * sp * r * (1.0 - r)
        dpre_i = dig * ig * (1.0 - ig)
        dlam_ref[...] += _colsum(dlog_a * r) * (_RG_C * _sigmoid(-lam_ref[...]))
        dba_ref[...] += _colsum(dpre_r)
        dbx_ref[...] += _colsum(dpre_i)
        dpre = jnp.concatenate([dpre_r, dpre_i], axis=1).astype(_MXU)
        dlx = dlx + _dot(dpre, wab_ref[...], _NT)
        dwab_ref[...] += _dot(lx, dpre, _TN)

        dlcb_ref[...] += _colsum(dlx)
        up_ref[0:t, :] = dlx
        up_ref[t:t + 8, :] = dlxc_ref[...]
        dlxc_ref[...] = dlx[0:8, :]
        acc = lcw_ref[0:1, :] * up_ref[pl.ds(_LRU_K - 1, t), :]
        for k in range(1, _LRU_K):
            acc = acc + lcw_ref[k:k + 1, :] * up_ref[pl.ds(_LRU_K - 1 - k, t), :]
        dz_ref[:, 0:c] = acc.astype(dz_ref.dtype)
        for k in range(_LRU_K):
            dlcw_ref[k:k + 1, :] += _colsum(dlx * ext_ref[pl.ds(_LRU_HALO - (_LRU_K - 1) + k, t), :])

        sig_b = _sigmoid(cb_ref[...])
        ca = ca_ref[...]
        cge_ref[0:_CONF_HALO, :] = jnp.where(first_tile, 0.0, cah_ref[...] * _sigmoid(cbh_ref[...]))
        cge_ref[_CONF_HALO:_CONF_HALO + t, :] = ca * sig_b
        ccv = cc_ref[...]
        xcen = ccv - jnp.mean(ccv, axis=-1, keepdims=True)
        rstd = lax.rsqrt(jnp.mean(xcen * xcen, axis=-1, keepdims=True) + _EPS)
        xn = xcen * rstd
        ln = xn * lng_ref[...] + lnb_ref[...]
        sg = _sigmoid(ln)
        dln = dc_ref[...] * (sg * (1.0 + ln * (1.0 - sg)))
        dlng_ref[...] += _colsum(dln * xn)
        dlnb_ref[...] += _colsum(dln)
        dxn = dln * lng_ref[...]
        dcc = rstd * (dxn - jnp.mean(dxn, axis=-1, keepdims=True)
                      - xn * jnp.mean(dxn * xn, axis=-1, keepdims=True))
        dccb_ref[...] += _colsum(dcc)
        win = _Windows(cge_ref, shifted_ref, t)
        for k in range(_CONF_K):
            dccw_ref[k:k + 1, :] += _colsum(dcc * win(_CONF_HALO - (_CONF_K - 1) + k))
        dce_ref[0:t, :] = dcc
        dce_ref[t:t + _CONF_HALO, :] = dccc_ref[...]
        dccc_ref[...] = dcc[0:_CONF_HALO, :]
        win = _Windows(dce_ref, shifted_ref, t)
        dcg = ccw_ref[0:1, :] * win(_CONF_K - 1)
        for k in range(1, _CONF_K):
            dcg = dcg + ccw_ref[k:k + 1, :] * win(_CONF_K - 1 - k)
        dz_ref[:, 2 * c:3 * c] = (dcg * sig_b).astype(dz_ref.dtype)
        dz_ref[:, 3 * c:4 * c] = (dcg * ca * sig_b * (1.0 - sig_b)).astype(dz_ref.dtype)

    def col(j):
        return pl.BlockSpec((t, c), lambda i: (nt - 1 - i, j))

    def halo(j, rows_):
        per = t // rows_
        return pl.BlockSpec((rows_, c), lambda i: (jnp.maximum((nt - 1 - i) * per - 1, 0), j))

    def full(shape):
        return pl.BlockSpec(shape, lambda i: (0,) * len(shape))

    params = (lcw, lcb, wab, ba, bx, lam, ccw, ccb, lng, lnb)
    small = [(_LRU_K, c), (1, c), (c, 2 * c), (1, c), (1, c), (1, c), (_CONF_K, c), (1, c), (1, c), (1, c)]
    return _pcall(
        body, args=(dycat, dycat, z, z, z, z, z, z, z, hs, hs, cc, *params), grid=(nt,),
        in_specs=[col(0), col(1),
                  col(0), halo(0, _LRU_HALO), col(1), col(2), halo(2, _CONF_HALO), col(3), halo(3, _CONF_HALO),
                  col(0), halo(0, 8), col(0)]
        + [full(p.shape) for p in params],
        out_specs=[pl.BlockSpec((t, 4 * c), lambda i: (nt - 1 - i, 0))] + [full(sh) for sh in small],
        out_shape=[jax.ShapeDtypeStruct((s, 4 * c), _MXU)] + [jax.ShapeDtypeStruct(sh, _F32) for sh in small],
        scratch_shapes=[pltpu.VMEM((t + _LRU_HALO, c), _F32), pltpu.VMEM((t + 8, c), _F32),
                        pltpu.VMEM((t + _CONF_HALO, c), _F32), pltpu.VMEM((t + _CONF_HALO, c), _F32),
                        pltpu.VMEM((8, c), _F32), pltpu.VMEM((8, c), _F32), pltpu.VMEM((_CONF_HALO, c), _F32),
                        pltpu.VMEM((7, t + _CONF_HALO, c), _F32)],
        sem=("arbitrary",), name=name, comm=comm)


_ATT_T = 512


def _attn_probs(qh, kh, scale):
    sc = _dot(qh, kh, _NT) * scale
    e = jnp.exp(sc - jnp.max(sc, axis=-1, keepdims=True))
    return e / jnp.sum(e, axis=-1, keepdims=True)


def _attn_fwd(q, kv, name):
    s, d = q.shape
    nm = kv.shape[0]
    hd = d // _XA_HEADS
    t = _tile(s, _ATT_T)
    scale = hd ** -0.5

    def body(q_ref, k_ref, v_ref, o_ref):
        for hh in range(_XA_HEADS):
            sl = slice(hh * hd, (hh + 1) * hd)
            p = _attn_probs(q_ref[:, sl], k_ref[:, sl], scale)
            o_ref[:, sl] = _dot(p, v_ref[:, sl], _NN).astype(o_ref.dtype)

    return pl.pallas_call(
        body, grid=(s // t,),
        in_specs=[pl.BlockSpec((t, d), lambda i: (i, 0)), pl.BlockSpec((nm, d), lambda i: (0, 0)),
                  pl.BlockSpec((nm, d), lambda i: (0, 1))],
        out_specs=pl.BlockSpec((t, d), lambda i: (i, 0)),
        out_shape=jax.ShapeDtypeStruct((s, d), _MXU),
        compiler_params=_cparams("parallel"), name=name,
    )(q, kv, kv)


def _attn_bwd(q, kv, do, name):
    s, d = q.shape
    nm = kv.shape[0]
    hd = d // _XA_HEADS
    t = _tile(s, _ATT_T)
    scale = hd ** -0.5

    def body(q_ref, k_ref, v_ref, do_ref, dq_ref, dk_ref, dv_ref):
        @pl.when(pl.program_id(0) == 0)
        def _():
            dk_ref[...] = jnp.zeros_like(dk_ref)
            dv_ref[...] = jnp.zeros_like(dv_ref)

        for hh in range(_XA_HEADS):
            sl = slice(hh * hd, (hh + 1) * hd)
            qh = q_ref[:, sl]
            kh = k_ref[:, sl]
            doh = do_ref[:, sl]
            p = _attn_probs(qh, kh, scale)
            dp = _dot(doh, v_ref[:, sl], _NT)
            dv_ref[:, sl] += _dot(p, doh, _TN)
            ds = (p * (dp - jnp.sum(dp * p, axis=-1, keepdims=True)) * scale).astype(_MXU)
            dq_ref[:, sl] = _dot(ds, kh, _NN).astype(dq_ref.dtype)
            dk_ref[:, sl] += _dot(ds, qh, _TN)

    row = pl.BlockSpec((t, d), lambda i: (i, 0))
    return pl.pallas_call(
        body, grid=(s // t,),
        in_specs=[row, pl.BlockSpec((nm, d), lambda i: (0, 0)), pl.BlockSpec((nm, d), lambda i: (0, 1)), row],
        out_specs=[row, pl.BlockSpec((nm, d), lambda i: (0, 0)), pl.BlockSpec((nm, d), lambda i: (0, 0))],
        out_shape=[jax.ShapeDtypeStruct((s, d), _MXU), jax.ShapeDtypeStruct((nm, d), _F32),
                   jax.ShapeDtypeStruct((nm, d), _F32)],
        compiler_params=_cparams("arbitrary"), name=name,
    )(q, kv, kv, do)


_FFN_K = 3
_FFN_FUSED_T = 256
_VMEM_LIMIT_FUSED = 58 * 1024 * 1024


def _ffn_fused_fwd(h3, w_up, w_down, fcw, fcb, x2, target, gfin, name):
    s, d = h3.shape
    nblk, _, n = w_up.shape
    half = nblk // 2
    f = half * n
    t = _tile(s, _FFN_FUSED_T)

    def body(h_ref, wup_ref, wdown_ref, w_ref, b_ref, x2_ref, t_ref, g_ref,
             gu_ref, act_ref, dx_ref, dxb_ref, l_ref, dg_ref, ext0_ref, ext1_ref, halo_ref):
        i = pl.program_id(0)
        first = i == 0
        h = h_ref[...]
        total = None
        for j in range(half):
            cs = slice(j * n, (j + 1) * n)
            ext_ref = ext0_ref if j % 2 == 0 else ext1_ref
            g0 = _dot(h, wup_ref[j], _NN)
            u = _dot(h, wup_ref[half + j], _NN)
            gu_ref[:, cs] = g0.astype(gu_ref.dtype)
            gu_ref[:, f + j * n:f + (j + 1) * n] = u.astype(gu_ref.dtype)
            ext_ref[0:8, :] = jnp.where(first, 0.0, halo_ref[:, cs])
            ext_ref[8:8 + t, :] = g0
            halo_ref[:, cs] = g0[t - 8:t, :]
            g = _causal_conv(ext_ref, 8, w_ref.at[:, cs], b_ref.at[:, cs], _FFN_K, t)
            act = (_gelu(g) * u).astype(act_ref.dtype)
            act_ref[:, cs] = act
            p = _dot(act, wdown_ref[cs, :], _NN)
            total = p if total is None else total + p
        _epi_final(total, (x2_ref, t_ref, g_ref), (dx_ref, dxb_ref, l_ref, dg_ref), first)

    def const(shape):
        return pl.BlockSpec(shape, lambda i: (0,) * len(shape), pipeline_mode=pl.Buffered(1))

    row = pl.BlockSpec((t, d), lambda i: (i, 0))
    vec = pl.BlockSpec((1, d), lambda i: (0, 0))
    return pl.pallas_call(
        body, grid=(s // t,),
        in_specs=[row, const(w_up.shape), const(w_down.shape), const(fcw.shape), const(fcb.shape), row, row, vec],
        out_specs=[pl.BlockSpec((t, 2 * f), lambda i: (i, 0)), pl.BlockSpec((t, f), lambda i: (i, 0)),
                   row, row, vec, vec],
        out_shape=[jax.ShapeDtypeStruct((s, 2 * f), _MXU), jax.ShapeDtypeStruct((s, f), _MXU),
                   jax.ShapeDtypeStruct((s, d), _F32), jax.ShapeDtypeStruct((s, d), _MXU),
                   jax.ShapeDtypeStruct((1, d), _F32), jax.ShapeDtypeStruct((1, d), _F32)],
        scratch_shapes=[pltpu.VMEM((t + 8, n), _F32), pltpu.VMEM((t + 8, n), _F32), pltpu.VMEM((8, f), _F32)],
        compiler_params=_cparams("arbitrary", vmem=_VMEM_LIMIT_FUSED), name=name,
    )(h3, w_up, w_down, fcw, fcb, x2, target, gfin)


def _ffn_fused_bwd(dx3b, gu, w_down, w_up, fcw, fcb, x2, gnorm, dx3, name, comm=None):
    s, d = x2.shape
    nblk, _, n = w_up.shape
    half = nblk // 2
    f = half * n
    t = _tile(s, _FFN_FUSED_T)
    nt = s // t
    hrows = 16

    def body(dxb_ref, gu_ref, guh_ref, wdown_ref, wup_ref, w_ref, b_ref, x2_ref, g_ref, dx3_ref,
             dgu_ref, dx2_ref, dx2b_ref, dgn_ref, dw_ref, db_ref, ext0_ref, ext1_ref, up0_ref, up1_ref, car_ref):
        i = pl.program_id(0)
        first_tile = i == nt - 1
        last_tile = i == 0

        @pl.when(last_tile)
        def _():
            dw_ref[...] = jnp.zeros_like(dw_ref)
            db_ref[...] = jnp.zeros_like(db_ref)
            car_ref[...] = jnp.zeros_like(car_ref)

        dxb = dxb_ref[...]
        total = None
        for j in range(half):
            cs = slice(j * n, (j + 1) * n)
            us = slice(f + j * n, f + (j + 1) * n)
            ext_ref = ext0_ref if j % 2 == 0 else ext1_ref
            up_ref = up0_ref if j % 2 == 0 else up1_ref
            dact = _dot(dxb, wdown_ref[cs, :], _NT)
            ext_ref[0:8, :] = jnp.where(first_tile, 0.0, guh_ref[:, cs].astype(_F32)[hrows - 8:hrows])
            ext_ref[8:8 + t, :] = gu_ref[:, cs].astype(_F32)
            g = _causal_conv(ext_ref, 8, w_ref.at[:, cs], b_ref.at[:, cs], _FFN_K, t)
            gl, dgl = _gelu_and_grad(g)
            du = (dact * gl).astype(dgu_ref.dtype)
            dgu_ref[:, us] = du
            dg = dact * gu_ref[:, us].astype(_F32) * dgl
            db_ref[:, cs] += _colsum(dg)
            for k in range(_FFN_K):
                dw_ref[k:k + 1, cs] += _colsum(dg * ext_ref[pl.ds(8 - (_FFN_K - 1) + k, t), :])
            up_ref[0:t, :] = dg
            up_ref[t:t + 8, :] = car_ref[:, cs]
            car_ref[:, cs] = dg[0:8, :]
            dg0 = w_ref[0:1, cs] * up_ref[pl.ds(_FFN_K - 1, t), :]
            for k in range(1, _FFN_K):
                dg0 = dg0 + w_ref[k:k + 1, cs] * up_ref[pl.ds(_FFN_K - 1 - k, t), :]
            dg0 = dg0.astype(dgu_ref.dtype)
            dgu_ref[:, cs] = dg0
            p = _dot(dg0, wup_ref[j], _NT) + _dot(du, wup_ref[half + j], _NT)
            total = p if total is None else total + p
        _epi_rms_bwd(total, (x2_ref, g_ref, dx3_ref), (dx2_ref, dx2b_ref, dgn_ref), last_tile)

    def const(shape):
        return pl.BlockSpec(shape, lambda i: (0,) * len(shape), pipeline_mode=pl.Buffered(1))

    row = pl.BlockSpec((t, d), lambda i: (nt - 1 - i, 0))
    vec = pl.BlockSpec((1, d), lambda i: (0, 0))
    per = t // hrows
    return _pcall(
        body, args=(dx3b, gu, gu, w_down, w_up, fcw, fcb, x2, gnorm, dx3), grid=(nt,),
        in_specs=[row, pl.BlockSpec((t, 2 * f), lambda i: (nt - 1 - i, 0)),
                  pl.BlockSpec((hrows, 2 * f), lambda i: (jnp.maximum((nt - 1 - i) * per - 1, 0), 0)),
                  const(w_down.shape), const(w_up.shape), const(fcw.shape), const(fcb.shape), row, vec, row],
        out_specs=[pl.BlockSpec((t, 2 * f), lambda i: (nt - 1 - i, 0)), row, row, vec,
                   pl.BlockSpec((_FFN_K, f), lambda i: (0, 0)), pl.BlockSpec((1, f), lambda i: (0, 0))],
        out_shape=[jax.ShapeDtypeStruct((s, 2 * f), _MXU), jax.ShapeDtypeStruct((s, d), _F32),
                   jax.ShapeDtypeStruct((s, d), _MXU), jax.ShapeDtypeStruct((1, d), _F32),
                   jax.ShapeDtypeStruct((_FFN_K, f), _F32), jax.ShapeDtypeStruct((1, f), _F32)],
        scratch_shapes=[pltpu.VMEM((t + 8, n), _F32), pltpu.VMEM((t + 8, n), _F32),
                        pltpu.VMEM((t + 8, n), _F32), pltpu.VMEM((t + 8, n), _F32), pltpu.VMEM((8, f), _F32)],
        sem=("arbitrary",), name=name, comm=comm, vmem=_VMEM_LIMIT_FUSED)


def _mesh_pos():
    return lax.axis_index("x"), lax.axis_index("y"), lax.axis_index("c")


def _flip(v, bit):
    return 1 - v if bit else v


def _sem_scratch(n):
    return [pltpu.SemaphoreType.DMA((7 * n,)), pltpu.SemaphoreType.DMA((7 * n,)), pltpu.SemaphoreType.DMA((n,))]


class _Gather:
    def __init__(self, xs):
        self.ins = list(xs)
        self.outs = [jax.ShapeDtypeStruct((_NDEV,) + v.shape, v.dtype) for v in xs]
        self.scratch = _sem_scratch(len(xs))

    def _plan(self, x_refs, out_refs, sems):
        send_sems, recv_sems, local_sems = sems
        x, y, c = _mesh_pos()
        me, sibling = (x, y, c), (x, y, 1 - c)
        chips = [(1 - x, y), (x, 1 - y), (1 - x, 1 - y)]

        def copy(a, k, block, to, src=None):
            slot = out_refs[a].at[4 * block[0] + 2 * block[1] + block[2]]
            return pltpu.make_async_remote_copy(
                src_ref=slot if src is None else src, dst_ref=slot,
                send_sem=send_sems.at[a * 7 + k], recv_sem=recv_sems.at[a * 7 + k],
                device_id=to, device_id_type=_MESH_ID)

        def own(a):
            return pltpu.make_async_copy(x_refs[a], out_refs[a].at[4 * x + 2 * y + c], local_sems.at[a])

        def first(a):
            return [copy(a, 0, me, sibling, src=x_refs[a])] + [
                copy(a, 1 + j, me, (*chip, c), src=x_refs[a]) for j, chip in enumerate(chips)]

        return me, sibling, chips, c, copy, own, first

    def start(self, x_refs, out_refs, sems):
        _, _, _, _, _, own, first = self._plan(x_refs, out_refs, sems)
        for a in range(len(self.ins)):
            own(a).start()
            for cp in first(a):
                cp.start()

    def finish(self, x_refs, out_refs, sems):
        me, sibling, chips, c, copy, own, first = self._plan(x_refs, out_refs, sems)
        n = len(self.ins)
        passed = []
        for a in range(n):
            for j, chip in enumerate(chips):
                copy(a, 1 + j, (*chip, c), me).wait_recv()
                fwd = copy(a, 4 + j, (*chip, c), sibling)
                fwd.start()
                passed.append(fwd)
        for a in range(n):
            copy(a, 0, sibling, me).wait_recv()
            for j, chip in enumerate(chips):
                copy(a, 4 + j, (*chip, 1 - c), me).wait_recv()
        for a in range(n):
            for cp in first(a):
                cp.wait_send()
        for cp in passed:
            cp.wait_send()
        for a in range(n):
            own(a).wait()


class _Exchange:
    def __init__(self, gs):
        self.ins = list(gs)
        self.outs = [jax.ShapeDtypeStruct(v.shape, v.dtype) for v in gs]
        self.scratch = _sem_scratch(len(gs))

    def _plan(self, g_refs, r_refs, sems):
        send_sems, recv_sems, local_sems = sems
        x, y, c = _mesh_pos()
        me_idx = 4 * x + 2 * y + c
        n = len(self.ins)

        def copy(a, k):
            peer = (_flip(x, k & 4), _flip(y, k & 2), _flip(c, k & 1))
            peer_idx = 4 * peer[0] + 2 * peer[1] + peer[2]
            return pltpu.make_async_remote_copy(
                src_ref=g_refs[a].at[peer_idx], dst_ref=r_refs[a].at[me_idx],
                send_sem=send_sems.at[a * 7 + k - 1], recv_sem=recv_sems.at[a * 7 + k - 1],
                device_id=peer, device_id_type=_MESH_ID)

        copies = [copy(a, k) for a in range(n) for k in range(1, _NDEV)]
        mine = [pltpu.make_async_copy(g_refs[a].at[me_idx], r_refs[a].at[me_idx], local_sems.at[a])
                for a in range(n)]
        return copies, mine

    def start(self, g_refs, r_refs, sems):
        copies, mine = self._plan(g_refs, r_refs, sems)
        for cp in copies + mine:
            cp.start()

    def finish(self, g_refs, r_refs, sems):
        copies, mine = self._plan(g_refs, r_refs, sems)
        for cp in copies:
            cp.wait_recv()
        for cp in copies:
            cp.wait_send()
        for cp in mine:
            cp.wait()


class _Both:
    def __init__(self, first, second):
        self.parts = (first, second)
        self.ins = first.ins + second.ins
        self.outs = first.outs + second.outs
        self.scratch = first.scratch + second.scratch

    def _split(self, ins, outs, sems):
        a, b = self.parts
        na, nb = len(a.ins), len(a.scratch)
        return (a, ins[:na], outs[:na], sems[:nb]), (b, ins[na:], outs[na:], sems[nb:])

    def start(self, ins, outs, sems):
        for part, i, o, s in self._split(ins, outs, sems):
            part.start(i, o, s)

    def finish(self, ins, outs, sems):
        for part, i, o, s in self._split(ins, outs, sems):
            part.finish(i, o, s)


def _comm_call(comm, name):
    def body(*refs):
        n_i, n_o = len(comm.ins), len(comm.outs)
        ins, outs, sems = refs[:n_i], refs[n_i:n_i + n_o], refs[n_i + n_o:]
        comm.start(ins, outs, sems)
        comm.finish(ins, outs, sems)

    return pl.pallas_call(
        body, out_shape=list(comm.outs), in_specs=[_ANY] * len(comm.ins), out_specs=[_ANY] * len(comm.outs),
        scratch_shapes=list(comm.scratch), name=name)(*comm.ins)


def _adamw_math(w, g, m, v):
    m = _ADAM_B1 * m + (1.0 - _ADAM_B1) * g
    v = _ADAM_B2 * v + (1.0 - _ADAM_B2) * (g * g)
    m_hat = m / (1.0 - _ADAM_B1 ** _ADAM_STEP)
    v_hat = v / (1.0 - _ADAM_B2 ** _ADAM_STEP)
    delta = -_ADAM_LR * (m_hat / (jnp.sqrt(v_hat) + _ADAM_EPS) + _ADAM_WD * w)
    return delta, m, v


def _sum_adamw(parts, w, m, v, name):
    r, c = w.shape
    tr = _tile(r, 128)

    def body(p_ref, w_ref, m_ref, v_ref, g_ref, d_ref, nm_ref, nv_ref):
        g = p_ref[0].astype(_F32)
        for j in range(1, _NDEV):
            g = g + p_ref[j].astype(_F32)
        delta, nm, nv = _adamw_math(w_ref[...], g, m_ref[...], v_ref[...])
        g_ref[...] = g
        d_ref[...] = delta
        nm_ref[...] = nm
        nv_ref[...] = nv

    blk = pl.BlockSpec((tr, c), lambda i: (i, 0))
    return pl.pallas_call(
        body, grid=(r // tr,),
        in_specs=[pl.BlockSpec((_NDEV, tr, c), lambda i: (0, i, 0)), blk, blk, blk],
        out_specs=[blk] * 4, out_shape=[jax.ShapeDtypeStruct((r, c), _F32)] * 4,
        compiler_params=_cparams("parallel"), name=name,
    )(parts, w, m, v)


def _sum8(parts, name):
    _, r, c = parts.shape

    def body(p_ref, o_ref):
        g = p_ref[0]
        for j in range(1, _NDEV):
            g = g + p_ref[j]
        o_ref[...] = g

    return pl.pallas_call(
        body, grid=(1,), in_specs=[pl.BlockSpec((_NDEV, r, c), lambda i: (0, 0, 0))],
        out_specs=pl.BlockSpec((r, c), lambda i: (0, 0)), out_shape=jax.ShapeDtypeStruct((r, c), _F32),
        compiler_params=_cparams("arbitrary"), name=name,
    )(parts)


def _adamw_flat(g, w, m, v, name):
    r, c = w.shape

    def body(g_ref, w_ref, m_ref, v_ref, d_ref, nm_ref, nv_ref):
        delta, nm, nv = _adamw_math(w_ref[...], g_ref[...], m_ref[...], v_ref[...])
        d_ref[...] = delta
        nm_ref[...] = nm
        nv_ref[...] = nv

    blk = pl.BlockSpec((r, c), lambda i: (0, 0))
    return pl.pallas_call(
        body, grid=(1,), in_specs=[blk] * 4, out_specs=[blk] * 3,
        out_shape=[jax.ShapeDtypeStruct((r, c), _F32)] * 3,
        compiler_params=_cparams("arbitrary"), name=name,
    )(g, w, m, v)


def _pack(arrs):
    flat = jnp.concatenate([a.reshape(-1).astype(_F32) for a in arrs])
    pad = (-flat.shape[0]) % 1024
    return jnp.pad(flat, (0, pad)).reshape(-1, 128)


def _unpack(flat2d, shapes):
    flat = flat2d.reshape(-1)
    out, off = [], 0
    for sh in shapes:
        size = 1
        for dim in sh:
            size *= dim
        out.append(flat[off:off + size].reshape(sh))
        off += size
    return out


def _block_diag(w):
    h, hd, _ = w.shape
    eye = jnp.eye(h, dtype=w.dtype)
    return (eye[:, None, :, None] * w[:, :, None, :]).reshape(h * hd, h * hd)


def _diag_blocks(full, h):
    hd = full.shape[0] // h
    return jnp.stack([full[i * hd:(i + 1) * hd, i * hd:(i + 1) * hd] for i in range(h)])


def kernel(x, mem, mix_norm_g, w_in, lru_conv_w, lru_conv_b, lru_w_a, lru_b_a, lru_w_x, lru_b_x, lru_lambda, conf_conv_w, conf_conv_b, conf_ln_g, conf_ln_b, w_out, xa_norm_g, mem_norm_g, w_q, w_kv, w_o, ffn_norm_g, w_up, ffn_conv_w, ffn_conv_b, w_down, final_norm_g, loss_target, m_mix_norm_g, m_w_in, m_lru_conv_w, m_lru_conv_b, m_lru_w_a, m_lru_b_a, m_lru_w_x, m_lru_b_x, m_lru_lambda, m_conf_conv_w, m_conf_conv_b, m_conf_ln_g, m_conf_ln_b, m_w_out, m_xa_norm_g, m_mem_norm_g, m_w_q, m_w_kv, m_w_o, m_ffn_norm_g, m_w_up, m_ffn_conv_w, m_ffn_conv_b, m_w_down, m_final_norm_g, v_mix_norm_g, v_w_in, v_lru_conv_w, v_lru_conv_b, v_lru_w_a, v_lru_b_a, v_lru_w_x, v_lru_b_x, v_lru_lambda, v_conf_conv_w, v_conf_conv_b, v_conf_ln_g, v_conf_ln_b, v_w_out, v_xa_norm_g, v_mem_norm_g, v_w_q, v_w_kv, v_w_o, v_ffn_norm_g, v_w_up, v_ffn_conv_w, v_ffn_conv_b, v_w_down, v_final_norm_g):
    names = ["mix_norm_g", "w_in", "lru_conv_w", "lru_conv_b", "lru_w_a", "lru_b_a", "lru_w_x", "lru_b_x",
             "lru_lambda", "conf_conv_w", "conf_conv_b", "conf_ln_g", "conf_ln_b", "w_out", "xa_norm_g",
             "mem_norm_g", "w_q", "w_kv", "w_o", "ffn_norm_g", "w_up", "ffn_conv_w", "ffn_conv_b", "w_down",
             "final_norm_g"]
    loc = locals()
    W = {n: loc[n] for n in names}
    M = {n: loc["m_" + n] for n in names}
    V = {n: loc["v_" + n] for n in names}
    big = ["w_in", "w_out", "w_q", "w_kv", "w_o", "w_up", "w_down"]
    conv_sharded = ["lru_conv_w", "conf_conv_w", "ffn_conv_w"]

    xs = x[0]
    mems = mem[0]
    tgt = loss_target[0]
    me = 4 * lax.axis_index("x") + 2 * lax.axis_index("y") + lax.axis_index("c")

    conv_shapes = [W[n].shape[1:] for n in conv_sharded]
    conv_pack = _pack([W[n][0] for n in conv_sharded])
    shard = {n: W[n][0].astype(_XFER) for n in big}
    h1, (g_in, g_conv) = _rms_fwd(xs, mix_norm_g, "rms1_fwd", comm=_Gather([shard["w_in"], conv_pack]))
    convs = [[] for _ in conv_sharded]
    for j in range(_NDEV):
        for idx, part in enumerate(_unpack(g_conv[j], conv_shapes)):
            convs[idx].append(part)
    lcw, ccw, fcw = [jnp.concatenate(parts, axis=-1) for parts in convs]

    wab = jnp.concatenate([_block_diag(lru_w_a[0]), _block_diag(lru_w_x[0])], axis=1).astype(_MXU)
    mixer_params = (lcw, lru_conv_b, wab, lru_b_a, lru_b_x, lru_lambda, ccw, conf_conv_b, conf_ln_g, conf_ln_b)

    z, (g_out, g_q, g_kv, g_o) = _mm_nn_stacked(
        h1, g_in, _F32, "mm_in_fwd", comm=_Gather([shard[n] for n in ("w_out", "w_q", "w_kv", "w_o")]))
    (ycat, hs, cc), (g_up, g_down) = _mixer_fwd(
        z, *mixer_params, "mixer_fwd", comm=_Gather([shard["w_up"], shard["w_down"]]))
    w_out_f = g_out.reshape(-1, g_out.shape[-1])
    w_q_f = g_q.reshape(-1, g_q.shape[-1])
    w_o_f = g_o.reshape(-1, g_o.shape[-1])
    w_down_f = g_down.reshape(-1, g_down.shape[-1])
    row32, row16, vec32 = (_F32, "row"), (_MXU, "row"), (_F32, "vec")
    x1, h2 = _mm_nn_nat(ycat, w_out_f, "mm_out_fwd", epi=_epi_residual_rms,
                        extra=[(xs, "row"), (xa_norm_g, "vec")], outs=[row32, row16])
    mn = _rms_fwd(mems, mem_norm_g, "rmsm_fwd")
    q = _mm_nn_nat(h2, w_q_f, "mm_q_fwd", outs=[row16])
    kv = _mm_nn_stacked(mn, g_kv, _MXU, "mm_kv_fwd")
    o = _attn_fwd(q, kv, "attn_fwd")
    x2, h3 = _mm_nn_nat(o, w_o_f, "mm_o_fwd", epi=_epi_residual_rms,
                        extra=[(x1, "row"), (ffn_norm_g, "vec")], outs=[row32, row16])

    gfin = final_norm_g.reshape(1, -1)
    gu, act, dx3, dx3b, lvec, dg_final = _ffn_fused_fwd(
        h3, g_up, w_down_f, fcw, ffn_conv_b, x2, tgt, gfin, "ffn_fwd")
    loss_local = 0.5 * jnp.sum(lvec) / xs.shape[1]
    loss = lax.psum(loss_local, _AXES)

    def rows8(p):
        return p.reshape(_NDEV, p.shape[0] // _NDEV, p.shape[1])

    p_down = _mm_tn_nat(act, dx3b, _XFER, "mm_down_wgrad", ts=2048)
    (dgu, dx2, dx2b, dg_ffn, dfcw, dfcb), (r_down,) = _ffn_fused_bwd(
        dx3b, gu, w_down_f, g_up, fcw, ffn_conv_b, x2, ffn_norm_g, dx3, "ffn_bwd",
        comm=_Exchange([rows8(p_down)]))
    p_up = _mm_tn_stacked(h3, dgu, _NDEV, _XFER, "mm_up_wgrad", slabs=2)

    do = _mm_nt_nat(dx2b, w_o_f, "mm_o_dgrad", outs=[row16])
    p_o = _mm_tn_nat(o, dx2b, _XFER, "mm_o_wgrad", ts=2048)
    dq, dk, dv = _attn_bwd(q, kv, do, "attn_bwd")
    dx1, dx1b, dg_xa = _mm_nt_nat(
        dq, w_q_f, "mm_q_dgrad", epi=_epi_rms_bwd, extra=[(x1, "row"), (xa_norm_g, "vec"), (dx2, "row")],
        outs=[row32, row16, vec32], tm=512)
    p_q = _mm_tn_nat(h2, dq, _XFER, "mm_q_wgrad", ts=2048)
    dkv = jnp.concatenate([dk, dv], axis=1).astype(_MXU)
    dmn = _mm_nt_stacked(dkv, g_kv, "mm_kv_dgrad", outs=[row32], slabs=_NDEV)
    p_kv = _mm_tn_stacked(mn, dkv, _NDEV, _XFER, "mm_kv_wgrad", slabs=_NDEV)
    _, _, dg_mem = _rms_bwd(dmn, mems, mem_norm_g, None, "rmsm_bwd")

    dycat = _mm_nt_nat(dx1b, w_out_f, "mm_out_dgrad", outs=[row32])
    p_out = _mm_tn_nat(ycat, dx1b, _XFER, "mm_out_wgrad", ts=2048)
    ((dz, dlcw, dlcb, dwab, dba, dbx, dlam, dccw, dccb, dlng, dlnb), (r_up, r_o, r_q, r_kv, r_out)) = _mixer_bwd(
        dycat, z, hs, cc, *mixer_params, "mixer_bwd",
        comm=_Exchange([p_up, rows8(p_o), rows8(p_q), p_kv, rows8(p_out)]))
    p_in = _mm_tn_stacked(h1, dz, _NDEV, _XFER, "mm_in_wgrad", slabs=_NDEV)

    c = _D_LRU
    heads = lru_w_a.shape[1]
    small_partial = {
        "lru_conv_w": dlcw, "lru_conv_b": dlcb,
        "lru_w_a": _diag_blocks(dwab[:, :c], heads), "lru_b_a": dba,
        "lru_w_x": _diag_blocks(dwab[:, c:], heads), "lru_b_x": dbx, "lru_lambda": dlam,
        "conf_conv_w": dccw, "conf_conv_b": dccb, "conf_ln_g": dlng, "conf_ln_b": dlnb,
        "xa_norm_g": dg_xa, "mem_norm_g": dg_mem, "ffn_norm_g": dg_ffn,
        "ffn_conv_w": dfcw, "ffn_conv_b": dfcb, "final_norm_g": dg_final,
    }
    early = list(small_partial)
    early_shapes = [small_partial[n].shape for n in early]
    (grad_x, dg_mix), (r_in, early_all) = _mm_nt_stacked(
        dz, g_in, "mm_in_dgrad", epi=_epi_rms_bwd, extra=[(xs, "row"), (mix_norm_g, "vec"), (dx1, "row")],
        outs=[row32, vec32], tm=512, slabs=_NDEV,
        comm=_Both(_Exchange([p_in]), _Gather([_pack([small_partial[n] for n in early])])))
    (mix_all,) = _comm_call(_Gather([dg_mix]), "gather_mix_grad")
    small = early + ["mix_norm_g"]
    small_sum = _unpack(_sum8(early_all, "sum_small_grads"), early_shapes)
    small_sum.append(_sum8(mix_all.reshape(_NDEV, 8, -1), "sum_mix_grad").reshape(dg_mix.shape))
    received = {"w_in": r_in, "w_out": r_out, "w_q": r_q, "w_kv": r_kv, "w_o": r_o, "w_up": r_up,
                "w_down": r_down}

    grads, deltas, new_m, new_v = {}, {}, {}, {}
    for n, rec in ((n, received[n]) for n in big):
        shp = W[n].shape
        w2, m2, v2 = (t.reshape(rec.shape[1:]) for t in (W[n], M[n], V[n]))
        outs = _sum_adamw(rec, w2, m2, v2, "adamw_" + n)
        grads[n], deltas[n], new_m[n], new_v[n] = (t.reshape(shp) for t in outs)

    small_g = []
    for n, g in zip(small, small_sum):
        if n in conv_sharded:
            width = W[n].shape[-1]
            g = lax.dynamic_slice_in_dim(g, me * width, width, axis=1)
        small_g.append(g.reshape(W[n].shape))
    small_shapes = [W[n].shape for n in small]
    sd, sm, sv = _adamw_flat(_pack(small_g), _pack([W[n] for n in small]), _pack([M[n] for n in small]),
                             _pack([V[n] for n in small]), "adamw_small")
    for n, g, d_, m_, v_ in zip(small, small_g, _unpack(sd, small_shapes), _unpack(sm, small_shapes),
                                _unpack(sv, small_shapes)):
        grads[n], deltas[n], new_m[n], new_v[n] = g, d_, m_, v_

    return (loss, grad_x[None], *[grads[n] for n in names], *[deltas[n] for n in names],
            *[new_m[n] for n in names], *[new_v[n] for n in names])
```

```python
import functools

import jax
import jax.numpy as jnp
from jax import lax
from jax.experimental import pallas as pl
from jax.experimental.pallas import tpu as pltpu

_MXU = jnp.bfloat16
_XFER = jnp.bfloat16
_F32 = jnp.float32
_EPS = 1e-6
_NDEV = 8
_AXES = ("x", "y", "c")
_VMEM_LIMIT = 48 * 1024 * 1024

_D_LRU = 512
_XA_HEADS = 4
_RG_C = 8.0
_ADAM_LR, _ADAM_B1, _ADAM_B2, _ADAM_EPS, _ADAM_WD, _ADAM_STEP = 0.001, 0.9, 0.999, 1e-08, 0.01, 10

_MESH_ID = pl.DeviceIdType.MESH
_ANY = pl.BlockSpec(memory_space=pl.ANY)


def _cparams(*sem, vmem=_VMEM_LIMIT):
    return pltpu.CompilerParams(dimension_semantics=tuple(sem), vmem_limit_bytes=vmem)


def _pcall(body, *, args, grid, in_specs, out_specs, out_shape, sem, name, scratch_shapes=(), comm=None,
           vmem=_VMEM_LIMIT):
    outs_l = list(out_shape) if isinstance(out_shape, (list, tuple)) else [out_shape]
    ospecs_l = list(out_specs) if isinstance(out_specs, (list, tuple)) else [out_specs]
    n_in, n_out, n_scr = len(args), len(outs_l), len(scratch_shapes)
    if comm is None:
        res = pl.pallas_call(
            body, grid=grid, in_specs=list(in_specs), out_specs=ospecs_l, out_shape=outs_l,
            scratch_shapes=list(scratch_shapes), compiler_params=_cparams(*sem, vmem=vmem), name=name)(*args)
        return list(res), []
    n_ci, n_co = len(comm.ins), len(comm.outs)

    def wrapped(*refs):
        ins, cins = refs[:n_in], refs[n_in:n_in + n_ci]
        o = n_in + n_ci
        outs, couts = refs[o:o + n_out], refs[o + n_out:o + n_out + n_co]
        s = o + n_out + n_co
        scr, cscr = refs[s:s + n_scr], refs[s + n_scr:]
        first = pl.program_id(0) == 0
        last = pl.program_id(0) == grid[0] - 1
        for ax in range(1, len(grid)):
            first = jnp.logical_and(first, pl.program_id(ax) == 0)
            last = jnp.logical_and(last, pl.program_id(ax) == grid[ax] - 1)

        @pl.when(first)
        def _():
            comm.start(cins, couts, cscr)

        body(*ins, *outs, *scr)

        @pl.when(last)
        def _():
            comm.finish(cins, couts, cscr)

    res = pl.pallas_call(
        wrapped, grid=grid, in_specs=list(in_specs) + [_ANY] * n_ci, out_specs=ospecs_l + [_ANY] * n_co,
        out_shape=outs_l + list(comm.outs), scratch_shapes=list(scratch_shapes) + list(comm.scratch),
        compiler_params=_cparams(*(("arbitrary",) * len(grid)), vmem=vmem), name=name)(*args, *comm.ins)
    return list(res[:n_out]), list(res[n_out:])


def _sigmoid(v):
    return 1.0 / (1.0 + jnp.exp(-v))


_GELU_C = 0.7978845608028654
_GELU_K = 0.044715


def _gelu(v):
    t = jnp.tanh(_GELU_C * (v + _GELU_K * v * v * v))
    return 0.5 * v * (1.0 + t)


def _gelu_and_grad(v):
    v2 = v * v
    s = 0.5 * jnp.tanh(v * (_GELU_C + (_GELU_C * _GELU_K) * v2)) + 0.5
    g = v * s
    dg = s + (g * (1.0 - s)) * ((2.0 * _GELU_C) + (6.0 * _GELU_C * _GELU_K) * v2)
    return g, dg


def _softplus(v):
    e = jnp.exp(-jnp.abs(v))
    log1p = jnp.where(e < 1e-2, e * (1.0 - e * (0.5 - e * (1.0 / 3.0))), jnp.log(1.0 + e))
    return jnp.maximum(v, 0.0) + log1p


def _neg_expm1(v):
    series = -v * (1.0 + v * (0.5 + v * ((1.0 / 6.0) + v * (1.0 / 24.0))))
    return jnp.where(v > -0.0625, series, 1.0 - jnp.exp(v))


def _dot(a, b, dims):
    return lax.dot_general(a.astype(_MXU), b.astype(_MXU), (dims, ((), ())), preferred_element_type=_F32)


_NN = ((1,), (0,))
_NT = ((1,), (1,))
_TN = ((0,), (0,))


def _scan_fwd(a, b, rows):
    n = a.shape[0]
    d = 1
    while d < n:
        keep = rows >= d
        b = jnp.where(keep, b + a * pltpu.roll(b, d, 0), b)
        a = jnp.where(keep, a * pltpu.roll(a, d, 0), a)
        d *= 2
    return a, b


def _scan_rev(a, b, rows):
    n = a.shape[0]
    d = 1
    while d < n:
        keep = rows < n - d
        b = jnp.where(keep, b + a * pltpu.roll(b, n - d, 0), b)
        a = jnp.where(keep, a * pltpu.roll(a, n - d, 0), a)
        d *= 2
    return a, b


def _colsum(v):
    return jnp.sum(v, axis=0, keepdims=True)


def _mm(a, b, *, dims, grid, a_spec, b_spec, outs, acc_shape, name, extra=(), epi=None, slabs=1, comm=None):
    nred = grid[-1]
    red_axis = len(grid) - 1
    n_ex, n_out = len(extra), len(outs)
    epi = _epi_store if epi is None else epi

    def body(*refs):
        a_ref, b_ref = refs[:2]
        ex, o_refs, acc_ref = refs[2:2 + n_ex], refs[2 + n_ex:2 + n_ex + n_out], refs[-1]
        if slabs == 1:
            p = _dot(a_ref[...], b_ref[...], dims)
        else:
            n = b_ref.shape[-1]
            p = _dot(a_ref[:, 0:n], b_ref[0], dims)
            for jj in range(1, slabs):
                p = p + _dot(a_ref[:, jj * n:(jj + 1) * n], b_ref[jj], dims)

        first_rows = pl.program_id(0) == 0
        if nred == 1:
            epi(p, ex, o_refs, first_rows)
        else:
            k = pl.program_id(red_axis)

            @pl.when(k == 0)
            def _():
                acc_ref[...] = p

            @pl.when(jnp.logical_and(k > 0, k < nred - 1))
            def _():
                acc_ref[...] += p

            @pl.when(k == nred - 1)
            def _():
                epi(acc_ref[...] + p, ex, o_refs, first_rows)

    sem = ("parallel",) * (len(grid) - 1) + ("arbitrary",)
    if any(o[0].shape[0] == 1 for o in outs):
        sem = ("arbitrary",) * len(grid)
    res, cres = _pcall(
        body, args=(a, b) + tuple(e[0] for e in extra), grid=grid,
        in_specs=[a_spec, b_spec] + [e[1] for e in extra],
        out_specs=[o[1] for o in outs], out_shape=[o[0] for o in outs],
        scratch_shapes=[pltpu.VMEM(acc_shape if nred > 1 else (8, 128), _F32)], sem=sem, name=name, comm=comm)
    res = res[0] if n_out == 1 else res
    return res if comm is None else (res, cres)


def _epi_store(total, ex, outs, first_rows):
    outs[0][...] = total.astype(outs[0].dtype)


def _epi_residual_rms(total, ex, outs, first_rows):
    res_ref, g_ref = ex
    xn = total + res_ref[...]
    outs[0][...] = xn
    r = lax.rsqrt(jnp.mean(xn * xn, axis=-1, keepdims=True) + _EPS)
    outs[1][...] = (xn * r * g_ref[...]).astype(outs[1].dtype)


def _epi_rms_bwd(total, ex, outs, first_rows):
    x_ref, g_ref, dres_ref = ex
    dg_ref = outs[-1]
    xv = x_ref[...]
    r = lax.rsqrt(jnp.mean(xv * xv, axis=-1, keepdims=True) + _EPS)
    xhat = xv * r
    dxh = total * g_ref[...]
    dx = dres_ref[...] + r * (dxh - xhat * jnp.mean(dxh * xhat, axis=-1, keepdims=True))
    for o_ref in outs[:-1]:
        o_ref[...] = dx.astype(o_ref.dtype)

    @pl.when(first_rows)
    def _():
        dg_ref[...] = jnp.zeros_like(dg_ref)

    dg_ref[...] += _colsum(total * xhat)


def _epi_final(total, ex, outs, first_rows):
    res_ref, t_ref, g_ref = ex
    dx_ref, dxb_ref, l_ref, dg_ref = outs
    xv = total + res_ref[...]
    gv = g_ref[...]
    d = xv.shape[-1]
    r = lax.rsqrt(jnp.mean(xv * xv, axis=-1, keepdims=True) + _EPS)
    xhat = xv * r
    err = xhat * gv - t_ref[...]
    dy = err * (1.0 / d)
    dxh = dy * gv
    dx = r * (dxh - xhat * jnp.mean(dxh * xhat, axis=-1, keepdims=True))
    dx_ref[...] = dx
    dxb_ref[...] = dx.astype(dxb_ref.dtype)

    @pl.when(first_rows)
    def _():
        l_ref[...] = jnp.zeros_like(l_ref)
        dg_ref[...] = jnp.zeros_like(dg_ref)

    l_ref[...] += _colsum(err * err)
    dg_ref[...] += _colsum(dy * xhat)


def _tile(m, cap):
    t = min(m, cap)
    assert m % t == 0
    return t


def _row_spec(tm, n):
    return pl.BlockSpec((tm, n), lambda i, *_: (i, 0))


def _vec_spec(n):
    return pl.BlockSpec((1, n), lambda *_: (0, 0))


def _row_io(m, n, tm, extra, outs):
    def spec(kind):
        return _row_spec(tm, n) if kind == "row" else _vec_spec(n)

    ex = [(arr, spec(kind)) for arr, kind in extra]
    os_ = [(jax.ShapeDtypeStruct((m, n) if kind == "row" else (1, n), dt), spec(kind)) for dt, kind in outs]
    return ex, os_


def _mm_nn_stacked(a, w, out_dtype, name, comm=None, tm=1024):
    m, k = a.shape
    j, _, n = w.shape
    tm = _tile(m, tm)
    return _mm(a, w, dims=_NN, grid=(m // tm, j, 1),
               a_spec=pl.BlockSpec((tm, k), lambda i, jj, r: (i, 0)),
               b_spec=pl.BlockSpec((None, k, n), lambda i, jj, r: (jj, 0, 0)),
               outs=[(jax.ShapeDtypeStruct((m, j * n), out_dtype), pl.BlockSpec((tm, n), lambda i, jj, r: (i, jj)))],
               acc_shape=(tm, n), name=name, comm=comm)


def _mm_nn_nat(a, w, name, *, outs, extra=(), epi=None, tm=1024):
    m, kt = a.shape
    _, n = w.shape
    tm = _tile(m, tm)
    tk = _tile(kt, 1024)
    ex, os_ = _row_io(m, n, tm, extra, outs)
    return _mm(a, w, dims=_NN, grid=(m // tm, kt // tk),
               a_spec=pl.BlockSpec((tm, tk), lambda i, r: (i, r)),
               b_spec=pl.BlockSpec((tk, n), lambda i, r: (r, 0)),
               outs=os_, extra=ex, epi=epi, acc_shape=(tm, n), name=name)


def _mm_nt_stacked(dc, w, name, *, outs, extra=(), epi=None, comm=None, tm=1024, slabs=1):
    m = dc.shape[0]
    j, k, n = w.shape
    tm = _tile(m, tm)
    assert j % slabs == 0
    ex, os_ = _row_io(m, k, tm, extra, outs)
    wblk = (None, k, n) if slabs == 1 else (slabs, k, n)
    return _mm(dc, w, dims=_NT, grid=(m // tm, j // slabs),
               a_spec=pl.BlockSpec((tm, slabs * n), lambda i, r: (i, r)),
               b_spec=pl.BlockSpec(wblk, lambda i, r: (r, 0, 0)),
               outs=os_, extra=ex, epi=epi, acc_shape=(tm, k), name=name, slabs=slabs, comm=comm)


def _mm_nt_nat(dc, w, name, *, outs, extra=(), epi=None, tm=1024):
    m, n = dc.shape
    kt = w.shape[0]
    tm = _tile(m, tm)
    tkb = _tile(kt, 1024)
    if kt == tkb:
        ex, os_ = _row_io(m, kt, tm, extra, outs)
    else:
        assert not extra and len(outs) == 1
        ex, os_ = [], [(jax.ShapeDtypeStruct((m, kt), outs[0][0]), pl.BlockSpec((tm, tkb), lambda i, kb, r: (i, kb)))]
    return _mm(dc, w, dims=_NT, grid=(m // tm, kt // tkb, 1),
               a_spec=pl.BlockSpec((tm, n), lambda i, kb, r: (i, 0)),
               b_spec=pl.BlockSpec((tkb, n), lambda i, kb, r: (kb, 0)),
               outs=os_, extra=ex, epi=epi, acc_shape=(tm, tkb), name=name)


def _mm_tn_stacked(a, dc, j, out_dtype, name, slabs=1, ts=1024):
    s, k = a.shape
    n = dc.shape[1] // j
    ts = _tile(s, ts)
    assert j % slabs == 0

    def epi(total, ex, outs, first_rows):
        for jj in range(slabs):
            outs[0][jj] = total[:, jj * n:(jj + 1) * n].astype(outs[0].dtype)

    return _mm(a, dc, dims=_TN, grid=(j // slabs, s // ts),
               a_spec=pl.BlockSpec((ts, k), lambda jj, r: (r, 0)),
               b_spec=pl.BlockSpec((ts, slabs * n), lambda jj, r: (r, jj)),
               outs=[(jax.ShapeDtypeStruct((j, k, n), out_dtype),
                      pl.BlockSpec((slabs, k, n), lambda jj, r: (jj, 0, 0)))],
               epi=epi, acc_shape=(k, slabs * n), name=name)


def _mm_tn_nat(a, dc, out_dtype, name, ts=1024):
    s, kt = a.shape
    n = dc.shape[1]
    ts = _tile(s, ts)
    tkb = _tile(kt, 512)
    return _mm(a, dc, dims=_TN, grid=(kt // tkb, s // ts),
               a_spec=pl.BlockSpec((ts, tkb), lambda kb, r: (r, kb)),
               b_spec=pl.BlockSpec((ts, n), lambda kb, r: (r, 0)),
               outs=[(jax.ShapeDtypeStruct((kt, n), out_dtype), pl.BlockSpec((tkb, n), lambda kb, r: (kb, 0)))],
               acc_shape=(tkb, n), name=name)


def _rms_fwd(x, g, name, comm=None):
    s, d = x.shape
    t = _tile(s, 256)

    def body(x_ref, g_ref, h_ref):
        xv = x_ref[...]
        r = lax.rsqrt(jnp.mean(xv * xv, axis=-1, keepdims=True) + _EPS)
        h_ref[...] = (xv * r * g_ref[...]).astype(h_ref.dtype)

    res, cres = _pcall(
        body, args=(x, g), grid=(s // t,),
        in_specs=[pl.BlockSpec((t, d), lambda i: (i, 0)), pl.BlockSpec((1, d), lambda i: (0, 0))],
        out_specs=pl.BlockSpec((t, d), lambda i: (i, 0)),
        out_shape=jax.ShapeDtypeStruct((s, d), _MXU), sem=("parallel",), name=name, comm=comm)
    return res[0] if comm is None else (res[0], cres)


def _rms_bwd(dh, x, g, dres, name):
    s, d = x.shape
    t = _tile(s, 256)
    has_res = dres is not None

    def body(*refs):
        if has_res:
            dh_ref, x_ref, g_ref, dres_ref, dx_ref, dxb_ref, dg_ref = refs
        else:
            dh_ref, x_ref, g_ref, dx_ref, dxb_ref, dg_ref = refs
        xv = x_ref[...]
        dhv = dh_ref[...]
        r = lax.rsqrt(jnp.mean(xv * xv, axis=-1, keepdims=True) + _EPS)
        xhat = xv * r
        dxh = dhv * g_ref[...]
        dx = r * (dxh - xhat * jnp.mean(dxh * xhat, axis=-1, keepdims=True))
        if has_res:
            dx = dx + dres_ref[...]
        dx_ref[...] = dx
        dxb_ref[...] = dx.astype(dxb_ref.dtype)

        @pl.when(pl.program_id(0) == 0)
        def _():
            dg_ref[...] = jnp.zeros_like(dg_ref)

        dg_ref[...] += _colsum(dhv * xhat)

    row = pl.BlockSpec((t, d), lambda i: (i, 0))
    vec = pl.BlockSpec((1, d), lambda i: (0, 0))
    in_specs = [row, row, vec] + ([row] if has_res else [])
    args = (dh, x, g) + ((dres,) if has_res else ())
    return pl.pallas_call(
        body, grid=(s // t,), in_specs=in_specs, out_specs=[row, row, vec],
        out_shape=[jax.ShapeDtypeStruct((s, d), _F32), jax.ShapeDtypeStruct((s, d), _MXU),
                   jax.ShapeDtypeStruct((1, d), _F32)],
        compiler_params=_cparams("arbitrary"), name=name,
    )(*args)


_LRU_K = 4
_CONF_K = 31
_LRU_HALO = 8
_CONF_HALO = 32
_MIX_T = 256


def _lru_gates(lx, wab_ref, ba_ref, bx_ref, lam_ref):
    c = _D_LRU
    pre = _dot(lx, wab_ref[...], _NN)
    r = _sigmoid(pre[:, :c] + ba_ref[...])
    ig = _sigmoid(pre[:, c:] + bx_ref[...])
    sp = _softplus(-lam_ref[...])
    log_a = (-_RG_C) * r * sp
    a = jnp.exp(log_a)
    mult = jnp.sqrt(_neg_expm1(2.0 * log_a))
    return r, ig, sp, a, mult


def _causal_conv(ext_ref, halo, w_ref, b_ref, taps, t):
    acc = b_ref[...] + w_ref[0:1, :] * ext_ref[pl.ds(halo - (taps - 1), t), :]
    for k in range(1, taps):
        acc = acc + w_ref[k:k + 1, :] * ext_ref[pl.ds(halo - (taps - 1) + k, t), :]
    return acc


class _Windows:
    def __init__(self, ext_ref, shifted_ref, t):
        self.ext_ref, self.shifted_ref, self.t = ext_ref, shifted_ref, t
        rows = t + 24
        for r in range(1, 8):
            shifted_ref[r - 1, 0:rows, :] = ext_ref[pl.ds(r, rows), :]

    def __call__(self, off):
        q, r = divmod(off, 8)
        if r == 0:
            return self.ext_ref[pl.ds(8 * q, self.t), :]
        return self.shifted_ref[r - 1, pl.ds(8 * q, self.t), :]


def _mixer_fwd(xs, h1, w_in, w_out, gnorm, lcw, lcb, wab, ba, bx, lam, ccw, ccb, lng, lnb, name, comm=None):
    s, d = xs.shape
    nblk, _, n = w_in.shape
    c = _D_LRU
    t = _tile(s, _MIX_T)
    nt = s // t

    def body(x_ref, h_ref, win_ref, wout_ref, gn_ref,
             lcw_ref, lcb_ref, wab_ref, ba_ref, bx_ref, lam_ref, ccw_ref, ccb_ref, lng_ref, lnb_ref,
             z_ref, ycat_ref, hs_ref, cc_ref, x1_ref, h2_ref,
             ext_ref, cge_ref, hc_ref, shifted_ref, zprev_ref):
        i = pl.program_id(0)
        first = i == 0
        rows = lax.broadcasted_iota(jnp.int32, (t, c), 0)

        @pl.when(first)
        def _():
            zprev_ref[...] = jnp.zeros_like(zprev_ref)

        hv = h_ref[...]
        for j in range(nblk):
            z_ref[:, j * n:(j + 1) * n] = _dot(hv, win_ref[j], _NN)
        lx0_ref, gate_ref = z_ref.at[:, 0:c], z_ref.at[:, c:2 * c]
        ca_ref, cb_ref = z_ref.at[:, 2 * c:3 * c], z_ref.at[:, 3 * c:4 * c]
        lx0h_ref = zprev_ref.at[_CONF_HALO - _LRU_HALO:_CONF_HALO, 0:c]
        cah_ref, cbh_ref = zprev_ref.at[:, 2 * c:3 * c], zprev_ref.at[:, 3 * c:4 * c]

        ext_ref[0:_LRU_HALO, :] = jnp.where(first, 0.0, lx0h_ref[...])
        ext_ref[_LRU_HALO:_LRU_HALO + t, :] = lx0_ref[...]
        lx = _causal_conv(ext_ref, _LRU_HALO, lcw_ref, lcb_ref, _LRU_K, t)
        r, ig, sp, a, mult = _lru_gates(lx, wab_ref, ba_ref, bx_ref, lam_ref)
        u = mult * (ig * lx)
        a_cum, h_loc = _scan_fwd(a, u, rows)

        @pl.when(first)
        def _():
            hc_ref[...] = jnp.zeros_like(hc_ref)

        h = h_loc + a_cum * hc_ref[7:8, :]
        hs_ref[...] = h
        hc_ref[...] = hs_ref[pl.ds(t - 8, 8), :]
        ycat_ref[:, 0:c] = (h * _gelu(gate_ref[...])).astype(ycat_ref.dtype)

        cge_ref[0:_CONF_HALO, :] = jnp.where(first, 0.0, cah_ref[...] * _sigmoid(cbh_ref[...]))
        cge_ref[_CONF_HALO:_CONF_HALO + t, :] = ca_ref[...] * _sigmoid(cb_ref[...])
        win = _Windows(cge_ref, shifted_ref, t)
        first_off = _CONF_HALO - (_CONF_K - 1)
        cc = ccb_ref[...] + ccw_ref[0:1, :] * win(first_off)
        for k in range(1, _CONF_K):
            cc = cc + ccw_ref[k:k + 1, :] * win(first_off + k)
        cc_ref[...] = cc
        xc = cc - jnp.mean(cc, axis=-1, keepdims=True)
        rstd = lax.rsqrt(jnp.mean(xc * xc, axis=-1, keepdims=True) + _EPS)
        ln = xc * rstd * lng_ref[...] + lnb_ref[...]
        ycat_ref[:, c:2 * c] = (ln * _sigmoid(ln)).astype(ycat_ref.dtype)

        zprev_ref[...] = z_ref[pl.ds(t - _CONF_HALO, _CONF_HALO), :]
        y = _dot(ycat_ref[...], wout_ref[...], _NN)
        _epi_residual_rms(y, (x_ref, gn_ref), (x1_ref, h2_ref), first)

    def const(arr):
        return pl.BlockSpec(arr.shape, lambda i: (0,) * arr.ndim, pipeline_mode=pl.Buffered(1))

    def rows_of(width):
        return pl.BlockSpec((t, width), lambda i: (i, 0))

    params = (lcw, lcb, wab, ba, bx, lam, ccw, ccb, lng, lnb)
    res, cres = _pcall(
        body, args=(xs, h1, w_in, w_out, gnorm, *params), grid=(nt,),
        in_specs=[rows_of(d), rows_of(d), const(w_in), const(w_out), const(gnorm)] + [const(p) for p in params],
        out_specs=[rows_of(nblk * n), rows_of(2 * c), rows_of(c), rows_of(c), rows_of(d), rows_of(d)],
        out_shape=[jax.ShapeDtypeStruct((s, nblk * n), _F32), jax.ShapeDtypeStruct((s, 2 * c), _MXU),
                   jax.ShapeDtypeStruct((s, c), _F32), jax.ShapeDtypeStruct((s, c), _F32),
                   jax.ShapeDtypeStruct((s, d), _F32), jax.ShapeDtypeStruct((s, d), _MXU)],
        scratch_shapes=[pltpu.VMEM((t + _LRU_HALO, c), _F32), pltpu.VMEM((t + _CONF_HALO, c), _F32),
                        pltpu.VMEM((8, c), _F32), pltpu.VMEM((7, t + _CONF_HALO, c), _F32),
                        pltpu.VMEM((_CONF_HALO, nblk * n), _F32)],
        sem=("arbitrary",), name=name, comm=comm)
    return res, cres


def _mixer_bwd(dycat, z, hs, cc, lcw, lcb, wab, ba, bx, lam, ccw, ccb, lng, lnb, name, comm=None):
    s = z.shape[0]
    c = _D_LRU
    t = _tile(s, _MIX_T)
    nt = s // t

    def body(dyl_ref, dc_ref, lx0_ref, lx0h_ref, gate_ref, ca_ref, cah_ref, cb_ref, cbh_ref,
             hs_ref, hsh_ref, cc_ref,
             lcw_ref, lcb_ref, wab_ref, ba_ref, bx_ref, lam_ref, ccw_ref, ccb_ref, lng_ref, lnb_ref,
             dz_ref, dlcw_ref, dlcb_ref, dwab_ref, dba_ref, dbx_ref, dlam_ref, dccw_ref, dccb_ref, dlng_ref,
             dlnb_ref,
             ext_ref, up_ref, cge_ref, dce_ref, xc_ref, dlxc_ref, dccc_ref, shifted_ref):
        i = pl.program_id(0)
        first_tile = i == nt - 1
        last_tile = i == 0
        rows = lax.broadcasted_iota(jnp.int32, (t, c), 0)

        @pl.when(last_tile)
        def _():
            for ref in (dlcw_ref, dlcb_ref, dwab_ref, dba_ref, dbx_ref, dlam_ref, dccw_ref, dccb_ref, dlng_ref,
                        dlnb_ref, xc_ref, dlxc_ref, dccc_ref):
                ref[...] = jnp.zeros_like(ref)

        ext_ref[0:_LRU_HALO, :] = jnp.where(first_tile, 0.0, lx0h_ref[...])
        ext_ref[_LRU_HALO:_LRU_HALO + t, :] = lx0_ref[...]
        lx = _causal_conv(ext_ref, _LRU_HALO, lcw_ref, lcb_ref, _LRU_K, t)
        r, ig, sp, a, mult = _lru_gates(lx, wab_ref, ba_ref, bx_ref, lam_ref)
        h = hs_ref[...]
        gl, dgl = _gelu_and_grad(gate_ref[...])
        dyl = dyl_ref[...]
        dz_ref[:, c:2 * c] = (dyl * h * dgl).astype(dz_ref.dtype)
        dh = dyl * gl

        up_ref[0:t, :] = a
        up_ref[t:t + 8, :] = jnp.ones((8, c), _F32)
        a_up = up_ref[pl.ds(1, t), :]
        a_cum, g_loc = _scan_rev(a_up, dh, rows)
        gt = g_loc + a_cum * xc_ref[0:1, :]
        xc_ref[...] = (a * gt)[0:8, :]

        up_ref[0:8, :] = jnp.where(first_tile, 0.0, hsh_ref[...])
        up_ref[8:8 + t, :] = h
        hprev = up_ref[pl.ds(7, t), :]

        da = gt * hprev
        dmult = gt * ig * lx
        dig = gt * mult * lx
        dlx = gt * mult * ig
        dlog_a = da * a - dmult * a * a / mult
        dpre_r = dlog_a * (-_RG_C) * sp * r * (1.0 - r)
        dpre_i = dig * ig * (1.0 - ig)
        dlam_ref[...] += _colsum(dlog_a * r) * (_RG_C * _sigmoid(-lam_ref[...]))
        dba_ref[...] += _colsum(dpre_r)
        dbx_ref[...] += _colsum(dpre_i)
        dpre = jnp.concatenate([dpre_r, dpre_i], axis=1).astype(_MXU)
        dlx = dlx + _dot(dpre, wab_ref[...], _NT)
        dwab_ref[...] += _dot(lx, dpre, _TN)

        dlcb_ref[...] += _colsum(dlx)
        up_ref[0:t, :] = dlx
        up_ref[t:t + 8, :] = dlxc_ref[...]
        dlxc_ref[...] = dlx[0:8, :]
        acc = lcw_ref[0:1, :] * up_ref[pl.ds(_LRU_K - 1, t), :]
        for k in range(1, _LRU_K):
            acc = acc + lcw_ref[k:k + 1, :] * up_ref[pl.ds(_LRU_K - 1 - k, t), :]
        dz_ref[:, 0:c] = acc.astype(dz_ref.dtype)
        for k in range(_LRU_K):
            dlcw_ref[k:k + 1, :] += _colsum(dlx * ext_ref[pl.ds(_LRU_HALO - (_LRU_K - 1) + k, t), :])

        sig_b = _sigmoid(cb_ref[...])
        ca = ca_ref[...]
        cge_ref[0:_CONF_HALO, :] = jnp.where(first_tile, 0.0, cah_ref[...] * _sigmoid(cbh_ref[...]))
        cge_ref[_CONF_HALO:_CONF_HALO + t, :] = ca * sig_b
        ccv = cc_ref[...]
        xcen = ccv - jnp.mean(ccv, axis=-1, keepdims=True)
        rstd = lax.rsqrt(jnp.mean(xcen * xcen, axis=-1, keepdims=True) + _EPS)
        xn = xcen * rstd
        ln = xn * lng_ref[...] + lnb_ref[...]
        sg = _sigmoid(ln)
        dln = dc_ref[...] * (sg * (1.0 + ln * (1.0 - sg)))
        dlng_ref[...] += _colsum(dln * xn)
        dlnb_ref[...] += _colsum(dln)
        dxn = dln * lng_ref[...]
        dcc = rstd * (dxn - jnp.mean(dxn, axis=-1, keepdims=True)
                      - xn * jnp.mean(dxn * xn, axis=-1, keepdims=True))
        dccb_ref[...] += _colsum(dcc)
        win = _Windows(cge_ref, shifted_ref, t)
        for k in range(_CONF_K):
            dccw_ref[k:k + 1, :] += _colsum(dcc * win(_CONF_HALO - (_CONF_K - 1) + k))
        dce_ref[0:t, :] = dcc
        dce_ref[t:t + _CONF_HALO, :] = dccc_ref[...]
        dccc_ref[...] = dcc[0:_CONF_HALO, :]
        win = _Windows(dce_ref, shifted_ref, t)
        dcg = ccw_ref[0:1, :] * win(_CONF_K - 1)
        for k in range(1, _CONF_K):
            dcg = dcg + ccw_ref[k:k + 1, :] * win(_CONF_K - 1 - k)
        dz_ref[:, 2 * c:3 * c] = (dcg * sig_b).astype(dz_ref.dtype)
        dz_ref[:, 3 * c:4 * c] = (dcg * ca * sig_b * (1.0 - sig_b)).astype(dz_ref.dtype)

    def col(j):
        return pl.BlockSpec((t, c), lambda i: (nt - 1 - i, j))

    def halo(j, rows_):
        per = t // rows_
        return pl.BlockSpec((rows_, c), lambda i: (jnp.maximum((nt - 1 - i) * per - 1, 0), j))

    def full(shape):
        return pl.BlockSpec(shape, lambda i: (0,) * len(shape))

    params = (lcw, lcb, wab, ba, bx, lam, ccw, ccb, lng, lnb)
    small = [(_LRU_K, c), (1, c), (c, 2 * c), (1, c), (1, c), (1, c), (_CONF_K, c), (1, c), (1, c), (1, c)]
    return _pcall(
        body, args=(dycat, dycat, z, z, z, z, z, z, z, hs, hs, cc, *params), grid=(nt,),
        in_specs=[col(0), col(1),
                  col(0), halo(0, _LRU_HALO), col(1), col(2), halo(2, _CONF_HALO), col(3), halo(3, _CONF_HALO),
                  col(0), halo(0, 8), col(0)]
        + [full(p.shape) for p in params],
        out_specs=[pl.BlockSpec((t, 4 * c), lambda i: (nt - 1 - i, 0))] + [full(sh) for sh in small],
        out_shape=[jax.ShapeDtypeStruct((s, 4 * c), _MXU)] + [jax.ShapeDtypeStruct(sh, _F32) for sh in small],
        scratch_shapes=[pltpu.VMEM((t + _LRU_HALO, c), _F32), pltpu.VMEM((t + 8, c), _F32),
                        pltpu.VMEM((t + _CONF_HALO, c), _F32), pltpu.VMEM((t + _CONF_HALO, c), _F32),
                        pltpu.VMEM((8, c), _F32), pltpu.VMEM((8, c), _F32), pltpu.VMEM((_CONF_HALO, c), _F32),
                        pltpu.VMEM((7, t + _CONF_HALO, c), _F32)],
        sem=("arbitrary",), name=name, comm=comm)


_ATT_T = 512


def _attn_probs(qh, kh, scale):
    sc = _dot(qh, kh, _NT) * scale
    e = jnp.exp(sc - jnp.max(sc, axis=-1, keepdims=True))
    return e / jnp.sum(e, axis=-1, keepdims=True)


def _attn_fwd(q, kv, name):
    s, d = q.shape
    nm = kv.shape[0]
    hd = d // _XA_HEADS
    t = _tile(s, _ATT_T)
    scale = hd ** -0.5

    def body(q_ref, k_ref, v_ref, o_ref):
        for hh in range(_XA_HEADS):
            sl = slice(hh * hd, (hh + 1) * hd)
            p = _attn_probs(q_ref[:, sl], k_ref[:, sl], scale)
            o_ref[:, sl] = _dot(p, v_ref[:, sl], _NN).astype(o_ref.dtype)

    return pl.pallas_call(
        body, grid=(s // t,),
        in_specs=[pl.BlockSpec((t, d), lambda i: (i, 0)), pl.BlockSpec((nm, d), lambda i: (0, 0)),
                  pl.BlockSpec((nm, d), lambda i: (0, 1))],
        out_specs=pl.BlockSpec((t, d), lambda i: (i, 0)),
        out_shape=jax.ShapeDtypeStruct((s, d), _MXU),
        compiler_params=_cparams("parallel"), name=name,
    )(q, kv, kv)


def _attn_bwd(q, kv, do, name):
    s, d = q.shape
    nm = kv.shape[0]
    hd = d // _XA_HEADS
    t = _tile(s, _ATT_T)
    scale = hd ** -0.5

    def body(q_ref, k_ref, v_ref, do_ref, dq_ref, dk_ref, dv_ref):
        @pl.when(pl.program_id(0) == 0)
        def _():
            dk_ref[...] = jnp.zeros_like(dk_ref)
            dv_ref[...] = jnp.zeros_like(dv_ref)

        for hh in range(_XA_HEADS):
            sl = slice(hh * hd, (hh + 1) * hd)
            qh = q_ref[:, sl]
            kh = k_ref[:, sl]
            doh = do_ref[:, sl]
            p = _attn_probs(qh, kh, scale)
            dp = _dot(doh, v_ref[:, sl], _NT)
            dv_ref[:, sl] += _dot(p, doh, _TN)
            ds = (p * (dp - jnp.sum(dp * p, axis=-1, keepdims=True)) * scale).astype(_MXU)
            dq_ref[:, sl] = _dot(ds, kh, _NN).astype(dq_ref.dtype)
            dk_ref[:, sl] += _dot(ds, qh, _TN)

    row = pl.BlockSpec((t, d), lambda i: (i, 0))
    return pl.pallas_call(
        body, grid=(s // t,),
        in_specs=[row, pl.BlockSpec((nm, d), lambda i: (0, 0)), pl.BlockSpec((nm, d), lambda i: (0, 1)), row],
        out_specs=[row, pl.BlockSpec((nm, d), lambda i: (0, 0)), pl.BlockSpec((nm, d), lambda i: (0, 0))],
        out_shape=[jax.ShapeDtypeStruct((s, d), _MXU), jax.ShapeDtypeStruct((nm, d), _F32),
                   jax.ShapeDtypeStruct((nm, d), _F32)],
        compiler_params=_cparams("arbitrary"), name=name,
    )(q, kv, kv, do)


_FFN_K = 3
_FFN_FUSED_T = 256
_VMEM_LIMIT_FUSED = 58 * 1024 * 1024


def _ffn_fused_fwd(h3, w_up, w_down, fcw, fcb, x2, target, gfin, name):
    s, d = h3.shape
    nblk, _, n = w_up.shape
    half = nblk // 2
    f = half * n
    t = _tile(s, _FFN_FUSED_T)

    def body(h_ref, wup_ref, wdown_ref, w_ref, b_ref, x2_ref, t_ref, g_ref,
             g0_ref, act_ref, gl_ref, udgl_ref, dx_ref, dxb_ref, l_ref, dg_ref, ext0_ref, ext1_ref, halo_ref):
        i = pl.program_id(0)
        first = i == 0
        h = h_ref[...]
        total = None
        for j in range(half):
            cs = slice(j * n, (j + 1) * n)
            ext_ref = ext0_ref if j % 2 == 0 else ext1_ref
            g0 = _dot(h, wup_ref[j], _NN)
            u = _dot(h, wup_ref[half + j], _NN)
            g0_ref[:, cs] = g0.astype(g0_ref.dtype)
            ext_ref[0:8, :] = jnp.where(first, 0.0, halo_ref[:, cs])
            ext_ref[8:8 + t, :] = g0
            halo_ref[:, cs] = g0[t - 8:t, :]
            g = _causal_conv(ext_ref, 8, w_ref.at[:, cs], b_ref.at[:, cs], _FFN_K, t)
            gl, dgl = _gelu_and_grad(g)
            gl_ref[:, cs] = gl.astype(gl_ref.dtype)
            udgl_ref[:, cs] = (u * dgl).astype(udgl_ref.dtype)
            act = (gl * u).astype(act_ref.dtype)
            act_ref[:, cs] = act
            p = _dot(act, wdown_ref[cs, :], _NN)
            total = p if total is None else total + p
        _epi_final(total, (x2_ref, t_ref, g_ref), (dx_ref, dxb_ref, l_ref, dg_ref), first)

    def const(shape):
        return pl.BlockSpec(shape, lambda i: (0,) * len(shape), pipeline_mode=pl.Buffered(1))

    row = pl.BlockSpec((t, d), lambda i: (i, 0))
    vec = pl.BlockSpec((1, d), lambda i: (0, 0))
    return pl.pallas_call(
        body, grid=(s // t,),
        in_specs=[row, const(w_up.shape), const(w_down.shape), const(fcw.shape), const(fcb.shape), row, row, vec],
        out_specs=[pl.BlockSpec((t, f), lambda i: (i, 0))] * 4 + [row, row, vec, vec],
        out_shape=[jax.ShapeDtypeStruct((s, f), _MXU)] * 4
        + [jax.ShapeDtypeStruct((s, d), _F32), jax.ShapeDtypeStruct((s, d), _MXU),
                   jax.ShapeDtypeStruct((1, d), _F32), jax.ShapeDtypeStruct((1, d), _F32)],
        scratch_shapes=[pltpu.VMEM((t + 8, n), _F32), pltpu.VMEM((t + 8, n), _F32), pltpu.VMEM((8, f), _F32)],
        compiler_params=_cparams("arbitrary", vmem=_VMEM_LIMIT_FUSED), name=name,
    )(h3, w_up, w_down, fcw, fcb, x2, target, gfin)


def _ffn_fused_bwd(dx3b, g0, gl, udgl, w_down, w_up, fcw, x2, gnorm, dx3, name, comm=None):
    s, d = x2.shape
    nblk, _, n = w_up.shape
    half = nblk // 2
    f = half * n
    t = _tile(s, _FFN_FUSED_T)
    nt = s // t
    hrows = 16

    def body(dxb_ref, g0_ref, g0h_ref, gl_ref, udgl_ref, wdown_ref, wup_ref, w_ref, x2_ref, g_ref, dx3_ref,
             dgu_ref, dx2_ref, dx2b_ref, dgn_ref, dw_ref, db_ref, ext0_ref, ext1_ref, up0_ref, up1_ref, car_ref):
        i = pl.program_id(0)
        first_tile = i == nt - 1
        last_tile = i == 0

        @pl.when(last_tile)
        def _():
            dw_ref[...] = jnp.zeros_like(dw_ref)
            db_ref[...] = jnp.zeros_like(db_ref)
            car_ref[...] = jnp.zeros_like(car_ref)

        dxb = dxb_ref[...]
        total = None
        for j in range(half):
            cs = slice(j * n, (j + 1) * n)
            us = slice(f + j * n, f + (j + 1) * n)
            ext_ref = ext0_ref if j % 2 == 0 else ext1_ref
            up_ref = up0_ref if j % 2 == 0 else up1_ref
            dact = _dot(dxb, wdown_ref[cs, :], _NT)
            ext_ref[0:8, :] = jnp.where(first_tile, 0.0, g0h_ref[:, cs].astype(_F32)[hrows - 8:hrows])
            ext_ref[8:8 + t, :] = g0_ref[:, cs].astype(_F32)
            du = (dact * gl_ref[:, cs].astype(_F32)).astype(dgu_ref.dtype)
            dgu_ref[:, us] = du
            dg = dact * udgl_ref[:, cs].astype(_F32)
            db_ref[:, cs] += _colsum(dg)
            for k in range(_FFN_K):
                dw_ref[k:k + 1, cs] += _colsum(dg * ext_ref[pl.ds(8 - (_FFN_K - 1) + k, t), :])
            up_ref[0:t, :] = dg
            up_ref[t:t + 8, :] = car_ref[:, cs]
            car_ref[:, cs] = dg[0:8, :]
            dg0 = w_ref[0:1, cs] * up_ref[pl.ds(_FFN_K - 1, t), :]
            for k in range(1, _FFN_K):
                dg0 = dg0 + w_ref[k:k + 1, cs] * up_ref[pl.ds(_FFN_K - 1 - k, t), :]
            dg0 = dg0.astype(dgu_ref.dtype)
            dgu_ref[:, cs] = dg0
            p = _dot(dg0, wup_ref[j], _NT) + _dot(du, wup_ref[half + j], _NT)
            total = p if total is None else total + p
        _epi_rms_bwd(total, (x2_ref, g_ref, dx3_ref), (dx2_ref, dx2b_ref, dgn_ref), last_tile)

    def const(shape):
        return pl.BlockSpec(shape, lambda i: (0,) * len(shape), pipeline_mode=pl.Buffered(1))

    row = pl.BlockSpec((t, d), lambda i: (nt - 1 - i, 0))
    vec = pl.BlockSpec((1, d), lambda i: (0, 0))
    per = t // hrows
    wide = pl.BlockSpec((t, f), lambda i: (nt - 1 - i, 0))
    return _pcall(
        body, args=(dx3b, g0, g0, gl, udgl, w_down, w_up, fcw, x2, gnorm, dx3), grid=(nt,),
        in_specs=[row, wide, pl.BlockSpec((hrows, f), lambda i: (jnp.maximum((nt - 1 - i) * per - 1, 0), 0)),
                  wide, wide, const(w_down.shape), const(w_up.shape), const(fcw.shape), row, vec, row],
        out_specs=[pl.BlockSpec((t, 2 * f), lambda i: (nt - 1 - i, 0)), row, row, vec,
                   pl.BlockSpec((_FFN_K, f), lambda i: (0, 0)), pl.BlockSpec((1, f), lambda i: (0, 0))],
        out_shape=[jax.ShapeDtypeStruct((s, 2 * f), _MXU), jax.ShapeDtypeStruct((s, d), _F32),
                   jax.ShapeDtypeStruct((s, d), _MXU), jax.ShapeDtypeStruct((1, d), _F32),
                   jax.ShapeDtypeStruct((_FFN_K, f), _F32), jax.ShapeDtypeStruct((1, f), _F32)],
        scratch_shapes=[pltpu.VMEM((t + 8, n), _F32), pltpu.VMEM((t + 8, n), _F32),
                        pltpu.VMEM((t + 8, n), _F32), pltpu.VMEM((t + 8, n), _F32), pltpu.VMEM((8, f), _F32)],
        sem=("arbitrary",), name=name, comm=comm, vmem=_VMEM_LIMIT_FUSED)


def _mesh_pos():
    return lax.axis_index("x"), lax.axis_index("y"), lax.axis_index("c")


def _flip(v, bit):
    return 1 - v if bit else v


def _sem_scratch(n):
    return [pltpu.SemaphoreType.DMA((7 * n,)), pltpu.SemaphoreType.DMA((7 * n,)), pltpu.SemaphoreType.DMA((n,))]


class _Gather:
    def __init__(self, xs):
        self.ins = list(xs)
        self.outs = [jax.ShapeDtypeStruct((_NDEV,) + v.shape, v.dtype) for v in xs]
        self.scratch = _sem_scratch(len(xs))

    def _plan(self, x_refs, out_refs, sems):
        send_sems, recv_sems, local_sems = sems
        x, y, c = _mesh_pos()
        me, sibling = (x, y, c), (x, y, 1 - c)
        chips = [(1 - x, y), (x, 1 - y), (1 - x, 1 - y)]

        def copy(a, k, block, to, src=None):
            slot = out_refs[a].at[4 * block[0] + 2 * block[1] + block[2]]
            return pltpu.make_async_remote_copy(
                src_ref=slot if src is None else src, dst_ref=slot,
                send_sem=send_sems.at[a * 7 + k], recv_sem=recv_sems.at[a * 7 + k],
                device_id=to, device_id_type=_MESH_ID)

        def own(a):
            return pltpu.make_async_copy(x_refs[a], out_refs[a].at[4 * x + 2 * y + c], local_sems.at[a])

        def first(a):
            return [copy(a, 0, me, sibling, src=x_refs[a])] + [
                copy(a, 1 + j, me, (*chip, c), src=x_refs[a]) for j, chip in enumerate(chips)]

        return me, sibling, chips, c, copy, own, first

    def start(self, x_refs, out_refs, sems):
        _, _, _, _, _, own, first = self._plan(x_refs, out_refs, sems)
        for a in range(len(self.ins)):
            own(a).start()
            for cp in first(a):
                cp.start()

    def finish(self, x_refs, out_refs, sems):
        me, sibling, chips, c, copy, own, first = self._plan(x_refs, out_refs, sems)
        n = len(self.ins)
        passed = []
        for a in range(n):
            for j, chip in enumerate(chips):
                copy(a, 1 + j, (*chip, c), me).wait_recv()
                fwd = copy(a, 4 + j, (*chip, c), sibling)
                fwd.start()
                passed.append(fwd)
        for a in range(n):
            copy(a, 0, sibling, me).wait_recv()
            for j, chip in enumerate(chips):
                copy(a, 4 + j, (*chip, 1 - c), me).wait_recv()
        for a in range(n):
            for cp in first(a):
                cp.wait_send()
        for cp in passed:
            cp.wait_send()
        for a in range(n):
            own(a).wait()


class _Exchange:
    def __init__(self, gs):
        self.ins = list(gs)
        self.outs = [jax.ShapeDtypeStruct(v.shape, v.dtype) for v in gs]
        self.scratch = _sem_scratch(len(gs))

    def _plan(self, g_refs, r_refs, sems):
        send_sems, recv_sems, local_sems = sems
        x, y, c = _mesh_pos()
        me_idx = 4 * x + 2 * y + c
        n = len(self.ins)

        def copy(a, k):
            peer = (_flip(x, k & 4), _flip(y, k & 2), _flip(c, k & 1))
            peer_idx = 4 * peer[0] + 2 * peer[1] + peer[2]
            return pltpu.make_async_remote_copy(
                src_ref=g_refs[a].at[peer_idx], dst_ref=r_refs[a].at[me_idx],
                send_sem=send_sems.at[a * 7 + k - 1], recv_sem=recv_sems.at[a * 7 + k - 1],
                device_id=peer, device_id_type=_MESH_ID)

        copies = [copy(a, k) for a in range(n) for k in range(1, _NDEV)]
        mine = [pltpu.make_async_copy(g_refs[a].at[me_idx], r_refs[a].at[me_idx], local_sems.at[a])
                for a in range(n)]
        return copies, mine

    def start(self, g_refs, r_refs, sems):
        copies, mine = self._plan(g_refs, r_refs, sems)
        for cp in copies + mine:
            cp.start()

    def finish(self, g_refs, r_refs, sems):
        copies, mine = self._plan(g_refs, r_refs, sems)
        for cp in copies:
            cp.wait_recv()
        for cp in copies:
            cp.wait_send()
        for cp in mine:
            cp.wait()


class _Both:
    def __init__(self, first, second):
        self.parts = (first, second)
        self.ins = first.ins + second.ins
        self.outs = first.outs + second.outs
        self.scratch = first.scratch + second.scratch

    def _split(self, ins, outs, sems):
        a, b = self.parts
        na, nb = len(a.ins), len(a.scratch)
        return (a, ins[:na], outs[:na], sems[:nb]), (b, ins[na:], outs[na:], sems[nb:])

    def start(self, ins, outs, sems):
        for part, i, o, s in self._split(ins, outs, sems):
            part.start(i, o, s)

    def finish(self, ins, outs, sems):
        for part, i, o, s in self._split(ins, outs, sems):
            part.finish(i, o, s)


def _comm_call(comm, name):
    def body(*refs):
        n_i, n_o = len(comm.ins), len(comm.outs)
        ins, outs, sems = refs[:n_i], refs[n_i:n_i + n_o], refs[n_i + n_o:]
        comm.start(ins, outs, sems)
        comm.finish(ins, outs, sems)

    return pl.pallas_call(
        body, out_shape=list(comm.outs), in_specs=[_ANY] * len(comm.ins), out_specs=[_ANY] * len(comm.outs),
        scratch_shapes=list(comm.scratch), name=name)(*comm.ins)


def _adamw_math(w, g, m, v):
    m = _ADAM_B1 * m + (1.0 - _ADAM_B1) * g
    v = _ADAM_B2 * v + (1.0 - _ADAM_B2) * (g * g)
    m_hat = m / (1.0 - _ADAM_B1 ** _ADAM_STEP)
    v_hat = v / (1.0 - _ADAM_B2 ** _ADAM_STEP)
    delta = -_ADAM_LR * (m_hat / (jnp.sqrt(v_hat) + _ADAM_EPS) + _ADAM_WD * w)
    return delta, m, v


def _sum_adamw(parts, w, m, v, name):
    r, c = w.shape
    tr = _tile(r, 128)

    def body(p_ref, w_ref, m_ref, v_ref, g_ref, d_ref, nm_ref, nv_ref):
        g = p_ref[0].astype(_F32)
        for j in range(1, _NDEV):
            g = g + p_ref[j].astype(_F32)
        delta, nm, nv = _adamw_math(w_ref[...], g, m_ref[...], v_ref[...])
        g_ref[...] = g
        d_ref[...] = delta
        nm_ref[...] = nm
        nv_ref[...] = nv

    blk = pl.BlockSpec((tr, c), lambda i: (i, 0))
    return pl.pallas_call(
        body, grid=(r // tr,),
        in_specs=[pl.BlockSpec((_NDEV, tr, c), lambda i: (0, i, 0)), blk, blk, blk],
        out_specs=[blk] * 4, out_shape=[jax.ShapeDtypeStruct((r, c), _F32)] * 4,
        compiler_params=_cparams("parallel"), name=name,
    )(parts, w, m, v)


def _sum8(parts, name):
    _, r, c = parts.shape

    def body(p_ref, o_ref):
        g = p_ref[0]
        for j in range(1, _NDEV):
            g = g + p_ref[j]
        o_ref[...] = g

    return pl.pallas_call(
        body, grid=(1,), in_specs=[pl.BlockSpec((_NDEV, r, c), lambda i: (0, 0, 0))],
        out_specs=pl.BlockSpec((r, c), lambda i: (0, 0)), out_shape=jax.ShapeDtypeStruct((r, c), _F32),
        compiler_params=_cparams("arbitrary"), name=name,
    )(parts)


def _adamw_flat(g, w, m, v, name):
    r, c = w.shape

    def body(g_ref, w_ref, m_ref, v_ref, d_ref, nm_ref, nv_ref):
        delta, nm, nv = _adamw_math(w_ref[...], g_ref[...], m_ref[...], v_ref[...])
        d_ref[...] = delta
        nm_ref[...] = nm
        nv_ref[...] = nv

    blk = pl.BlockSpec((r, c), lambda i: (0, 0))
    return pl.pallas_call(
        body, grid=(1,), in_specs=[blk] * 4, out_specs=[blk] * 3,
        out_shape=[jax.ShapeDtypeStruct((r, c), _F32)] * 3,
        compiler_params=_cparams("arbitrary"), name=name,
    )(g, w, m, v)


def _pack(arrs):
    flat = jnp.concatenate([a.reshape(-1).astype(_F32) for a in arrs])
    pad = (-flat.shape[0]) % 1024
    return jnp.pad(flat, (0, pad)).reshape(-1, 128)


def _unpack(flat2d, shapes):
    flat = flat2d.reshape(-1)
    out, off = [], 0
    for sh in shapes:
        size = 1
        for dim in sh:
            size *= dim
        out.append(flat[off:off + size].reshape(sh))
        off += size
    return out


def _block_diag(w):
    h, hd, _ = w.shape
    eye = jnp.eye(h, dtype=w.dtype)
    return (eye[:, None, :, None] * w[:, :, None, :]).reshape(h * hd, h * hd)


def _diag_blocks(full, h):
    hd = full.shape[0] // h
    return jnp.stack([full[i * hd:(i + 1) * hd, i * hd:(i + 1) * hd] for i in range(h)])


def kernel(x, mem, mix_norm_g, w_in, lru_conv_w, lru_conv_b, lru_w_a, lru_b_a, lru_w_x, lru_b_x, lru_lambda, conf_conv_w, conf_conv_b, conf_ln_g, conf_ln_b, w_out, xa_norm_g, mem_norm_g, w_q, w_kv, w_o, ffn_norm_g, w_up, ffn_conv_w, ffn_conv_b, w_down, final_norm_g, loss_target, m_mix_norm_g, m_w_in, m_lru_conv_w, m_lru_conv_b, m_lru_w_a, m_lru_b_a, m_lru_w_x, m_lru_b_x, m_lru_lambda, m_conf_conv_w, m_conf_conv_b, m_conf_ln_g, m_conf_ln_b, m_w_out, m_xa_norm_g, m_mem_norm_g, m_w_q, m_w_kv, m_w_o, m_ffn_norm_g, m_w_up, m_ffn_conv_w, m_ffn_conv_b, m_w_down, m_final_norm_g, v_mix_norm_g, v_w_in, v_lru_conv_w, v_lru_conv_b, v_lru_w_a, v_lru_b_a, v_lru_w_x, v_lru_b_x, v_lru_lambda, v_conf_conv_w, v_conf_conv_b, v_conf_ln_g, v_conf_ln_b, v_w_out, v_xa_norm_g, v_mem_norm_g, v_w_q, v_w_kv, v_w_o, v_ffn_norm_g, v_w_up, v_ffn_conv_w, v_ffn_conv_b, v_w_down, v_final_norm_g):
    names = ["mix_norm_g", "w_in", "lru_conv_w", "lru_conv_b", "lru_w_a", "lru_b_a", "lru_w_x", "lru_b_x",
             "lru_lambda", "conf_conv_w", "conf_conv_b", "conf_ln_g", "conf_ln_b", "w_out", "xa_norm_g",
             "mem_norm_g", "w_q", "w_kv", "w_o", "ffn_norm_g", "w_up", "ffn_conv_w", "ffn_conv_b", "w_down",
             "final_norm_g"]
    loc = locals()
    W = {n: loc[n] for n in names}
    M = {n: loc["m_" + n] for n in names}
    V = {n: loc["v_" + n] for n in names}
    big = ["w_in", "w_out", "w_q", "w_kv", "w_o", "w_up", "w_down"]
    conv_sharded = ["lru_conv_w", "conf_conv_w", "ffn_conv_w"]

    xs = x[0]
    mems = mem[0]
    tgt = loss_target[0]
    me = 4 * lax.axis_index("x") + 2 * lax.axis_index("y") + lax.axis_index("c")

    conv_shapes = [W[n].shape[1:] for n in conv_sharded]
    conv_pack = _pack([W[n][0] for n in conv_sharded])
    shard = {n: W[n][0].astype(_XFER) for n in big}
    h1, (g_in, g_out, g_conv) = _rms_fwd(
        xs, mix_norm_g, "rms1_fwd", comm=_Gather([shard["w_in"], shard["w_out"], conv_pack]))
    convs = [[] for _ in conv_sharded]
    for j in range(_NDEV):
        for idx, part in enumerate(_unpack(g_conv[j], conv_shapes)):
            convs[idx].append(part)
    lcw, ccw, fcw = [jnp.concatenate(parts, axis=-1) for parts in convs]

    wab = jnp.concatenate([_block_diag(lru_w_a[0]), _block_diag(lru_w_x[0])], axis=1).astype(_MXU)
    mixer_params = (lcw, lru_conv_b, wab, lru_b_a, lru_b_x, lru_lambda, ccw, conf_conv_b, conf_ln_g, conf_ln_b)

    w_out_f = g_out.reshape(-1, g_out.shape[-1])
    (z, ycat, hs, cc, x1, h2), (g_q, g_kv, g_o, g_up, g_down) = _mixer_fwd(
        xs, h1, g_in, w_out_f, xa_norm_g, *mixer_params, "mixer_fwd",
        comm=_Gather([shard[n] for n in ("w_q", "w_kv", "w_o", "w_up", "w_down")]))
    w_q_f = g_q.reshape(-1, g_q.shape[-1])
    w_o_f = g_o.reshape(-1, g_o.shape[-1])
    w_down_f = g_down.reshape(-1, g_down.shape[-1])
    row32, row16, vec32 = (_F32, "row"), (_MXU, "row"), (_F32, "vec")
    mn = _rms_fwd(mems, mem_norm_g, "rmsm_fwd")
    q = _mm_nn_nat(h2, w_q_f, "mm_q_fwd", outs=[row16])
    kv = _mm_nn_stacked(mn, g_kv, _MXU, "mm_kv_fwd")
    o = _attn_fwd(q, kv, "attn_fwd")
    x2, h3 = _mm_nn_nat(o, w_o_f, "mm_o_fwd", epi=_epi_residual_rms,
                        extra=[(x1, "row"), (ffn_norm_g, "vec")], outs=[row32, row16])

    gfin = final_norm_g.reshape(1, -1)
    g0, act, gelu_g, u_dgelu, dx3, dx3b, lvec, dg_final = _ffn_fused_fwd(
        h3, g_up, w_down_f, fcw, ffn_conv_b, x2, tgt, gfin, "ffn_fwd")
    loss_local = 0.5 * jnp.sum(lvec) / xs.shape[1]
    loss = lax.psum(loss_local, _AXES)

    def rows8(p):
        return p.reshape(_NDEV, p.shape[0] // _NDEV, p.shape[1])

    p_down = _mm_tn_nat(act, dx3b, _XFER, "mm_down_wgrad", ts=2048)
    (dgu, dx2, dx2b, dg_ffn, dfcw, dfcb), (r_down,) = _ffn_fused_bwd(
        dx3b, g0, gelu_g, u_dgelu, w_down_f, g_up, fcw, x2, ffn_norm_g, dx3, "ffn_bwd",
        comm=_Exchange([rows8(p_down)]))
    p_up = _mm_tn_stacked(h3, dgu, _NDEV, _XFER, "mm_up_wgrad", slabs=2)

    do = _mm_nt_nat(dx2b, w_o_f, "mm_o_dgrad", outs=[row16])
    p_o = _mm_tn_nat(o, dx2b, _XFER, "mm_o_wgrad", ts=2048)
    dq, dk, dv = _attn_bwd(q, kv, do, "attn_bwd")
    dx1, dx1b, dg_xa = _mm_nt_nat(
        dq, w_q_f, "mm_q_dgrad", epi=_epi_rms_bwd, extra=[(x1, "row"), (xa_norm_g, "vec"), (dx2, "row")],
        outs=[row32, row16, vec32], tm=512)
    p_q = _mm_tn_nat(h2, dq, _XFER, "mm_q_wgrad", ts=2048)
    dkv = jnp.concatenate([dk, dv], axis=1).astype(_MXU)
    dmn = _mm_nt_stacked(dkv, g_kv, "mm_kv_dgrad", outs=[row32], slabs=_NDEV)
    p_kv = _mm_tn_stacked(mn, dkv, _NDEV, _XFER, "mm_kv_wgrad", slabs=_NDEV)
    _, _, dg_mem = _rms_bwd(dmn, mems, mem_norm_g, None, "rmsm_bwd")

    dycat = _mm_nt_nat(dx1b, w_out_f, "mm_out_dgrad", outs=[row32])
    p_out = _mm_tn_nat(ycat, dx1b, _XFER, "mm_out_wgrad", ts=2048)
    ((dz, dlcw, dlcb, dwab, dba, dbx, dlam, dccw, dccb, dlng, dlnb), (r_up, r_o, r_q, r_kv, r_out)) = _mixer_bwd(
        dycat, z, hs, cc, *mixer_params, "mixer_bwd",
        comm=_Exchange([p_up, rows8(p_o), rows8(p_q), p_kv, rows8(p_out)]))
    p_in = _mm_tn_stacked(h1, dz, _NDEV, _XFER, "mm_in_wgrad", slabs=_NDEV)

    c = _D_LRU
    heads = lru_w_a.shape[1]
    small_partial = {
        "lru_conv_w": dlcw, "lru_conv_b": dlcb,
        "lru_w_a": _diag_blocks(dwab[:, :c], heads), "lru_b_a": dba,
        "lru_w_x": _diag_blocks(dwab[:, c:], heads), "lru_b_x": dbx, "lru_lambda": dlam,
        "conf_conv_w": dccw, "conf_conv_b": dccb, "conf_ln_g": dlng, "conf_ln_b": dlnb,
        "xa_norm_g": dg_xa, "mem_norm_g": dg_mem, "ffn_norm_g": dg_ffn,
        "ffn_conv_w": dfcw, "ffn_conv_b": dfcb, "final_norm_g": dg_final,
    }
    early = list(small_partial)
    early_shapes = [small_partial[n].shape for n in early]
    (grad_x, dg_mix), (r_in, early_all) = _mm_nt_stacked(
        dz, g_in, "mm_in_dgrad", epi=_epi_rms_bwd, extra=[(xs, "row"), (mix_norm_g, "vec"), (dx1, "row")],
        outs=[row32, vec32], tm=512, slabs=_NDEV,
        comm=_Both(_Exchange([p_in]), _Gather([_pack([small_partial[n] for n in early])])))
    (mix_all,) = _comm_call(_Gather([dg_mix]), "gather_mix_grad")
    small = early + ["mix_norm_g"]
    small_sum = _unpack(_sum8(early_all, "sum_small_grads"), early_shapes)
    small_sum.append(_sum8(mix_all.reshape(_NDEV, 8, -1), "sum_mix_grad").reshape(dg_mix.shape))
    received = {"w_in": r_in, "w_out": r_out, "w_q": r_q, "w_kv": r_kv, "w_o": r_o, "w_up": r_up,
                "w_down": r_down}

    grads, deltas, new_m, new_v = {}, {}, {}, {}
    for n, rec in ((n, received[n]) for n in big):
        shp = W[n].shape
        w2, m2, v2 = (t.reshape(rec.shape[1:]) for t in (W[n], M[n], V[n]))
        outs = _sum_adamw(rec, w2, m2, v2, "adamw_" + n)
        grads[n], deltas[n], new_m[n], new_v[n] = (t.reshape(shp) for t in outs)

    small_g = []
    for n, g in zip(small, small_sum):
        if n in conv_sharded:
            width = W[n].shape[-1]
            g = lax.dynamic_slice_in_dim(g, me * width, width, axis=1)
        small_g.append(g.reshape(W[n].shape))
    small_shapes = [W[n].shape for n in small]
    sd, sm, sv = _adamw_flat(_pack(small_g), _pack([W[n] for n in small]), _pack([M[n] for n in small]),
                             _pack([V[n] for n in small]), "adamw_small")
    for n, g, d_, m_, v_ in zip(small, small_g, _unpack(sd, small_shapes), _unpack(sm, small_shapes),
                                _unpack(sv, small_shapes)):
        grads[n], deltas[n], new_m[n], new_v[n] = g, d_, m_, v_

    return (loss, grad_x[None], *[grads[n] for n in names], *[deltas[n] for n in names],
            *[new_m[n] for n in names], *[new_v[n] for n in names])
```

```python
import functools

import jax
import jax.numpy as jnp
from jax import lax
from jax.experimental import pallas as pl
from jax.experimental.pallas import tpu as pltpu

_MXU = jnp.bfloat16
_XFER = jnp.bfloat16
_F32 = jnp.float32
_EPS = 1e-6
_NDEV = 8
_AXES = ("x", "y", "c")
_VMEM_LIMIT = 48 * 1024 * 1024

_D_LRU = 512
_XA_HEADS = 4
_RG_C = 8.0
_ADAM_LR, _ADAM_B1, _ADAM_B2, _ADAM_EPS, _ADAM_WD, _ADAM_STEP = 0.001, 0.9, 0.999, 1e-08, 0.01, 10

_MESH_ID = pl.DeviceIdType.MESH
_ANY = pl.BlockSpec(memory_space=pl.ANY)


def _cparams(*sem, vmem=_VMEM_LIMIT):
    return pltpu.CompilerParams(dimension_semantics=tuple(sem), vmem_limit_bytes=vmem)


def _pcall(body, *, args, grid, in_specs, out_specs, out_shape, sem, name, scratch_shapes=(), comm=None,
           vmem=_VMEM_LIMIT):
    outs_l = list(out_shape) if isinstance(out_shape, (list, tuple)) else [out_shape]
    ospecs_l = list(out_specs) if isinstance(out_specs, (list, tuple)) else [out_specs]
    n_in, n_out, n_scr = len(args), len(outs_l), len(scratch_shapes)
    if comm is None:
        res = pl.pallas_call(
            body, grid=grid, in_specs=list(in_specs), out_specs=ospecs_l, out_shape=outs_l,
            scratch_shapes=list(scratch_shapes), compiler_params=_cparams(*sem, vmem=vmem), name=name)(*args)
        return list(res), []
    n_ci, n_co = len(comm.ins), len(comm.outs)

    def wrapped(*refs):
        ins, cins = refs[:n_in], refs[n_in:n_in + n_ci]
        o = n_in + n_ci
        outs, couts = refs[o:o + n_out], refs[o + n_out:o + n_out + n_co]
        s = o + n_out + n_co
        scr, cscr = refs[s:s + n_scr], refs[s + n_scr:]
        first = pl.program_id(0) == 0
        last = pl.program_id(0) == grid[0] - 1
        for ax in range(1, len(grid)):
            first = jnp.logical_and(first, pl.program_id(ax) == 0)
            last = jnp.logical_and(last, pl.program_id(ax) == grid[ax] - 1)

        @pl.when(first)
        def _():
            comm.start(cins, couts, cscr)

        body(*ins, *outs, *scr)

        @pl.when(last)
        def _():
            comm.finish(cins, couts, cscr)

    res = pl.pallas_call(
        wrapped, grid=grid, in_specs=list(in_specs) + [_ANY] * n_ci, out_specs=ospecs_l + [_ANY] * n_co,
        out_shape=outs_l + list(comm.outs), scratch_shapes=list(scratch_shapes) + list(comm.scratch),
        compiler_params=_cparams(*(("arbitrary",) * len(grid)), vmem=vmem), name=name)(*args, *comm.ins)
    return list(res[:n_out]), list(res[n_out:])


def _sigmoid(v):
    return 1.0 / (1.0 + jnp.exp(-v))


_GELU_C = 0.7978845608028654
_GELU_K = 0.044715


def _gelu(v):
    t = jnp.tanh(_GELU_C * (v + _GELU_K * v * v * v))
    return 0.5 * v * (1.0 + t)


def _gelu_and_grad(v):
    v2 = v * v
    s = 0.5 * jnp.tanh(v * (_GELU_C + (_GELU_C * _GELU_K) * v2)) + 0.5
    g = v * s
    dg = s + (g * (1.0 - s)) * ((2.0 * _GELU_C) + (6.0 * _GELU_C * _GELU_K) * v2)
    return g, dg


def _softplus(v):
    e = jnp.exp(-jnp.abs(v))
    log1p = jnp.where(e < 1e-2, e * (1.0 - e * (0.5 - e * (1.0 / 3.0))), jnp.log(1.0 + e))
    return jnp.maximum(v, 0.0) + log1p


def _neg_expm1(v):
    series = -v * (1.0 + v * (0.5 + v * ((1.0 / 6.0) + v * (1.0 / 24.0))))
    return jnp.where(v > -0.0625, series, 1.0 - jnp.exp(v))


def _dot(a, b, dims):
    return lax.dot_general(a.astype(_MXU), b.astype(_MXU), (dims, ((), ())), preferred_element_type=_F32)


_NN = ((1,), (0,))
_NT = ((1,), (1,))
_TN = ((0,), (0,))


def _scan_fwd(a, b, rows):
    n = a.shape[0]
    d = 1
    while d < n:
        keep = rows >= d
        b = jnp.where(keep, b + a * pltpu.roll(b, d, 0), b)
        a = jnp.where(keep, a * pltpu.roll(a, d, 0), a)
        d *= 2
    return a, b


def _scan_rev(a, b, rows):
    n = a.shape[0]
    d = 1
    while d < n:
        keep = rows < n - d
        b = jnp.where(keep, b + a * pltpu.roll(b, n - d, 0), b)
        a = jnp.where(keep, a * pltpu.roll(a, n - d, 0), a)
        d *= 2
    return a, b


def _colsum(v):
    return jnp.sum(v, axis=0, keepdims=True)


def _mm(a, b, *, dims, grid, a_spec, b_spec, outs, acc_shape, name, extra=(), epi=None, slabs=1, comm=None):
    nred = grid[-1]
    red_axis = len(grid) - 1
    n_ex, n_out = len(extra), len(outs)
    epi = _epi_store if epi is None else epi

    def body(*refs):
        a_ref, b_ref = refs[:2]
        ex, o_refs, acc_ref = refs[2:2 + n_ex], refs[2 + n_ex:2 + n_ex + n_out], refs[-1]
        if slabs == 1:
            p = _dot(a_ref[...], b_ref[...], dims)
        else:
            n = b_ref.shape[-1]
            p = _dot(a_ref[:, 0:n], b_ref[0], dims)
            for jj in range(1, slabs):
                p = p + _dot(a_ref[:, jj * n:(jj + 1) * n], b_ref[jj], dims)

        first_rows = pl.program_id(0) == 0
        if nred == 1:
            epi(p, ex, o_refs, first_rows)
        else:
            k = pl.program_id(red_axis)

            @pl.when(k == 0)
            def _():
                acc_ref[...] = p

            @pl.when(jnp.logical_and(k > 0, k < nred - 1))
            def _():
                acc_ref[...] += p

            @pl.when(k == nred - 1)
            def _():
                epi(acc_ref[...] + p, ex, o_refs, first_rows)

    sem = ("parallel",) * (len(grid) - 1) + ("arbitrary",)
    if any(o[0].shape[0] == 1 for o in outs):
        sem = ("arbitrary",) * len(grid)
    res, cres = _pcall(
        body, args=(a, b) + tuple(e[0] for e in extra), grid=grid,
        in_specs=[a_spec, b_spec] + [e[1] for e in extra],
        out_specs=[o[1] for o in outs], out_shape=[o[0] for o in outs],
        scratch_shapes=[pltpu.VMEM(acc_shape if nred > 1 else (8, 128), _F32)], sem=sem, name=name, comm=comm)
    res = res[0] if n_out == 1 else res
    return res if comm is None else (res, cres)


def _epi_store(total, ex, outs, first_rows):
    outs[0][...] = total.astype(outs[0].dtype)


def _epi_residual_rms(total, ex, outs, first_rows):
    res_ref, g_ref = ex
    xn = total + res_ref[...]
    outs[0][...] = xn
    r = lax.rsqrt(jnp.mean(xn * xn, axis=-1, keepdims=True) + _EPS)
    outs[1][...] = (xn * r * g_ref[...]).astype(outs[1].dtype)


def _epi_rms_bwd(total, ex, outs, first_rows):
    x_ref, g_ref, dres_ref = ex
    dg_ref = outs[-1]
    xv = x_ref[...]
    r = lax.rsqrt(jnp.mean(xv * xv, axis=-1, keepdims=True) + _EPS)
    xhat = xv * r
    dxh = total * g_ref[...]
    dx = dres_ref[...] + r * (dxh - xhat * jnp.mean(dxh * xhat, axis=-1, keepdims=True))
    for o_ref in outs[:-1]:
        o_ref[...] = dx.astype(o_ref.dtype)

    @pl.when(first_rows)
    def _():
        dg_ref[...] = jnp.zeros_like(dg_ref)

    dg_ref[...] += _colsum(total * xhat)


def _epi_final(total, ex, outs, first_rows):
    res_ref, t_ref, g_ref = ex
    dx_ref, dxb_ref, l_ref, dg_ref = outs
    xv = total + res_ref[...]
    gv = g_ref[...]
    d = xv.shape[-1]
    r = lax.rsqrt(jnp.mean(xv * xv, axis=-1, keepdims=True) + _EPS)
    xhat = xv * r
    err = xhat * gv - t_ref[...]
    dy = err * (1.0 / d)
    dxh = dy * gv
    dx = r * (dxh - xhat * jnp.mean(dxh * xhat, axis=-1, keepdims=True))
    dx_ref[...] = dx
    dxb_ref[...] = dx.astype(dxb_ref.dtype)

    @pl.when(first_rows)
    def _():
        l_ref[...] = jnp.zeros_like(l_ref)
        dg_ref[...] = jnp.zeros_like(dg_ref)

    l_ref[...] += _colsum(err * err)
    dg_ref[...] += _colsum(dy * xhat)


def _tile(m, cap):
    t = min(m, cap)
    assert m % t == 0
    return t


def _row_spec(tm, n):
    return pl.BlockSpec((tm, n), lambda i, *_: (i, 0))


def _vec_spec(n):
    return pl.BlockSpec((1, n), lambda *_: (0, 0))


def _row_io(m, n, tm, extra, outs):
    def spec(kind):
        return _row_spec(tm, n) if kind == "row" else _vec_spec(n)

    ex = [(arr, spec(kind)) for arr, kind in extra]
    os_ = [(jax.ShapeDtypeStruct((m, n) if kind == "row" else (1, n), dt), spec(kind)) for dt, kind in outs]
    return ex, os_


def _mm_nn_stacked(a, w, out_dtype, name, comm=None, tm=1024):
    m, k = a.shape
    j, _, n = w.shape
    tm = _tile(m, tm)
    return _mm(a, w, dims=_NN, grid=(m // tm, j, 1),
               a_spec=pl.BlockSpec((tm, k), lambda i, jj, r: (i, 0)),
               b_spec=pl.BlockSpec((None, k, n), lambda i, jj, r: (jj, 0, 0)),
               outs=[(jax.ShapeDtypeStruct((m, j * n), out_dtype), pl.BlockSpec((tm, n), lambda i, jj, r: (i, jj)))],
               acc_shape=(tm, n), name=name, comm=comm)


def _mm_nn_nat(a, w, name, *, outs, extra=(), epi=None, tm=1024):
    m, kt = a.shape
    _, n = w.shape
    tm = _tile(m, tm)
    tk = _tile(kt, 1024)
    ex, os_ = _row_io(m, n, tm, extra, outs)
    return _mm(a, w, dims=_NN, grid=(m // tm, kt // tk),
               a_spec=pl.BlockSpec((tm, tk), lambda i, r: (i, r)),
               b_spec=pl.BlockSpec((tk, n), lambda i, r: (r, 0)),
               outs=os_, extra=ex, epi=epi, acc_shape=(tm, n), name=name)


def _mm_nt_stacked(dc, w, name, *, outs, extra=(), epi=None, comm=None, tm=1024, slabs=1):
    m = dc.shape[0]
    j, k, n = w.shape
    tm = _tile(m, tm)
    assert j % slabs == 0
    ex, os_ = _row_io(m, k, tm, extra, outs)
    wblk = (None, k, n) if slabs == 1 else (slabs, k, n)
    return _mm(dc, w, dims=_NT, grid=(m // tm, j // slabs),
               a_spec=pl.BlockSpec((tm, slabs * n), lambda i, r: (i, r)),
               b_spec=pl.BlockSpec(wblk, lambda i, r: (r, 0, 0)),
               outs=os_, extra=ex, epi=epi, acc_shape=(tm, k), name=name, slabs=slabs, comm=comm)


def _mm_nt_nat(dc, w, name, *, outs, extra=(), epi=None, tm=1024):
    m, n = dc.shape
    kt = w.shape[0]
    tm = _tile(m, tm)
    tkb = _tile(kt, 1024)
    if kt == tkb:
        ex, os_ = _row_io(m, kt, tm, extra, outs)
    else:
        assert not extra and len(outs) == 1
        ex, os_ = [], [(jax.ShapeDtypeStruct((m, kt), outs[0][0]), pl.BlockSpec((tm, tkb), lambda i, kb, r: (i, kb)))]
    return _mm(dc, w, dims=_NT, grid=(m // tm, kt // tkb, 1),
               a_spec=pl.BlockSpec((tm, n), lambda i, kb, r: (i, 0)),
               b_spec=pl.BlockSpec((tkb, n), lambda i, kb, r: (kb, 0)),
               outs=os_, extra=ex, epi=epi, acc_shape=(tm, tkb), name=name)


def _mm_tn_stacked(a, dc, j, out_dtype, name, slabs=1, ts=1024):
    s, k = a.shape
    n = dc.shape[1] // j
    ts = _tile(s, ts)
    assert j % slabs == 0

    def epi(total, ex, outs, first_rows):
        for jj in range(slabs):
            outs[0][jj] = total[:, jj * n:(jj + 1) * n].astype(outs[0].dtype)

    return _mm(a, dc, dims=_TN, grid=(j // slabs, s // ts),
               a_spec=pl.BlockSpec((ts, k), lambda jj, r: (r, 0)),
               b_spec=pl.BlockSpec((ts, slabs * n), lambda jj, r: (r, jj)),
               outs=[(jax.ShapeDtypeStruct((j, k, n), out_dtype),
                      pl.BlockSpec((slabs, k, n), lambda jj, r: (jj, 0, 0)))],
               epi=epi, acc_shape=(k, slabs * n), name=name)


def _mm_tn_nat(a, dc, out_dtype, name, ts=1024):
    s, kt = a.shape
    n = dc.shape[1]
    ts = _tile(s, ts)
    tkb = _tile(kt, 512)
    return _mm(a, dc, dims=_TN, grid=(kt // tkb, s // ts),
               a_spec=pl.BlockSpec((ts, tkb), lambda kb, r: (r, kb)),
               b_spec=pl.BlockSpec((ts, n), lambda kb, r: (r, 0)),
               outs=[(jax.ShapeDtypeStruct((kt, n), out_dtype), pl.BlockSpec((tkb, n), lambda kb, r: (kb, 0)))],
               acc_shape=(tkb, n), name=name)


def _rms_fwd(x, g, name, comm=None):
    s, d = x.shape
    t = _tile(s, 256)

    def body(x_ref, g_ref, h_ref):
        xv = x_ref[...]
        r = lax.rsqrt(jnp.mean(xv * xv, axis=-1, keepdims=True) + _EPS)
        h_ref[...] = (xv * r * g_ref[...]).astype(h_ref.dtype)

    res, cres = _pcall(
        body, args=(x, g), grid=(s // t,),
        in_specs=[pl.BlockSpec((t, d), lambda i: (i, 0)), pl.BlockSpec((1, d), lambda i: (0, 0))],
        out_specs=pl.BlockSpec((t, d), lambda i: (i, 0)),
        out_shape=jax.ShapeDtypeStruct((s, d), _MXU), sem=("parallel",), name=name, comm=comm)
    return res[0] if comm is None else (res[0], cres)


def _rms_bwd(dh, x, g, dres, name):
    s, d = x.shape
    t = _tile(s, 256)
    has_res = dres is not None

    def body(*refs):
        if has_res:
            dh_ref, x_ref, g_ref, dres_ref, dx_ref, dxb_ref, dg_ref = refs
        else:
            dh_ref, x_ref, g_ref, dx_ref, dxb_ref, dg_ref = refs
        xv = x_ref[...]
        dhv = dh_ref[...]
        r = lax.rsqrt(jnp.mean(xv * xv, axis=-1, keepdims=True) + _EPS)
        xhat = xv * r
        dxh = dhv * g_ref[...]
        dx = r * (dxh - xhat * jnp.mean(dxh * xhat, axis=-1, keepdims=True))
        if has_res:
            dx = dx + dres_ref[...]
        dx_ref[...] = dx
        dxb_ref[...] = dx.astype(dxb_ref.dtype)

        @pl.when(pl.program_id(0) == 0)
        def _():
            dg_ref[...] = jnp.zeros_like(dg_ref)

        dg_ref[...] += _colsum(dhv * xhat)

    row = pl.BlockSpec((t, d), lambda i: (i, 0))
    vec = pl.BlockSpec((1, d), lambda i: (0, 0))
    in_specs = [row, row, vec] + ([row] if has_res else [])
    args = (dh, x, g) + ((dres,) if has_res else ())
    return pl.pallas_call(
        body, grid=(s // t,), in_specs=in_specs, out_specs=[row, row, vec],
        out_shape=[jax.ShapeDtypeStruct((s, d), _F32), jax.ShapeDtypeStruct((s, d), _MXU),
                   jax.ShapeDtypeStruct((1, d), _F32)],
        compiler_params=_cparams("arbitrary"), name=name,
    )(*args)


_LRU_K = 4
_CONF_K = 31
_LRU_HALO = 8
_CONF_HALO = 32
_MIX_T = 256


def _lru_gates(lx, wab_ref, ba_ref, bx_ref, lam_ref):
    c = _D_LRU
    pre = _dot(lx, wab_ref[...], _NN)
    r = _sigmoid(pre[:, :c] + ba_ref[...])
    ig = _sigmoid(pre[:, c:] + bx_ref[...])
    sp = _softplus(-lam_ref[...])
    log_a = (-_RG_C) * r * sp
    a = jnp.exp(log_a)
    mult = jnp.sqrt(_neg_expm1(2.0 * log_a))
    return r, ig, sp, a, mult


def _causal_conv(ext_ref, halo, w_ref, b_ref, taps, t):
    acc = b_ref[...] + w_ref[0:1, :] * ext_ref[pl.ds(halo - (taps - 1), t), :]
    for k in range(1, taps):
        acc = acc + w_ref[k:k + 1, :] * ext_ref[pl.ds(halo - (taps - 1) + k, t), :]
    return acc


class _Windows:
    def __init__(self, ext_ref, shifted_ref, t):
        self.ext_ref, self.shifted_ref, self.t = ext_ref, shifted_ref, t
        rows = t + 24
        for r in range(1, 8):
            shifted_ref[r - 1, 0:rows, :] = ext_ref[pl.ds(r, rows), :]

    def __call__(self, off):
        q, r = divmod(off, 8)
        if r == 0:
            return self.ext_ref[pl.ds(8 * q, self.t), :]
        return self.shifted_ref[r - 1, pl.ds(8 * q, self.t), :]


def _mixer_fwd(xs, h1, w_in, w_out, gnorm, lcw, lcb, wab, ba, bx, lam, ccw, ccb, lng, lnb, name, comm=None):
    s, d = xs.shape
    nblk, _, n = w_in.shape
    c = _D_LRU
    t = _tile(s, _MIX_T)
    nt = s // t

    def body(x_ref, h_ref, win_ref, wout_ref, gn_ref,
             lcw_ref, lcb_ref, wab_ref, ba_ref, bx_ref, lam_ref, ccw_ref, ccb_ref, lng_ref, lnb_ref,
             z_ref, ycat_ref, hs_ref, cc_ref, x1_ref, h2_ref,
             ext_ref, cge_ref, hc_ref, shifted_ref, zprev_ref):
        i = pl.program_id(0)
        first = i == 0
        rows = lax.broadcasted_iota(jnp.int32, (t, c), 0)

        @pl.when(first)
        def _():
            zprev_ref[...] = jnp.zeros_like(zprev_ref)

        hv = h_ref[...]
        for j in range(nblk):
            z_ref[:, j * n:(j + 1) * n] = _dot(hv, win_ref[j], _NN)
        lx0_ref, gate_ref = z_ref.at[:, 0:c], z_ref.at[:, c:2 * c]
        ca_ref, cb_ref = z_ref.at[:, 2 * c:3 * c], z_ref.at[:, 3 * c:4 * c]
        lx0h_ref = zprev_ref.at[_CONF_HALO - _LRU_HALO:_CONF_HALO, 0:c]
        cah_ref, cbh_ref = zprev_ref.at[:, 2 * c:3 * c], zprev_ref.at[:, 3 * c:4 * c]

        ext_ref[0:_LRU_HALO, :] = jnp.where(first, 0.0, lx0h_ref[...])
        ext_ref[_LRU_HALO:_LRU_HALO + t, :] = lx0_ref[...]
        lx = _causal_conv(ext_ref, _LRU_HALO, lcw_ref, lcb_ref, _LRU_K, t)
        r, ig, sp, a, mult = _lru_gates(lx, wab_ref, ba_ref, bx_ref, lam_ref)
        u = mult * (ig * lx)
        a_cum, h_loc = _scan_fwd(a, u, rows)

        @pl.when(first)
        def _():
            hc_ref[...] = jnp.zeros_like(hc_ref)

        h = h_loc + a_cum * hc_ref[7:8, :]
        hs_ref[...] = h
        hc_ref[...] = hs_ref[pl.ds(t - 8, 8), :]
        ycat_ref[:, 0:c] = (h * _gelu(gate_ref[...])).astype(ycat_ref.dtype)

        cge_ref[0:_CONF_HALO, :] = jnp.where(first, 0.0, cah_ref[...] * _sigmoid(cbh_ref[...]))
        cge_ref[_CONF_HALO:_CONF_HALO + t, :] = ca_ref[...] * _sigmoid(cb_ref[...])
        win = _Windows(cge_ref, shifted_ref, t)
        first_off = _CONF_HALO - (_CONF_K - 1)
        cc = ccb_ref[...] + ccw_ref[0:1, :] * win(first_off)
        for k in range(1, _CONF_K):
            cc = cc + ccw_ref[k:k + 1, :] * win(first_off + k)
        cc_ref[...] = cc
        xc = cc - jnp.mean(cc, axis=-1, keepdims=True)
        rstd = lax.rsqrt(jnp.mean(xc * xc, axis=-1, keepdims=True) + _EPS)
        ln = xc * rstd * lng_ref[...] + lnb_ref[...]
        ycat_ref[:, c:2 * c] = (ln * _sigmoid(ln)).astype(ycat_ref.dtype)

        zprev_ref[...] = z_ref[pl.ds(t - _CONF_HALO, _CONF_HALO), :]
        y = _dot(ycat_ref[...], wout_ref[...], _NN)
        _epi_residual_rms(y, (x_ref, gn_ref), (x1_ref, h2_ref), first)

    def const(arr):
        return pl.BlockSpec(arr.shape, lambda i: (0,) * arr.ndim, pipeline_mode=pl.Buffered(1))

    def rows_of(width):
        return pl.BlockSpec((t, width), lambda i: (i, 0))

    params = (lcw, lcb, wab, ba, bx, lam, ccw, ccb, lng, lnb)
    res, cres = _pcall(
        body, args=(xs, h1, w_in, w_out, gnorm, *params), grid=(nt,),
        in_specs=[rows_of(d), rows_of(d), const(w_in), const(w_out), const(gnorm)] + [const(p) for p in params],
        out_specs=[rows_of(nblk * n), rows_of(2 * c), rows_of(c), rows_of(c), rows_of(d), rows_of(d)],
        out_shape=[jax.ShapeDtypeStruct((s, nblk * n), _F32), jax.ShapeDtypeStruct((s, 2 * c), _MXU),
                   jax.ShapeDtypeStruct((s, c), _F32), jax.ShapeDtypeStruct((s, c), _F32),
                   jax.ShapeDtypeStruct((s, d), _F32), jax.ShapeDtypeStruct((s, d), _MXU)],
        scratch_shapes=[pltpu.VMEM((t + _LRU_HALO, c), _F32), pltpu.VMEM((t + _CONF_HALO, c), _F32),
                        pltpu.VMEM((8, c), _F32), pltpu.VMEM((7, t + _CONF_HALO, c), _F32),
                        pltpu.VMEM((_CONF_HALO, nblk * n), _F32)],
        sem=("arbitrary",), name=name, comm=comm)
    return res, cres


def _mixer_bwd(dx1b, w_out, h1, z, hs, cc, lcw, lcb, wab, ba, bx, lam, ccw, ccb, lng, lnb, name, comm=None):
    s = z.shape[0]
    d = h1.shape[1]
    c = _D_LRU
    t = _tile(s, _MIX_T)
    nt = s // t
    nblk = z.shape[1] // 256

    def body(dxb_ref, wout_ref, h1_ref, lx0_ref, lx0h_ref, gate_ref, ca_ref, cah_ref, cb_ref, cbh_ref,
             hs_ref, hsh_ref, cc_ref,
             lcw_ref, lcb_ref, wab_ref, ba_ref, bx_ref, lam_ref, ccw_ref, ccb_ref, lng_ref, lnb_ref,
             dz_ref, pin_ref, dlcw_ref, dlcb_ref, dwab_ref, dba_ref, dbx_ref, dlam_ref, dccw_ref, dccb_ref, dlng_ref,
             dlnb_ref,
             ext_ref, up_ref, cge_ref, dce_ref, xc_ref, dlxc_ref, dccc_ref, shifted_ref, dwin_ref):
        i = pl.program_id(0)
        first_tile = i == nt - 1
        last_tile = i == 0
        rows = lax.broadcasted_iota(jnp.int32, (t, c), 0)

        @pl.when(last_tile)
        def _():
            for ref in (dlcw_ref, dlcb_ref, dwab_ref, dba_ref, dbx_ref, dlam_ref, dccw_ref, dccb_ref, dlng_ref,
                        dlnb_ref, xc_ref, dlxc_ref, dccc_ref, dwin_ref):
                ref[...] = jnp.zeros_like(ref)

        dycat = _dot(dxb_ref[...], wout_ref[...], _NT)

        ext_ref[0:_LRU_HALO, :] = jnp.where(first_tile, 0.0, lx0h_ref[...])
        ext_ref[_LRU_HALO:_LRU_HALO + t, :] = lx0_ref[...]
        lx = _causal_conv(ext_ref, _LRU_HALO, lcw_ref, lcb_ref, _LRU_K, t)
        r, ig, sp, a, mult = _lru_gates(lx, wab_ref, ba_ref, bx_ref, lam_ref)
        h = hs_ref[...]
        gl, dgl = _gelu_and_grad(gate_ref[...])
        dyl = dycat[:, 0:c]
        dz_ref[:, c:2 * c] = (dyl * h * dgl).astype(dz_ref.dtype)
        dh = dyl * gl

        up_ref[0:t, :] = a
        up_ref[t:t + 8, :] = jnp.ones((8, c), _F32)
        a_up = up_ref[pl.ds(1, t), :]
        a_cum, g_loc = _scan_rev(a_up, dh, rows)
        gt = g_loc + a_cum * xc_ref[0:1, :]
        xc_ref[...] = (a * gt)[0:8, :]

        up_ref[0:8, :] = jnp.where(first_tile, 0.0, hsh_ref[...])
        up_ref[8:8 + t, :] = h
        hprev = up_ref[pl.ds(7, t), :]

        da = gt * hprev
        dmult = gt * ig * lx
        dig = gt * mult * lx
        dlx = gt * mult * ig
        dlog_a = da * a - dmult * a * a / mult
        dpre_r = dlog_a * (-_RG_C) * sp * r * (1.0 - r)
        dpre_i = dig * ig * (1.0 - ig)
        dlam_ref[...] += _colsum(dlog_a * r) * (_RG_C * _sigmoid(-lam_ref[...]))
        dba_ref[...] += _colsum(dpre_r)
        dbx_ref[...] += _colsum(dpre_i)
        dpre = jnp.concatenate([dpre_r, dpre_i], axis=1).astype(_MXU)
        dlx = dlx + _dot(dpre, wab_ref[...], _NT)
        dwab_ref[...] += _dot(lx, dpre, _TN)

        dlcb_ref[...] += _colsum(dlx)
        up_ref[0:t, :] = dlx
        up_ref[t:t + 8, :] = dlxc_ref[...]
        dlxc_ref[...] = dlx[0:8, :]
        acc = lcw_ref[0:1, :] * up_ref[pl.ds(_LRU_K - 1, t), :]
        for k in range(1, _LRU_K):
            acc = acc + lcw_ref[k:k + 1, :] * up_ref[pl.ds(_LRU_K - 1 - k, t), :]
        dz_ref[:, 0:c] = acc.astype(dz_ref.dtype)
        for k in range(_LRU_K):
            dlcw_ref[k:k + 1, :] += _colsum(dlx * ext_ref[pl.ds(_LRU_HALO - (_LRU_K - 1) + k, t), :])

        sig_b = _sigmoid(cb_ref[...])
        ca = ca_ref[...]
        cge_ref[0:_CONF_HALO, :] = jnp.where(first_tile, 0.0, cah_ref[...] * _sigmoid(cbh_ref[...]))
        cge_ref[_CONF_HALO:_CONF_HALO + t, :] = ca * sig_b
        ccv = cc_ref[...]
        xcen = ccv - jnp.mean(ccv, axis=-1, keepdims=True)
        rstd = lax.rsqrt(jnp.mean(xcen * xcen, axis=-1, keepdims=True) + _EPS)
        xn = xcen * rstd
        ln = xn * lng_ref[...] + lnb_ref[...]
        sg = _sigmoid(ln)
        dln = dycat[:, c:2 * c] * (sg * (1.0 + ln * (1.0 - sg)))
        dlng_ref[...] += _colsum(dln * xn)
        dlnb_ref[...] += _colsum(dln)
        dxn = dln * lng_ref[...]
        dcc = rstd * (dxn - jnp.mean(dxn, axis=-1, keepdims=True)
                      - xn * jnp.mean(dxn * xn, axis=-1, keepdims=True))
        dccb_ref[...] += _colsum(dcc)
        win = _Windows(cge_ref, shifted_ref, t)
        for k in range(_CONF_K):
            dccw_ref[k:k + 1, :] += _colsum(dcc * win(_CONF_HALO - (_CONF_K - 1) + k))
        dce_ref[0:t, :] = dcc
        dce_ref[t:t + _CONF_HALO, :] = dccc_ref[...]
        dccc_ref[...] = dcc[0:_CONF_HALO, :]
        win = _Windows(dce_ref, shifted_ref, t)
        dcg = ccw_ref[0:1, :] * win(_CONF_K - 1)
        for k in range(1, _CONF_K):
            dcg = dcg + ccw_ref[k:k + 1, :] * win(_CONF_K - 1 - k)
        dz_ref[:, 2 * c:3 * c] = (dcg * sig_b).astype(dz_ref.dtype)
        dz_ref[:, 3 * c:4 * c] = (dcg * ca * sig_b * (1.0 - sig_b)).astype(dz_ref.dtype)

        dwin_ref[...] += _dot(h1_ref[...], dz_ref[...], _TN)

        @pl.when(first_tile)
        def _():
            for j in range(nblk):
                pin_ref[j] = dwin_ref[:, j * 256:(j + 1) * 256].astype(pin_ref.dtype)

    def col(j):
        return pl.BlockSpec((t, c), lambda i: (nt - 1 - i, j))

    def halo(j, rows_):
        per = t // rows_
        return pl.BlockSpec((rows_, c), lambda i: (jnp.maximum((nt - 1 - i) * per - 1, 0), j))

    def full(shape):
        return pl.BlockSpec(shape, lambda i: (0,) * len(shape))

    params = (lcw, lcb, wab, ba, bx, lam, ccw, ccb, lng, lnb)
    small = [(_LRU_K, c), (1, c), (c, 2 * c), (1, c), (1, c), (1, c), (_CONF_K, c), (1, c), (1, c), (1, c)]
    wide = pl.BlockSpec((t, d), lambda i: (nt - 1 - i, 0))
    pin_shape = (nblk, d, 256)
    return _pcall(
        body, args=(dx1b, w_out, h1, z, z, z, z, z, z, z, hs, hs, cc, *params), grid=(nt,),
        in_specs=[wide, pl.BlockSpec(w_out.shape, lambda i: (0, 0), pipeline_mode=pl.Buffered(1)), wide,
                  col(0), halo(0, _LRU_HALO), col(1), col(2), halo(2, _CONF_HALO), col(3), halo(3, _CONF_HALO),
                  col(0), halo(0, 8), col(0)]
        + [full(p.shape) for p in params],
        out_specs=[pl.BlockSpec((t, 4 * c), lambda i: (nt - 1 - i, 0)), full(pin_shape)] + [full(sh) for sh in small],
        out_shape=[jax.ShapeDtypeStruct((s, 4 * c), _MXU), jax.ShapeDtypeStruct(pin_shape, _XFER)]
        + [jax.ShapeDtypeStruct(sh, _F32) for sh in small],
        scratch_shapes=[pltpu.VMEM((t + _LRU_HALO, c), _F32), pltpu.VMEM((t + 8, c), _F32),
                        pltpu.VMEM((t + _CONF_HALO, c), _F32), pltpu.VMEM((t + _CONF_HALO, c), _F32),
                        pltpu.VMEM((8, c), _F32), pltpu.VMEM((8, c), _F32), pltpu.VMEM((_CONF_HALO, c), _F32),
                        pltpu.VMEM((7, t + _CONF_HALO, c), _F32), pltpu.VMEM((d, 4 * c), _F32)],
        sem=("arbitrary",), name=name, comm=comm, vmem=_VMEM_LIMIT_FUSED)


_ATT_T = 512


def _attn_probs(qh, kh, scale):
    sc = _dot(qh, kh, _NT) * scale
    e = jnp.exp(sc - jnp.max(sc, axis=-1, keepdims=True))
    return e / jnp.sum(e, axis=-1, keepdims=True)


def _attn_fwd(q, kv, name):
    s, d = q.shape
    nm = kv.shape[0]
    hd = d // _XA_HEADS
    t = _tile(s, _ATT_T)
    scale = hd ** -0.5

    def body(q_ref, k_ref, v_ref, o_ref):
        for hh in range(_XA_HEADS):
            sl = slice(hh * hd, (hh + 1) * hd)
            p = _attn_probs(q_ref[:, sl], k_ref[:, sl], scale)
            o_ref[:, sl] = _dot(p, v_ref[:, sl], _NN).astype(o_ref.dtype)

    return pl.pallas_call(
        body, grid=(s // t,),
        in_specs=[pl.BlockSpec((t, d), lambda i: (i, 0)), pl.BlockSpec((nm, d), lambda i: (0, 0)),
                  pl.BlockSpec((nm, d), lambda i: (0, 1))],
        out_specs=pl.BlockSpec((t, d), lambda i: (i, 0)),
        out_shape=jax.ShapeDtypeStruct((s, d), _MXU),
        compiler_params=_cparams("parallel"), name=name,
    )(q, kv, kv)


def _attn_bwd(q, kv, do, name):
    s, d = q.shape
    nm = kv.shape[0]
    hd = d // _XA_HEADS
    t = _tile(s, _ATT_T)
    scale = hd ** -0.5

    def body(q_ref, k_ref, v_ref, do_ref, dq_ref, dk_ref, dv_ref):
        @pl.when(pl.program_id(0) == 0)
        def _():
            dk_ref[...] = jnp.zeros_like(dk_ref)
            dv_ref[...] = jnp.zeros_like(dv_ref)

        for hh in range(_XA_HEADS):
            sl = slice(hh * hd, (hh + 1) * hd)
            qh = q_ref[:, sl]
            kh = k_ref[:, sl]
            doh = do_ref[:, sl]
            p = _attn_probs(qh, kh, scale)
            dp = _dot(doh, v_ref[:, sl], _NT)
            dv_ref[:, sl] += _dot(p, doh, _TN)
            ds = (p * (dp - jnp.sum(dp * p, axis=-1, keepdims=True)) * scale).astype(_MXU)
            dq_ref[:, sl] = _dot(ds, kh, _NN).astype(dq_ref.dtype)
            dk_ref[:, sl] += _dot(ds, qh, _TN)

    row = pl.BlockSpec((t, d), lambda i: (i, 0))
    return pl.pallas_call(
        body, grid=(s // t,),
        in_specs=[row, pl.BlockSpec((nm, d), lambda i: (0, 0)), pl.BlockSpec((nm, d), lambda i: (0, 1)), row],
        out_specs=[row, pl.BlockSpec((nm, d), lambda i: (0, 0)), pl.BlockSpec((nm, d), lambda i: (0, 0))],
        out_shape=[jax.ShapeDtypeStruct((s, d), _MXU), jax.ShapeDtypeStruct((nm, d), _F32),
                   jax.ShapeDtypeStruct((nm, d), _F32)],
        compiler_params=_cparams("arbitrary"), name=name,
    )(q, kv, kv, do)


_FFN_K = 3
_FFN_FUSED_T = 256
_VMEM_LIMIT_FUSED = 58 * 1024 * 1024


def _ffn_fused_fwd(h3, w_up, w_down, fcw, fcb, x2, target, gfin, name):
    s, d = h3.shape
    nblk, _, n = w_up.shape
    half = nblk // 2
    f = half * n
    t = _tile(s, _FFN_FUSED_T)

    def body(h_ref, wup_ref, wdown_ref, w_ref, b_ref, x2_ref, t_ref, g_ref,
             g0_ref, act_ref, gl_ref, udgl_ref, dx_ref, dxb_ref, l_ref, dg_ref, ext0_ref, ext1_ref, halo_ref):
        i = pl.program_id(0)
        first = i == 0
        h = h_ref[...]
        total = None
        for j in range(half):
            cs = slice(j * n, (j + 1) * n)
            ext_ref = ext0_ref if j % 2 == 0 else ext1_ref
            g0 = _dot(h, wup_ref[j], _NN)
            u = _dot(h, wup_ref[half + j], _NN)
            g0_ref[:, cs] = g0.astype(g0_ref.dtype)
            ext_ref[0:8, :] = jnp.where(first, 0.0, halo_ref[:, cs])
            ext_ref[8:8 + t, :] = g0
            halo_ref[:, cs] = g0[t - 8:t, :]
            g = _causal_conv(ext_ref, 8, w_ref.at[:, cs], b_ref.at[:, cs], _FFN_K, t)
            gl, dgl = _gelu_and_grad(g)
            gl_ref[:, cs] = gl.astype(gl_ref.dtype)
            udgl_ref[:, cs] = (u * dgl).astype(udgl_ref.dtype)
            act = (gl * u).astype(act_ref.dtype)
            act_ref[:, cs] = act
            p = _dot(act, wdown_ref[cs, :], _NN)
            total = p if total is None else total + p
        _epi_final(total, (x2_ref, t_ref, g_ref), (dx_ref, dxb_ref, l_ref, dg_ref), first)

    def const(shape):
        return pl.BlockSpec(shape, lambda i: (0,) * len(shape), pipeline_mode=pl.Buffered(1))

    row = pl.BlockSpec((t, d), lambda i: (i, 0))
    vec = pl.BlockSpec((1, d), lambda i: (0, 0))
    return pl.pallas_call(
        body, grid=(s // t,),
        in_specs=[row, const(w_up.shape), const(w_down.shape), const(fcw.shape), const(fcb.shape), row, row, vec],
        out_specs=[pl.BlockSpec((t, f), lambda i: (i, 0))] * 4 + [row, row, vec, vec],
        out_shape=[jax.ShapeDtypeStruct((s, f), _MXU)] * 4
        + [jax.ShapeDtypeStruct((s, d), _F32), jax.ShapeDtypeStruct((s, d), _MXU),
                   jax.ShapeDtypeStruct((1, d), _F32), jax.ShapeDtypeStruct((1, d), _F32)],
        scratch_shapes=[pltpu.VMEM((t + 8, n), _F32), pltpu.VMEM((t + 8, n), _F32), pltpu.VMEM((8, f), _F32)],
        compiler_params=_cparams("arbitrary", vmem=_VMEM_LIMIT_FUSED), name=name,
    )(h3, w_up, w_down, fcw, fcb, x2, target, gfin)


def _ffn_fused_bwd(dx3b, g0, gl, udgl, w_down, w_up, fcw, x2, gnorm, dx3, name, comm=None):
    s, d = x2.shape
    nblk, _, n = w_up.shape
    half = nblk // 2
    f = half * n
    t = _tile(s, _FFN_FUSED_T)
    nt = s // t
    hrows = 16

    def body(dxb_ref, g0_ref, g0h_ref, gl_ref, udgl_ref, wdown_ref, wup_ref, w_ref, x2_ref, g_ref, dx3_ref,
             dgu_ref, dx2_ref, dx2b_ref, dgn_ref, dw_ref, db_ref, ext0_ref, ext1_ref, up0_ref, up1_ref, car_ref):
        i = pl.program_id(0)
        first_tile = i == nt - 1
        last_tile = i == 0

        @pl.when(last_tile)
        def _():
            dw_ref[...] = jnp.zeros_like(dw_ref)
            db_ref[...] = jnp.zeros_like(db_ref)
            car_ref[...] = jnp.zeros_like(car_ref)

        dxb = dxb_ref[...]
        total = None
        for j in range(half):
            cs = slice(j * n, (j + 1) * n)
            us = slice(f + j * n, f + (j + 1) * n)
            ext_ref = ext0_ref if j % 2 == 0 else ext1_ref
            up_ref = up0_ref if j % 2 == 0 else up1_ref
            dact = _dot(dxb, wdown_ref[cs, :], _NT)
            ext_ref[0:8, :] = jnp.where(first_tile, 0.0, g0h_ref[:, cs].astype(_F32)[hrows - 8:hrows])
            ext_ref[8:8 + t, :] = g0_ref[:, cs].astype(_F32)
            du = (dact * gl_ref[:, cs].astype(_F32)).astype(dgu_ref.dtype)
            dgu_ref[:, us] = du
            dg = dact * udgl_ref[:, cs].astype(_F32)
            db_ref[:, cs] += _colsum(dg)
            for k in range(_FFN_K):
                dw_ref[k:k + 1, cs] += _colsum(dg * ext_ref[pl.ds(8 - (_FFN_K - 1) + k, t), :])
            up_ref[0:t, :] = dg
            up_ref[t:t + 8, :] = car_ref[:, cs]
            car_ref[:, cs] = dg[0:8, :]
            dg0 = w_ref[0:1, cs] * up_ref[pl.ds(_FFN_K - 1, t), :]
            for k in range(1, _FFN_K):
                dg0 = dg0 + w_ref[k:k + 1, cs] * up_ref[pl.ds(_FFN_K - 1 - k, t), :]
            dg0 = dg0.astype(dgu_ref.dtype)
            dgu_ref[:, cs] = dg0
            p = _dot(dg0, wup_ref[j], _NT) + _dot(du, wup_ref[half + j], _NT)
            total = p if total is None else total + p
        _epi_rms_bwd(total, (x2_ref, g_ref, dx3_ref), (dx2_ref, dx2b_ref, dgn_ref), last_tile)

    def const(shape):
        return pl.BlockSpec(shape, lambda i: (0,) * len(shape), pipeline_mode=pl.Buffered(1))

    row = pl.BlockSpec((t, d), lambda i: (nt - 1 - i, 0))
    vec = pl.BlockSpec((1, d), lambda i: (0, 0))
    per = t // hrows
    wide = pl.BlockSpec((t, f), lambda i: (nt - 1 - i, 0))
    return _pcall(
        body, args=(dx3b, g0, g0, gl, udgl, w_down, w_up, fcw, x2, gnorm, dx3), grid=(nt,),
        in_specs=[row, wide, pl.BlockSpec((hrows, f), lambda i: (jnp.maximum((nt - 1 - i) * per - 1, 0), 0)),
                  wide, wide, const(w_down.shape), const(w_up.shape), const(fcw.shape), row, vec, row],
        out_specs=[pl.BlockSpec((t, 2 * f), lambda i: (nt - 1 - i, 0)), row, row, vec,
                   pl.BlockSpec((_FFN_K, f), lambda i: (0, 0)), pl.BlockSpec((1, f), lambda i: (0, 0))],
        out_shape=[jax.ShapeDtypeStruct((s, 2 * f), _MXU), jax.ShapeDtypeStruct((s, d), _F32),
                   jax.ShapeDtypeStruct((s, d), _MXU), jax.ShapeDtypeStruct((1, d), _F32),
                   jax.ShapeDtypeStruct((_FFN_K, f), _F32), jax.ShapeDtypeStruct((1, f), _F32)],
        scratch_shapes=[pltpu.VMEM((t + 8, n), _F32), pltpu.VMEM((t + 8, n), _F32),
                        pltpu.VMEM((t + 8, n), _F32), pltpu.VMEM((t + 8, n), _F32), pltpu.VMEM((8, f), _F32)],
        sem=("arbitrary",), name=name, comm=comm, vmem=_VMEM_LIMIT_FUSED)


def _mesh_pos():
    return lax.axis_index("x"), lax.axis_index("y"), lax.axis_index("c")


def _flip(v, bit):
    return 1 - v if bit else v


def _sem_scratch(n):
    return [pltpu.SemaphoreType.DMA((7 * n,)), pltpu.SemaphoreType.DMA((7 * n,)), pltpu.SemaphoreType.DMA((n,))]


class _Gather:
    def __init__(self, xs):
        self.ins = list(xs)
        self.outs = [jax.ShapeDtypeStruct((_NDEV,) + v.shape, v.dtype) for v in xs]
        self.scratch = _sem_scratch(len(xs))

    def _plan(self, x_refs, out_refs, sems):
        send_sems, recv_sems, local_sems = sems
        x, y, c = _mesh_pos()
        me, sibling = (x, y, c), (x, y, 1 - c)
        chips = [(1 - x, y), (x, 1 - y), (1 - x, 1 - y)]

        def copy(a, k, block, to, src=None):
            slot = out_refs[a].at[4 * block[0] + 2 * block[1] + block[2]]
            return pltpu.make_async_remote_copy(
                src_ref=slot if src is None else src, dst_ref=slot,
                send_sem=send_sems.at[a * 7 + k], recv_sem=recv_sems.at[a * 7 + k],
                device_id=to, device_id_type=_MESH_ID)

        def own(a):
            return pltpu.make_async_copy(x_refs[a], out_refs[a].at[4 * x + 2 * y + c], local_sems.at[a])

        def first(a):
            return [copy(a, 0, me, sibling, src=x_refs[a])] + [
                copy(a, 1 + j, me, (*chip, c), src=x_refs[a]) for j, chip in enumerate(chips)]

        return me, sibling, chips, c, copy, own, first

    def start(self, x_refs, out_refs, sems):
        _, _, _, _, _, own, first = self._plan(x_refs, out_refs, sems)
        for a in range(len(self.ins)):
            own(a).start()
            for cp in first(a):
                cp.start()

    def finish(self, x_refs, out_refs, sems):
        me, sibling, chips, c, copy, own, first = self._plan(x_refs, out_refs, sems)
        n = len(self.ins)
        passed = []
        for a in range(n):
            for j, chip in enumerate(chips):
                copy(a, 1 + j, (*chip, c), me).wait_recv()
                fwd = copy(a, 4 + j, (*chip, c), sibling)
                fwd.start()
                passed.append(fwd)
        for a in range(n):
            copy(a, 0, sibling, me).wait_recv()
            for j, chip in enumerate(chips):
                copy(a, 4 + j, (*chip, 1 - c), me).wait_recv()
        for a in range(n):
            for cp in first(a):
                cp.wait_send()
        for cp in passed:
            cp.wait_send()
        for a in range(n):
            own(a).wait()


class _Exchange:
    def __init__(self, gs):
        self.ins = list(gs)
        self.outs = [jax.ShapeDtypeStruct(v.shape, v.dtype) for v in gs]
        self.scratch = _sem_scratch(len(gs))

    def _plan(self, g_refs, r_refs, sems):
        send_sems, recv_sems, local_sems = sems
        x, y, c = _mesh_pos()
        me_idx = 4 * x + 2 * y + c
        n = len(self.ins)

        def copy(a, k):
            peer = (_flip(x, k & 4), _flip(y, k & 2), _flip(c, k & 1))
            peer_idx = 4 * peer[0] + 2 * peer[1] + peer[2]
            return pltpu.make_async_remote_copy(
                src_ref=g_refs[a].at[peer_idx], dst_ref=r_refs[a].at[me_idx],
                send_sem=send_sems.at[a * 7 + k - 1], recv_sem=recv_sems.at[a * 7 + k - 1],
                device_id=peer, device_id_type=_MESH_ID)

        copies = [copy(a, k) for a in range(n) for k in range(1, _NDEV)]
        mine = [pltpu.make_async_copy(g_refs[a].at[me_idx], r_refs[a].at[me_idx], local_sems.at[a])
                for a in range(n)]
        return copies, mine

    def start(self, g_refs, r_refs, sems):
        copies, mine = self._plan(g_refs, r_refs, sems)
        for cp in copies + mine:
            cp.start()

    def finish(self, g_refs, r_refs, sems):
        copies, mine = self._plan(g_refs, r_refs, sems)
        for cp in copies:
            cp.wait_recv()
        for cp in copies:
            cp.wait_send()
        for cp in mine:
            cp.wait()


class _Both:
    def __init__(self, first, second):
        self.parts = (first, second)
        self.ins = first.ins + second.ins
        self.outs = first.outs + second.outs
        self.scratch = first.scratch + second.scratch

    def _split(self, ins, outs, sems):
        a, b = self.parts
        na, nb = len(a.ins), len(a.scratch)
        return (a, ins[:na], outs[:na], sems[:nb]), (b, ins[na:], outs[na:], sems[nb:])

    def start(self, ins, outs, sems):
        for part, i, o, s in self._split(ins, outs, sems):
            part.start(i, o, s)

    def finish(self, ins, outs, sems):
        for part, i, o, s in self._split(ins, outs, sems):
            part.finish(i, o, s)


def _comm_call(comm, name):
    def body(*refs):
        n_i, n_o = len(comm.ins), len(comm.outs)
        ins, outs, sems = refs[:n_i], refs[n_i:n_i + n_o], refs[n_i + n_o:]
        comm.start(ins, outs, sems)
        comm.finish(ins, outs, sems)

    return pl.pallas_call(
        body, out_shape=list(comm.outs), in_specs=[_ANY] * len(comm.ins), out_specs=[_ANY] * len(comm.outs),
        scratch_shapes=list(comm.scratch), name=name)(*comm.ins)


def _adamw_math(w, g, m, v):
    m = _ADAM_B1 * m + (1.0 - _ADAM_B1) * g
    v = _ADAM_B2 * v + (1.0 - _ADAM_B2) * (g * g)
    m_hat = m / (1.0 - _ADAM_B1 ** _ADAM_STEP)
    v_hat = v / (1.0 - _ADAM_B2 ** _ADAM_STEP)
    delta = -_ADAM_LR * (m_hat / (jnp.sqrt(v_hat) + _ADAM_EPS) + _ADAM_WD * w)
    return delta, m, v


def _sum_adamw(parts, w, m, v, name):
    r, c = w.shape
    tr = _tile(r, 128)

    def body(p_ref, w_ref, m_ref, v_ref, g_ref, d_ref, nm_ref, nv_ref):
        g = p_ref[0].astype(_F32)
        for j in range(1, _NDEV):
            g = g + p_ref[j].astype(_F32)
        delta, nm, nv = _adamw_math(w_ref[...], g, m_ref[...], v_ref[...])
        g_ref[...] = g
        d_ref[...] = delta
        nm_ref[...] = nm
        nv_ref[...] = nv

    blk = pl.BlockSpec((tr, c), lambda i: (i, 0))
    return pl.pallas_call(
        body, grid=(r // tr,),
        in_specs=[pl.BlockSpec((_NDEV, tr, c), lambda i: (0, i, 0)), blk, blk, blk],
        out_specs=[blk] * 4, out_shape=[jax.ShapeDtypeStruct((r, c), _F32)] * 4,
        compiler_params=_cparams("parallel"), name=name,
    )(parts, w, m, v)


def _sum8(parts, name):
    _, r, c = parts.shape

    def body(p_ref, o_ref):
        g = p_ref[0]
        for j in range(1, _NDEV):
            g = g + p_ref[j]
        o_ref[...] = g

    return pl.pallas_call(
        body, grid=(1,), in_specs=[pl.BlockSpec((_NDEV, r, c), lambda i: (0, 0, 0))],
        out_specs=pl.BlockSpec((r, c), lambda i: (0, 0)), out_shape=jax.ShapeDtypeStruct((r, c), _F32),
        compiler_params=_cparams("arbitrary"), name=name,
    )(parts)


def _adamw_flat(g, w, m, v, name):
    r, c = w.shape

    def body(g_ref, w_ref, m_ref, v_ref, d_ref, nm_ref, nv_ref):
        delta, nm, nv = _adamw_math(w_ref[...], g_ref[...], m_ref[...], v_ref[...])
        d_ref[...] = delta
        nm_ref[...] = nm
        nv_ref[...] = nv

    blk = pl.BlockSpec((r, c), lambda i: (0, 0))
    return pl.pallas_call(
        body, grid=(1,), in_specs=[blk] * 4, out_specs=[blk] * 3,
        out_shape=[jax.ShapeDtypeStruct((r, c), _F32)] * 3,
        compiler_params=_cparams("arbitrary"), name=name,
    )(g, w, m, v)


def _pack(arrs):
    flat = jnp.concatenate([a.reshape(-1).astype(_F32) for a in arrs])
    pad = (-flat.shape[0]) % 1024
    return jnp.pad(flat, (0, pad)).reshape(-1, 128)


def _unpack(flat2d, shapes):
    flat = flat2d.reshape(-1)
    out, off = [], 0
    for sh in shapes:
        size = 1
        for dim in sh:
            size *= dim
        out.append(flat[off:off + size].reshape(sh))
        off += size
    return out


def _block_diag(w):
    h, hd, _ = w.shape
    eye = jnp.eye(h, dtype=w.dtype)
    return (eye[:, None, :, None] * w[:, :, None, :]).reshape(h * hd, h * hd)


def _diag_blocks(full, h):
    hd = full.shape[0] // h
    return jnp.stack([full[i * hd:(i + 1) * hd, i * hd:(i + 1) * hd] for i in range(h)])


def kernel(x, mem, mix_norm_g, w_in, lru_conv_w, lru_conv_b, lru_w_a, lru_b_a, lru_w_x, lru_b_x, lru_lambda, conf_conv_w, conf_conv_b, conf_ln_g, conf_ln_b, w_out, xa_norm_g, mem_norm_g, w_q, w_kv, w_o, ffn_norm_g, w_up, ffn_conv_w, ffn_conv_b, w_down, final_norm_g, loss_target, m_mix_norm_g, m_w_in, m_lru_conv_w, m_lru_conv_b, m_lru_w_a, m_lru_b_a, m_lru_w_x, m_lru_b_x, m_lru_lambda, m_conf_conv_w, m_conf_conv_b, m_conf_ln_g, m_conf_ln_b, m_w_out, m_xa_norm_g, m_mem_norm_g, m_w_q, m_w_kv, m_w_o, m_ffn_norm_g, m_w_up, m_ffn_conv_w, m_ffn_conv_b, m_w_down, m_final_norm_g, v_mix_norm_g, v_w_in, v_lru_conv_w, v_lru_conv_b, v_lru_w_a, v_lru_b_a, v_lru_w_x, v_lru_b_x, v_lru_lambda, v_conf_conv_w, v_conf_conv_b, v_conf_ln_g, v_conf_ln_b, v_w_out, v_xa_norm_g, v_mem_norm_g, v_w_q, v_w_kv, v_w_o, v_ffn_norm_g, v_w_up, v_ffn_conv_w, v_ffn_conv_b, v_w_down, v_final_norm_g):
    names = ["mix_norm_g", "w_in", "lru_conv_w", "lru_conv_b", "lru_w_a", "lru_b_a", "lru_w_x", "lru_b_x",
             "lru_lambda", "conf_conv_w", "conf_conv_b", "conf_ln_g", "conf_ln_b", "w_out", "xa_norm_g",
             "mem_norm_g", "w_q", "w_kv", "w_o", "ffn_norm_g", "w_up", "ffn_conv_w", "ffn_conv_b", "w_down",
             "final_norm_g"]
    loc = locals()
    W = {n: loc[n] for n in names}
    M = {n: loc["m_" + n] for n in names}
    V = {n: loc["v_" + n] for n in names}
    big = ["w_in", "w_out", "w_q", "w_kv", "w_o", "w_up", "w_down"]
    conv_sharded = ["lru_conv_w", "conf_conv_w", "ffn_conv_w"]

    xs = x[0]
    mems = mem[0]
    tgt = loss_target[0]
    me = 4 * lax.axis_index("x") + 2 * lax.axis_index("y") + lax.axis_index("c")

    conv_shapes = [W[n].shape[1:] for n in conv_sharded]
    conv_pack = _pack([W[n][0] for n in conv_sharded])
    shard = {n: W[n][0].astype(_XFER) for n in big}
    h1, (g_in, g_out, g_conv) = _rms_fwd(
        xs, mix_norm_g, "rms1_fwd", comm=_Gather([shard["w_in"], shard["w_out"], conv_pack]))
    convs = [[] for _ in conv_sharded]
    for j in range(_NDEV):
        for idx, part in enumerate(_unpack(g_conv[j], conv_shapes)):
            convs[idx].append(part)
    lcw, ccw, fcw = [jnp.concatenate(parts, axis=-1) for parts in convs]

    wab = jnp.concatenate([_block_diag(lru_w_a[0]), _block_diag(lru_w_x[0])], axis=1).astype(_MXU)
    mixer_params = (lcw, lru_conv_b, wab, lru_b_a, lru_b_x, lru_lambda, ccw, conf_conv_b, conf_ln_g, conf_ln_b)

    w_out_f = g_out.reshape(-1, g_out.shape[-1])
    (z, ycat, hs, cc, x1, h2), (g_q, g_kv, g_o, g_up, g_down) = _mixer_fwd(
        xs, h1, g_in, w_out_f, xa_norm_g, *mixer_params, "mixer_fwd",
        comm=_Gather([shard[n] for n in ("w_q", "w_kv", "w_o", "w_up", "w_down")]))
    w_q_f = g_q.reshape(-1, g_q.shape[-1])
    w_o_f = g_o.reshape(-1, g_o.shape[-1])
    w_down_f = g_down.reshape(-1, g_down.shape[-1])
    row32, row16, vec32 = (_F32, "row"), (_MXU, "row"), (_F32, "vec")
    mn = _rms_fwd(mems, mem_norm_g, "rmsm_fwd")
    q = _mm_nn_nat(h2, w_q_f, "mm_q_fwd", outs=[row16])
    kv = _mm_nn_stacked(mn, g_kv, _MXU, "mm_kv_fwd")
    o = _attn_fwd(q, kv, "attn_fwd")
    x2, h3 = _mm_nn_nat(o, w_o_f, "mm_o_fwd", epi=_epi_residual_rms,
                        extra=[(x1, "row"), (ffn_norm_g, "vec")], outs=[row32, row16])

    gfin = final_norm_g.reshape(1, -1)
    g0, act, gelu_g, u_dgelu, dx3, dx3b, lvec, dg_final = _ffn_fused_fwd(
        h3, g_up, w_down_f, fcw, ffn_conv_b, x2, tgt, gfin, "ffn_fwd")
    loss_local = 0.5 * jnp.sum(lvec) / xs.shape[1]
    loss = lax.psum(loss_local, _AXES)

    def rows8(p):
        return p.reshape(_NDEV, p.shape[0] // _NDEV, p.shape[1])

    p_down = _mm_tn_nat(act, dx3b, _XFER, "mm_down_wgrad", ts=2048)
    (dgu, dx2, dx2b, dg_ffn, dfcw, dfcb), (r_down,) = _ffn_fused_bwd(
        dx3b, g0, gelu_g, u_dgelu, w_down_f, g_up, fcw, x2, ffn_norm_g, dx3, "ffn_bwd",
        comm=_Exchange([rows8(p_down)]))
    p_up = _mm_tn_stacked(h3, dgu, _NDEV, _XFER, "mm_up_wgrad", slabs=2)

    do = _mm_nt_nat(dx2b, w_o_f, "mm_o_dgrad", outs=[row16])
    p_o = _mm_tn_nat(o, dx2b, _XFER, "mm_o_wgrad", ts=2048)
    dq, dk, dv = _attn_bwd(q, kv, do, "attn_bwd")
    dx1, dx1b, dg_xa = _mm_nt_nat(
        dq, w_q_f, "mm_q_dgrad", epi=_epi_rms_bwd, extra=[(x1, "row"), (xa_norm_g, "vec"), (dx2, "row")],
        outs=[row32, row16, vec32], tm=512)
    p_q = _mm_tn_nat(h2, dq, _XFER, "mm_q_wgrad", ts=2048)
    dkv = jnp.concatenate([dk, dv], axis=1).astype(_MXU)
    dmn = _mm_nt_stacked(dkv, g_kv, "mm_kv_dgrad", outs=[row32], slabs=_NDEV)
    p_kv = _mm_tn_stacked(mn, dkv, _NDEV, _XFER, "mm_kv_wgrad", slabs=_NDEV)
    _, _, dg_mem = _rms_bwd(dmn, mems, mem_norm_g, None, "rmsm_bwd")

    p_out = _mm_tn_nat(ycat, dx1b, _XFER, "mm_out_wgrad", ts=2048)
    ((dz, p_in, dlcw, dlcb, dwab, dba, dbx, dlam, dccw, dccb, dlng, dlnb),
     (r_up, r_o, r_q, r_kv, r_out)) = _mixer_bwd(
        dx1b, w_out_f, h1, z, hs, cc, *mixer_params, "mixer_bwd",
        comm=_Exchange([p_up, rows8(p_o), rows8(p_q), p_kv, rows8(p_out)]))

    c = _D_LRU
    heads = lru_w_a.shape[1]
    small_partial = {
        "lru_conv_w": dlcw, "lru_conv_b": dlcb,
        "lru_w_a": _diag_blocks(dwab[:, :c], heads), "lru_b_a": dba,
        "lru_w_x": _diag_blocks(dwab[:, c:], heads), "lru_b_x": dbx, "lru_lambda": dlam,
        "conf_conv_w": dccw, "conf_conv_b": dccb, "conf_ln_g": dlng, "conf_ln_b": dlnb,
        "xa_norm_g": dg_xa, "mem_norm_g": dg_mem, "ffn_norm_g": dg_ffn,
        "ffn_conv_w": dfcw, "ffn_conv_b": dfcb, "final_norm_g": dg_final,
    }
    early = list(small_partial)
    early_shapes = [small_partial[n].shape for n in early]
    (grad_x, dg_mix), (r_in, early_all) = _mm_nt_stacked(
        dz, g_in, "mm_in_dgrad", epi=_epi_rms_bwd, extra=[(xs, "row"), (mix_norm_g, "vec"), (dx1, "row")],
        outs=[row32, vec32], tm=512, slabs=_NDEV,
        comm=_Both(_Exchange([p_in]), _Gather([_pack([small_partial[n] for n in early])])))
    (mix_all,) = _comm_call(_Gather([dg_mix]), "gather_mix_grad")
    small = early + ["mix_norm_g"]
    small_sum = _unpack(_sum8(early_all, "sum_small_grads"), early_shapes)
    small_sum.append(_sum8(mix_all.reshape(_NDEV, 8, -1), "sum_mix_grad").reshape(dg_mix.shape))
    received = {"w_in": r_in, "w_out": r_out, "w_q": r_q, "w_kv": r_kv, "w_o": r_o, "w_up": r_up,
                "w_down": r_down}

    grads, deltas, new_m, new_v = {}, {}, {}, {}
    for n, rec in ((n, received[n]) for n in big):
        shp = W[n].shape
        w2, m2, v2 = (t.reshape(rec.shape[1:]) for t in (W[n], M[n], V[n]))
        outs = _sum_adamw(rec, w2, m2, v2, "adamw_" + n)
        grads[n], deltas[n], new_m[n], new_v[n] = (t.reshape(shp) for t in outs)

    small_g = []
    for n, g in zip(small, small_sum):
        if n in conv_sharded:
            width = W[n].shape[-1]
            g = lax.dynamic_slice_in_dim(g, me * width, width, axis=1)
        small_g.append(g.reshape(W[n].shape))
    small_shapes = [W[n].shape for n in small]
    sd, sm, sv = _adamw_flat(_pack(small_g), _pack([W[n] for n in small]), _pack([M[n] for n in small]),
                             _pack([V[n] for n in small]), "adamw_small")
    for n, g, d_, m_, v_ in zip(small, small_g, _unpack(sd, small_shapes), _unpack(sm, small_shapes),
                                _unpack(sv, small_shapes)):
        grads[n], deltas[n], new_m[n], new_v[n] = g, d_, m_, v_

    return (loss, grad_x[None], *[grads[n] for n in names], *[deltas[n] for n in names],
            *[new_m[n] for n in names], *[new_v[n] for n in names])
```

```python
import functools

import jax
import jax.numpy as jnp
from jax import lax
from jax.experimental import pallas as pl
from jax.experimental.pallas import tpu as pltpu

_MXU = jnp.bfloat16
_XFER = jnp.bfloat16
_F32 = jnp.float32
_EPS = 1e-6
_NDEV = 8
_VMEM_LIMIT = 48 * 1024 * 1024

_D_LRU = 512
_XA_HEADS = 4
_RG_C = 8.0
_ADAM_LR, _ADAM_B1, _ADAM_B2, _ADAM_EPS, _ADAM_WD, _ADAM_STEP = 0.001, 0.9, 0.999, 1e-08, 0.01, 10

_MESH_ID = pl.DeviceIdType.MESH
_ANY = pl.BlockSpec(memory_space=pl.ANY)


def _cparams(*sem, vmem=_VMEM_LIMIT):
    return pltpu.CompilerParams(dimension_semantics=tuple(sem), vmem_limit_bytes=vmem)


def _pcall(body, *, args, grid, in_specs, out_specs, out_shape, sem, name, scratch_shapes=(), comm=None,
           vmem=_VMEM_LIMIT):
    outs_l = list(out_shape) if isinstance(out_shape, (list, tuple)) else [out_shape]
    ospecs_l = list(out_specs) if isinstance(out_specs, (list, tuple)) else [out_specs]
    n_in, n_out, n_scr = len(args), len(outs_l), len(scratch_shapes)
    if comm is None:
        res = pl.pallas_call(
            body, grid=grid, in_specs=list(in_specs), out_specs=ospecs_l, out_shape=outs_l,
            scratch_shapes=list(scratch_shapes), compiler_params=_cparams(*sem, vmem=vmem), name=name)(*args)
        return list(res), []
    n_ci, n_co = len(comm.ins), len(comm.outs)

    def wrapped(*refs):
        ins, cins = refs[:n_in], refs[n_in:n_in + n_ci]
        o = n_in + n_ci
        outs, couts = refs[o:o + n_out], refs[o + n_out:o + n_out + n_co]
        s = o + n_out + n_co
        scr, cscr = refs[s:s + n_scr], refs[s + n_scr:]
        first = pl.program_id(0) == 0
        last = pl.program_id(0) == grid[0] - 1
        for ax in range(1, len(grid)):
            first = jnp.logical_and(first, pl.program_id(ax) == 0)
            last = jnp.logical_and(last, pl.program_id(ax) == grid[ax] - 1)

        @pl.when(first)
        def _():
            comm.start(cins, couts, cscr)

        body(*ins, *outs, *scr)

        @pl.when(last)
        def _():
            comm.finish(cins, couts, cscr)

    res = pl.pallas_call(
        wrapped, grid=grid, in_specs=list(in_specs) + [_ANY] * n_ci, out_specs=ospecs_l + [_ANY] * n_co,
        out_shape=outs_l + list(comm.outs), scratch_shapes=list(scratch_shapes) + list(comm.scratch),
        compiler_params=_cparams(*(("arbitrary",) * len(grid)), vmem=vmem), name=name)(*args, *comm.ins)
    return list(res[:n_out]), list(res[n_out:])


def _sigmoid(v):
    return 1.0 / (1.0 + jnp.exp(-v))


_GELU_C = 0.7978845608028654
_GELU_K = 0.044715


def _gelu(v):
    t = jnp.tanh(_GELU_C * (v + _GELU_K * v * v * v))
    return 0.5 * v * (1.0 + t)


def _gelu_and_grad(v):
    v2 = v * v
    s = 0.5 * jnp.tanh(v * (_GELU_C + (_GELU_C * _GELU_K) * v2)) + 0.5
    g = v * s
    dg = s + (g * (1.0 - s)) * ((2.0 * _GELU_C) + (6.0 * _GELU_C * _GELU_K) * v2)
    return g, dg


def _softplus(v):
    e = jnp.exp(-jnp.abs(v))
    log1p = jnp.where(e < 1e-2, e * (1.0 - e * (0.5 - e * (1.0 / 3.0))), jnp.log(1.0 + e))
    return jnp.maximum(v, 0.0) + log1p


def _neg_expm1(v):
    series = -v * (1.0 + v * (0.5 + v * ((1.0 / 6.0) + v * (1.0 / 24.0))))
    return jnp.where(v > -0.0625, series, 1.0 - jnp.exp(v))


def _dot(a, b, dims):
    return lax.dot_general(a.astype(_MXU), b.astype(_MXU), (dims, ((), ())), preferred_element_type=_F32)


_NN = ((1,), (0,))
_NT = ((1,), (1,))
_TN = ((0,), (0,))


def _scan_fwd(a, b, rows):
    n = a.shape[0]
    d = 1
    while d < n:
        keep = rows >= d
        b = jnp.where(keep, b + a * pltpu.roll(b, d, 0), b)
        a = jnp.where(keep, a * pltpu.roll(a, d, 0), a)
        d *= 2
    return a, b


def _scan_rev(a, b, rows):
    n = a.shape[0]
    d = 1
    while d < n:
        keep = rows < n - d
        b = jnp.where(keep, b + a * pltpu.roll(b, n - d, 0), b)
        a = jnp.where(keep, a * pltpu.roll(a, n - d, 0), a)
        d *= 2
    return a, b


def _colsum(v):
    return jnp.sum(v, axis=0, keepdims=True)


def _mm(a, b, *, dims, grid, a_spec, b_spec, outs, acc_shape, name, extra=(), epi=None, slabs=1, comm=None):
    nred = grid[-1]
    red_axis = len(grid) - 1
    n_ex, n_out = len(extra), len(outs)
    epi = _epi_store if epi is None else epi

    def body(*refs):
        a_ref, b_ref = refs[:2]
        ex, o_refs, acc_ref = refs[2:2 + n_ex], refs[2 + n_ex:2 + n_ex + n_out], refs[-1]
        if slabs == 1:
            p = _dot(a_ref[...], b_ref[...], dims)
        else:
            n = b_ref.shape[-1]
            p = _dot(a_ref[:, 0:n], b_ref[0], dims)
            for jj in range(1, slabs):
                p = p + _dot(a_ref[:, jj * n:(jj + 1) * n], b_ref[jj], dims)

        first_rows = pl.program_id(0) == 0
        if nred == 1:
            epi(p, ex, o_refs, first_rows)
        else:
            k = pl.program_id(red_axis)

            @pl.when(k == 0)
            def _():
                acc_ref[...] = p

            @pl.when(jnp.logical_and(k > 0, k < nred - 1))
            def _():
                acc_ref[...] += p

            @pl.when(k == nred - 1)
            def _():
                epi(acc_ref[...] + p, ex, o_refs, first_rows)

    sem = ("parallel",) * (len(grid) - 1) + ("arbitrary",)
    if any(o[0].shape[0] == 1 for o in outs):
        sem = ("arbitrary",) * len(grid)
    res, cres = _pcall(
        body, args=(a, b) + tuple(e[0] for e in extra), grid=grid,
        in_specs=[a_spec, b_spec] + [e[1] for e in extra],
        out_specs=[o[1] for o in outs], out_shape=[o[0] for o in outs],
        scratch_shapes=[pltpu.VMEM(acc_shape if nred > 1 else (8, 128), _F32)], sem=sem, name=name, comm=comm)
    res = res[0] if n_out == 1 else res
    return res if comm is None else (res, cres)


def _epi_store(total, ex, outs, first_rows):
    outs[0][...] = total.astype(outs[0].dtype)


def _epi_residual_rms(total, ex, outs, first_rows):
    res_ref, g_ref = ex
    xn = total + res_ref[...]
    outs[0][...] = xn
    r = lax.rsqrt(jnp.mean(xn * xn, axis=-1, keepdims=True) + _EPS)
    outs[1][...] = (xn * r * g_ref[...]).astype(outs[1].dtype)


def _epi_rms_bwd(total, ex, outs, first_rows):
    x_ref, g_ref, dres_ref = ex
    dg_ref = outs[-1]
    xv = x_ref[...]
    r = lax.rsqrt(jnp.mean(xv * xv, axis=-1, keepdims=True) + _EPS)
    xhat = xv * r
    dxh = total * g_ref[...]
    dx = dres_ref[...] + r * (dxh - xhat * jnp.mean(dxh * xhat, axis=-1, keepdims=True))
    for o_ref in outs[:-1]:
        o_ref[...] = dx.astype(o_ref.dtype)

    @pl.when(first_rows)
    def _():
        dg_ref[...] = jnp.zeros_like(dg_ref)

    dg_ref[...] += _colsum(total * xhat)


def _epi_final(total, ex, outs, first_rows):
    res_ref, t_ref, g_ref = ex
    dx_ref, dxb_ref, l_ref, dg_ref = outs
    xv = total + res_ref[...]
    gv = g_ref[...]
    d = xv.shape[-1]
    r = lax.rsqrt(jnp.mean(xv * xv, axis=-1, keepdims=True) + _EPS)
    xhat = xv * r
    err = xhat * gv - t_ref[...]
    dy = err * (1.0 / d)
    dxh = dy * gv
    dx = r * (dxh - xhat * jnp.mean(dxh * xhat, axis=-1, keepdims=True))
    dx_ref[...] = dx
    dxb_ref[...] = dx.astype(dxb_ref.dtype)

    @pl.when(first_rows)
    def _():
        l_ref[...] = jnp.zeros_like(l_ref)
        dg_ref[...] = jnp.zeros_like(dg_ref)

    l_ref[...] += _colsum(err * err)
    dg_ref[...] += _colsum(dy * xhat)


def _tile(m, cap):
    t = min(m, cap)
    assert m % t == 0
    return t


def _row_spec(tm, n):
    return pl.BlockSpec((tm, n), lambda i, *_: (i, 0))


def _vec_spec(n):
    return pl.BlockSpec((1, n), lambda *_: (0, 0))


def _row_io(m, n, tm, extra, outs):
    def spec(kind):
        return _row_spec(tm, n) if kind == "row" else _vec_spec(n)

    ex = [(arr, spec(kind)) for arr, kind in extra]
    os_ = [(jax.ShapeDtypeStruct((m, n) if kind == "row" else (1, n), dt), spec(kind)) for dt, kind in outs]
    return ex, os_


def _mm_nn_stacked(a, w, out_dtype, name, comm=None, tm=1024):
    m, k = a.shape
    j, _, n = w.shape
    tm = _tile(m, tm)
    return _mm(a, w, dims=_NN, grid=(m // tm, j, 1),
               a_spec=pl.BlockSpec((tm, k), lambda i, jj, r: (i, 0)),
               b_spec=pl.BlockSpec((None, k, n), lambda i, jj, r: (jj, 0, 0)),
               outs=[(jax.ShapeDtypeStruct((m, j * n), out_dtype), pl.BlockSpec((tm, n), lambda i, jj, r: (i, jj)))],
               acc_shape=(tm, n), name=name, comm=comm)


def _mm_nt_stacked(dc, w, name, *, outs, extra=(), epi=None, comm=None, tm=1024, slabs=1):
    m = dc.shape[0]
    j, k, n = w.shape
    tm = _tile(m, tm)
    assert j % slabs == 0
    ex, os_ = _row_io(m, k, tm, extra, outs)
    wblk = (None, k, n) if slabs == 1 else (slabs, k, n)
    return _mm(dc, w, dims=_NT, grid=(m // tm, j // slabs),
               a_spec=pl.BlockSpec((tm, slabs * n), lambda i, r: (i, r)),
               b_spec=pl.BlockSpec(wblk, lambda i, r: (r, 0, 0)),
               outs=os_, extra=ex, epi=epi, acc_shape=(tm, k), name=name, slabs=slabs, comm=comm)


def _mm_tn_stacked(a, dc, j, out_dtype, name, slabs=1, ts=1024):
    s, k = a.shape
    n = dc.shape[1] // j
    ts = _tile(s, ts)
    assert j % slabs == 0

    def epi(total, ex, outs, first_rows):
        for jj in range(slabs):
            outs[0][jj] = total[:, jj * n:(jj + 1) * n].astype(outs[0].dtype)

    return _mm(a, dc, dims=_TN, grid=(j // slabs, s // ts),
               a_spec=pl.BlockSpec((ts, k), lambda jj, r: (r, 0)),
               b_spec=pl.BlockSpec((ts, slabs * n), lambda jj, r: (r, jj)),
               outs=[(jax.ShapeDtypeStruct((j, k, n), out_dtype),
                      pl.BlockSpec((slabs, k, n), lambda jj, r: (jj, 0, 0)))],
               epi=epi, acc_shape=(k, slabs * n), name=name)


def _mm_tn_nat(a, dc, out_dtype, name, ts=1024):
    s, kt = a.shape
    n = dc.shape[1]
    ts = _tile(s, ts)
    tkb = _tile(kt, 512)
    return _mm(a, dc, dims=_TN, grid=(kt // tkb, s // ts),
               a_spec=pl.BlockSpec((ts, tkb), lambda kb, r: (r, kb)),
               b_spec=pl.BlockSpec((ts, n), lambda kb, r: (r, 0)),
               outs=[(jax.ShapeDtypeStruct((kt, n), out_dtype), pl.BlockSpec((tkb, n), lambda kb, r: (kb, 0)))],
               acc_shape=(tkb, n), name=name)


def _rms_fwd(x, g, name, comm=None):
    s, d = x.shape
    t = _tile(s, 1024)

    def body(x_ref, g_ref, h_ref):
        xv = x_ref[...]
        r = lax.rsqrt(jnp.mean(xv * xv, axis=-1, keepdims=True) + _EPS)
        h_ref[...] = (xv * r * g_ref[...]).astype(h_ref.dtype)

    res, cres = _pcall(
        body, args=(x, g), grid=(s // t,),
        in_specs=[pl.BlockSpec((t, d), lambda i: (i, 0)), pl.BlockSpec((1, d), lambda i: (0, 0))],
        out_specs=pl.BlockSpec((t, d), lambda i: (i, 0)),
        out_shape=jax.ShapeDtypeStruct((s, d), _MXU), sem=("parallel",), name=name, comm=comm)
    return res[0] if comm is None else (res[0], cres)


def _rms_bwd(dh, x, g, dres, name):
    s, d = x.shape
    t = _tile(s, 256)
    has_res = dres is not None

    def body(*refs):
        if has_res:
            dh_ref, x_ref, g_ref, dres_ref, dx_ref, dxb_ref, dg_ref = refs
        else:
            dh_ref, x_ref, g_ref, dx_ref, dxb_ref, dg_ref = refs
        xv = x_ref[...]
        dhv = dh_ref[...]
        r = lax.rsqrt(jnp.mean(xv * xv, axis=-1, keepdims=True) + _EPS)
        xhat = xv * r
        dxh = dhv * g_ref[...]
        dx = r * (dxh - xhat * jnp.mean(dxh * xhat, axis=-1, keepdims=True))
        if has_res:
            dx = dx + dres_ref[...]
        dx_ref[...] = dx
        dxb_ref[...] = dx.astype(dxb_ref.dtype)

        @pl.when(pl.program_id(0) == 0)
        def _():
            dg_ref[...] = jnp.zeros_like(dg_ref)

        dg_ref[...] += _colsum(dhv * xhat)

    row = pl.BlockSpec((t, d), lambda i: (i, 0))
    vec = pl.BlockSpec((1, d), lambda i: (0, 0))
    in_specs = [row, row, vec] + ([row] if has_res else [])
    args = (dh, x, g) + ((dres,) if has_res else ())
    return pl.pallas_call(
        body, grid=(s // t,), in_specs=in_specs, out_specs=[row, row, vec],
        out_shape=[jax.ShapeDtypeStruct((s, d), _F32), jax.ShapeDtypeStruct((s, d), _MXU),
                   jax.ShapeDtypeStruct((1, d), _F32)],
        compiler_params=_cparams("arbitrary"), name=name,
    )(*args)


_LRU_K = 4
_CONF_K = 31
_LRU_HALO = 8
_CONF_HALO = 32
_MIX_T = 256


def _lru_gates(lx, wab_ref, ba_ref, bx_ref, lam_ref):
    c = _D_LRU
    pre = _dot(lx, wab_ref[...], _NN)
    r = _sigmoid(pre[:, :c] + ba_ref[...])
    ig = _sigmoid(pre[:, c:] + bx_ref[...])
    sp = _softplus(-lam_ref[...])
    log_a = (-_RG_C) * r * sp
    a = jnp.exp(log_a)
    mult = jnp.sqrt(_neg_expm1(2.0 * log_a))
    return r, ig, sp, a, mult


def _causal_conv(ext_ref, halo, w_ref, b_ref, taps, t):
    acc = b_ref[...] + w_ref[0:1, :] * ext_ref[pl.ds(halo - (taps - 1), t), :]
    for k in range(1, taps):
        acc = acc + w_ref[k:k + 1, :] * ext_ref[pl.ds(halo - (taps - 1) + k, t), :]
    return acc


class _Windows:
    def __init__(self, ext_ref, shifted_ref, t):
        self.ext_ref, self.shifted_ref, self.t = ext_ref, shifted_ref, t
        rows = t + 24
        for r in range(1, 8):
            shifted_ref[r - 1, 0:rows, :] = ext_ref[pl.ds(r, rows), :]

    def __call__(self, off):
        q, r = divmod(off, 8)
        if r == 0:
            return self.ext_ref[pl.ds(8 * q, self.t), :]
        return self.shifted_ref[r - 1, pl.ds(8 * q, self.t), :]


def _mixer_fwd(xs, h1, w_in, w_out, gnorm, lcw, lcb, wab, ba, bx, lam, ccw, ccb, lng, lnb, name, comm=None):
    s, d = xs.shape
    nblk, _, n = w_in.shape
    c = _D_LRU
    t = _tile(s, _MIX_T)
    nt = s // t

    def body(x_ref, h_ref, win_ref, wout_ref, gn_ref,
             lcw_ref, lcb_ref, wab_ref, ba_ref, bx_ref, lam_ref, ccw_ref, ccb_ref, lng_ref, lnb_ref,
             z_ref, ycat_ref, hs_ref, cc_ref, x1_ref, h2_ref,
             ext_ref, cge_ref, hc_ref, shifted_ref, zprev_ref):
        i = pl.program_id(0)
        first = i == 0
        rows = lax.broadcasted_iota(jnp.int32, (t, c), 0)

        @pl.when(first)
        def _():
            zprev_ref[...] = jnp.zeros_like(zprev_ref)

        hv = h_ref[...]
        for j in range(nblk):
            z_ref[:, j * n:(j + 1) * n] = _dot(hv, win_ref[j], _NN)
        lx0_ref, gate_ref = z_ref.at[:, 0:c], z_ref.at[:, c:2 * c]
        ca_ref, cb_ref = z_ref.at[:, 2 * c:3 * c], z_ref.at[:, 3 * c:4 * c]
        lx0h_ref = zprev_ref.at[_CONF_HALO - _LRU_HALO:_CONF_HALO, 0:c]
        cah_ref, cbh_ref = zprev_ref.at[:, 2 * c:3 * c], zprev_ref.at[:, 3 * c:4 * c]

        ext_ref[0:_LRU_HALO, :] = jnp.where(first, 0.0, lx0h_ref[...])
        ext_ref[_LRU_HALO:_LRU_HALO + t, :] = lx0_ref[...]
        lx = _causal_conv(ext_ref, _LRU_HALO, lcw_ref, lcb_ref, _LRU_K, t)
        r, ig, sp, a, mult = _lru_gates(lx, wab_ref, ba_ref, bx_ref, lam_ref)
        u = mult * (ig * lx)
        a_cum, h_loc = _scan_fwd(a, u, rows)

        @pl.when(first)
        def _():
            hc_ref[...] = jnp.zeros_like(hc_ref)

        h = h_loc + a_cum * hc_ref[7:8, :]
        hs_ref[...] = h
        hc_ref[...] = hs_ref[pl.ds(t - 8, 8), :]
        ycat_ref[:, 0:c] = (h * _gelu(gate_ref[...])).astype(ycat_ref.dtype)

        cge_ref[0:_CONF_HALO, :] = jnp.where(first, 0.0, cah_ref[...] * _sigmoid(cbh_ref[...]))
        cge_ref[_CONF_HALO:_CONF_HALO + t, :] = ca_ref[...] * _sigmoid(cb_ref[...])
        win = _Windows(cge_ref, shifted_ref, t)
        first_off = _CONF_HALO - (_CONF_K - 1)
        cc = ccb_ref[...] + ccw_ref[0:1, :] * win(first_off)
        for k in range(1, _CONF_K):
            cc = cc + ccw_ref[k:k + 1, :] * win(first_off + k)
        cc_ref[...] = cc
        xc = cc - jnp.mean(cc, axis=-1, keepdims=True)
        rstd = lax.rsqrt(jnp.mean(xc * xc, axis=-1, keepdims=True) + _EPS)
        ln = xc * rstd * lng_ref[...] + lnb_ref[...]
        ycat_ref[:, c:2 * c] = (ln * _sigmoid(ln)).astype(ycat_ref.dtype)

        zprev_ref[...] = z_ref[pl.ds(t - _CONF_HALO, _CONF_HALO), :]
        y = _dot(ycat_ref[...], wout_ref[...], _NN)
        _epi_residual_rms(y, (x_ref, gn_ref), (x1_ref, h2_ref), first)

    def const(arr):
        return pl.BlockSpec(arr.shape, lambda i: (0,) * arr.ndim, pipeline_mode=pl.Buffered(1))

    def rows_of(width):
        return pl.BlockSpec((t, width), lambda i: (i, 0))

    params = (lcw, lcb, wab, ba, bx, lam, ccw, ccb, lng, lnb)
    res, cres = _pcall(
        body, args=(xs, h1, w_in, w_out, gnorm, *params), grid=(nt,),
        in_specs=[rows_of(d), rows_of(d), const(w_in), const(w_out), const(gnorm)] + [const(p) for p in params],
        out_specs=[rows_of(nblk * n), rows_of(2 * c), rows_of(c), rows_of(c), rows_of(d), rows_of(d)],
        out_shape=[jax.ShapeDtypeStruct((s, nblk * n), _F32), jax.ShapeDtypeStruct((s, 2 * c), _MXU),
                   jax.ShapeDtypeStruct((s, c), _F32), jax.ShapeDtypeStruct((s, c), _F32),
                   jax.ShapeDtypeStruct((s, d), _F32), jax.ShapeDtypeStruct((s, d), _MXU)],
        scratch_shapes=[pltpu.VMEM((t + _LRU_HALO, c), _F32), pltpu.VMEM((t + _CONF_HALO, c), _F32),
                        pltpu.VMEM((8, c), _F32), pltpu.VMEM((7, t + _CONF_HALO, c), _F32),
                        pltpu.VMEM((_CONF_HALO, nblk * n), _F32)],
        sem=("arbitrary",), name=name, comm=comm)
    return res, cres


def _mixer_bwd(dx1b, w_out, h1, z, hs, cc, lcw, lcb, wab, ba, bx, lam, ccw, ccb, lng, lnb, name, comm=None):
    s = z.shape[0]
    d = h1.shape[1]
    c = _D_LRU
    t = _tile(s, _MIX_T)
    nt = s // t
    nblk = z.shape[1] // 256

    def body(dxb_ref, wout_ref, h1_ref, lx0_ref, lx0h_ref, gate_ref, ca_ref, cah_ref, cb_ref, cbh_ref,
             hs_ref, hsh_ref, cc_ref,
             lcw_ref, lcb_ref, wab_ref, ba_ref, bx_ref, lam_ref, ccw_ref, ccb_ref, lng_ref, lnb_ref,
             dz_ref, pin_ref, dlcw_ref, dlcb_ref, dwab_ref, dba_ref, dbx_ref, dlam_ref, dccw_ref, dccb_ref, dlng_ref,
             dlnb_ref,
             ext_ref, up_ref, cge_ref, dce_ref, xc_ref, dlxc_ref, dccc_ref, shifted_ref, dwin_ref):
        i = pl.program_id(0)
        first_tile = i == nt - 1
        last_tile = i == 0
        rows = lax.broadcasted_iota(jnp.int32, (t, c), 0)

        @pl.when(last_tile)
        def _():
            for ref in (dlcw_ref, dlcb_ref, dwab_ref, dba_ref, dbx_ref, dlam_ref, dccw_ref, dccb_ref, dlng_ref,
                        dlnb_ref, xc_ref, dlxc_ref, dccc_ref, dwin_ref):
                ref[...] = jnp.zeros_like(ref)

        dycat = _dot(dxb_ref[...], wout_ref[...], _NT)

        ext_ref[0:_LRU_HALO, :] = jnp.where(first_tile, 0.0, lx0h_ref[...])
        ext_ref[_LRU_HALO:_LRU_HALO + t, :] = lx0_ref[...]
        lx = _causal_conv(ext_ref, _LRU_HALO, lcw_ref, lcb_ref, _LRU_K, t)
        r, ig, sp, a, mult = _lru_gates(lx, wab_ref, ba_ref, bx_ref, lam_ref)
        h = hs_ref[...]
        gl, dgl = _gelu_and_grad(gate_ref[...])
        dyl = dycat[:, 0:c]
        dz_ref[:, c:2 * c] = (dyl * h * dgl).astype(dz_ref.dtype)
        dh = dyl * gl

        up_ref[0:t, :] = a
        up_ref[t:t + 8, :] = jnp.ones((8, c), _F32)
        a_up = up_ref[pl.ds(1, t), :]
        a_cum, g_loc = _scan_rev(a_up, dh, rows)
        gt = g_loc + a_cum * xc_ref[0:1, :]
        xc_ref[...] = (a * gt)[0:8, :]

        up_ref[0:8, :] = jnp.where(first_tile, 0.0, hsh_ref[...])
        up_ref[8:8 + t, :] = h
        hprev = up_ref[pl.ds(7, t), :]

        da = gt * hprev
        dmult = gt * ig * lx
        dig = gt * mult * lx
        dlx = gt * mult * ig
        dlog_a = da * a - dmult * a * a / mult
        dpre_r = dlog_a * (-_RG_C) * sp * r * (1.0 - r)
        dpre_i = dig * ig * (1.0 - ig)
        dlam_ref[...] += _colsum(dlog_a * r) * (_RG_C * _sigmoid(-lam_ref[...]))
        dba_ref[...] += _colsum(dpre_r)
        dbx_ref[...] += _colsum(dpre_i)
        dpre = jnp.concatenate([dpre_r, dpre_i], axis=1).astype(_MXU)
        dlx = dlx + _dot(dpre, wab_ref[...], _NT)
        dwab_ref[...] += _dot(lx, dpre, _TN)

        dlcb_ref[...] += _colsum(dlx)
        up_ref[0:t, :] = dlx
        up_ref[t:t + 8, :] = dlxc_ref[...]
        dlxc_ref[...] = dlx[0:8, :]
        acc = lcw_ref[0:1, :] * up_ref[pl.ds(_LRU_K - 1, t), :]
        for k in range(1, _LRU_K):
            acc = acc + lcw_ref[k:k + 1, :] * up_ref[pl.ds(_LRU_K - 1 - k, t), :]
        dz_ref[:, 0:c] = acc.astype(dz_ref.dtype)
        for k in range(_LRU_K):
            dlcw_ref[k:k + 1, :] += _colsum(dlx * ext_ref[pl.ds(_LRU_HALO - (_LRU_K - 1) + k, t), :])

        sig_b = _sigmoid(cb_ref[...])
        ca = ca_ref[...]
        cge_ref[0:_CONF_HALO, :] = jnp.where(first_tile, 0.0, cah_ref[...] * _sigmoid(cbh_ref[...]))
        cge_ref[_CONF_HALO:_CONF_HALO + t, :] = ca * sig_b
        ccv = cc_ref[...]
        xcen = ccv - jnp.mean(ccv, axis=-1, keepdims=True)
        rstd = lax.rsqrt(jnp.mean(xcen * xcen, axis=-1, keepdims=True) + _EPS)
        xn = xcen * rstd
        ln = xn * lng_ref[...] + lnb_ref[...]
        sg = _sigmoid(ln)
        dln = dycat[:, c:2 * c] * (sg * (1.0 + ln * (1.0 - sg)))
        dlng_ref[...] += _colsum(dln * xn)
        dlnb_ref[...] += _colsum(dln)
        dxn = dln * lng_ref[...]
        dcc = rstd * (dxn - jnp.mean(dxn, axis=-1, keepdims=True)
                      - xn * jnp.mean(dxn * xn, axis=-1, keepdims=True))
        dccb_ref[...] += _colsum(dcc)
        win = _Windows(cge_ref, shifted_ref, t)
        for k in range(_CONF_K):
            dccw_ref[k:k + 1, :] += _colsum(dcc * win(_CONF_HALO - (_CONF_K - 1) + k))
        dce_ref[0:t, :] = dcc
        dce_ref[t:t + _CONF_HALO, :] = dccc_ref[...]
        dccc_ref[...] = dcc[0:_CONF_HALO, :]
        win = _Windows(dce_ref, shifted_ref, t)
        dcg = ccw_ref[0:1, :] * win(_CONF_K - 1)
        for k in range(1, _CONF_K):
            dcg = dcg + ccw_ref[k:k + 1, :] * win(_CONF_K - 1 - k)
        dz_ref[:, 2 * c:3 * c] = (dcg * sig_b).astype(dz_ref.dtype)
        dz_ref[:, 3 * c:4 * c] = (dcg * ca * sig_b * (1.0 - sig_b)).astype(dz_ref.dtype)

        dwin_ref[...] += _dot(h1_ref[...], dz_ref[...], _TN)

        @pl.when(first_tile)
        def _():
            for j in range(nblk):
                pin_ref[j] = dwin_ref[:, j * 256:(j + 1) * 256].astype(pin_ref.dtype)

    def col(j):
        return pl.BlockSpec((t, c), lambda i: (nt - 1 - i, j))

    def halo(j, rows_):
        per = t // rows_
        return pl.BlockSpec((rows_, c), lambda i: (jnp.maximum((nt - 1 - i) * per - 1, 0), j))

    def full(shape):
        return pl.BlockSpec(shape, lambda i: (0,) * len(shape))

    params = (lcw, lcb, wab, ba, bx, lam, ccw, ccb, lng, lnb)
    small = [(_LRU_K, c), (1, c), (c, 2 * c), (1, c), (1, c), (1, c), (_CONF_K, c), (1, c), (1, c), (1, c)]
    wide = pl.BlockSpec((t, d), lambda i: (nt - 1 - i, 0))
    pin_shape = (nblk, d, 256)
    return _pcall(
        body, args=(dx1b, w_out, h1, z, z, z, z, z, z, z, hs, hs, cc, *params), grid=(nt,),
        in_specs=[wide, pl.BlockSpec(w_out.shape, lambda i: (0, 0), pipeline_mode=pl.Buffered(1)), wide,
                  col(0), halo(0, _LRU_HALO), col(1), col(2), halo(2, _CONF_HALO), col(3), halo(3, _CONF_HALO),
                  col(0), halo(0, 8), col(0)]
        + [full(p.shape) for p in params],
        out_specs=[pl.BlockSpec((t, 4 * c), lambda i: (nt - 1 - i, 0)), full(pin_shape)] + [full(sh) for sh in small],
        out_shape=[jax.ShapeDtypeStruct((s, 4 * c), _MXU), jax.ShapeDtypeStruct(pin_shape, _XFER)]
        + [jax.ShapeDtypeStruct(sh, _F32) for sh in small],
        scratch_shapes=[pltpu.VMEM((t + _LRU_HALO, c), _F32), pltpu.VMEM((t + 8, c), _F32),
                        pltpu.VMEM((t + _CONF_HALO, c), _F32), pltpu.VMEM((t + _CONF_HALO, c), _F32),
                        pltpu.VMEM((8, c), _F32), pltpu.VMEM((8, c), _F32), pltpu.VMEM((_CONF_HALO, c), _F32),
                        pltpu.VMEM((7, t + _CONF_HALO, c), _F32), pltpu.VMEM((d, 4 * c), _F32)],
        sem=("arbitrary",), name=name, comm=comm, vmem=_VMEM_LIMIT_FUSED)


_ATT_T = 512


def _attn_probs(qh, kh, scale):
    sc = _dot(qh, kh, _NT) * scale
    e = jnp.exp(sc - jnp.max(sc, axis=-1, keepdims=True))
    return e / jnp.sum(e, axis=-1, keepdims=True)


def _const_spec(arr):
    return pl.BlockSpec(arr.shape, lambda i: (0,) * arr.ndim, pipeline_mode=pl.Buffered(1))


def _attn_fwd(h2, x1, w_q, kv, w_o, gnorm, name):
    s, d = h2.shape
    nm = kv.shape[0]
    hd = d // _XA_HEADS
    t = _tile(s, _ATT_T)
    scale = hd ** -0.5

    def body(h_ref, x1_ref, wq_ref, k_ref, v_ref, wo_ref, g_ref, q_ref, o_ref, x2_ref, h3_ref):
        q_ref[...] = _dot(h_ref[...], wq_ref[...], _NN).astype(q_ref.dtype)
        for hh in range(_XA_HEADS):
            sl = slice(hh * hd, (hh + 1) * hd)
            p = _attn_probs(q_ref[:, sl], k_ref[:, sl], scale)
            o_ref[:, sl] = _dot(p, v_ref[:, sl], _NN).astype(o_ref.dtype)
        y = _dot(o_ref[...], wo_ref[...], _NN)
        _epi_residual_rms(y, (x1_ref, g_ref), (x2_ref, h3_ref), None)

    row = pl.BlockSpec((t, d), lambda i: (i, 0))
    half = pl.BlockSpec((nm, d), lambda i: (0, 0), pipeline_mode=pl.Buffered(1))
    half2 = pl.BlockSpec((nm, d), lambda i: (0, 1), pipeline_mode=pl.Buffered(1))
    return pl.pallas_call(
        body, grid=(s // t,),
        in_specs=[row, row, _const_spec(w_q), half, half2, _const_spec(w_o), _const_spec(gnorm)],
        out_specs=[row, row, row, row],
        out_shape=[jax.ShapeDtypeStruct((s, d), _MXU), jax.ShapeDtypeStruct((s, d), _MXU),
                   jax.ShapeDtypeStruct((s, d), _F32), jax.ShapeDtypeStruct((s, d), _MXU)],
        compiler_params=_cparams("parallel"), name=name,
    )(h2, x1, w_q, kv, kv, w_o, gnorm)


def _attn_bwd(dx2b, dx2, q, o, h2, x1, kv, w_o, w_q, gnorm, name):
    s, d = q.shape
    nm = kv.shape[0]
    hd = d // _XA_HEADS
    t = _tile(s, _ATT_T)
    nt = s // t
    scale = hd ** -0.5

    def body(dxb_ref, dx2_ref, q_ref, o_ref, h_ref, x1_ref, k_ref, v_ref, wo_ref, wq_ref, g_ref,
             dx1_ref, dx1b_ref, dgn_ref, dk_ref, dv_ref, pwo_ref, pwq_ref, dq_ref, awo_ref, awq_ref):
        i = pl.program_id(0)
        first = i == 0

        @pl.when(first)
        def _():
            for ref in (dk_ref, dv_ref, awo_ref, awq_ref):
                ref[...] = jnp.zeros_like(ref)

        dxb = dxb_ref[...]
        do = _dot(dxb, wo_ref[...], _NT).astype(_MXU)
        awo_ref[...] += _dot(o_ref[...], dxb, _TN)
        for hh in range(_XA_HEADS):
            sl = slice(hh * hd, (hh + 1) * hd)
            qh = q_ref[:, sl]
            kh = k_ref[:, sl]
            doh = do[:, sl]
            p = _attn_probs(qh, kh, scale)
            dp = _dot(doh, v_ref[:, sl], _NT)
            dv_ref[:, sl] += _dot(p, doh, _TN)
            ds = (p * (dp - jnp.sum(dp * p, axis=-1, keepdims=True)) * scale).astype(_MXU)
            dq_ref[:, sl] = _dot(ds, kh, _NN).astype(dq_ref.dtype)
            dk_ref[:, sl] += _dot(ds, qh, _TN)
        dq = dq_ref[...]
        awq_ref[...] += _dot(h_ref[...], dq, _TN)
        dh = _dot(dq, wq_ref[...], _NT)
        _epi_rms_bwd(dh, (x1_ref, g_ref, dx2_ref), (dx1_ref, dx1b_ref, dgn_ref), first)

        @pl.when(i == nt - 1)
        def _():
            pwo_ref[...] = awo_ref[...].astype(pwo_ref.dtype)
            pwq_ref[...] = awq_ref[...].astype(pwq_ref.dtype)

    row = pl.BlockSpec((t, d), lambda i: (i, 0))
    vec = pl.BlockSpec((1, d), lambda i: (0, 0))
    mem_blk = pl.BlockSpec((nm, d), lambda i: (0, 0))
    sq = pl.BlockSpec((d, d), lambda i: (0, 0))
    half = pl.BlockSpec((nm, d), lambda i: (0, 0), pipeline_mode=pl.Buffered(1))
    half2 = pl.BlockSpec((nm, d), lambda i: (0, 1), pipeline_mode=pl.Buffered(1))
    return pl.pallas_call(
        body, grid=(nt,),
        in_specs=[row, row, row, row, row, row, half, half2, _const_spec(w_o), _const_spec(w_q), _const_spec(gnorm)],
        out_specs=[row, row, vec, mem_blk, mem_blk, sq, sq],
        out_shape=[jax.ShapeDtypeStruct((s, d), _F32), jax.ShapeDtypeStruct((s, d), _MXU),
                   jax.ShapeDtypeStruct((1, d), _F32), jax.ShapeDtypeStruct((nm, d), _F32),
                   jax.ShapeDtypeStruct((nm, d), _F32), jax.ShapeDtypeStruct((d, d), _XFER),
                   jax.ShapeDtypeStruct((d, d), _XFER)],
        scratch_shapes=[pltpu.VMEM((t, d), _MXU), pltpu.VMEM((d, d), _F32), pltpu.VMEM((d, d), _F32)],
        compiler_params=_cparams("arbitrary", vmem=_VMEM_LIMIT_FUSED), name=name,
    )(dx2b, dx2, q, o, h2, x1, kv, kv, w_o, w_q, gnorm)


_FFN_K = 3
_FFN_FUSED_T = 256
_VMEM_LIMIT_FUSED = 58 * 1024 * 1024


def _ffn_fused_fwd(h3, w_up, w_down, fcw, fcb, x2, target, gfin, name):
    s, d = h3.shape
    nblk, _, n = w_up.shape
    half = nblk // 2
    f = half * n
    t = _tile(s, _FFN_FUSED_T)

    def body(h_ref, wup_ref, wdown_ref, w_ref, b_ref, x2_ref, t_ref, g_ref,
             g0_ref, act_ref, gl_ref, udgl_ref, dx_ref, dxb_ref, l_ref, dg_ref, ext0_ref, ext1_ref, halo_ref):
        i = pl.program_id(0)
        first = i == 0
        h = h_ref[...]
        total = None
        for j in range(half):
            cs = slice(j * n, (j + 1) * n)
            ext_ref = ext0_ref if j % 2 == 0 else ext1_ref
            g0 = _dot(h, wup_ref[j], _NN)
            u = _dot(h, wup_ref[half + j], _NN)
            g0_ref[:, cs] = g0.astype(g0_ref.dtype)
            ext_ref[0:8, :] = jnp.where(first, 0.0, halo_ref[:, cs])
            ext_ref[8:8 + t, :] = g0
            halo_ref[:, cs] = g0[t - 8:t, :]
            g = _causal_conv(ext_ref, 8, w_ref.at[:, cs], b_ref.at[:, cs], _FFN_K, t)
            gl, dgl = _gelu_and_grad(g)
            gl_ref[:, cs] = gl.astype(gl_ref.dtype)
            udgl_ref[:, cs] = (u * dgl).astype(udgl_ref.dtype)
            act = (gl * u).astype(act_ref.dtype)
            act_ref[:, cs] = act
            p = _dot(act, wdown_ref[cs, :], _NN)
            total = p if total is None else total + p
        _epi_final(total, (x2_ref, t_ref, g_ref), (dx_ref, dxb_ref, l_ref, dg_ref), first)

    def const(shape):
        return pl.BlockSpec(shape, lambda i: (0,) * len(shape), pipeline_mode=pl.Buffered(1))

    row = pl.BlockSpec((t, d), lambda i: (i, 0))
    vec = pl.BlockSpec((1, d), lambda i: (0, 0))
    return pl.pallas_call(
        body, grid=(s // t,),
        in_specs=[row, const(w_up.shape), const(w_down.shape), const(fcw.shape), const(fcb.shape), row, row, vec],
        out_specs=[pl.BlockSpec((t, f), lambda i: (i, 0))] * 4 + [row, row, vec, vec],
        out_shape=[jax.ShapeDtypeStruct((s, f), _MXU)] * 4
        + [jax.ShapeDtypeStruct((s, d), _F32), jax.ShapeDtypeStruct((s, d), _MXU),
                   jax.ShapeDtypeStruct((1, d), _F32), jax.ShapeDtypeStruct((1, d), _F32)],
        scratch_shapes=[pltpu.VMEM((t + 8, n), _F32), pltpu.VMEM((t + 8, n), _F32), pltpu.VMEM((8, f), _F32)],
        compiler_params=_cparams("arbitrary", vmem=_VMEM_LIMIT_FUSED), name=name,
    )(h3, w_up, w_down, fcw, fcb, x2, target, gfin)


def _ffn_fused_bwd(dx3b, g0, gl, udgl, w_down, w_up, fcw, x2, gnorm, dx3, name, comm=None):
    s, d = x2.shape
    nblk, _, n = w_up.shape
    half = nblk // 2
    f = half * n
    t = _tile(s, _FFN_FUSED_T)
    nt = s // t
    hrows = 16

    def body(dxb_ref, g0_ref, g0h_ref, gl_ref, udgl_ref, wdown_ref, wup_ref, w_ref, x2_ref, g_ref, dx3_ref,
             dgu_ref, dx2_ref, dx2b_ref, dgn_ref, dw_ref, db_ref, ext0_ref, ext1_ref, up0_ref, up1_ref, car_ref):
        i = pl.program_id(0)
        first_tile = i == nt - 1
        last_tile = i == 0

        @pl.when(last_tile)
        def _():
            dw_ref[...] = jnp.zeros_like(dw_ref)
            db_ref[...] = jnp.zeros_like(db_ref)
            car_ref[...] = jnp.zeros_like(car_ref)

        dxb = dxb_ref[...]
        total = None
        for j in range(half):
            cs = slice(j * n, (j + 1) * n)
            us = slice(f + j * n, f + (j + 1) * n)
            ext_ref = ext0_ref if j % 2 == 0 else ext1_ref
            up_ref = up0_ref if j % 2 == 0 else up1_ref
            dact = _dot(dxb, wdown_ref[cs, :], _NT)
            ext_ref[0:8, :] = jnp.where(first_tile, 0.0, g0h_ref[:, cs].astype(_F32)[hrows - 8:hrows])
            ext_ref[8:8 + t, :] = g0_ref[:, cs].astype(_F32)
            du = (dact * gl_ref[:, cs].astype(_F32)).astype(dgu_ref.dtype)
            dgu_ref[:, us] = du
            dg = dact * udgl_ref[:, cs].astype(_F32)
            db_ref[:, cs] += _colsum(dg)
            for k in range(_FFN_K):
                dw_ref[k:k + 1, cs] += _colsum(dg * ext_ref[pl.ds(8 - (_FFN_K - 1) + k, t), :])
            up_ref[0:t, :] = dg
            up_ref[t:t + 8, :] = car_ref[:, cs]
            car_ref[:, cs] = dg[0:8, :]
            dg0 = w_ref[0:1, cs] * up_ref[pl.ds(_FFN_K - 1, t), :]
            for k in range(1, _FFN_K):
                dg0 = dg0 + w_ref[k:k + 1, cs] * up_ref[pl.ds(_FFN_K - 1 - k, t), :]
            dg0 = dg0.astype(dgu_ref.dtype)
            dgu_ref[:, cs] = dg0
            p = _dot(dg0, wup_ref[j], _NT) + _dot(du, wup_ref[half + j], _NT)
            total = p if total is None else total + p
        _epi_rms_bwd(total, (x2_ref, g_ref, dx3_ref), (dx2_ref, dx2b_ref, dgn_ref), last_tile)

    def const(shape):
        return pl.BlockSpec(shape, lambda i: (0,) * len(shape), pipeline_mode=pl.Buffered(1))

    row = pl.BlockSpec((t, d), lambda i: (nt - 1 - i, 0))
    vec = pl.BlockSpec((1, d), lambda i: (0, 0))
    per = t // hrows
    wide = pl.BlockSpec((t, f), lambda i: (nt - 1 - i, 0))
    return _pcall(
        body, args=(dx3b, g0, g0, gl, udgl, w_down, w_up, fcw, x2, gnorm, dx3), grid=(nt,),
        in_specs=[row, wide, pl.BlockSpec((hrows, f), lambda i: (jnp.maximum((nt - 1 - i) * per - 1, 0), 0)),
                  wide, wide, const(w_down.shape), const(w_up.shape), const(fcw.shape), row, vec, row],
        out_specs=[pl.BlockSpec((t, 2 * f), lambda i: (nt - 1 - i, 0)), row, row, vec,
                   pl.BlockSpec((_FFN_K, f), lambda i: (0, 0)), pl.BlockSpec((1, f), lambda i: (0, 0))],
        out_shape=[jax.ShapeDtypeStruct((s, 2 * f), _MXU), jax.ShapeDtypeStruct((s, d), _F32),
                   jax.ShapeDtypeStruct((s, d), _MXU), jax.ShapeDtypeStruct((1, d), _F32),
                   jax.ShapeDtypeStruct((_FFN_K, f), _F32), jax.ShapeDtypeStruct((1, f), _F32)],
        scratch_shapes=[pltpu.VMEM((t + 8, n), _F32), pltpu.VMEM((t + 8, n), _F32),
                        pltpu.VMEM((t + 8, n), _F32), pltpu.VMEM((t + 8, n), _F32), pltpu.VMEM((8, f), _F32)],
        sem=("arbitrary",), name=name, comm=comm, vmem=_VMEM_LIMIT_FUSED)


def _mesh_pos():
    return lax.axis_index("x"), lax.axis_index("y"), lax.axis_index("c")


def _flip(v, bit):
    return 1 - v if bit else v


def _sem_scratch(n):
    return [pltpu.SemaphoreType.DMA((7 * n,)), pltpu.SemaphoreType.DMA((7 * n,)), pltpu.SemaphoreType.DMA((n,))]


class _Gather:
    def __init__(self, xs):
        self.ins = list(xs)
        self.outs = [jax.ShapeDtypeStruct((_NDEV,) + v.shape, v.dtype) for v in xs]
        self.scratch = _sem_scratch(len(xs))

    def _plan(self, x_refs, out_refs, sems):
        send_sems, recv_sems, local_sems = sems
        x, y, c = _mesh_pos()
        me, sibling = (x, y, c), (x, y, 1 - c)
        chips = [(1 - x, y), (x, 1 - y), (1 - x, 1 - y)]

        def copy(a, k, block, to, src=None):
            slot = out_refs[a].at[4 * block[0] + 2 * block[1] + block[2]]
            return pltpu.make_async_remote_copy(
                src_ref=slot if src is None else src, dst_ref=slot,
                send_sem=send_sems.at[a * 7 + k], recv_sem=recv_sems.at[a * 7 + k],
                device_id=to, device_id_type=_MESH_ID)

        def own(a):
            return pltpu.make_async_copy(x_refs[a], out_refs[a].at[4 * x + 2 * y + c], local_sems.at[a])

        def first(a):
            return [copy(a, 0, me, sibling, src=x_refs[a])] + [
                copy(a, 1 + j, me, (*chip, c), src=x_refs[a]) for j, chip in enumerate(chips)]

        return me, sibling, chips, c, copy, own, first

    def start(self, x_refs, out_refs, sems):
        _, _, _, _, _, own, first = self._plan(x_refs, out_refs, sems)
        for a in range(len(self.ins)):
            own(a).start()
            for cp in first(a):
                cp.start()

    def finish(self, x_refs, out_refs, sems):
        me, sibling, chips, c, copy, own, first = self._plan(x_refs, out_refs, sems)
        n = len(self.ins)
        passed = []
        for a in range(n):
            for j, chip in enumerate(chips):
                copy(a, 1 + j, (*chip, c), me).wait_recv()
                fwd = copy(a, 4 + j, (*chip, c), sibling)
                fwd.start()
                passed.append(fwd)
        for a in range(n):
            copy(a, 0, sibling, me).wait_recv()
            for j, chip in enumerate(chips):
                copy(a, 4 + j, (*chip, 1 - c), me).wait_recv()
        for a in range(n):
            for cp in first(a):
                cp.wait_send()
        for cp in passed:
            cp.wait_send()
        for a in range(n):
            own(a).wait()


class _Exchange:
    def __init__(self, gs):
        self.ins = list(gs)
        self.outs = [jax.ShapeDtypeStruct(v.shape, v.dtype) for v in gs]
        self.scratch = _sem_scratch(len(gs))

    def _plan(self, g_refs, r_refs, sems):
        send_sems, recv_sems, local_sems = sems
        x, y, c = _mesh_pos()
        me_idx = 4 * x + 2 * y + c
        n = len(self.ins)

        def copy(a, k):
            peer = (_flip(x, k & 4), _flip(y, k & 2), _flip(c, k & 1))
            peer_idx = 4 * peer[0] + 2 * peer[1] + peer[2]
            return pltpu.make_async_remote_copy(
                src_ref=g_refs[a].at[peer_idx], dst_ref=r_refs[a].at[me_idx],
                send_sem=send_sems.at[a * 7 + k - 1], recv_sem=recv_sems.at[a * 7 + k - 1],
                device_id=peer, device_id_type=_MESH_ID)

        copies = [copy(a, k) for a in range(n) for k in range(1, _NDEV)]
        mine = [pltpu.make_async_copy(g_refs[a].at[me_idx], r_refs[a].at[me_idx], local_sems.at[a])
                for a in range(n)]
        return copies, mine

    def start(self, g_refs, r_refs, sems):
        copies, mine = self._plan(g_refs, r_refs, sems)
        for cp in copies + mine:
            cp.start()

    def finish(self, g_refs, r_refs, sems):
        copies, mine = self._plan(g_refs, r_refs, sems)
        for cp in copies:
            cp.wait_recv()
        for cp in copies:
            cp.wait_send()
        for cp in mine:
            cp.wait()


class _Both:
    def __init__(self, first, second):
        self.parts = (first, second)
        self.ins = first.ins + second.ins
        self.outs = first.outs + second.outs
        self.scratch = first.scratch + second.scratch

    def _split(self, ins, outs, sems):
        a, b = self.parts
        na, nb = len(a.ins), len(a.scratch)
        return (a, ins[:na], outs[:na], sems[:nb]), (b, ins[na:], outs[na:], sems[nb:])

    def start(self, ins, outs, sems):
        for part, i, o, s in self._split(ins, outs, sems):
            part.start(i, o, s)

    def finish(self, ins, outs, sems):
        for part, i, o, s in self._split(ins, outs, sems):
            part.finish(i, o, s)


def _comm_call(comm, name):
    def body(*refs):
        n_i, n_o = len(comm.ins), len(comm.outs)
        ins, outs, sems = refs[:n_i], refs[n_i:n_i + n_o], refs[n_i + n_o:]
        comm.start(ins, outs, sems)
        comm.finish(ins, outs, sems)

    return pl.pallas_call(
        body, out_shape=list(comm.outs), in_specs=[_ANY] * len(comm.ins), out_specs=[_ANY] * len(comm.outs),
        scratch_shapes=list(comm.scratch), name=name)(*comm.ins)


def _adamw_math(w, g, m, v):
    m = _ADAM_B1 * m + (1.0 - _ADAM_B1) * g
    v = _ADAM_B2 * v + (1.0 - _ADAM_B2) * (g * g)
    m_hat = m / (1.0 - _ADAM_B1 ** _ADAM_STEP)
    v_hat = v / (1.0 - _ADAM_B2 ** _ADAM_STEP)
    delta = -_ADAM_LR * (m_hat / (jnp.sqrt(v_hat) + _ADAM_EPS) + _ADAM_WD * w)
    return delta, m, v


def _sum_adamw(parts, w, m, v, name):
    r, c = w.shape
    tr = _tile(r, 128)

    def body(p_ref, w_ref, m_ref, v_ref, g_ref, d_ref, nm_ref, nv_ref):
        g = p_ref[0].astype(_F32)
        for j in range(1, _NDEV):
            g = g + p_ref[j].astype(_F32)
        delta, nm, nv = _adamw_math(w_ref[...], g, m_ref[...], v_ref[...])
        g_ref[...] = g
        d_ref[...] = delta
        nm_ref[...] = nm
        nv_ref[...] = nv

    blk = pl.BlockSpec((tr, c), lambda i: (i, 0))
    return pl.pallas_call(
        body, grid=(r // tr,),
        in_specs=[pl.BlockSpec((_NDEV, tr, c), lambda i: (0, i, 0)), blk, blk, blk],
        out_specs=[blk] * 4, out_shape=[jax.ShapeDtypeStruct((r, c), _F32)] * 4,
        compiler_params=_cparams("parallel"), name=name,
    )(parts, w, m, v)


def _sum8(parts, name):
    _, r, c = parts.shape

    def body(p_ref, o_ref):
        g = p_ref[0]
        for j in range(1, _NDEV):
            g = g + p_ref[j]
        o_ref[...] = g

    return pl.pallas_call(
        body, grid=(1,), in_specs=[pl.BlockSpec((_NDEV, r, c), lambda i: (0, 0, 0))],
        out_specs=pl.BlockSpec((r, c), lambda i: (0, 0)), out_shape=jax.ShapeDtypeStruct((r, c), _F32),
        compiler_params=_cparams("arbitrary"), name=name,
    )(parts)


def _adamw_flat(g, w, m, v, name):
    r, c = w.shape

    def body(g_ref, w_ref, m_ref, v_ref, d_ref, nm_ref, nv_ref):
        delta, nm, nv = _adamw_math(w_ref[...], g_ref[...], m_ref[...], v_ref[...])
        d_ref[...] = delta
        nm_ref[...] = nm
        nv_ref[...] = nv

    blk = pl.BlockSpec((r, c), lambda i: (0, 0))
    return pl.pallas_call(
        body, grid=(1,), in_specs=[blk] * 4, out_specs=[blk] * 3,
        out_shape=[jax.ShapeDtypeStruct((r, c), _F32)] * 3,
        compiler_params=_cparams("arbitrary"), name=name,
    )(g, w, m, v)


def _pack(arrs):
    flat = jnp.concatenate([a.reshape(-1).astype(_F32) for a in arrs])
    pad = (-flat.shape[0]) % 1024
    return jnp.pad(flat, (0, pad)).reshape(-1, 128)


def _unpack(flat2d, shapes):
    flat = flat2d.reshape(-1)
    out, off = [], 0
    for sh in shapes:
        size = 1
        for dim in sh:
            size *= dim
        out.append(flat[off:off + size].reshape(sh))
        off += size
    return out


def _block_diag(w):
    h, hd, _ = w.shape
    eye = jnp.eye(h, dtype=w.dtype)
    return (eye[:, None, :, None] * w[:, :, None, :]).reshape(h * hd, h * hd)


def _diag_blocks(full, h):
    hd = full.shape[0] // h
    return jnp.stack([full[i * hd:(i + 1) * hd, i * hd:(i + 1) * hd] for i in range(h)])


def kernel(x, mem, mix_norm_g, w_in, lru_conv_w, lru_conv_b, lru_w_a, lru_b_a, lru_w_x, lru_b_x, lru_lambda, conf_conv_w, conf_conv_b, conf_ln_g, conf_ln_b, w_out, xa_norm_g, mem_norm_g, w_q, w_kv, w_o, ffn_norm_g, w_up, ffn_conv_w, ffn_conv_b, w_down, final_norm_g, loss_target, m_mix_norm_g, m_w_in, m_lru_conv_w, m_lru_conv_b, m_lru_w_a, m_lru_b_a, m_lru_w_x, m_lru_b_x, m_lru_lambda, m_conf_conv_w, m_conf_conv_b, m_conf_ln_g, m_conf_ln_b, m_w_out, m_xa_norm_g, m_mem_norm_g, m_w_q, m_w_kv, m_w_o, m_ffn_norm_g, m_w_up, m_ffn_conv_w, m_ffn_conv_b, m_w_down, m_final_norm_g, v_mix_norm_g, v_w_in, v_lru_conv_w, v_lru_conv_b, v_lru_w_a, v_lru_b_a, v_lru_w_x, v_lru_b_x, v_lru_lambda, v_conf_conv_w, v_conf_conv_b, v_conf_ln_g, v_conf_ln_b, v_w_out, v_xa_norm_g, v_mem_norm_g, v_w_q, v_w_kv, v_w_o, v_ffn_norm_g, v_w_up, v_ffn_conv_w, v_ffn_conv_b, v_w_down, v_final_norm_g):
    names = ["mix_norm_g", "w_in", "lru_conv_w", "lru_conv_b", "lru_w_a", "lru_b_a", "lru_w_x", "lru_b_x",
             "lru_lambda", "conf_conv_w", "conf_conv_b", "conf_ln_g", "conf_ln_b", "w_out", "xa_norm_g",
             "mem_norm_g", "w_q", "w_kv", "w_o", "ffn_norm_g", "w_up", "ffn_conv_w", "ffn_conv_b", "w_down",
             "final_norm_g"]
    loc = locals()
    W = {n: loc[n] for n in names}
    M = {n: loc["m_" + n] for n in names}
    V = {n: loc["v_" + n] for n in names}
    big = ["w_in", "w_out", "w_q", "w_kv", "w_o", "w_up", "w_down"]
    conv_sharded = ["lru_conv_w", "conf_conv_w", "ffn_conv_w"]

    xs = x[0]
    mems = mem[0]
    tgt = loss_target[0]
    me = 4 * lax.axis_index("x") + 2 * lax.axis_index("y") + lax.axis_index("c")

    conv_shapes = [W[n].shape[1:] for n in conv_sharded]
    conv_pack = _pack([W[n][0] for n in conv_sharded])
    shard = {n: W[n][0].astype(_XFER) for n in big}
    h1, (g_in, g_out, g_conv) = _rms_fwd(
        xs, mix_norm_g, "rms1_fwd", comm=_Gather([shard["w_in"], shard["w_out"], conv_pack]))
    convs = [[] for _ in conv_sharded]
    for j in range(_NDEV):
        for idx, part in enumerate(_unpack(g_conv[j], conv_shapes)):
            convs[idx].append(part)
    lcw, ccw, fcw = [jnp.concatenate(parts, axis=-1) for parts in convs]

    wab = jnp.concatenate([_block_diag(lru_w_a[0]), _block_diag(lru_w_x[0])], axis=1).astype(_MXU)
    mixer_params = (lcw, lru_conv_b, wab, lru_b_a, lru_b_x, lru_lambda, ccw, conf_conv_b, conf_ln_g, conf_ln_b)

    w_out_f = g_out.reshape(-1, g_out.shape[-1])
    (z, ycat, hs, cc, x1, h2), (g_q, g_kv, g_o, g_up, g_down) = _mixer_fwd(
        xs, h1, g_in, w_out_f, xa_norm_g, *mixer_params, "mixer_fwd",
        comm=_Gather([shard[n] for n in ("w_q", "w_kv", "w_o", "w_up", "w_down")]))
    w_q_f = g_q.reshape(-1, g_q.shape[-1])
    w_o_f = g_o.reshape(-1, g_o.shape[-1])
    w_down_f = g_down.reshape(-1, g_down.shape[-1])
    row32, row16, vec32 = (_F32, "row"), (_MXU, "row"), (_F32, "vec")
    mn = _rms_fwd(mems, mem_norm_g, "rmsm_fwd")
    kv = _mm_nn_stacked(mn, g_kv, _MXU, "mm_kv_fwd")
    q, o, x2, h3 = _attn_fwd(h2, x1, w_q_f, kv, w_o_f, ffn_norm_g, "attn_fwd")

    gfin = final_norm_g.reshape(1, -1)
    g0, act, gelu_g, u_dgelu, dx3, dx3b, lvec, dg_final = _ffn_fused_fwd(
        h3, g_up, w_down_f, fcw, ffn_conv_b, x2, tgt, gfin, "ffn_fwd")

    def rows8(p):
        return p.reshape(_NDEV, p.shape[0] // _NDEV, p.shape[1])

    p_down = _mm_tn_nat(act, dx3b, _XFER, "mm_down_wgrad", ts=2048)
    (dgu, dx2, dx2b, dg_ffn, dfcw, dfcb), (r_down,) = _ffn_fused_bwd(
        dx3b, g0, gelu_g, u_dgelu, w_down_f, g_up, fcw, x2, ffn_norm_g, dx3, "ffn_bwd",
        comm=_Exchange([rows8(p_down)]))
    p_up = _mm_tn_stacked(h3, dgu, _NDEV, _XFER, "mm_up_wgrad", slabs=2)

    dx1, dx1b, dg_xa, dk, dv, p_o, p_q = _attn_bwd(
        dx2b, dx2, q, o, h2, x1, kv, w_o_f, w_q_f, xa_norm_g, "attn_bwd")
    dkv = jnp.concatenate([dk, dv], axis=1).astype(_MXU)
    dmn = _mm_nt_stacked(dkv, g_kv, "mm_kv_dgrad", outs=[row32], slabs=_NDEV)
    p_kv = _mm_tn_stacked(mn, dkv, _NDEV, _XFER, "mm_kv_wgrad", slabs=_NDEV)
    _, _, dg_mem = _rms_bwd(dmn, mems, mem_norm_g, None, "rmsm_bwd")

    p_out = _mm_tn_nat(ycat, dx1b, _XFER, "mm_out_wgrad", ts=2048)
    ((dz, p_in, dlcw, dlcb, dwab, dba, dbx, dlam, dccw, dccb, dlng, dlnb),
     (r_up, r_o, r_q, r_kv, r_out)) = _mixer_bwd(
        dx1b, w_out_f, h1, z, hs, cc, *mixer_params, "mixer_bwd",
        comm=_Exchange([p_up, rows8(p_o), rows8(p_q), p_kv, rows8(p_out)]))

    c = _D_LRU
    heads = lru_w_a.shape[1]
    small_partial = {
        "lru_conv_w": dlcw, "lru_conv_b": dlcb,
        "lru_w_a": _diag_blocks(dwab[:, :c], heads), "lru_b_a": dba,
        "lru_w_x": _diag_blocks(dwab[:, c:], heads), "lru_b_x": dbx, "lru_lambda": dlam,
        "conf_conv_w": dccw, "conf_conv_b": dccb, "conf_ln_g": dlng, "conf_ln_b": dlnb,
        "xa_norm_g": dg_xa, "mem_norm_g": dg_mem, "ffn_norm_g": dg_ffn,
        "ffn_conv_w": dfcw, "ffn_conv_b": dfcb, "final_norm_g": dg_final,
    }
    early = list(small_partial)
    early_shapes = [small_partial[n].shape for n in early] + [lvec.shape]
    (grad_x, dg_mix), (r_in, early_all) = _mm_nt_stacked(
        dz, g_in, "mm_in_dgrad", epi=_epi_rms_bwd, extra=[(xs, "row"), (mix_norm_g, "vec"), (dx1, "row")],
        outs=[row32, vec32], tm=512, slabs=_NDEV,
        comm=_Both(_Exchange([p_in]), _Gather([_pack([small_partial[n] for n in early] + [lvec])])))
    (mix_all,) = _comm_call(_Gather([dg_mix]), "gather_mix_grad")
    small = early + ["mix_norm_g"]
    small_sum = _unpack(_sum8(early_all, "sum_small_grads"), early_shapes)
    loss = 0.5 * jnp.sum(small_sum.pop()) / xs.shape[1]
    small_sum.append(_sum8(mix_all.reshape(_NDEV, 8, -1), "sum_mix_grad").reshape(dg_mix.shape))
    received = {"w_in": r_in, "w_out": r_out, "w_q": r_q, "w_kv": r_kv, "w_o": r_o, "w_up": r_up,
                "w_down": r_down}

    grads, deltas, new_m, new_v = {}, {}, {}, {}
    for n, rec in ((n, received[n]) for n in big):
        shp = W[n].shape
        w2, m2, v2 = (t.reshape(rec.shape[1:]) for t in (W[n], M[n], V[n]))
        outs = _sum_adamw(rec, w2, m2, v2, "adamw_" + n)
        grads[n], deltas[n], new_m[n], new_v[n] = (t.reshape(shp) for t in outs)

    small_g = []
    for n, g in zip(small, small_sum):
        if n in conv_sharded:
            width = W[n].shape[-1]
            g = lax.dynamic_slice_in_dim(g, me * width, width, axis=1)
        small_g.append(g.reshape(W[n].shape))
    small_shapes = [W[n].shape for n in small]
    sd, sm, sv = _adamw_flat(_pack(small_g), _pack([W[n] for n in small]), _pack([M[n] for n in small]),
                             _pack([V[n] for n in small]), "adamw_small")
    for n, g, d_, m_, v_ in zip(small, small_g, _unpack(sd, small_shapes), _unpack(sm, small_shapes),
                                _unpack(sv, small_shapes)):
        grads[n], deltas[n], new_m[n], new_v[n] = g, d_, m_, v_

    return (loss, grad_x[None], *[grads[n] for n in names], *[deltas[n] for n in names],
            *[new_m[n] for n in names], *[new_v[n] for n in names])
```

```python
import functools

import jax
import jax.numpy as jnp
from jax import lax
from jax.experimental import pallas as pl
from jax.experimental.pallas import tpu as pltpu

_MXU = jnp.bfloat16
_XFER = jnp.bfloat16
_F32 = jnp.float32
_EPS = 1e-6
_NDEV = 8
_VMEM_LIMIT = 48 * 1024 * 1024

_D_LRU = 512
_XA_HEADS = 4
_RG_C = 8.0
_ADAM_LR, _ADAM_B1, _ADAM_B2, _ADAM_EPS, _ADAM_WD, _ADAM_STEP = 0.001, 0.9, 0.999, 1e-08, 0.01, 10

_MESH_ID = pl.DeviceIdType.MESH
_ANY = pl.BlockSpec(memory_space=pl.ANY)


def _cparams(*sem, vmem=_VMEM_LIMIT):
    return pltpu.CompilerParams(dimension_semantics=tuple(sem), vmem_limit_bytes=vmem)


def _pcall(body, *, args, grid, in_specs, out_specs, out_shape, sem, name, scratch_shapes=(), comm=None,
           vmem=_VMEM_LIMIT):
    outs_l = list(out_shape) if isinstance(out_shape, (list, tuple)) else [out_shape]
    ospecs_l = list(out_specs) if isinstance(out_specs, (list, tuple)) else [out_specs]
    n_in, n_out, n_scr = len(args), len(outs_l), len(scratch_shapes)
    if comm is None:
        res = pl.pallas_call(
            body, grid=grid, in_specs=list(in_specs), out_specs=ospecs_l, out_shape=outs_l,
            scratch_shapes=list(scratch_shapes), compiler_params=_cparams(*sem, vmem=vmem), name=name)(*args)
        return list(res), []
    n_ci, n_co = len(comm.ins), len(comm.outs)

    def wrapped(*refs):
        ins, cins = refs[:n_in], refs[n_in:n_in + n_ci]
        o = n_in + n_ci
        outs, couts = refs[o:o + n_out], refs[o + n_out:o + n_out + n_co]
        s = o + n_out + n_co
        scr, cscr = refs[s:s + n_scr], refs[s + n_scr:]
        first = pl.program_id(0) == 0
        last = pl.program_id(0) == grid[0] - 1
        for ax in range(1, len(grid)):
            first = jnp.logical_and(first, pl.program_id(ax) == 0)
            last = jnp.logical_and(last, pl.program_id(ax) == grid[ax] - 1)

        @pl.when(first)
        def _():
            comm.start(cins, couts, cscr)

        body(*ins, *outs, *scr)

        @pl.when(last)
        def _():
            comm.finish(cins, couts, cscr)

    res = pl.pallas_call(
        wrapped, grid=grid, in_specs=list(in_specs) + [_ANY] * n_ci, out_specs=ospecs_l + [_ANY] * n_co,
        out_shape=outs_l + list(comm.outs), scratch_shapes=list(scratch_shapes) + list(comm.scratch),
        compiler_params=_cparams(*(("arbitrary",) * len(grid)), vmem=vmem), name=name)(*args, *comm.ins)
    return list(res[:n_out]), list(res[n_out:])


def _sigmoid(v):
    return 1.0 / (1.0 + jnp.exp(-v))


_GELU_C = 0.7978845608028654
_GELU_K = 0.044715


def _gelu(v):
    t = jnp.tanh(_GELU_C * (v + _GELU_K * v * v * v))
    return 0.5 * v * (1.0 + t)


def _gelu_and_grad(v):
    v2 = v * v
    s = 0.5 * jnp.tanh(v * (_GELU_C + (_GELU_C * _GELU_K) * v2)) + 0.5
    g = v * s
    dg = s + (g * (1.0 - s)) * ((2.0 * _GELU_C) + (6.0 * _GELU_C * _GELU_K) * v2)
    return g, dg


def _softplus(v):
    e = jnp.exp(-jnp.abs(v))
    log1p = jnp.where(e < 1e-2, e * (1.0 - e * (0.5 - e * (1.0 / 3.0))), jnp.log(1.0 + e))
    return jnp.maximum(v, 0.0) + log1p


def _neg_expm1(v):
    series = -v * (1.0 + v * (0.5 + v * ((1.0 / 6.0) + v * (1.0 / 24.0))))
    return jnp.where(v > -0.0625, series, 1.0 - jnp.exp(v))


def _dot(a, b, dims):
    return lax.dot_general(a.astype(_MXU), b.astype(_MXU), (dims, ((), ())), preferred_element_type=_F32)


_NN = ((1,), (0,))
_NT = ((1,), (1,))
_TN = ((0,), (0,))


def _scan_fwd(a, b, rows):
    n = a.shape[0]
    d = 1
    while d < n:
        keep = rows >= d
        b = jnp.where(keep, b + a * pltpu.roll(b, d, 0), b)
        a = jnp.where(keep, a * pltpu.roll(a, d, 0), a)
        d *= 2
    return a, b


def _scan_rev(a, b, rows):
    n = a.shape[0]
    d = 1
    while d < n:
        keep = rows < n - d
        b = jnp.where(keep, b + a * pltpu.roll(b, n - d, 0), b)
        a = jnp.where(keep, a * pltpu.roll(a, n - d, 0), a)
        d *= 2
    return a, b


def _colsum(v):
    return jnp.sum(v, axis=0, keepdims=True)


def _mm(a, b, *, dims, grid, a_spec, b_spec, outs, acc_shape, name, extra=(), epi=None, slabs=1, comm=None):
    nred = grid[-1]
    red_axis = len(grid) - 1
    n_ex, n_out = len(extra), len(outs)
    epi = _epi_store if epi is None else epi

    def body(*refs):
        a_ref, b_ref = refs[:2]
        ex, o_refs, acc_ref = refs[2:2 + n_ex], refs[2 + n_ex:2 + n_ex + n_out], refs[-1]
        if slabs == 1:
            p = _dot(a_ref[...], b_ref[...], dims)
        else:
            n = b_ref.shape[-1]
            p = _dot(a_ref[:, 0:n], b_ref[0], dims)
            for jj in range(1, slabs):
                p = p + _dot(a_ref[:, jj * n:(jj + 1) * n], b_ref[jj], dims)

        first_rows = pl.program_id(0) == 0
        if nred == 1:
            epi(p, ex, o_refs, first_rows)
        else:
            k = pl.program_id(red_axis)

            @pl.when(k == 0)
            def _():
                acc_ref[...] = p

            @pl.when(jnp.logical_and(k > 0, k < nred - 1))
            def _():
                acc_ref[...] += p

            @pl.when(k == nred - 1)
            def _():
                epi(acc_ref[...] + p, ex, o_refs, first_rows)

    sem = ("parallel",) * (len(grid) - 1) + ("arbitrary",)
    if any(o[0].shape[0] == 1 for o in outs):
        sem = ("arbitrary",) * len(grid)
    res, cres = _pcall(
        body, args=(a, b) + tuple(e[0] for e in extra), grid=grid,
        in_specs=[a_spec, b_spec] + [e[1] for e in extra],
        out_specs=[o[1] for o in outs], out_shape=[o[0] for o in outs],
        scratch_shapes=[pltpu.VMEM(acc_shape if nred > 1 else (8, 128), _F32)], sem=sem, name=name, comm=comm)
    res = res[0] if n_out == 1 else res
    return res if comm is None else (res, cres)


def _epi_store(total, ex, outs, first_rows):
    outs[0][...] = total.astype(outs[0].dtype)


def _epi_residual_rms(total, ex, outs, first_rows):
    res_ref, g_ref = ex
    xn = total + res_ref[...]
    outs[0][...] = xn
    r = lax.rsqrt(jnp.mean(xn * xn, axis=-1, keepdims=True) + _EPS)
    outs[1][...] = (xn * r * g_ref[...]).astype(outs[1].dtype)


def _epi_rms_bwd(total, ex, outs, first_rows):
    x_ref, g_ref, dres_ref = ex
    dg_ref = outs[-1]
    xv = x_ref[...]
    r = lax.rsqrt(jnp.mean(xv * xv, axis=-1, keepdims=True) + _EPS)
    xhat = xv * r
    dxh = total * g_ref[...]
    dx = dres_ref[...] + r * (dxh - xhat * jnp.mean(dxh * xhat, axis=-1, keepdims=True))
    for o_ref in outs[:-1]:
        o_ref[...] = dx.astype(o_ref.dtype)

    @pl.when(first_rows)
    def _():
        dg_ref[...] = jnp.zeros_like(dg_ref)

    dg_ref[...] += _colsum(total * xhat)


def _epi_final(total, ex, outs, first_rows):
    res_ref, t_ref, g_ref = ex
    dx_ref, dxb_ref, l_ref, dg_ref = outs
    xv = total + res_ref[...]
    gv = g_ref[...]
    d = xv.shape[-1]
    r = lax.rsqrt(jnp.mean(xv * xv, axis=-1, keepdims=True) + _EPS)
    xhat = xv * r
    err = xhat * gv - t_ref[...]
    dy = err * (1.0 / d)
    dxh = dy * gv
    dx = r * (dxh - xhat * jnp.mean(dxh * xhat, axis=-1, keepdims=True))
    dx_ref[...] = dx
    dxb_ref[...] = dx.astype(dxb_ref.dtype)

    @pl.when(first_rows)
    def _():
        l_ref[...] = jnp.zeros_like(l_ref)
        dg_ref[...] = jnp.zeros_like(dg_ref)

    l_ref[...] += _colsum(err * err)
    dg_ref[...] += _colsum(dy * xhat)


def _tile(m, cap):
    t = min(m, cap)
    assert m % t == 0
    return t


def _row_spec(tm, n):
    return pl.BlockSpec((tm, n), lambda i, *_: (i, 0))


def _vec_spec(n):
    return pl.BlockSpec((1, n), lambda *_: (0, 0))


def _row_io(m, n, tm, extra, outs):
    def spec(kind):
        return _row_spec(tm, n) if kind == "row" else _vec_spec(n)

    ex = [(arr, spec(kind)) for arr, kind in extra]
    os_ = [(jax.ShapeDtypeStruct((m, n) if kind == "row" else (1, n), dt), spec(kind)) for dt, kind in outs]
    return ex, os_


def _mm_nn_stacked(a, w, out_dtype, name, comm=None, tm=1024):
    m, k = a.shape
    j, _, n = w.shape
    tm = _tile(m, tm)
    return _mm(a, w, dims=_NN, grid=(m // tm, j, 1),
               a_spec=pl.BlockSpec((tm, k), lambda i, jj, r: (i, 0)),
               b_spec=pl.BlockSpec((None, k, n), lambda i, jj, r: (jj, 0, 0)),
               outs=[(jax.ShapeDtypeStruct((m, j * n), out_dtype), pl.BlockSpec((tm, n), lambda i, jj, r: (i, jj)))],
               acc_shape=(tm, n), name=name, comm=comm)


def _mm_nt_stacked(dc, w, name, *, outs, extra=(), epi=None, comm=None, tm=1024, slabs=1):
    m = dc.shape[0]
    j, k, n = w.shape
    tm = _tile(m, tm)
    assert j % slabs == 0
    ex, os_ = _row_io(m, k, tm, extra, outs)
    wblk = (None, k, n) if slabs == 1 else (slabs, k, n)
    return _mm(dc, w, dims=_NT, grid=(m // tm, j // slabs),
               a_spec=pl.BlockSpec((tm, slabs * n), lambda i, r: (i, r)),
               b_spec=pl.BlockSpec(wblk, lambda i, r: (r, 0, 0)),
               outs=os_, extra=ex, epi=epi, acc_shape=(tm, k), name=name, slabs=slabs, comm=comm)


def _mm_tn_stacked(a, dc, j, out_dtype, name, slabs=1, ts=1024):
    s, k = a.shape
    n = dc.shape[1] // j
    ts = _tile(s, ts)
    assert j % slabs == 0

    def epi(total, ex, outs, first_rows):
        for jj in range(slabs):
            outs[0][jj] = total[:, jj * n:(jj + 1) * n].astype(outs[0].dtype)

    return _mm(a, dc, dims=_TN, grid=(j // slabs, s // ts),
               a_spec=pl.BlockSpec((ts, k), lambda jj, r: (r, 0)),
               b_spec=pl.BlockSpec((ts, slabs * n), lambda jj, r: (r, jj)),
               outs=[(jax.ShapeDtypeStruct((j, k, n), out_dtype),
                      pl.BlockSpec((slabs, k, n), lambda jj, r: (jj, 0, 0)))],
               epi=epi, acc_shape=(k, slabs * n), name=name)


def _mm_tn_nat(a, dc, out_dtype, name, ts=1024):
    s, kt = a.shape
    n = dc.shape[1]
    ts = _tile(s, ts)
    tkb = _tile(kt, 512)
    return _mm(a, dc, dims=_TN, grid=(kt // tkb, s // ts),
               a_spec=pl.BlockSpec((ts, tkb), lambda kb, r: (r, kb)),
               b_spec=pl.BlockSpec((ts, n), lambda kb, r: (r, 0)),
               outs=[(jax.ShapeDtypeStruct((kt, n), out_dtype), pl.BlockSpec((tkb, n), lambda kb, r: (kb, 0)))],
               acc_shape=(tkb, n), name=name)


def _rms_fwd(x, g, name, comm=None):
    s, d = x.shape
    t = _tile(s, 1024)

    def body(x_ref, g_ref, h_ref):
        xv = x_ref[...]
        r = lax.rsqrt(jnp.mean(xv * xv, axis=-1, keepdims=True) + _EPS)
        h_ref[...] = (xv * r * g_ref[...]).astype(h_ref.dtype)

    res, cres = _pcall(
        body, args=(x, g), grid=(s // t,),
        in_specs=[pl.BlockSpec((t, d), lambda i: (i, 0)), pl.BlockSpec((1, d), lambda i: (0, 0))],
        out_specs=pl.BlockSpec((t, d), lambda i: (i, 0)),
        out_shape=jax.ShapeDtypeStruct((s, d), _MXU), sem=("parallel",), name=name, comm=comm)
    return res[0] if comm is None else (res[0], cres)


def _rms_bwd(dh, x, g, dres, name):
    s, d = x.shape
    t = _tile(s, 256)
    has_res = dres is not None

    def body(*refs):
        if has_res:
            dh_ref, x_ref, g_ref, dres_ref, dx_ref, dxb_ref, dg_ref = refs
        else:
            dh_ref, x_ref, g_ref, dx_ref, dxb_ref, dg_ref = refs
        xv = x_ref[...]
        dhv = dh_ref[...]
        r = lax.rsqrt(jnp.mean(xv * xv, axis=-1, keepdims=True) + _EPS)
        xhat = xv * r
        dxh = dhv * g_ref[...]
        dx = r * (dxh - xhat * jnp.mean(dxh * xhat, axis=-1, keepdims=True))
        if has_res:
            dx = dx + dres_ref[...]
        dx_ref[...] = dx
        dxb_ref[...] = dx.astype(dxb_ref.dtype)

        @pl.when(pl.program_id(0) == 0)
        def _():
            dg_ref[...] = jnp.zeros_like(dg_ref)

        dg_ref[...] += _colsum(dhv * xhat)

    row = pl.BlockSpec((t, d), lambda i: (i, 0))
    vec = pl.BlockSpec((1, d), lambda i: (0, 0))
    in_specs = [row, row, vec] + ([row] if has_res else [])
    args = (dh, x, g) + ((dres,) if has_res else ())
    return pl.pallas_call(
        body, grid=(s // t,), in_specs=in_specs, out_specs=[row, row, vec],
        out_shape=[jax.ShapeDtypeStruct((s, d), _F32), jax.ShapeDtypeStruct((s, d), _MXU),
                   jax.ShapeDtypeStruct((1, d), _F32)],
        compiler_params=_cparams("arbitrary"), name=name,
    )(*args)


_LRU_K = 4
_CONF_K = 31
_LRU_HALO = 8
_CONF_HALO = 32
_MIX_T = 512


def _lru_gates(lx, wab_ref, ba_ref, bx_ref, lam_ref):
    c = _D_LRU
    pre = _dot(lx, wab_ref[...], _NN)
    r = _sigmoid(pre[:, :c] + ba_ref[...])
    ig = _sigmoid(pre[:, c:] + bx_ref[...])
    sp = _softplus(-lam_ref[...])
    log_a = (-_RG_C) * r * sp
    a = jnp.exp(log_a)
    mult = jnp.sqrt(_neg_expm1(2.0 * log_a))
    return r, ig, sp, a, mult


def _causal_conv(ext_ref, halo, w_ref, b_ref, taps, t):
    acc = b_ref[...] + w_ref[0:1, :] * ext_ref[pl.ds(halo - (taps - 1), t), :]
    for k in range(1, taps):
        acc = acc + w_ref[k:k + 1, :] * ext_ref[pl.ds(halo - (taps - 1) + k, t), :]
    return acc


class _Windows:
    def __init__(self, ext_ref, shifted_ref, t):
        self.ext_ref, self.shifted_ref, self.t = ext_ref, shifted_ref, t
        rows = t + 24
        for r in range(1, 8):
            shifted_ref[r - 1, 0:rows, :] = ext_ref[pl.ds(r, rows), :]

    def __call__(self, off):
        q, r = divmod(off, 8)
        if r == 0:
            return self.ext_ref[pl.ds(8 * q, self.t), :]
        return self.shifted_ref[r - 1, pl.ds(8 * q, self.t), :]


def _mixer_fwd(xs, h1, w_in, w_out, gnorm, lcw, lcb, wab, ba, bx, lam, ccw, ccb, lng, lnb, name, comm=None):
    s, d = xs.shape
    nblk, _, n = w_in.shape
    c = _D_LRU
    t = _tile(s, _MIX_T)
    nt = s // t

    def body(x_ref, h_ref, win_ref, wout_ref, gn_ref,
             lcw_ref, lcb_ref, wab_ref, ba_ref, bx_ref, lam_ref, ccw_ref, ccb_ref, lng_ref, lnb_ref,
             z_ref, ycat_ref, hs_ref, cc_ref, x1_ref, h2_ref,
             ext_ref, cge_ref, hc_ref, shifted_ref, zprev_ref):
        i = pl.program_id(0)
        first = i == 0
        rows = lax.broadcasted_iota(jnp.int32, (t, c), 0)

        @pl.when(first)
        def _():
            zprev_ref[...] = jnp.zeros_like(zprev_ref)

        hv = h_ref[...]
        for j in range(nblk):
            z_ref[:, j * n:(j + 1) * n] = _dot(hv, win_ref[j], _NN)
        lx0_ref, gate_ref = z_ref.at[:, 0:c], z_ref.at[:, c:2 * c]
        ca_ref, cb_ref = z_ref.at[:, 2 * c:3 * c], z_ref.at[:, 3 * c:4 * c]
        lx0h_ref = zprev_ref.at[_CONF_HALO - _LRU_HALO:_CONF_HALO, 0:c]
        cah_ref, cbh_ref = zprev_ref.at[:, 2 * c:3 * c], zprev_ref.at[:, 3 * c:4 * c]

        ext_ref[0:_LRU_HALO, :] = jnp.where(first, 0.0, lx0h_ref[...])
        ext_ref[_LRU_HALO:_LRU_HALO + t, :] = lx0_ref[...]
        lx = _causal_conv(ext_ref, _LRU_HALO, lcw_ref, lcb_ref, _LRU_K, t)
        r, ig, sp, a, mult = _lru_gates(lx, wab_ref, ba_ref, bx_ref, lam_ref)
        u = mult * (ig * lx)
        a_cum, h_loc = _scan_fwd(a, u, rows)

        @pl.when(first)
        def _():
            hc_ref[...] = jnp.zeros_like(hc_ref)

        h = h_loc + a_cum * hc_ref[7:8, :]
        hs_ref[...] = h
        hc_ref[...] = hs_ref[pl.ds(t - 8, 8), :]
        ycat_ref[:, 0:c] = (h * _gelu(gate_ref[...])).astype(ycat_ref.dtype)

        cge_ref[0:_CONF_HALO, :] = jnp.where(first, 0.0, cah_ref[...] * _sigmoid(cbh_ref[...]))
        cge_ref[_CONF_HALO:_CONF_HALO + t, :] = ca_ref[...] * _sigmoid(cb_ref[...])
        win = _Windows(cge_ref, shifted_ref, t)
        first_off = _CONF_HALO - (_CONF_K - 1)
        cc = ccb_ref[...] + ccw_ref[0:1, :] * win(first_off)
        for k in range(1, _CONF_K):
            cc = cc + ccw_ref[k:k + 1, :] * win(first_off + k)
        cc_ref[...] = cc
        xc = cc - jnp.mean(cc, axis=-1, keepdims=True)
        rstd = lax.rsqrt(jnp.mean(xc * xc, axis=-1, keepdims=True) + _EPS)
        ln = xc * rstd * lng_ref[...] + lnb_ref[...]
        ycat_ref[:, c:2 * c] = (ln * _sigmoid(ln)).astype(ycat_ref.dtype)

        zprev_ref[...] = z_ref[pl.ds(t - _CONF_HALO, _CONF_HALO), :]
        y = _dot(ycat_ref[...], wout_ref[...], _NN)
        _epi_residual_rms(y, (x_ref, gn_ref), (x1_ref, h2_ref), first)

    def const(arr):
        return pl.BlockSpec(arr.shape, lambda i: (0,) * arr.ndim, pipeline_mode=pl.Buffered(1))

    def rows_of(width):
        return pl.BlockSpec((t, width), lambda i: (i, 0))

    params = (lcw, lcb, wab, ba, bx, lam, ccw, ccb, lng, lnb)
    res, cres = _pcall(
        body, args=(xs, h1, w_in, w_out, gnorm, *params), grid=(nt,),
        in_specs=[rows_of(d), rows_of(d), const(w_in), const(w_out), const(gnorm)] + [const(p) for p in params],
        out_specs=[rows_of(nblk * n), rows_of(2 * c), rows_of(c), rows_of(c), rows_of(d), rows_of(d)],
        out_shape=[jax.ShapeDtypeStruct((s, nblk * n), _F32), jax.ShapeDtypeStruct((s, 2 * c), _MXU),
                   jax.ShapeDtypeStruct((s, c), _F32), jax.ShapeDtypeStruct((s, c), _F32),
                   jax.ShapeDtypeStruct((s, d), _F32), jax.ShapeDtypeStruct((s, d), _MXU)],
        scratch_shapes=[pltpu.VMEM((t + _LRU_HALO, c), _F32), pltpu.VMEM((t + _CONF_HALO, c), _F32),
                        pltpu.VMEM((8, c), _F32), pltpu.VMEM((7, t + _CONF_HALO, c), _F32),
                        pltpu.VMEM((_CONF_HALO, nblk * n), _F32)],
        sem=("arbitrary",), name=name, comm=comm)
    return res, cres


def _mixer_bwd(dx1b, w_out, h1, z, hs, cc, lcw, lcb, wab, ba, bx, lam, ccw, ccb, lng, lnb, name, comm=None):
    s = z.shape[0]
    d = h1.shape[1]
    c = _D_LRU
    t = _tile(s, _MIX_T)
    nt = s // t
    nblk = z.shape[1] // 256

    def body(dxb_ref, wout_ref, h1_ref, lx0_ref, lx0h_ref, gate_ref, ca_ref, cah_ref, cb_ref, cbh_ref,
             hs_ref, hsh_ref, cc_ref,
             lcw_ref, lcb_ref, wab_ref, ba_ref, bx_ref, lam_ref, ccw_ref, ccb_ref, lng_ref, lnb_ref,
             dz_ref, pin_ref, dlcw_ref, dlcb_ref, dwab_ref, dba_ref, dbx_ref, dlam_ref, dccw_ref, dccb_ref, dlng_ref,
             dlnb_ref,
             ext_ref, up_ref, cge_ref, dce_ref, xc_ref, dlxc_ref, dccc_ref, shifted_ref, dwin_ref):
        i = pl.program_id(0)
        first_tile = i == nt - 1
        last_tile = i == 0
        rows = lax.broadcasted_iota(jnp.int32, (t, c), 0)

        @pl.when(last_tile)
        def _():
            for ref in (dlcw_ref, dlcb_ref, dwab_ref, dba_ref, dbx_ref, dlam_ref, dccw_ref, dccb_ref, dlng_ref,
                        dlnb_ref, xc_ref, dlxc_ref, dccc_ref, dwin_ref):
                ref[...] = jnp.zeros_like(ref)

        dycat = _dot(dxb_ref[...], wout_ref[...], _NT)

        ext_ref[0:_LRU_HALO, :] = jnp.where(first_tile, 0.0, lx0h_ref[...])
        ext_ref[_LRU_HALO:_LRU_HALO + t, :] = lx0_ref[...]
        lx = _causal_conv(ext_ref, _LRU_HALO, lcw_ref, lcb_ref, _LRU_K, t)
        r, ig, sp, a, mult = _lru_gates(lx, wab_ref, ba_ref, bx_ref, lam_ref)
        h = hs_ref[...]
        gl, dgl = _gelu_and_grad(gate_ref[...])
        dyl = dycat[:, 0:c]
        dz_ref[:, c:2 * c] = (dyl * h * dgl).astype(dz_ref.dtype)
        dh = dyl * gl

        up_ref[0:t, :] = a
        up_ref[t:t + 8, :] = jnp.ones((8, c), _F32)
        a_up = up_ref[pl.ds(1, t), :]
        a_cum, g_loc = _scan_rev(a_up, dh, rows)
        gt = g_loc + a_cum * xc_ref[0:1, :]
        xc_ref[...] = (a * gt)[0:8, :]

        up_ref[0:8, :] = jnp.where(first_tile, 0.0, hsh_ref[...])
        up_ref[8:8 + t, :] = h
        hprev = up_ref[pl.ds(7, t), :]

        da = gt * hprev
        dmult = gt * ig * lx
        dig = gt * mult * lx
        dlx = gt * mult * ig
        dlog_a = da * a - dmult * a * a / mult
        dpre_r = dlog_a * (-_RG_C) * sp * r * (1.0 - r)
        dpre_i = dig * ig * (1.0 - ig)
        dlam_ref[...] += _colsum(dlog_a * r) * (_RG_C * _sigmoid(-lam_ref[...]))
        dba_ref[...] += _colsum(dpre_r)
        dbx_ref[...] += _colsum(dpre_i)
        dpre = jnp.concatenate([dpre_r, dpre_i], axis=1).astype(_MXU)
        dlx = dlx + _dot(dpre, wab_ref[...], _NT)
        dwab_ref[...] += _dot(lx, dpre, _TN)

        dlcb_ref[...] += _colsum(dlx)
        up_ref[0:t, :] = dlx
        up_ref[t:t + 8, :] = dlxc_ref[...]
        dlxc_ref[...] = dlx[0:8, :]
        acc = lcw_ref[0:1, :] * up_ref[pl.ds(_LRU_K - 1, t), :]
        for k in range(1, _LRU_K):
            acc = acc + lcw_ref[k:k + 1, :] * up_ref[pl.ds(_LRU_K - 1 - k, t), :]
        dz_ref[:, 0:c] = acc.astype(dz_ref.dtype)
        for k in range(_LRU_K):
            dlcw_ref[k:k + 1, :] += _colsum(dlx * ext_ref[pl.ds(_LRU_HALO - (_LRU_K - 1) + k, t), :])

        sig_b = _sigmoid(cb_ref[...])
        ca = ca_ref[...]
        cge_ref[0:_CONF_HALO, :] = jnp.where(first_tile, 0.0, cah_ref[...] * _sigmoid(cbh_ref[...]))
        cge_ref[_CONF_HALO:_CONF_HALO + t, :] = ca * sig_b
        ccv = cc_ref[...]
        xcen = ccv - jnp.mean(ccv, axis=-1, keepdims=True)
        rstd = lax.rsqrt(jnp.mean(xcen * xcen, axis=-1, keepdims=True) + _EPS)
        xn = xcen * rstd
        ln = xn * lng_ref[...] + lnb_ref[...]
        sg = _sigmoid(ln)
        dln = dycat[:, c:2 * c] * (sg * (1.0 + ln * (1.0 - sg)))
        dlng_ref[...] += _colsum(dln * xn)
        dlnb_ref[...] += _colsum(dln)
        dxn = dln * lng_ref[...]
        dcc = rstd * (dxn - jnp.mean(dxn, axis=-1, keepdims=True)
                      - xn * jnp.mean(dxn * xn, axis=-1, keepdims=True))
        dccb_ref[...] += _colsum(dcc)
        win = _Windows(cge_ref, shifted_ref, t)
        for k in range(_CONF_K):
            dccw_ref[k:k + 1, :] += _colsum(dcc * win(_CONF_HALO - (_CONF_K - 1) + k))
        dce_ref[0:t, :] = dcc
        dce_ref[t:t + _CONF_HALO, :] = dccc_ref[...]
        dccc_ref[...] = dcc[0:_CONF_HALO, :]
        win = _Windows(dce_ref, shifted_ref, t)
        dcg = ccw_ref[0:1, :] * win(_CONF_K - 1)
        for k in range(1, _CONF_K):
            dcg = dcg + ccw_ref[k:k + 1, :] * win(_CONF_K - 1 - k)
        dz_ref[:, 2 * c:3 * c] = (dcg * sig_b).astype(dz_ref.dtype)
        dz_ref[:, 3 * c:4 * c] = (dcg * ca * sig_b * (1.0 - sig_b)).astype(dz_ref.dtype)

        dwin_ref[...] += _dot(h1_ref[...], dz_ref[...], _TN)

        @pl.when(first_tile)
        def _():
            for j in range(nblk):
                pin_ref[j] = dwin_ref[:, j * 256:(j + 1) * 256].astype(pin_ref.dtype)

    def col(j):
        return pl.BlockSpec((t, c), lambda i: (nt - 1 - i, j))

    def halo(j, rows_):
        per = t // rows_
        return pl.BlockSpec((rows_, c), lambda i: (jnp.maximum((nt - 1 - i) * per - 1, 0), j))

    def full(shape):
        return pl.BlockSpec(shape, lambda i: (0,) * len(shape))

    params = (lcw, lcb, wab, ba, bx, lam, ccw, ccb, lng, lnb)
    small = [(_LRU_K, c), (1, c), (c, 2 * c), (1, c), (1, c), (1, c), (_CONF_K, c), (1, c), (1, c), (1, c)]
    wide = pl.BlockSpec((t, d), lambda i: (nt - 1 - i, 0))
    pin_shape = (nblk, d, 256)
    return _pcall(
        body, args=(dx1b, w_out, h1, z, z, z, z, z, z, z, hs, hs, cc, *params), grid=(nt,),
        in_specs=[wide, pl.BlockSpec(w_out.shape, lambda i: (0, 0), pipeline_mode=pl.Buffered(1)), wide,
                  col(0), halo(0, _LRU_HALO), col(1), col(2), halo(2, _CONF_HALO), col(3), halo(3, _CONF_HALO),
                  col(0), halo(0, 8), col(0)]
        + [full(p.shape) for p in params],
        out_specs=[pl.BlockSpec((t, 4 * c), lambda i: (nt - 1 - i, 0)), full(pin_shape)] + [full(sh) for sh in small],
        out_shape=[jax.ShapeDtypeStruct((s, 4 * c), _MXU), jax.ShapeDtypeStruct(pin_shape, _XFER)]
        + [jax.ShapeDtypeStruct(sh, _F32) for sh in small],
        scratch_shapes=[pltpu.VMEM((t + _LRU_HALO, c), _F32), pltpu.VMEM((t + 8, c), _F32),
                        pltpu.VMEM((t + _CONF_HALO, c), _F32), pltpu.VMEM((t + _CONF_HALO, c), _F32),
                        pltpu.VMEM((8, c), _F32), pltpu.VMEM((8, c), _F32), pltpu.VMEM((_CONF_HALO, c), _F32),
                        pltpu.VMEM((7, t + _CONF_HALO, c), _F32), pltpu.VMEM((d, 4 * c), _F32)],
        sem=("arbitrary",), name=name, comm=comm, vmem=_VMEM_LIMIT_FUSED)


_ATT_T = 512


def _attn_probs(qh, kh, scale):
    sc = _dot(qh, kh, _NT) * scale
    e = jnp.exp(sc - jnp.max(sc, axis=-1, keepdims=True))
    return e / jnp.sum(e, axis=-1, keepdims=True)


def _const_spec(arr):
    return pl.BlockSpec(arr.shape, lambda i: (0,) * arr.ndim, pipeline_mode=pl.Buffered(1))


def _attn_fwd(h2, x1, w_q, kv, w_o, gnorm, name):
    s, d = h2.shape
    nm = kv.shape[0]
    hd = d // _XA_HEADS
    t = _tile(s, _ATT_T)
    scale = hd ** -0.5

    def body(h_ref, x1_ref, wq_ref, k_ref, v_ref, wo_ref, g_ref, q_ref, o_ref, x2_ref, h3_ref):
        q_ref[...] = _dot(h_ref[...], wq_ref[...], _NN).astype(q_ref.dtype)
        for hh in range(_XA_HEADS):
            sl = slice(hh * hd, (hh + 1) * hd)
            p = _attn_probs(q_ref[:, sl], k_ref[:, sl], scale)
            o_ref[:, sl] = _dot(p, v_ref[:, sl], _NN).astype(o_ref.dtype)
        y = _dot(o_ref[...], wo_ref[...], _NN)
        _epi_residual_rms(y, (x1_ref, g_ref), (x2_ref, h3_ref), None)

    row = pl.BlockSpec((t, d), lambda i: (i, 0))
    half = pl.BlockSpec((nm, d), lambda i: (0, 0), pipeline_mode=pl.Buffered(1))
    half2 = pl.BlockSpec((nm, d), lambda i: (0, 1), pipeline_mode=pl.Buffered(1))
    return pl.pallas_call(
        body, grid=(s // t,),
        in_specs=[row, row, _const_spec(w_q), half, half2, _const_spec(w_o), _const_spec(gnorm)],
        out_specs=[row, row, row, row],
        out_shape=[jax.ShapeDtypeStruct((s, d), _MXU), jax.ShapeDtypeStruct((s, d), _MXU),
                   jax.ShapeDtypeStruct((s, d), _F32), jax.ShapeDtypeStruct((s, d), _MXU)],
        compiler_params=_cparams("parallel"), name=name,
    )(h2, x1, w_q, kv, kv, w_o, gnorm)


def _attn_bwd(dx2b, dx2, q, o, h2, x1, kv, w_o, w_q, gnorm, name):
    s, d = q.shape
    nm = kv.shape[0]
    hd = d // _XA_HEADS
    t = _tile(s, _ATT_T)
    nt = s // t
    scale = hd ** -0.5

    def body(dxb_ref, dx2_ref, q_ref, o_ref, h_ref, x1_ref, k_ref, v_ref, wo_ref, wq_ref, g_ref,
             dx1_ref, dx1b_ref, dgn_ref, dk_ref, dv_ref, pwo_ref, pwq_ref, dq_ref, awo_ref, awq_ref):
        i = pl.program_id(0)
        first = i == 0

        @pl.when(first)
        def _():
            for ref in (dk_ref, dv_ref, awo_ref, awq_ref):
                ref[...] = jnp.zeros_like(ref)

        dxb = dxb_ref[...]
        do = _dot(dxb, wo_ref[...], _NT).astype(_MXU)
        awo_ref[...] += _dot(o_ref[...], dxb, _TN)
        for hh in range(_XA_HEADS):
            sl = slice(hh * hd, (hh + 1) * hd)
            qh = q_ref[:, sl]
            kh = k_ref[:, sl]
            doh = do[:, sl]
            p = _attn_probs(qh, kh, scale)
            dp = _dot(doh, v_ref[:, sl], _NT)
            dv_ref[:, sl] += _dot(p, doh, _TN)
            ds = (p * (dp - jnp.sum(dp * p, axis=-1, keepdims=True)) * scale).astype(_MXU)
            dq_ref[:, sl] = _dot(ds, kh, _NN).astype(dq_ref.dtype)
            dk_ref[:, sl] += _dot(ds, qh, _TN)
        dq = dq_ref[...]
        awq_ref[...] += _dot(h_ref[...], dq, _TN)
        dh = _dot(dq, wq_ref[...], _NT)
        _epi_rms_bwd(dh, (x1_ref, g_ref, dx2_ref), (dx1_ref, dx1b_ref, dgn_ref), first)

        @pl.when(i == nt - 1)
        def _():
            pwo_ref[...] = awo_ref[...].astype(pwo_ref.dtype)
            pwq_ref[...] = awq_ref[...].astype(pwq_ref.dtype)

    row = pl.BlockSpec((t, d), lambda i: (i, 0))
    vec = pl.BlockSpec((1, d), lambda i: (0, 0))
    mem_blk = pl.BlockSpec((nm, d), lambda i: (0, 0))
    sq = pl.BlockSpec((d, d), lambda i: (0, 0))
    half = pl.BlockSpec((nm, d), lambda i: (0, 0), pipeline_mode=pl.Buffered(1))
    half2 = pl.BlockSpec((nm, d), lambda i: (0, 1), pipeline_mode=pl.Buffered(1))
    return pl.pallas_call(
        body, grid=(nt,),
        in_specs=[row, row, row, row, row, row, half, half2, _const_spec(w_o), _const_spec(w_q), _const_spec(gnorm)],
        out_specs=[row, row, vec, mem_blk, mem_blk, sq, sq],
        out_shape=[jax.ShapeDtypeStruct((s, d), _F32), jax.ShapeDtypeStruct((s, d), _MXU),
                   jax.ShapeDtypeStruct((1, d), _F32), jax.ShapeDtypeStruct((nm, d), _F32),
                   jax.ShapeDtypeStruct((nm, d), _F32), jax.ShapeDtypeStruct((d, d), _XFER),
                   jax.ShapeDtypeStruct((d, d), _XFER)],
        scratch_shapes=[pltpu.VMEM((t, d), _MXU), pltpu.VMEM((d, d), _F32), pltpu.VMEM((d, d), _F32)],
        compiler_params=_cparams("arbitrary", vmem=_VMEM_LIMIT_FUSED), name=name,
    )(dx2b, dx2, q, o, h2, x1, kv, kv, w_o, w_q, gnorm)


_FFN_K = 3
_FFN_FUSED_T = 256
_VMEM_LIMIT_FUSED = 58 * 1024 * 1024


def _ffn_fused_fwd(h3, w_up, w_down, fcw, fcb, x2, target, gfin, name):
    s, d = h3.shape
    nblk, _, n = w_up.shape
    half = nblk // 2
    f = half * n
    t = _tile(s, _FFN_FUSED_T)

    def body(h_ref, wup_ref, wdown_ref, w_ref, b_ref, x2_ref, t_ref, g_ref,
             g0_ref, act_ref, gl_ref, udgl_ref, dx_ref, dxb_ref, l_ref, dg_ref, ext0_ref, ext1_ref, halo_ref):
        i = pl.program_id(0)
        first = i == 0
        h = h_ref[...]
        total = None
        for j in range(half):
            cs = slice(j * n, (j + 1) * n)
            ext_ref = ext0_ref if j % 2 == 0 else ext1_ref
            g0 = _dot(h, wup_ref[j], _NN)
            u = _dot(h, wup_ref[half + j], _NN)
            g0_ref[:, cs] = g0.astype(g0_ref.dtype)
            ext_ref[0:8, :] = jnp.where(first, 0.0, halo_ref[:, cs])
            ext_ref[8:8 + t, :] = g0
            halo_ref[:, cs] = g0[t - 8:t, :]
            g = _causal_conv(ext_ref, 8, w_ref.at[:, cs], b_ref.at[:, cs], _FFN_K, t)
            gl, dgl = _gelu_and_grad(g)
            gl_ref[:, cs] = gl.astype(gl_ref.dtype)
            udgl_ref[:, cs] = (u * dgl).astype(udgl_ref.dtype)
            act = (gl * u).astype(act_ref.dtype)
            act_ref[:, cs] = act
            p = _dot(act, wdown_ref[cs, :], _NN)
            total = p if total is None else total + p
        _epi_final(total, (x2_ref, t_ref, g_ref), (dx_ref, dxb_ref, l_ref, dg_ref), first)

    def const(shape):
        return pl.BlockSpec(shape, lambda i: (0,) * len(shape), pipeline_mode=pl.Buffered(1))

    row = pl.BlockSpec((t, d), lambda i: (i, 0))
    vec = pl.BlockSpec((1, d), lambda i: (0, 0))
    return pl.pallas_call(
        body, grid=(s // t,),
        in_specs=[row, const(w_up.shape), const(w_down.shape), const(fcw.shape), const(fcb.shape), row, row, vec],
        out_specs=[pl.BlockSpec((t, f), lambda i: (i, 0))] * 4 + [row, row, vec, vec],
        out_shape=[jax.ShapeDtypeStruct((s, f), _MXU)] * 4
        + [jax.ShapeDtypeStruct((s, d), _F32), jax.ShapeDtypeStruct((s, d), _MXU),
                   jax.ShapeDtypeStruct((1, d), _F32), jax.ShapeDtypeStruct((1, d), _F32)],
        scratch_shapes=[pltpu.VMEM((t + 8, n), _F32), pltpu.VMEM((t + 8, n), _F32), pltpu.VMEM((8, f), _F32)],
        compiler_params=_cparams("arbitrary", vmem=_VMEM_LIMIT_FUSED), name=name,
    )(h3, w_up, w_down, fcw, fcb, x2, target, gfin)


def _ffn_fused_bwd(dx3b, g0, gl, udgl, w_down, w_up, fcw, x2, gnorm, dx3, name, comm=None):
    s, d = x2.shape
    nblk, _, n = w_up.shape
    half = nblk // 2
    f = half * n
    t = _tile(s, _FFN_FUSED_T)
    nt = s // t
    hrows = 16

    def body(dxb_ref, g0_ref, g0h_ref, gl_ref, udgl_ref, wdown_ref, wup_ref, w_ref, x2_ref, g_ref, dx3_ref,
             dgu_ref, dx2_ref, dx2b_ref, dgn_ref, dw_ref, db_ref, ext0_ref, ext1_ref, up0_ref, up1_ref, car_ref):
        i = pl.program_id(0)
        first_tile = i == nt - 1
        last_tile = i == 0

        @pl.when(last_tile)
        def _():
            dw_ref[...] = jnp.zeros_like(dw_ref)
            db_ref[...] = jnp.zeros_like(db_ref)
            car_ref[...] = jnp.zeros_like(car_ref)

        dxb = dxb_ref[...]
        total = None
        for j in range(half):
            cs = slice(j * n, (j + 1) * n)
            us = slice(f + j * n, f + (j + 1) * n)
            ext_ref = ext0_ref if j % 2 == 0 else ext1_ref
            up_ref = up0_ref if j % 2 == 0 else up1_ref
            dact = _dot(dxb, wdown_ref[cs, :], _NT)
            ext_ref[0:8, :] = jnp.where(first_tile, 0.0, g0h_ref[:, cs].astype(_F32)[hrows - 8:hrows])
            ext_ref[8:8 + t, :] = g0_ref[:, cs].astype(_F32)
            du = (dact * gl_ref[:, cs].astype(_F32)).astype(dgu_ref.dtype)
            dgu_ref[:, us] = du
            dg = dact * udgl_ref[:, cs].astype(_F32)
            db_ref[:, cs] += _colsum(dg)
            for k in range(_FFN_K):
                dw_ref[k:k + 1, cs] += _colsum(dg * ext_ref[pl.ds(8 - (_FFN_K - 1) + k, t), :])
            up_ref[0:t, :] = dg
            up_ref[t:t + 8, :] = car_ref[:, cs]
            car_ref[:, cs] = dg[0:8, :]
            dg0 = w_ref[0:1, cs] * up_ref[pl.ds(_FFN_K - 1, t), :]
            for k in range(1, _FFN_K):
                dg0 = dg0 + w_ref[k:k + 1, cs] * up_ref[pl.ds(_FFN_K - 1 - k, t), :]
            dg0 = dg0.astype(dgu_ref.dtype)
            dgu_ref[:, cs] = dg0
            p = _dot(dg0, wup_ref[j], _NT) + _dot(du, wup_ref[half + j], _NT)
            total = p if total is None else total + p
        _epi_rms_bwd(total, (x2_ref, g_ref, dx3_ref), (dx2_ref, dx2b_ref, dgn_ref), last_tile)

    def const(shape):
        return pl.BlockSpec(shape, lambda i: (0,) * len(shape), pipeline_mode=pl.Buffered(1))

    row = pl.BlockSpec((t, d), lambda i: (nt - 1 - i, 0))
    vec = pl.BlockSpec((1, d), lambda i: (0, 0))
    per = t // hrows
    wide = pl.BlockSpec((t, f), lambda i: (nt - 1 - i, 0))
    return _pcall(
        body, args=(dx3b, g0, g0, gl, udgl, w_down, w_up, fcw, x2, gnorm, dx3), grid=(nt,),
        in_specs=[row, wide, pl.BlockSpec((hrows, f), lambda i: (jnp.maximum((nt - 1 - i) * per - 1, 0), 0)),
                  wide, wide, const(w_down.shape), const(w_up.shape), const(fcw.shape), row, vec, row],
        out_specs=[pl.BlockSpec((t, 2 * f), lambda i: (nt - 1 - i, 0)), row, row, vec,
                   pl.BlockSpec((_FFN_K, f), lambda i: (0, 0)), pl.BlockSpec((1, f), lambda i: (0, 0))],
        out_shape=[jax.ShapeDtypeStruct((s, 2 * f), _MXU), jax.ShapeDtypeStruct((s, d), _F32),
                   jax.ShapeDtypeStruct((s, d), _MXU), jax.ShapeDtypeStruct((1, d), _F32),
                   jax.ShapeDtypeStruct((_FFN_K, f), _F32), jax.ShapeDtypeStruct((1, f), _F32)],
        scratch_shapes=[pltpu.VMEM((t + 8, n), _F32), pltpu.VMEM((t + 8, n), _F32),
                        pltpu.VMEM((t + 8, n), _F32), pltpu.VMEM((t + 8, n), _F32), pltpu.VMEM((8, f), _F32)],
        sem=("arbitrary",), name=name, comm=comm, vmem=_VMEM_LIMIT_FUSED)


def _mesh_pos():
    return lax.axis_index("x"), lax.axis_index("y"), lax.axis_index("c")


def _flip(v, bit):
    return 1 - v if bit else v


def _sem_scratch(n):
    return [pltpu.SemaphoreType.DMA((7 * n,)), pltpu.SemaphoreType.DMA((7 * n,)), pltpu.SemaphoreType.DMA((n,))]


class _Gather:
    def __init__(self, xs):
        self.ins = list(xs)
        self.outs = [jax.ShapeDtypeStruct((_NDEV,) + v.shape, v.dtype) for v in xs]
        self.scratch = _sem_scratch(len(xs))

    def _plan(self, x_refs, out_refs, sems):
        send_sems, recv_sems, local_sems = sems
        x, y, c = _mesh_pos()
        me, sibling = (x, y, c), (x, y, 1 - c)
        chips = [(1 - x, y), (x, 1 - y), (1 - x, 1 - y)]

        def copy(a, k, block, to, src=None):
            slot = out_refs[a].at[4 * block[0] + 2 * block[1] + block[2]]
            return pltpu.make_async_remote_copy(
                src_ref=slot if src is None else src, dst_ref=slot,
                send_sem=send_sems.at[a * 7 + k], recv_sem=recv_sems.at[a * 7 + k],
                device_id=to, device_id_type=_MESH_ID)

        def own(a):
            return pltpu.make_async_copy(x_refs[a], out_refs[a].at[4 * x + 2 * y + c], local_sems.at[a])

        def first(a):
            return [copy(a, 0, me, sibling, src=x_refs[a])] + [
                copy(a, 1 + j, me, (*chip, c), src=x_refs[a]) for j, chip in enumerate(chips)]

        return me, sibling, chips, c, copy, own, first

    def start(self, x_refs, out_refs, sems):
        _, _, _, _, _, own, first = self._plan(x_refs, out_refs, sems)
        for a in range(len(self.ins)):
            own(a).start()
            for cp in first(a):
                cp.start()

    def finish(self, x_refs, out_refs, sems):
        me, sibling, chips, c, copy, own, first = self._plan(x_refs, out_refs, sems)
        n = len(self.ins)
        passed = []
        for a in range(n):
            for j, chip in enumerate(chips):
                copy(a, 1 + j, (*chip, c), me).wait_recv()
                fwd = copy(a, 4 + j, (*chip, c), sibling)
                fwd.start()
                passed.append(fwd)
        for a in range(n):
            copy(a, 0, sibling, me).wait_recv()
            for j, chip in enumerate(chips):
                copy(a, 4 + j, (*chip, 1 - c), me).wait_recv()
        for a in range(n):
            for cp in first(a):
                cp.wait_send()
        for cp in passed:
            cp.wait_send()
        for a in range(n):
            own(a).wait()


class _Exchange:
    def __init__(self, gs):
        self.ins = list(gs)
        self.outs = [jax.ShapeDtypeStruct(v.shape, v.dtype) for v in gs]
        self.scratch = _sem_scratch(len(gs))

    def _plan(self, g_refs, r_refs, sems):
        send_sems, recv_sems, local_sems = sems
        x, y, c = _mesh_pos()
        me_idx = 4 * x + 2 * y + c
        n = len(self.ins)

        def copy(a, k):
            peer = (_flip(x, k & 4), _flip(y, k & 2), _flip(c, k & 1))
            peer_idx = 4 * peer[0] + 2 * peer[1] + peer[2]
            return pltpu.make_async_remote_copy(
                src_ref=g_refs[a].at[peer_idx], dst_ref=r_refs[a].at[me_idx],
                send_sem=send_sems.at[a * 7 + k - 1], recv_sem=recv_sems.at[a * 7 + k - 1],
                device_id=peer, device_id_type=_MESH_ID)

        copies = [copy(a, k) for a in range(n) for k in range(1, _NDEV)]
        mine = [pltpu.make_async_copy(g_refs[a].at[me_idx], r_refs[a].at[me_idx], local_sems.at[a])
                for a in range(n)]
        return copies, mine

    def start(self, g_refs, r_refs, sems):
        copies, mine = self._plan(g_refs, r_refs, sems)
        for cp in copies + mine:
            cp.start()

    def finish(self, g_refs, r_refs, sems):
        copies, mine = self._plan(g_refs, r_refs, sems)
        for cp in copies:
            cp.wait_recv()
        for cp in copies:
            cp.wait_send()
        for cp in mine:
            cp.wait()


class _Both:
    def __init__(self, first, second):
        self.parts = (first, second)
        self.ins = first.ins + second.ins
        self.outs = first.outs + second.outs
        self.scratch = first.scratch + second.scratch

    def _split(self, ins, outs, sems):
        a, b = self.parts
        na, nb = len(a.ins), len(a.scratch)
        return (a, ins[:na], outs[:na], sems[:nb]), (b, ins[na:], outs[na:], sems[nb:])

    def start(self, ins, outs, sems):
        for part, i, o, s in self._split(ins, outs, sems):
            part.start(i, o, s)

    def finish(self, ins, outs, sems):
        for part, i, o, s in self._split(ins, outs, sems):
            part.finish(i, o, s)


def _comm_call(comm, name):
    def body(*refs):
        n_i, n_o = len(comm.ins), len(comm.outs)
        ins, outs, sems = refs[:n_i], refs[n_i:n_i + n_o], refs[n_i + n_o:]
        comm.start(ins, outs, sems)
        comm.finish(ins, outs, sems)

    return pl.pallas_call(
        body, out_shape=list(comm.outs), in_specs=[_ANY] * len(comm.ins), out_specs=[_ANY] * len(comm.outs),
        scratch_shapes=list(comm.scratch), name=name)(*comm.ins)


def _adamw_math(w, g, m, v):
    m = _ADAM_B1 * m + (1.0 - _ADAM_B1) * g
    v = _ADAM_B2 * v + (1.0 - _ADAM_B2) * (g * g)
    m_hat = m / (1.0 - _ADAM_B1 ** _ADAM_STEP)
    v_hat = v / (1.0 - _ADAM_B2 ** _ADAM_STEP)
    delta = -_ADAM_LR * (m_hat / (jnp.sqrt(v_hat) + _ADAM_EPS) + _ADAM_WD * w)
    return delta, m, v


def _sum_adamw(parts, w, m, v, name):
    r, c = w.shape
    tr = _tile(r, 128)

    def body(p_ref, w_ref, m_ref, v_ref, g_ref, d_ref, nm_ref, nv_ref):
        g = p_ref[0].astype(_F32)
        for j in range(1, _NDEV):
            g = g + p_ref[j].astype(_F32)
        delta, nm, nv = _adamw_math(w_ref[...], g, m_ref[...], v_ref[...])
        g_ref[...] = g
        d_ref[...] = delta
        nm_ref[...] = nm
        nv_ref[...] = nv

    blk = pl.BlockSpec((tr, c), lambda i: (i, 0))
    return pl.pallas_call(
        body, grid=(r // tr,),
        in_specs=[pl.BlockSpec((_NDEV, tr, c), lambda i: (0, i, 0)), blk, blk, blk],
        out_specs=[blk] * 4, out_shape=[jax.ShapeDtypeStruct((r, c), _F32)] * 4,
        compiler_params=_cparams("parallel"), name=name,
    )(parts, w, m, v)


def _sum8(parts, name):
    _, r, c = parts.shape

    def body(p_ref, o_ref):
        g = p_ref[0]
        for j in range(1, _NDEV):
            g = g + p_ref[j]
        o_ref[...] = g

    return pl.pallas_call(
        body, grid=(1,), in_specs=[pl.BlockSpec((_NDEV, r, c), lambda i: (0, 0, 0))],
        out_specs=pl.BlockSpec((r, c), lambda i: (0, 0)), out_shape=jax.ShapeDtypeStruct((r, c), _F32),
        compiler_params=_cparams("arbitrary"), name=name,
    )(parts)


def _adamw_flat(g, w, m, v, name):
    r, c = w.shape

    def body(g_ref, w_ref, m_ref, v_ref, d_ref, nm_ref, nv_ref):
        delta, nm, nv = _adamw_math(w_ref[...], g_ref[...], m_ref[...], v_ref[...])
        d_ref[...] = delta
        nm_ref[...] = nm
        nv_ref[...] = nv

    blk = pl.BlockSpec((r, c), lambda i: (0, 0))
    return pl.pallas_call(
        body, grid=(1,), in_specs=[blk] * 4, out_specs=[blk] * 3,
        out_shape=[jax.ShapeDtypeStruct((r, c), _F32)] * 3,
        compiler_params=_cparams("arbitrary"), name=name,
    )(g, w, m, v)


def _pack(arrs):
    flat = jnp.concatenate([a.reshape(-1).astype(_F32) for a in arrs])
    pad = (-flat.shape[0]) % 1024
    return jnp.pad(flat, (0, pad)).reshape(-1, 128)


def _unpack(flat2d, shapes):
    flat = flat2d.reshape(-1)
    out, off = [], 0
    for sh in shapes:
        size = 1
        for dim in sh:
            size *= dim
        out.append(flat[off:off + size].reshape(sh))
        off += size
    return out


def _block_diag(w):
    h, hd, _ = w.shape
    eye = jnp.eye(h, dtype=w.dtype)
    return (eye[:, None, :, None] * w[:, :, None, :]).reshape(h * hd, h * hd)


def _diag_blocks(full, h):
    hd = full.shape[0] // h
    return jnp.stack([full[i * hd:(i + 1) * hd, i * hd:(i + 1) * hd] for i in range(h)])


def kernel(x, mem, mix_norm_g, w_in, lru_conv_w, lru_conv_b, lru_w_a, lru_b_a, lru_w_x, lru_b_x, lru_lambda, conf_conv_w, conf_conv_b, conf_ln_g, conf_ln_b, w_out, xa_norm_g, mem_norm_g, w_q, w_kv, w_o, ffn_norm_g, w_up, ffn_conv_w, ffn_conv_b, w_down, final_norm_g, loss_target, m_mix_norm_g, m_w_in, m_lru_conv_w, m_lru_conv_b, m_lru_w_a, m_lru_b_a, m_lru_w_x, m_lru_b_x, m_lru_lambda, m_conf_conv_w, m_conf_conv_b, m_conf_ln_g, m_conf_ln_b, m_w_out, m_xa_norm_g, m_mem_norm_g, m_w_q, m_w_kv, m_w_o, m_ffn_norm_g, m_w_up, m_ffn_conv_w, m_ffn_conv_b, m_w_down, m_final_norm_g, v_mix_norm_g, v_w_in, v_lru_conv_w, v_lru_conv_b, v_lru_w_a, v_lru_b_a, v_lru_w_x, v_lru_b_x, v_lru_lambda, v_conf_conv_w, v_conf_conv_b, v_conf_ln_g, v_conf_ln_b, v_w_out, v_xa_norm_g, v_mem_norm_g, v_w_q, v_w_kv, v_w_o, v_ffn_norm_g, v_w_up, v_ffn_conv_w, v_ffn_conv_b, v_w_down, v_final_norm_g):
    names = ["mix_norm_g", "w_in", "lru_conv_w", "lru_conv_b", "lru_w_a", "lru_b_a", "lru_w_x", "lru_b_x",
             "lru_lambda", "conf_conv_w", "conf_conv_b", "conf_ln_g", "conf_ln_b", "w_out", "xa_norm_g",
             "mem_norm_g", "w_q", "w_kv", "w_o", "ffn_norm_g", "w_up", "ffn_conv_w", "ffn_conv_b", "w_down",
             "final_norm_g"]
    loc = locals()
    W = {n: loc[n] for n in names}
    M = {n: loc["m_" + n] for n in names}
    V = {n: loc["v_" + n] for n in names}
    big = ["w_in", "w_out", "w_q", "w_kv", "w_o", "w_up", "w_down"]
    conv_sharded = ["lru_conv_w", "conf_conv_w", "ffn_conv_w"]

    xs = x[0]
    mems = mem[0]
    tgt = loss_target[0]
    me = 4 * lax.axis_index("x") + 2 * lax.axis_index("y") + lax.axis_index("c")

    conv_shapes = [W[n].shape[1:] for n in conv_sharded]
    conv_pack = _pack([W[n][0] for n in conv_sharded])
    shard = {n: W[n][0].astype(_XFER) for n in big}
    h1, (g_in, g_out, g_conv) = _rms_fwd(
        xs, mix_norm_g, "rms1_fwd", comm=_Gather([shard["w_in"], shard["w_out"], conv_pack]))
    convs = [[] for _ in conv_sharded]
    for j in range(_NDEV):
        for idx, part in enumerate(_unpack(g_conv[j], conv_shapes)):
            convs[idx].append(part)
    lcw, ccw, fcw = [jnp.concatenate(parts, axis=-1) for parts in convs]

    wab = jnp.concatenate([_block_diag(lru_w_a[0]), _block_diag(lru_w_x[0])], axis=1).astype(_MXU)
    mixer_params = (lcw, lru_conv_b, wab, lru_b_a, lru_b_x, lru_lambda, ccw, conf_conv_b, conf_ln_g, conf_ln_b)

    w_out_f = g_out.reshape(-1, g_out.shape[-1])
    (z, ycat, hs, cc, x1, h2), (g_q, g_kv, g_o, g_up, g_down) = _mixer_fwd(
        xs, h1, g_in, w_out_f, xa_norm_g, *mixer_params, "mixer_fwd",
        comm=_Gather([shard[n] for n in ("w_q", "w_kv", "w_o", "w_up", "w_down")]))
    w_q_f = g_q.reshape(-1, g_q.shape[-1])
    w_o_f = g_o.reshape(-1, g_o.shape[-1])
    w_down_f = g_down.reshape(-1, g_down.shape[-1])
    row32, row16, vec32 = (_F32, "row"), (_MXU, "row"), (_F32, "vec")
    mn = _rms_fwd(mems, mem_norm_g, "rmsm_fwd")
    kv = _mm_nn_stacked(mn, g_kv, _MXU, "mm_kv_fwd")
    q, o, x2, h3 = _attn_fwd(h2, x1, w_q_f, kv, w_o_f, ffn_norm_g, "attn_fwd")

    gfin = final_norm_g.reshape(1, -1)
    g0, act, gelu_g, u_dgelu, dx3, dx3b, lvec, dg_final = _ffn_fused_fwd(
        h3, g_up, w_down_f, fcw, ffn_conv_b, x2, tgt, gfin, "ffn_fwd")

    def rows8(p):
        return p.reshape(_NDEV, p.shape[0] // _NDEV, p.shape[1])

    p_down = _mm_tn_nat(act, dx3b, _XFER, "mm_down_wgrad", ts=2048)
    (dgu, dx2, dx2b, dg_ffn, dfcw, dfcb), (r_down,) = _ffn_fused_bwd(
        dx3b, g0, gelu_g, u_dgelu, w_down_f, g_up, fcw, x2, ffn_norm_g, dx3, "ffn_bwd",
        comm=_Exchange([rows8(p_down)]))
    p_up = _mm_tn_stacked(h3, dgu, _NDEV, _XFER, "mm_up_wgrad", slabs=2)

    dx1, dx1b, dg_xa, dk, dv, p_o, p_q = _attn_bwd(
        dx2b, dx2, q, o, h2, x1, kv, w_o_f, w_q_f, xa_norm_g, "attn_bwd")
    dkv = jnp.concatenate([dk, dv], axis=1).astype(_MXU)
    dmn = _mm_nt_stacked(dkv, g_kv, "mm_kv_dgrad", outs=[row32], slabs=_NDEV)
    p_kv = _mm_tn_stacked(mn, dkv, _NDEV, _XFER, "mm_kv_wgrad", slabs=_NDEV)
    _, _, dg_mem = _rms_bwd(dmn, mems, mem_norm_g, None, "rmsm_bwd")

    p_out = _mm_tn_nat(ycat, dx1b, _XFER, "mm_out_wgrad", ts=2048)
    ((dz, p_in, dlcw, dlcb, dwab, dba, dbx, dlam, dccw, dccb, dlng, dlnb),
     (r_up, r_o, r_q, r_kv, r_out)) = _mixer_bwd(
        dx1b, w_out_f, h1, z, hs, cc, *mixer_params, "mixer_bwd",
        comm=_Exchange([p_up, rows8(p_o), rows8(p_q), p_kv, rows8(p_out)]))

    c = _D_LRU
    heads = lru_w_a.shape[1]
    small_partial = {
        "lru_conv_w": dlcw, "lru_conv_b": dlcb,
        "lru_w_a": _diag_blocks(dwab[:, :c], heads), "lru_b_a": dba,
        "lru_w_x": _diag_blocks(dwab[:, c:], heads), "lru_b_x": dbx, "lru_lambda": dlam,
        "conf_conv_w": dccw, "conf_conv_b": dccb, "conf_ln_g": dlng, "conf_ln_b": dlnb,
        "xa_norm_g": dg_xa, "mem_norm_g": dg_mem, "ffn_norm_g": dg_ffn,
        "ffn_conv_w": dfcw, "ffn_conv_b": dfcb, "final_norm_g": dg_final,
    }
    early = list(small_partial)
    early_shapes = [small_partial[n].shape for n in early] + [lvec.shape]
    (grad_x, dg_mix), (r_in, early_all) = _mm_nt_stacked(
        dz, g_in, "mm_in_dgrad", epi=_epi_rms_bwd, extra=[(xs, "row"), (mix_norm_g, "vec"), (dx1, "row")],
        outs=[row32, vec32], tm=512, slabs=_NDEV,
        comm=_Both(_Exchange([p_in]), _Gather([_pack([small_partial[n] for n in early] + [lvec])])))
    (mix_all,) = _comm_call(_Gather([dg_mix]), "gather_mix_grad")
    small = early + ["mix_norm_g"]
    small_sum = _unpack(_sum8(early_all, "sum_small_grads"), early_shapes)
    loss = 0.5 * jnp.sum(small_sum.pop()) / xs.shape[1]
    small_sum.append(_sum8(mix_all.reshape(_NDEV, 8, -1), "sum_mix_grad").reshape(dg_mix.shape))
    received = {"w_in": r_in, "w_out": r_out, "w_q": r_q, "w_kv": r_kv, "w_o": r_o, "w_up": r_up,
                "w_down": r_down}

    grads, deltas, new_m, new_v = {}, {}, {}, {}
    for n, rec in ((n, received[n]) for n in big):
        shp = W[n].shape
        w2, m2, v2 = (t.reshape(rec.shape[1:]) for t in (W[n], M[n], V[n]))
        outs = _sum_adamw(rec, w2, m2, v2, "adamw_" + n)
        grads[n], deltas[n], new_m[n], new_v[n] = (t.reshape(shp) for t in outs)

    small_g = []
    for n, g in zip(small, small_sum):
        if n in conv_sharded:
            width = W[n].shape[-1]
            g = lax.dynamic_slice_in_dim(g, me * width, width, axis=1)
        small_g.append(g.reshape(W[n].shape))
    small_shapes = [W[n].shape for n in small]
    sd, sm, sv = _adamw_flat(_pack(small_g), _pack([W[n] for n in small]), _pack([M[n] for n in small]),
                             _pack([V[n] for n in small]), "adamw_small")
    for n, g, d_, m_, v_ in zip(small, small_g, _unpack(sd, small_shapes), _unpack(sm, small_shapes),
                                _unpack(sv, small_shapes)):
        grads[n], deltas[n], new_m[n], new_v[n] = g, d_, m_, v_

    return (loss, grad_x[None], *[grads[n] for n in names], *[deltas[n] for n in names],
            *[new_m[n] for n in names], *[new_v[n] for n in names])
```

```python
import functools

import jax
import jax.numpy as jnp
from jax import lax
from jax.experimental import pallas as pl
from jax.experimental.pallas import tpu as pltpu

_MXU = jnp.bfloat16
_XFER = jnp.bfloat16
_F32 = jnp.float32
_EPS = 1e-6
_NDEV = 8
_VMEM_LIMIT = 48 * 1024 * 1024

_D_LRU = 512
_XA_HEADS = 4
_RG_C = 8.0
_ADAM_LR, _ADAM_B1, _ADAM_B2, _ADAM_EPS, _ADAM_WD, _ADAM_STEP = 0.001, 0.9, 0.999, 1e-08, 0.01, 10

_MESH_ID = pl.DeviceIdType.MESH
_ANY = pl.BlockSpec(memory_space=pl.ANY)


def _cparams(*sem, vmem=_VMEM_LIMIT):
    return pltpu.CompilerParams(dimension_semantics=tuple(sem), vmem_limit_bytes=vmem)


def _pcall(body, *, args, grid, in_specs, out_specs, out_shape, sem, name, scratch_shapes=(), comm=None,
           vmem=_VMEM_LIMIT):
    outs_l = list(out_shape) if isinstance(out_shape, (list, tuple)) else [out_shape]
    ospecs_l = list(out_specs) if isinstance(out_specs, (list, tuple)) else [out_specs]
    n_in, n_out, n_scr = len(args), len(outs_l), len(scratch_shapes)
    if comm is None:
        res = pl.pallas_call(
            body, grid=grid, in_specs=list(in_specs), out_specs=ospecs_l, out_shape=outs_l,
            scratch_shapes=list(scratch_shapes), compiler_params=_cparams(*sem, vmem=vmem), name=name)(*args)
        return list(res), []
    n_ci, n_co = len(comm.ins), len(comm.outs)

    def wrapped(*refs):
        ins, cins = refs[:n_in], refs[n_in:n_in + n_ci]
        o = n_in + n_ci
        outs, couts = refs[o:o + n_out], refs[o + n_out:o + n_out + n_co]
        s = o + n_out + n_co
        scr, cscr = refs[s:s + n_scr], refs[s + n_scr:]
        first = pl.program_id(0) == 0
        last = pl.program_id(0) == grid[0] - 1
        for ax in range(1, len(grid)):
            first = jnp.logical_and(first, pl.program_id(ax) == 0)
            last = jnp.logical_and(last, pl.program_id(ax) == grid[ax] - 1)

        @pl.when(first)
        def _():
            comm.start(cins, couts, cscr)

        body(*ins, *outs, *scr)

        @pl.when(last)
        def _():
            comm.finish(cins, couts, cscr)

    res = pl.pallas_call(
        wrapped, grid=grid, in_specs=list(in_specs) + [_ANY] * n_ci, out_specs=ospecs_l + [_ANY] * n_co,
        out_shape=outs_l + list(comm.outs), scratch_shapes=list(scratch_shapes) + list(comm.scratch),
        compiler_params=_cparams(*(("arbitrary",) * len(grid)), vmem=vmem), name=name)(*args, *comm.ins)
    return list(res[:n_out]), list(res[n_out:])


def _sigmoid(v):
    return 1.0 / (1.0 + jnp.exp(-v))


_GELU_C = 0.7978845608028654
_GELU_K = 0.044715


def _gelu(v):
    t = jnp.tanh(_GELU_C * (v + _GELU_K * v * v * v))
    return 0.5 * v * (1.0 + t)


def _gelu_and_grad(v):
    v2 = v * v
    s = 0.5 * jnp.tanh(v * (_GELU_C + (_GELU_C * _GELU_K) * v2)) + 0.5
    g = v * s
    dg = s + (g * (1.0 - s)) * ((2.0 * _GELU_C) + (6.0 * _GELU_C * _GELU_K) * v2)
    return g, dg


def _softplus(v):
    e = jnp.exp(-jnp.abs(v))
    log1p = jnp.where(e < 1e-2, e * (1.0 - e * (0.5 - e * (1.0 / 3.0))), jnp.log(1.0 + e))
    return jnp.maximum(v, 0.0) + log1p


def _neg_expm1(v):
    series = -v * (1.0 + v * (0.5 + v * ((1.0 / 6.0) + v * (1.0 / 24.0))))
    return jnp.where(v > -0.0625, series, 1.0 - jnp.exp(v))


def _dot(a, b, dims):
    return lax.dot_general(a.astype(_MXU), b.astype(_MXU), (dims, ((), ())), preferred_element_type=_F32)


_NN = ((1,), (0,))
_NT = ((1,), (1,))
_TN = ((0,), (0,))


def _scan_fwd(a, b, rows):
    n = a.shape[0]
    d = 1
    while d < n:
        keep = rows >= d
        b = jnp.where(keep, b + a * pltpu.roll(b, d, 0), b)
        a = jnp.where(keep, a * pltpu.roll(a, d, 0), a)
        d *= 2
    return a, b


def _scan_rev(a, b, rows):
    n = a.shape[0]
    d = 1
    while d < n:
        keep = rows < n - d
        b = jnp.where(keep, b + a * pltpu.roll(b, n - d, 0), b)
        a = jnp.where(keep, a * pltpu.roll(a, n - d, 0), a)
        d *= 2
    return a, b


def _colsum(v):
    return jnp.sum(v, axis=0, keepdims=True)


def _mm(a, b, *, dims, grid, a_spec, b_spec, outs, acc_shape, name, extra=(), epi=None, slabs=1, comm=None):
    nred = grid[-1]
    red_axis = len(grid) - 1
    n_ex, n_out = len(extra), len(outs)
    epi = _epi_store if epi is None else epi

    def body(*refs):
        a_ref, b_ref = refs[:2]
        ex, o_refs, acc_ref = refs[2:2 + n_ex], refs[2 + n_ex:2 + n_ex + n_out], refs[-1]
        if slabs == 1:
            p = _dot(a_ref[...], b_ref[...], dims)
        else:
            n = b_ref.shape[-1]
            p = _dot(a_ref[:, 0:n], b_ref[0], dims)
            for jj in range(1, slabs):
                p = p + _dot(a_ref[:, jj * n:(jj + 1) * n], b_ref[jj], dims)

        first_rows = pl.program_id(0) == 0
        if nred == 1:
            epi(p, ex, o_refs, first_rows)
        else:
            k = pl.program_id(red_axis)

            @pl.when(k == 0)
            def _():
                acc_ref[...] = p

            @pl.when(jnp.logical_and(k > 0, k < nred - 1))
            def _():
                acc_ref[...] += p

            @pl.when(k == nred - 1)
            def _():
                epi(acc_ref[...] + p, ex, o_refs, first_rows)

    sem = ("parallel",) * (len(grid) - 1) + ("arbitrary",)
    if any(o[0].shape[0] == 1 for o in outs):
        sem = ("arbitrary",) * len(grid)
    res, cres = _pcall(
        body, args=(a, b) + tuple(e[0] for e in extra), grid=grid,
        in_specs=[a_spec, b_spec] + [e[1] for e in extra],
        out_specs=[o[1] for o in outs], out_shape=[o[0] for o in outs],
        scratch_shapes=[pltpu.VMEM(acc_shape if nred > 1 else (8, 128), _F32)], sem=sem, name=name, comm=comm)
    res = res[0] if n_out == 1 else res
    return res if comm is None else (res, cres)


def _epi_store(total, ex, outs, first_rows):
    outs[0][...] = total.astype(outs[0].dtype)


def _epi_residual_rms(total, ex, outs, first_rows):
    res_ref, g_ref = ex
    xn = total + res_ref[...]
    outs[0][...] = xn
    r = lax.rsqrt(jnp.mean(xn * xn, axis=-1, keepdims=True) + _EPS)
    outs[1][...] = (xn * r * g_ref[...]).astype(outs[1].dtype)


def _epi_rms_bwd(total, ex, outs, first_rows):
    x_ref, g_ref, dres_ref = ex
    dg_ref = outs[-1]
    xv = x_ref[...]
    r = lax.rsqrt(jnp.mean(xv * xv, axis=-1, keepdims=True) + _EPS)
    xhat = xv * r
    dxh = total * g_ref[...]
    dx = dres_ref[...] + r * (dxh - xhat * jnp.mean(dxh * xhat, axis=-1, keepdims=True))
    for o_ref in outs[:-1]:
        o_ref[...] = dx.astype(o_ref.dtype)

    @pl.when(first_rows)
    def _():
        dg_ref[...] = jnp.zeros_like(dg_ref)

    dg_ref[...] += _colsum(total * xhat)


def _epi_final(total, ex, outs, first_rows):
    res_ref, t_ref, g_ref = ex
    dx_ref, dxb_ref, l_ref, dg_ref = outs
    xv = total + res_ref[...]
    gv = g_ref[...]
    d = xv.shape[-1]
    r = lax.rsqrt(jnp.mean(xv * xv, axis=-1, keepdims=True) + _EPS)
    xhat = xv * r
    err = xhat * gv - t_ref[...]
    dy = err * (1.0 / d)
    dxh = dy * gv
    dx = r * (dxh - xhat * jnp.mean(dxh * xhat, axis=-1, keepdims=True))
    dx_ref[...] = dx
    dxb_ref[...] = dx.astype(dxb_ref.dtype)

    @pl.when(first_rows)
    def _():
        l_ref[...] = jnp.zeros_like(l_ref)
        dg_ref[...] = jnp.zeros_like(dg_ref)

    l_ref[...] += _colsum(err * err)
    dg_ref[...] += _colsum(dy * xhat)


def _tile(m, cap):
    t = min(m, cap)
    assert m % t == 0
    return t


def _row_spec(tm, n):
    return pl.BlockSpec((tm, n), lambda i, *_: (i, 0))


def _vec_spec(n):
    return pl.BlockSpec((1, n), lambda *_: (0, 0))


def _row_io(m, n, tm, extra, outs):
    def spec(kind):
        return _row_spec(tm, n) if kind == "row" else _vec_spec(n)

    ex = [(arr, spec(kind)) for arr, kind in extra]
    os_ = [(jax.ShapeDtypeStruct((m, n) if kind == "row" else (1, n), dt), spec(kind)) for dt, kind in outs]
    return ex, os_


def _mm_nn_stacked(a, w, out_dtype, name, comm=None, tm=1024):
    m, k = a.shape
    j, _, n = w.shape
    tm = _tile(m, tm)
    return _mm(a, w, dims=_NN, grid=(m // tm, j, 1),
               a_spec=pl.BlockSpec((tm, k), lambda i, jj, r: (i, 0)),
               b_spec=pl.BlockSpec((None, k, n), lambda i, jj, r: (jj, 0, 0)),
               outs=[(jax.ShapeDtypeStruct((m, j * n), out_dtype), pl.BlockSpec((tm, n), lambda i, jj, r: (i, jj)))],
               acc_shape=(tm, n), name=name, comm=comm)


def _mm_nt_stacked(dc, w, name, *, outs, extra=(), epi=None, comm=None, tm=1024, slabs=1):
    m = dc.shape[0]
    j, k, n = w.shape
    tm = _tile(m, tm)
    assert j % slabs == 0
    ex, os_ = _row_io(m, k, tm, extra, outs)
    wblk = (None, k, n) if slabs == 1 else (slabs, k, n)
    return _mm(dc, w, dims=_NT, grid=(m // tm, j // slabs),
               a_spec=pl.BlockSpec((tm, slabs * n), lambda i, r: (i, r)),
               b_spec=pl.BlockSpec(wblk, lambda i, r: (r, 0, 0)),
               outs=os_, extra=ex, epi=epi, acc_shape=(tm, k), name=name, slabs=slabs, comm=comm)


def _mm_tn_stacked(a, dc, j, out_dtype, name, slabs=1, ts=1024):
    s, k = a.shape
    n = dc.shape[1] // j
    ts = _tile(s, ts)
    assert j % slabs == 0

    def epi(total, ex, outs, first_rows):
        for jj in range(slabs):
            outs[0][jj] = total[:, jj * n:(jj + 1) * n].astype(outs[0].dtype)

    return _mm(a, dc, dims=_TN, grid=(j // slabs, s // ts),
               a_spec=pl.BlockSpec((ts, k), lambda jj, r: (r, 0)),
               b_spec=pl.BlockSpec((ts, slabs * n), lambda jj, r: (r, jj)),
               outs=[(jax.ShapeDtypeStruct((j, k, n), out_dtype),
                      pl.BlockSpec((slabs, k, n), lambda jj, r: (jj, 0, 0)))],
               epi=epi, acc_shape=(k, slabs * n), name=name)


def _mm_tn_nat(a, dc, out_dtype, name, ts=1024):
    s, kt = a.shape
    n = dc.shape[1]
    ts = _tile(s, ts)
    tkb = _tile(kt, 512)
    return _mm(a, dc, dims=_TN, grid=(kt // tkb, s // ts),
               a_spec=pl.BlockSpec((ts, tkb), lambda kb, r: (r, kb)),
               b_spec=pl.BlockSpec((ts, n), lambda kb, r: (r, 0)),
               outs=[(jax.ShapeDtypeStruct((kt, n), out_dtype), pl.BlockSpec((tkb, n), lambda kb, r: (kb, 0)))],
               acc_shape=(tkb, n), name=name)


def _rms_fwd(x, g, name, comm=None):
    s, d = x.shape
    t = _tile(s, 1024)

    def body(x_ref, g_ref, h_ref):
        xv = x_ref[...]
        r = lax.rsqrt(jnp.mean(xv * xv, axis=-1, keepdims=True) + _EPS)
        h_ref[...] = (xv * r * g_ref[...]).astype(h_ref.dtype)

    res, cres = _pcall(
        body, args=(x, g), grid=(s // t,),
        in_specs=[pl.BlockSpec((t, d), lambda i: (i, 0)), pl.BlockSpec((1, d), lambda i: (0, 0))],
        out_specs=pl.BlockSpec((t, d), lambda i: (i, 0)),
        out_shape=jax.ShapeDtypeStruct((s, d), _MXU), sem=("parallel",), name=name, comm=comm)
    return res[0] if comm is None else (res[0], cres)


def _rms_bwd(dh, x, g, dres, name):
    s, d = x.shape
    t = _tile(s, 256)
    has_res = dres is not None

    def body(*refs):
        if has_res:
            dh_ref, x_ref, g_ref, dres_ref, dx_ref, dxb_ref, dg_ref = refs
        else:
            dh_ref, x_ref, g_ref, dx_ref, dxb_ref, dg_ref = refs
        xv = x_ref[...]
        dhv = dh_ref[...]
        r = lax.rsqrt(jnp.mean(xv * xv, axis=-1, keepdims=True) + _EPS)
        xhat = xv * r
        dxh = dhv * g_ref[...]
        dx = r * (dxh - xhat * jnp.mean(dxh * xhat, axis=-1, keepdims=True))
        if has_res:
            dx = dx + dres_ref[...]
        dx_ref[...] = dx
        dxb_ref[...] = dx.astype(dxb_ref.dtype)

        @pl.when(pl.program_id(0) == 0)
        def _():
            dg_ref[...] = jnp.zeros_like(dg_ref)

        dg_ref[...] += _colsum(dhv * xhat)

    row = pl.BlockSpec((t, d), lambda i: (i, 0))
    vec = pl.BlockSpec((1, d), lambda i: (0, 0))
    in_specs = [row, row, vec] + ([row] if has_res else [])
    args = (dh, x, g) + ((dres,) if has_res else ())
    return pl.pallas_call(
        body, grid=(s // t,), in_specs=in_specs, out_specs=[row, row, vec],
        out_shape=[jax.ShapeDtypeStruct((s, d), _F32), jax.ShapeDtypeStruct((s, d), _MXU),
                   jax.ShapeDtypeStruct((1, d), _F32)],
        compiler_params=_cparams("arbitrary"), name=name,
    )(*args)


_LRU_K = 4
_CONF_K = 31
_LRU_HALO = 8
_CONF_HALO = 32
_MIX_T = 512


def _lru_gates(lx, wab_ref, ba_ref, bx_ref, lam_ref):
    c = _D_LRU
    pre = _dot(lx, wab_ref[...], _NN)
    r = _sigmoid(pre[:, :c] + ba_ref[...])
    ig = _sigmoid(pre[:, c:] + bx_ref[...])
    sp = _softplus(-lam_ref[...])
    log_a = (-_RG_C) * r * sp
    a = jnp.exp(log_a)
    mult = jnp.sqrt(_neg_expm1(2.0 * log_a))
    return r, ig, sp, a, mult


def _causal_conv(ext_ref, halo, w_ref, b_ref, taps, t):
    acc = b_ref[...] + w_ref[0:1, :] * ext_ref[pl.ds(halo - (taps - 1), t), :]
    for k in range(1, taps):
        acc = acc + w_ref[k:k + 1, :] * ext_ref[pl.ds(halo - (taps - 1) + k, t), :]
    return acc


class _Windows:
    def __init__(self, ext_ref, shifted_ref, t):
        self.ext_ref, self.shifted_ref, self.t = ext_ref, shifted_ref, t
        rows = t + 24
        for r in range(1, 8):
            shifted_ref[r - 1, 0:rows, :] = ext_ref[pl.ds(r, rows), :]

    def __call__(self, off):
        q, r = divmod(off, 8)
        if r == 0:
            return self.ext_ref[pl.ds(8 * q, self.t), :]
        return self.shifted_ref[r - 1, pl.ds(8 * q, self.t), :]


def _mixer_fwd(xs, h1, w_in, w_out, gnorm, lcw, lcb, wab, ba, bx, lam, ccw, ccb, lng, lnb, name, comm=None):
    s, d = xs.shape
    nblk, _, n = w_in.shape
    c = _D_LRU
    t = _tile(s, _MIX_T)
    nt = s // t

    def body(x_ref, h_ref, win_ref, wout_ref, gn_ref,
             lcw_ref, lcb_ref, wab_ref, ba_ref, bx_ref, lam_ref, ccw_ref, ccb_ref, lng_ref, lnb_ref,
             z_ref, ycat_ref, hs_ref, cc_ref, x1_ref, h2_ref,
             ext_ref, cge_ref, hc_ref, shifted_ref, zprev_ref):
        i = pl.program_id(0)
        first = i == 0
        rows = lax.broadcasted_iota(jnp.int32, (t, c), 0)

        @pl.when(first)
        def _():
            zprev_ref[...] = jnp.zeros_like(zprev_ref)

        hv = h_ref[...]
        for j in range(nblk):
            z_ref[:, j * n:(j + 1) * n] = _dot(hv, win_ref[j], _NN)
        lx0_ref, gate_ref = z_ref.at[:, 0:c], z_ref.at[:, c:2 * c]
        ca_ref, cb_ref = z_ref.at[:, 2 * c:3 * c], z_ref.at[:, 3 * c:4 * c]
        lx0h_ref = zprev_ref.at[_CONF_HALO - _LRU_HALO:_CONF_HALO, 0:c]
        cah_ref, cbh_ref = zprev_ref.at[:, 2 * c:3 * c], zprev_ref.at[:, 3 * c:4 * c]

        ext_ref[0:_LRU_HALO, :] = jnp.where(first, 0.0, lx0h_ref[...])
        ext_ref[_LRU_HALO:_LRU_HALO + t, :] = lx0_ref[...]
        lx = _causal_conv(ext_ref, _LRU_HALO, lcw_ref, lcb_ref, _LRU_K, t)
        r, ig, sp, a, mult = _lru_gates(lx, wab_ref, ba_ref, bx_ref, lam_ref)
        u = mult * (ig * lx)
        a_cum, h_loc = _scan_fwd(a, u, rows)

        @pl.when(first)
        def _():
            hc_ref[...] = jnp.zeros_like(hc_ref)

        h = h_loc + a_cum * hc_ref[7:8, :]
        hs_ref[...] = h
        hc_ref[...] = hs_ref[pl.ds(t - 8, 8), :]
        ycat_ref[:, 0:c] = (h * _gelu(gate_ref[...])).astype(ycat_ref.dtype)

        cge_ref[0:_CONF_HALO, :] = jnp.where(first, 0.0, cah_ref[...] * _sigmoid(cbh_ref[...]))
        cge_ref[_CONF_HALO:_CONF_HALO + t, :] = ca_ref[...] * _sigmoid(cb_ref[...])
        win = _Windows(cge_ref, shifted_ref, t)
        first_off = _CONF_HALO - (_CONF_K - 1)
        cc = ccb_ref[...] + ccw_ref[0:1, :] * win(first_off)
        for k in range(1, _CONF_K):
            cc = cc + ccw_ref[k:k + 1, :] * win(first_off + k)
        cc_ref[...] = cc
        xc = cc - jnp.mean(cc, axis=-1, keepdims=True)
        rstd = lax.rsqrt(jnp.mean(xc * xc, axis=-1, keepdims=True) + _EPS)
        ln = xc * rstd * lng_ref[...] + lnb_ref[...]
        ycat_ref[:, c:2 * c] = (ln * _sigmoid(ln)).astype(ycat_ref.dtype)

        zprev_ref[...] = z_ref[pl.ds(t - _CONF_HALO, _CONF_HALO), :]
        y = _dot(ycat_ref[...], wout_ref[...], _NN)
        _epi_residual_rms(y, (x_ref, gn_ref), (x1_ref, h2_ref), first)

    def const(arr):
        return pl.BlockSpec(arr.shape, lambda i: (0,) * arr.ndim, pipeline_mode=pl.Buffered(1))

    def rows_of(width):
        return pl.BlockSpec((t, width), lambda i: (i, 0))

    params = (lcw, lcb, wab, ba, bx, lam, ccw, ccb, lng, lnb)
    res, cres = _pcall(
        body, args=(xs, h1, w_in, w_out, gnorm, *params), grid=(nt,),
        in_specs=[rows_of(d), rows_of(d), const(w_in), const(w_out), const(gnorm)] + [const(p) for p in params],
        out_specs=[rows_of(nblk * n), rows_of(2 * c), rows_of(c), rows_of(c), rows_of(d), rows_of(d)],
        out_shape=[jax.ShapeDtypeStruct((s, nblk * n), _F32), jax.ShapeDtypeStruct((s, 2 * c), _MXU),
                   jax.ShapeDtypeStruct((s, c), _F32), jax.ShapeDtypeStruct((s, c), _F32),
                   jax.ShapeDtypeStruct((s, d), _F32), jax.ShapeDtypeStruct((s, d), _MXU)],
        scratch_shapes=[pltpu.VMEM((t + _LRU_HALO, c), _F32), pltpu.VMEM((t + _CONF_HALO, c), _F32),
                        pltpu.VMEM((8, c), _F32), pltpu.VMEM((7, t + _CONF_HALO, c), _F32),
                        pltpu.VMEM((_CONF_HALO, nblk * n), _F32)],
        sem=("arbitrary",), name=name, comm=comm)
    return res, cres


def _mixer_bwd(dx1b, w_out, h1, z, hs, cc, lcw, lcb, wab, ba, bx, lam, ccw, ccb, lng, lnb, name, comm=None):
    s = z.shape[0]
    d = h1.shape[1]
    c = _D_LRU
    t = _tile(s, _MIX_T)
    nt = s // t
    nblk = z.shape[1] // 256

    def body(dxb_ref, wout_ref, h1_ref, lx0_ref, lx0h_ref, gate_ref, ca_ref, cah_ref, cb_ref, cbh_ref,
             hs_ref, hsh_ref, cc_ref,
             lcw_ref, lcb_ref, wab_ref, ba_ref, bx_ref, lam_ref, ccw_ref, ccb_ref, lng_ref, lnb_ref,
             dz_ref, pin_ref, dlcw_ref, dlcb_ref, dwab_ref, dba_ref, dbx_ref, dlam_ref, dccw_ref, dccb_ref, dlng_ref,
             dlnb_ref,
             ext_ref, up_ref, cge_ref, dce_ref, xc_ref, dlxc_ref, dccc_ref, shifted_ref, dwin_ref):
        i = pl.program_id(0)
        first_tile = i == nt - 1
        last_tile = i == 0
        rows = lax.broadcasted_iota(jnp.int32, (t, c), 0)

        @pl.when(last_tile)
        def _():
            for ref in (dlcw_ref, dlcb_ref, dwab_ref, dba_ref, dbx_ref, dlam_ref, dccw_ref, dccb_ref, dlng_ref,
                        dlnb_ref, xc_ref, dlxc_ref, dccc_ref, dwin_ref):
                ref[...] = jnp.zeros_like(ref)

        dycat = _dot(dxb_ref[...], wout_ref[...], _NT)

        ext_ref[0:_LRU_HALO, :] = jnp.where(first_tile, 0.0, lx0h_ref[...])
        ext_ref[_LRU_HALO:_LRU_HALO + t, :] = lx0_ref[...]
        lx = _causal_conv(ext_ref, _LRU_HALO, lcw_ref, lcb_ref, _LRU_K, t)
        r, ig, sp, a, mult = _lru_gates(lx, wab_ref, ba_ref, bx_ref, lam_ref)
        h = hs_ref[...]
        gl, dgl = _gelu_and_grad(gate_ref[...])
        dyl = dycat[:, 0:c]
        dz_ref[:, c:2 * c] = (dyl * h * dgl).astype(dz_ref.dtype)
        dh = dyl * gl

        up_ref[0:t, :] = a
        up_ref[t:t + 8, :] = jnp.ones((8, c), _F32)
        a_up = up_ref[pl.ds(1, t), :]
        a_cum, g_loc = _scan_rev(a_up, dh, rows)
        gt = g_loc + a_cum * xc_ref[0:1, :]
        xc_ref[...] = (a * gt)[0:8, :]

        up_ref[0:8, :] = jnp.where(first_tile, 0.0, hsh_ref[...])
        up_ref[8:8 + t, :] = h
        hprev = up_ref[pl.ds(7, t), :]

        da = gt * hprev
        dmult = gt * ig * lx
        dig = gt * mult * lx
        dlx = gt * mult * ig
        dlog_a = da * a - dmult * a * a / mult
        dpre_r = dlog_a * (-_RG_C) * sp * r * (1.0 - r)
        dpre_i = dig * ig * (1.0 - ig)
        dlam_ref[...] += _colsum(dlog_a * r) * (_RG_C * _sigmoid(-lam_ref[...]))
        dba_ref[...] += _colsum(dpre_r)
        dbx_ref[...] += _colsum(dpre_i)
        dpre = jnp.concatenate([dpre_r, dpre_i], axis=1).astype(_MXU)
        dlx = dlx + _dot(dpre, wab_ref[...], _NT)
        dwab_ref[...] += _dot(lx, dpre, _TN)

        dlcb_ref[...] += _colsum(dlx)
        up_ref[0:t, :] = dlx
        up_ref[t:t + 8, :] = dlxc_ref[...]
        dlxc_ref[...] = dlx[0:8, :]
        acc = lcw_ref[0:1, :] * up_ref[pl.ds(_LRU_K - 1, t), :]
        for k in range(1, _LRU_K):
            acc = acc + lcw_ref[k:k + 1, :] * up_ref[pl.ds(_LRU_K - 1 - k, t), :]
        dz_ref[:, 0:c] = acc.astype(dz_ref.dtype)
        for k in range(_LRU_K):
            dlcw_ref[k:k + 1, :] += _colsum(dlx * ext_ref[pl.ds(_LRU_HALO - (_LRU_K - 1) + k, t), :])

        sig_b = _sigmoid(cb_ref[...])
        ca = ca_ref[...]
        cge_ref[0:_CONF_HALO, :] = jnp.where(first_tile, 0.0, cah_ref[...] * _sigmoid(cbh_ref[...]))
        cge_ref[_CONF_HALO:_CONF_HALO + t, :] = ca * sig_b
        ccv = cc_ref[...]
        xcen = ccv - jnp.mean(ccv, axis=-1, keepdims=True)
        rstd = lax.rsqrt(jnp.mean(xcen * xcen, axis=-1, keepdims=True) + _EPS)
        xn = xcen * rstd
        ln = xn * lng_ref[...] + lnb_ref[...]
        sg = _sigmoid(ln)
        dln = dycat[:, c:2 * c] * (sg * (1.0 + ln * (1.0 - sg)))
        dlng_ref[...] += _colsum(dln * xn)
        dlnb_ref[...] += _colsum(dln)
        dxn = dln * lng_ref[...]
        dcc = rstd * (dxn - jnp.mean(dxn, axis=-1, keepdims=True)
                      - xn * jnp.mean(dxn * xn, axis=-1, keepdims=True))
        dccb_ref[...] += _colsum(dcc)
        win = _Windows(cge_ref, shifted_ref, t)
        for k in range(_CONF_K):
            dccw_ref[k:k + 1, :] += _colsum(dcc * win(_CONF_HALO - (_CONF_K - 1) + k))
        dce_ref[0:t, :] = dcc
        dce_ref[t:t + _CONF_HALO, :] = dccc_ref[...]
        dccc_ref[...] = dcc[0:_CONF_HALO, :]
        win = _Windows(dce_ref, shifted_ref, t)
        dcg = ccw_ref[0:1, :] * win(_CONF_K - 1)
        for k in range(1, _CONF_K):
            dcg = dcg + ccw_ref[k:k + 1, :] * win(_CONF_K - 1 - k)
        dz_ref[:, 2 * c:3 * c] = (dcg * sig_b).astype(dz_ref.dtype)
        dz_ref[:, 3 * c:4 * c] = (dcg * ca * sig_b * (1.0 - sig_b)).astype(dz_ref.dtype)

        dwin_ref[...] += _dot(h1_ref[...], dz_ref[...], _TN)

        @pl.when(first_tile)
        def _():
            for j in range(nblk):
                pin_ref[j] = dwin_ref[:, j * 256:(j + 1) * 256].astype(pin_ref.dtype)

    def col(j):
        return pl.BlockSpec((t, c), lambda i: (nt - 1 - i, j))

    def halo(j, rows_):
        per = t // rows_
        return pl.BlockSpec((rows_, c), lambda i: (jnp.maximum((nt - 1 - i) * per - 1, 0), j))

    def full(shape):
        return pl.BlockSpec(shape, lambda i: (0,) * len(shape))

    params = (lcw, lcb, wab, ba, bx, lam, ccw, ccb, lng, lnb)
    small = [(_LRU_K, c), (1, c), (c, 2 * c), (1, c), (1, c), (1, c), (_CONF_K, c), (1, c), (1, c), (1, c)]
    wide = pl.BlockSpec((t, d), lambda i: (nt - 1 - i, 0))
    pin_shape = (nblk, d, 256)
    return _pcall(
        body, args=(dx1b, w_out, h1, z, z, z, z, z, z, z, hs, hs, cc, *params), grid=(nt,),
        in_specs=[wide, pl.BlockSpec(w_out.shape, lambda i: (0, 0), pipeline_mode=pl.Buffered(1)), wide,
                  col(0), halo(0, _LRU_HALO), col(1), col(2), halo(2, _CONF_HALO), col(3), halo(3, _CONF_HALO),
                  col(0), halo(0, 8), col(0)]
        + [full(p.shape) for p in params],
        out_specs=[pl.BlockSpec((t, 4 * c), lambda i: (nt - 1 - i, 0)), full(pin_shape)] + [full(sh) for sh in small],
        out_shape=[jax.ShapeDtypeStruct((s, 4 * c), _MXU), jax.ShapeDtypeStruct(pin_shape, _XFER)]
        + [jax.ShapeDtypeStruct(sh, _F32) for sh in small],
        scratch_shapes=[pltpu.VMEM((t + _LRU_HALO, c), _F32), pltpu.VMEM((t + 8, c), _F32),
                        pltpu.VMEM((t + _CONF_HALO, c), _F32), pltpu.VMEM((t + _CONF_HALO, c), _F32),
                        pltpu.VMEM((8, c), _F32), pltpu.VMEM((8, c), _F32), pltpu.VMEM((_CONF_HALO, c), _F32),
                        pltpu.VMEM((7, t + _CONF_HALO, c), _F32), pltpu.VMEM((d, 4 * c), _F32)],
        sem=("arbitrary",), name=name, comm=comm, vmem=_VMEM_LIMIT_FUSED)


_ATT_T = 512


def _attn_probs(qh, kh, scale):
    sc = _dot(qh, kh, _NT) * scale
    e = jnp.exp(sc - jnp.max(sc, axis=-1, keepdims=True))
    return e / jnp.sum(e, axis=-1, keepdims=True)


def _const_spec(arr):
    return pl.BlockSpec(arr.shape, lambda i: (0,) * arr.ndim, pipeline_mode=pl.Buffered(1))


def _attn_fwd(h2, x1, w_q, kv, w_o, gnorm, name):
    s, d = h2.shape
    nm = kv.shape[0]
    hd = d // _XA_HEADS
    t = _tile(s, _ATT_T)
    scale = hd ** -0.5

    def body(h_ref, x1_ref, wq_ref, k_ref, v_ref, wo_ref, g_ref, q_ref, o_ref, x2_ref, h3_ref):
        q_ref[...] = _dot(h_ref[...], wq_ref[...], _NN).astype(q_ref.dtype)
        for hh in range(_XA_HEADS):
            sl = slice(hh * hd, (hh + 1) * hd)
            p = _attn_probs(q_ref[:, sl], k_ref[:, sl], scale)
            o_ref[:, sl] = _dot(p, v_ref[:, sl], _NN).astype(o_ref.dtype)
        y = _dot(o_ref[...], wo_ref[...], _NN)
        _epi_residual_rms(y, (x1_ref, g_ref), (x2_ref, h3_ref), None)

    row = pl.BlockSpec((t, d), lambda i: (i, 0))
    half = pl.BlockSpec((nm, d), lambda i: (0, 0), pipeline_mode=pl.Buffered(1))
    half2 = pl.BlockSpec((nm, d), lambda i: (0, 1), pipeline_mode=pl.Buffered(1))
    return pl.pallas_call(
        body, grid=(s // t,),
        in_specs=[row, row, _const_spec(w_q), half, half2, _const_spec(w_o), _const_spec(gnorm)],
        out_specs=[row, row, row, row],
        out_shape=[jax.ShapeDtypeStruct((s, d), _MXU), jax.ShapeDtypeStruct((s, d), _MXU),
                   jax.ShapeDtypeStruct((s, d), _F32), jax.ShapeDtypeStruct((s, d), _MXU)],
        compiler_params=_cparams("parallel"), name=name,
    )(h2, x1, w_q, kv, kv, w_o, gnorm)


def _attn_bwd(dx2b, dx2, q, o, h2, x1, kv, w_o, w_q, gnorm, name, comm=None):
    s, d = q.shape
    nm = kv.shape[0]
    hd = d // _XA_HEADS
    t = _tile(s, _ATT_T)
    nt = s // t
    scale = hd ** -0.5

    def body(dxb_ref, dx2_ref, q_ref, o_ref, h_ref, x1_ref, k_ref, v_ref, wo_ref, wq_ref, g_ref,
             dx1_ref, dx1b_ref, dgn_ref, dk_ref, dv_ref, pwo_ref, pwq_ref, dq_ref, awo_ref, awq_ref):
        i = pl.program_id(0)
        first = i == 0

        @pl.when(first)
        def _():
            for ref in (dk_ref, dv_ref, awo_ref, awq_ref):
                ref[...] = jnp.zeros_like(ref)

        dxb = dxb_ref[...]
        do = _dot(dxb, wo_ref[...], _NT).astype(_MXU)
        awo_ref[...] += _dot(o_ref[...], dxb, _TN)
        for hh in range(_XA_HEADS):
            sl = slice(hh * hd, (hh + 1) * hd)
            qh = q_ref[:, sl]
            kh = k_ref[:, sl]
            doh = do[:, sl]
            p = _attn_probs(qh, kh, scale)
            dp = _dot(doh, v_ref[:, sl], _NT)
            dv_ref[:, sl] += _dot(p, doh, _TN)
            ds = (p * (dp - jnp.sum(dp * p, axis=-1, keepdims=True)) * scale).astype(_MXU)
            dq_ref[:, sl] = _dot(ds, kh, _NN).astype(dq_ref.dtype)
            dk_ref[:, sl] += _dot(ds, qh, _TN)
        dq = dq_ref[...]
        awq_ref[...] += _dot(h_ref[...], dq, _TN)
        dh = _dot(dq, wq_ref[...], _NT)
        _epi_rms_bwd(dh, (x1_ref, g_ref, dx2_ref), (dx1_ref, dx1b_ref, dgn_ref), first)

        @pl.when(i == nt - 1)
        def _():
            pwo_ref[...] = awo_ref[...].astype(pwo_ref.dtype)
            pwq_ref[...] = awq_ref[...].astype(pwq_ref.dtype)

    row = pl.BlockSpec((t, d), lambda i: (i, 0))
    vec = pl.BlockSpec((1, d), lambda i: (0, 0))
    mem_blk = pl.BlockSpec((nm, d), lambda i: (0, 0))
    sq = pl.BlockSpec((d, d), lambda i: (0, 0))
    half = pl.BlockSpec((nm, d), lambda i: (0, 0), pipeline_mode=pl.Buffered(1))
    half2 = pl.BlockSpec((nm, d), lambda i: (0, 1), pipeline_mode=pl.Buffered(1))
    return _pcall(
        body, args=(dx2b, dx2, q, o, h2, x1, kv, kv, w_o, w_q, gnorm), grid=(nt,),
        in_specs=[row, row, row, row, row, row, half, half2, _const_spec(w_o), _const_spec(w_q), _const_spec(gnorm)],
        out_specs=[row, row, vec, mem_blk, mem_blk, sq, sq],
        out_shape=[jax.ShapeDtypeStruct((s, d), _F32), jax.ShapeDtypeStruct((s, d), _MXU),
                   jax.ShapeDtypeStruct((1, d), _F32), jax.ShapeDtypeStruct((nm, d), _F32),
                   jax.ShapeDtypeStruct((nm, d), _F32), jax.ShapeDtypeStruct((d, d), _XFER),
                   jax.ShapeDtypeStruct((d, d), _XFER)],
        scratch_shapes=[pltpu.VMEM((t, d), _MXU), pltpu.VMEM((d, d), _F32), pltpu.VMEM((d, d), _F32)],
        sem=("arbitrary",), name=name, comm=comm, vmem=_VMEM_LIMIT_FUSED)


_FFN_K = 3
_FFN_FUSED_T = 256
_VMEM_LIMIT_FUSED = 58 * 1024 * 1024


def _ffn_fused_fwd(h3, w_up, w_down, fcw, fcb, x2, target, gfin, name):
    s, d = h3.shape
    nblk, _, n = w_up.shape
    half = nblk // 2
    f = half * n
    t = _tile(s, _FFN_FUSED_T)

    def body(h_ref, wup_ref, wdown_ref, w_ref, b_ref, x2_ref, t_ref, g_ref,
             g0_ref, act_ref, gl_ref, udgl_ref, dx_ref, dxb_ref, l_ref, dg_ref, ext0_ref, ext1_ref, halo_ref):
        i = pl.program_id(0)
        first = i == 0
        h = h_ref[...]
        total = None
        for j in range(half):
            cs = slice(j * n, (j + 1) * n)
            ext_ref = ext0_ref if j % 2 == 0 else ext1_ref
            g0 = _dot(h, wup_ref[j], _NN)
            u = _dot(h, wup_ref[half + j], _NN)
            g0_ref[:, cs] = g0.astype(g0_ref.dtype)
            ext_ref[0:8, :] = jnp.where(first, 0.0, halo_ref[:, cs])
            ext_ref[8:8 + t, :] = g0
            halo_ref[:, cs] = g0[t - 8:t, :]
            g = _causal_conv(ext_ref, 8, w_ref.at[:, cs], b_ref.at[:, cs], _FFN_K, t)
            gl, dgl = _gelu_and_grad(g)
            gl_ref[:, cs] = gl.astype(gl_ref.dtype)
            udgl_ref[:, cs] = (u * dgl).astype(udgl_ref.dtype)
            act = (gl * u).astype(act_ref.dtype)
            act_ref[:, cs] = act
            p = _dot(act, wdown_ref[cs, :], _NN)
            total = p if total is None else total + p
        _epi_final(total, (x2_ref, t_ref, g_ref), (dx_ref, dxb_ref, l_ref, dg_ref), first)

    def const(shape):
        return pl.BlockSpec(shape, lambda i: (0,) * len(shape), pipeline_mode=pl.Buffered(1))

    row = pl.BlockSpec((t, d), lambda i: (i, 0))
    vec = pl.BlockSpec((1, d), lambda i: (0, 0))
    return pl.pallas_call(
        body, grid=(s // t,),
        in_specs=[row, const(w_up.shape), const(w_down.shape), const(fcw.shape), const(fcb.shape), row, row, vec],
        out_specs=[pl.BlockSpec((t, f), lambda i: (i, 0))] * 4 + [row, row, vec, vec],
        out_shape=[jax.ShapeDtypeStruct((s, f), _MXU)] * 4
        + [jax.ShapeDtypeStruct((s, d), _F32), jax.ShapeDtypeStruct((s, d), _MXU),
                   jax.ShapeDtypeStruct((1, d), _F32), jax.ShapeDtypeStruct((1, d), _F32)],
        scratch_shapes=[pltpu.VMEM((t + 8, n), _F32), pltpu.VMEM((t + 8, n), _F32), pltpu.VMEM((8, f), _F32)],
        compiler_params=_cparams("arbitrary", vmem=_VMEM_LIMIT_FUSED), name=name,
    )(h3, w_up, w_down, fcw, fcb, x2, target, gfin)


def _ffn_fused_bwd(dx3b, g0, gl, udgl, w_down, w_up, fcw, x2, gnorm, dx3, name, comm=None):
    s, d = x2.shape
    nblk, _, n = w_up.shape
    half = nblk // 2
    f = half * n
    t = _tile(s, _FFN_FUSED_T)
    nt = s // t
    hrows = 16

    def body(dxb_ref, g0_ref, g0h_ref, gl_ref, udgl_ref, wdown_ref, wup_ref, w_ref, x2_ref, g_ref, dx3_ref,
             dgu_ref, dx2_ref, dx2b_ref, dgn_ref, dw_ref, db_ref, ext0_ref, ext1_ref, up0_ref, up1_ref, car_ref):
        i = pl.program_id(0)
        first_tile = i == nt - 1
        last_tile = i == 0

        @pl.when(last_tile)
        def _():
            dw_ref[...] = jnp.zeros_like(dw_ref)
            db_ref[...] = jnp.zeros_like(db_ref)
            car_ref[...] = jnp.zeros_like(car_ref)

        dxb = dxb_ref[...]
        total = None
        for j in range(half):
            cs = slice(j * n, (j + 1) * n)
            us = slice(f + j * n, f + (j + 1) * n)
            ext_ref = ext0_ref if j % 2 == 0 else ext1_ref
            up_ref = up0_ref if j % 2 == 0 else up1_ref
            dact = _dot(dxb, wdown_ref[cs, :], _NT)
            ext_ref[0:8, :] = jnp.where(first_tile, 0.0, g0h_ref[:, cs].astype(_F32)[hrows - 8:hrows])
            ext_ref[8:8 + t, :] = g0_ref[:, cs].astype(_F32)
            du = (dact * gl_ref[:, cs].astype(_F32)).astype(dgu_ref.dtype)
            dgu_ref[:, us] = du
            dg = dact * udgl_ref[:, cs].astype(_F32)
            db_ref[:, cs] += _colsum(dg)
            for k in range(_FFN_K):
                dw_ref[k:k + 1, cs] += _colsum(dg * ext_ref[pl.ds(8 - (_FFN_K - 1) + k, t), :])
            up_ref[0:t, :] = dg
            up_ref[t:t + 8, :] = car_ref[:, cs]
            car_ref[:, cs] = dg[0:8, :]
            dg0 = w_ref[0:1, cs] * up_ref[pl.ds(_FFN_K - 1, t), :]
            for k in range(1, _FFN_K):
                dg0 = dg0 + w_ref[k:k + 1, cs] * up_ref[pl.ds(_FFN_K - 1 - k, t), :]
            dg0 = dg0.astype(dgu_ref.dtype)
            dgu_ref[:, cs] = dg0
            p = _dot(dg0, wup_ref[j], _NT) + _dot(du, wup_ref[half + j], _NT)
            total = p if total is None else total + p
        _epi_rms_bwd(total, (x2_ref, g_ref, dx3_ref), (dx2_ref, dx2b_ref, dgn_ref), last_tile)

    def const(shape):
        return pl.BlockSpec(shape, lambda i: (0,) * len(shape), pipeline_mode=pl.Buffered(1))

    row = pl.BlockSpec((t, d), lambda i: (nt - 1 - i, 0))
    vec = pl.BlockSpec((1, d), lambda i: (0, 0))
    per = t // hrows
    wide = pl.BlockSpec((t, f), lambda i: (nt - 1 - i, 0))
    return _pcall(
        body, args=(dx3b, g0, g0, gl, udgl, w_down, w_up, fcw, x2, gnorm, dx3), grid=(nt,),
        in_specs=[row, wide, pl.BlockSpec((hrows, f), lambda i: (jnp.maximum((nt - 1 - i) * per - 1, 0), 0)),
                  wide, wide, const(w_down.shape), const(w_up.shape), const(fcw.shape), row, vec, row],
        out_specs=[pl.BlockSpec((t, 2 * f), lambda i: (nt - 1 - i, 0)), row, row, vec,
                   pl.BlockSpec((_FFN_K, f), lambda i: (0, 0)), pl.BlockSpec((1, f), lambda i: (0, 0))],
        out_shape=[jax.ShapeDtypeStruct((s, 2 * f), _MXU), jax.ShapeDtypeStruct((s, d), _F32),
                   jax.ShapeDtypeStruct((s, d), _MXU), jax.ShapeDtypeStruct((1, d), _F32),
                   jax.ShapeDtypeStruct((_FFN_K, f), _F32), jax.ShapeDtypeStruct((1, f), _F32)],
        scratch_shapes=[pltpu.VMEM((t + 8, n), _F32), pltpu.VMEM((t + 8, n), _F32),
                        pltpu.VMEM((t + 8, n), _F32), pltpu.VMEM((t + 8, n), _F32), pltpu.VMEM((8, f), _F32)],
        sem=("arbitrary",), name=name, comm=comm, vmem=_VMEM_LIMIT_FUSED)


def _mesh_pos():
    return lax.axis_index("x"), lax.axis_index("y"), lax.axis_index("c")


def _flip(v, bit):
    return 1 - v if bit else v


def _sem_scratch(n):
    return [pltpu.SemaphoreType.DMA((7 * n,)), pltpu.SemaphoreType.DMA((7 * n,)), pltpu.SemaphoreType.DMA((n,))]


class _Gather:
    def __init__(self, xs):
        self.ins = list(xs)
        self.outs = [jax.ShapeDtypeStruct((_NDEV,) + v.shape, v.dtype) for v in xs]
        self.scratch = _sem_scratch(len(xs))

    def _plan(self, x_refs, out_refs, sems):
        send_sems, recv_sems, local_sems = sems
        x, y, c = _mesh_pos()
        me, sibling = (x, y, c), (x, y, 1 - c)
        chips = [(1 - x, y), (x, 1 - y), (1 - x, 1 - y)]

        def copy(a, k, block, to, src=None):
            slot = out_refs[a].at[4 * block[0] + 2 * block[1] + block[2]]
            return pltpu.make_async_remote_copy(
                src_ref=slot if src is None else src, dst_ref=slot,
                send_sem=send_sems.at[a * 7 + k], recv_sem=recv_sems.at[a * 7 + k],
                device_id=to, device_id_type=_MESH_ID)

        def own(a):
            return pltpu.make_async_copy(x_refs[a], out_refs[a].at[4 * x + 2 * y + c], local_sems.at[a])

        def first(a):
            return [copy(a, 0, me, sibling, src=x_refs[a])] + [
                copy(a, 1 + j, me, (*chip, c), src=x_refs[a]) for j, chip in enumerate(chips)]

        return me, sibling, chips, c, copy, own, first

    def start(self, x_refs, out_refs, sems):
        _, _, _, _, _, own, first = self._plan(x_refs, out_refs, sems)
        for a in range(len(self.ins)):
            own(a).start()
            for cp in first(a):
                cp.start()

    def finish(self, x_refs, out_refs, sems):
        me, sibling, chips, c, copy, own, first = self._plan(x_refs, out_refs, sems)
        n = len(self.ins)
        passed = []
        for a in range(n):
            for j, chip in enumerate(chips):
                copy(a, 1 + j, (*chip, c), me).wait_recv()
                fwd = copy(a, 4 + j, (*chip, c), sibling)
                fwd.start()
                passed.append(fwd)
        for a in range(n):
            copy(a, 0, sibling, me).wait_recv()
            for j, chip in enumerate(chips):
                copy(a, 4 + j, (*chip, 1 - c), me).wait_recv()
        for a in range(n):
            for cp in first(a):
                cp.wait_send()
        for cp in passed:
            cp.wait_send()
        for a in range(n):
            own(a).wait()


class _Exchange:
    def __init__(self, gs):
        self.ins = list(gs)
        self.outs = [jax.ShapeDtypeStruct(v.shape, v.dtype) for v in gs]
        self.scratch = _sem_scratch(len(gs))

    def _plan(self, g_refs, r_refs, sems):
        send_sems, recv_sems, local_sems = sems
        x, y, c = _mesh_pos()
        me_idx = 4 * x + 2 * y + c
        n = len(self.ins)

        def copy(a, k):
            peer = (_flip(x, k & 4), _flip(y, k & 2), _flip(c, k & 1))
            peer_idx = 4 * peer[0] + 2 * peer[1] + peer[2]
            return pltpu.make_async_remote_copy(
                src_ref=g_refs[a].at[peer_idx], dst_ref=r_refs[a].at[me_idx],
                send_sem=send_sems.at[a * 7 + k - 1], recv_sem=recv_sems.at[a * 7 + k - 1],
                device_id=peer, device_id_type=_MESH_ID)

        copies = [copy(a, k) for a in range(n) for k in range(1, _NDEV)]
        mine = [pltpu.make_async_copy(g_refs[a].at[me_idx], r_refs[a].at[me_idx], local_sems.at[a])
                for a in range(n)]
        return copies, mine

    def start(self, g_refs, r_refs, sems):
        copies, mine = self._plan(g_refs, r_refs, sems)
        for cp in copies + mine:
            cp.start()

    def finish(self, g_refs, r_refs, sems):
        copies, mine = self._plan(g_refs, r_refs, sems)
        for cp in copies:
            cp.wait_recv()
        for cp in copies:
            cp.wait_send()
        for cp in mine:
            cp.wait()


class _Both:
    def __init__(self, first, second):
        self.parts = (first, second)
        self.ins = first.ins + second.ins
        self.outs = first.outs + second.outs
        self.scratch = first.scratch + second.scratch

    def _split(self, ins, outs, sems):
        a, b = self.parts
        na, nb = len(a.ins), len(a.scratch)
        return (a, ins[:na], outs[:na], sems[:nb]), (b, ins[na:], outs[na:], sems[nb:])

    def start(self, ins, outs, sems):
        for part, i, o, s in self._split(ins, outs, sems):
            part.start(i, o, s)

    def finish(self, ins, outs, sems):
        for part, i, o, s in self._split(ins, outs, sems):
            part.finish(i, o, s)


def _comm_call(comm, name):
    def body(*refs):
        n_i, n_o = len(comm.ins), len(comm.outs)
        ins, outs, sems = refs[:n_i], refs[n_i:n_i + n_o], refs[n_i + n_o:]
        comm.start(ins, outs, sems)
        comm.finish(ins, outs, sems)

    return pl.pallas_call(
        body, out_shape=list(comm.outs), in_specs=[_ANY] * len(comm.ins), out_specs=[_ANY] * len(comm.outs),
        scratch_shapes=list(comm.scratch), name=name)(*comm.ins)


def _adamw_math(w, g, m, v):
    m = _ADAM_B1 * m + (1.0 - _ADAM_B1) * g
    v = _ADAM_B2 * v + (1.0 - _ADAM_B2) * (g * g)
    m_hat = m / (1.0 - _ADAM_B1 ** _ADAM_STEP)
    v_hat = v / (1.0 - _ADAM_B2 ** _ADAM_STEP)
    delta = -_ADAM_LR * (m_hat / (jnp.sqrt(v_hat) + _ADAM_EPS) + _ADAM_WD * w)
    return delta, m, v


def _sum_adamw(parts, w, m, v, name):
    r, c = w.shape
    tr = _tile(r, 128)

    def body(p_ref, w_ref, m_ref, v_ref, g_ref, d_ref, nm_ref, nv_ref):
        g = p_ref[0].astype(_F32)
        for j in range(1, _NDEV):
            g = g + p_ref[j].astype(_F32)
        delta, nm, nv = _adamw_math(w_ref[...], g, m_ref[...], v_ref[...])
        g_ref[...] = g
        d_ref[...] = delta
        nm_ref[...] = nm
        nv_ref[...] = nv

    blk = pl.BlockSpec((tr, c), lambda i: (i, 0))
    return pl.pallas_call(
        body, grid=(r // tr,),
        in_specs=[pl.BlockSpec((_NDEV, tr, c), lambda i: (0, i, 0)), blk, blk, blk],
        out_specs=[blk] * 4, out_shape=[jax.ShapeDtypeStruct((r, c), _F32)] * 4,
        compiler_params=_cparams("parallel"), name=name,
    )(parts, w, m, v)


def _sum8(parts, name):
    _, r, c = parts.shape

    def body(p_ref, o_ref):
        g = p_ref[0]
        for j in range(1, _NDEV):
            g = g + p_ref[j]
        o_ref[...] = g

    return pl.pallas_call(
        body, grid=(1,), in_specs=[pl.BlockSpec((_NDEV, r, c), lambda i: (0, 0, 0))],
        out_specs=pl.BlockSpec((r, c), lambda i: (0, 0)), out_shape=jax.ShapeDtypeStruct((r, c), _F32),
        compiler_params=_cparams("arbitrary"), name=name,
    )(parts)


def _adamw_many(gs, ws, ms, vs, name):
    n = len(ws)

    def body(*refs):
        g_refs, w_refs, m_refs, v_refs = (refs[k * n:(k + 1) * n] for k in range(4))
        d_refs, nm_refs, nv_refs = (refs[(4 + k) * n:(5 + k) * n] for k in range(3))
        for k in range(n):
            delta, nm, nv = _adamw_math(w_refs[k][...], g_refs[k][...], m_refs[k][...], v_refs[k][...])
            d_refs[k][...] = delta
            nm_refs[k][...] = nm
            nv_refs[k][...] = nv

    def whole(arr):
        return pl.BlockSpec(arr.shape, lambda i, nd=arr.ndim: (0,) * nd)

    specs = [whole(w) for w in ws]
    res = pl.pallas_call(
        body, grid=(1,), in_specs=specs * 4, out_specs=specs * 3,
        out_shape=[jax.ShapeDtypeStruct(w.shape, _F32) for w in ws] * 3,
        compiler_params=_cparams("arbitrary"), name=name,
    )(*gs, *ws, *ms, *vs)
    return res[:n], res[n:2 * n], res[2 * n:]


def _pack(arrs):
    flat = jnp.concatenate([a.reshape(-1).astype(_F32) for a in arrs])
    pad = (-flat.shape[0]) % 1024
    return jnp.pad(flat, (0, pad)).reshape(-1, 128)


def _unpack(flat2d, shapes):
    flat = flat2d.reshape(-1)
    out, off = [], 0
    for sh in shapes:
        size = 1
        for dim in sh:
            size *= dim
        out.append(flat[off:off + size].reshape(sh))
        off += size
    return out


def _block_diag(w):
    h, hd, _ = w.shape
    eye = jnp.eye(h, dtype=w.dtype)
    return (eye[:, None, :, None] * w[:, :, None, :]).reshape(h * hd, h * hd)


def _diag_blocks(full, h):
    hd = full.shape[0] // h
    return jnp.stack([full[i * hd:(i + 1) * hd, i * hd:(i + 1) * hd] for i in range(h)])


def kernel(x, mem, mix_norm_g, w_in, lru_conv_w, lru_conv_b, lru_w_a, lru_b_a, lru_w_x, lru_b_x, lru_lambda, conf_conv_w, conf_conv_b, conf_ln_g, conf_ln_b, w_out, xa_norm_g, mem_norm_g, w_q, w_kv, w_o, ffn_norm_g, w_up, ffn_conv_w, ffn_conv_b, w_down, final_norm_g, loss_target, m_mix_norm_g, m_w_in, m_lru_conv_w, m_lru_conv_b, m_lru_w_a, m_lru_b_a, m_lru_w_x, m_lru_b_x, m_lru_lambda, m_conf_conv_w, m_conf_conv_b, m_conf_ln_g, m_conf_ln_b, m_w_out, m_xa_norm_g, m_mem_norm_g, m_w_q, m_w_kv, m_w_o, m_ffn_norm_g, m_w_up, m_ffn_conv_w, m_ffn_conv_b, m_w_down, m_final_norm_g, v_mix_norm_g, v_w_in, v_lru_conv_w, v_lru_conv_b, v_lru_w_a, v_lru_b_a, v_lru_w_x, v_lru_b_x, v_lru_lambda, v_conf_conv_w, v_conf_conv_b, v_conf_ln_g, v_conf_ln_b, v_w_out, v_xa_norm_g, v_mem_norm_g, v_w_q, v_w_kv, v_w_o, v_ffn_norm_g, v_w_up, v_ffn_conv_w, v_ffn_conv_b, v_w_down, v_final_norm_g):
    names = ["mix_norm_g", "w_in", "lru_conv_w", "lru_conv_b", "lru_w_a", "lru_b_a", "lru_w_x", "lru_b_x",
             "lru_lambda", "conf_conv_w", "conf_conv_b", "conf_ln_g", "conf_ln_b", "w_out", "xa_norm_g",
             "mem_norm_g", "w_q", "w_kv", "w_o", "ffn_norm_g", "w_up", "ffn_conv_w", "ffn_conv_b", "w_down",
             "final_norm_g"]
    loc = locals()
    W = {n: loc[n] for n in names}
    M = {n: loc["m_" + n] for n in names}
    V = {n: loc["v_" + n] for n in names}
    big = ["w_in", "w_out", "w_q", "w_kv", "w_o", "w_up", "w_down"]
    conv_sharded = ["lru_conv_w", "conf_conv_w", "ffn_conv_w"]

    xs = x[0]
    mems = mem[0]
    tgt = loss_target[0]
    me = 4 * lax.axis_index("x") + 2 * lax.axis_index("y") + lax.axis_index("c")

    conv_shapes = [W[n].shape[1:] for n in conv_sharded]
    conv_pack = _pack([W[n][0] for n in conv_sharded])
    shard = {n: W[n][0].astype(_XFER) for n in big}
    h1, (g_in, g_out, g_conv) = _rms_fwd(
        xs, mix_norm_g, "rms1_fwd", comm=_Gather([shard["w_in"], shard["w_out"], conv_pack]))
    convs = [[] for _ in conv_sharded]
    for j in range(_NDEV):
        for idx, part in enumerate(_unpack(g_conv[j], conv_shapes)):
            convs[idx].append(part)
    lcw, ccw, fcw = [jnp.concatenate(parts, axis=-1) for parts in convs]

    wab = jnp.concatenate([_block_diag(lru_w_a[0]), _block_diag(lru_w_x[0])], axis=1).astype(_MXU)
    mixer_params = (lcw, lru_conv_b, wab, lru_b_a, lru_b_x, lru_lambda, ccw, conf_conv_b, conf_ln_g, conf_ln_b)

    w_out_f = g_out.reshape(-1, g_out.shape[-1])
    (z, ycat, hs, cc, x1, h2), (g_q, g_kv, g_o, g_up, g_down) = _mixer_fwd(
        xs, h1, g_in, w_out_f, xa_norm_g, *mixer_params, "mixer_fwd",
        comm=_Gather([shard[n] for n in ("w_q", "w_kv", "w_o", "w_up", "w_down")]))
    w_q_f = g_q.reshape(-1, g_q.shape[-1])
    w_o_f = g_o.reshape(-1, g_o.shape[-1])
    w_down_f = g_down.reshape(-1, g_down.shape[-1])
    row32, row16, vec32 = (_F32, "row"), (_MXU, "row"), (_F32, "vec")
    mn = _rms_fwd(mems, mem_norm_g, "rmsm_fwd")
    kv = _mm_nn_stacked(mn, g_kv, _MXU, "mm_kv_fwd")
    q, o, x2, h3 = _attn_fwd(h2, x1, w_q_f, kv, w_o_f, ffn_norm_g, "attn_fwd")

    gfin = final_norm_g.reshape(1, -1)
    g0, act, gelu_g, u_dgelu, dx3, dx3b, lvec, dg_final = _ffn_fused_fwd(
        h3, g_up, w_down_f, fcw, ffn_conv_b, x2, tgt, gfin, "ffn_fwd")

    def rows8(p):
        return p.reshape(_NDEV, p.shape[0] // _NDEV, p.shape[1])

    p_down = _mm_tn_nat(act, dx3b, _XFER, "mm_down_wgrad", ts=2048)
    (dgu, dx2, dx2b, dg_ffn, dfcw, dfcb), (r_down,) = _ffn_fused_bwd(
        dx3b, g0, gelu_g, u_dgelu, w_down_f, g_up, fcw, x2, ffn_norm_g, dx3, "ffn_bwd",
        comm=_Exchange([rows8(p_down)]))
    p_up = _mm_tn_stacked(h3, dgu, _NDEV, _XFER, "mm_up_wgrad", slabs=2)

    (dx1, dx1b, dg_xa, dk, dv, p_o, p_q), (r_up,) = _attn_bwd(
        dx2b, dx2, q, o, h2, x1, kv, w_o_f, w_q_f, xa_norm_g, "attn_bwd", comm=_Exchange([p_up]))
    dkv = jnp.concatenate([dk, dv], axis=1).astype(_MXU)
    dmn = _mm_nt_stacked(dkv, g_kv, "mm_kv_dgrad", outs=[row32], slabs=_NDEV)
    p_kv = _mm_tn_stacked(mn, dkv, _NDEV, _XFER, "mm_kv_wgrad", slabs=_NDEV)
    _, _, dg_mem = _rms_bwd(dmn, mems, mem_norm_g, None, "rmsm_bwd")

    p_out = _mm_tn_nat(ycat, dx1b, _XFER, "mm_out_wgrad", ts=2048)
    ((dz, p_in, dlcw, dlcb, dwab, dba, dbx, dlam, dccw, dccb, dlng, dlnb),
     (r_o, r_q, r_kv, r_out)) = _mixer_bwd(
        dx1b, w_out_f, h1, z, hs, cc, *mixer_params, "mixer_bwd",
        comm=_Exchange([rows8(p_o), rows8(p_q), p_kv, rows8(p_out)]))

    c = _D_LRU
    heads = lru_w_a.shape[1]
    small_partial = {
        "lru_conv_w": dlcw, "lru_conv_b": dlcb,
        "lru_w_a": _diag_blocks(dwab[:, :c], heads), "lru_b_a": dba,
        "lru_w_x": _diag_blocks(dwab[:, c:], heads), "lru_b_x": dbx, "lru_lambda": dlam,
        "conf_conv_w": dccw, "conf_conv_b": dccb, "conf_ln_g": dlng, "conf_ln_b": dlnb,
        "xa_norm_g": dg_xa, "mem_norm_g": dg_mem, "ffn_norm_g": dg_ffn,
        "ffn_conv_w": dfcw, "ffn_conv_b": dfcb, "final_norm_g": dg_final,
    }
    early = list(small_partial)
    early_shapes = [small_partial[n].shape for n in early] + [lvec.shape]
    (grad_x, dg_mix), (r_in, early_all) = _mm_nt_stacked(
        dz, g_in, "mm_in_dgrad", epi=_epi_rms_bwd, extra=[(xs, "row"), (mix_norm_g, "vec"), (dx1, "row")],
        outs=[row32, vec32], tm=512, slabs=_NDEV,
        comm=_Both(_Exchange([p_in]), _Gather([_pack([small_partial[n] for n in early] + [lvec])])))
    (mix_all,) = _comm_call(_Gather([dg_mix]), "gather_mix_grad")
    small = early + ["mix_norm_g"]
    small_sum = _unpack(_sum8(early_all, "sum_small_grads"), early_shapes)
    loss = 0.5 * jnp.sum(small_sum.pop()) / xs.shape[1]
    small_sum.append(_sum8(mix_all.reshape(_NDEV, 8, -1), "sum_mix_grad").reshape(dg_mix.shape))
    received = {"w_in": r_in, "w_out": r_out, "w_q": r_q, "w_kv": r_kv, "w_o": r_o, "w_up": r_up,
                "w_down": r_down}

    grads, deltas, new_m, new_v = {}, {}, {}, {}
    for n, rec in ((n, received[n]) for n in big):
        shp = W[n].shape
        w2, m2, v2 = (t.reshape(rec.shape[1:]) for t in (W[n], M[n], V[n]))
        outs = _sum_adamw(rec, w2, m2, v2, "adamw_" + n)
        grads[n], deltas[n], new_m[n], new_v[n] = (t.reshape(shp) for t in outs)

    small_g = []
    for n, g in zip(small, small_sum):
        if n in conv_sharded:
            width = W[n].shape[-1]
            g = lax.dynamic_slice_in_dim(g, me * width, width, axis=1)
        small_g.append(g.reshape(W[n].shape))

    def at_least_2d(a):
        return a.reshape(1, -1) if a.ndim == 1 else a

    sd, sm, sv = _adamw_many([at_least_2d(g) for g in small_g], [at_least_2d(W[n]) for n in small],
                             [at_least_2d(M[n]) for n in small], [at_least_2d(V[n]) for n in small], "adamw_small")
    for n, g, d_, m_, v_ in zip(small, small_g, sd, sm, sv):
        shp = W[n].shape
        grads[n], deltas[n], new_m[n], new_v[n] = g, d_.reshape(shp), m_.reshape(shp), v_.reshape(shp)

    return (loss, grad_x[None], *[grads[n] for n in names], *[deltas[n] for n in names],
            *[new_m[n] for n in names], *[new_v[n] for n in names])
```

```python
import functools

import jax
import jax.numpy as jnp
from jax import lax
from jax.experimental import pallas as pl
from jax.experimental.pallas import tpu as pltpu

_MXU = jnp.bfloat16
_XFER = jnp.bfloat16
_F32 = jnp.float32
_EPS = 1e-6
_NDEV = 8
_VMEM_LIMIT = 48 * 1024 * 1024

_D_LRU = 512
_XA_HEADS = 4
_RG_C = 8.0
_ADAM_LR, _ADAM_B1, _ADAM_B2, _ADAM_EPS, _ADAM_WD, _ADAM_STEP = 0.001, 0.9, 0.999, 1e-08, 0.01, 10

_MESH_ID = pl.DeviceIdType.MESH
_ANY = pl.BlockSpec(memory_space=pl.ANY)


def _cparams(*sem, vmem=_VMEM_LIMIT):
    return pltpu.CompilerParams(dimension_semantics=tuple(sem), vmem_limit_bytes=vmem)


def _pcall(body, *, args, grid, in_specs, out_specs, out_shape, sem, name, scratch_shapes=(), comm=None,
           vmem=_VMEM_LIMIT):
    outs_l = list(out_shape) if isinstance(out_shape, (list, tuple)) else [out_shape]
    ospecs_l = list(out_specs) if isinstance(out_specs, (list, tuple)) else [out_specs]
    n_in, n_out, n_scr = len(args), len(outs_l), len(scratch_shapes)
    if comm is None:
        res = pl.pallas_call(
            body, grid=grid, in_specs=list(in_specs), out_specs=ospecs_l, out_shape=outs_l,
            scratch_shapes=list(scratch_shapes), compiler_params=_cparams(*sem, vmem=vmem), name=name)(*args)
        return list(res), []
    n_ci, n_co = len(comm.ins), len(comm.outs)

    def wrapped(*refs):
        ins, cins = refs[:n_in], refs[n_in:n_in + n_ci]
        o = n_in + n_ci
        outs, couts = refs[o:o + n_out], refs[o + n_out:o + n_out + n_co]
        s = o + n_out + n_co
        scr, cscr = refs[s:s + n_scr], refs[s + n_scr:]
        first = pl.program_id(0) == 0
        last = pl.program_id(0) == grid[0] - 1
        for ax in range(1, len(grid)):
            first = jnp.logical_and(first, pl.program_id(ax) == 0)
            last = jnp.logical_and(last, pl.program_id(ax) == grid[ax] - 1)

        @pl.when(first)
        def _():
            comm.start(cins, couts, cscr)

        body(*ins, *outs, *scr)

        @pl.when(last)
        def _():
            comm.finish(cins, couts, cscr)

    res = pl.pallas_call(
        wrapped, grid=grid, in_specs=list(in_specs) + [_ANY] * n_ci, out_specs=ospecs_l + [_ANY] * n_co,
        out_shape=outs_l + list(comm.outs), scratch_shapes=list(scratch_shapes) + list(comm.scratch),
        compiler_params=_cparams(*(("arbitrary",) * len(grid)), vmem=vmem), name=name)(*args, *comm.ins)
    return list(res[:n_out]), list(res[n_out:])


def _sigmoid(v):
    return 1.0 / (1.0 + jnp.exp(-v))


_GELU_C = 0.7978845608028654
_GELU_K = 0.044715


def _gelu(v):
    t = jnp.tanh(_GELU_C * (v + _GELU_K * v * v * v))
    return 0.5 * v * (1.0 + t)


def _gelu_and_grad(v):
    v2 = v * v
    s = 0.5 * jnp.tanh(v * (_GELU_C + (_GELU_C * _GELU_K) * v2)) + 0.5
    g = v * s
    dg = s + (g * (1.0 - s)) * ((2.0 * _GELU_C) + (6.0 * _GELU_C * _GELU_K) * v2)
    return g, dg


def _softplus(v):
    e = jnp.exp(-jnp.abs(v))
    log1p = jnp.where(e < 1e-2, e * (1.0 - e * (0.5 - e * (1.0 / 3.0))), jnp.log(1.0 + e))
    return jnp.maximum(v, 0.0) + log1p


def _neg_expm1(v):
    series = -v * (1.0 + v * (0.5 + v * ((1.0 / 6.0) + v * (1.0 / 24.0))))
    return jnp.where(v > -0.0625, series, 1.0 - jnp.exp(v))


def _dot(a, b, dims):
    return lax.dot_general(a.astype(_MXU), b.astype(_MXU), (dims, ((), ())), preferred_element_type=_F32)


_NN = ((1,), (0,))
_NT = ((1,), (1,))
_TN = ((0,), (0,))


def _scan_fwd(a, b, rows):
    n = a.shape[0]
    d = 1
    while d < n:
        keep = rows >= d
        b = jnp.where(keep, b + a * pltpu.roll(b, d, 0), b)
        a = jnp.where(keep, a * pltpu.roll(a, d, 0), a)
        d *= 2
    return a, b


def _scan_rev(a, b, rows):
    n = a.shape[0]
    d = 1
    while d < n:
        keep = rows < n - d
        b = jnp.where(keep, b + a * pltpu.roll(b, n - d, 0), b)
        a = jnp.where(keep, a * pltpu.roll(a, n - d, 0), a)
        d *= 2
    return a, b


def _colsum(v):
    return jnp.sum(v, axis=0, keepdims=True)


def _mm(a, b, *, dims, grid, a_spec, b_spec, outs, acc_shape, name, extra=(), epi=None, slabs=1, comm=None,
        vmem=_VMEM_LIMIT):
    nred = grid[-1]
    red_axis = len(grid) - 1
    n_ex, n_out = len(extra), len(outs)
    epi = _epi_store if epi is None else epi

    def body(*refs):
        a_ref, b_ref = refs[:2]
        ex, o_refs, acc_ref = refs[2:2 + n_ex], refs[2 + n_ex:2 + n_ex + n_out], refs[-1]
        if slabs == 1:
            p = _dot(a_ref[...], b_ref[...], dims)
        else:
            n = b_ref.shape[-1]
            p = _dot(a_ref[:, 0:n], b_ref[0], dims)
            for jj in range(1, slabs):
                p = p + _dot(a_ref[:, jj * n:(jj + 1) * n], b_ref[jj], dims)

        first_rows = pl.program_id(0) == 0
        if nred == 1:
            epi(p, ex, o_refs, first_rows)
        else:
            k = pl.program_id(red_axis)

            @pl.when(k == 0)
            def _():
                acc_ref[...] = p

            @pl.when(jnp.logical_and(k > 0, k < nred - 1))
            def _():
                acc_ref[...] += p

            @pl.when(k == nred - 1)
            def _():
                epi(acc_ref[...] + p, ex, o_refs, first_rows)

    sem = ("parallel",) * (len(grid) - 1) + ("arbitrary",)
    if any(o[0].shape[0] == 1 for o in outs):
        sem = ("arbitrary",) * len(grid)
    res, cres = _pcall(
        body, args=(a, b) + tuple(e[0] for e in extra), grid=grid,
        in_specs=[a_spec, b_spec] + [e[1] for e in extra],
        out_specs=[o[1] for o in outs], out_shape=[o[0] for o in outs],
        scratch_shapes=[pltpu.VMEM(acc_shape if nred > 1 else (8, 128), _F32)], sem=sem, name=name, comm=comm,
        vmem=vmem)
    res = res[0] if n_out == 1 else res
    return res if comm is None else (res, cres)


def _epi_store(total, ex, outs, first_rows):
    outs[0][...] = total.astype(outs[0].dtype)


def _epi_residual_rms(total, ex, outs, first_rows):
    res_ref, g_ref = ex
    xn = total + res_ref[...]
    outs[0][...] = xn
    r = lax.rsqrt(jnp.mean(xn * xn, axis=-1, keepdims=True) + _EPS)
    outs[1][...] = (xn * r * g_ref[...]).astype(outs[1].dtype)


def _epi_rms_bwd(total, ex, outs, first_rows):
    x_ref, g_ref, dres_ref = ex
    dg_ref = outs[-1]
    xv = x_ref[...]
    r = lax.rsqrt(jnp.mean(xv * xv, axis=-1, keepdims=True) + _EPS)
    xhat = xv * r
    dxh = total * g_ref[...]
    dx = dres_ref[...] + r * (dxh - xhat * jnp.mean(dxh * xhat, axis=-1, keepdims=True))
    for o_ref in outs[:-1]:
        o_ref[...] = dx.astype(o_ref.dtype)

    @pl.when(first_rows)
    def _():
        dg_ref[...] = jnp.zeros_like(dg_ref)

    dg_ref[...] += _colsum(total * xhat)


def _epi_final(total, ex, outs, first_rows):
    res_ref, t_ref, g_ref = ex
    dx_ref, dxb_ref, l_ref, dg_ref = outs
    xv = total + res_ref[...]
    gv = g_ref[...]
    d = xv.shape[-1]
    r = lax.rsqrt(jnp.mean(xv * xv, axis=-1, keepdims=True) + _EPS)
    xhat = xv * r
    err = xhat * gv - t_ref[...]
    dy = err * (1.0 / d)
    dxh = dy * gv
    dx = r * (dxh - xhat * jnp.mean(dxh * xhat, axis=-1, keepdims=True))
    dx_ref[...] = dx
    dxb_ref[...] = dx.astype(dxb_ref.dtype)

    @pl.when(first_rows)
    def _():
        l_ref[...] = jnp.zeros_like(l_ref)
        dg_ref[...] = jnp.zeros_like(dg_ref)

    l_ref[...] += _colsum(err * err)
    dg_ref[...] += _colsum(dy * xhat)


def _tile(m, cap):
    t = min(m, cap)
    assert m % t == 0
    return t


def _row_spec(tm, n):
    return pl.BlockSpec((tm, n), lambda i, *_: (i, 0))


def _vec_spec(n):
    return pl.BlockSpec((1, n), lambda *_: (0, 0))


def _row_io(m, n, tm, extra, outs):
    def spec(kind):
        return _row_spec(tm, n) if kind == "row" else _vec_spec(n)

    ex = [(arr, spec(kind)) for arr, kind in extra]
    os_ = [(jax.ShapeDtypeStruct((m, n) if kind == "row" else (1, n), dt), spec(kind)) for dt, kind in outs]
    return ex, os_


def _mm_nn_stacked(a, w, out_dtype, name, comm=None, tm=1024):
    m, k = a.shape
    j, _, n = w.shape
    tm = _tile(m, tm)
    return _mm(a, w, dims=_NN, grid=(m // tm, j, 1),
               a_spec=pl.BlockSpec((tm, k), lambda i, jj, r: (i, 0)),
               b_spec=pl.BlockSpec((None, k, n), lambda i, jj, r: (jj, 0, 0)),
               outs=[(jax.ShapeDtypeStruct((m, j * n), out_dtype), pl.BlockSpec((tm, n), lambda i, jj, r: (i, jj)))],
               acc_shape=(tm, n), name=name, comm=comm)


def _mm_nt_stacked(dc, w, name, *, outs, extra=(), epi=None, comm=None, tm=1024, slabs=1):
    m = dc.shape[0]
    j, k, n = w.shape
    tm = _tile(m, tm)
    assert j % slabs == 0
    ex, os_ = _row_io(m, k, tm, extra, outs)
    wblk = (None, k, n) if slabs == 1 else (slabs, k, n)
    return _mm(dc, w, dims=_NT, grid=(m // tm, j // slabs),
               a_spec=pl.BlockSpec((tm, slabs * n), lambda i, r: (i, r)),
               b_spec=pl.BlockSpec(wblk, lambda i, r: (r, 0, 0)),
               outs=os_, extra=ex, epi=epi, acc_shape=(tm, k), name=name, slabs=slabs, comm=comm)


def _mm_tn_stacked(a, dc, j, out_dtype, name, slabs=1, ts=1024, vmem=_VMEM_LIMIT):
    s, k = a.shape
    n = dc.shape[1] // j
    ts = _tile(s, ts)
    assert j % slabs == 0

    def epi(total, ex, outs, first_rows):
        for jj in range(slabs):
            outs[0][jj] = total[:, jj * n:(jj + 1) * n].astype(outs[0].dtype)

    return _mm(a, dc, dims=_TN, grid=(j // slabs, s // ts),
               a_spec=pl.BlockSpec((ts, k), lambda jj, r: (r, 0)),
               b_spec=pl.BlockSpec((ts, slabs * n), lambda jj, r: (r, jj)),
               outs=[(jax.ShapeDtypeStruct((j, k, n), out_dtype),
                      pl.BlockSpec((slabs, k, n), lambda jj, r: (jj, 0, 0)))],
               epi=epi, acc_shape=(k, slabs * n), name=name, vmem=vmem)


def _mm_tn_nat(a, dc, out_dtype, name, ts=1024):
    s, kt = a.shape
    n = dc.shape[1]
    ts = _tile(s, ts)
    tkb = _tile(kt, 512)
    return _mm(a, dc, dims=_TN, grid=(kt // tkb, s // ts),
               a_spec=pl.BlockSpec((ts, tkb), lambda kb, r: (r, kb)),
               b_spec=pl.BlockSpec((ts, n), lambda kb, r: (r, 0)),
               outs=[(jax.ShapeDtypeStruct((kt, n), out_dtype), pl.BlockSpec((tkb, n), lambda kb, r: (kb, 0)))],
               acc_shape=(tkb, n), name=name)


def _rms_fwd(x, g, name, comm=None):
    s, d = x.shape
    t = _tile(s, 1024)

    def body(x_ref, g_ref, h_ref):
        xv = x_ref[...]
        r = lax.rsqrt(jnp.mean(xv * xv, axis=-1, keepdims=True) + _EPS)
        h_ref[...] = (xv * r * g_ref[...]).astype(h_ref.dtype)

    res, cres = _pcall(
        body, args=(x, g), grid=(s // t,),
        in_specs=[pl.BlockSpec((t, d), lambda i: (i, 0)), pl.BlockSpec((1, d), lambda i: (0, 0))],
        out_specs=pl.BlockSpec((t, d), lambda i: (i, 0)),
        out_shape=jax.ShapeDtypeStruct((s, d), _MXU), sem=("parallel",), name=name, comm=comm)
    return res[0] if comm is None else (res[0], cres)


def _rms_bwd(dh, x, g, dres, name):
    s, d = x.shape
    t = _tile(s, 256)
    has_res = dres is not None

    def body(*refs):
        if has_res:
            dh_ref, x_ref, g_ref, dres_ref, dx_ref, dxb_ref, dg_ref = refs
        else:
            dh_ref, x_ref, g_ref, dx_ref, dxb_ref, dg_ref = refs
        xv = x_ref[...]
        dhv = dh_ref[...]
        r = lax.rsqrt(jnp.mean(xv * xv, axis=-1, keepdims=True) + _EPS)
        xhat = xv * r
        dxh = dhv * g_ref[...]
        dx = r * (dxh - xhat * jnp.mean(dxh * xhat, axis=-1, keepdims=True))
        if has_res:
            dx = dx + dres_ref[...]
        dx_ref[...] = dx
        dxb_ref[...] = dx.astype(dxb_ref.dtype)

        @pl.when(pl.program_id(0) == 0)
        def _():
            dg_ref[...] = jnp.zeros_like(dg_ref)

        dg_ref[...] += _colsum(dhv * xhat)

    row = pl.BlockSpec((t, d), lambda i: (i, 0))
    vec = pl.BlockSpec((1, d), lambda i: (0, 0))
    in_specs = [row, row, vec] + ([row] if has_res else [])
    args = (dh, x, g) + ((dres,) if has_res else ())
    return pl.pallas_call(
        body, grid=(s // t,), in_specs=in_specs, out_specs=[row, row, vec],
        out_shape=[jax.ShapeDtypeStruct((s, d), _F32), jax.ShapeDtypeStruct((s, d), _MXU),
                   jax.ShapeDtypeStruct((1, d), _F32)],
        compiler_params=_cparams("arbitrary"), name=name,
    )(*args)


_LRU_K = 4
_CONF_K = 31
_LRU_HALO = 8
_CONF_HALO = 32
_MIX_T = 512


def _lru_gates(lx, wab_ref, ba_ref, bx_ref, lam_ref):
    c = _D_LRU
    pre = _dot(lx, wab_ref[...], _NN)
    r = _sigmoid(pre[:, :c] + ba_ref[...])
    ig = _sigmoid(pre[:, c:] + bx_ref[...])
    sp = _softplus(-lam_ref[...])
    log_a = (-_RG_C) * r * sp
    a = jnp.exp(log_a)
    mult = jnp.sqrt(_neg_expm1(2.0 * log_a))
    return r, ig, sp, a, mult


def _causal_conv(ext_ref, halo, w_ref, b_ref, taps, t):
    acc = b_ref[...] + w_ref[0:1, :] * ext_ref[pl.ds(halo - (taps - 1), t), :]
    for k in range(1, taps):
        acc = acc + w_ref[k:k + 1, :] * ext_ref[pl.ds(halo - (taps - 1) + k, t), :]
    return acc


class _Windows:
    def __init__(self, ext_ref, shifted_ref, t):
        self.ext_ref, self.shifted_ref, self.t = ext_ref, shifted_ref, t
        rows = t + 24
        for r in range(1, 8):
            shifted_ref[r - 1, 0:rows, :] = ext_ref[pl.ds(r, rows), :]

    def __call__(self, off):
        q, r = divmod(off, 8)
        if r == 0:
            return self.ext_ref[pl.ds(8 * q, self.t), :]
        return self.shifted_ref[r - 1, pl.ds(8 * q, self.t), :]


def _mixer_fwd(xs, h1, w_in, w_out, gnorm, lcw, lcb, wab, ba, bx, lam, ccw, ccb, lng, lnb, name, comm=None):
    s, d = xs.shape
    nblk, _, n = w_in.shape
    c = _D_LRU
    t = _tile(s, _MIX_T)
    nt = s // t

    def body(x_ref, h_ref, win_ref, wout_ref, gn_ref,
             lcw_ref, lcb_ref, wab_ref, ba_ref, bx_ref, lam_ref, ccw_ref, ccb_ref, lng_ref, lnb_ref,
             z_ref, ycat_ref, hs_ref, cc_ref, x1_ref, h2_ref,
             ext_ref, cge_ref, hc_ref, shifted_ref, zprev_ref):
        i = pl.program_id(0)
        first = i == 0
        rows = lax.broadcasted_iota(jnp.int32, (t, c), 0)

        @pl.when(first)
        def _():
            zprev_ref[...] = jnp.zeros_like(zprev_ref)

        hv = h_ref[...]
        for j in range(nblk):
            z_ref[:, j * n:(j + 1) * n] = _dot(hv, win_ref[j], _NN)
        lx0_ref, gate_ref = z_ref.at[:, 0:c], z_ref.at[:, c:2 * c]
        ca_ref, cb_ref = z_ref.at[:, 2 * c:3 * c], z_ref.at[:, 3 * c:4 * c]
        lx0h_ref = zprev_ref.at[_CONF_HALO - _LRU_HALO:_CONF_HALO, 0:c]
        cah_ref, cbh_ref = zprev_ref.at[:, 2 * c:3 * c], zprev_ref.at[:, 3 * c:4 * c]

        ext_ref[0:_LRU_HALO, :] = jnp.where(first, 0.0, lx0h_ref[...])
        ext_ref[_LRU_HALO:_LRU_HALO + t, :] = lx0_ref[...]
        lx = _causal_conv(ext_ref, _LRU_HALO, lcw_ref, lcb_ref, _LRU_K, t)
        r, ig, sp, a, mult = _lru_gates(lx, wab_ref, ba_ref, bx_ref, lam_ref)
        u = mult * (ig * lx)
        a_cum, h_loc = _scan_fwd(a, u, rows)

        @pl.when(first)
        def _():
            hc_ref[...] = jnp.zeros_like(hc_ref)

        h = h_loc + a_cum * hc_ref[7:8, :]
        hs_ref[...] = h
        hc_ref[...] = hs_ref[pl.ds(t - 8, 8), :]
        ycat_ref[:, 0:c] = (h * _gelu(gate_ref[...])).astype(ycat_ref.dtype)

        cge_ref[0:_CONF_HALO, :] = jnp.where(first, 0.0, cah_ref[...] * _sigmoid(cbh_ref[...]))
        cge_ref[_CONF_HALO:_CONF_HALO + t, :] = ca_ref[...] * _sigmoid(cb_ref[...])
        win = _Windows(cge_ref, shifted_ref, t)
        first_off = _CONF_HALO - (_CONF_K - 1)
        cc = ccb_ref[...] + ccw_ref[0:1, :] * win(first_off)
        for k in range(1, _CONF_K):
            cc = cc + ccw_ref[k:k + 1, :] * win(first_off + k)
        cc_ref[...] = cc
        xc = cc - jnp.mean(cc, axis=-1, keepdims=True)
        rstd = lax.rsqrt(jnp.mean(xc * xc, axis=-1, keepdims=True) + _EPS)
        ln = xc * rstd * lng_ref[...] + lnb_ref[...]
        ycat_ref[:, c:2 * c] = (ln * _sigmoid(ln)).astype(ycat_ref.dtype)

        zprev_ref[...] = z_ref[pl.ds(t - _CONF_HALO, _CONF_HALO), :]
        y = _dot(ycat_ref[...], wout_ref[...], _NN)
        _epi_residual_rms(y, (x_ref, gn_ref), (x1_ref, h2_ref), first)

    def const(arr):
        return pl.BlockSpec(arr.shape, lambda i: (0,) * arr.ndim, pipeline_mode=pl.Buffered(1))

    def rows_of(width):
        return pl.BlockSpec((t, width), lambda i: (i, 0))

    params = (lcw, lcb, wab, ba, bx, lam, ccw, ccb, lng, lnb)
    res, cres = _pcall(
        body, args=(xs, h1, w_in, w_out, gnorm, *params), grid=(nt,),
        in_specs=[rows_of(d), rows_of(d), const(w_in), const(w_out), const(gnorm)] + [const(p) for p in params],
        out_specs=[rows_of(nblk * n), rows_of(2 * c), rows_of(c), rows_of(c), rows_of(d), rows_of(d)],
        out_shape=[jax.ShapeDtypeStruct((s, nblk * n), _F32), jax.ShapeDtypeStruct((s, 2 * c), _MXU),
                   jax.ShapeDtypeStruct((s, c), _F32), jax.ShapeDtypeStruct((s, c), _F32),
                   jax.ShapeDtypeStruct((s, d), _F32), jax.ShapeDtypeStruct((s, d), _MXU)],
        scratch_shapes=[pltpu.VMEM((t + _LRU_HALO, c), _F32), pltpu.VMEM((t + _CONF_HALO, c), _F32),
                        pltpu.VMEM((8, c), _F32), pltpu.VMEM((7, t + _CONF_HALO, c), _F32),
                        pltpu.VMEM((_CONF_HALO, nblk * n), _F32)],
        sem=("arbitrary",), name=name, comm=comm)
    return res, cres


def _mixer_bwd(dx1b, w_out, h1, z, hs, cc, lcw, lcb, wab, ba, bx, lam, ccw, ccb, lng, lnb, name, comm=None):
    s = z.shape[0]
    d = h1.shape[1]
    c = _D_LRU
    t = _tile(s, _MIX_T)
    nt = s // t
    nblk = z.shape[1] // 256

    def body(dxb_ref, wout_ref, h1_ref, lx0_ref, lx0h_ref, gate_ref, ca_ref, cah_ref, cb_ref, cbh_ref,
             hs_ref, hsh_ref, cc_ref,
             lcw_ref, lcb_ref, wab_ref, ba_ref, bx_ref, lam_ref, ccw_ref, ccb_ref, lng_ref, lnb_ref,
             dz_ref, pin_ref, dlcw_ref, dlcb_ref, dwab_ref, dba_ref, dbx_ref, dlam_ref, dccw_ref, dccb_ref, dlng_ref,
             dlnb_ref,
             ext_ref, up_ref, cge_ref, dce_ref, xc_ref, dlxc_ref, dccc_ref, shifted_ref, dwin_ref):
        i = pl.program_id(0)
        first_tile = i == nt - 1
        last_tile = i == 0
        rows = lax.broadcasted_iota(jnp.int32, (t, c), 0)

        @pl.when(last_tile)
        def _():
            for ref in (dlcw_ref, dlcb_ref, dwab_ref, dba_ref, dbx_ref, dlam_ref, dccw_ref, dccb_ref, dlng_ref,
                        dlnb_ref, xc_ref, dlxc_ref, dccc_ref, dwin_ref):
                ref[...] = jnp.zeros_like(ref)

        dycat = _dot(dxb_ref[...], wout_ref[...], _NT)

        ext_ref[0:_LRU_HALO, :] = jnp.where(first_tile, 0.0, lx0h_ref[...])
        ext_ref[_LRU_HALO:_LRU_HALO + t, :] = lx0_ref[...]
        lx = _causal_conv(ext_ref, _LRU_HALO, lcw_ref, lcb_ref, _LRU_K, t)
        r, ig, sp, a, mult = _lru_gates(lx, wab_ref, ba_ref, bx_ref, lam_ref)
        h = hs_ref[...]
        gl, dgl = _gelu_and_grad(gate_ref[...])
        dyl = dycat[:, 0:c]
        dz_ref[:, c:2 * c] = (dyl * h * dgl).astype(dz_ref.dtype)
        dh = dyl * gl

        up_ref[0:t, :] = a
        up_ref[t:t + 8, :] = jnp.ones((8, c), _F32)
        a_up = up_ref[pl.ds(1, t), :]
        a_cum, g_loc = _scan_rev(a_up, dh, rows)
        gt = g_loc + a_cum * xc_ref[0:1, :]
        xc_ref[...] = (a * gt)[0:8, :]

        up_ref[0:8, :] = jnp.where(first_tile, 0.0, hsh_ref[...])
        up_ref[8:8 + t, :] = h
        hprev = up_ref[pl.ds(7, t), :]

        da = gt * hprev
        dmult = gt * ig * lx
        dig = gt * mult * lx
        dlx = gt * mult * ig
        dlog_a = da * a - dmult * a * a / mult
        dpre_r = dlog_a * (-_RG_C) * sp * r * (1.0 - r)
        dpre_i = dig * ig * (1.0 - ig)
        dlam_ref[...] += _colsum(dlog_a * r) * (_RG_C * _sigmoid(-lam_ref[...]))
        dba_ref[...] += _colsum(dpre_r)
        dbx_ref[...] += _colsum(dpre_i)
        dpre = jnp.concatenate([dpre_r, dpre_i], axis=1).astype(_MXU)
        dlx = dlx + _dot(dpre, wab_ref[...], _NT)
        dwab_ref[...] += _dot(lx, dpre, _TN)

        dlcb_ref[...] += _colsum(dlx)
        up_ref[0:t, :] = dlx
        up_ref[t:t + 8, :] = dlxc_ref[...]
        dlxc_ref[...] = dlx[0:8, :]
        acc = lcw_ref[0:1, :] * up_ref[pl.ds(_LRU_K - 1, t), :]
        for k in range(1, _LRU_K):
            acc = acc + lcw_ref[k:k + 1, :] * up_ref[pl.ds(_LRU_K - 1 - k, t), :]
        dz_ref[:, 0:c] = acc.astype(dz_ref.dtype)
        for k in range(_LRU_K):
            dlcw_ref[k:k + 1, :] += _colsum(dlx * ext_ref[pl.ds(_LRU_HALO - (_LRU_K - 1) + k, t), :])

        sig_b = _sigmoid(cb_ref[...])
        ca = ca_ref[...]
        cge_ref[0:_CONF_HALO, :] = jnp.where(first_tile, 0.0, cah_ref[...] * _sigmoid(cbh_ref[...]))
        cge_ref[_CONF_HALO:_CONF_HALO + t, :] = ca * sig_b
        ccv = cc_ref[...]
        xcen = ccv - jnp.mean(ccv, axis=-1, keepdims=True)
        rstd = lax.rsqrt(jnp.mean(xcen * xcen, axis=-1, keepdims=True) + _EPS)
        xn = xcen * rstd
        ln = xn * lng_ref[...] + lnb_ref[...]
        sg = _sigmoid(ln)
        dln = dycat[:, c:2 * c] * (sg * (1.0 + ln * (1.0 - sg)))
        dlng_ref[...] += _colsum(dln * xn)
        dlnb_ref[...] += _colsum(dln)
        dxn = dln * lng_ref[...]
        dcc = rstd * (dxn - jnp.mean(dxn, axis=-1, keepdims=True)
                      - xn * jnp.mean(dxn * xn, axis=-1, keepdims=True))
        dccb_ref[...] += _colsum(dcc)
        win = _Windows(cge_ref, shifted_ref, t)
        for k in range(_CONF_K):
            dccw_ref[k:k + 1, :] += _colsum(dcc * win(_CONF_HALO - (_CONF_K - 1) + k))
        dce_ref[0:t, :] = dcc
        dce_ref[t:t + _CONF_HALO, :] = dccc_ref[...]
        dccc_ref[...] = dcc[0:_CONF_HALO, :]
        win = _Windows(dce_ref, shifted_ref, t)
        dcg = ccw_ref[0:1, :] * win(_CONF_K - 1)
        for k in range(1, _CONF_K):
            dcg = dcg + ccw_ref[k:k + 1, :] * win(_CONF_K - 1 - k)
        dz_ref[:, 2 * c:3 * c] = (dcg * sig_b).astype(dz_ref.dtype)
        dz_ref[:, 3 * c:4 * c] = (dcg * ca * sig_b * (1.0 - sig_b)).astype(dz_ref.dtype)

        dwin_ref[...] += _dot(h1_ref[...], dz_ref[...], _TN)

        @pl.when(first_tile)
        def _():
            for j in range(nblk):
                pin_ref[j] = dwin_ref[:, j * 256:(j + 1) * 256].astype(pin_ref.dtype)

    def col(j):
        return pl.BlockSpec((t, c), lambda i: (nt - 1 - i, j))

    def halo(j, rows_):
        per = t // rows_
        return pl.BlockSpec((rows_, c), lambda i: (jnp.maximum((nt - 1 - i) * per - 1, 0), j))

    def full(shape):
        return pl.BlockSpec(shape, lambda i: (0,) * len(shape))

    params = (lcw, lcb, wab, ba, bx, lam, ccw, ccb, lng, lnb)
    small = [(_LRU_K, c), (1, c), (c, 2 * c), (1, c), (1, c), (1, c), (_CONF_K, c), (1, c), (1, c), (1, c)]
    wide = pl.BlockSpec((t, d), lambda i: (nt - 1 - i, 0))
    pin_shape = (nblk, d, 256)
    return _pcall(
        body, args=(dx1b, w_out, h1, z, z, z, z, z, z, z, hs, hs, cc, *params), grid=(nt,),
        in_specs=[wide, pl.BlockSpec(w_out.shape, lambda i: (0, 0), pipeline_mode=pl.Buffered(1)), wide,
                  col(0), halo(0, _LRU_HALO), col(1), col(2), halo(2, _CONF_HALO), col(3), halo(3, _CONF_HALO),
                  col(0), halo(0, 8), col(0)]
        + [full(p.shape) for p in params],
        out_specs=[pl.BlockSpec((t, 4 * c), lambda i: (nt - 1 - i, 0)), full(pin_shape)] + [full(sh) for sh in small],
        out_shape=[jax.ShapeDtypeStruct((s, 4 * c), _MXU), jax.ShapeDtypeStruct(pin_shape, _XFER)]
        + [jax.ShapeDtypeStruct(sh, _F32) for sh in small],
        scratch_shapes=[pltpu.VMEM((t + _LRU_HALO, c), _F32), pltpu.VMEM((t + 8, c), _F32),
                        pltpu.VMEM((t + _CONF_HALO, c), _F32), pltpu.VMEM((t + _CONF_HALO, c), _F32),
                        pltpu.VMEM((8, c), _F32), pltpu.VMEM((8, c), _F32), pltpu.VMEM((_CONF_HALO, c), _F32),
                        pltpu.VMEM((7, t + _CONF_HALO, c), _F32), pltpu.VMEM((d, 4 * c), _F32)],
        sem=("arbitrary",), name=name, comm=comm, vmem=_VMEM_LIMIT_FUSED)


_ATT_T = 512


def _attn_probs(qh, kh, scale):
    sc = _dot(qh, kh, _NT) * scale
    e = jnp.exp(sc - jnp.max(sc, axis=-1, keepdims=True))
    return e / jnp.sum(e, axis=-1, keepdims=True)


def _const_spec(arr):
    return pl.BlockSpec(arr.shape, lambda i: (0,) * arr.ndim, pipeline_mode=pl.Buffered(1))


def _attn_fwd(h2, x1, w_q, kv, w_o, gnorm, name):
    s, d = h2.shape
    nm = kv.shape[0]
    hd = d // _XA_HEADS
    t = _tile(s, _ATT_T)
    scale = hd ** -0.5

    def body(h_ref, x1_ref, wq_ref, k_ref, v_ref, wo_ref, g_ref, q_ref, o_ref, x2_ref, h3_ref):
        q_ref[...] = _dot(h_ref[...], wq_ref[...], _NN).astype(q_ref.dtype)
        for hh in range(_XA_HEADS):
            sl = slice(hh * hd, (hh + 1) * hd)
            p = _attn_probs(q_ref[:, sl], k_ref[:, sl], scale)
            o_ref[:, sl] = _dot(p, v_ref[:, sl], _NN).astype(o_ref.dtype)
        y = _dot(o_ref[...], wo_ref[...], _NN)
        _epi_residual_rms(y, (x1_ref, g_ref), (x2_ref, h3_ref), None)

    row = pl.BlockSpec((t, d), lambda i: (i, 0))
    half = pl.BlockSpec((nm, d), lambda i: (0, 0), pipeline_mode=pl.Buffered(1))
    half2 = pl.BlockSpec((nm, d), lambda i: (0, 1), pipeline_mode=pl.Buffered(1))
    return pl.pallas_call(
        body, grid=(s // t,),
        in_specs=[row, row, _const_spec(w_q), half, half2, _const_spec(w_o), _const_spec(gnorm)],
        out_specs=[row, row, row, row],
        out_shape=[jax.ShapeDtypeStruct((s, d), _MXU), jax.ShapeDtypeStruct((s, d), _MXU),
                   jax.ShapeDtypeStruct((s, d), _F32), jax.ShapeDtypeStruct((s, d), _MXU)],
        compiler_params=_cparams("parallel"), name=name,
    )(h2, x1, w_q, kv, kv, w_o, gnorm)


def _attn_bwd(dx2b, dx2, q, o, h2, x1, kv, w_o, w_q, gnorm, name, comm=None):
    s, d = q.shape
    nm = kv.shape[0]
    hd = d // _XA_HEADS
    t = _tile(s, _ATT_T)
    nt = s // t
    scale = hd ** -0.5

    def body(dxb_ref, dx2_ref, q_ref, o_ref, h_ref, x1_ref, k_ref, v_ref, wo_ref, wq_ref, g_ref,
             dx1_ref, dx1b_ref, dgn_ref, dk_ref, dv_ref, pwo_ref, pwq_ref, dq_ref, awo_ref, awq_ref):
        i = pl.program_id(0)
        first = i == 0

        @pl.when(first)
        def _():
            for ref in (dk_ref, dv_ref, awo_ref, awq_ref):
                ref[...] = jnp.zeros_like(ref)

        dxb = dxb_ref[...]
        do = _dot(dxb, wo_ref[...], _NT).astype(_MXU)
        awo_ref[...] += _dot(o_ref[...], dxb, _TN)
        for hh in range(_XA_HEADS):
            sl = slice(hh * hd, (hh + 1) * hd)
            qh = q_ref[:, sl]
            kh = k_ref[:, sl]
            doh = do[:, sl]
            p = _attn_probs(qh, kh, scale)
            dp = _dot(doh, v_ref[:, sl], _NT)
            dv_ref[:, sl] += _dot(p, doh, _TN)
            ds = (p * (dp - jnp.sum(dp * p, axis=-1, keepdims=True)) * scale).astype(_MXU)
            dq_ref[:, sl] = _dot(ds, kh, _NN).astype(dq_ref.dtype)
            dk_ref[:, sl] += _dot(ds, qh, _TN)
        dq = dq_ref[...]
        awq_ref[...] += _dot(h_ref[...], dq, _TN)
        dh = _dot(dq, wq_ref[...], _NT)
        _epi_rms_bwd(dh, (x1_ref, g_ref, dx2_ref), (dx1_ref, dx1b_ref, dgn_ref), first)

        @pl.when(i == nt - 1)
        def _():
            pwo_ref[...] = awo_ref[...].astype(pwo_ref.dtype)
            pwq_ref[...] = awq_ref[...].astype(pwq_ref.dtype)

    row = pl.BlockSpec((t, d), lambda i: (i, 0))
    vec = pl.BlockSpec((1, d), lambda i: (0, 0))
    mem_blk = pl.BlockSpec((nm, d), lambda i: (0, 0))
    sq = pl.BlockSpec((d, d), lambda i: (0, 0))
    half = pl.BlockSpec((nm, d), lambda i: (0, 0), pipeline_mode=pl.Buffered(1))
    half2 = pl.BlockSpec((nm, d), lambda i: (0, 1), pipeline_mode=pl.Buffered(1))
    return _pcall(
        body, args=(dx2b, dx2, q, o, h2, x1, kv, kv, w_o, w_q, gnorm), grid=(nt,),
        in_specs=[row, row, row, row, row, row, half, half2, _const_spec(w_o), _const_spec(w_q), _const_spec(gnorm)],
        out_specs=[row, row, vec, mem_blk, mem_blk, sq, sq],
        out_shape=[jax.ShapeDtypeStruct((s, d), _F32), jax.ShapeDtypeStruct((s, d), _MXU),
                   jax.ShapeDtypeStruct((1, d), _F32), jax.ShapeDtypeStruct((nm, d), _F32),
                   jax.ShapeDtypeStruct((nm, d), _F32), jax.ShapeDtypeStruct((d, d), _XFER),
                   jax.ShapeDtypeStruct((d, d), _XFER)],
        scratch_shapes=[pltpu.VMEM((t, d), _MXU), pltpu.VMEM((d, d), _F32), pltpu.VMEM((d, d), _F32)],
        sem=("arbitrary",), name=name, comm=comm, vmem=_VMEM_LIMIT_FUSED)


_FFN_K = 3
_FFN_FUSED_T = 256
_VMEM_LIMIT_FUSED = 58 * 1024 * 1024


def _ffn_fused_fwd(h3, w_up, w_down, fcw, fcb, x2, target, gfin, name):
    s, d = h3.shape
    nblk, _, n = w_up.shape
    half = nblk // 2
    f = half * n
    t = _tile(s, _FFN_FUSED_T)

    def body(h_ref, wup_ref, wdown_ref, w_ref, b_ref, x2_ref, t_ref, g_ref,
             g0_ref, act_ref, gl_ref, udgl_ref, dx_ref, dxb_ref, l_ref, dg_ref, ext0_ref, ext1_ref, halo_ref):
        i = pl.program_id(0)
        first = i == 0
        h = h_ref[...]
        total = None
        ahead = (_dot(h, wup_ref[0], _NN), _dot(h, wup_ref[half], _NN))
        for j in range(half):
            cs = slice(j * n, (j + 1) * n)
            ext_ref = ext0_ref if j % 2 == 0 else ext1_ref
            g0, u = ahead
            if j + 1 < half:
                ahead = (_dot(h, wup_ref[j + 1], _NN), _dot(h, wup_ref[half + j + 1], _NN))
            if j > 0:
                prev = slice((j - 1) * n, j * n)
                p = _dot(act_ref[:, prev], wdown_ref[prev, :], _NN)
                total = p if total is None else total + p
            g0_ref[:, cs] = g0.astype(g0_ref.dtype)
            ext_ref[0:8, :] = jnp.where(first, 0.0, halo_ref[:, cs])
            ext_ref[8:8 + t, :] = g0
            halo_ref[:, cs] = g0[t - 8:t, :]
            g = _causal_conv(ext_ref, 8, w_ref.at[:, cs], b_ref.at[:, cs], _FFN_K, t)
            gl, dgl = _gelu_and_grad(g)
            gl_ref[:, cs] = gl.astype(gl_ref.dtype)
            udgl_ref[:, cs] = (u * dgl).astype(udgl_ref.dtype)
            act_ref[:, cs] = (gl * u).astype(act_ref.dtype)
        last = slice((half - 1) * n, half * n)
        total = total + _dot(act_ref[:, last], wdown_ref[last, :], _NN)
        _epi_final(total, (x2_ref, t_ref, g_ref), (dx_ref, dxb_ref, l_ref, dg_ref), first)

    def const(shape):
        return pl.BlockSpec(shape, lambda i: (0,) * len(shape), pipeline_mode=pl.Buffered(1))

    row = pl.BlockSpec((t, d), lambda i: (i, 0))
    vec = pl.BlockSpec((1, d), lambda i: (0, 0))
    return pl.pallas_call(
        body, grid=(s // t,),
        in_specs=[row, const(w_up.shape), const(w_down.shape), const(fcw.shape), const(fcb.shape), row, row, vec],
        out_specs=[pl.BlockSpec((t, f), lambda i: (i, 0))] * 4 + [row, row, vec, vec],
        out_shape=[jax.ShapeDtypeStruct((s, f), _MXU)] * 4
        + [jax.ShapeDtypeStruct((s, d), _F32), jax.ShapeDtypeStruct((s, d), _MXU),
                   jax.ShapeDtypeStruct((1, d), _F32), jax.ShapeDtypeStruct((1, d), _F32)],
        scratch_shapes=[pltpu.VMEM((t + 8, n), _F32), pltpu.VMEM((t + 8, n), _F32), pltpu.VMEM((8, f), _F32)],
        compiler_params=_cparams("arbitrary", vmem=_VMEM_LIMIT_FUSED), name=name,
    )(h3, w_up, w_down, fcw, fcb, x2, target, gfin)


def _ffn_fused_bwd(dx3b, g0, gl, udgl, w_down, w_up, fcw, x2, gnorm, dx3, name, comm=None):
    s, d = x2.shape
    nblk, _, n = w_up.shape
    half = nblk // 2
    f = half * n
    t = _tile(s, _FFN_FUSED_T)
    nt = s // t
    hrows = 16

    def body(dxb_ref, g0_ref, g0h_ref, gl_ref, udgl_ref, wdown_ref, wup_ref, w_ref, x2_ref, g_ref, dx3_ref,
             dgu_ref, dx2_ref, dx2b_ref, dgn_ref, dw_ref, db_ref, ext0_ref, ext1_ref, up0_ref, up1_ref, car_ref):
        i = pl.program_id(0)
        first_tile = i == nt - 1
        last_tile = i == 0

        @pl.when(last_tile)
        def _():
            dw_ref[...] = jnp.zeros_like(dw_ref)
            db_ref[...] = jnp.zeros_like(db_ref)
            car_ref[...] = jnp.zeros_like(car_ref)

        dxb = dxb_ref[...]
        total = None
        ahead = _dot(dxb, wdown_ref[0:n, :], _NT)
        for j in range(half):
            cs = slice(j * n, (j + 1) * n)
            us = slice(f + j * n, f + (j + 1) * n)
            ext_ref = ext0_ref if j % 2 == 0 else ext1_ref
            up_ref = up0_ref if j % 2 == 0 else up1_ref
            dact = ahead
            if j + 1 < half:
                ahead = _dot(dxb, wdown_ref[(j + 1) * n:(j + 2) * n, :], _NT)
            ext_ref[0:8, :] = jnp.where(first_tile, 0.0, g0h_ref[:, cs].astype(_F32)[hrows - 8:hrows])
            ext_ref[8:8 + t, :] = g0_ref[:, cs].astype(_F32)
            du = (dact * gl_ref[:, cs].astype(_F32)).astype(dgu_ref.dtype)
            dgu_ref[:, us] = du
            dg = dact * udgl_ref[:, cs].astype(_F32)
            db_ref[:, cs] += _colsum(dg)
            for k in range(_FFN_K):
                dw_ref[k:k + 1, cs] += _colsum(dg * ext_ref[pl.ds(8 - (_FFN_K - 1) + k, t), :])
            up_ref[0:t, :] = dg
            up_ref[t:t + 8, :] = car_ref[:, cs]
            car_ref[:, cs] = dg[0:8, :]
            dg0 = w_ref[0:1, cs] * up_ref[pl.ds(_FFN_K - 1, t), :]
            for k in range(1, _FFN_K):
                dg0 = dg0 + w_ref[k:k + 1, cs] * up_ref[pl.ds(_FFN_K - 1 - k, t), :]
            dg0 = dg0.astype(dgu_ref.dtype)
            dgu_ref[:, cs] = dg0
            p = _dot(dg0, wup_ref[j], _NT) + _dot(du, wup_ref[half + j], _NT)
            total = p if total is None else total + p
        _epi_rms_bwd(total, (x2_ref, g_ref, dx3_ref), (dx2_ref, dx2b_ref, dgn_ref), last_tile)

    def const(shape):
        return pl.BlockSpec(shape, lambda i: (0,) * len(shape), pipeline_mode=pl.Buffered(1))

    row = pl.BlockSpec((t, d), lambda i: (nt - 1 - i, 0))
    vec = pl.BlockSpec((1, d), lambda i: (0, 0))
    per = t // hrows
    wide = pl.BlockSpec((t, f), lambda i: (nt - 1 - i, 0))
    return _pcall(
        body, args=(dx3b, g0, g0, gl, udgl, w_down, w_up, fcw, x2, gnorm, dx3), grid=(nt,),
        in_specs=[row, wide, pl.BlockSpec((hrows, f), lambda i: (jnp.maximum((nt - 1 - i) * per - 1, 0), 0)),
                  wide, wide, const(w_down.shape), const(w_up.shape), const(fcw.shape), row, vec, row],
        out_specs=[pl.BlockSpec((t, 2 * f), lambda i: (nt - 1 - i, 0)), row, row, vec,
                   pl.BlockSpec((_FFN_K, f), lambda i: (0, 0)), pl.BlockSpec((1, f), lambda i: (0, 0))],
        out_shape=[jax.ShapeDtypeStruct((s, 2 * f), _MXU), jax.ShapeDtypeStruct((s, d), _F32),
                   jax.ShapeDtypeStruct((s, d), _MXU), jax.ShapeDtypeStruct((1, d), _F32),
                   jax.ShapeDtypeStruct((_FFN_K, f), _F32), jax.ShapeDtypeStruct((1, f), _F32)],
        scratch_shapes=[pltpu.VMEM((t + 8, n), _F32), pltpu.VMEM((t + 8, n), _F32),
                        pltpu.VMEM((t + 8, n), _F32), pltpu.VMEM((t + 8, n), _F32), pltpu.VMEM((8, f), _F32)],
        sem=("arbitrary",), name=name, comm=comm, vmem=_VMEM_LIMIT_FUSED)


def _mesh_pos():
    return lax.axis_index("x"), lax.axis_index("y"), lax.axis_index("c")


def _flip(v, bit):
    return 1 - v if bit else v


def _sem_scratch(n):
    return [pltpu.SemaphoreType.DMA((7 * n,)), pltpu.SemaphoreType.DMA((7 * n,)), pltpu.SemaphoreType.DMA((n,))]


class _Gather:
    def __init__(self, xs):
        self.ins = list(xs)
        self.outs = [jax.ShapeDtypeStruct((_NDEV,) + v.shape, v.dtype) for v in xs]
        self.scratch = _sem_scratch(len(xs))

    def _plan(self, x_refs, out_refs, sems):
        send_sems, recv_sems, local_sems = sems
        x, y, c = _mesh_pos()
        me, sibling = (x, y, c), (x, y, 1 - c)
        chips = [(1 - x, y), (x, 1 - y), (1 - x, 1 - y)]

        def copy(a, k, block, to, src=None):
            slot = out_refs[a].at[4 * block[0] + 2 * block[1] + block[2]]
            return pltpu.make_async_remote_copy(
                src_ref=slot if src is None else src, dst_ref=slot,
                send_sem=send_sems.at[a * 7 + k], recv_sem=recv_sems.at[a * 7 + k],
                device_id=to, device_id_type=_MESH_ID)

        def own(a):
            return pltpu.make_async_copy(x_refs[a], out_refs[a].at[4 * x + 2 * y + c], local_sems.at[a])

        def first(a):
            return [copy(a, 0, me, sibling, src=x_refs[a])] + [
                copy(a, 1 + j, me, (*chip, c), src=x_refs[a]) for j, chip in enumerate(chips)]

        return me, sibling, chips, c, copy, own, first

    def start(self, x_refs, out_refs, sems):
        _, _, _, _, _, own, first = self._plan(x_refs, out_refs, sems)
        for a in range(len(self.ins)):
            own(a).start()
            for cp in first(a):
                cp.start()

    def finish(self, x_refs, out_refs, sems):
        me, sibling, chips, c, copy, own, first = self._plan(x_refs, out_refs, sems)
        n = len(self.ins)
        passed = []
        for a in range(n):
            for j, chip in enumerate(chips):
                copy(a, 1 + j, (*chip, c), me).wait_recv()
                fwd = copy(a, 4 + j, (*chip, c), sibling)
                fwd.start()
                passed.append(fwd)
        for a in range(n):
            copy(a, 0, sibling, me).wait_recv()
            for j, chip in enumerate(chips):
                copy(a, 4 + j, (*chip, 1 - c), me).wait_recv()
        for a in range(n):
            for cp in first(a):
                cp.wait_send()
        for cp in passed:
            cp.wait_send()
        for a in range(n):
            own(a).wait()


class _Exchange:
    def __init__(self, gs):
        self.ins = list(gs)
        self.outs = [jax.ShapeDtypeStruct(v.shape, v.dtype) for v in gs]
        self.scratch = _sem_scratch(len(gs))

    def _plan(self, g_refs, r_refs, sems):
        send_sems, recv_sems, local_sems = sems
        x, y, c = _mesh_pos()
        me_idx = 4 * x + 2 * y + c
        n = len(self.ins)

        def copy(a, k):
            peer = (_flip(x, k & 4), _flip(y, k & 2), _flip(c, k & 1))
            peer_idx = 4 * peer[0] + 2 * peer[1] + peer[2]
            return pltpu.make_async_remote_copy(
                src_ref=g_refs[a].at[peer_idx], dst_ref=r_refs[a].at[me_idx],
                send_sem=send_sems.at[a * 7 + k - 1], recv_sem=recv_sems.at[a * 7 + k - 1],
                device_id=peer, device_id_type=_MESH_ID)

        copies = [copy(a, k) for a in range(n) for k in range(1, _NDEV)]
        mine = [pltpu.make_async_copy(g_refs[a].at[me_idx], r_refs[a].at[me_idx], local_sems.at[a])
                for a in range(n)]
        return copies, mine

    def start(self, g_refs, r_refs, sems):
        copies, mine = self._plan(g_refs, r_refs, sems)
        for cp in copies + mine:
            cp.start()

    def finish(self, g_refs, r_refs, sems):
        copies, mine = self._plan(g_refs, r_refs, sems)
        for cp in copies:
            cp.wait_recv()
        for cp in copies:
            cp.wait_send()
        for cp in mine:
            cp.wait()


class _Both:
    def __init__(self, first, second):
        self.parts = (first, second)
        self.ins = first.ins + second.ins
        self.outs = first.outs + second.outs
        self.scratch = first.scratch + second.scratch

    def _split(self, ins, outs, sems):
        a, b = self.parts
        na, nb = len(a.ins), len(a.scratch)
        return (a, ins[:na], outs[:na], sems[:nb]), (b, ins[na:], outs[na:], sems[nb:])

    def start(self, ins, outs, sems):
        for part, i, o, s in self._split(ins, outs, sems):
            part.start(i, o, s)

    def finish(self, ins, outs, sems):
        for part, i, o, s in self._split(ins, outs, sems):
            part.finish(i, o, s)


def _comm_call(comm, name):
    def body(*refs):
        n_i, n_o = len(comm.ins), len(comm.outs)
        ins, outs, sems = refs[:n_i], refs[n_i:n_i + n_o], refs[n_i + n_o:]
        comm.start(ins, outs, sems)
        comm.finish(ins, outs, sems)

    return pl.pallas_call(
        body, out_shape=list(comm.outs), in_specs=[_ANY] * len(comm.ins), out_specs=[_ANY] * len(comm.outs),
        scratch_shapes=list(comm.scratch), name=name)(*comm.ins)


def _adamw_math(w, g, m, v):
    m = _ADAM_B1 * m + (1.0 - _ADAM_B1) * g
    v = _ADAM_B2 * v + (1.0 - _ADAM_B2) * (g * g)
    m_hat = m / (1.0 - _ADAM_B1 ** _ADAM_STEP)
    v_hat = v / (1.0 - _ADAM_B2 ** _ADAM_STEP)
    delta = -_ADAM_LR * (m_hat / (jnp.sqrt(v_hat) + _ADAM_EPS) + _ADAM_WD * w)
    return delta, m, v


def _sum_adamw(parts, w, m, v, name):
    r, c = w.shape
    tr = _tile(r, 128)

    def body(p_ref, w_ref, m_ref, v_ref, g_ref, d_ref, nm_ref, nv_ref):
        g = p_ref[0].astype(_F32)
        for j in range(1, _NDEV):
            g = g + p_ref[j].astype(_F32)
        delta, nm, nv = _adamw_math(w_ref[...], g, m_ref[...], v_ref[...])
        g_ref[...] = g
        d_ref[...] = delta
        nm_ref[...] = nm
        nv_ref[...] = nv

    blk = pl.BlockSpec((tr, c), lambda i: (i, 0))
    return pl.pallas_call(
        body, grid=(r // tr,),
        in_specs=[pl.BlockSpec((_NDEV, tr, c), lambda i: (0, i, 0)), blk, blk, blk],
        out_specs=[blk] * 4, out_shape=[jax.ShapeDtypeStruct((r, c), _F32)] * 4,
        compiler_params=_cparams("parallel"), name=name,
    )(parts, w, m, v)


def _sum8(parts, name):
    _, r, c = parts.shape

    def body(p_ref, o_ref):
        g = p_ref[0]
        for j in range(1, _NDEV):
            g = g + p_ref[j]
        o_ref[...] = g

    return pl.pallas_call(
        body, grid=(1,), in_specs=[pl.BlockSpec((_NDEV, r, c), lambda i: (0, 0, 0))],
        out_specs=pl.BlockSpec((r, c), lambda i: (0, 0)), out_shape=jax.ShapeDtypeStruct((r, c), _F32),
        compiler_params=_cparams("arbitrary"), name=name,
    )(parts)


def _adamw_many(gs, ws, ms, vs, name):
    n = len(ws)

    def body(*refs):
        g_refs, w_refs, m_refs, v_refs = (refs[k * n:(k + 1) * n] for k in range(4))
        d_refs, nm_refs, nv_refs = (refs[(4 + k) * n:(5 + k) * n] for k in range(3))
        for k in range(n):
            delta, nm, nv = _adamw_math(w_refs[k][...], g_refs[k][...], m_refs[k][...], v_refs[k][...])
            d_refs[k][...] = delta
            nm_refs[k][...] = nm
            nv_refs[k][...] = nv

    def whole(arr):
        return pl.BlockSpec(arr.shape, lambda i, nd=arr.ndim: (0,) * nd)

    specs = [whole(w) for w in ws]
    res = pl.pallas_call(
        body, grid=(1,), in_specs=specs * 4, out_specs=specs * 3,
        out_shape=[jax.ShapeDtypeStruct(w.shape, _F32) for w in ws] * 3,
        compiler_params=_cparams("arbitrary"), name=name,
    )(*gs, *ws, *ms, *vs)
    return res[:n], res[n:2 * n], res[2 * n:]


def _pack(arrs):
    flat = jnp.concatenate([a.reshape(-1).astype(_F32) for a in arrs])
    pad = (-flat.shape[0]) % 1024
    return jnp.pad(flat, (0, pad)).reshape(-1, 128)


def _unpack(flat2d, shapes):
    flat = flat2d.reshape(-1)
    out, off = [], 0
    for sh in shapes:
        size = 1
        for dim in sh:
            size *= dim
        out.append(flat[off:off + size].reshape(sh))
        off += size
    return out


def _block_diag(w):
    h, hd, _ = w.shape
    eye = jnp.eye(h, dtype=w.dtype)
    return (eye[:, None, :, None] * w[:, :, None, :]).reshape(h * hd, h * hd)


def _diag_blocks(full, h):
    hd = full.shape[0] // h
    return jnp.stack([full[i * hd:(i + 1) * hd, i * hd:(i + 1) * hd] for i in range(h)])


def kernel(x, mem, mix_norm_g, w_in, lru_conv_w, lru_conv_b, lru_w_a, lru_b_a, lru_w_x, lru_b_x, lru_lambda, conf_conv_w, conf_conv_b, conf_ln_g, conf_ln_b, w_out, xa_norm_g, mem_norm_g, w_q, w_kv, w_o, ffn_norm_g, w_up, ffn_conv_w, ffn_conv_b, w_down, final_norm_g, loss_target, m_mix_norm_g, m_w_in, m_lru_conv_w, m_lru_conv_b, m_lru_w_a, m_lru_b_a, m_lru_w_x, m_lru_b_x, m_lru_lambda, m_conf_conv_w, m_conf_conv_b, m_conf_ln_g, m_conf_ln_b, m_w_out, m_xa_norm_g, m_mem_norm_g, m_w_q, m_w_kv, m_w_o, m_ffn_norm_g, m_w_up, m_ffn_conv_w, m_ffn_conv_b, m_w_down, m_final_norm_g, v_mix_norm_g, v_w_in, v_lru_conv_w, v_lru_conv_b, v_lru_w_a, v_lru_b_a, v_lru_w_x, v_lru_b_x, v_lru_lambda, v_conf_conv_w, v_conf_conv_b, v_conf_ln_g, v_conf_ln_b, v_w_out, v_xa_norm_g, v_mem_norm_g, v_w_q, v_w_kv, v_w_o, v_ffn_norm_g, v_w_up, v_ffn_conv_w, v_ffn_conv_b, v_w_down, v_final_norm_g):
    names = ["mix_norm_g", "w_in", "lru_conv_w", "lru_conv_b", "lru_w_a", "lru_b_a", "lru_w_x", "lru_b_x",
             "lru_lambda", "conf_conv_w", "conf_conv_b", "conf_ln_g", "conf_ln_b", "w_out", "xa_norm_g",
             "mem_norm_g", "w_q", "w_kv", "w_o", "ffn_norm_g", "w_up", "ffn_conv_w", "ffn_conv_b", "w_down",
             "final_norm_g"]
    loc = locals()
    W = {n: loc[n] for n in names}
    M = {n: loc["m_" + n] for n in names}
    V = {n: loc["v_" + n] for n in names}
    big = ["w_in", "w_out", "w_q", "w_kv", "w_o", "w_up", "w_down"]
    conv_sharded = ["lru_conv_w", "conf_conv_w", "ffn_conv_w"]

    xs = x[0]
    mems = mem[0]
    tgt = loss_target[0]
    me = 4 * lax.axis_index("x") + 2 * lax.axis_index("y") + lax.axis_index("c")

    conv_shapes = [W[n].shape[1:] for n in conv_sharded]
    conv_pack = _pack([W[n][0] for n in conv_sharded])
    shard = {n: W[n][0].astype(_XFER) for n in big}
    h1, (g_in, g_out, g_conv) = _rms_fwd(
        xs, mix_norm_g, "rms1_fwd", comm=_Gather([shard["w_in"], shard["w_out"], conv_pack]))
    convs = [[] for _ in conv_sharded]
    for j in range(_NDEV):
        for idx, part in enumerate(_unpack(g_conv[j], conv_shapes)):
            convs[idx].append(part)
    lcw, ccw, fcw = [jnp.concatenate(parts, axis=-1) for parts in convs]

    wab = jnp.concatenate([_block_diag(lru_w_a[0]), _block_diag(lru_w_x[0])], axis=1).astype(_MXU)
    mixer_params = (lcw, lru_conv_b, wab, lru_b_a, lru_b_x, lru_lambda, ccw, conf_conv_b, conf_ln_g, conf_ln_b)

    w_out_f = g_out.reshape(-1, g_out.shape[-1])
    (z, ycat, hs, cc, x1, h2), (g_q, g_kv, g_o, g_up, g_down) = _mixer_fwd(
        xs, h1, g_in, w_out_f, xa_norm_g, *mixer_params, "mixer_fwd",
        comm=_Gather([shard[n] for n in ("w_q", "w_kv", "w_o", "w_up", "w_down")]))
    w_q_f = g_q.reshape(-1, g_q.shape[-1])
    w_o_f = g_o.reshape(-1, g_o.shape[-1])
    w_down_f = g_down.reshape(-1, g_down.shape[-1])
    row32, row16, vec32 = (_F32, "row"), (_MXU, "row"), (_F32, "vec")
    mn = _rms_fwd(mems, mem_norm_g, "rmsm_fwd")
    kv = _mm_nn_stacked(mn, g_kv, _MXU, "mm_kv_fwd")
    q, o, x2, h3 = _attn_fwd(h2, x1, w_q_f, kv, w_o_f, ffn_norm_g, "attn_fwd")

    gfin = final_norm_g.reshape(1, -1)
    g0, act, gelu_g, u_dgelu, dx3, dx3b, lvec, dg_final = _ffn_fused_fwd(
        h3, g_up, w_down_f, fcw, ffn_conv_b, x2, tgt, gfin, "ffn_fwd")

    def rows8(p):
        return p.reshape(_NDEV, p.shape[0] // _NDEV, p.shape[1])

    p_down = _mm_tn_nat(act, dx3b, _XFER, "mm_down_wgrad", ts=2048)
    (dgu, dx2, dx2b, dg_ffn, dfcw, dfcb), (r_down,) = _ffn_fused_bwd(
        dx3b, g0, gelu_g, u_dgelu, w_down_f, g_up, fcw, x2, ffn_norm_g, dx3, "ffn_bwd",
        comm=_Exchange([rows8(p_down)]))
    p_up = _mm_tn_stacked(h3, dgu, _NDEV, _XFER, "mm_up_wgrad", slabs=2)

    (dx1, dx1b, dg_xa, dk, dv, p_o, p_q), (r_up,) = _attn_bwd(
        dx2b, dx2, q, o, h2, x1, kv, w_o_f, w_q_f, xa_norm_g, "attn_bwd", comm=_Exchange([p_up]))
    dkv = jnp.concatenate([dk, dv], axis=1).astype(_MXU)
    dmn = _mm_nt_stacked(dkv, g_kv, "mm_kv_dgrad", outs=[row32], slabs=_NDEV)
    p_kv = _mm_tn_stacked(mn, dkv, _NDEV, _XFER, "mm_kv_wgrad", slabs=_NDEV)
    _, _, dg_mem = _rms_bwd(dmn, mems, mem_norm_g, None, "rmsm_bwd")

    p_out = _mm_tn_nat(ycat, dx1b, _XFER, "mm_out_wgrad", ts=2048)
    ((dz, p_in, dlcw, dlcb, dwab, dba, dbx, dlam, dccw, dccb, dlng, dlnb),
     (r_o, r_q, r_kv, r_out)) = _mixer_bwd(
        dx1b, w_out_f, h1, z, hs, cc, *mixer_params, "mixer_bwd",
        comm=_Exchange([rows8(p_o), rows8(p_q), p_kv, rows8(p_out)]))

    c = _D_LRU
    heads = lru_w_a.shape[1]
    small_partial = {
        "lru_conv_w": dlcw, "lru_conv_b": dlcb,
        "lru_w_a": _diag_blocks(dwab[:, :c], heads), "lru_b_a": dba,
        "lru_w_x": _diag_blocks(dwab[:, c:], heads), "lru_b_x": dbx, "lru_lambda": dlam,
        "conf_conv_w": dccw, "conf_conv_b": dccb, "conf_ln_g": dlng, "conf_ln_b": dlnb,
        "xa_norm_g": dg_xa, "mem_norm_g": dg_mem, "ffn_norm_g": dg_ffn,
        "ffn_conv_w": dfcw, "ffn_conv_b": dfcb, "final_norm_g": dg_final,
    }
    early = list(small_partial)
    early_shapes = [small_partial[n].shape for n in early] + [lvec.shape]
    (grad_x, dg_mix), (r_in, early_all) = _mm_nt_stacked(
        dz, g_in, "mm_in_dgrad", epi=_epi_rms_bwd, extra=[(xs, "row"), (mix_norm_g, "vec"), (dx1, "row")],
        outs=[row32, vec32], tm=512, slabs=_NDEV,
        comm=_Both(_Exchange([p_in]), _Gather([_pack([small_partial[n] for n in early] + [lvec])])))
    (mix_all,) = _comm_call(_Gather([dg_mix]), "gather_mix_grad")
    small = early + ["mix_norm_g"]
    small_sum = _unpack(_sum8(early_all, "sum_small_grads"), early_shapes)
    loss = 0.5 * jnp.sum(small_sum.pop()) / xs.shape[1]
    small_sum.append(_sum8(mix_all.reshape(_NDEV, 8, -1), "sum_mix_grad").reshape(dg_mix.shape))
    received = {"w_in": r_in, "w_out": r_out, "w_q": r_q, "w_kv": r_kv, "w_o": r_o, "w_up": r_up,
                "w_down": r_down}

    grads, deltas, new_m, new_v = {}, {}, {}, {}
    for n, rec in ((n, received[n]) for n in big):
        shp = W[n].shape
        w2, m2, v2 = (t.reshape(shp[1:]) for t in (W[n], M[n], V[n]))
        outs = _sum_adamw(rec, w2, m2, v2, "adamw_" + n)
        grads[n], deltas[n], new_m[n], new_v[n] = (t.reshape(shp) for t in outs)

    small_g = []
    for n, g in zip(small, small_sum):
        if n in conv_sharded:
            width = W[n].shape[-1]
            g = lax.dynamic_slice_in_dim(g, me * width, width, axis=1)
        small_g.append(g.reshape(W[n].shape))

    def at_least_2d(a):
        return a.reshape(1, -1) if a.ndim == 1 else a

    sd, sm, sv = _adamw_many([at_least_2d(g) for g in small_g], [at_least_2d(W[n]) for n in small],
                             [at_least_2d(M[n]) for n in small], [at_least_2d(V[n]) for n in small], "adamw_small")
    for n, g, d_, m_, v_ in zip(small, small_g, sd, sm, sv):
        shp = W[n].shape
        grads[n], deltas[n], new_m[n], new_v[n] = g, d_.reshape(shp), m_.reshape(shp), v_.reshape(shp)

    return (loss, grad_x[None], *[grads[n] for n in names], *[deltas[n] for n in names],
            *[new_m[n] for n in names], *[new_v[n] for n in names])
```

```python
import functools

import jax
import jax.numpy as jnp
from jax import lax
from jax.experimental import pallas as pl
from jax.experimental.pallas import tpu as pltpu

_MXU = jnp.bfloat16
_XFER = jnp.bfloat16
_F32 = jnp.float32
_EPS = 1e-6
_NDEV = 8
_VMEM_LIMIT = 48 * 1024 * 1024

_D_LRU = 512
_XA_HEADS = 4
_RG_C = 8.0
_ADAM_LR, _ADAM_B1, _ADAM_B2, _ADAM_EPS, _ADAM_WD, _ADAM_STEP = 0.001, 0.9, 0.999, 1e-08, 0.01, 10

_MESH_ID = pl.DeviceIdType.MESH
_ANY = pl.BlockSpec(memory_space=pl.ANY)


def _cparams(*sem, vmem=_VMEM_LIMIT):
    return pltpu.CompilerParams(dimension_semantics=tuple(sem), vmem_limit_bytes=vmem)


def _pcall(body, *, args, grid, in_specs, out_specs, out_shape, sem, name, scratch_shapes=(), comm=None,
           vmem=_VMEM_LIMIT):
    outs_l = list(out_shape) if isinstance(out_shape, (list, tuple)) else [out_shape]
    ospecs_l = list(out_specs) if isinstance(out_specs, (list, tuple)) else [out_specs]
    n_in, n_out, n_scr = len(args), len(outs_l), len(scratch_shapes)
    if comm is None:
        res = pl.pallas_call(
            body, grid=grid, in_specs=list(in_specs), out_specs=ospecs_l, out_shape=outs_l,
            scratch_shapes=list(scratch_shapes), compiler_params=_cparams(*sem, vmem=vmem), name=name)(*args)
        return list(res), []
    n_ci, n_co = len(comm.ins), len(comm.outs)

    def wrapped(*refs):
        ins, cins = refs[:n_in], refs[n_in:n_in + n_ci]
        o = n_in + n_ci
        outs, couts = refs[o:o + n_out], refs[o + n_out:o + n_out + n_co]
        s = o + n_out + n_co
        scr, cscr = refs[s:s + n_scr], refs[s + n_scr:]
        first = pl.program_id(0) == 0
        last = pl.program_id(0) == grid[0] - 1
        for ax in range(1, len(grid)):
            first = jnp.logical_and(first, pl.program_id(ax) == 0)
            last = jnp.logical_and(last, pl.program_id(ax) == grid[ax] - 1)

        @pl.when(first)
        def _():
            comm.start(cins, couts, cscr)

        body(*ins, *outs, *scr)

        @pl.when(last)
        def _():
            comm.finish(cins, couts, cscr)

    res = pl.pallas_call(
        wrapped, grid=grid, in_specs=list(in_specs) + [_ANY] * n_ci, out_specs=ospecs_l + [_ANY] * n_co,
        out_shape=outs_l + list(comm.outs), scratch_shapes=list(scratch_shapes) + list(comm.scratch),
        compiler_params=_cparams(*(("arbitrary",) * len(grid)), vmem=vmem), name=name)(*args, *comm.ins)
    return list(res[:n_out]), list(res[n_out:])


def _sigmoid(v):
    return 1.0 / (1.0 + jnp.exp(-v))


_GELU_C = 0.7978845608028654
_GELU_K = 0.044715


def _gelu(v):
    t = jnp.tanh(_GELU_C * (v + _GELU_K * v * v * v))
    return 0.5 * v * (1.0 + t)


def _gelu_and_grad(v):
    v2 = v * v
    s = 0.5 * jnp.tanh(v * (_GELU_C + (_GELU_C * _GELU_K) * v2)) + 0.5
    g = v * s
    dg = s + (g * (1.0 - s)) * ((2.0 * _GELU_C) + (6.0 * _GELU_C * _GELU_K) * v2)
    return g, dg


def _softplus(v):
    e = jnp.exp(-jnp.abs(v))
    log1p = jnp.where(e < 1e-2, e * (1.0 - e * (0.5 - e * (1.0 / 3.0))), jnp.log(1.0 + e))
    return jnp.maximum(v, 0.0) + log1p


def _neg_expm1(v):
    series = -v * (1.0 + v * (0.5 + v * ((1.0 / 6.0) + v * (1.0 / 24.0))))
    return jnp.where(v > -0.0625, series, 1.0 - jnp.exp(v))


def _dot(a, b, dims):
    return lax.dot_general(a.astype(_MXU), b.astype(_MXU), (dims, ((), ())), preferred_element_type=_F32)


_NN = ((1,), (0,))
_NT = ((1,), (1,))
_TN = ((0,), (0,))


def _scan_fwd(a, b, rows):
    n = a.shape[0]
    d = 1
    while d < n:
        keep = rows >= d
        b = jnp.where(keep, b + a * pltpu.roll(b, d, 0), b)
        a = jnp.where(keep, a * pltpu.roll(a, d, 0), a)
        d *= 2
    return a, b


def _scan_rev(a, b, rows):
    n = a.shape[0]
    d = 1
    while d < n:
        keep = rows < n - d
        b = jnp.where(keep, b + a * pltpu.roll(b, n - d, 0), b)
        a = jnp.where(keep, a * pltpu.roll(a, n - d, 0), a)
        d *= 2
    return a, b


def _colsum(v):
    return jnp.sum(v, axis=0, keepdims=True)


def _mm(a, b, *, dims, grid, a_spec, b_spec, outs, acc_shape, name, extra=(), epi=None, slabs=1, comm=None):
    nred = grid[-1]
    red_axis = len(grid) - 1
    n_ex, n_out = len(extra), len(outs)
    epi = _epi_store if epi is None else epi

    def body(*refs):
        a_ref, b_ref = refs[:2]
        ex, o_refs, acc_ref = refs[2:2 + n_ex], refs[2 + n_ex:2 + n_ex + n_out], refs[-1]
        if slabs == 1:
            p = _dot(a_ref[...], b_ref[...], dims)
        else:
            n = b_ref.shape[-1]
            p = _dot(a_ref[:, 0:n], b_ref[0], dims)
            for jj in range(1, slabs):
                p = p + _dot(a_ref[:, jj * n:(jj + 1) * n], b_ref[jj], dims)

        first_rows = pl.program_id(0) == 0
        if nred == 1:
            epi(p, ex, o_refs, first_rows)
        else:
            k = pl.program_id(red_axis)

            @pl.when(k == 0)
            def _():
                acc_ref[...] = p

            @pl.when(jnp.logical_and(k > 0, k < nred - 1))
            def _():
                acc_ref[...] += p

            @pl.when(k == nred - 1)
            def _():
                epi(acc_ref[...] + p, ex, o_refs, first_rows)

    sem = ("parallel",) * (len(grid) - 1) + ("arbitrary",)
    if any(o[0].shape[0] == 1 for o in outs):
        sem = ("arbitrary",) * len(grid)
    res, cres = _pcall(
        body, args=(a, b) + tuple(e[0] for e in extra), grid=grid,
        in_specs=[a_spec, b_spec] + [e[1] for e in extra],
        out_specs=[o[1] for o in outs], out_shape=[o[0] for o in outs],
        scratch_shapes=[pltpu.VMEM(acc_shape if nred > 1 else (8, 128), _F32)], sem=sem, name=name, comm=comm)
    res = res[0] if n_out == 1 else res
    return res if comm is None else (res, cres)


def _epi_store(total, ex, outs, first_rows):
    outs[0][...] = total.astype(outs[0].dtype)


def _epi_residual_rms(total, ex, outs, first_rows):
    res_ref, g_ref = ex
    xn = total + res_ref[...]
    outs[0][...] = xn
    r = lax.rsqrt(jnp.mean(xn * xn, axis=-1, keepdims=True) + _EPS)
    outs[1][...] = (xn * r * g_ref[...]).astype(outs[1].dtype)


def _epi_rms_bwd(total, ex, outs, first_rows):
    x_ref, g_ref, dres_ref = ex
    dg_ref = outs[-1]
    xv = x_ref[...]
    r = lax.rsqrt(jnp.mean(xv * xv, axis=-1, keepdims=True) + _EPS)
    xhat = xv * r
    dxh = total * g_ref[...]
    dx = dres_ref[...] + r * (dxh - xhat * jnp.mean(dxh * xhat, axis=-1, keepdims=True))
    for o_ref in outs[:-1]:
        o_ref[...] = dx.astype(o_ref.dtype)

    @pl.when(first_rows)
    def _():
        dg_ref[...] = jnp.zeros_like(dg_ref)

    dg_ref[...] += _colsum(total * xhat)


def _epi_final(total, ex, outs, first_rows):
    res_ref, t_ref, g_ref = ex
    dx_ref, dxb_ref, l_ref, dg_ref = outs
    xv = total + res_ref[...]
    gv = g_ref[...]
    d = xv.shape[-1]
    r = lax.rsqrt(jnp.mean(xv * xv, axis=-1, keepdims=True) + _EPS)
    xhat = xv * r
    err = xhat * gv - t_ref[...]
    dy = err * (1.0 / d)
    dxh = dy * gv
    dx = r * (dxh - xhat * jnp.mean(dxh * xhat, axis=-1, keepdims=True))
    dx_ref[...] = dx
    dxb_ref[...] = dx.astype(dxb_ref.dtype)

    @pl.when(first_rows)
    def _():
        l_ref[...] = jnp.zeros_like(l_ref)
        dg_ref[...] = jnp.zeros_like(dg_ref)

    l_ref[...] += _colsum(err * err)
    dg_ref[...] += _colsum(dy * xhat)


def _tile(m, cap):
    t = min(m, cap)
    assert m % t == 0
    return t


def _row_spec(tm, n):
    return pl.BlockSpec((tm, n), lambda i, *_: (i, 0))


def _vec_spec(n):
    return pl.BlockSpec((1, n), lambda *_: (0, 0))


def _row_io(m, n, tm, extra, outs):
    def spec(kind):
        return _row_spec(tm, n) if kind == "row" else _vec_spec(n)

    ex = [(arr, spec(kind)) for arr, kind in extra]
    os_ = [(jax.ShapeDtypeStruct((m, n) if kind == "row" else (1, n), dt), spec(kind)) for dt, kind in outs]
    return ex, os_


def _mm_nn_stacked(a, w, out_dtype, name, comm=None, tm=1024):
    m, k = a.shape
    j, _, n = w.shape
    tm = _tile(m, tm)
    return _mm(a, w, dims=_NN, grid=(m // tm, j, 1),
               a_spec=pl.BlockSpec((tm, k), lambda i, jj, r: (i, 0)),
               b_spec=pl.BlockSpec((None, k, n), lambda i, jj, r: (jj, 0, 0)),
               outs=[(jax.ShapeDtypeStruct((m, j * n), out_dtype), pl.BlockSpec((tm, n), lambda i, jj, r: (i, jj)))],
               acc_shape=(tm, n), name=name, comm=comm)


def _mm_nt_stacked(dc, w, name, *, outs, extra=(), epi=None, comm=None, tm=1024, slabs=1):
    m = dc.shape[0]
    j, k, n = w.shape
    tm = _tile(m, tm)
    assert j % slabs == 0
    ex, os_ = _row_io(m, k, tm, extra, outs)
    wblk = (None, k, n) if slabs == 1 else (slabs, k, n)
    return _mm(dc, w, dims=_NT, grid=(m // tm, j // slabs),
               a_spec=pl.BlockSpec((tm, slabs * n), lambda i, r: (i, r)),
               b_spec=pl.BlockSpec(wblk, lambda i, r: (r, 0, 0)),
               outs=os_, extra=ex, epi=epi, acc_shape=(tm, k), name=name, slabs=slabs, comm=comm)


def _mm_tn_stacked(a, dc, j, out_dtype, name, slabs=1, ts=1024):
    s, k = a.shape
    n = dc.shape[1] // j
    ts = _tile(s, ts)
    assert j % slabs == 0

    def epi(total, ex, outs, first_rows):
        for jj in range(slabs):
            outs[0][jj] = total[:, jj * n:(jj + 1) * n].astype(outs[0].dtype)

    return _mm(a, dc, dims=_TN, grid=(j // slabs, s // ts),
               a_spec=pl.BlockSpec((ts, k), lambda jj, r: (r, 0)),
               b_spec=pl.BlockSpec((ts, slabs * n), lambda jj, r: (r, jj)),
               outs=[(jax.ShapeDtypeStruct((j, k, n), out_dtype),
                      pl.BlockSpec((slabs, k, n), lambda jj, r: (jj, 0, 0)))],
               epi=epi, acc_shape=(k, slabs * n), name=name)


def _mm_tn_nat(a, dc, out_dtype, name, ts=1024):
    s, kt = a.shape
    n = dc.shape[1]
    ts = _tile(s, ts)
    tkb = _tile(kt, 1024)
    return _mm(a, dc, dims=_TN, grid=(kt // tkb, s // ts),
               a_spec=pl.BlockSpec((ts, tkb), lambda kb, r: (r, kb)),
               b_spec=pl.BlockSpec((ts, n), lambda kb, r: (r, 0)),
               outs=[(jax.ShapeDtypeStruct((kt, n), out_dtype), pl.BlockSpec((tkb, n), lambda kb, r: (kb, 0)))],
               acc_shape=(tkb, n), name=name)


def _rms_fwd(x, g, name, comm=None):
    s, d = x.shape
    t = _tile(s, 1024)

    def body(x_ref, g_ref, h_ref):
        xv = x_ref[...]
        r = lax.rsqrt(jnp.mean(xv * xv, axis=-1, keepdims=True) + _EPS)
        h_ref[...] = (xv * r * g_ref[...]).astype(h_ref.dtype)

    res, cres = _pcall(
        body, args=(x, g), grid=(s // t,),
        in_specs=[pl.BlockSpec((t, d), lambda i: (i, 0)), pl.BlockSpec((1, d), lambda i: (0, 0))],
        out_specs=pl.BlockSpec((t, d), lambda i: (i, 0)),
        out_shape=jax.ShapeDtypeStruct((s, d), _MXU), sem=("parallel",), name=name, comm=comm)
    return res[0] if comm is None else (res[0], cres)


def _rms_bwd(dh, x, g, dres, name):
    s, d = x.shape
    t = _tile(s, 256)
    has_res = dres is not None

    def body(*refs):
        if has_res:
            dh_ref, x_ref, g_ref, dres_ref, dx_ref, dxb_ref, dg_ref = refs
        else:
            dh_ref, x_ref, g_ref, dx_ref, dxb_ref, dg_ref = refs
        xv = x_ref[...]
        dhv = dh_ref[...]
        r = lax.rsqrt(jnp.mean(xv * xv, axis=-1, keepdims=True) + _EPS)
        xhat = xv * r
        dxh = dhv * g_ref[...]
        dx = r * (dxh - xhat * jnp.mean(dxh * xhat, axis=-1, keepdims=True))
        if has_res:
            dx = dx + dres_ref[...]
        dx_ref[...] = dx
        dxb_ref[...] = dx.astype(dxb_ref.dtype)

        @pl.when(pl.program_id(0) == 0)
        def _():
            dg_ref[...] = jnp.zeros_like(dg_ref)

        dg_ref[...] += _colsum(dhv * xhat)

    row = pl.BlockSpec((t, d), lambda i: (i, 0))
    vec = pl.BlockSpec((1, d), lambda i: (0, 0))
    in_specs = [row, row, vec] + ([row] if has_res else [])
    args = (dh, x, g) + ((dres,) if has_res else ())
    return pl.pallas_call(
        body, grid=(s // t,), in_specs=in_specs, out_specs=[row, row, vec],
        out_shape=[jax.ShapeDtypeStruct((s, d), _F32), jax.ShapeDtypeStruct((s, d), _MXU),
                   jax.ShapeDtypeStruct((1, d), _F32)],
        compiler_params=_cparams("arbitrary"), name=name,
    )(*args)


_LRU_K = 4
_CONF_K = 31
_LRU_HALO = 8
_CONF_HALO = 32
_MIX_T = 512


def _lru_gates(lx, wab_ref, ba_ref, bx_ref, lam_ref):
    c = _D_LRU
    pre = _dot(lx, wab_ref[...], _NN)
    r = _sigmoid(pre[:, :c] + ba_ref[...])
    ig = _sigmoid(pre[:, c:] + bx_ref[...])
    sp = _softplus(-lam_ref[...])
    log_a = (-_RG_C) * r * sp
    a = jnp.exp(log_a)
    mult = jnp.sqrt(_neg_expm1(2.0 * log_a))
    return r, ig, sp, a, mult


def _causal_conv(ext_ref, halo, w_ref, b_ref, taps, t):
    acc = b_ref[...] + w_ref[0:1, :] * ext_ref[pl.ds(halo - (taps - 1), t), :]
    for k in range(1, taps):
        acc = acc + w_ref[k:k + 1, :] * ext_ref[pl.ds(halo - (taps - 1) + k, t), :]
    return acc


class _Windows:
    def __init__(self, ext_ref, shifted_ref, t):
        self.ext_ref, self.shifted_ref, self.t = ext_ref, shifted_ref, t
        rows = t + 24
        for r in range(1, 8):
            shifted_ref[r - 1, 0:rows, :] = ext_ref[pl.ds(r, rows), :]

    def __call__(self, off):
        q, r = divmod(off, 8)
        if r == 0:
            return self.ext_ref[pl.ds(8 * q, self.t), :]
        return self.shifted_ref[r - 1, pl.ds(8 * q, self.t), :]


def _mixer_fwd(xs, h1, w_in, w_out, gnorm, lcw, lcb, wab, ba, bx, lam, ccw, ccb, lng, lnb, name, comm=None):
    s, d = xs.shape
    nblk, _, n = w_in.shape
    c = _D_LRU
    t = _tile(s, _MIX_T)
    nt = s // t

    def body(x_ref, h_ref, win_ref, wout_ref, gn_ref,
             lcw_ref, lcb_ref, wab_ref, ba_ref, bx_ref, lam_ref, ccw_ref, ccb_ref, lng_ref, lnb_ref,
             z_ref, ycat_ref, hs_ref, cc_ref, x1_ref, h2_ref,
             ext_ref, cge_ref, hc_ref, shifted_ref, zprev_ref):
        i = pl.program_id(0)
        first = i == 0
        rows = lax.broadcasted_iota(jnp.int32, (t, c), 0)

        @pl.when(first)
        def _():
            zprev_ref[...] = jnp.zeros_like(zprev_ref)

        hv = h_ref[...]
        for j in range(nblk):
            z_ref[:, j * n:(j + 1) * n] = _dot(hv, win_ref[j], _NN)
        lx0_ref, gate_ref = z_ref.at[:, 0:c], z_ref.at[:, c:2 * c]
        ca_ref, cb_ref = z_ref.at[:, 2 * c:3 * c], z_ref.at[:, 3 * c:4 * c]
        lx0h_ref = zprev_ref.at[_CONF_HALO - _LRU_HALO:_CONF_HALO, 0:c]
        cah_ref, cbh_ref = zprev_ref.at[:, 2 * c:3 * c], zprev_ref.at[:, 3 * c:4 * c]

        ext_ref[0:_LRU_HALO, :] = jnp.where(first, 0.0, lx0h_ref[...])
        ext_ref[_LRU_HALO:_LRU_HALO + t, :] = lx0_ref[...]
        lx = _causal_conv(ext_ref, _LRU_HALO, lcw_ref, lcb_ref, _LRU_K, t)
        r, ig, sp, a, mult = _lru_gates(lx, wab_ref, ba_ref, bx_ref, lam_ref)
        u = mult * (ig * lx)
        a_cum, h_loc = _scan_fwd(a, u, rows)

        @pl.when(first)
        def _():
            hc_ref[...] = jnp.zeros_like(hc_ref)

        h = h_loc + a_cum * hc_ref[7:8, :]
        hs_ref[...] = h
        hc_ref[...] = hs_ref[pl.ds(t - 8, 8), :]
        ycat_ref[:, 0:c] = (h * _gelu(gate_ref[...])).astype(ycat_ref.dtype)

        cge_ref[0:_CONF_HALO, :] = jnp.where(first, 0.0, cah_ref[...] * _sigmoid(cbh_ref[...]))
        cge_ref[_CONF_HALO:_CONF_HALO + t, :] = ca_ref[...] * _sigmoid(cb_ref[...])
        win = _Windows(cge_ref, shifted_ref, t)
        first_off = _CONF_HALO - (_CONF_K - 1)
        cc = ccb_ref[...] + ccw_ref[0:1, :] * win(first_off)
        for k in range(1, _CONF_K):
            cc = cc + ccw_ref[k:k + 1, :] * win(first_off + k)
        cc_ref[...] = cc
        xc = cc - jnp.mean(cc, axis=-1, keepdims=True)
        rstd = lax.rsqrt(jnp.mean(xc * xc, axis=-1, keepdims=True) + _EPS)
        ln = xc * rstd * lng_ref[...] + lnb_ref[...]
        ycat_ref[:, c:2 * c] = (ln * _sigmoid(ln)).astype(ycat_ref.dtype)

        zprev_ref[...] = z_ref[pl.ds(t - _CONF_HALO, _CONF_HALO), :]
        y = _dot(ycat_ref[...], wout_ref[...], _NN)
        _epi_residual_rms(y, (x_ref, gn_ref), (x1_ref, h2_ref), first)

    def const(arr):
        return pl.BlockSpec(arr.shape, lambda i: (0,) * arr.ndim, pipeline_mode=pl.Buffered(1))

    def rows_of(width):
        return pl.BlockSpec((t, width), lambda i: (i, 0))

    params = (lcw, lcb, wab, ba, bx, lam, ccw, ccb, lng, lnb)
    res, cres = _pcall(
        body, args=(xs, h1, w_in, w_out, gnorm, *params), grid=(nt,),
        in_specs=[rows_of(d), rows_of(d), const(w_in), const(w_out), const(gnorm)] + [const(p) for p in params],
        out_specs=[rows_of(nblk * n), rows_of(2 * c), rows_of(c), rows_of(c), rows_of(d), rows_of(d)],
        out_shape=[jax.ShapeDtypeStruct((s, nblk * n), _F32), jax.ShapeDtypeStruct((s, 2 * c), _MXU),
                   jax.ShapeDtypeStruct((s, c), _F32), jax.ShapeDtypeStruct((s, c), _F32),
                   jax.ShapeDtypeStruct((s, d), _F32), jax.ShapeDtypeStruct((s, d), _MXU)],
        scratch_shapes=[pltpu.VMEM((t + _LRU_HALO, c), _F32), pltpu.VMEM((t + _CONF_HALO, c), _F32),
                        pltpu.VMEM((8, c), _F32), pltpu.VMEM((7, t + _CONF_HALO, c), _F32),
                        pltpu.VMEM((_CONF_HALO, nblk * n), _F32)],
        sem=("arbitrary",), name=name, comm=comm)
    return res, cres


def _mixer_bwd(dx1b, w_out, h1, z, hs, cc, lcw, lcb, wab, ba, bx, lam, ccw, ccb, lng, lnb, name, comm=None):
    s = z.shape[0]
    d = h1.shape[1]
    c = _D_LRU
    t = _tile(s, _MIX_T)
    nt = s // t
    nblk = z.shape[1] // 256

    def body(dxb_ref, wout_ref, h1_ref, lx0_ref, lx0h_ref, gate_ref, ca_ref, cah_ref, cb_ref, cbh_ref,
             hs_ref, hsh_ref, cc_ref,
             lcw_ref, lcb_ref, wab_ref, ba_ref, bx_ref, lam_ref, ccw_ref, ccb_ref, lng_ref, lnb_ref,
             dz_ref, pin_ref, dlcw_ref, dlcb_ref, dwab_ref, dba_ref, dbx_ref, dlam_ref, dccw_ref, dccb_ref, dlng_ref,
             dlnb_ref,
             ext_ref, up_ref, cge_ref, dce_ref, xc_ref, dlxc_ref, dccc_ref, shifted_ref, dwin_ref):
        i = pl.program_id(0)
        first_tile = i == nt - 1
        last_tile = i == 0
        rows = lax.broadcasted_iota(jnp.int32, (t, c), 0)

        @pl.when(last_tile)
        def _():
            for ref in (dlcw_ref, dlcb_ref, dwab_ref, dba_ref, dbx_ref, dlam_ref, dccw_ref, dccb_ref, dlng_ref,
                        dlnb_ref, xc_ref, dlxc_ref, dccc_ref, dwin_ref):
                ref[...] = jnp.zeros_like(ref)

        dycat = _dot(dxb_ref[...], wout_ref[...], _NT)

        ext_ref[0:_LRU_HALO, :] = jnp.where(first_tile, 0.0, lx0h_ref[...])
        ext_ref[_LRU_HALO:_LRU_HALO + t, :] = lx0_ref[...]
        lx = _causal_conv(ext_ref, _LRU_HALO, lcw_ref, lcb_ref, _LRU_K, t)
        r, ig, sp, a, mult = _lru_gates(lx, wab_ref, ba_ref, bx_ref, lam_ref)
        h = hs_ref[...]
        gl, dgl = _gelu_and_grad(gate_ref[...])
        dyl = dycat[:, 0:c]
        dz_ref[:, c:2 * c] = (dyl * h * dgl).astype(dz_ref.dtype)
        dh = dyl * gl

        up_ref[0:t, :] = a
        up_ref[t:t + 8, :] = jnp.ones((8, c), _F32)
        a_up = up_ref[pl.ds(1, t), :]
        a_cum, g_loc = _scan_rev(a_up, dh, rows)
        gt = g_loc + a_cum * xc_ref[0:1, :]
        xc_ref[...] = (a * gt)[0:8, :]

        up_ref[0:8, :] = jnp.where(first_tile, 0.0, hsh_ref[...])
        up_ref[8:8 + t, :] = h
        hprev = up_ref[pl.ds(7, t), :]

        da = gt * hprev
        dmult = gt * ig * lx
        dig = gt * mult * lx
        dlx = gt * mult * ig
        dlog_a = da * a - dmult * a * a / mult
        dpre_r = dlog_a * (-_RG_C) * sp * r * (1.0 - r)
        dpre_i = dig * ig * (1.0 - ig)
        dlam_ref[...] += _colsum(dlog_a * r) * (_RG_C * _sigmoid(-lam_ref[...]))
        dba_ref[...] += _colsum(dpre_r)
        dbx_ref[...] += _colsum(dpre_i)
        dpre = jnp.concatenate([dpre_r, dpre_i], axis=1).astype(_MXU)
        dlx = dlx + _dot(dpre, wab_ref[...], _NT)
        dwab_ref[...] += _dot(lx, dpre, _TN)

        dlcb_ref[...] += _colsum(dlx)
        up_ref[0:t, :] = dlx
        up_ref[t:t + 8, :] = dlxc_ref[...]
        dlxc_ref[...] = dlx[0:8, :]
        acc = lcw_ref[0:1, :] * up_ref[pl.ds(_LRU_K - 1, t), :]
        for k in range(1, _LRU_K):
            acc = acc + lcw_ref[k:k + 1, :] * up_ref[pl.ds(_LRU_K - 1 - k, t), :]
        dz_ref[:, 0:c] = acc.astype(dz_ref.dtype)
        for k in range(_LRU_K):
            dlcw_ref[k:k + 1, :] += _colsum(dlx * ext_ref[pl.ds(_LRU_HALO - (_LRU_K - 1) + k, t), :])

        sig_b = _sigmoid(cb_ref[...])
        ca = ca_ref[...]
        cge_ref[0:_CONF_HALO, :] = jnp.where(first_tile, 0.0, cah_ref[...] * _sigmoid(cbh_ref[...]))
        cge_ref[_CONF_HALO:_CONF_HALO + t, :] = ca * sig_b
        ccv = cc_ref[...]
        xcen = ccv - jnp.mean(ccv, axis=-1, keepdims=True)
        rstd = lax.rsqrt(jnp.mean(xcen * xcen, axis=-1, keepdims=True) + _EPS)
        xn = xcen * rstd
        ln = xn * lng_ref[...] + lnb_ref[...]
        sg = _sigmoid(ln)
        dln = dycat[:, c:2 * c] * (sg * (1.0 + ln * (1.0 - sg)))
        dlng_ref[...] += _colsum(dln * xn)
        dlnb_ref[...] += _colsum(dln)
        dxn = dln * lng_ref[...]
        dcc = rstd * (dxn - jnp.mean(dxn, axis=-1, keepdims=True)
                      - xn * jnp.mean(dxn * xn, axis=-1, keepdims=True))
        dccb_ref[...] += _colsum(dcc)
        win = _Windows(cge_ref, shifted_ref, t)
        for k in range(_CONF_K):
            dccw_ref[k:k + 1, :] += _colsum(dcc * win(_CONF_HALO - (_CONF_K - 1) + k))
        dce_ref[0:t, :] = dcc
        dce_ref[t:t + _CONF_HALO, :] = dccc_ref[...]
        dccc_ref[...] = dcc[0:_CONF_HALO, :]
        win = _Windows(dce_ref, shifted_ref, t)
        dcg = ccw_ref[0:1, :] * win(_CONF_K - 1)
        for k in range(1, _CONF_K):
            dcg = dcg + ccw_ref[k:k + 1, :] * win(_CONF_K - 1 - k)
        dz_ref[:, 2 * c:3 * c] = (dcg * sig_b).astype(dz_ref.dtype)
        dz_ref[:, 3 * c:4 * c] = (dcg * ca * sig_b * (1.0 - sig_b)).astype(dz_ref.dtype)

        dwin_ref[...] += _dot(h1_ref[...], dz_ref[...], _TN)

        @pl.when(first_tile)
        def _():
            for j in range(nblk):
                pin_ref[j] = dwin_ref[:, j * 256:(j + 1) * 256].astype(pin_ref.dtype)

    def col(j):
        return pl.BlockSpec((t, c), lambda i: (nt - 1 - i, j))

    def halo(j, rows_):
        per = t // rows_
        return pl.BlockSpec((rows_, c), lambda i: (jnp.maximum((nt - 1 - i) * per - 1, 0), j))

    def full(shape):
        return pl.BlockSpec(shape, lambda i: (0,) * len(shape))

    params = (lcw, lcb, wab, ba, bx, lam, ccw, ccb, lng, lnb)
    small = [(_LRU_K, c), (1, c), (c, 2 * c), (1, c), (1, c), (1, c), (_CONF_K, c), (1, c), (1, c), (1, c)]
    wide = pl.BlockSpec((t, d), lambda i: (nt - 1 - i, 0))
    pin_shape = (nblk, d, 256)
    return _pcall(
        body, args=(dx1b, w_out, h1, z, z, z, z, z, z, z, hs, hs, cc, *params), grid=(nt,),
        in_specs=[wide, pl.BlockSpec(w_out.shape, lambda i: (0, 0), pipeline_mode=pl.Buffered(1)), wide,
                  col(0), halo(0, _LRU_HALO), col(1), col(2), halo(2, _CONF_HALO), col(3), halo(3, _CONF_HALO),
                  col(0), halo(0, 8), col(0)]
        + [full(p.shape) for p in params],
        out_specs=[pl.BlockSpec((t, 4 * c), lambda i: (nt - 1 - i, 0)), full(pin_shape)] + [full(sh) for sh in small],
        out_shape=[jax.ShapeDtypeStruct((s, 4 * c), _MXU), jax.ShapeDtypeStruct(pin_shape, _XFER)]
        + [jax.ShapeDtypeStruct(sh, _F32) for sh in small],
        scratch_shapes=[pltpu.VMEM((t + _LRU_HALO, c), _F32), pltpu.VMEM((t + 8, c), _F32),
                        pltpu.VMEM((t + _CONF_HALO, c), _F32), pltpu.VMEM((t + _CONF_HALO, c), _F32),
                        pltpu.VMEM((8, c), _F32), pltpu.VMEM((8, c), _F32), pltpu.VMEM((_CONF_HALO, c), _F32),
                        pltpu.VMEM((7, t + _CONF_HALO, c), _F32), pltpu.VMEM((d, 4 * c), _F32)],
        sem=("arbitrary",), name=name, comm=comm, vmem=_VMEM_LIMIT_FUSED)


_ATT_T = 512


def _attn_probs(qh, kh, scale):
    sc = _dot(qh, kh, _NT) * scale
    e = jnp.exp(sc - jnp.max(sc, axis=-1, keepdims=True))
    return e / jnp.sum(e, axis=-1, keepdims=True)


def _const_spec(arr):
    return pl.BlockSpec(arr.shape, lambda i: (0,) * arr.ndim, pipeline_mode=pl.Buffered(1))


def _attn_fwd(h2, x1, w_q, kv, w_o, gnorm, name):
    s, d = h2.shape
    nm = kv.shape[0]
    hd = d // _XA_HEADS
    t = _tile(s, _ATT_T)
    scale = hd ** -0.5

    def body(h_ref, x1_ref, wq_ref, k_ref, v_ref, wo_ref, g_ref, q_ref, o_ref, x2_ref, h3_ref):
        q_ref[...] = _dot(h_ref[...], wq_ref[...], _NN).astype(q_ref.dtype)
        for hh in range(_XA_HEADS):
            sl = slice(hh * hd, (hh + 1) * hd)
            p = _attn_probs(q_ref[:, sl], k_ref[:, sl], scale)
            o_ref[:, sl] = _dot(p, v_ref[:, sl], _NN).astype(o_ref.dtype)
        y = _dot(o_ref[...], wo_ref[...], _NN)
        _epi_residual_rms(y, (x1_ref, g_ref), (x2_ref, h3_ref), None)

    row = pl.BlockSpec((t, d), lambda i: (i, 0))
    half = pl.BlockSpec((nm, d), lambda i: (0, 0), pipeline_mode=pl.Buffered(1))
    half2 = pl.BlockSpec((nm, d), lambda i: (0, 1), pipeline_mode=pl.Buffered(1))
    return pl.pallas_call(
        body, grid=(s // t,),
        in_specs=[row, row, _const_spec(w_q), half, half2, _const_spec(w_o), _const_spec(gnorm)],
        out_specs=[row, row, row, row],
        out_shape=[jax.ShapeDtypeStruct((s, d), _MXU), jax.ShapeDtypeStruct((s, d), _MXU),
                   jax.ShapeDtypeStruct((s, d), _F32), jax.ShapeDtypeStruct((s, d), _MXU)],
        compiler_params=_cparams("parallel"), name=name,
    )(h2, x1, w_q, kv, kv, w_o, gnorm)


def _attn_bwd(dx2b, dx2, q, o, h2, x1, kv, w_o, w_q, gnorm, name, comm=None):
    s, d = q.shape
    nm = kv.shape[0]
    hd = d // _XA_HEADS
    t = _tile(s, _ATT_T)
    nt = s // t
    scale = hd ** -0.5

    def body(dxb_ref, dx2_ref, q_ref, o_ref, h_ref, x1_ref, k_ref, v_ref, wo_ref, wq_ref, g_ref,
             dx1_ref, dx1b_ref, dgn_ref, dk_ref, dv_ref, pwo_ref, pwq_ref, dq_ref, awo_ref, awq_ref):
        i = pl.program_id(0)
        first = i == 0

        @pl.when(first)
        def _():
            for ref in (dk_ref, dv_ref, awo_ref, awq_ref):
                ref[...] = jnp.zeros_like(ref)

        dxb = dxb_ref[...]
        do = _dot(dxb, wo_ref[...], _NT).astype(_MXU)
        awo_ref[...] += _dot(o_ref[...], dxb, _TN)
        for hh in range(_XA_HEADS):
            sl = slice(hh * hd, (hh + 1) * hd)
            qh = q_ref[:, sl]
            kh = k_ref[:, sl]
            doh = do[:, sl]
            p = _attn_probs(qh, kh, scale)
            dp = _dot(doh, v_ref[:, sl], _NT)
            dv_ref[:, sl] += _dot(p, doh, _TN)
            ds = (p * (dp - jnp.sum(dp * p, axis=-1, keepdims=True)) * scale).astype(_MXU)
            dq_ref[:, sl] = _dot(ds, kh, _NN).astype(dq_ref.dtype)
            dk_ref[:, sl] += _dot(ds, qh, _TN)
        dq = dq_ref[...]
        awq_ref[...] += _dot(h_ref[...], dq, _TN)
        dh = _dot(dq, wq_ref[...], _NT)
        _epi_rms_bwd(dh, (x1_ref, g_ref, dx2_ref), (dx1_ref, dx1b_ref, dgn_ref), first)

        @pl.when(i == nt - 1)
        def _():
            pwo_ref[...] = awo_ref[...].astype(pwo_ref.dtype)
            pwq_ref[...] = awq_ref[...].astype(pwq_ref.dtype)

    row = pl.BlockSpec((t, d), lambda i: (i, 0))
    vec = pl.BlockSpec((1, d), lambda i: (0, 0))
    mem_blk = pl.BlockSpec((nm, d), lambda i: (0, 0))
    sq = pl.BlockSpec((d, d), lambda i: (0, 0))
    half = pl.BlockSpec((nm, d), lambda i: (0, 0), pipeline_mode=pl.Buffered(1))
    half2 = pl.BlockSpec((nm, d), lambda i: (0, 1), pipeline_mode=pl.Buffered(1))
    return _pcall(
        body, args=(dx2b, dx2, q, o, h2, x1, kv, kv, w_o, w_q, gnorm), grid=(nt,),
        in_specs=[row, row, row, row, row, row, half, half2, _const_spec(w_o), _const_spec(w_q), _const_spec(gnorm)],
        out_specs=[row, row, vec, mem_blk, mem_blk, sq, sq],
        out_shape=[jax.ShapeDtypeStruct((s, d), _F32), jax.ShapeDtypeStruct((s, d), _MXU),
                   jax.ShapeDtypeStruct((1, d), _F32), jax.ShapeDtypeStruct((nm, d), _F32),
                   jax.ShapeDtypeStruct((nm, d), _F32), jax.ShapeDtypeStruct((d, d), _XFER),
                   jax.ShapeDtypeStruct((d, d), _XFER)],
        scratch_shapes=[pltpu.VMEM((t, d), _MXU), pltpu.VMEM((d, d), _F32), pltpu.VMEM((d, d), _F32)],
        sem=("arbitrary",), name=name, comm=comm, vmem=_VMEM_LIMIT_FUSED)


_FFN_K = 3
_FFN_FUSED_T = 256
_VMEM_LIMIT_FUSED = 58 * 1024 * 1024


def _ffn_fused_fwd(h3, w_up, w_down, fcw, fcb, x2, target, gfin, name):
    s, d = h3.shape
    nblk, _, n = w_up.shape
    half = nblk // 2
    f = half * n
    t = _tile(s, _FFN_FUSED_T)

    def body(h_ref, wup_ref, wdown_ref, w_ref, b_ref, x2_ref, t_ref, g_ref,
             g0_ref, act_ref, gl_ref, udgl_ref, dx_ref, dxb_ref, l_ref, dg_ref, ext0_ref, ext1_ref, halo_ref):
        i = pl.program_id(0)
        first = i == 0
        h = h_ref[...]
        total = None
        ahead = (_dot(h, wup_ref[0], _NN), _dot(h, wup_ref[half], _NN))
        for j in range(half):
            cs = slice(j * n, (j + 1) * n)
            ext_ref = ext0_ref if j % 2 == 0 else ext1_ref
            g0, u = ahead
            if j + 1 < half:
                ahead = (_dot(h, wup_ref[j + 1], _NN), _dot(h, wup_ref[half + j + 1], _NN))
            if j > 0:
                prev = slice((j - 1) * n, j * n)
                p = _dot(act_ref[:, prev], wdown_ref[prev, :], _NN)
                total = p if total is None else total + p
            g0_ref[:, cs] = g0.astype(g0_ref.dtype)
            ext_ref[0:8, :] = jnp.where(first, 0.0, halo_ref[:, cs])
            ext_ref[8:8 + t, :] = g0
            halo_ref[:, cs] = g0[t - 8:t, :]
            g = _causal_conv(ext_ref, 8, w_ref.at[:, cs], b_ref.at[:, cs], _FFN_K, t)
            gl, dgl = _gelu_and_grad(g)
            gl_ref[:, cs] = gl.astype(gl_ref.dtype)
            udgl_ref[:, cs] = (u * dgl).astype(udgl_ref.dtype)
            act_ref[:, cs] = (gl * u).astype(act_ref.dtype)
        last = slice((half - 1) * n, half * n)
        total = total + _dot(act_ref[:, last], wdown_ref[last, :], _NN)
        _epi_final(total, (x2_ref, t_ref, g_ref), (dx_ref, dxb_ref, l_ref, dg_ref), first)

    def const(shape):
        return pl.BlockSpec(shape, lambda i: (0,) * len(shape), pipeline_mode=pl.Buffered(1))

    row = pl.BlockSpec((t, d), lambda i: (i, 0))
    vec = pl.BlockSpec((1, d), lambda i: (0, 0))
    return pl.pallas_call(
        body, grid=(s // t,),
        in_specs=[row, const(w_up.shape), const(w_down.shape), const(fcw.shape), const(fcb.shape), row, row, vec],
        out_specs=[pl.BlockSpec((t, f), lambda i: (i, 0))] * 4 + [row, row, vec, vec],
        out_shape=[jax.ShapeDtypeStruct((s, f), _MXU)] * 4
        + [jax.ShapeDtypeStruct((s, d), _F32), jax.ShapeDtypeStruct((s, d), _MXU),
                   jax.ShapeDtypeStruct((1, d), _F32), jax.ShapeDtypeStruct((1, d), _F32)],
        scratch_shapes=[pltpu.VMEM((t + 8, n), _F32), pltpu.VMEM((t + 8, n), _F32), pltpu.VMEM((8, f), _F32)],
        compiler_params=_cparams("arbitrary", vmem=_VMEM_LIMIT_FUSED), name=name,
    )(h3, w_up, w_down, fcw, fcb, x2, target, gfin)


def _ffn_fused_bwd(dx3b, g0, gl, udgl, w_down, w_up, fcw, x2, gnorm, dx3, name, comm=None):
    s, d = x2.shape
    nblk, _, n = w_up.shape
    half = nblk // 2
    f = half * n
    t = _tile(s, _FFN_FUSED_T)
    nt = s // t
    hrows = 16

    def body(dxb_ref, g0_ref, g0h_ref, gl_ref, udgl_ref, wdown_ref, wup_ref, w_ref, x2_ref, g_ref, dx3_ref,
             dgu_ref, dx2_ref, dx2b_ref, dgn_ref, dw_ref, db_ref, ext0_ref, ext1_ref, up0_ref, up1_ref, car_ref):
        i = pl.program_id(0)
        first_tile = i == nt - 1
        last_tile = i == 0

        @pl.when(last_tile)
        def _():
            dw_ref[...] = jnp.zeros_like(dw_ref)
            db_ref[...] = jnp.zeros_like(db_ref)
            car_ref[...] = jnp.zeros_like(car_ref)

        dxb = dxb_ref[...]
        total = None
        for j in range(half):
            cs = slice(j * n, (j + 1) * n)
            us = slice(f + j * n, f + (j + 1) * n)
            ext_ref = ext0_ref if j % 2 == 0 else ext1_ref
            up_ref = up0_ref if j % 2 == 0 else up1_ref
            dact = _dot(dxb, wdown_ref[cs, :], _NT)
            ext_ref[0:8, :] = jnp.where(first_tile, 0.0, g0h_ref[:, cs].astype(_F32)[hrows - 8:hrows])
            ext_ref[8:8 + t, :] = g0_ref[:, cs].astype(_F32)
            du = (dact * gl_ref[:, cs].astype(_F32)).astype(dgu_ref.dtype)
            dgu_ref[:, us] = du
            dg = dact * udgl_ref[:, cs].astype(_F32)
            db_ref[:, cs] += _colsum(dg)
            for k in range(_FFN_K):
                dw_ref[k:k + 1, cs] += _colsum(dg * ext_ref[pl.ds(8 - (_FFN_K - 1) + k, t), :])
            up_ref[0:t, :] = dg
            up_ref[t:t + 8, :] = car_ref[:, cs]
            car_ref[:, cs] = dg[0:8, :]
            dg0 = w_ref[0:1, cs] * up_ref[pl.ds(_FFN_K - 1, t), :]
            for k in range(1, _FFN_K):
                dg0 = dg0 + w_ref[k:k + 1, cs] * up_ref[pl.ds(_FFN_K - 1 - k, t), :]
            dg0 = dg0.astype(dgu_ref.dtype)
            dgu_ref[:, cs] = dg0
            p = _dot(dg0, wup_ref[j], _NT) + _dot(du, wup_ref[half + j], _NT)
            total = p if total is None else total + p
        _epi_rms_bwd(total, (x2_ref, g_ref, dx3_ref), (dx2_ref, dx2b_ref, dgn_ref), last_tile)

    def const(shape):
        return pl.BlockSpec(shape, lambda i: (0,) * len(shape), pipeline_mode=pl.Buffered(1))

    row = pl.BlockSpec((t, d), lambda i: (nt - 1 - i, 0))
    vec = pl.BlockSpec((1, d), lambda i: (0, 0))
    per = t // hrows
    wide = pl.BlockSpec((t, f), lambda i: (nt - 1 - i, 0))
    return _pcall(
        body, args=(dx3b, g0, g0, gl, udgl, w_down, w_up, fcw, x2, gnorm, dx3), grid=(nt,),
        in_specs=[row, wide, pl.BlockSpec((hrows, f), lambda i: (jnp.maximum((nt - 1 - i) * per - 1, 0), 0)),
                  wide, wide, const(w_down.shape), const(w_up.shape), const(fcw.shape), row, vec, row],
        out_specs=[pl.BlockSpec((t, 2 * f), lambda i: (nt - 1 - i, 0)), row, row, vec,
                   pl.BlockSpec((_FFN_K, f), lambda i: (0, 0)), pl.BlockSpec((1, f), lambda i: (0, 0))],
        out_shape=[jax.ShapeDtypeStruct((s, 2 * f), _MXU), jax.ShapeDtypeStruct((s, d), _F32),
                   jax.ShapeDtypeStruct((s, d), _MXU), jax.ShapeDtypeStruct((1, d), _F32),
                   jax.ShapeDtypeStruct((_FFN_K, f), _F32), jax.ShapeDtypeStruct((1, f), _F32)],
        scratch_shapes=[pltpu.VMEM((t + 8, n), _F32), pltpu.VMEM((t + 8, n), _F32),
                        pltpu.VMEM((t + 8, n), _F32), pltpu.VMEM((t + 8, n), _F32), pltpu.VMEM((8, f), _F32)],
        sem=("arbitrary",), name=name, comm=comm, vmem=_VMEM_LIMIT_FUSED)


def _mesh_pos():
    return lax.axis_index("x"), lax.axis_index("y"), lax.axis_index("c")


def _flip(v, bit):
    return 1 - v if bit else v


def _sem_scratch(n):
    return [pltpu.SemaphoreType.DMA((7 * n,)), pltpu.SemaphoreType.DMA((7 * n,)), pltpu.SemaphoreType.DMA((n,))]


class _Gather:
    def __init__(self, xs):
        self.ins = list(xs)
        self.outs = [jax.ShapeDtypeStruct((_NDEV,) + v.shape, v.dtype) for v in xs]
        self.scratch = _sem_scratch(len(xs))

    def _plan(self, x_refs, out_refs, sems):
        send_sems, recv_sems, local_sems = sems
        x, y, c = _mesh_pos()
        me, sibling = (x, y, c), (x, y, 1 - c)
        chips = [(1 - x, y), (x, 1 - y), (1 - x, 1 - y)]

        def copy(a, k, block, to, src=None):
            slot = out_refs[a].at[4 * block[0] + 2 * block[1] + block[2]]
            return pltpu.make_async_remote_copy(
                src_ref=slot if src is None else src, dst_ref=slot,
                send_sem=send_sems.at[a * 7 + k], recv_sem=recv_sems.at[a * 7 + k],
                device_id=to, device_id_type=_MESH_ID)

        def own(a):
            return pltpu.make_async_copy(x_refs[a], out_refs[a].at[4 * x + 2 * y + c], local_sems.at[a])

        def first(a):
            return [copy(a, 0, me, sibling, src=x_refs[a])] + [
                copy(a, 1 + j, me, (*chip, c), src=x_refs[a]) for j, chip in enumerate(chips)]

        return me, sibling, chips, c, copy, own, first

    def start(self, x_refs, out_refs, sems):
        _, _, _, _, _, own, first = self._plan(x_refs, out_refs, sems)
        for a in range(len(self.ins)):
            own(a).start()
            for cp in first(a):
                cp.start()

    def finish(self, x_refs, out_refs, sems):
        me, sibling, chips, c, copy, own, first = self._plan(x_refs, out_refs, sems)
        n = len(self.ins)
        passed = []
        for a in range(n):
            for j, chip in enumerate(chips):
                copy(a, 1 + j, (*chip, c), me).wait_recv()
                fwd = copy(a, 4 + j, (*chip, c), sibling)
                fwd.start()
                passed.append(fwd)
        for a in range(n):
            copy(a, 0, sibling, me).wait_recv()
            for j, chip in enumerate(chips):
                copy(a, 4 + j, (*chip, 1 - c), me).wait_recv()
        for a in range(n):
            for cp in first(a):
                cp.wait_send()
        for cp in passed:
            cp.wait_send()
        for a in range(n):
            own(a).wait()


class _Exchange:
    def __init__(self, gs):
        self.ins = list(gs)
        self.outs = [jax.ShapeDtypeStruct(v.shape, v.dtype) for v in gs]
        self.scratch = _sem_scratch(len(gs))

    def _plan(self, g_refs, r_refs, sems):
        send_sems, recv_sems, local_sems = sems
        x, y, c = _mesh_pos()
        me_idx = 4 * x + 2 * y + c
        n = len(self.ins)

        def copy(a, k):
            peer = (_flip(x, k & 4), _flip(y, k & 2), _flip(c, k & 1))
            peer_idx = 4 * peer[0] + 2 * peer[1] + peer[2]
            return pltpu.make_async_remote_copy(
                src_ref=g_refs[a].at[peer_idx], dst_ref=r_refs[a].at[me_idx],
                send_sem=send_sems.at[a * 7 + k - 1], recv_sem=recv_sems.at[a * 7 + k - 1],
                device_id=peer, device_id_type=_MESH_ID)

        copies = [copy(a, k) for a in range(n) for k in range(1, _NDEV)]
        mine = [pltpu.make_async_copy(g_refs[a].at[me_idx], r_refs[a].at[me_idx], local_sems.at[a])
                for a in range(n)]
        return copies, mine

    def start(self, g_refs, r_refs, sems):
        copies, mine = self._plan(g_refs, r_refs, sems)
        for cp in copies + mine:
            cp.start()

    def finish(self, g_refs, r_refs, sems):
        copies, mine = self._plan(g_refs, r_refs, sems)
        for cp in copies:
            cp.wait_recv()
        for cp in copies:
            cp.wait_send()
        for cp in mine:
            cp.wait()


class _Both:
    def __init__(self, first, second):
        self.parts = (first, second)
        self.ins = first.ins + second.ins
        self.outs = first.outs + second.outs
        self.scratch = first.scratch + second.scratch

    def _split(self, ins, outs, sems):
        a, b = self.parts
        na, nb = len(a.ins), len(a.scratch)
        return (a, ins[:na], outs[:na], sems[:nb]), (b, ins[na:], outs[na:], sems[nb:])

    def start(self, ins, outs, sems):
        for part, i, o, s in self._split(ins, outs, sems):
            part.start(i, o, s)

    def finish(self, ins, outs, sems):
        for part, i, o, s in self._split(ins, outs, sems):
            part.finish(i, o, s)


def _comm_call(comm, name):
    def body(*refs):
        n_i, n_o = len(comm.ins), len(comm.outs)
        ins, outs, sems = refs[:n_i], refs[n_i:n_i + n_o], refs[n_i + n_o:]
        comm.start(ins, outs, sems)
        comm.finish(ins, outs, sems)

    return pl.pallas_call(
        body, out_shape=list(comm.outs), in_specs=[_ANY] * len(comm.ins), out_specs=[_ANY] * len(comm.outs),
        scratch_shapes=list(comm.scratch), name=name)(*comm.ins)


def _adamw_math(w, g, m, v):
    m = _ADAM_B1 * m + (1.0 - _ADAM_B1) * g
    v = _ADAM_B2 * v + (1.0 - _ADAM_B2) * (g * g)
    m_hat = m / (1.0 - _ADAM_B1 ** _ADAM_STEP)
    v_hat = v / (1.0 - _ADAM_B2 ** _ADAM_STEP)
    delta = -_ADAM_LR * (m_hat / (jnp.sqrt(v_hat) + _ADAM_EPS) + _ADAM_WD * w)
    return delta, m, v


def _sum_adamw(parts, w, m, v, name):
    r, c = w.shape
    tr = _tile(r, 128)

    def body(p_ref, w_ref, m_ref, v_ref, g_ref, d_ref, nm_ref, nv_ref):
        g = p_ref[0].astype(_F32)
        for j in range(1, _NDEV):
            g = g + p_ref[j].astype(_F32)
        delta, nm, nv = _adamw_math(w_ref[...], g, m_ref[...], v_ref[...])
        g_ref[...] = g
        d_ref[...] = delta
        nm_ref[...] = nm
        nv_ref[...] = nv

    blk = pl.BlockSpec((tr, c), lambda i: (i, 0))
    return pl.pallas_call(
        body, grid=(r // tr,),
        in_specs=[pl.BlockSpec((_NDEV, tr, c), lambda i: (0, i, 0)), blk, blk, blk],
        out_specs=[blk] * 4, out_shape=[jax.ShapeDtypeStruct((r, c), _F32)] * 4,
        compiler_params=_cparams("parallel"), name=name,
    )(parts, w, m, v)


def _sum8(parts, name):
    _, r, c = parts.shape

    def body(p_ref, o_ref):
        g = p_ref[0]
        for j in range(1, _NDEV):
            g = g + p_ref[j]
        o_ref[...] = g

    return pl.pallas_call(
        body, grid=(1,), in_specs=[pl.BlockSpec((_NDEV, r, c), lambda i: (0, 0, 0))],
        out_specs=pl.BlockSpec((r, c), lambda i: (0, 0)), out_shape=jax.ShapeDtypeStruct((r, c), _F32),
        compiler_params=_cparams("arbitrary"), name=name,
    )(parts)


def _adamw_many(gs, ws, ms, vs, name):
    n = len(ws)

    def body(*refs):
        g_refs, w_refs, m_refs, v_refs = (refs[k * n:(k + 1) * n] for k in range(4))
        d_refs, nm_refs, nv_refs = (refs[(4 + k) * n:(5 + k) * n] for k in range(3))
        for k in range(n):
            delta, nm, nv = _adamw_math(w_refs[k][...], g_refs[k][...], m_refs[k][...], v_refs[k][...])
            d_refs[k][...] = delta
            nm_refs[k][...] = nm
            nv_refs[k][...] = nv

    def whole(arr):
        return pl.BlockSpec(arr.shape, lambda i, nd=arr.ndim: (0,) * nd)

    specs = [whole(w) for w in ws]
    res = pl.pallas_call(
        body, grid=(1,), in_specs=specs * 4, out_specs=specs * 3,
        out_shape=[jax.ShapeDtypeStruct(w.shape, _F32) for w in ws] * 3,
        compiler_params=_cparams("arbitrary"), name=name,
    )(*gs, *ws, *ms, *vs)
    return res[:n], res[n:2 * n], res[2 * n:]


def _pack(arrs):
    flat = jnp.concatenate([a.reshape(-1).astype(_F32) for a in arrs])
    pad = (-flat.shape[0]) % 1024
    return jnp.pad(flat, (0, pad)).reshape(-1, 128)


def _unpack(flat2d, shapes):
    flat = flat2d.reshape(-1)
    out, off = [], 0
    for sh in shapes:
        size = 1
        for dim in sh:
            size *= dim
        out.append(flat[off:off + size].reshape(sh))
        off += size
    return out


def _block_diag(w):
    h, hd, _ = w.shape
    eye = jnp.eye(h, dtype=w.dtype)
    return (eye[:, None, :, None] * w[:, :, None, :]).reshape(h * hd, h * hd)


def _diag_blocks(full, h):
    hd = full.shape[0] // h
    return jnp.stack([full[i * hd:(i + 1) * hd, i * hd:(i + 1) * hd] for i in range(h)])


def kernel(x, mem, mix_norm_g, w_in, lru_conv_w, lru_conv_b, lru_w_a, lru_b_a, lru_w_x, lru_b_x, lru_lambda, conf_conv_w, conf_conv_b, conf_ln_g, conf_ln_b, w_out, xa_norm_g, mem_norm_g, w_q, w_kv, w_o, ffn_norm_g, w_up, ffn_conv_w, ffn_conv_b, w_down, final_norm_g, loss_target, m_mix_norm_g, m_w_in, m_lru_conv_w, m_lru_conv_b, m_lru_w_a, m_lru_b_a, m_lru_w_x, m_lru_b_x, m_lru_lambda, m_conf_conv_w, m_conf_conv_b, m_conf_ln_g, m_conf_ln_b, m_w_out, m_xa_norm_g, m_mem_norm_g, m_w_q, m_w_kv, m_w_o, m_ffn_norm_g, m_w_up, m_ffn_conv_w, m_ffn_conv_b, m_w_down, m_final_norm_g, v_mix_norm_g, v_w_in, v_lru_conv_w, v_lru_conv_b, v_lru_w_a, v_lru_b_a, v_lru_w_x, v_lru_b_x, v_lru_lambda, v_conf_conv_w, v_conf_conv_b, v_conf_ln_g, v_conf_ln_b, v_w_out, v_xa_norm_g, v_mem_norm_g, v_w_q, v_w_kv, v_w_o, v_ffn_norm_g, v_w_up, v_ffn_conv_w, v_ffn_conv_b, v_w_down, v_final_norm_g):
    names = ["mix_norm_g", "w_in", "lru_conv_w", "lru_conv_b", "lru_w_a", "lru_b_a", "lru_w_x", "lru_b_x",
             "lru_lambda", "conf_conv_w", "conf_conv_b", "conf_ln_g", "conf_ln_b", "w_out", "xa_norm_g",
             "mem_norm_g", "w_q", "w_kv", "w_o", "ffn_norm_g", "w_up", "ffn_conv_w", "ffn_conv_b", "w_down",
             "final_norm_g"]
    loc = locals()
    W = {n: loc[n] for n in names}
    M = {n: loc["m_" + n] for n in names}
    V = {n: loc["v_" + n] for n in names}
    big = ["w_in", "w_out", "w_q", "w_kv", "w_o", "w_up", "w_down"]
    conv_sharded = ["lru_conv_w", "conf_conv_w", "ffn_conv_w"]

    xs = x[0]
    mems = mem[0]
    tgt = loss_target[0]
    me = 4 * lax.axis_index("x") + 2 * lax.axis_index("y") + lax.axis_index("c")

    conv_shapes = [W[n].shape[1:] for n in conv_sharded]
    conv_pack = _pack([W[n][0] for n in conv_sharded])
    shard = {n: W[n][0].astype(_XFER) for n in big}
    h1, (g_in, g_out, g_conv) = _rms_fwd(
        xs, mix_norm_g, "rms1_fwd", comm=_Gather([shard["w_in"], shard["w_out"], conv_pack]))
    convs = [[] for _ in conv_sharded]
    for j in range(_NDEV):
        for idx, part in enumerate(_unpack(g_conv[j], conv_shapes)):
            convs[idx].append(part)
    lcw, ccw, fcw = [jnp.concatenate(parts, axis=-1) for parts in convs]

    wab = jnp.concatenate([_block_diag(lru_w_a[0]), _block_diag(lru_w_x[0])], axis=1).astype(_MXU)
    mixer_params = (lcw, lru_conv_b, wab, lru_b_a, lru_b_x, lru_lambda, ccw, conf_conv_b, conf_ln_g, conf_ln_b)

    w_out_f = g_out.reshape(-1, g_out.shape[-1])
    (z, ycat, hs, cc, x1, h2), (g_q, g_kv, g_o, g_up, g_down) = _mixer_fwd(
        xs, h1, g_in, w_out_f, xa_norm_g, *mixer_params, "mixer_fwd",
        comm=_Gather([shard[n] for n in ("w_q", "w_kv", "w_o", "w_up", "w_down")]))
    w_q_f = g_q.reshape(-1, g_q.shape[-1])
    w_o_f = g_o.reshape(-1, g_o.shape[-1])
    w_down_f = g_down.reshape(-1, g_down.shape[-1])
    row32, row16, vec32 = (_F32, "row"), (_MXU, "row"), (_F32, "vec")
    mn = _rms_fwd(mems, mem_norm_g, "rmsm_fwd")
    kv = _mm_nn_stacked(mn, g_kv, _MXU, "mm_kv_fwd")
    q, o, x2, h3 = _attn_fwd(h2, x1, w_q_f, kv, w_o_f, ffn_norm_g, "attn_fwd")

    gfin = final_norm_g.reshape(1, -1)
    g0, act, gelu_g, u_dgelu, dx3, dx3b, lvec, dg_final = _ffn_fused_fwd(
        h3, g_up, w_down_f, fcw, ffn_conv_b, x2, tgt, gfin, "ffn_fwd")

    def rows8(p):
        return p.reshape(_NDEV, p.shape[0] // _NDEV, p.shape[1])

    p_down = _mm_tn_nat(act, dx3b, _XFER, "mm_down_wgrad", ts=2048)
    (dgu, dx2, dx2b, dg_ffn, dfcw, dfcb), (r_down,) = _ffn_fused_bwd(
        dx3b, g0, gelu_g, u_dgelu, w_down_f, g_up, fcw, x2, ffn_norm_g, dx3, "ffn_bwd",
        comm=_Exchange([rows8(p_down)]))
    p_up = _mm_tn_stacked(h3, dgu, _NDEV, _XFER, "mm_up_wgrad", slabs=2, ts=2048)

    (dx1, dx1b, dg_xa, dk, dv, p_o, p_q), (r_up,) = _attn_bwd(
        dx2b, dx2, q, o, h2, x1, kv, w_o_f, w_q_f, xa_norm_g, "attn_bwd", comm=_Exchange([p_up]))
    dkv = jnp.concatenate([dk, dv], axis=1).astype(_MXU)
    dmn = _mm_nt_stacked(dkv, g_kv, "mm_kv_dgrad", outs=[row32], slabs=_NDEV)
    p_kv = _mm_tn_stacked(mn, dkv, _NDEV, _XFER, "mm_kv_wgrad", slabs=_NDEV)
    _, _, dg_mem = _rms_bwd(dmn, mems, mem_norm_g, None, "rmsm_bwd")

    p_out = _mm_tn_nat(ycat, dx1b, _XFER, "mm_out_wgrad", ts=2048)
    ((dz, p_in, dlcw, dlcb, dwab, dba, dbx, dlam, dccw, dccb, dlng, dlnb),
     (r_o, r_q, r_kv, r_out)) = _mixer_bwd(
        dx1b, w_out_f, h1, z, hs, cc, *mixer_params, "mixer_bwd",
        comm=_Exchange([rows8(p_o), rows8(p_q), p_kv, rows8(p_out)]))

    c = _D_LRU
    heads = lru_w_a.shape[1]
    small_partial = {
        "lru_conv_w": dlcw, "lru_conv_b": dlcb,
        "lru_w_a": _diag_blocks(dwab[:, :c], heads), "lru_b_a": dba,
        "lru_w_x": _diag_blocks(dwab[:, c:], heads), "lru_b_x": dbx, "lru_lambda": dlam,
        "conf_conv_w": dccw, "conf_conv_b": dccb, "conf_ln_g": dlng, "conf_ln_b": dlnb,
        "xa_norm_g": dg_xa, "mem_norm_g": dg_mem, "ffn_norm_g": dg_ffn,
        "ffn_conv_w": dfcw, "ffn_conv_b": dfcb, "final_norm_g": dg_final,
    }
    early = list(small_partial)
    early_shapes = [small_partial[n].shape for n in early] + [lvec.shape]
    (grad_x, dg_mix), (r_in, early_all) = _mm_nt_stacked(
        dz, g_in, "mm_in_dgrad", epi=_epi_rms_bwd, extra=[(xs, "row"), (mix_norm_g, "vec"), (dx1, "row")],
        outs=[row32, vec32], tm=512, slabs=_NDEV,
        comm=_Both(_Exchange([p_in]), _Gather([_pack([small_partial[n] for n in early] + [lvec])])))
    (mix_all,) = _comm_call(_Gather([dg_mix]), "gather_mix_grad")
    small = early + ["mix_norm_g"]
    small_sum = _unpack(_sum8(early_all, "sum_small_grads"), early_shapes)
    loss = 0.5 * jnp.sum(small_sum.pop()) / xs.shape[1]
    small_sum.append(_sum8(mix_all.reshape(_NDEV, 8, -1), "sum_mix_grad").reshape(dg_mix.shape))
    received = {"w_in": r_in, "w_out": r_out, "w_q": r_q, "w_kv": r_kv, "w_o": r_o, "w_up": r_up,
                "w_down": r_down}

    grads, deltas, new_m, new_v = {}, {}, {}, {}
    for n, rec in ((n, received[n]) for n in big):
        shp = W[n].shape
        w2, m2, v2 = (t.reshape(shp[1:]) for t in (W[n], M[n], V[n]))
        outs = _sum_adamw(rec, w2, m2, v2, "adamw_" + n)
        grads[n], deltas[n], new_m[n], new_v[n] = (t.reshape(shp) for t in outs)

    small_g = []
    for n, g in zip(small, small_sum):
        if n in conv_sharded:
            width = W[n].shape[-1]
            g = lax.dynamic_slice_in_dim(g, me * width, width, axis=1)
        small_g.append(g.reshape(W[n].shape))

    def at_least_2d(a):
        return a.reshape(1, -1) if a.ndim == 1 else a

    sd, sm, sv = _adamw_many([at_least_2d(g) for g in small_g], [at_least_2d(W[n]) for n in small],
                             [at_least_2d(M[n]) for n in small], [at_least_2d(V[n]) for n in small], "adamw_small")
    for n, g, d_, m_, v_ in zip(small, small_g, sd, sm, sv):
        shp = W[n].shape
        grads[n], deltas[n], new_m[n], new_v[n] = g, d_.reshape(shp), m_.reshape(shp), v_.reshape(shp)

    return (loss, grad_x[None], *[grads[n] for n in names], *[deltas[n] for n in names],
            *[new_m[n] for n in names], *[new_v[n] for n in names])
```

```python
import functools

import jax
import jax.numpy as jnp
from jax import lax
from jax.experimental import pallas as pl
from jax.experimental.pallas import tpu as pltpu

_MXU = jnp.bfloat16
_XFER = jnp.bfloat16
_F32 = jnp.float32
_EPS = 1e-6
_NDEV = 8
_VMEM_LIMIT = 48 * 1024 * 1024

_D_LRU = 512
_XA_HEADS = 4
_RG_C = 8.0
_ADAM_LR, _ADAM_B1, _ADAM_B2, _ADAM_EPS, _ADAM_WD, _ADAM_STEP = 0.001, 0.9, 0.999, 1e-08, 0.01, 10

_MESH_ID = pl.DeviceIdType.MESH
_ANY = pl.BlockSpec(memory_space=pl.ANY)


def _cparams(*sem, vmem=_VMEM_LIMIT):
    return pltpu.CompilerParams(dimension_semantics=tuple(sem), vmem_limit_bytes=vmem)


def _pcall(body, *, args, grid, in_specs, out_specs, out_shape, sem, name, scratch_shapes=(), comm=None,
           vmem=_VMEM_LIMIT):
    outs_l = list(out_shape) if isinstance(out_shape, (list, tuple)) else [out_shape]
    ospecs_l = list(out_specs) if isinstance(out_specs, (list, tuple)) else [out_specs]
    n_in, n_out, n_scr = len(args), len(outs_l), len(scratch_shapes)
    if comm is None:
        res = pl.pallas_call(
            body, grid=grid, in_specs=list(in_specs), out_specs=ospecs_l, out_shape=outs_l,
            scratch_shapes=list(scratch_shapes), compiler_params=_cparams(*sem, vmem=vmem), name=name)(*args)
        return list(res), []
    n_ci, n_co = len(comm.ins), len(comm.outs)

    def wrapped(*refs):
        ins, cins = refs[:n_in], refs[n_in:n_in + n_ci]
        o = n_in + n_ci
        outs, couts = refs[o:o + n_out], refs[o + n_out:o + n_out + n_co]
        s = o + n_out + n_co
        scr, cscr = refs[s:s + n_scr], refs[s + n_scr:]
        first = pl.program_id(0) == 0
        last = pl.program_id(0) == grid[0] - 1
        for ax in range(1, len(grid)):
            first = jnp.logical_and(first, pl.program_id(ax) == 0)
            last = jnp.logical_and(last, pl.program_id(ax) == grid[ax] - 1)

        @pl.when(first)
        def _():
            comm.start(cins, couts, cscr)

        body(*ins, *outs, *scr)

        @pl.when(last)
        def _():
            comm.finish(cins, couts, cscr)

    res = pl.pallas_call(
        wrapped, grid=grid, in_specs=list(in_specs) + [_ANY] * n_ci, out_specs=ospecs_l + [_ANY] * n_co,
        out_shape=outs_l + list(comm.outs), scratch_shapes=list(scratch_shapes) + list(comm.scratch),
        compiler_params=_cparams(*(("arbitrary",) * len(grid)), vmem=vmem), name=name)(*args, *comm.ins)
    return list(res[:n_out]), list(res[n_out:])


def _sigmoid(v):
    return 1.0 / (1.0 + jnp.exp(-v))


_GELU_C = 0.7978845608028654
_GELU_K = 0.044715


def _gelu(v):
    t = jnp.tanh(_GELU_C * (v + _GELU_K * v * v * v))
    return 0.5 * v * (1.0 + t)


def _gelu_and_grad(v):
    v2 = v * v
    s = 0.5 * jnp.tanh(v * (_GELU_C + (_GELU_C * _GELU_K) * v2)) + 0.5
    g = v * s
    dg = s + (g * (1.0 - s)) * ((2.0 * _GELU_C) + (6.0 * _GELU_C * _GELU_K) * v2)
    return g, dg


def _softplus(v):
    e = jnp.exp(-jnp.abs(v))
    log1p = jnp.where(e < 1e-2, e * (1.0 - e * (0.5 - e * (1.0 / 3.0))), jnp.log(1.0 + e))
    return jnp.maximum(v, 0.0) + log1p


def _neg_expm1(v):
    series = -v * (1.0 + v * (0.5 + v * ((1.0 / 6.0) + v * (1.0 / 24.0))))
    return jnp.where(v > -0.0625, series, 1.0 - jnp.exp(v))


def _dot(a, b, dims):
    return lax.dot_general(a.astype(_MXU), b.astype(_MXU), (dims, ((), ())), preferred_element_type=_F32)


_NN = ((1,), (0,))
_NT = ((1,), (1,))
_TN = ((0,), (0,))


def _scan_fwd(a, b, rows):
    n = a.shape[0]
    d = 1
    while d < n:
        keep = rows >= d
        b = jnp.where(keep, b + a * pltpu.roll(b, d, 0), b)
        a = jnp.where(keep, a * pltpu.roll(a, d, 0), a)
        d *= 2
    return a, b


def _scan_rev(a, b, rows):
    n = a.shape[0]
    d = 1
    while d < n:
        keep = rows < n - d
        b = jnp.where(keep, b + a * pltpu.roll(b, n - d, 0), b)
        a = jnp.where(keep, a * pltpu.roll(a, n - d, 0), a)
        d *= 2
    return a, b


def _colsum(v):
    return jnp.sum(v, axis=0, keepdims=True)


def _mm(a, b, *, dims, grid, a_spec, b_spec, outs, acc_shape, name, extra=(), epi=None, slabs=1, comm=None):
    nred = grid[-1]
    red_axis = len(grid) - 1
    n_ex, n_out = len(extra), len(outs)
    epi = _epi_store if epi is None else epi

    def body(*refs):
        a_ref, b_ref = refs[:2]
        ex, o_refs, acc_ref = refs[2:2 + n_ex], refs[2 + n_ex:2 + n_ex + n_out], refs[-1]
        if slabs == 1:
            p = _dot(a_ref[...], b_ref[...], dims)
        else:
            n = b_ref.shape[-1]
            p = _dot(a_ref[:, 0:n], b_ref[0], dims)
            for jj in range(1, slabs):
                p = p + _dot(a_ref[:, jj * n:(jj + 1) * n], b_ref[jj], dims)

        first_rows = pl.program_id(0) == 0
        if nred == 1:
            epi(p, ex, o_refs, first_rows)
        else:
            k = pl.program_id(red_axis)

            @pl.when(k == 0)
            def _():
                acc_ref[...] = p

            @pl.when(jnp.logical_and(k > 0, k < nred - 1))
            def _():
                acc_ref[...] += p

            @pl.when(k == nred - 1)
            def _():
                epi(acc_ref[...] + p, ex, o_refs, first_rows)

    sem = ("parallel",) * (len(grid) - 1) + ("arbitrary",)
    if any(o[0].shape[0] == 1 for o in outs):
        sem = ("arbitrary",) * len(grid)
    res, cres = _pcall(
        body, args=(a, b) + tuple(e[0] for e in extra), grid=grid,
        in_specs=[a_spec, b_spec] + [e[1] for e in extra],
        out_specs=[o[1] for o in outs], out_shape=[o[0] for o in outs],
        scratch_shapes=[pltpu.VMEM(acc_shape if nred > 1 else (8, 128), _F32)], sem=sem, name=name, comm=comm)
    res = res[0] if n_out == 1 else res
    return res if comm is None else (res, cres)


def _epi_store(total, ex, outs, first_rows):
    outs[0][...] = total.astype(outs[0].dtype)


def _epi_residual_rms(total, ex, outs, first_rows):
    res_ref, g_ref = ex
    xn = total + res_ref[...]
    outs[0][...] = xn
    r = lax.rsqrt(jnp.mean(xn * xn, axis=-1, keepdims=True) + _EPS)
    outs[1][...] = (xn * r * g_ref[...]).astype(outs[1].dtype)


def _epi_rms_bwd(total, ex, outs, first_rows):
    x_ref, g_ref, dres_ref = ex
    dg_ref = outs[-1]
    xv = x_ref[...]
    r = lax.rsqrt(jnp.mean(xv * xv, axis=-1, keepdims=True) + _EPS)
    xhat = xv * r
    dxh = total * g_ref[...]
    dx = dres_ref[...] + r * (dxh - xhat * jnp.mean(dxh * xhat, axis=-1, keepdims=True))
    for o_ref in outs[:-1]:
        o_ref[...] = dx.astype(o_ref.dtype)

    @pl.when(first_rows)
    def _():
        dg_ref[...] = jnp.zeros_like(dg_ref)

    dg_ref[...] += _colsum(total * xhat)


def _epi_final(total, ex, outs, first_rows):
    res_ref, t_ref, g_ref = ex
    dx_ref, dxb_ref, l_ref, dg_ref = outs
    xv = total + res_ref[...]
    gv = g_ref[...]
    d = xv.shape[-1]
    r = lax.rsqrt(jnp.mean(xv * xv, axis=-1, keepdims=True) + _EPS)
    xhat = xv * r
    err = xhat * gv - t_ref[...]
    dy = err * (1.0 / d)
    dxh = dy * gv
    dx = r * (dxh - xhat * jnp.mean(dxh * xhat, axis=-1, keepdims=True))
    dx_ref[...] = dx
    dxb_ref[...] = dx.astype(dxb_ref.dtype)

    @pl.when(first_rows)
    def _():
        l_ref[...] = jnp.zeros_like(l_ref)
        dg_ref[...] = jnp.zeros_like(dg_ref)

    l_ref[...] += _colsum(err * err)
    dg_ref[...] += _colsum(dy * xhat)


def _tile(m, cap):
    t = min(m, cap)
    assert m % t == 0
    return t


def _row_spec(tm, n):
    return pl.BlockSpec((tm, n), lambda i, *_: (i, 0))


def _vec_spec(n):
    return pl.BlockSpec((1, n), lambda *_: (0, 0))


def _row_io(m, n, tm, extra, outs):
    def spec(kind):
        return _row_spec(tm, n) if kind == "row" else _vec_spec(n)

    ex = [(arr, spec(kind)) for arr, kind in extra]
    os_ = [(jax.ShapeDtypeStruct((m, n) if kind == "row" else (1, n), dt), spec(kind)) for dt, kind in outs]
    return ex, os_


def _mm_nn_stacked(a, w, out_dtype, name, comm=None, tm=1024):
    m, k = a.shape
    j, _, n = w.shape
    tm = _tile(m, tm)
    return _mm(a, w, dims=_NN, grid=(m // tm, j, 1),
               a_spec=pl.BlockSpec((tm, k), lambda i, jj, r: (i, 0)),
               b_spec=pl.BlockSpec((None, k, n), lambda i, jj, r: (jj, 0, 0)),
               outs=[(jax.ShapeDtypeStruct((m, j * n), out_dtype), pl.BlockSpec((tm, n), lambda i, jj, r: (i, jj)))],
               acc_shape=(tm, n), name=name, comm=comm)


def _mm_nt_stacked(dc, w, name, *, outs, extra=(), epi=None, comm=None, tm=1024, slabs=1):
    m = dc.shape[0]
    j, k, n = w.shape
    tm = _tile(m, tm)
    assert j % slabs == 0
    ex, os_ = _row_io(m, k, tm, extra, outs)
    wblk = (None, k, n) if slabs == 1 else (slabs, k, n)
    return _mm(dc, w, dims=_NT, grid=(m // tm, j // slabs),
               a_spec=pl.BlockSpec((tm, slabs * n), lambda i, r: (i, r)),
               b_spec=pl.BlockSpec(wblk, lambda i, r: (r, 0, 0)),
               outs=os_, extra=ex, epi=epi, acc_shape=(tm, k), name=name, slabs=slabs, comm=comm)


def _mm_tn_stacked(a, dc, j, out_dtype, name, slabs=1, ts=1024):
    s, k = a.shape
    n = dc.shape[1] // j
    ts = _tile(s, ts)
    assert j % slabs == 0

    def epi(total, ex, outs, first_rows):
        for jj in range(slabs):
            outs[0][jj] = total[:, jj * n:(jj + 1) * n].astype(outs[0].dtype)

    return _mm(a, dc, dims=_TN, grid=(j // slabs, s // ts),
               a_spec=pl.BlockSpec((ts, k), lambda jj, r: (r, 0)),
               b_spec=pl.BlockSpec((ts, slabs * n), lambda jj, r: (r, jj)),
               outs=[(jax.ShapeDtypeStruct((j, k, n), out_dtype),
                      pl.BlockSpec((slabs, k, n), lambda jj, r: (jj, 0, 0)))],
               epi=epi, acc_shape=(k, slabs * n), name=name)


def _mm_tn_nat(a, dc, out_dtype, name, ts=1024):
    s, kt = a.shape
    n = dc.shape[1]
    ts = _tile(s, ts)
    tkb = _tile(kt, 1024)
    return _mm(a, dc, dims=_TN, grid=(kt // tkb, s // ts),
               a_spec=pl.BlockSpec((ts, tkb), lambda kb, r: (r, kb)),
               b_spec=pl.BlockSpec((ts, n), lambda kb, r: (r, 0)),
               outs=[(jax.ShapeDtypeStruct((kt, n), out_dtype), pl.BlockSpec((tkb, n), lambda kb, r: (kb, 0)))],
               acc_shape=(tkb, n), name=name)


def _rms_fwd(x, g, name, comm=None):
    s, d = x.shape
    t = _tile(s, 1024)

    def body(x_ref, g_ref, h_ref):
        xv = x_ref[...]
        r = lax.rsqrt(jnp.mean(xv * xv, axis=-1, keepdims=True) + _EPS)
        h_ref[...] = (xv * r * g_ref[...]).astype(h_ref.dtype)

    res, cres = _pcall(
        body, args=(x, g), grid=(s // t,),
        in_specs=[pl.BlockSpec((t, d), lambda i: (i, 0)), pl.BlockSpec((1, d), lambda i: (0, 0))],
        out_specs=pl.BlockSpec((t, d), lambda i: (i, 0)),
        out_shape=jax.ShapeDtypeStruct((s, d), _MXU), sem=("parallel",), name=name, comm=comm)
    return res[0] if comm is None else (res[0], cres)


def _rms_bwd(dh, x, g, dres, name):
    s, d = x.shape
    t = _tile(s, 256)
    has_res = dres is not None

    def body(*refs):
        if has_res:
            dh_ref, x_ref, g_ref, dres_ref, dx_ref, dxb_ref, dg_ref = refs
        else:
            dh_ref, x_ref, g_ref, dx_ref, dxb_ref, dg_ref = refs
        xv = x_ref[...]
        dhv = dh_ref[...]
        r = lax.rsqrt(jnp.mean(xv * xv, axis=-1, keepdims=True) + _EPS)
        xhat = xv * r
        dxh = dhv * g_ref[...]
        dx = r * (dxh - xhat * jnp.mean(dxh * xhat, axis=-1, keepdims=True))
        if has_res:
            dx = dx + dres_ref[...]
        dx_ref[...] = dx
        dxb_ref[...] = dx.astype(dxb_ref.dtype)

        @pl.when(pl.program_id(0) == 0)
        def _():
            dg_ref[...] = jnp.zeros_like(dg_ref)

        dg_ref[...] += _colsum(dhv * xhat)

    row = pl.BlockSpec((t, d), lambda i: (i, 0))
    vec = pl.BlockSpec((1, d), lambda i: (0, 0))
    in_specs = [row, row, vec] + ([row] if has_res else [])
    args = (dh, x, g) + ((dres,) if has_res else ())
    return pl.pallas_call(
        body, grid=(s // t,), in_specs=in_specs, out_specs=[row, row, vec],
        out_shape=[jax.ShapeDtypeStruct((s, d), _F32), jax.ShapeDtypeStruct((s, d), _MXU),
                   jax.ShapeDtypeStruct((1, d), _F32)],
        compiler_params=_cparams("arbitrary"), name=name,
    )(*args)


_LRU_K = 4
_CONF_K = 31
_LRU_HALO = 8
_CONF_HALO = 32
_MIX_T = 512


def _lru_gates(lx, wab_ref, ba_ref, bx_ref, lam_ref):
    c = _D_LRU
    pre = _dot(lx, wab_ref[...], _NN)
    r = _sigmoid(pre[:, :c] + ba_ref[...])
    ig = _sigmoid(pre[:, c:] + bx_ref[...])
    sp = _softplus(-lam_ref[...])
    log_a = (-_RG_C) * r * sp
    a = jnp.exp(log_a)
    mult = jnp.sqrt(_neg_expm1(2.0 * log_a))
    return r, ig, sp, a, mult


def _causal_conv(ext_ref, halo, w_ref, b_ref, taps, t):
    acc = b_ref[...] + w_ref[0:1, :] * ext_ref[pl.ds(halo - (taps - 1), t), :]
    for k in range(1, taps):
        acc = acc + w_ref[k:k + 1, :] * ext_ref[pl.ds(halo - (taps - 1) + k, t), :]
    return acc


class _Windows:
    def __init__(self, ext_ref, shifted_ref, t):
        self.ext_ref, self.shifted_ref, self.t = ext_ref, shifted_ref, t
        rows = t + 24
        for r in range(1, 8):
            shifted_ref[r - 1, 0:rows, :] = ext_ref[pl.ds(r, rows), :]

    def __call__(self, off):
        q, r = divmod(off, 8)
        if r == 0:
            return self.ext_ref[pl.ds(8 * q, self.t), :]
        return self.shifted_ref[r - 1, pl.ds(8 * q, self.t), :]


def _mixer_fwd(xs, h1, w_in, w_out, gnorm, lcw, lcb, wab, ba, bx, lam, ccw, ccb, lng, lnb, name, comm=None):
    s, d = xs.shape
    nblk, _, n = w_in.shape
    c = _D_LRU
    t = _tile(s, _MIX_T)
    nt = s // t

    def body(x_ref, h_ref, win_ref, wout_ref, gn_ref,
             lcw_ref, lcb_ref, wab_ref, ba_ref, bx_ref, lam_ref, ccw_ref, ccb_ref, lng_ref, lnb_ref,
             z_ref, ycat_ref, hs_ref, cc_ref, x1_ref, h2_ref,
             ext_ref, cge_ref, hc_ref, shifted_ref, zprev_ref):
        i = pl.program_id(0)
        first = i == 0
        rows = lax.broadcasted_iota(jnp.int32, (t, c), 0)

        @pl.when(first)
        def _():
            zprev_ref[...] = jnp.zeros_like(zprev_ref)

        hv = h_ref[...]
        for j in range(nblk):
            z_ref[:, j * n:(j + 1) * n] = _dot(hv, win_ref[j], _NN)
        lx0_ref, gate_ref = z_ref.at[:, 0:c], z_ref.at[:, c:2 * c]
        ca_ref, cb_ref = z_ref.at[:, 2 * c:3 * c], z_ref.at[:, 3 * c:4 * c]
        lx0h_ref = zprev_ref.at[_CONF_HALO - _LRU_HALO:_CONF_HALO, 0:c]
        cah_ref, cbh_ref = zprev_ref.at[:, 2 * c:3 * c], zprev_ref.at[:, 3 * c:4 * c]

        ext_ref[0:_LRU_HALO, :] = jnp.where(first, 0.0, lx0h_ref[...])
        ext_ref[_LRU_HALO:_LRU_HALO + t, :] = lx0_ref[...]
        lx = _causal_conv(ext_ref, _LRU_HALO, lcw_ref, lcb_ref, _LRU_K, t)
        r, ig, sp, a, mult = _lru_gates(lx, wab_ref, ba_ref, bx_ref, lam_ref)
        u = mult * (ig * lx)
        a_cum, h_loc = _scan_fwd(a, u, rows)

        @pl.when(first)
        def _():
            hc_ref[...] = jnp.zeros_like(hc_ref)

        h = h_loc + a_cum * hc_ref[7:8, :]
        hs_ref[...] = h
        hc_ref[...] = hs_ref[pl.ds(t - 8, 8), :]
        ycat_ref[:, 0:c] = (h * _gelu(gate_ref[...])).astype(ycat_ref.dtype)

        cge_ref[0:_CONF_HALO, :] = jnp.where(first, 0.0, cah_ref[...] * _sigmoid(cbh_ref[...]))
        cge_ref[_CONF_HALO:_CONF_HALO + t, :] = ca_ref[...] * _sigmoid(cb_ref[...])
        win = _Windows(cge_ref, shifted_ref, t)
        first_off = _CONF_HALO - (_CONF_K - 1)
        cc = ccb_ref[...] + ccw_ref[0:1, :] * win(first_off)
        for k in range(1, _CONF_K):
            cc = cc + ccw_ref[k:k + 1, :] * win(first_off + k)
        cc_ref[...] = cc
        xc = cc - jnp.mean(cc, axis=-1, keepdims=True)
        rstd = lax.rsqrt(jnp.mean(xc * xc, axis=-1, keepdims=True) + _EPS)
        ln = xc * rstd * lng_ref[...] + lnb_ref[...]
        ycat_ref[:, c:2 * c] = (ln * _sigmoid(ln)).astype(ycat_ref.dtype)

        zprev_ref[...] = z_ref[pl.ds(t - _CONF_HALO, _CONF_HALO), :]
        y = _dot(ycat_ref[...], wout_ref[...], _NN)
        _epi_residual_rms(y, (x_ref, gn_ref), (x1_ref, h2_ref), first)

    def const(arr):
        return pl.BlockSpec(arr.shape, lambda i: (0,) * arr.ndim, pipeline_mode=pl.Buffered(1))

    def rows_of(width):
        return pl.BlockSpec((t, width), lambda i: (i, 0))

    params = (lcw, lcb, wab, ba, bx, lam, ccw, ccb, lng, lnb)
    res, cres = _pcall(
        body, args=(xs, h1, w_in, w_out, gnorm, *params), grid=(nt,),
        in_specs=[rows_of(d), rows_of(d), const(w_in), const(w_out), const(gnorm)] + [const(p) for p in params],
        out_specs=[rows_of(nblk * n), rows_of(2 * c), rows_of(c), rows_of(c), rows_of(d), rows_of(d)],
        out_shape=[jax.ShapeDtypeStruct((s, nblk * n), _F32), jax.ShapeDtypeStruct((s, 2 * c), _MXU),
                   jax.ShapeDtypeStruct((s, c), _F32), jax.ShapeDtypeStruct((s, c), _F32),
                   jax.ShapeDtypeStruct((s, d), _F32), jax.ShapeDtypeStruct((s, d), _MXU)],
        scratch_shapes=[pltpu.VMEM((t + _LRU_HALO, c), _F32), pltpu.VMEM((t + _CONF_HALO, c), _F32),
                        pltpu.VMEM((8, c), _F32), pltpu.VMEM((7, t + _CONF_HALO, c), _F32),
                        pltpu.VMEM((_CONF_HALO, nblk * n), _F32)],
        sem=("arbitrary",), name=name, comm=comm)
    return res, cres


def _mixer_bwd(dx1b, w_out, h1, z, hs, cc, lcw, lcb, wab, ba, bx, lam, ccw, ccb, lng, lnb, name, comm=None):
    s = z.shape[0]
    d = h1.shape[1]
    c = _D_LRU
    t = _tile(s, _MIX_T)
    nt = s // t
    nblk = z.shape[1] // 256

    def body(dxb_ref, wout_ref, h1_ref, lx0_ref, lx0h_ref, gate_ref, ca_ref, cah_ref, cb_ref, cbh_ref,
             hs_ref, hsh_ref, cc_ref,
             lcw_ref, lcb_ref, wab_ref, ba_ref, bx_ref, lam_ref, ccw_ref, ccb_ref, lng_ref, lnb_ref,
             dz_ref, pin_ref, dlcw_ref, dlcb_ref, dwab_ref, dba_ref, dbx_ref, dlam_ref, dccw_ref, dccb_ref, dlng_ref,
             dlnb_ref,
             ext_ref, up_ref, cge_ref, dce_ref, xc_ref, dlxc_ref, dccc_ref, shifted_ref, dwin_ref):
        i = pl.program_id(0)
        first_tile = i == nt - 1
        last_tile = i == 0
        rows = lax.broadcasted_iota(jnp.int32, (t, c), 0)

        @pl.when(last_tile)
        def _():
            for ref in (dlcw_ref, dlcb_ref, dwab_ref, dba_ref, dbx_ref, dlam_ref, dccw_ref, dccb_ref, dlng_ref,
                        dlnb_ref, xc_ref, dlxc_ref, dccc_ref, dwin_ref):
                ref[...] = jnp.zeros_like(ref)

        dycat = _dot(dxb_ref[...], wout_ref[...], _NT)

        ext_ref[0:_LRU_HALO, :] = jnp.where(first_tile, 0.0, lx0h_ref[...])
        ext_ref[_LRU_HALO:_LRU_HALO + t, :] = lx0_ref[...]
        lx = _causal_conv(ext_ref, _LRU_HALO, lcw_ref, lcb_ref, _LRU_K, t)
        r, ig, sp, a, mult = _lru_gates(lx, wab_ref, ba_ref, bx_ref, lam_ref)
        h = hs_ref[...]
        gl, dgl = _gelu_and_grad(gate_ref[...])
        dyl = dycat[:, 0:c]
        dz_ref[:, c:2 * c] = (dyl * h * dgl).astype(dz_ref.dtype)
        dh = dyl * gl

        up_ref[0:t, :] = a
        up_ref[t:t + 8, :] = jnp.ones((8, c), _F32)
        a_up = up_ref[pl.ds(1, t), :]
        a_cum, g_loc = _scan_rev(a_up, dh, rows)
        gt = g_loc + a_cum * xc_ref[0:1, :]
        xc_ref[...] = (a * gt)[0:8, :]

        up_ref[0:8, :] = jnp.where(first_tile, 0.0, hsh_ref[...])
        up_ref[8:8 + t, :] = h
        hprev = up_ref[pl.ds(7, t), :]

        da = gt * hprev
        dmult = gt * ig * lx
        dig = gt * mult * lx
        dlx = gt * mult * ig
        dlog_a = da * a - dmult * a * a / mult
        dpre_r = dlog_a * (-_RG_C) * sp * r * (1.0 - r)
        dpre_i = dig * ig * (1.0 - ig)
        dlam_ref[...] += _colsum(dlog_a * r) * (_RG_C * _sigmoid(-lam_ref[...]))
        dba_ref[...] += _colsum(dpre_r)
        dbx_ref[...] += _colsum(dpre_i)
        dpre = jnp.concatenate([dpre_r, dpre_i], axis=1).astype(_MXU)
        dlx = dlx + _dot(dpre, wab_ref[...], _NT)
        dwab_ref[...] += _dot(lx, dpre, _TN)

        dlcb_ref[...] += _colsum(dlx)
        up_ref[0:t, :] = dlx
        up_ref[t:t + 8, :] = dlxc_ref[...]
        dlxc_ref[...] = dlx[0:8, :]
        acc = lcw_ref[0:1, :] * up_ref[pl.ds(_LRU_K - 1, t), :]
        for k in range(1, _LRU_K):
            acc = acc + lcw_ref[k:k + 1, :] * up_ref[pl.ds(_LRU_K - 1 - k, t), :]
        dz_ref[:, 0:c] = acc.astype(dz_ref.dtype)
        for k in range(_LRU_K):
            dlcw_ref[k:k + 1, :] += _colsum(dlx * ext_ref[pl.ds(_LRU_HALO - (_LRU_K - 1) + k, t), :])

        sig_b = _sigmoid(cb_ref[...])
        ca = ca_ref[...]
        cge_ref[0:_CONF_HALO, :] = jnp.where(first_tile, 0.0, cah_ref[...] * _sigmoid(cbh_ref[...]))
        cge_ref[_CONF_HALO:_CONF_HALO + t, :] = ca * sig_b
        ccv = cc_ref[...]
        xcen = ccv - jnp.mean(ccv, axis=-1, keepdims=True)
        rstd = lax.rsqrt(jnp.mean(xcen * xcen, axis=-1, keepdims=True) + _EPS)
        xn = xcen * rstd
        ln = xn * lng_ref[...] + lnb_ref[...]
        sg = _sigmoid(ln)
        dln = dycat[:, c:2 * c] * (sg * (1.0 + ln * (1.0 - sg)))
        dlng_ref[...] += _colsum(dln * xn)
        dlnb_ref[...] += _colsum(dln)
        dxn = dln * lng_ref[...]
        dcc = rstd * (dxn - jnp.mean(dxn, axis=-1, keepdims=True)
                      - xn * jnp.mean(dxn * xn, axis=-1, keepdims=True))
        dccb_ref[...] += _colsum(dcc)
        win = _Windows(cge_ref, shifted_ref, t)
        for k in range(_CONF_K):
            dccw_ref[k:k + 1, :] += _colsum(dcc * win(_CONF_HALO - (_CONF_K - 1) + k))
        dce_ref[0:t, :] = dcc
        dce_ref[t:t + _CONF_HALO, :] = dccc_ref[...]
        dccc_ref[...] = dcc[0:_CONF_HALO, :]
        win = _Windows(dce_ref, shifted_ref, t)
        dcg = ccw_ref[0:1, :] * win(_CONF_K - 1)
        for k in range(1, _CONF_K):
            dcg = dcg + ccw_ref[k:k + 1, :] * win(_CONF_K - 1 - k)
        dz_ref[:, 2 * c:3 * c] = (dcg * sig_b).astype(dz_ref.dtype)
        dz_ref[:, 3 * c:4 * c] = (dcg * ca * sig_b * (1.0 - sig_b)).astype(dz_ref.dtype)

        dwin_ref[...] += _dot(h1_ref[...], dz_ref[...], _TN)

        @pl.when(first_tile)
        def _():
            for j in range(nblk):
                pin_ref[j] = dwin_ref[:, j * 256:(j + 1) * 256].astype(pin_ref.dtype)

    def col(j):
        return pl.BlockSpec((t, c), lambda i: (nt - 1 - i, j))

    def halo(j, rows_):
        per = t // rows_
        return pl.BlockSpec((rows_, c), lambda i: (jnp.maximum((nt - 1 - i) * per - 1, 0), j))

    def full(shape):
        return pl.BlockSpec(shape, lambda i: (0,) * len(shape))

    params = (lcw, lcb, wab, ba, bx, lam, ccw, ccb, lng, lnb)
    small = [(_LRU_K, c), (1, c), (c, 2 * c), (1, c), (1, c), (1, c), (_CONF_K, c), (1, c), (1, c), (1, c)]
    wide = pl.BlockSpec((t, d), lambda i: (nt - 1 - i, 0))
    pin_shape = (nblk, d, 256)
    return _pcall(
        body, args=(dx1b, w_out, h1, z, z, z, z, z, z, z, hs, hs, cc, *params), grid=(nt,),
        in_specs=[wide, pl.BlockSpec(w_out.shape, lambda i: (0, 0), pipeline_mode=pl.Buffered(1)), wide,
                  col(0), halo(0, _LRU_HALO), col(1), col(2), halo(2, _CONF_HALO), col(3), halo(3, _CONF_HALO),
                  col(0), halo(0, 8), col(0)]
        + [full(p.shape) for p in params],
        out_specs=[pl.BlockSpec((t, 4 * c), lambda i: (nt - 1 - i, 0)), full(pin_shape)] + [full(sh) for sh in small],
        out_shape=[jax.ShapeDtypeStruct((s, 4 * c), _MXU), jax.ShapeDtypeStruct(pin_shape, _XFER)]
        + [jax.ShapeDtypeStruct(sh, _F32) for sh in small],
        scratch_shapes=[pltpu.VMEM((t + _LRU_HALO, c), _F32), pltpu.VMEM((t + 8, c), _F32),
                        pltpu.VMEM((t + _CONF_HALO, c), _F32), pltpu.VMEM((t + _CONF_HALO, c), _F32),
                        pltpu.VMEM((8, c), _F32), pltpu.VMEM((8, c), _F32), pltpu.VMEM((_CONF_HALO, c), _F32),
                        pltpu.VMEM((7, t + _CONF_HALO, c), _F32), pltpu.VMEM((d, 4 * c), _F32)],
        sem=("arbitrary",), name=name, comm=comm, vmem=_VMEM_LIMIT_FUSED)


_ATT_T = 512


def _attn_probs(qh, kh, scale):
    sc = _dot(qh, kh, _NT) * scale
    e = jnp.exp(sc - jnp.max(sc, axis=-1, keepdims=True))
    return e / jnp.sum(e, axis=-1, keepdims=True)


def _const_spec(arr):
    return pl.BlockSpec(arr.shape, lambda i: (0,) * arr.ndim, pipeline_mode=pl.Buffered(1))


def _attn_fwd(h2, x1, w_q, kv, w_o, gnorm, name):
    s, d = h2.shape
    nm = kv.shape[0]
    hd = d // _XA_HEADS
    t = _tile(s, _ATT_T)
    scale = hd ** -0.5

    def body(h_ref, x1_ref, wq_ref, k_ref, v_ref, wo_ref, g_ref, q_ref, o_ref, x2_ref, h3_ref):
        q_ref[...] = _dot(h_ref[...], wq_ref[...], _NN).astype(q_ref.dtype)
        for hh in range(_XA_HEADS):
            sl = slice(hh * hd, (hh + 1) * hd)
            p = _attn_probs(q_ref[:, sl], k_ref[:, sl], scale)
            o_ref[:, sl] = _dot(p, v_ref[:, sl], _NN).astype(o_ref.dtype)
        y = _dot(o_ref[...], wo_ref[...], _NN)
        _epi_residual_rms(y, (x1_ref, g_ref), (x2_ref, h3_ref), None)

    row = pl.BlockSpec((t, d), lambda i: (i, 0))
    half = pl.BlockSpec((nm, d), lambda i: (0, 0), pipeline_mode=pl.Buffered(1))
    half2 = pl.BlockSpec((nm, d), lambda i: (0, 1), pipeline_mode=pl.Buffered(1))
    return pl.pallas_call(
        body, grid=(s // t,),
        in_specs=[row, row, _const_spec(w_q), half, half2, _const_spec(w_o), _const_spec(gnorm)],
        out_specs=[row, row, row, row],
        out_shape=[jax.ShapeDtypeStruct((s, d), _MXU), jax.ShapeDtypeStruct((s, d), _MXU),
                   jax.ShapeDtypeStruct((s, d), _F32), jax.ShapeDtypeStruct((s, d), _MXU)],
        compiler_params=_cparams("parallel"), name=name,
    )(h2, x1, w_q, kv, kv, w_o, gnorm)


def _attn_bwd(dx2b, dx2, q, o, h2, x1, kv, w_o, w_q, gnorm, name, comm=None):
    s, d = q.shape
    nm = kv.shape[0]
    hd = d // _XA_HEADS
    t = _tile(s, _ATT_T)
    nt = s // t
    scale = hd ** -0.5

    def body(dxb_ref, dx2_ref, q_ref, o_ref, h_ref, x1_ref, k_ref, v_ref, wo_ref, wq_ref, g_ref,
             dx1_ref, dx1b_ref, dgn_ref, dk_ref, dv_ref, pwo_ref, pwq_ref, dq_ref, awo_ref, awq_ref):
        i = pl.program_id(0)
        first = i == 0

        @pl.when(first)
        def _():
            for ref in (dk_ref, dv_ref, awo_ref, awq_ref):
                ref[...] = jnp.zeros_like(ref)

        dxb = dxb_ref[...]
        do = _dot(dxb, wo_ref[...], _NT).astype(_MXU)
        awo_ref[...] += _dot(o_ref[...], dxb, _TN)
        for hh in range(_XA_HEADS):
            sl = slice(hh * hd, (hh + 1) * hd)
            qh = q_ref[:, sl]
            kh = k_ref[:, sl]
            doh = do[:, sl]
            p = _attn_probs(qh, kh, scale)
            dp = _dot(doh, v_ref[:, sl], _NT)
            dv_ref[:, sl] += _dot(p, doh, _TN)
            ds = (p * (dp - jnp.sum(dp * p, axis=-1, keepdims=True)) * scale).astype(_MXU)
            dq_ref[:, sl] = _dot(ds, kh, _NN).astype(dq_ref.dtype)
            dk_ref[:, sl] += _dot(ds, qh, _TN)
        dq = dq_ref[...]
        awq_ref[...] += _dot(h_ref[...], dq, _TN)
        dh = _dot(dq, wq_ref[...], _NT)
        _epi_rms_bwd(dh, (x1_ref, g_ref, dx2_ref), (dx1_ref, dx1b_ref, dgn_ref), first)

        @pl.when(i == nt - 1)
        def _():
            pwo_ref[...] = awo_ref[...].astype(pwo_ref.dtype)
            pwq_ref[...] = awq_ref[...].astype(pwq_ref.dtype)

    row = pl.BlockSpec((t, d), lambda i: (i, 0))
    vec = pl.BlockSpec((1, d), lambda i: (0, 0))
    mem_blk = pl.BlockSpec((nm, d), lambda i: (0, 0))
    sq = pl.BlockSpec((d, d), lambda i: (0, 0))
    half = pl.BlockSpec((nm, d), lambda i: (0, 0), pipeline_mode=pl.Buffered(1))
    half2 = pl.BlockSpec((nm, d), lambda i: (0, 1), pipeline_mode=pl.Buffered(1))
    return _pcall(
        body, args=(dx2b, dx2, q, o, h2, x1, kv, kv, w_o, w_q, gnorm), grid=(nt,),
        in_specs=[row, row, row, row, row, row, half, half2, _const_spec(w_o), _const_spec(w_q), _const_spec(gnorm)],
        out_specs=[row, row, vec, mem_blk, mem_blk, sq, sq],
        out_shape=[jax.ShapeDtypeStruct((s, d), _F32), jax.ShapeDtypeStruct((s, d), _MXU),
                   jax.ShapeDtypeStruct((1, d), _F32), jax.ShapeDtypeStruct((nm, d), _F32),
                   jax.ShapeDtypeStruct((nm, d), _F32), jax.ShapeDtypeStruct((d, d), _XFER),
                   jax.ShapeDtypeStruct((d, d), _XFER)],
        scratch_shapes=[pltpu.VMEM((t, d), _MXU), pltpu.VMEM((d, d), _F32), pltpu.VMEM((d, d), _F32)],
        sem=("arbitrary",), name=name, comm=comm, vmem=_VMEM_LIMIT_FUSED)


_FFN_K = 3
_FFN_FUSED_T = 256
_VMEM_LIMIT_FUSED = 58 * 1024 * 1024


def _ffn_fused_fwd(h3, w_up, w_down, fcw, fcb, x2, target, gfin, name):
    s, d = h3.shape
    nblk, _, n = w_up.shape
    half = nblk // 2
    f = half * n
    t = _tile(s, _FFN_FUSED_T)

    def body(h_ref, wup_ref, wdown_ref, w_ref, b_ref, x2_ref, t_ref, g_ref,
             g0_ref, act_ref, gl_ref, udgl_ref, dx_ref, dxb_ref, l_ref, dg_ref, ext0_ref, ext1_ref, halo_ref):
        i = pl.program_id(0)
        first = i == 0
        h = h_ref[...]
        total = None
        ahead = (_dot(h, wup_ref[0], _NN), _dot(h, wup_ref[half], _NN))
        for j in range(half):
            cs = slice(j * n, (j + 1) * n)
            ext_ref = ext0_ref if j % 2 == 0 else ext1_ref
            g0, u = ahead
            if j + 1 < half:
                ahead = (_dot(h, wup_ref[j + 1], _NN), _dot(h, wup_ref[half + j + 1], _NN))
            if j > 0:
                prev = slice((j - 1) * n, j * n)
                p = _dot(act_ref[:, prev], wdown_ref[prev, :], _NN)
                total = p if total is None else total + p
            g0_ref[:, cs] = g0.astype(g0_ref.dtype)
            ext_ref[0:8, :] = jnp.where(first, 0.0, halo_ref[:, cs])
            ext_ref[8:8 + t, :] = g0
            halo_ref[:, cs] = g0[t - 8:t, :]
            g = _causal_conv(ext_ref, 8, w_ref.at[:, cs], b_ref.at[:, cs], _FFN_K, t)
            gl, dgl = _gelu_and_grad(g)
            gl_ref[:, cs] = gl.astype(gl_ref.dtype)
            udgl_ref[:, cs] = (u * dgl).astype(udgl_ref.dtype)
            act_ref[:, cs] = (gl * u).astype(act_ref.dtype)
        last = slice((half - 1) * n, half * n)
        total = total + _dot(act_ref[:, last], wdown_ref[last, :], _NN)
        _epi_final(total, (x2_ref, t_ref, g_ref), (dx_ref, dxb_ref, l_ref, dg_ref), first)

    def const(shape):
        return pl.BlockSpec(shape, lambda i: (0,) * len(shape), pipeline_mode=pl.Buffered(1))

    row = pl.BlockSpec((t, d), lambda i: (i, 0))
    vec = pl.BlockSpec((1, d), lambda i: (0, 0))
    return pl.pallas_call(
        body, grid=(s // t,),
        in_specs=[row, const(w_up.shape), const(w_down.shape), const(fcw.shape), const(fcb.shape), row, row, vec],
        out_specs=[pl.BlockSpec((t, f), lambda i: (i, 0))] * 4 + [row, row, vec, vec],
        out_shape=[jax.ShapeDtypeStruct((s, f), _MXU)] * 4
        + [jax.ShapeDtypeStruct((s, d), _F32), jax.ShapeDtypeStruct((s, d), _MXU),
                   jax.ShapeDtypeStruct((1, d), _F32), jax.ShapeDtypeStruct((1, d), _F32)],
        scratch_shapes=[pltpu.VMEM((t + 8, n), _F32), pltpu.VMEM((t + 8, n), _F32), pltpu.VMEM((8, f), _F32)],
        compiler_params=_cparams("arbitrary", vmem=_VMEM_LIMIT_FUSED), name=name,
    )(h3, w_up, w_down, fcw, fcb, x2, target, gfin)


def _ffn_fused_bwd(dx3b, g0, gl, udgl, w_down, w_up, fcw, x2, gnorm, dx3, name, comm=None):
    s, d = x2.shape
    nblk, _, n = w_up.shape
    half = nblk // 2
    f = half * n
    t = _tile(s, _FFN_FUSED_T)
    nt = s // t
    hrows = 16

    def body(dxb_ref, g0_ref, g0h_ref, gl_ref, udgl_ref, wdown_ref, wup_ref, w_ref, x2_ref, g_ref, dx3_ref,
             dgu_ref, dx2_ref, dx2b_ref, dgn_ref, dw_ref, db_ref, ext0_ref, ext1_ref, up0_ref, up1_ref, car_ref):
        i = pl.program_id(0)
        first_tile = i == nt - 1
        last_tile = i == 0

        @pl.when(last_tile)
        def _():
            dw_ref[...] = jnp.zeros_like(dw_ref)
            db_ref[...] = jnp.zeros_like(db_ref)
            car_ref[...] = jnp.zeros_like(car_ref)

        dxb = dxb_ref[...]
        total = None
        for j in range(half):
            cs = slice(j * n, (j + 1) * n)
            us = slice(f + j * n, f + (j + 1) * n)
            ext_ref = ext0_ref if j % 2 == 0 else ext1_ref
            up_ref = up0_ref if j % 2 == 0 else up1_ref
            dact = _dot(dxb, wdown_ref[cs, :], _NT)
            ext_ref[0:8, :] = jnp.where(first_tile, 0.0, g0h_ref[:, cs].astype(_F32)[hrows - 8:hrows])
            ext_ref[8:8 + t, :] = g0_ref[:, cs].astype(_F32)
            du = (dact * gl_ref[:, cs].astype(_F32)).astype(dgu_ref.dtype)
            dgu_ref[:, us] = du
            dg = dact * udgl_ref[:, cs].astype(_F32)
            db_ref[:, cs] += _colsum(dg)
            for k in range(_FFN_K):
                dw_ref[k:k + 1, cs] += _colsum(dg * ext_ref[pl.ds(8 - (_FFN_K - 1) + k, t), :])
            up_ref[0:t, :] = dg
            up_ref[t:t + 8, :] = car_ref[:, cs]
            car_ref[:, cs] = dg[0:8, :]
            dg0 = w_ref[0:1, cs] * up_ref[pl.ds(_FFN_K - 1, t), :]
            for k in range(1, _FFN_K):
                dg0 = dg0 + w_ref[k:k + 1, cs] * up_ref[pl.ds(_FFN_K - 1 - k, t), :]
            dg0 = dg0.astype(dgu_ref.dtype)
            dgu_ref[:, cs] = dg0
            p = _dot(dg0, wup_ref[j], _NT) + _dot(du, wup_ref[half + j], _NT)
            total = p if total is None else total + p
        _epi_rms_bwd(total, (x2_ref, g_ref, dx3_ref), (dx2_ref, dx2b_ref, dgn_ref), last_tile)

    def const(shape):
        return pl.BlockSpec(shape, lambda i: (0,) * len(shape), pipeline_mode=pl.Buffered(1))

    row = pl.BlockSpec((t, d), lambda i: (nt - 1 - i, 0))
    vec = pl.BlockSpec((1, d), lambda i: (0, 0))
    per = t // hrows
    wide = pl.BlockSpec((t, f), lambda i: (nt - 1 - i, 0))
    return _pcall(
        body, args=(dx3b, g0, g0, gl, udgl, w_down, w_up, fcw, x2, gnorm, dx3), grid=(nt,),
        in_specs=[row, wide, pl.BlockSpec((hrows, f), lambda i: (jnp.maximum((nt - 1 - i) * per - 1, 0), 0)),
                  wide, wide, const(w_down.shape), const(w_up.shape), const(fcw.shape), row, vec, row],
        out_specs=[pl.BlockSpec((t, 2 * f), lambda i: (nt - 1 - i, 0)), row, row, vec,
                   pl.BlockSpec((_FFN_K, f), lambda i: (0, 0)), pl.BlockSpec((1, f), lambda i: (0, 0))],
        out_shape=[jax.ShapeDtypeStruct((s, 2 * f), _MXU), jax.ShapeDtypeStruct((s, d), _F32),
                   jax.ShapeDtypeStruct((s, d), _MXU), jax.ShapeDtypeStruct((1, d), _F32),
                   jax.ShapeDtypeStruct((_FFN_K, f), _F32), jax.ShapeDtypeStruct((1, f), _F32)],
        scratch_shapes=[pltpu.VMEM((t + 8, n), _F32), pltpu.VMEM((t + 8, n), _F32),
                        pltpu.VMEM((t + 8, n), _F32), pltpu.VMEM((t + 8, n), _F32), pltpu.VMEM((8, f), _F32)],
        sem=("arbitrary",), name=name, comm=comm, vmem=_VMEM_LIMIT_FUSED)


def _mesh_pos():
    return lax.axis_index("x"), lax.axis_index("y"), lax.axis_index("c")


def _flip(v, bit):
    return 1 - v if bit else v


def _sem_scratch(n):
    return [pltpu.SemaphoreType.DMA((7 * n,)), pltpu.SemaphoreType.DMA((7 * n,)), pltpu.SemaphoreType.DMA((n,))]


class _Gather:
    def __init__(self, xs):
        self.ins = list(xs)
        self.outs = [jax.ShapeDtypeStruct((_NDEV,) + v.shape, v.dtype) for v in xs]
        self.scratch = _sem_scratch(len(xs))

    def _plan(self, x_refs, out_refs, sems):
        send_sems, recv_sems, local_sems = sems
        x, y, c = _mesh_pos()
        me, sibling = (x, y, c), (x, y, 1 - c)
        chips = [(1 - x, y), (x, 1 - y), (1 - x, 1 - y)]

        def copy(a, k, block, to, src=None):
            slot = out_refs[a].at[4 * block[0] + 2 * block[1] + block[2]]
            return pltpu.make_async_remote_copy(
                src_ref=slot if src is None else src, dst_ref=slot,
                send_sem=send_sems.at[a * 7 + k], recv_sem=recv_sems.at[a * 7 + k],
                device_id=to, device_id_type=_MESH_ID)

        def own(a):
            return pltpu.make_async_copy(x_refs[a], out_refs[a].at[4 * x + 2 * y + c], local_sems.at[a])

        def first(a):
            return [copy(a, 0, me, sibling, src=x_refs[a])] + [
                copy(a, 1 + j, me, (*chip, c), src=x_refs[a]) for j, chip in enumerate(chips)]

        return me, sibling, chips, c, copy, own, first

    def start(self, x_refs, out_refs, sems):
        _, _, _, _, _, own, first = self._plan(x_refs, out_refs, sems)
        for a in range(len(self.ins)):
            own(a).start()
            for cp in first(a):
                cp.start()

    def finish(self, x_refs, out_refs, sems):
        me, sibling, chips, c, copy, own, first = self._plan(x_refs, out_refs, sems)
        n = len(self.ins)
        passed = []
        for a in range(n):
            for j, chip in enumerate(chips):
                copy(a, 1 + j, (*chip, c), me).wait_recv()
                fwd = copy(a, 4 + j, (*chip, c), sibling)
                fwd.start()
                passed.append(fwd)
        for a in range(n):
            copy(a, 0, sibling, me).wait_recv()
            for j, chip in enumerate(chips):
                copy(a, 4 + j, (*chip, 1 - c), me).wait_recv()
        for a in range(n):
            for cp in first(a):
                cp.wait_send()
        for cp in passed:
            cp.wait_send()
        for a in range(n):
            own(a).wait()


class _Exchange:
    def __init__(self, gs):
        self.ins = list(gs)
        self.outs = [jax.ShapeDtypeStruct(v.shape, v.dtype) for v in gs]
        self.scratch = _sem_scratch(len(gs))

    def _plan(self, g_refs, r_refs, sems):
        send_sems, recv_sems, local_sems = sems
        x, y, c = _mesh_pos()
        me_idx = 4 * x + 2 * y + c
        n = len(self.ins)

        def copy(a, k):
            peer = (_flip(x, k & 4), _flip(y, k & 2), _flip(c, k & 1))
            peer_idx = 4 * peer[0] + 2 * peer[1] + peer[2]
            return pltpu.make_async_remote_copy(
                src_ref=g_refs[a].at[peer_idx], dst_ref=r_refs[a].at[me_idx],
                send_sem=send_sems.at[a * 7 + k - 1], recv_sem=recv_sems.at[a * 7 + k - 1],
                device_id=peer, device_id_type=_MESH_ID)

        copies = [copy(a, k) for a in range(n) for k in range(1, _NDEV)]
        mine = [pltpu.make_async_copy(g_refs[a].at[me_idx], r_refs[a].at[me_idx], local_sems.at[a])
                for a in range(n)]
        return copies, mine

    def start(self, g_refs, r_refs, sems):
        copies, mine = self._plan(g_refs, r_refs, sems)
        for cp in copies + mine:
            cp.start()

    def finish(self, g_refs, r_refs, sems):
        copies, mine = self._plan(g_refs, r_refs, sems)
        for cp in copies:
            cp.wait_recv()
        for cp in copies:
            cp.wait_send()
        for cp in mine:
            cp.wait()


class _Both:
    def __init__(self, first, second):
        self.parts = (first, second)
        self.ins = first.ins + second.ins
        self.outs = first.outs + second.outs
        self.scratch = first.scratch + second.scratch

    def _split(self, ins, outs, sems):
        a, b = self.parts
        na, nb = len(a.ins), len(a.scratch)
        return (a, ins[:na], outs[:na], sems[:nb]), (b, ins[na:], outs[na:], sems[nb:])

    def start(self, ins, outs, sems):
        for part, i, o, s in self._split(ins, outs, sems):
            part.start(i, o, s)

    def finish(self, ins, outs, sems):
        for part, i, o, s in self._split(ins, outs, sems):
            part.finish(i, o, s)


def _comm_call(comm, name):
    def body(*refs):
        n_i, n_o = len(comm.ins), len(comm.outs)
        ins, outs, sems = refs[:n_i], refs[n_i:n_i + n_o], refs[n_i + n_o:]
        comm.start(ins, outs, sems)
        comm.finish(ins, outs, sems)

    return pl.pallas_call(
        body, out_shape=list(comm.outs), in_specs=[_ANY] * len(comm.ins), out_specs=[_ANY] * len(comm.outs),
        scratch_shapes=list(comm.scratch), name=name)(*comm.ins)


def _adamw_math(w, g, m, v):
    m = _ADAM_B1 * m + (1.0 - _ADAM_B1) * g
    v = _ADAM_B2 * v + (1.0 - _ADAM_B2) * (g * g)
    m_hat = m / (1.0 - _ADAM_B1 ** _ADAM_STEP)
    v_hat = v / (1.0 - _ADAM_B2 ** _ADAM_STEP)
    delta = -_ADAM_LR * (m_hat / (jnp.sqrt(v_hat) + _ADAM_EPS) + _ADAM_WD * w)
    return delta, m, v


def _sum_adamw(parts, w, m, v, name):
    r, c = w.shape
    tr = _tile(r, 128)

    def body(p_ref, w_ref, m_ref, v_ref, g_ref, d_ref, nm_ref, nv_ref):
        g = p_ref[0].astype(_F32)
        for j in range(1, _NDEV):
            g = g + p_ref[j].astype(_F32)
        delta, nm, nv = _adamw_math(w_ref[...], g, m_ref[...], v_ref[...])
        g_ref[...] = g
        d_ref[...] = delta
        nm_ref[...] = nm
        nv_ref[...] = nv

    blk = pl.BlockSpec((tr, c), lambda i: (i, 0))
    return pl.pallas_call(
        body, grid=(r // tr,),
        in_specs=[pl.BlockSpec((_NDEV, tr, c), lambda i: (0, i, 0)), blk, blk, blk],
        out_specs=[blk] * 4, out_shape=[jax.ShapeDtypeStruct((r, c), _F32)] * 4,
        compiler_params=_cparams("parallel"), name=name,
    )(parts, w, m, v)


def _sum8(parts, name):
    _, r, c = parts.shape

    def body(p_ref, o_ref):
        g = p_ref[0]
        for j in range(1, _NDEV):
            g = g + p_ref[j]
        o_ref[...] = g

    return pl.pallas_call(
        body, grid=(1,), in_specs=[pl.BlockSpec((_NDEV, r, c), lambda i: (0, 0, 0))],
        out_specs=pl.BlockSpec((r, c), lambda i: (0, 0)), out_shape=jax.ShapeDtypeStruct((r, c), _F32),
        compiler_params=_cparams("arbitrary"), name=name,
    )(parts)


def _adamw_many(gs, ws, ms, vs, name):
    n = len(ws)

    def body(*refs):
        g_refs, w_refs, m_refs, v_refs = (refs[k * n:(k + 1) * n] for k in range(4))
        d_refs, nm_refs, nv_refs = (refs[(4 + k) * n:(5 + k) * n] for k in range(3))
        for k in range(n):
            delta, nm, nv = _adamw_math(w_refs[k][...], g_refs[k][...], m_refs[k][...], v_refs[k][...])
            d_refs[k][...] = delta
            nm_refs[k][...] = nm
            nv_refs[k][...] = nv

    def whole(arr):
        return pl.BlockSpec(arr.shape, lambda i, nd=arr.ndim: (0,) * nd)

    specs = [whole(w) for w in ws]
    res = pl.pallas_call(
        body, grid=(1,), in_specs=specs * 4, out_specs=specs * 3,
        out_shape=[jax.ShapeDtypeStruct(w.shape, _F32) for w in ws] * 3,
        compiler_params=_cparams("arbitrary"), name=name,
    )(*gs, *ws, *ms, *vs)
    return res[:n], res[n:2 * n], res[2 * n:]


def _pack(arrs):
    flat = jnp.concatenate([a.reshape(-1).astype(_F32) for a in arrs])
    pad = (-flat.shape[0]) % 1024
    return jnp.pad(flat, (0, pad)).reshape(-1, 128)


def _unpack(flat2d, shapes):
    flat = flat2d.reshape(-1)
    out, off = [], 0
    for sh in shapes:
        size = 1
        for dim in sh:
            size *= dim
        out.append(flat[off:off + size].reshape(sh))
        off += size
    return out


def _block_diag(w):
    h, hd, _ = w.shape
    eye = jnp.eye(h, dtype=w.dtype)
    return (eye[:, None, :, None] * w[:, :, None, :]).reshape(h * hd, h * hd)


def _diag_blocks(full, h):
    hd = full.shape[0] // h
    return jnp.stack([full[i * hd:(i + 1) * hd, i * hd:(i + 1) * hd] for i in range(h)])


def kernel(x, mem, mix_norm_g, w_in, lru_conv_w, lru_conv_b, lru_w_a, lru_b_a, lru_w_x, lru_b_x, lru_lambda, conf_conv_w, conf_conv_b, conf_ln_g, conf_ln_b, w_out, xa_norm_g, mem_norm_g, w_q, w_kv, w_o, ffn_norm_g, w_up, ffn_conv_w, ffn_conv_b, w_down, final_norm_g, loss_target, m_mix_norm_g, m_w_in, m_lru_conv_w, m_lru_conv_b, m_lru_w_a, m_lru_b_a, m_lru_w_x, m_lru_b_x, m_lru_lambda, m_conf_conv_w, m_conf_conv_b, m_conf_ln_g, m_conf_ln_b, m_w_out, m_xa_norm_g, m_mem_norm_g, m_w_q, m_w_kv, m_w_o, m_ffn_norm_g, m_w_up, m_ffn_conv_w, m_ffn_conv_b, m_w_down, m_final_norm_g, v_mix_norm_g, v_w_in, v_lru_conv_w, v_lru_conv_b, v_lru_w_a, v_lru_b_a, v_lru_w_x, v_lru_b_x, v_lru_lambda, v_conf_conv_w, v_conf_conv_b, v_conf_ln_g, v_conf_ln_b, v_w_out, v_xa_norm_g, v_mem_norm_g, v_w_q, v_w_kv, v_w_o, v_ffn_norm_g, v_w_up, v_ffn_conv_w, v_ffn_conv_b, v_w_down, v_final_norm_g):
    names = ["mix_norm_g", "w_in", "lru_conv_w", "lru_conv_b", "lru_w_a", "lru_b_a", "lru_w_x", "lru_b_x",
             "lru_lambda", "conf_conv_w", "conf_conv_b", "conf_ln_g", "conf_ln_b", "w_out", "xa_norm_g",
             "mem_norm_g", "w_q", "w_kv", "w_o", "ffn_norm_g", "w_up", "ffn_conv_w", "ffn_conv_b", "w_down",
             "final_norm_g"]
    loc = locals()
    W = {n: loc[n] for n in names}
    M = {n: loc["m_" + n] for n in names}
    V = {n: loc["v_" + n] for n in names}
    big = ["w_in", "w_out", "w_q", "w_kv", "w_o", "w_up", "w_down"]
    conv_sharded = ["lru_conv_w", "conf_conv_w", "ffn_conv_w"]

    xs = x[0]
    mems = mem[0]
    tgt = loss_target[0]
    me = 4 * lax.axis_index("x") + 2 * lax.axis_index("y") + lax.axis_index("c")

    conv_shapes = [W[n].shape[1:] for n in conv_sharded]
    conv_pack = _pack([W[n][0] for n in conv_sharded])
    shard = {n: W[n][0].astype(_XFER) for n in big}
    h1, (g_in, g_out, g_conv) = _rms_fwd(
        xs, mix_norm_g, "rms1_fwd", comm=_Gather([shard["w_in"], shard["w_out"], conv_pack]))
    convs = [[] for _ in conv_sharded]
    for j in range(_NDEV):
        for idx, part in enumerate(_unpack(g_conv[j], conv_shapes)):
            convs[idx].append(part)
    lcw, ccw, fcw = [jnp.concatenate(parts, axis=-1) for parts in convs]

    wab = jnp.concatenate([_block_diag(lru_w_a[0]), _block_diag(lru_w_x[0])], axis=1).astype(_MXU)
    mixer_params = (lcw, lru_conv_b, wab, lru_b_a, lru_b_x, lru_lambda, ccw, conf_conv_b, conf_ln_g, conf_ln_b)

    w_out_f = g_out.reshape(-1, g_out.shape[-1])
    (z, ycat, hs, cc, x1, h2), (g_q, g_kv, g_o, g_up, g_down) = _mixer_fwd(
        xs, h1, g_in, w_out_f, xa_norm_g, *mixer_params, "mixer_fwd",
        comm=_Gather([shard[n] for n in ("w_q", "w_kv", "w_o", "w_up", "w_down")]))
    w_q_f = g_q.reshape(-1, g_q.shape[-1])
    w_o_f = g_o.reshape(-1, g_o.shape[-1])
    w_down_f = g_down.reshape(-1, g_down.shape[-1])
    row32, row16, vec32 = (_F32, "row"), (_MXU, "row"), (_F32, "vec")
    mn = _rms_fwd(mems, mem_norm_g, "rmsm_fwd")
    kv = _mm_nn_stacked(mn, g_kv, _MXU, "mm_kv_fwd")
    q, o, x2, h3 = _attn_fwd(h2, x1, w_q_f, kv, w_o_f, ffn_norm_g, "attn_fwd")

    gfin = final_norm_g.reshape(1, -1)
    g0, act, gelu_g, u_dgelu, dx3, dx3b, lvec, dg_final = _ffn_fused_fwd(
        h3, g_up, w_down_f, fcw, ffn_conv_b, x2, tgt, gfin, "ffn_fwd")

    def rows8(p):
        return p.reshape(_NDEV, p.shape[0] // _NDEV, p.shape[1])

    p_down = _mm_tn_nat(act, dx3b, _XFER, "mm_down_wgrad", ts=2048)
    (dgu, dx2, dx2b, dg_ffn, dfcw, dfcb), (r_down,) = _ffn_fused_bwd(
        dx3b, g0, gelu_g, u_dgelu, w_down_f, g_up, fcw, x2, ffn_norm_g, dx3, "ffn_bwd",
        comm=_Exchange([rows8(p_down)]))
    p_up = _mm_tn_stacked(h3, dgu, _NDEV, _XFER, "mm_up_wgrad", slabs=1, ts=4096)

    (dx1, dx1b, dg_xa, dk, dv, p_o, p_q), (r_up,) = _attn_bwd(
        dx2b, dx2, q, o, h2, x1, kv, w_o_f, w_q_f, xa_norm_g, "attn_bwd", comm=_Exchange([p_up]))
    dkv = jnp.concatenate([dk, dv], axis=1).astype(_MXU)
    dmn = _mm_nt_stacked(dkv, g_kv, "mm_kv_dgrad", outs=[row32], slabs=_NDEV)
    p_kv = _mm_tn_stacked(mn, dkv, _NDEV, _XFER, "mm_kv_wgrad", slabs=_NDEV)
    _, _, dg_mem = _rms_bwd(dmn, mems, mem_norm_g, None, "rmsm_bwd")

    p_out = _mm_tn_nat(ycat, dx1b, _XFER, "mm_out_wgrad", ts=2048)
    ((dz, p_in, dlcw, dlcb, dwab, dba, dbx, dlam, dccw, dccb, dlng, dlnb),
     (r_o, r_q, r_kv, r_out)) = _mixer_bwd(
        dx1b, w_out_f, h1, z, hs, cc, *mixer_params, "mixer_bwd",
        comm=_Exchange([rows8(p_o), rows8(p_q), p_kv, rows8(p_out)]))

    c = _D_LRU
    heads = lru_w_a.shape[1]
    small_partial = {
        "lru_conv_w": dlcw, "lru_conv_b": dlcb,
        "lru_w_a": _diag_blocks(dwab[:, :c], heads), "lru_b_a": dba,
        "lru_w_x": _diag_blocks(dwab[:, c:], heads), "lru_b_x": dbx, "lru_lambda": dlam,
        "conf_conv_w": dccw, "conf_conv_b": dccb, "conf_ln_g": dlng, "conf_ln_b": dlnb,
        "xa_norm_g": dg_xa, "mem_norm_g": dg_mem, "ffn_norm_g": dg_ffn,
        "ffn_conv_w": dfcw, "ffn_conv_b": dfcb, "final_norm_g": dg_final,
    }
    early = list(small_partial)
    early_shapes = [small_partial[n].shape for n in early] + [lvec.shape]
    (grad_x, dg_mix), (r_in, early_all) = _mm_nt_stacked(
        dz, g_in, "mm_in_dgrad", epi=_epi_rms_bwd, extra=[(xs, "row"), (mix_norm_g, "vec"), (dx1, "row")],
        outs=[row32, vec32], tm=512, slabs=_NDEV,
        comm=_Both(_Exchange([p_in]), _Gather([_pack([small_partial[n] for n in early] + [lvec])])))
    (mix_all,) = _comm_call(_Gather([dg_mix]), "gather_mix_grad")
    small = early + ["mix_norm_g"]
    small_sum = _unpack(_sum8(early_all, "sum_small_grads"), early_shapes)
    loss = 0.5 * jnp.sum(small_sum.pop()) / xs.shape[1]
    small_sum.append(_sum8(mix_all.reshape(_NDEV, 8, -1), "sum_mix_grad").reshape(dg_mix.shape))
    received = {"w_in": r_in, "w_out": r_out, "w_q": r_q, "w_kv": r_kv, "w_o": r_o, "w_up": r_up,
                "w_down": r_down}

    grads, deltas, new_m, new_v = {}, {}, {}, {}
    for n, rec in ((n, received[n]) for n in big):
        shp = W[n].shape
        w2, m2, v2 = (t.reshape(shp[1:]) for t in (W[n], M[n], V[n]))
        outs = _sum_adamw(rec, w2, m2, v2, "adamw_" + n)
        grads[n], deltas[n], new_m[n], new_v[n] = (t.reshape(shp) for t in outs)

    small_g = []
    for n, g in zip(small, small_sum):
        if n in conv_sharded:
            width = W[n].shape[-1]
            g = lax.dynamic_slice_in_dim(g, me * width, width, axis=1)
        small_g.append(g.reshape(W[n].shape))

    def at_least_2d(a):
        return a.reshape(1, -1) if a.ndim == 1 else a

    sd, sm, sv = _adamw_many([at_least_2d(g) for g in small_g], [at_least_2d(W[n]) for n in small],
                             [at_least_2d(M[n]) for n in small], [at_least_2d(V[n]) for n in small], "adamw_small")
    for n, g, d_, m_, v_ in zip(small, small_g, sd, sm, sv):
        shp = W[n].shape
        grads[n], deltas[n], new_m[n], new_v[n] = g, d_.reshape(shp), m_.reshape(shp), v_.reshape(shp)

    return (loss, grad_x[None], *[grads[n] for n in names], *[deltas[n] for n in names],
            *[new_m[n] for n in names], *[new_v[n] for n in names])
```

```python
import functools

import jax
import jax.numpy as jnp
from jax import lax
from jax.experimental import pallas as pl
from jax.experimental.pallas import tpu as pltpu

_MXU = jnp.bfloat16
_XFER = jnp.bfloat16
_F32 = jnp.float32
_EPS = 1e-6
_NDEV = 8
_VMEM_LIMIT = 48 * 1024 * 1024

_D_LRU = 512
_XA_HEADS = 4
_RG_C = 8.0
_ADAM_LR, _ADAM_B1, _ADAM_B2, _ADAM_EPS, _ADAM_WD, _ADAM_STEP = 0.001, 0.9, 0.999, 1e-08, 0.01, 10

_MESH_ID = pl.DeviceIdType.MESH
_ANY = pl.BlockSpec(memory_space=pl.ANY)


def _cparams(*sem, vmem=_VMEM_LIMIT):
    return pltpu.CompilerParams(dimension_semantics=tuple(sem), vmem_limit_bytes=vmem)


def _pcall(body, *, args, grid, in_specs, out_specs, out_shape, sem, name, scratch_shapes=(), comm=None,
           vmem=_VMEM_LIMIT):
    outs_l = list(out_shape) if isinstance(out_shape, (list, tuple)) else [out_shape]
    ospecs_l = list(out_specs) if isinstance(out_specs, (list, tuple)) else [out_specs]
    n_in, n_out, n_scr = len(args), len(outs_l), len(scratch_shapes)
    if comm is None:
        res = pl.pallas_call(
            body, grid=grid, in_specs=list(in_specs), out_specs=ospecs_l, out_shape=outs_l,
            scratch_shapes=list(scratch_shapes), compiler_params=_cparams(*sem, vmem=vmem), name=name)(*args)
        return list(res), []
    n_ci, n_co = len(comm.ins), len(comm.outs)

    def wrapped(*refs):
        ins, cins = refs[:n_in], refs[n_in:n_in + n_ci]
        o = n_in + n_ci
        outs, couts = refs[o:o + n_out], refs[o + n_out:o + n_out + n_co]
        s = o + n_out + n_co
        scr, cscr = refs[s:s + n_scr], refs[s + n_scr:]
        first = pl.program_id(0) == 0
        last = pl.program_id(0) == grid[0] - 1
        for ax in range(1, len(grid)):
            first = jnp.logical_and(first, pl.program_id(ax) == 0)
            last = jnp.logical_and(last, pl.program_id(ax) == grid[ax] - 1)

        @pl.when(first)
        def _():
            comm.start(cins, couts, cscr)

        body(*ins, *outs, *scr)

        @pl.when(last)
        def _():
            comm.finish(cins, couts, cscr)

    res = pl.pallas_call(
        wrapped, grid=grid, in_specs=list(in_specs) + [_ANY] * n_ci, out_specs=ospecs_l + [_ANY] * n_co,
        out_shape=outs_l + list(comm.outs), scratch_shapes=list(scratch_shapes) + list(comm.scratch),
        compiler_params=_cparams(*(("arbitrary",) * len(grid)), vmem=vmem), name=name)(*args, *comm.ins)
    return list(res[:n_out]), list(res[n_out:])


def _sigmoid(v):
    return 1.0 / (1.0 + jnp.exp(-v))


_GELU_C = 0.7978845608028654
_GELU_K = 0.044715


def _gelu(v):
    t = jnp.tanh(_GELU_C * (v + _GELU_K * v * v * v))
    return 0.5 * v * (1.0 + t)


def _gelu_and_grad(v):
    v2 = v * v
    s = 0.5 * jnp.tanh(v * (_GELU_C + (_GELU_C * _GELU_K) * v2)) + 0.5
    g = v * s
    dg = s + (g * (1.0 - s)) * ((2.0 * _GELU_C) + (6.0 * _GELU_C * _GELU_K) * v2)
    return g, dg


def _softplus(v):
    e = jnp.exp(-jnp.abs(v))
    log1p = jnp.where(e < 1e-2, e * (1.0 - e * (0.5 - e * (1.0 / 3.0))), jnp.log(1.0 + e))
    return jnp.maximum(v, 0.0) + log1p


def _neg_expm1(v):
    series = -v * (1.0 + v * (0.5 + v * ((1.0 / 6.0) + v * (1.0 / 24.0))))
    return jnp.where(v > -0.0625, series, 1.0 - jnp.exp(v))


def _dot(a, b, dims):
    return lax.dot_general(a.astype(_MXU), b.astype(_MXU), (dims, ((), ())), preferred_element_type=_F32)


_NN = ((1,), (0,))
_NT = ((1,), (1,))
_TN = ((0,), (0,))


def _scan_fwd(a, b, rows):
    n = a.shape[0]
    d = 1
    while d < n:
        keep = rows >= d
        b = jnp.where(keep, b + a * pltpu.roll(b, d, 0), b)
        a = jnp.where(keep, a * pltpu.roll(a, d, 0), a)
        d *= 2
    return a, b


def _scan_rev(a, b, rows):
    n = a.shape[0]
    d = 1
    while d < n:
        keep = rows < n - d
        b = jnp.where(keep, b + a * pltpu.roll(b, n - d, 0), b)
        a = jnp.where(keep, a * pltpu.roll(a, n - d, 0), a)
        d *= 2
    return a, b


def _colsum(v):
    return jnp.sum(v, axis=0, keepdims=True)


def _mm(a, b, *, dims, grid, a_spec, b_spec, outs, acc_shape, name, extra=(), epi=None, slabs=1, comm=None):
    nred = grid[-1]
    red_axis = len(grid) - 1
    n_ex, n_out = len(extra), len(outs)
    epi = _epi_store if epi is None else epi

    def body(*refs):
        a_ref, b_ref = refs[:2]
        ex, o_refs, acc_ref = refs[2:2 + n_ex], refs[2 + n_ex:2 + n_ex + n_out], refs[-1]
        if slabs == 1:
            p = _dot(a_ref[...], b_ref[...], dims)
        else:
            n = b_ref.shape[-1]
            p = _dot(a_ref[:, 0:n], b_ref[0], dims)
            for jj in range(1, slabs):
                p = p + _dot(a_ref[:, jj * n:(jj + 1) * n], b_ref[jj], dims)

        first_rows = pl.program_id(0) == 0
        if nred == 1:
            epi(p, ex, o_refs, first_rows)
        else:
            k = pl.program_id(red_axis)

            @pl.when(k == 0)
            def _():
                acc_ref[...] = p

            @pl.when(jnp.logical_and(k > 0, k < nred - 1))
            def _():
                acc_ref[...] += p

            @pl.when(k == nred - 1)
            def _():
                epi(acc_ref[...] + p, ex, o_refs, first_rows)

    sem = ("parallel",) * (len(grid) - 1) + ("arbitrary",)
    if any(o[0].shape[0] == 1 for o in outs):
        sem = ("arbitrary",) * len(grid)
    res, cres = _pcall(
        body, args=(a, b) + tuple(e[0] for e in extra), grid=grid,
        in_specs=[a_spec, b_spec] + [e[1] for e in extra],
        out_specs=[o[1] for o in outs], out_shape=[o[0] for o in outs],
        scratch_shapes=[pltpu.VMEM(acc_shape if nred > 1 else (8, 128), _F32)], sem=sem, name=name, comm=comm)
    res = res[0] if n_out == 1 else res
    return res if comm is None else (res, cres)


def _epi_store(total, ex, outs, first_rows):
    outs[0][...] = total.astype(outs[0].dtype)


def _epi_residual_rms(total, ex, outs, first_rows):
    res_ref, g_ref = ex
    xn = total + res_ref[...]
    outs[0][...] = xn
    r = lax.rsqrt(jnp.mean(xn * xn, axis=-1, keepdims=True) + _EPS)
    outs[1][...] = (xn * r * g_ref[...]).astype(outs[1].dtype)


def _epi_rms_bwd(total, ex, outs, first_rows):
    x_ref, g_ref, dres_ref = ex
    dg_ref = outs[-1]
    xv = x_ref[...]
    r = lax.rsqrt(jnp.mean(xv * xv, axis=-1, keepdims=True) + _EPS)
    xhat = xv * r
    dxh = total * g_ref[...]
    dx = dres_ref[...] + r * (dxh - xhat * jnp.mean(dxh * xhat, axis=-1, keepdims=True))
    for o_ref in outs[:-1]:
        o_ref[...] = dx.astype(o_ref.dtype)

    @pl.when(first_rows)
    def _():
        dg_ref[...] = jnp.zeros_like(dg_ref)

    dg_ref[...] += _colsum(total * xhat)


def _epi_final(total, ex, outs, first_rows):
    res_ref, t_ref, g_ref = ex
    dx_ref, dxb_ref, l_ref, dg_ref = outs
    xv = total + res_ref[...]
    gv = g_ref[...]
    d = xv.shape[-1]
    r = lax.rsqrt(jnp.mean(xv * xv, axis=-1, keepdims=True) + _EPS)
    xhat = xv * r
    err = xhat * gv - t_ref[...]
    dy = err * (1.0 / d)
    dxh = dy * gv
    dx = r * (dxh - xhat * jnp.mean(dxh * xhat, axis=-1, keepdims=True))
    dx_ref[...] = dx
    dxb_ref[...] = dx.astype(dxb_ref.dtype)

    @pl.when(first_rows)
    def _():
        l_ref[...] = jnp.zeros_like(l_ref)
        dg_ref[...] = jnp.zeros_like(dg_ref)

    l_ref[...] += _colsum(err * err)
    dg_ref[...] += _colsum(dy * xhat)


def _tile(m, cap):
    t = min(m, cap)
    assert m % t == 0
    return t


def _row_spec(tm, n):
    return pl.BlockSpec((tm, n), lambda i, *_: (i, 0))


def _vec_spec(n):
    return pl.BlockSpec((1, n), lambda *_: (0, 0))


def _row_io(m, n, tm, extra, outs):
    def spec(kind):
        return _row_spec(tm, n) if kind == "row" else _vec_spec(n)

    ex = [(arr, spec(kind)) for arr, kind in extra]
    os_ = [(jax.ShapeDtypeStruct((m, n) if kind == "row" else (1, n), dt), spec(kind)) for dt, kind in outs]
    return ex, os_


def _mm_nn_stacked(a, w, out_dtype, name, comm=None, tm=1024):
    m, k = a.shape
    j, _, n = w.shape
    tm = _tile(m, tm)
    return _mm(a, w, dims=_NN, grid=(m // tm, j, 1),
               a_spec=pl.BlockSpec((tm, k), lambda i, jj, r: (i, 0)),
               b_spec=pl.BlockSpec((None, k, n), lambda i, jj, r: (jj, 0, 0)),
               outs=[(jax.ShapeDtypeStruct((m, j * n), out_dtype), pl.BlockSpec((tm, n), lambda i, jj, r: (i, jj)))],
               acc_shape=(tm, n), name=name, comm=comm)


def _mm_nt_stacked(dc, w, name, *, outs, extra=(), epi=None, comm=None, tm=1024, slabs=1):
    m = dc.shape[0]
    j, k, n = w.shape
    tm = _tile(m, tm)
    assert j % slabs == 0
    ex, os_ = _row_io(m, k, tm, extra, outs)
    wblk = (None, k, n) if slabs == 1 else (slabs, k, n)
    return _mm(dc, w, dims=_NT, grid=(m // tm, j // slabs),
               a_spec=pl.BlockSpec((tm, slabs * n), lambda i, r: (i, r)),
               b_spec=pl.BlockSpec(wblk, lambda i, r: (r, 0, 0)),
               outs=os_, extra=ex, epi=epi, acc_shape=(tm, k), name=name, slabs=slabs, comm=comm)


def _mm_tn_stacked(a, dc, j, out_dtype, name, slabs=1, ts=1024):
    s, k = a.shape
    n = dc.shape[1] // j
    ts = _tile(s, ts)
    assert j % slabs == 0

    def epi(total, ex, outs, first_rows):
        for jj in range(slabs):
            outs[0][jj] = total[:, jj * n:(jj + 1) * n].astype(outs[0].dtype)

    return _mm(a, dc, dims=_TN, grid=(j // slabs, s // ts),
               a_spec=pl.BlockSpec((ts, k), lambda jj, r: (r, 0)),
               b_spec=pl.BlockSpec((ts, slabs * n), lambda jj, r: (r, jj)),
               outs=[(jax.ShapeDtypeStruct((j, k, n), out_dtype),
                      pl.BlockSpec((slabs, k, n), lambda jj, r: (jj, 0, 0)))],
               epi=epi, acc_shape=(k, slabs * n), name=name)


def _mm_tn_nat(a, dc, out_dtype, name, ts=1024):
    s, kt = a.shape
    n = dc.shape[1]
    ts = _tile(s, ts)
    tkb = _tile(kt, 1024)
    return _mm(a, dc, dims=_TN, grid=(kt // tkb, s // ts),
               a_spec=pl.BlockSpec((ts, tkb), lambda kb, r: (r, kb)),
               b_spec=pl.BlockSpec((ts, n), lambda kb, r: (r, 0)),
               outs=[(jax.ShapeDtypeStruct((kt, n), out_dtype), pl.BlockSpec((tkb, n), lambda kb, r: (kb, 0)))],
               acc_shape=(tkb, n), name=name)


def _rms_fwd(x, g, name, comm=None):
    s, d = x.shape
    t = _tile(s, 1024)

    def body(x_ref, g_ref, h_ref):
        xv = x_ref[...]
        r = lax.rsqrt(jnp.mean(xv * xv, axis=-1, keepdims=True) + _EPS)
        h_ref[...] = (xv * r * g_ref[...]).astype(h_ref.dtype)

    res, cres = _pcall(
        body, args=(x, g), grid=(s // t,),
        in_specs=[pl.BlockSpec((t, d), lambda i: (i, 0)), pl.BlockSpec((1, d), lambda i: (0, 0))],
        out_specs=pl.BlockSpec((t, d), lambda i: (i, 0)),
        out_shape=jax.ShapeDtypeStruct((s, d), _MXU), sem=("parallel",), name=name, comm=comm)
    return res[0] if comm is None else (res[0], cres)


def _rms_bwd(dh, x, g, dres, name):
    s, d = x.shape
    t = _tile(s, 256)
    has_res = dres is not None

    def body(*refs):
        if has_res:
            dh_ref, x_ref, g_ref, dres_ref, dx_ref, dxb_ref, dg_ref = refs
        else:
            dh_ref, x_ref, g_ref, dx_ref, dxb_ref, dg_ref = refs
        xv = x_ref[...]
        dhv = dh_ref[...]
        r = lax.rsqrt(jnp.mean(xv * xv, axis=-1, keepdims=True) + _EPS)
        xhat = xv * r
        dxh = dhv * g_ref[...]
        dx = r * (dxh - xhat * jnp.mean(dxh * xhat, axis=-1, keepdims=True))
        if has_res:
            dx = dx + dres_ref[...]
        dx_ref[...] = dx
        dxb_ref[...] = dx.astype(dxb_ref.dtype)

        @pl.when(pl.program_id(0) == 0)
        def _():
            dg_ref[...] = jnp.zeros_like(dg_ref)

        dg_ref[...] += _colsum(dhv * xhat)

    row = pl.BlockSpec((t, d), lambda i: (i, 0))
    vec = pl.BlockSpec((1, d), lambda i: (0, 0))
    in_specs = [row, row, vec] + ([row] if has_res else [])
    args = (dh, x, g) + ((dres,) if has_res else ())
    return pl.pallas_call(
        body, grid=(s // t,), in_specs=in_specs, out_specs=[row, row, vec],
        out_shape=[jax.ShapeDtypeStruct((s, d), _F32), jax.ShapeDtypeStruct((s, d), _MXU),
                   jax.ShapeDtypeStruct((1, d), _F32)],
        compiler_params=_cparams("arbitrary"), name=name,
    )(*args)


_LRU_K = 4
_CONF_K = 31
_LRU_HALO = 8
_CONF_HALO = 32
_MIX_T = 512


def _lru_gates(lx, wab_ref, ba_ref, bx_ref, lam_ref):
    c = _D_LRU
    pre = _dot(lx, wab_ref[...], _NN)
    r = _sigmoid(pre[:, :c] + ba_ref[...])
    ig = _sigmoid(pre[:, c:] + bx_ref[...])
    sp = _softplus(-lam_ref[...])
    log_a = (-_RG_C) * r * sp
    a = jnp.exp(log_a)
    mult = jnp.sqrt(_neg_expm1(2.0 * log_a))
    return r, ig, sp, a, mult


def _causal_conv(ext_ref, halo, w_ref, b_ref, taps, t):
    acc = b_ref[...] + w_ref[0:1, :] * ext_ref[pl.ds(halo - (taps - 1), t), :]
    for k in range(1, taps):
        acc = acc + w_ref[k:k + 1, :] * ext_ref[pl.ds(halo - (taps - 1) + k, t), :]
    return acc


class _Windows:
    def __init__(self, ext_ref, shifted_ref, t):
        self.ext_ref, self.shifted_ref, self.t = ext_ref, shifted_ref, t
        rows = t + 24
        for r in range(1, 8):
            shifted_ref[r - 1, 0:rows, :] = ext_ref[pl.ds(r, rows), :]

    def __call__(self, off):
        q, r = divmod(off, 8)
        if r == 0:
            return self.ext_ref[pl.ds(8 * q, self.t), :]
        return self.shifted_ref[r - 1, pl.ds(8 * q, self.t), :]


def _mixer_fwd(xs, h1, w_in, w_out, gnorm, lcw, lcb, wab, ba, bx, lam, ccw, ccb, lng, lnb, name, comm=None):
    s, d = xs.shape
    nblk, _, n = w_in.shape
    c = _D_LRU
    t = _tile(s, _MIX_T)
    nt = s // t

    def body(x_ref, h_ref, win_ref, wout_ref, gn_ref,
             lcw_ref, lcb_ref, wab_ref, ba_ref, bx_ref, lam_ref, ccw_ref, ccb_ref, lng_ref, lnb_ref,
             z_ref, ycat_ref, hs_ref, cc_ref, x1_ref, h2_ref,
             ext_ref, cge_ref, hc_ref, shifted_ref, zprev_ref):
        i = pl.program_id(0)
        first = i == 0
        rows = lax.broadcasted_iota(jnp.int32, (t, c), 0)

        @pl.when(first)
        def _():
            zprev_ref[...] = jnp.zeros_like(zprev_ref)

        hv = h_ref[...]
        for j in range(nblk):
            z_ref[:, j * n:(j + 1) * n] = _dot(hv, win_ref[j], _NN)
        lx0_ref, gate_ref = z_ref.at[:, 0:c], z_ref.at[:, c:2 * c]
        ca_ref, cb_ref = z_ref.at[:, 2 * c:3 * c], z_ref.at[:, 3 * c:4 * c]
        lx0h_ref = zprev_ref.at[_CONF_HALO - _LRU_HALO:_CONF_HALO, 0:c]
        cah_ref, cbh_ref = zprev_ref.at[:, 2 * c:3 * c], zprev_ref.at[:, 3 * c:4 * c]

        ext_ref[0:_LRU_HALO, :] = jnp.where(first, 0.0, lx0h_ref[...])
        ext_ref[_LRU_HALO:_LRU_HALO + t, :] = lx0_ref[...]
        lx = _causal_conv(ext_ref, _LRU_HALO, lcw_ref, lcb_ref, _LRU_K, t)
        r, ig, sp, a, mult = _lru_gates(lx, wab_ref, ba_ref, bx_ref, lam_ref)
        u = mult * (ig * lx)
        a_cum, h_loc = _scan_fwd(a, u, rows)

        @pl.when(first)
        def _():
            hc_ref[...] = jnp.zeros_like(hc_ref)

        h = h_loc + a_cum * hc_ref[7:8, :]
        hs_ref[...] = h
        hc_ref[...] = hs_ref[pl.ds(t - 8, 8), :]
        ycat_ref[:, 0:c] = (h * _gelu(gate_ref[...])).astype(ycat_ref.dtype)

        cge_ref[0:_CONF_HALO, :] = jnp.where(first, 0.0, cah_ref[...] * _sigmoid(cbh_ref[...]))
        cge_ref[_CONF_HALO:_CONF_HALO + t, :] = ca_ref[...] * _sigmoid(cb_ref[...])
        win = _Windows(cge_ref, shifted_ref, t)
        first_off = _CONF_HALO - (_CONF_K - 1)
        cc = ccb_ref[...] + ccw_ref[0:1, :] * win(first_off)
        for k in range(1, _CONF_K):
            cc = cc + ccw_ref[k:k + 1, :] * win(first_off + k)
        cc_ref[...] = cc
        xc = cc - jnp.mean(cc, axis=-1, keepdims=True)
        rstd = lax.rsqrt(jnp.mean(xc * xc, axis=-1, keepdims=True) + _EPS)
        ln = xc * rstd * lng_ref[...] + lnb_ref[...]
        ycat_ref[:, c:2 * c] = (ln * _sigmoid(ln)).astype(ycat_ref.dtype)

        zprev_ref[...] = z_ref[pl.ds(t - _CONF_HALO, _CONF_HALO), :]
        y = _dot(ycat_ref[...], wout_ref[...], _NN)
        _epi_residual_rms(y, (x_ref, gn_ref), (x1_ref, h2_ref), first)

    def const(arr):
        return pl.BlockSpec(arr.shape, lambda i: (0,) * arr.ndim, pipeline_mode=pl.Buffered(1))

    def rows_of(width):
        return pl.BlockSpec((t, width), lambda i: (i, 0))

    params = (lcw, lcb, wab, ba, bx, lam, ccw, ccb, lng, lnb)
    res, cres = _pcall(
        body, args=(xs, h1, w_in, w_out, gnorm, *params), grid=(nt,),
        in_specs=[rows_of(d), rows_of(d), const(w_in), const(w_out), const(gnorm)] + [const(p) for p in params],
        out_specs=[rows_of(nblk * n), rows_of(2 * c), rows_of(c), rows_of(c), rows_of(d), rows_of(d)],
        out_shape=[jax.ShapeDtypeStruct((s, nblk * n), _F32), jax.ShapeDtypeStruct((s, 2 * c), _MXU),
                   jax.ShapeDtypeStruct((s, c), _F32), jax.ShapeDtypeStruct((s, c), _F32),
                   jax.ShapeDtypeStruct((s, d), _F32), jax.ShapeDtypeStruct((s, d), _MXU)],
        scratch_shapes=[pltpu.VMEM((t + _LRU_HALO, c), _F32), pltpu.VMEM((t + _CONF_HALO, c), _F32),
                        pltpu.VMEM((8, c), _F32), pltpu.VMEM((7, t + _CONF_HALO, c), _F32),
                        pltpu.VMEM((_CONF_HALO, nblk * n), _F32)],
        sem=("arbitrary",), name=name, comm=comm)
    return res, cres


def _mixer_bwd(dx1b, w_out, h1, z, hs, cc, lcw, lcb, wab, ba, bx, lam, ccw, ccb, lng, lnb, name, comm=None):
    s = z.shape[0]
    d = h1.shape[1]
    c = _D_LRU
    t = _tile(s, _MIX_T)
    nt = s // t
    nblk = z.shape[1] // 256

    def body(dxb_ref, wout_ref, h1_ref, lx0_ref, lx0h_ref, gate_ref, ca_ref, cah_ref, cb_ref, cbh_ref,
             hs_ref, hsh_ref, cc_ref,
             lcw_ref, lcb_ref, wab_ref, ba_ref, bx_ref, lam_ref, ccw_ref, ccb_ref, lng_ref, lnb_ref,
             dz_ref, pin_ref, dlcw_ref, dlcb_ref, dwab_ref, dba_ref, dbx_ref, dlam_ref, dccw_ref, dccb_ref, dlng_ref,
             dlnb_ref,
             ext_ref, up_ref, cge_ref, dce_ref, xc_ref, dlxc_ref, dccc_ref, shifted_ref, dwin_ref):
        i = pl.program_id(0)
        first_tile = i == nt - 1
        last_tile = i == 0
        rows = lax.broadcasted_iota(jnp.int32, (t, c), 0)

        @pl.when(last_tile)
        def _():
            for ref in (dlcw_ref, dlcb_ref, dwab_ref, dba_ref, dbx_ref, dlam_ref, dccw_ref, dccb_ref, dlng_ref,
                        dlnb_ref, xc_ref, dlxc_ref, dccc_ref, dwin_ref):
                ref[...] = jnp.zeros_like(ref)

        dycat = _dot(dxb_ref[...], wout_ref[...], _NT)

        ext_ref[0:_LRU_HALO, :] = jnp.where(first_tile, 0.0, lx0h_ref[...])
        ext_ref[_LRU_HALO:_LRU_HALO + t, :] = lx0_ref[...]
        lx = _causal_conv(ext_ref, _LRU_HALO, lcw_ref, lcb_ref, _LRU_K, t)
        r, ig, sp, a, mult = _lru_gates(lx, wab_ref, ba_ref, bx_ref, lam_ref)
        h = hs_ref[...]
        gl, dgl = _gelu_and_grad(gate_ref[...])
        dyl = dycat[:, 0:c]
        dz_ref[:, c:2 * c] = (dyl * h * dgl).astype(dz_ref.dtype)
        dh = dyl * gl

        up_ref[0:t, :] = a
        up_ref[t:t + 8, :] = jnp.ones((8, c), _F32)
        a_up = up_ref[pl.ds(1, t), :]
        a_cum, g_loc = _scan_rev(a_up, dh, rows)
        gt = g_loc + a_cum * xc_ref[0:1, :]
        xc_ref[...] = (a * gt)[0:8, :]

        up_ref[0:8, :] = jnp.where(first_tile, 0.0, hsh_ref[...])
        up_ref[8:8 + t, :] = h
        hprev = up_ref[pl.ds(7, t), :]

        da = gt * hprev
        dmult = gt * ig * lx
        dig = gt * mult * lx
        dlx = gt * mult * ig
        dlog_a = da * a - dmult * a * a / mult
        dpre_r = dlog_a * (-_RG_C) * sp * r * (1.0 - r)
        dpre_i = dig * ig * (1.0 - ig)
        dlam_ref[...] += _colsum(dlog_a * r) * (_RG_C * _sigmoid(-lam_ref[...]))
        dba_ref[...] += _colsum(dpre_r)
        dbx_ref[...] += _colsum(dpre_i)
        dpre = jnp.concatenate([dpre_r, dpre_i], axis=1).astype(_MXU)
        dlx = dlx + _dot(dpre, wab_ref[...], _NT)
        dwab_ref[...] += _dot(lx, dpre, _TN)

        dlcb_ref[...] += _colsum(dlx)
        up_ref[0:t, :] = dlx
        up_ref[t:t + 8, :] = dlxc_ref[...]
        dlxc_ref[...] = dlx[0:8, :]
        acc = lcw_ref[0:1, :] * up_ref[pl.ds(_LRU_K - 1, t), :]
        for k in range(1, _LRU_K):
            acc = acc + lcw_ref[k:k + 1, :] * up_ref[pl.ds(_LRU_K - 1 - k, t), :]
        dz_ref[:, 0:c] = acc.astype(dz_ref.dtype)
        for k in range(_LRU_K):
            dlcw_ref[k:k + 1, :] += _colsum(dlx * ext_ref[pl.ds(_LRU_HALO - (_LRU_K - 1) + k, t), :])

        sig_b = _sigmoid(cb_ref[...])
        ca = ca_ref[...]
        cge_ref[0:_CONF_HALO, :] = jnp.where(first_tile, 0.0, cah_ref[...] * _sigmoid(cbh_ref[...]))
        cge_ref[_CONF_HALO:_CONF_HALO + t, :] = ca * sig_b
        ccv = cc_ref[...]
        xcen = ccv - jnp.mean(ccv, axis=-1, keepdims=True)
        rstd = lax.rsqrt(jnp.mean(xcen * xcen, axis=-1, keepdims=True) + _EPS)
        xn = xcen * rstd
        ln = xn * lng_ref[...] + lnb_ref[...]
        sg = _sigmoid(ln)
        dln = dycat[:, c:2 * c] * (sg * (1.0 + ln * (1.0 - sg)))
        dlng_ref[...] += _colsum(dln * xn)
        dlnb_ref[...] += _colsum(dln)
        dxn = dln * lng_ref[...]
        dcc = rstd * (dxn - jnp.mean(dxn, axis=-1, keepdims=True)
                      - xn * jnp.mean(dxn * xn, axis=-1, keepdims=True))
        dccb_ref[...] += _colsum(dcc)
        win = _Windows(cge_ref, shifted_ref, t)
        for k in range(_CONF_K):
            dccw_ref[k:k + 1, :] += _colsum(dcc * win(_CONF_HALO - (_CONF_K - 1) + k))
        dce_ref[0:t, :] = dcc
        dce_ref[t:t + _CONF_HALO, :] = dccc_ref[...]
        dccc_ref[...] = dcc[0:_CONF_HALO, :]
        win = _Windows(dce_ref, shifted_ref, t)
        dcg = ccw_ref[0:1, :] * win(_CONF_K - 1)
        for k in range(1, _CONF_K):
            dcg = dcg + ccw_ref[k:k + 1, :] * win(_CONF_K - 1 - k)
        dz_ref[:, 2 * c:3 * c] = (dcg * sig_b).astype(dz_ref.dtype)
        dz_ref[:, 3 * c:4 * c] = (dcg * ca * sig_b * (1.0 - sig_b)).astype(dz_ref.dtype)

        dwin_ref[...] += _dot(h1_ref[...], dz_ref[...], _TN)

        @pl.when(first_tile)
        def _():
            for j in range(nblk):
                pin_ref[j] = dwin_ref[:, j * 256:(j + 1) * 256].astype(pin_ref.dtype)

    def col(j):
        return pl.BlockSpec((t, c), lambda i: (nt - 1 - i, j))

    def halo(j, rows_):
        per = t // rows_
        return pl.BlockSpec((rows_, c), lambda i: (jnp.maximum((nt - 1 - i) * per - 1, 0), j))

    def full(shape):
        return pl.BlockSpec(shape, lambda i: (0,) * len(shape))

    params = (lcw, lcb, wab, ba, bx, lam, ccw, ccb, lng, lnb)
    small = [(_LRU_K, c), (1, c), (c, 2 * c), (1, c), (1, c), (1, c), (_CONF_K, c), (1, c), (1, c), (1, c)]
    wide = pl.BlockSpec((t, d), lambda i: (nt - 1 - i, 0))
    pin_shape = (nblk, d, 256)
    return _pcall(
        body, args=(dx1b, w_out, h1, z, z, z, z, z, z, z, hs, hs, cc, *params), grid=(nt,),
        in_specs=[wide, pl.BlockSpec(w_out.shape, lambda i: (0, 0), pipeline_mode=pl.Buffered(1)), wide,
                  col(0), halo(0, _LRU_HALO), col(1), col(2), halo(2, _CONF_HALO), col(3), halo(3, _CONF_HALO),
                  col(0), halo(0, 8), col(0)]
        + [full(p.shape) for p in params],
        out_specs=[pl.BlockSpec((t, 4 * c), lambda i: (nt - 1 - i, 0)), full(pin_shape)] + [full(sh) for sh in small],
        out_shape=[jax.ShapeDtypeStruct((s, 4 * c), _MXU), jax.ShapeDtypeStruct(pin_shape, _XFER)]
        + [jax.ShapeDtypeStruct(sh, _F32) for sh in small],
        scratch_shapes=[pltpu.VMEM((t + _LRU_HALO, c), _F32), pltpu.VMEM((t + 8, c), _F32),
                        pltpu.VMEM((t + _CONF_HALO, c), _F32), pltpu.VMEM((t + _CONF_HALO, c), _F32),
                        pltpu.VMEM((8, c), _F32), pltpu.VMEM((8, c), _F32), pltpu.VMEM((_CONF_HALO, c), _F32),
                        pltpu.VMEM((7, t + _CONF_HALO, c), _F32), pltpu.VMEM((d, 4 * c), _F32)],
        sem=("arbitrary",), name=name, comm=comm, vmem=_VMEM_LIMIT_FUSED)


_ATT_T = 512


def _attn_probs(qh, kh, scale):
    sc = _dot(qh, kh, _NT) * scale
    e = jnp.exp(sc - jnp.max(sc, axis=-1, keepdims=True))
    return e / jnp.sum(e, axis=-1, keepdims=True)


def _const_spec(arr):
    return pl.BlockSpec(arr.shape, lambda i: (0,) * arr.ndim, pipeline_mode=pl.Buffered(1))


def _attn_fwd(h2, x1, w_q, kv, w_o, gnorm, name):
    s, d = h2.shape
    nm = kv.shape[0]
    hd = d // _XA_HEADS
    t = _tile(s, _ATT_T)
    scale = hd ** -0.5

    def body(h_ref, x1_ref, wq_ref, k_ref, v_ref, wo_ref, g_ref, q_ref, o_ref, x2_ref, h3_ref):
        q_ref[...] = _dot(h_ref[...], wq_ref[...], _NN).astype(q_ref.dtype)
        for hh in range(_XA_HEADS):
            sl = slice(hh * hd, (hh + 1) * hd)
            p = _attn_probs(q_ref[:, sl], k_ref[:, sl], scale)
            o_ref[:, sl] = _dot(p, v_ref[:, sl], _NN).astype(o_ref.dtype)
        y = _dot(o_ref[...], wo_ref[...], _NN)
        _epi_residual_rms(y, (x1_ref, g_ref), (x2_ref, h3_ref), None)

    row = pl.BlockSpec((t, d), lambda i: (i, 0))
    half = pl.BlockSpec((nm, d), lambda i: (0, 0), pipeline_mode=pl.Buffered(1))
    half2 = pl.BlockSpec((nm, d), lambda i: (0, 1), pipeline_mode=pl.Buffered(1))
    return pl.pallas_call(
        body, grid=(s // t,),
        in_specs=[row, row, _const_spec(w_q), half, half2, _const_spec(w_o), _const_spec(gnorm)],
        out_specs=[row, row, row, row],
        out_shape=[jax.ShapeDtypeStruct((s, d), _MXU), jax.ShapeDtypeStruct((s, d), _MXU),
                   jax.ShapeDtypeStruct((s, d), _F32), jax.ShapeDtypeStruct((s, d), _MXU)],
        compiler_params=_cparams("parallel"), name=name,
    )(h2, x1, w_q, kv, kv, w_o, gnorm)


def _attn_bwd(dx2b, dx2, q, o, h2, x1, kv, w_o, w_q, gnorm, name, comm=None):
    s, d = q.shape
    nm = kv.shape[0]
    hd = d // _XA_HEADS
    t = _tile(s, _ATT_T)
    nt = s // t
    scale = hd ** -0.5

    def body(dxb_ref, dx2_ref, q_ref, o_ref, h_ref, x1_ref, k_ref, v_ref, wo_ref, wq_ref, g_ref,
             dx1_ref, dx1b_ref, dgn_ref, dk_ref, dv_ref, pwo_ref, pwq_ref, dq_ref, awo_ref, awq_ref):
        i = pl.program_id(0)
        first = i == 0

        @pl.when(first)
        def _():
            for ref in (dk_ref, dv_ref, awo_ref, awq_ref):
                ref[...] = jnp.zeros_like(ref)

        dxb = dxb_ref[...]
        do = _dot(dxb, wo_ref[...], _NT).astype(_MXU)
        awo_ref[...] += _dot(o_ref[...], dxb, _TN)
        for hh in range(_XA_HEADS):
            sl = slice(hh * hd, (hh + 1) * hd)
            qh = q_ref[:, sl]
            kh = k_ref[:, sl]
            doh = do[:, sl]
            p = _attn_probs(qh, kh, scale)
            dp = _dot(doh, v_ref[:, sl], _NT)
            dv_ref[:, sl] += _dot(p, doh, _TN)
            ds = (p * (dp - jnp.sum(dp * p, axis=-1, keepdims=True)) * scale).astype(_MXU)
            dq_ref[:, sl] = _dot(ds, kh, _NN).astype(dq_ref.dtype)
            dk_ref[:, sl] += _dot(ds, qh, _TN)
        dq = dq_ref[...]
        awq_ref[...] += _dot(h_ref[...], dq, _TN)
        dh = _dot(dq, wq_ref[...], _NT)
        _epi_rms_bwd(dh, (x1_ref, g_ref, dx2_ref), (dx1_ref, dx1b_ref, dgn_ref), first)

        @pl.when(i == nt - 1)
        def _():
            pwo_ref[...] = awo_ref[...].astype(pwo_ref.dtype)
            pwq_ref[...] = awq_ref[...].astype(pwq_ref.dtype)

    row = pl.BlockSpec((t, d), lambda i: (i, 0))
    vec = pl.BlockSpec((1, d), lambda i: (0, 0))
    mem_blk = pl.BlockSpec((nm, d), lambda i: (0, 0))
    sq = pl.BlockSpec((d, d), lambda i: (0, 0))
    half = pl.BlockSpec((nm, d), lambda i: (0, 0), pipeline_mode=pl.Buffered(1))
    half2 = pl.BlockSpec((nm, d), lambda i: (0, 1), pipeline_mode=pl.Buffered(1))
    return _pcall(
        body, args=(dx2b, dx2, q, o, h2, x1, kv, kv, w_o, w_q, gnorm), grid=(nt,),
        in_specs=[row, row, row, row, row, row, half, half2, _const_spec(w_o), _const_spec(w_q), _const_spec(gnorm)],
        out_specs=[row, row, vec, mem_blk, mem_blk, sq, sq],
        out_shape=[jax.ShapeDtypeStruct((s, d), _F32), jax.ShapeDtypeStruct((s, d), _MXU),
                   jax.ShapeDtypeStruct((1, d), _F32), jax.ShapeDtypeStruct((nm, d), _F32),
                   jax.ShapeDtypeStruct((nm, d), _F32), jax.ShapeDtypeStruct((d, d), _XFER),
                   jax.ShapeDtypeStruct((d, d), _XFER)],
        scratch_shapes=[pltpu.VMEM((t, d), _MXU), pltpu.VMEM((d, d), _F32), pltpu.VMEM((d, d), _F32)],
        sem=("arbitrary",), name=name, comm=comm, vmem=_VMEM_LIMIT_FUSED)


_FFN_K = 3
_FFN_FUSED_T = 256
_VMEM_LIMIT_FUSED = 58 * 1024 * 1024


def _ffn_fused_fwd(h3, w_up, w_down, fcw, fcb, x2, target, gfin, name):
    s, d = h3.shape
    nblk, _, n = w_up.shape
    half = nblk // 2
    f = half * n
    t = _tile(s, _FFN_FUSED_T)

    def body(h_ref, wup_ref, wdown_ref, w_ref, b_ref, x2_ref, t_ref, g_ref,
             g0_ref, act_ref, gl_ref, udgl_ref, dx_ref, dxb_ref, l_ref, dg_ref, ext0_ref, ext1_ref, halo_ref):
        i = pl.program_id(0)
        first = i == 0
        h = h_ref[...]
        total = None
        ahead = (_dot(h, wup_ref[0], _NN), _dot(h, wup_ref[half], _NN))
        for j in range(half):
            cs = slice(j * n, (j + 1) * n)
            ext_ref = ext0_ref if j % 2 == 0 else ext1_ref
            g0, u = ahead
            if j + 1 < half:
                ahead = (_dot(h, wup_ref[j + 1], _NN), _dot(h, wup_ref[half + j + 1], _NN))
            if j > 0:
                prev = slice((j - 1) * n, j * n)
                p = _dot(act_ref[:, prev], wdown_ref[prev, :], _NN)
                total = p if total is None else total + p
            g0_ref[:, cs] = g0.astype(g0_ref.dtype)
            ext_ref[0:8, :] = jnp.where(first, 0.0, halo_ref[:, cs])
            ext_ref[8:8 + t, :] = g0
            halo_ref[:, cs] = g0[t - 8:t, :]
            g = _causal_conv(ext_ref, 8, w_ref.at[:, cs], b_ref.at[:, cs], _FFN_K, t)
            gl, dgl = _gelu_and_grad(g)
            gl_ref[:, cs] = gl.astype(gl_ref.dtype)
            udgl_ref[:, cs] = (u * dgl).astype(udgl_ref.dtype)
            act_ref[:, cs] = (gl * u).astype(act_ref.dtype)
        last = slice((half - 1) * n, half * n)
        total = total + _dot(act_ref[:, last], wdown_ref[last, :], _NN)
        _epi_final(total, (x2_ref, t_ref, g_ref), (dx_ref, dxb_ref, l_ref, dg_ref), first)

    def const(shape):
        return pl.BlockSpec(shape, lambda i: (0,) * len(shape), pipeline_mode=pl.Buffered(1))

    row = pl.BlockSpec((t, d), lambda i: (i, 0))
    vec = pl.BlockSpec((1, d), lambda i: (0, 0))
    return pl.pallas_call(
        body, grid=(s // t,),
        in_specs=[row, const(w_up.shape), const(w_down.shape), const(fcw.shape), const(fcb.shape), row, row, vec],
        out_specs=[pl.BlockSpec((t, f), lambda i: (i, 0))] * 4 + [row, row, vec, vec],
        out_shape=[jax.ShapeDtypeStruct((s, f), _MXU)] * 4
        + [jax.ShapeDtypeStruct((s, d), _F32), jax.ShapeDtypeStruct((s, d), _MXU),
                   jax.ShapeDtypeStruct((1, d), _F32), jax.ShapeDtypeStruct((1, d), _F32)],
        scratch_shapes=[pltpu.VMEM((t + 8, n), _F32), pltpu.VMEM((t + 8, n), _F32), pltpu.VMEM((8, f), _F32)],
        compiler_params=_cparams("arbitrary", vmem=_VMEM_LIMIT_FUSED), name=name,
    )(h3, w_up, w_down, fcw, fcb, x2, target, gfin)


def _ffn_fused_bwd(dx3b, g0, gl, udgl, w_down, w_up, fcw, x2, gnorm, dx3, name, comm=None):
    s, d = x2.shape
    nblk, _, n = w_up.shape
    half = nblk // 2
    f = half * n
    t = _tile(s, _FFN_FUSED_T)
    nt = s // t
    hrows = 16

    def body(dxb_ref, g0_ref, g0h_ref, gl_ref, udgl_ref, wdown_ref, wup_ref, w_ref, x2_ref, g_ref, dx3_ref,
             dgu_ref, dx2_ref, dx2b_ref, dgn_ref, dw_ref, db_ref, ext0_ref, ext1_ref, up0_ref, up1_ref, car_ref):
        i = pl.program_id(0)
        first_tile = i == nt - 1
        last_tile = i == 0

        @pl.when(last_tile)
        def _():
            dw_ref[...] = jnp.zeros_like(dw_ref)
            db_ref[...] = jnp.zeros_like(db_ref)
            car_ref[...] = jnp.zeros_like(car_ref)

        dxb = dxb_ref[...]
        total = None
        for j in range(half):
            cs = slice(j * n, (j + 1) * n)
            us = slice(f + j * n, f + (j + 1) * n)
            ext_ref = ext0_ref if j % 2 == 0 else ext1_ref
            up_ref = up0_ref if j % 2 == 0 else up1_ref
            dact = _dot(dxb, wdown_ref[cs, :], _NT)
            ext_ref[0:8, :] = jnp.where(first_tile, 0.0, g0h_ref[:, cs].astype(_F32)[hrows - 8:hrows])
            ext_ref[8:8 + t, :] = g0_ref[:, cs].astype(_F32)
            du = (dact * gl_ref[:, cs].astype(_F32)).astype(dgu_ref.dtype)
            dgu_ref[:, us] = du
            dg = dact * udgl_ref[:, cs].astype(_F32)
            db_ref[:, cs] += _colsum(dg)
            for k in range(_FFN_K):
                dw_ref[k:k + 1, cs] += _colsum(dg * ext_ref[pl.ds(8 - (_FFN_K - 1) + k, t), :])
            up_ref[0:t, :] = dg
            up_ref[t:t + 8, :] = car_ref[:, cs]
            car_ref[:, cs] = dg[0:8, :]
            dg0 = w_ref[0:1, cs] * up_ref[pl.ds(_FFN_K - 1, t), :]
            for k in range(1, _FFN_K):
                dg0 = dg0 + w_ref[k:k + 1, cs] * up_ref[pl.ds(_FFN_K - 1 - k, t), :]
            dg0 = dg0.astype(dgu_ref.dtype)
            dgu_ref[:, cs] = dg0
            p = _dot(dg0, wup_ref[j], _NT) + _dot(du, wup_ref[half + j], _NT)
            total = p if total is None else total + p
        _epi_rms_bwd(total, (x2_ref, g_ref, dx3_ref), (dx2_ref, dx2b_ref, dgn_ref), last_tile)

    def const(shape):
        return pl.BlockSpec(shape, lambda i: (0,) * len(shape), pipeline_mode=pl.Buffered(1))

    row = pl.BlockSpec((t, d), lambda i: (nt - 1 - i, 0))
    vec = pl.BlockSpec((1, d), lambda i: (0, 0))
    per = t // hrows
    wide = pl.BlockSpec((t, f), lambda i: (nt - 1 - i, 0))
    return _pcall(
        body, args=(dx3b, g0, g0, gl, udgl, w_down, w_up, fcw, x2, gnorm, dx3), grid=(nt,),
        in_specs=[row, wide, pl.BlockSpec((hrows, f), lambda i: (jnp.maximum((nt - 1 - i) * per - 1, 0), 0)),
                  wide, wide, const(w_down.shape), const(w_up.shape), const(fcw.shape), row, vec, row],
        out_specs=[pl.BlockSpec((t, 2 * f), lambda i: (nt - 1 - i, 0)), row, row, vec,
                   pl.BlockSpec((_FFN_K, f), lambda i: (0, 0)), pl.BlockSpec((1, f), lambda i: (0, 0))],
        out_shape=[jax.ShapeDtypeStruct((s, 2 * f), _MXU), jax.ShapeDtypeStruct((s, d), _F32),
                   jax.ShapeDtypeStruct((s, d), _MXU), jax.ShapeDtypeStruct((1, d), _F32),
                   jax.ShapeDtypeStruct((_FFN_K, f), _F32), jax.ShapeDtypeStruct((1, f), _F32)],
        scratch_shapes=[pltpu.VMEM((t + 8, n), _F32), pltpu.VMEM((t + 8, n), _F32),
                        pltpu.VMEM((t + 8, n), _F32), pltpu.VMEM((t + 8, n), _F32), pltpu.VMEM((8, f), _F32)],
        sem=("arbitrary",), name=name, comm=comm, vmem=_VMEM_LIMIT_FUSED)


def _mesh_pos():
    return lax.axis_index("x"), lax.axis_index("y"), lax.axis_index("c")


def _flip(v, bit):
    return 1 - v if bit else v


def _sem_scratch(n):
    return [pltpu.SemaphoreType.DMA((7 * n,)), pltpu.SemaphoreType.DMA((7 * n,)), pltpu.SemaphoreType.DMA((n,))]


class _Gather:
    def __init__(self, xs):
        self.ins = list(xs)
        self.outs = [jax.ShapeDtypeStruct((_NDEV,) + v.shape, v.dtype) for v in xs]
        self.scratch = _sem_scratch(len(xs))

    def _plan(self, x_refs, out_refs, sems):
        send_sems, recv_sems, local_sems = sems
        x, y, c = _mesh_pos()
        me, sibling = (x, y, c), (x, y, 1 - c)
        chips = [(1 - x, y), (x, 1 - y), (1 - x, 1 - y)]

        def copy(a, k, block, to, src=None):
            slot = out_refs[a].at[4 * block[0] + 2 * block[1] + block[2]]
            return pltpu.make_async_remote_copy(
                src_ref=slot if src is None else src, dst_ref=slot,
                send_sem=send_sems.at[a * 7 + k], recv_sem=recv_sems.at[a * 7 + k],
                device_id=to, device_id_type=_MESH_ID)

        def own(a):
            return pltpu.make_async_copy(x_refs[a], out_refs[a].at[4 * x + 2 * y + c], local_sems.at[a])

        def first(a):
            return [copy(a, 0, me, sibling, src=x_refs[a])] + [
                copy(a, 1 + j, me, (*chip, c), src=x_refs[a]) for j, chip in enumerate(chips)]

        return me, sibling, chips, c, copy, own, first

    def start(self, x_refs, out_refs, sems):
        _, _, _, _, _, own, first = self._plan(x_refs, out_refs, sems)
        for a in range(len(self.ins)):
            own(a).start()
            for cp in first(a):
                cp.start()

    def finish(self, x_refs, out_refs, sems):
        me, sibling, chips, c, copy, own, first = self._plan(x_refs, out_refs, sems)
        n = len(self.ins)
        passed = []
        for a in range(n):
            for j, chip in enumerate(chips):
                copy(a, 1 + j, (*chip, c), me).wait_recv()
                fwd = copy(a, 4 + j, (*chip, c), sibling)
                fwd.start()
                passed.append(fwd)
        for a in range(n):
            copy(a, 0, sibling, me).wait_recv()
            for j, chip in enumerate(chips):
                copy(a, 4 + j, (*chip, 1 - c), me).wait_recv()
        for a in range(n):
            for cp in first(a):
                cp.wait_send()
        for cp in passed:
            cp.wait_send()
        for a in range(n):
            own(a).wait()


class _Exchange:
    def __init__(self, gs):
        self.ins = list(gs)
        self.outs = [jax.ShapeDtypeStruct(v.shape, v.dtype) for v in gs]
        self.scratch = _sem_scratch(len(gs))

    def _plan(self, g_refs, r_refs, sems):
        send_sems, recv_sems, local_sems = sems
        x, y, c = _mesh_pos()
        me_idx = 4 * x + 2 * y + c
        n = len(self.ins)

        def copy(a, k):
            peer = (_flip(x, k & 4), _flip(y, k & 2), _flip(c, k & 1))
            peer_idx = 4 * peer[0] + 2 * peer[1] + peer[2]
            return pltpu.make_async_remote_copy(
                src_ref=g_refs[a].at[peer_idx], dst_ref=r_refs[a].at[me_idx],
                send_sem=send_sems.at[a * 7 + k - 1], recv_sem=recv_sems.at[a * 7 + k - 1],
                device_id=peer, device_id_type=_MESH_ID)

        copies = [copy(a, k) for a in range(n) for k in range(1, _NDEV)]
        mine = [pltpu.make_async_copy(g_refs[a].at[me_idx], r_refs[a].at[me_idx], local_sems.at[a])
                for a in range(n)]
        return copies, mine

    def start(self, g_refs, r_refs, sems):
        copies, mine = self._plan(g_refs, r_refs, sems)
        for cp in copies + mine:
            cp.start()

    def finish(self, g_refs, r_refs, sems):
        copies, mine = self._plan(g_refs, r_refs, sems)
        for cp in copies:
            cp.wait_recv()
        for cp in copies:
            cp.wait_send()
        for cp in mine:
            cp.wait()


class _Both:
    def __init__(self, first, second):
        self.parts = (first, second)
        self.ins = first.ins + second.ins
        self.outs = first.outs + second.outs
        self.scratch = first.scratch + second.scratch

    def _split(self, ins, outs, sems):
        a, b = self.parts
        na, nb = len(a.ins), len(a.scratch)
        return (a, ins[:na], outs[:na], sems[:nb]), (b, ins[na:], outs[na:], sems[nb:])

    def start(self, ins, outs, sems):
        for part, i, o, s in self._split(ins, outs, sems):
            part.start(i, o, s)

    def finish(self, ins, outs, sems):
        for part, i, o, s in self._split(ins, outs, sems):
            part.finish(i, o, s)


def _comm_call(comm, name):
    def body(*refs):
        n_i, n_o = len(comm.ins), len(comm.outs)
        ins, outs, sems = refs[:n_i], refs[n_i:n_i + n_o], refs[n_i + n_o:]
        comm.start(ins, outs, sems)
        comm.finish(ins, outs, sems)

    return pl.pallas_call(
        body, out_shape=list(comm.outs), in_specs=[_ANY] * len(comm.ins), out_specs=[_ANY] * len(comm.outs),
        scratch_shapes=list(comm.scratch), name=name)(*comm.ins)


def _adamw_math(w, g, m, v):
    m = _ADAM_B1 * m + (1.0 - _ADAM_B1) * g
    v = _ADAM_B2 * v + (1.0 - _ADAM_B2) * (g * g)
    m_hat = m / (1.0 - _ADAM_B1 ** _ADAM_STEP)
    v_hat = v / (1.0 - _ADAM_B2 ** _ADAM_STEP)
    delta = -_ADAM_LR * (m_hat / (jnp.sqrt(v_hat) + _ADAM_EPS) + _ADAM_WD * w)
    return delta, m, v


def _sum_adamw(parts, w, m, v, name):
    r, c = w.shape
    tr = _tile(r, 128)

    def body(p_ref, w_ref, m_ref, v_ref, g_ref, d_ref, nm_ref, nv_ref):
        g = p_ref[0].astype(_F32)
        for j in range(1, _NDEV):
            g = g + p_ref[j].astype(_F32)
        delta, nm, nv = _adamw_math(w_ref[...], g, m_ref[...], v_ref[...])
        g_ref[...] = g
        d_ref[...] = delta
        nm_ref[...] = nm
        nv_ref[...] = nv

    blk = pl.BlockSpec((tr, c), lambda i: (i, 0))
    return pl.pallas_call(
        body, grid=(r // tr,),
        in_specs=[pl.BlockSpec((_NDEV, tr, c), lambda i: (0, i, 0)), blk, blk, blk],
        out_specs=[blk] * 4, out_shape=[jax.ShapeDtypeStruct((r, c), _F32)] * 4,
        compiler_params=_cparams("parallel"), name=name,
    )(parts, w, m, v)


def _sum8(parts, name):
    _, r, c = parts.shape

    def body(p_ref, o_ref):
        g = p_ref[0]
        for j in range(1, _NDEV):
            g = g + p_ref[j]
        o_ref[...] = g

    return pl.pallas_call(
        body, grid=(1,), in_specs=[pl.BlockSpec((_NDEV, r, c), lambda i: (0, 0, 0))],
        out_specs=pl.BlockSpec((r, c), lambda i: (0, 0)), out_shape=jax.ShapeDtypeStruct((r, c), _F32),
        compiler_params=_cparams("arbitrary"), name=name,
    )(parts)


def _adamw_many(gs, ws, ms, vs, name):
    n = len(ws)

    def body(*refs):
        g_refs, w_refs, m_refs, v_refs = (refs[k * n:(k + 1) * n] for k in range(4))
        d_refs, nm_refs, nv_refs = (refs[(4 + k) * n:(5 + k) * n] for k in range(3))
        for k in range(n):
            delta, nm, nv = _adamw_math(w_refs[k][...], g_refs[k][...], m_refs[k][...], v_refs[k][...])
            d_refs[k][...] = delta
            nm_refs[k][...] = nm
            nv_refs[k][...] = nv

    def whole(arr):
        return pl.BlockSpec(arr.shape, lambda i, nd=arr.ndim: (0,) * nd)

    specs = [whole(w) for w in ws]
    res = pl.pallas_call(
        body, grid=(1,), in_specs=specs * 4, out_specs=specs * 3,
        out_shape=[jax.ShapeDtypeStruct(w.shape, _F32) for w in ws] * 3,
        compiler_params=_cparams("arbitrary"), name=name,
    )(*gs, *ws, *ms, *vs)
    return res[:n], res[n:2 * n], res[2 * n:]


def _pack(arrs):
    flat = jnp.concatenate([a.reshape(-1).astype(_F32) for a in arrs])
    pad = (-flat.shape[0]) % 1024
    return jnp.pad(flat, (0, pad)).reshape(-1, 128)


def _unpack(flat2d, shapes):
    flat = flat2d.reshape(-1)
    out, off = [], 0
    for sh in shapes:
        size = 1
        for dim in sh:
            size *= dim
        out.append(flat[off:off + size].reshape(sh))
        off += size
    return out


def _block_diag(w):
    h, hd, _ = w.shape
    eye = jnp.eye(h, dtype=w.dtype)
    return (eye[:, None, :, None] * w[:, :, None, :]).reshape(h * hd, h * hd)


def _diag_blocks(full, h):
    hd = full.shape[0] // h
    return jnp.stack([full[i * hd:(i + 1) * hd, i * hd:(i + 1) * hd] for i in range(h)])


def kernel(x, mem, mix_norm_g, w_in, lru_conv_w, lru_conv_b, lru_w_a, lru_b_a, lru_w_x, lru_b_x, lru_lambda, conf_conv_w, conf_conv_b, conf_ln_g, conf_ln_b, w_out, xa_norm_g, mem_norm_g, w_q, w_kv, w_o, ffn_norm_g, w_up, ffn_conv_w, ffn_conv_b, w_down, final_norm_g, loss_target, m_mix_norm_g, m_w_in, m_lru_conv_w, m_lru_conv_b, m_lru_w_a, m_lru_b_a, m_lru_w_x, m_lru_b_x, m_lru_lambda, m_conf_conv_w, m_conf_conv_b, m_conf_ln_g, m_conf_ln_b, m_w_out, m_xa_norm_g, m_mem_norm_g, m_w_q, m_w_kv, m_w_o, m_ffn_norm_g, m_w_up, m_ffn_conv_w, m_ffn_conv_b, m_w_down, m_final_norm_g, v_mix_norm_g, v_w_in, v_lru_conv_w, v_lru_conv_b, v_lru_w_a, v_lru_b_a, v_lru_w_x, v_lru_b_x, v_lru_lambda, v_conf_conv_w, v_conf_conv_b, v_conf_ln_g, v_conf_ln_b, v_w_out, v_xa_norm_g, v_mem_norm_g, v_w_q, v_w_kv, v_w_o, v_ffn_norm_g, v_w_up, v_ffn_conv_w, v_ffn_conv_b, v_w_down, v_final_norm_g):
    names = ["mix_norm_g", "w_in", "lru_conv_w", "lru_conv_b", "lru_w_a", "lru_b_a", "lru_w_x", "lru_b_x",
             "lru_lambda", "conf_conv_w", "conf_conv_b", "conf_ln_g", "conf_ln_b", "w_out", "xa_norm_g",
             "mem_norm_g", "w_q", "w_kv", "w_o", "ffn_norm_g", "w_up", "ffn_conv_w", "ffn_conv_b", "w_down",
             "final_norm_g"]
    loc = locals()
    W = {n: loc[n] for n in names}
    M = {n: loc["m_" + n] for n in names}
    V = {n: loc["v_" + n] for n in names}
    big = ["w_in", "w_out", "w_q", "w_kv", "w_o", "w_up", "w_down"]
    conv_sharded = ["lru_conv_w", "conf_conv_w", "ffn_conv_w"]

    xs = x[0]
    mems = mem[0]
    tgt = loss_target[0]
    me = 4 * lax.axis_index("x") + 2 * lax.axis_index("y") + lax.axis_index("c")

    conv_shapes = [W[n].shape[1:] for n in conv_sharded]
    conv_pack = _pack([W[n][0] for n in conv_sharded])
    shard = {n: W[n][0].astype(_XFER) for n in big}
    h1, (g_in, g_out, g_conv) = _rms_fwd(
        xs, mix_norm_g, "rms1_fwd", comm=_Gather([shard["w_in"], shard["w_out"], conv_pack]))
    convs = [[] for _ in conv_sharded]
    for j in range(_NDEV):
        for idx, part in enumerate(_unpack(g_conv[j], conv_shapes)):
            convs[idx].append(part)
    lcw, ccw, fcw = [jnp.concatenate(parts, axis=-1) for parts in convs]

    wab = jnp.concatenate([_block_diag(lru_w_a[0]), _block_diag(lru_w_x[0])], axis=1).astype(_MXU)
    mixer_params = (lcw, lru_conv_b, wab, lru_b_a, lru_b_x, lru_lambda, ccw, conf_conv_b, conf_ln_g, conf_ln_b)

    w_out_f = g_out.reshape(-1, g_out.shape[-1])
    (z, ycat, hs, cc, x1, h2), (g_q, g_kv, g_o, g_up, g_down) = _mixer_fwd(
        xs, h1, g_in, w_out_f, xa_norm_g, *mixer_params, "mixer_fwd",
        comm=_Gather([shard[n] for n in ("w_q", "w_kv", "w_o", "w_up", "w_down")]))
    w_q_f = g_q.reshape(-1, g_q.shape[-1])
    w_o_f = g_o.reshape(-1, g_o.shape[-1])
    w_down_f = g_down.reshape(-1, g_down.shape[-1])
    row32, row16, vec32 = (_F32, "row"), (_MXU, "row"), (_F32, "vec")
    mn = _rms_fwd(mems, mem_norm_g, "rmsm_fwd")
    kv = _mm_nn_stacked(mn, g_kv, _MXU, "mm_kv_fwd")
    q, o, x2, h3 = _attn_fwd(h2, x1, w_q_f, kv, w_o_f, ffn_norm_g, "attn_fwd")

    gfin = final_norm_g.reshape(1, -1)
    g0, act, gelu_g, u_dgelu, dx3, dx3b, lvec, dg_final = _ffn_fused_fwd(
        h3, g_up, w_down_f, fcw, ffn_conv_b, x2, tgt, gfin, "ffn_fwd")

    def rows8(p):
        return p.reshape(_NDEV, p.shape[0] // _NDEV, p.shape[1])

    p_down = _mm_tn_nat(act, dx3b, _XFER, "mm_down_wgrad", ts=2048)
    (dgu, dx2, dx2b, dg_ffn, dfcw, dfcb), _ = _ffn_fused_bwd(
        dx3b, g0, gelu_g, u_dgelu, w_down_f, g_up, fcw, x2, ffn_norm_g, dx3, "ffn_bwd")
    p_up = _mm_tn_stacked(h3, dgu, _NDEV, _XFER, "mm_up_wgrad", slabs=1, ts=4096)

    (dx1, dx1b, dg_xa, dk, dv, p_o, p_q), (r_up,) = _attn_bwd(
        dx2b, dx2, q, o, h2, x1, kv, w_o_f, w_q_f, xa_norm_g, "attn_bwd", comm=_Exchange([p_up]))
    dkv = jnp.concatenate([dk, dv], axis=1).astype(_MXU)
    dmn = _mm_nt_stacked(dkv, g_kv, "mm_kv_dgrad", outs=[row32], slabs=_NDEV)
    p_kv = _mm_tn_stacked(mn, dkv, _NDEV, _XFER, "mm_kv_wgrad", slabs=_NDEV)
    _, _, dg_mem = _rms_bwd(dmn, mems, mem_norm_g, None, "rmsm_bwd")

    p_out = _mm_tn_nat(ycat, dx1b, _XFER, "mm_out_wgrad", ts=2048)
    ((dz, p_in, dlcw, dlcb, dwab, dba, dbx, dlam, dccw, dccb, dlng, dlnb),
     (r_down, r_o, r_q, r_kv, r_out)) = _mixer_bwd(
        dx1b, w_out_f, h1, z, hs, cc, *mixer_params, "mixer_bwd",
        comm=_Exchange([rows8(p_down), rows8(p_o), rows8(p_q), p_kv, rows8(p_out)]))

    c = _D_LRU
    heads = lru_w_a.shape[1]
    small_partial = {
        "lru_conv_w": dlcw, "lru_conv_b": dlcb,
        "lru_w_a": _diag_blocks(dwab[:, :c], heads), "lru_b_a": dba,
        "lru_w_x": _diag_blocks(dwab[:, c:], heads), "lru_b_x": dbx, "lru_lambda": dlam,
        "conf_conv_w": dccw, "conf_conv_b": dccb, "conf_ln_g": dlng, "conf_ln_b": dlnb,
        "xa_norm_g": dg_xa, "mem_norm_g": dg_mem, "ffn_norm_g": dg_ffn,
        "ffn_conv_w": dfcw, "ffn_conv_b": dfcb, "final_norm_g": dg_final,
    }
    early = list(small_partial)
    early_shapes = [small_partial[n].shape for n in early] + [lvec.shape]
    (grad_x, dg_mix), (r_in, early_all) = _mm_nt_stacked(
        dz, g_in, "mm_in_dgrad", epi=_epi_rms_bwd, extra=[(xs, "row"), (mix_norm_g, "vec"), (dx1, "row")],
        outs=[row32, vec32], tm=512, slabs=_NDEV,
        comm=_Both(_Exchange([p_in]), _Gather([_pack([small_partial[n] for n in early] + [lvec])])))
    (mix_all,) = _comm_call(_Gather([dg_mix]), "gather_mix_grad")
    small = early + ["mix_norm_g"]
    small_sum = _unpack(_sum8(early_all, "sum_small_grads"), early_shapes)
    loss = 0.5 * jnp.sum(small_sum.pop()) / xs.shape[1]
    small_sum.append(_sum8(mix_all.reshape(_NDEV, 8, -1), "sum_mix_grad").reshape(dg_mix.shape))
    received = {"w_in": r_in, "w_out": r_out, "w_q": r_q, "w_kv": r_kv, "w_o": r_o, "w_up": r_up,
                "w_down": r_down}

    grads, deltas, new_m, new_v = {}, {}, {}, {}
    for n, rec in ((n, received[n]) for n in big):
        shp = W[n].shape
        w2, m2, v2 = (t.reshape(shp[1:]) for t in (W[n], M[n], V[n]))
        outs = _sum_adamw(rec, w2, m2, v2, "adamw_" + n)
        grads[n], deltas[n], new_m[n], new_v[n] = (t.reshape(shp) for t in outs)

    small_g = []
    for n, g in zip(small, small_sum):
        if n in conv_sharded:
            width = W[n].shape[-1]
            g = lax.dynamic_slice_in_dim(g, me * width, width, axis=1)
        small_g.append(g.reshape(W[n].shape))

    def at_least_2d(a):
        return a.reshape(1, -1) if a.ndim == 1 else a

    sd, sm, sv = _adamw_many([at_least_2d(g) for g in small_g], [at_least_2d(W[n]) for n in small],
                             [at_least_2d(M[n]) for n in small], [at_least_2d(V[n]) for n in small], "adamw_small")
    for n, g, d_, m_, v_ in zip(small, small_g, sd, sm, sv):
        shp = W[n].shape
        grads[n], deltas[n], new_m[n], new_v[n] = g, d_.reshape(shp), m_.reshape(shp), v_.reshape(shp)

    return (loss, grad_x[None], *[grads[n] for n in names], *[deltas[n] for n in names],
            *[new_m[n] for n in names], *[new_v[n] for n in names])
```

```python
import functools

import jax
import jax.numpy as jnp
from jax import lax
from jax.experimental import pallas as pl
from jax.experimental.pallas import tpu as pltpu

_MXU = jnp.bfloat16
_XFER = jnp.bfloat16
_F32 = jnp.float32
_EPS = 1e-6
_NDEV = 8
_VMEM_LIMIT = 48 * 1024 * 1024

_D_LRU = 512
_XA_HEADS = 4
_RG_C = 8.0
_ADAM_LR, _ADAM_B1, _ADAM_B2, _ADAM_EPS, _ADAM_WD, _ADAM_STEP = 0.001, 0.9, 0.999, 1e-08, 0.01, 10

_MESH_ID = pl.DeviceIdType.MESH
_ANY = pl.BlockSpec(memory_space=pl.ANY)


def _cparams(*sem, vmem=_VMEM_LIMIT):
    return pltpu.CompilerParams(dimension_semantics=tuple(sem), vmem_limit_bytes=vmem)


def _pcall(body, *, args, grid, in_specs, out_specs, out_shape, sem, name, scratch_shapes=(), comm=None,
           vmem=_VMEM_LIMIT):
    outs_l = list(out_shape) if isinstance(out_shape, (list, tuple)) else [out_shape]
    ospecs_l = list(out_specs) if isinstance(out_specs, (list, tuple)) else [out_specs]
    n_in, n_out, n_scr = len(args), len(outs_l), len(scratch_shapes)
    if comm is None:
        res = pl.pallas_call(
            body, grid=grid, in_specs=list(in_specs), out_specs=ospecs_l, out_shape=outs_l,
            scratch_shapes=list(scratch_shapes), compiler_params=_cparams(*sem, vmem=vmem), name=name)(*args)
        return list(res), []
    n_ci, n_co = len(comm.ins), len(comm.outs)

    def wrapped(*refs):
        ins, cins = refs[:n_in], refs[n_in:n_in + n_ci]
        o = n_in + n_ci
        outs, couts = refs[o:o + n_out], refs[o + n_out:o + n_out + n_co]
        s = o + n_out + n_co
        scr, cscr = refs[s:s + n_scr], refs[s + n_scr:]
        first = pl.program_id(0) == 0
        last = pl.program_id(0) == grid[0] - 1
        for ax in range(1, len(grid)):
            first = jnp.logical_and(first, pl.program_id(ax) == 0)
            last = jnp.logical_and(last, pl.program_id(ax) == grid[ax] - 1)

        @pl.when(first)
        def _():
            comm.start(cins, couts, cscr)

        body(*ins, *outs, *scr)

        @pl.when(last)
        def _():
            comm.finish(cins, couts, cscr)

    res = pl.pallas_call(
        wrapped, grid=grid, in_specs=list(in_specs) + [_ANY] * n_ci, out_specs=ospecs_l + [_ANY] * n_co,
        out_shape=outs_l + list(comm.outs), scratch_shapes=list(scratch_shapes) + list(comm.scratch),
        compiler_params=_cparams(*(("arbitrary",) * len(grid)), vmem=vmem), name=name)(*args, *comm.ins)
    return list(res[:n_out]), list(res[n_out:])


def _sigmoid(v):
    return 1.0 / (1.0 + jnp.exp(-v))


_GELU_C = 0.7978845608028654
_GELU_K = 0.044715


def _gelu(v):
    t = jnp.tanh(_GELU_C * (v + _GELU_K * v * v * v))
    return 0.5 * v * (1.0 + t)


def _gelu_and_grad(v):
    v2 = v * v
    s = 0.5 * jnp.tanh(v * (_GELU_C + (_GELU_C * _GELU_K) * v2)) + 0.5
    g = v * s
    dg = s + (g * (1.0 - s)) * ((2.0 * _GELU_C) + (6.0 * _GELU_C * _GELU_K) * v2)
    return g, dg


def _softplus(v):
    e = jnp.exp(-jnp.abs(v))
    log1p = jnp.where(e < 1e-2, e * (1.0 - e * (0.5 - e * (1.0 / 3.0))), jnp.log(1.0 + e))
    return jnp.maximum(v, 0.0) + log1p


def _neg_expm1(v):
    series = -v * (1.0 + v * (0.5 + v * ((1.0 / 6.0) + v * (1.0 / 24.0))))
    return jnp.where(v > -0.0625, series, 1.0 - jnp.exp(v))


def _dot(a, b, dims):
    return lax.dot_general(a.astype(_MXU), b.astype(_MXU), (dims, ((), ())), preferred_element_type=_F32)


_NN = ((1,), (0,))
_NT = ((1,), (1,))
_TN = ((0,), (0,))


def _scan_fwd(a, b, rows):
    n = a.shape[0]
    d = 1
    while d < n:
        keep = rows >= d
        b = jnp.where(keep, b + a * pltpu.roll(b, d, 0), b)
        a = jnp.where(keep, a * pltpu.roll(a, d, 0), a)
        d *= 2
    return a, b


def _scan_rev(a, b, rows):
    n = a.shape[0]
    d = 1
    while d < n:
        keep = rows < n - d
        b = jnp.where(keep, b + a * pltpu.roll(b, n - d, 0), b)
        a = jnp.where(keep, a * pltpu.roll(a, n - d, 0), a)
        d *= 2
    return a, b


def _colsum(v):
    return jnp.sum(v, axis=0, keepdims=True)


def _mm(a, b, *, dims, grid, a_spec, b_spec, outs, acc_shape, name, extra=(), epi=None, slabs=1, comm=None):
    nred = grid[-1]
    red_axis = len(grid) - 1
    n_ex, n_out = len(extra), len(outs)
    epi = _epi_store if epi is None else epi

    def body(*refs):
        a_ref, b_ref = refs[:2]
        ex, o_refs, acc_ref = refs[2:2 + n_ex], refs[2 + n_ex:2 + n_ex + n_out], refs[-1]
        if slabs == 1:
            p = _dot(a_ref[...], b_ref[...], dims)
        else:
            n = b_ref.shape[-1]
            p = _dot(a_ref[:, 0:n], b_ref[0], dims)
            for jj in range(1, slabs):
                p = p + _dot(a_ref[:, jj * n:(jj + 1) * n], b_ref[jj], dims)

        first_rows = pl.program_id(0) == 0
        if nred == 1:
            epi(p, ex, o_refs, first_rows)
        else:
            k = pl.program_id(red_axis)

            @pl.when(k == 0)
            def _():
                acc_ref[...] = p

            @pl.when(jnp.logical_and(k > 0, k < nred - 1))
            def _():
                acc_ref[...] += p

            @pl.when(k == nred - 1)
            def _():
                epi(acc_ref[...] + p, ex, o_refs, first_rows)

    sem = ("parallel",) * (len(grid) - 1) + ("arbitrary",)
    if any(o[0].shape[0] == 1 for o in outs):
        sem = ("arbitrary",) * len(grid)
    res, cres = _pcall(
        body, args=(a, b) + tuple(e[0] for e in extra), grid=grid,
        in_specs=[a_spec, b_spec] + [e[1] for e in extra],
        out_specs=[o[1] for o in outs], out_shape=[o[0] for o in outs],
        scratch_shapes=[pltpu.VMEM(acc_shape if nred > 1 else (8, 128), _F32)], sem=sem, name=name, comm=comm)
    res = res[0] if n_out == 1 else res
    return res if comm is None else (res, cres)


def _epi_store(total, ex, outs, first_rows):
    outs[0][...] = total.astype(outs[0].dtype)


def _epi_residual_rms(total, ex, outs, first_rows):
    res_ref, g_ref = ex
    xn = total + res_ref[...]
    outs[0][...] = xn
    r = lax.rsqrt(jnp.mean(xn * xn, axis=-1, keepdims=True) + _EPS)
    outs[1][...] = (xn * r * g_ref[...]).astype(outs[1].dtype)


def _epi_rms_bwd(total, ex, outs, first_rows):
    x_ref, g_ref, dres_ref = ex
    dg_ref = outs[-1]
    xv = x_ref[...]
    r = lax.rsqrt(jnp.mean(xv * xv, axis=-1, keepdims=True) + _EPS)
    xhat = xv * r
    dxh = total * g_ref[...]
    dx = dres_ref[...] + r * (dxh - xhat * jnp.mean(dxh * xhat, axis=-1, keepdims=True))
    for o_ref in outs[:-1]:
        o_ref[...] = dx.astype(o_ref.dtype)

    @pl.when(first_rows)
    def _():
        dg_ref[...] = jnp.zeros_like(dg_ref)

    dg_ref[...] += _colsum(total * xhat)


def _epi_final(total, ex, outs, first_rows):
    res_ref, t_ref, g_ref = ex
    dx_ref, dxb_ref, l_ref, dg_ref = outs
    xv = total + res_ref[...]
    gv = g_ref[...]
    d = xv.shape[-1]
    r = lax.rsqrt(jnp.mean(xv * xv, axis=-1, keepdims=True) + _EPS)
    xhat = xv * r
    err = xhat * gv - t_ref[...]
    dy = err * (1.0 / d)
    dxh = dy * gv
    dx = r * (dxh - xhat * jnp.mean(dxh * xhat, axis=-1, keepdims=True))
    dx_ref[...] = dx
    dxb_ref[...] = dx.astype(dxb_ref.dtype)

    @pl.when(first_rows)
    def _():
        l_ref[...] = jnp.zeros_like(l_ref)
        dg_ref[...] = jnp.zeros_like(dg_ref)

    l_ref[...] += _colsum(err * err)
    dg_ref[...] += _colsum(dy * xhat)


def _tile(m, cap):
    t = min(m, cap)
    assert m % t == 0
    return t


def _row_spec(tm, n):
    return pl.BlockSpec((tm, n), lambda i, *_: (i, 0))


def _vec_spec(n):
    return pl.BlockSpec((1, n), lambda *_: (0, 0))


def _row_io(m, n, tm, extra, outs):
    def spec(kind):
        return _row_spec(tm, n) if kind == "row" else _vec_spec(n)

    ex = [(arr, spec(kind)) for arr, kind in extra]
    os_ = [(jax.ShapeDtypeStruct((m, n) if kind == "row" else (1, n), dt), spec(kind)) for dt, kind in outs]
    return ex, os_


def _mm_nn_stacked(a, w, out_dtype, name, comm=None, tm=1024):
    m, k = a.shape
    j, _, n = w.shape
    tm = _tile(m, tm)
    return _mm(a, w, dims=_NN, grid=(m // tm, j, 1),
               a_spec=pl.BlockSpec((tm, k), lambda i, jj, r: (i, 0)),
               b_spec=pl.BlockSpec((None, k, n), lambda i, jj, r: (jj, 0, 0)),
               outs=[(jax.ShapeDtypeStruct((m, j * n), out_dtype), pl.BlockSpec((tm, n), lambda i, jj, r: (i, jj)))],
               acc_shape=(tm, n), name=name, comm=comm)


def _mm_nt_stacked(dc, w, name, *, outs, extra=(), epi=None, comm=None, tm=1024, slabs=1):
    m = dc.shape[0]
    j, k, n = w.shape
    tm = _tile(m, tm)
    assert j % slabs == 0
    ex, os_ = _row_io(m, k, tm, extra, outs)
    wblk = (None, k, n) if slabs == 1 else (slabs, k, n)
    return _mm(dc, w, dims=_NT, grid=(m // tm, j // slabs),
               a_spec=pl.BlockSpec((tm, slabs * n), lambda i, r: (i, r)),
               b_spec=pl.BlockSpec(wblk, lambda i, r: (r, 0, 0)),
               outs=os_, extra=ex, epi=epi, acc_shape=(tm, k), name=name, slabs=slabs, comm=comm)


def _mm_tn_stacked(a, dc, j, out_dtype, name, slabs=1, ts=1024):
    s, k = a.shape
    n = dc.shape[1] // j
    ts = _tile(s, ts)
    assert j % slabs == 0

    def epi(total, ex, outs, first_rows):
        for jj in range(slabs):
            outs[0][jj] = total[:, jj * n:(jj + 1) * n].astype(outs[0].dtype)

    return _mm(a, dc, dims=_TN, grid=(j // slabs, s // ts),
               a_spec=pl.BlockSpec((ts, k), lambda jj, r: (r, 0)),
               b_spec=pl.BlockSpec((ts, slabs * n), lambda jj, r: (r, jj)),
               outs=[(jax.ShapeDtypeStruct((j, k, n), out_dtype),
                      pl.BlockSpec((slabs, k, n), lambda jj, r: (jj, 0, 0)))],
               epi=epi, acc_shape=(k, slabs * n), name=name)


def _mm_tn_nat(a, dc, out_dtype, name, ts=1024):
    s, kt = a.shape
    n = dc.shape[1]
    ts = _tile(s, ts)
    tkb = _tile(kt, 1024)
    return _mm(a, dc, dims=_TN, grid=(kt // tkb, s // ts),
               a_spec=pl.BlockSpec((ts, tkb), lambda kb, r: (r, kb)),
               b_spec=pl.BlockSpec((ts, n), lambda kb, r: (r, 0)),
               outs=[(jax.ShapeDtypeStruct((kt, n), out_dtype), pl.BlockSpec((tkb, n), lambda kb, r: (kb, 0)))],
               acc_shape=(tkb, n), name=name)


def _rms_fwd(x, g, name, comm=None):
    s, d = x.shape
    t = _tile(s, 1024)

    def body(x_ref, g_ref, h_ref):
        xv = x_ref[...]
        r = lax.rsqrt(jnp.mean(xv * xv, axis=-1, keepdims=True) + _EPS)
        h_ref[...] = (xv * r * g_ref[...]).astype(h_ref.dtype)

    res, cres = _pcall(
        body, args=(x, g), grid=(s // t,),
        in_specs=[pl.BlockSpec((t, d), lambda i: (i, 0)), pl.BlockSpec((1, d), lambda i: (0, 0))],
        out_specs=pl.BlockSpec((t, d), lambda i: (i, 0)),
        out_shape=jax.ShapeDtypeStruct((s, d), _MXU), sem=("parallel",), name=name, comm=comm)
    return res[0] if comm is None else (res[0], cres)


def _rms_bwd(dh, x, g, dres, name):
    s, d = x.shape
    t = _tile(s, 256)
    has_res = dres is not None

    def body(*refs):
        if has_res:
            dh_ref, x_ref, g_ref, dres_ref, dx_ref, dxb_ref, dg_ref = refs
        else:
            dh_ref, x_ref, g_ref, dx_ref, dxb_ref, dg_ref = refs
        xv = x_ref[...]
        dhv = dh_ref[...]
        r = lax.rsqrt(jnp.mean(xv * xv, axis=-1, keepdims=True) + _EPS)
        xhat = xv * r
        dxh = dhv * g_ref[...]
        dx = r * (dxh - xhat * jnp.mean(dxh * xhat, axis=-1, keepdims=True))
        if has_res:
            dx = dx + dres_ref[...]
        dx_ref[...] = dx
        dxb_ref[...] = dx.astype(dxb_ref.dtype)

        @pl.when(pl.program_id(0) == 0)
        def _():
            dg_ref[...] = jnp.zeros_like(dg_ref)

        dg_ref[...] += _colsum(dhv * xhat)

    row = pl.BlockSpec((t, d), lambda i: (i, 0))
    vec = pl.BlockSpec((1, d), lambda i: (0, 0))
    in_specs = [row, row, vec] + ([row] if has_res else [])
    args = (dh, x, g) + ((dres,) if has_res else ())
    return pl.pallas_call(
        body, grid=(s // t,), in_specs=in_specs, out_specs=[row, row, vec],
        out_shape=[jax.ShapeDtypeStruct((s, d), _F32), jax.ShapeDtypeStruct((s, d), _MXU),
                   jax.ShapeDtypeStruct((1, d), _F32)],
        compiler_params=_cparams("arbitrary"), name=name,
    )(*args)


_LRU_K = 4
_CONF_K = 31
_LRU_HALO = 8
_CONF_HALO = 32
_MIX_T = 512


def _lru_gates(lx, wab_ref, ba_ref, bx_ref, lam_ref):
    c = _D_LRU
    pre = _dot(lx, wab_ref[...], _NN)
    r = _sigmoid(pre[:, :c] + ba_ref[...])
    ig = _sigmoid(pre[:, c:] + bx_ref[...])
    sp = _softplus(-lam_ref[...])
    log_a = (-_RG_C) * r * sp
    a = jnp.exp(log_a)
    mult = jnp.sqrt(_neg_expm1(2.0 * log_a))
    return r, ig, sp, a, mult


def _causal_conv(ext_ref, halo, w_ref, b_ref, taps, t):
    acc = b_ref[...] + w_ref[0:1, :] * ext_ref[pl.ds(halo - (taps - 1), t), :]
    for k in range(1, taps):
        acc = acc + w_ref[k:k + 1, :] * ext_ref[pl.ds(halo - (taps - 1) + k, t), :]
    return acc


class _Windows:
    def __init__(self, ext_ref, shifted_ref, t):
        self.ext_ref, self.shifted_ref, self.t = ext_ref, shifted_ref, t
        rows = t + 24
        for r in range(1, 8):
            shifted_ref[r - 1, 0:rows, :] = ext_ref[pl.ds(r, rows), :]

    def __call__(self, off):
        q, r = divmod(off, 8)
        if r == 0:
            return self.ext_ref[pl.ds(8 * q, self.t), :]
        return self.shifted_ref[r - 1, pl.ds(8 * q, self.t), :]


def _mixer_fwd(xs, h1, w_in, w_out, gnorm, lcw, lcb, wab, ba, bx, lam, ccw, ccb, lng, lnb, name, comm=None):
    s, d = xs.shape
    nblk, _, n = w_in.shape
    c = _D_LRU
    t = _tile(s, _MIX_T)
    nt = s // t

    def body(x_ref, h_ref, win_ref, wout_ref, gn_ref,
             lcw_ref, lcb_ref, wab_ref, ba_ref, bx_ref, lam_ref, ccw_ref, ccb_ref, lng_ref, lnb_ref,
             z_ref, ycat_ref, hs_ref, cc_ref, x1_ref, h2_ref,
             ext_ref, cge_ref, hc_ref, shifted_ref, zprev_ref):
        i = pl.program_id(0)
        first = i == 0
        rows = lax.broadcasted_iota(jnp.int32, (t, c), 0)

        @pl.when(first)
        def _():
            zprev_ref[...] = jnp.zeros_like(zprev_ref)

        hv = h_ref[...]
        for j in range(nblk):
            z_ref[:, j * n:(j + 1) * n] = _dot(hv, win_ref[j], _NN)
        lx0_ref, gate_ref = z_ref.at[:, 0:c], z_ref.at[:, c:2 * c]
        ca_ref, cb_ref = z_ref.at[:, 2 * c:3 * c], z_ref.at[:, 3 * c:4 * c]
        lx0h_ref = zprev_ref.at[_CONF_HALO - _LRU_HALO:_CONF_HALO, 0:c]
        cah_ref, cbh_ref = zprev_ref.at[:, 2 * c:3 * c], zprev_ref.at[:, 3 * c:4 * c]

        ext_ref[0:_LRU_HALO, :] = jnp.where(first, 0.0, lx0h_ref[...])
        ext_ref[_LRU_HALO:_LRU_HALO + t, :] = lx0_ref[...]
        lx = _causal_conv(ext_ref, _LRU_HALO, lcw_ref, lcb_ref, _LRU_K, t)
        r, ig, sp, a, mult = _lru_gates(lx, wab_ref, ba_ref, bx_ref, lam_ref)
        u = mult * (ig * lx)
        a_cum, h_loc = _scan_fwd(a, u, rows)

        @pl.when(first)
        def _():
            hc_ref[...] = jnp.zeros_like(hc_ref)

        h = h_loc + a_cum * hc_ref[7:8, :]
        hs_ref[...] = h
        hc_ref[...] = hs_ref[pl.ds(t - 8, 8), :]
        ycat_ref[:, 0:c] = (h * _gelu(gate_ref[...])).astype(ycat_ref.dtype)

        cge_ref[0:_CONF_HALO, :] = jnp.where(first, 0.0, cah_ref[...] * _sigmoid(cbh_ref[...]))
        cge_ref[_CONF_HALO:_CONF_HALO + t, :] = ca_ref[...] * _sigmoid(cb_ref[...])
        win = _Windows(cge_ref, shifted_ref, t)
        first_off = _CONF_HALO - (_CONF_K - 1)
        cc = ccb_ref[...] + ccw_ref[0:1, :] * win(first_off)
        for k in range(1, _CONF_K):
            cc = cc + ccw_ref[k:k + 1, :] * win(first_off + k)
        cc_ref[...] = cc
        xc = cc - jnp.mean(cc, axis=-1, keepdims=True)
        rstd = lax.rsqrt(jnp.mean(xc * xc, axis=-1, keepdims=True) + _EPS)
        ln = xc * rstd * lng_ref[...] + lnb_ref[...]
        ycat_ref[:, c:2 * c] = (ln * _sigmoid(ln)).astype(ycat_ref.dtype)

        zprev_ref[...] = z_ref[pl.ds(t - _CONF_HALO, _CONF_HALO), :]
        y = _dot(ycat_ref[...], wout_ref[...], _NN)
        _epi_residual_rms(y, (x_ref, gn_ref), (x1_ref, h2_ref), first)

    def const(arr):
        return pl.BlockSpec(arr.shape, lambda i: (0,) * arr.ndim, pipeline_mode=pl.Buffered(1))

    def rows_of(width):
        return pl.BlockSpec((t, width), lambda i: (i, 0))

    params = (lcw, lcb, wab, ba, bx, lam, ccw, ccb, lng, lnb)
    res, cres = _pcall(
        body, args=(xs, h1, w_in, w_out, gnorm, *params), grid=(nt,),
        in_specs=[rows_of(d), rows_of(d), const(w_in), const(w_out), const(gnorm)] + [const(p) for p in params],
        out_specs=[rows_of(nblk * n), rows_of(2 * c), rows_of(c), rows_of(c), rows_of(d), rows_of(d)],
        out_shape=[jax.ShapeDtypeStruct((s, nblk * n), _F32), jax.ShapeDtypeStruct((s, 2 * c), _MXU),
                   jax.ShapeDtypeStruct((s, c), _F32), jax.ShapeDtypeStruct((s, c), _F32),
                   jax.ShapeDtypeStruct((s, d), _F32), jax.ShapeDtypeStruct((s, d), _MXU)],
        scratch_shapes=[pltpu.VMEM((t + _LRU_HALO, c), _F32), pltpu.VMEM((t + _CONF_HALO, c), _F32),
                        pltpu.VMEM((8, c), _F32), pltpu.VMEM((7, t + _CONF_HALO, c), _F32),
                        pltpu.VMEM((_CONF_HALO, nblk * n), _F32)],
        sem=("arbitrary",), name=name, comm=comm)
    return res, cres


def _mixer_bwd(dx1b, w_out, h1, z, hs, cc, lcw, lcb, wab, ba, bx, lam, ccw, ccb, lng, lnb, name, comm=None):
    s = z.shape[0]
    d = h1.shape[1]
    c = _D_LRU
    t = _tile(s, _MIX_T)
    nt = s // t
    nblk = z.shape[1] // 256

    def body(dxb_ref, wout_ref, h1_ref, lx0_ref, lx0h_ref, gate_ref, ca_ref, cah_ref, cb_ref, cbh_ref,
             hs_ref, hsh_ref, cc_ref,
             lcw_ref, lcb_ref, wab_ref, ba_ref, bx_ref, lam_ref, ccw_ref, ccb_ref, lng_ref, lnb_ref,
             dz_ref, pin_ref, dlcw_ref, dlcb_ref, dwab_ref, dba_ref, dbx_ref, dlam_ref, dccw_ref, dccb_ref, dlng_ref,
             dlnb_ref,
             ext_ref, up_ref, cge_ref, dce_ref, xc_ref, dlxc_ref, dccc_ref, shifted_ref, dwin_ref):
        i = pl.program_id(0)
        first_tile = i == nt - 1
        last_tile = i == 0
        rows = lax.broadcasted_iota(jnp.int32, (t, c), 0)

        @pl.when(last_tile)
        def _():
            for ref in (dlcw_ref, dlcb_ref, dwab_ref, dba_ref, dbx_ref, dlam_ref, dccw_ref, dccb_ref, dlng_ref,
                        dlnb_ref, xc_ref, dlxc_ref, dccc_ref, dwin_ref):
                ref[...] = jnp.zeros_like(ref)

        dycat = _dot(dxb_ref[...], wout_ref[...], _NT)

        ext_ref[0:_LRU_HALO, :] = jnp.where(first_tile, 0.0, lx0h_ref[...])
        ext_ref[_LRU_HALO:_LRU_HALO + t, :] = lx0_ref[...]
        lx = _causal_conv(ext_ref, _LRU_HALO, lcw_ref, lcb_ref, _LRU_K, t)
        r, ig, sp, a, mult = _lru_gates(lx, wab_ref, ba_ref, bx_ref, lam_ref)
        h = hs_ref[...]
        gl, dgl = _gelu_and_grad(gate_ref[...])
        dyl = dycat[:, 0:c]
        dz_ref[:, c:2 * c] = (dyl * h * dgl).astype(dz_ref.dtype)
        dh = dyl * gl

        up_ref[0:t, :] = a
        up_ref[t:t + 8, :] = jnp.ones((8, c), _F32)
        a_up = up_ref[pl.ds(1, t), :]
        a_cum, g_loc = _scan_rev(a_up, dh, rows)
        gt = g_loc + a_cum * xc_ref[0:1, :]
        xc_ref[...] = (a * gt)[0:8, :]

        up_ref[0:8, :] = jnp.where(first_tile, 0.0, hsh_ref[...])
        up_ref[8:8 + t, :] = h
        hprev = up_ref[pl.ds(7, t), :]

        da = gt * hprev
        dmult = gt * ig * lx
        dig = gt * mult * lx
        dlx = gt * mult * ig
        dlog_a = da * a - dmult * a * a / mult
        dpre_r = dlog_a * (-_RG_C) * sp * r * (1.0 - r)
        dpre_i = dig * ig * (1.0 - ig)
        dlam_ref[...] += _colsum(dlog_a * r) * (_RG_C * _sigmoid(-lam_ref[...]))
        dba_ref[...] += _colsum(dpre_r)
        dbx_ref[...] += _colsum(dpre_i)
        dpre = jnp.concatenate([dpre_r, dpre_i], axis=1).astype(_MXU)
        dlx = dlx + _dot(dpre, wab_ref[...], _NT)
        dwab_ref[...] += _dot(lx, dpre, _TN)

        dlcb_ref[...] += _colsum(dlx)
        up_ref[0:t, :] = dlx
        up_ref[t:t + 8, :] = dlxc_ref[...]
        dlxc_ref[...] = dlx[0:8, :]
        acc = lcw_ref[0:1, :] * up_ref[pl.ds(_LRU_K - 1, t), :]
        for k in range(1, _LRU_K):
            acc = acc + lcw_ref[k:k + 1, :] * up_ref[pl.ds(_LRU_K - 1 - k, t), :]
        dz_ref[:, 0:c] = acc.astype(dz_ref.dtype)
        for k in range(_LRU_K):
            dlcw_ref[k:k + 1, :] += _colsum(dlx * ext_ref[pl.ds(_LRU_HALO - (_LRU_K - 1) + k, t), :])

        sig_b = _sigmoid(cb_ref[...])
        ca = ca_ref[...]
        cge_ref[0:_CONF_HALO, :] = jnp.where(first_tile, 0.0, cah_ref[...] * _sigmoid(cbh_ref[...]))
        cge_ref[_CONF_HALO:_CONF_HALO + t, :] = ca * sig_b
        ccv = cc_ref[...]
        xcen = ccv - jnp.mean(ccv, axis=-1, keepdims=True)
        rstd = lax.rsqrt(jnp.mean(xcen * xcen, axis=-1, keepdims=True) + _EPS)
        xn = xcen * rstd
        ln = xn * lng_ref[...] + lnb_ref[...]
        sg = _sigmoid(ln)
        dln = dycat[:, c:2 * c] * (sg * (1.0 + ln * (1.0 - sg)))
        dlng_ref[...] += _colsum(dln * xn)
        dlnb_ref[...] += _colsum(dln)
        dxn = dln * lng_ref[...]
        dcc = rstd * (dxn - jnp.mean(dxn, axis=-1, keepdims=True)
                      - xn * jnp.mean(dxn * xn, axis=-1, keepdims=True))
        dccb_ref[...] += _colsum(dcc)
        win = _Windows(cge_ref, shifted_ref, t)
        for k in range(_CONF_K):
            dccw_ref[k:k + 1, :] += _colsum(dcc * win(_CONF_HALO - (_CONF_K - 1) + k))
        dce_ref[0:t, :] = dcc
        dce_ref[t:t + _CONF_HALO, :] = dccc_ref[...]
        dccc_ref[...] = dcc[0:_CONF_HALO, :]
        win = _Windows(dce_ref, shifted_ref, t)
        dcg = ccw_ref[0:1, :] * win(_CONF_K - 1)
        for k in range(1, _CONF_K):
            dcg = dcg + ccw_ref[k:k + 1, :] * win(_CONF_K - 1 - k)
        dz_ref[:, 2 * c:3 * c] = (dcg * sig_b).astype(dz_ref.dtype)
        dz_ref[:, 3 * c:4 * c] = (dcg * ca * sig_b * (1.0 - sig_b)).astype(dz_ref.dtype)

        dwin_ref[...] += _dot(h1_ref[...], dz_ref[...], _TN)

        @pl.when(first_tile)
        def _():
            for j in range(nblk):
                pin_ref[j] = dwin_ref[:, j * 256:(j + 1) * 256].astype(pin_ref.dtype)

    def col(j):
        return pl.BlockSpec((t, c), lambda i: (nt - 1 - i, j))

    def halo(j, rows_):
        per = t // rows_
        return pl.BlockSpec((rows_, c), lambda i: (jnp.maximum((nt - 1 - i) * per - 1, 0), j))

    def full(shape):
        return pl.BlockSpec(shape, lambda i: (0,) * len(shape))

    params = (lcw, lcb, wab, ba, bx, lam, ccw, ccb, lng, lnb)
    small = [(_LRU_K, c), (1, c), (c, 2 * c), (1, c), (1, c), (1, c), (_CONF_K, c), (1, c), (1, c), (1, c)]
    wide = pl.BlockSpec((t, d), lambda i: (nt - 1 - i, 0))
    pin_shape = (nblk, d, 256)
    return _pcall(
        body, args=(dx1b, w_out, h1, z, z, z, z, z, z, z, hs, hs, cc, *params), grid=(nt,),
        in_specs=[wide, pl.BlockSpec(w_out.shape, lambda i: (0, 0), pipeline_mode=pl.Buffered(1)), wide,
                  col(0), halo(0, _LRU_HALO), col(1), col(2), halo(2, _CONF_HALO), col(3), halo(3, _CONF_HALO),
                  col(0), halo(0, 8), col(0)]
        + [full(p.shape) for p in params],
        out_specs=[pl.BlockSpec((t, 4 * c), lambda i: (nt - 1 - i, 0)), full(pin_shape)] + [full(sh) for sh in small],
        out_shape=[jax.ShapeDtypeStruct((s, 4 * c), _MXU), jax.ShapeDtypeStruct(pin_shape, _XFER)]
        + [jax.ShapeDtypeStruct(sh, _F32) for sh in small],
        scratch_shapes=[pltpu.VMEM((t + _LRU_HALO, c), _F32), pltpu.VMEM((t + 8, c), _F32),
                        pltpu.VMEM((t + _CONF_HALO, c), _F32), pltpu.VMEM((t + _CONF_HALO, c), _F32),
                        pltpu.VMEM((8, c), _F32), pltpu.VMEM((8, c), _F32), pltpu.VMEM((_CONF_HALO, c), _F32),
                        pltpu.VMEM((7, t + _CONF_HALO, c), _F32), pltpu.VMEM((d, 4 * c), _F32)],
        sem=("arbitrary",), name=name, comm=comm, vmem=_VMEM_LIMIT_FUSED)


_ATT_T = 512


def _attn_probs(qh, kh, scale):
    sc = _dot(qh, kh, _NT) * scale
    e = jnp.exp(sc - jnp.max(sc, axis=-1, keepdims=True))
    return e / jnp.sum(e, axis=-1, keepdims=True)


def _const_spec(arr):
    return pl.BlockSpec(arr.shape, lambda i: (0,) * arr.ndim, pipeline_mode=pl.Buffered(1))


def _attn_fwd(h2, x1, w_q, kv, w_o, gnorm, name):
    s, d = h2.shape
    nm = kv.shape[0]
    hd = d // _XA_HEADS
    t = _tile(s, _ATT_T)
    scale = hd ** -0.5

    def body(h_ref, x1_ref, wq_ref, k_ref, v_ref, wo_ref, g_ref, q_ref, o_ref, x2_ref, h3_ref):
        q_ref[...] = _dot(h_ref[...], wq_ref[...], _NN).astype(q_ref.dtype)
        for hh in range(_XA_HEADS):
            sl = slice(hh * hd, (hh + 1) * hd)
            p = _attn_probs(q_ref[:, sl], k_ref[:, sl], scale)
            o_ref[:, sl] = _dot(p, v_ref[:, sl], _NN).astype(o_ref.dtype)
        y = _dot(o_ref[...], wo_ref[...], _NN)
        _epi_residual_rms(y, (x1_ref, g_ref), (x2_ref, h3_ref), None)

    row = pl.BlockSpec((t, d), lambda i: (i, 0))
    half = pl.BlockSpec((nm, d), lambda i: (0, 0), pipeline_mode=pl.Buffered(1))
    half2 = pl.BlockSpec((nm, d), lambda i: (0, 1), pipeline_mode=pl.Buffered(1))
    return pl.pallas_call(
        body, grid=(s // t,),
        in_specs=[row, row, _const_spec(w_q), half, half2, _const_spec(w_o), _const_spec(gnorm)],
        out_specs=[row, row, row, row],
        out_shape=[jax.ShapeDtypeStruct((s, d), _MXU), jax.ShapeDtypeStruct((s, d), _MXU),
                   jax.ShapeDtypeStruct((s, d), _F32), jax.ShapeDtypeStruct((s, d), _MXU)],
        compiler_params=_cparams("parallel"), name=name,
    )(h2, x1, w_q, kv, kv, w_o, gnorm)


def _attn_bwd(dx2b, dx2, q, o, h2, x1, kv, w_o, w_q, gnorm, name, comm=None):
    s, d = q.shape
    nm = kv.shape[0]
    hd = d // _XA_HEADS
    t = _tile(s, _ATT_T)
    nt = s // t
    scale = hd ** -0.5

    def body(dxb_ref, dx2_ref, q_ref, o_ref, h_ref, x1_ref, k_ref, v_ref, wo_ref, wq_ref, g_ref,
             dx1_ref, dx1b_ref, dgn_ref, dk_ref, dv_ref, pwo_ref, pwq_ref, dq_ref, awo_ref, awq_ref):
        i = pl.program_id(0)
        first = i == 0

        @pl.when(first)
        def _():
            for ref in (dk_ref, dv_ref, awo_ref, awq_ref):
                ref[...] = jnp.zeros_like(ref)

        dxb = dxb_ref[...]
        do = _dot(dxb, wo_ref[...], _NT).astype(_MXU)
        awo_ref[...] += _dot(o_ref[...], dxb, _TN)
        for hh in range(_XA_HEADS):
            sl = slice(hh * hd, (hh + 1) * hd)
            qh = q_ref[:, sl]
            kh = k_ref[:, sl]
            doh = do[:, sl]
            p = _attn_probs(qh, kh, scale)
            dp = _dot(doh, v_ref[:, sl], _NT)
            dv_ref[:, sl] += _dot(p, doh, _TN)
            ds = (p * (dp - jnp.sum(dp * p, axis=-1, keepdims=True)) * scale).astype(_MXU)
            dq_ref[:, sl] = _dot(ds, kh, _NN).astype(dq_ref.dtype)
            dk_ref[:, sl] += _dot(ds, qh, _TN)
        dq = dq_ref[...]
        awq_ref[...] += _dot(h_ref[...], dq, _TN)
        dh = _dot(dq, wq_ref[...], _NT)
        _epi_rms_bwd(dh, (x1_ref, g_ref, dx2_ref), (dx1_ref, dx1b_ref, dgn_ref), first)

        @pl.when(i == nt - 1)
        def _():
            pwo_ref[...] = awo_ref[...].astype(pwo_ref.dtype)
            pwq_ref[...] = awq_ref[...].astype(pwq_ref.dtype)

    row = pl.BlockSpec((t, d), lambda i: (i, 0))
    vec = pl.BlockSpec((1, d), lambda i: (0, 0))
    mem_blk = pl.BlockSpec((nm, d), lambda i: (0, 0))
    sq = pl.BlockSpec((d, d), lambda i: (0, 0))
    half = pl.BlockSpec((nm, d), lambda i: (0, 0), pipeline_mode=pl.Buffered(1))
    half2 = pl.BlockSpec((nm, d), lambda i: (0, 1), pipeline_mode=pl.Buffered(1))
    return _pcall(
        body, args=(dx2b, dx2, q, o, h2, x1, kv, kv, w_o, w_q, gnorm), grid=(nt,),
        in_specs=[row, row, row, row, row, row, half, half2, _const_spec(w_o), _const_spec(w_q), _const_spec(gnorm)],
        out_specs=[row, row, vec, mem_blk, mem_blk, sq, sq],
        out_shape=[jax.ShapeDtypeStruct((s, d), _F32), jax.ShapeDtypeStruct((s, d), _MXU),
                   jax.ShapeDtypeStruct((1, d), _F32), jax.ShapeDtypeStruct((nm, d), _F32),
                   jax.ShapeDtypeStruct((nm, d), _F32), jax.ShapeDtypeStruct((d, d), _XFER),
                   jax.ShapeDtypeStruct((d, d), _XFER)],
        scratch_shapes=[pltpu.VMEM((t, d), _MXU), pltpu.VMEM((d, d), _F32), pltpu.VMEM((d, d), _F32)],
        sem=("arbitrary",), name=name, comm=comm, vmem=_VMEM_LIMIT_FUSED)


_FFN_K = 3
_FFN_FUSED_T = 256
_VMEM_LIMIT_FUSED = 58 * 1024 * 1024


def _ffn_fused_fwd(h3, w_up, w_down, fcw, fcb, x2, target, gfin, name):
    s, d = h3.shape
    nblk, _, n = w_up.shape
    half = nblk // 2
    f = half * n
    t = _tile(s, _FFN_FUSED_T)

    def body(h_ref, wup_ref, wdown_ref, w_ref, b_ref, x2_ref, t_ref, g_ref,
             g0_ref, act_ref, gl_ref, udgl_ref, dx_ref, dxb_ref, l_ref, dg_ref, ext0_ref, ext1_ref, halo_ref):
        i = pl.program_id(0)
        first = i == 0
        h = h_ref[...]
        total = None
        ahead = (_dot(h, wup_ref[0], _NN), _dot(h, wup_ref[half], _NN))
        for j in range(half):
            cs = slice(j * n, (j + 1) * n)
            ext_ref = ext0_ref if j % 2 == 0 else ext1_ref
            g0, u = ahead
            if j + 1 < half:
                ahead = (_dot(h, wup_ref[j + 1], _NN), _dot(h, wup_ref[half + j + 1], _NN))
            if j > 0:
                prev = slice((j - 1) * n, j * n)
                p = _dot(act_ref[:, prev], wdown_ref[prev, :], _NN)
                total = p if total is None else total + p
            g0_ref[:, cs] = g0.astype(g0_ref.dtype)
            ext_ref[0:8, :] = jnp.where(first, 0.0, halo_ref[:, cs])
            ext_ref[8:8 + t, :] = g0
            halo_ref[:, cs] = g0[t - 8:t, :]
            g = _causal_conv(ext_ref, 8, w_ref.at[:, cs], b_ref.at[:, cs], _FFN_K, t)
            gl, dgl = _gelu_and_grad(g)
            gl_ref[:, cs] = gl.astype(gl_ref.dtype)
            udgl_ref[:, cs] = (u * dgl).astype(udgl_ref.dtype)
            act_ref[:, cs] = (gl * u).astype(act_ref.dtype)
        last = slice((half - 1) * n, half * n)
        total = total + _dot(act_ref[:, last], wdown_ref[last, :], _NN)
        _epi_final(total, (x2_ref, t_ref, g_ref), (dx_ref, dxb_ref, l_ref, dg_ref), first)

    def const(shape):
        return pl.BlockSpec(shape, lambda i: (0,) * len(shape), pipeline_mode=pl.Buffered(1))

    row = pl.BlockSpec((t, d), lambda i: (i, 0))
    vec = pl.BlockSpec((1, d), lambda i: (0, 0))
    return pl.pallas_call(
        body, grid=(s // t,),
        in_specs=[row, const(w_up.shape), const(w_down.shape), const(fcw.shape), const(fcb.shape), row, row, vec],
        out_specs=[pl.BlockSpec((t, f), lambda i: (i, 0))] * 4 + [row, row, vec, vec],
        out_shape=[jax.ShapeDtypeStruct((s, f), _MXU)] * 4
        + [jax.ShapeDtypeStruct((s, d), _F32), jax.ShapeDtypeStruct((s, d), _MXU),
                   jax.ShapeDtypeStruct((1, d), _F32), jax.ShapeDtypeStruct((1, d), _F32)],
        scratch_shapes=[pltpu.VMEM((t + 8, n), _F32), pltpu.VMEM((t + 8, n), _F32), pltpu.VMEM((8, f), _F32)],
        compiler_params=_cparams("arbitrary", vmem=_VMEM_LIMIT_FUSED), name=name,
    )(h3, w_up, w_down, fcw, fcb, x2, target, gfin)


def _ffn_fused_bwd(dx3b, g0, gl, udgl, w_down, w_up, fcw, x2, gnorm, dx3, name, comm=None):
    s, d = x2.shape
    nblk, _, n = w_up.shape
    half = nblk // 2
    f = half * n
    t = _tile(s, _FFN_FUSED_T)
    nt = s // t
    hrows = 16

    def body(dxb_ref, g0_ref, g0h_ref, gl_ref, udgl_ref, wdown_ref, wup_ref, w_ref, x2_ref, g_ref, dx3_ref,
             dgu_ref, dx2_ref, dx2b_ref, dgn_ref, dw_ref, db_ref, ext0_ref, ext1_ref, up0_ref, up1_ref, car_ref):
        i = pl.program_id(0)
        first_tile = i == nt - 1
        last_tile = i == 0

        @pl.when(last_tile)
        def _():
            dw_ref[...] = jnp.zeros_like(dw_ref)
            db_ref[...] = jnp.zeros_like(db_ref)
            car_ref[...] = jnp.zeros_like(car_ref)

        dxb = dxb_ref[...]
        total = None
        for j in range(half):
            cs = slice(j * n, (j + 1) * n)
            us = slice(f + j * n, f + (j + 1) * n)
            ext_ref = ext0_ref if j % 2 == 0 else ext1_ref
            up_ref = up0_ref if j % 2 == 0 else up1_ref
            dact = _dot(dxb, wdown_ref[cs, :], _NT)
            ext_ref[0:8, :] = jnp.where(first_tile, 0.0, g0h_ref[:, cs].astype(_F32)[hrows - 8:hrows])
            ext_ref[8:8 + t, :] = g0_ref[:, cs].astype(_F32)
            du = (dact * gl_ref[:, cs].astype(_F32)).astype(dgu_ref.dtype)
            dgu_ref[:, us] = du
            dg = dact * udgl_ref[:, cs].astype(_F32)
            db_ref[:, cs] += _colsum(dg)
            for k in range(_FFN_K):
                dw_ref[k:k + 1, cs] += _colsum(dg * ext_ref[pl.ds(8 - (_FFN_K - 1) + k, t), :])
            up_ref[0:t, :] = dg
            up_ref[t:t + 8, :] = car_ref[:, cs]
            car_ref[:, cs] = dg[0:8, :]
            dg0 = w_ref[0:1, cs] * up_ref[pl.ds(_FFN_K - 1, t), :]
            for k in range(1, _FFN_K):
                dg0 = dg0 + w_ref[k:k + 1, cs] * up_ref[pl.ds(_FFN_K - 1 - k, t), :]
            dg0 = dg0.astype(dgu_ref.dtype)
            dgu_ref[:, cs] = dg0
            p = _dot(dg0, wup_ref[j], _NT) + _dot(du, wup_ref[half + j], _NT)
            total = p if total is None else total + p
        _epi_rms_bwd(total, (x2_ref, g_ref, dx3_ref), (dx2_ref, dx2b_ref, dgn_ref), last_tile)

    def const(shape):
        return pl.BlockSpec(shape, lambda i: (0,) * len(shape), pipeline_mode=pl.Buffered(1))

    row = pl.BlockSpec((t, d), lambda i: (nt - 1 - i, 0))
    vec = pl.BlockSpec((1, d), lambda i: (0, 0))
    per = t // hrows
    wide = pl.BlockSpec((t, f), lambda i: (nt - 1 - i, 0))
    return _pcall(
        body, args=(dx3b, g0, g0, gl, udgl, w_down, w_up, fcw, x2, gnorm, dx3), grid=(nt,),
        in_specs=[row, wide, pl.BlockSpec((hrows, f), lambda i: (jnp.maximum((nt - 1 - i) * per - 1, 0), 0)),
                  wide, wide, const(w_down.shape), const(w_up.shape), const(fcw.shape), row, vec, row],
        out_specs=[pl.BlockSpec((t, 2 * f), lambda i: (nt - 1 - i, 0)), row, row, vec,
                   pl.BlockSpec((_FFN_K, f), lambda i: (0, 0)), pl.BlockSpec((1, f), lambda i: (0, 0))],
        out_shape=[jax.ShapeDtypeStruct((s, 2 * f), _MXU), jax.ShapeDtypeStruct((s, d), _F32),
                   jax.ShapeDtypeStruct((s, d), _MXU), jax.ShapeDtypeStruct((1, d), _F32),
                   jax.ShapeDtypeStruct((_FFN_K, f), _F32), jax.ShapeDtypeStruct((1, f), _F32)],
        scratch_shapes=[pltpu.VMEM((t + 8, n), _F32), pltpu.VMEM((t + 8, n), _F32),
                        pltpu.VMEM((t + 8, n), _F32), pltpu.VMEM((t + 8, n), _F32), pltpu.VMEM((8, f), _F32)],
        sem=("arbitrary",), name=name, comm=comm, vmem=_VMEM_LIMIT_FUSED)


def _mesh_pos():
    return lax.axis_index("x"), lax.axis_index("y"), lax.axis_index("c")


def _flip(v, bit):
    return 1 - v if bit else v


def _sem_scratch(n):
    return [pltpu.SemaphoreType.DMA((7 * n,)), pltpu.SemaphoreType.DMA((7 * n,)), pltpu.SemaphoreType.DMA((n,))]


class _Gather:
    def __init__(self, xs):
        self.ins = list(xs)
        self.outs = [jax.ShapeDtypeStruct((_NDEV,) + v.shape, v.dtype) for v in xs]
        self.scratch = _sem_scratch(len(xs))

    def _plan(self, x_refs, out_refs, sems):
        send_sems, recv_sems, local_sems = sems
        x, y, c = _mesh_pos()
        me, sibling = (x, y, c), (x, y, 1 - c)
        chips = [(1 - x, y), (x, 1 - y), (1 - x, 1 - y)]

        def copy(a, k, block, to, src=None):
            slot = out_refs[a].at[4 * block[0] + 2 * block[1] + block[2]]
            return pltpu.make_async_remote_copy(
                src_ref=slot if src is None else src, dst_ref=slot,
                send_sem=send_sems.at[a * 7 + k], recv_sem=recv_sems.at[a * 7 + k],
                device_id=to, device_id_type=_MESH_ID)

        def own(a):
            return pltpu.make_async_copy(x_refs[a], out_refs[a].at[4 * x + 2 * y + c], local_sems.at[a])

        def first(a):
            return [copy(a, 0, me, sibling, src=x_refs[a])] + [
                copy(a, 1 + j, me, (*chip, c), src=x_refs[a]) for j, chip in enumerate(chips)]

        return me, sibling, chips, c, copy, own, first

    def start(self, x_refs, out_refs, sems):
        _, _, _, _, _, own, first = self._plan(x_refs, out_refs, sems)
        for a in range(len(self.ins)):
            own(a).start()
            for cp in first(a):
                cp.start()

    def finish(self, x_refs, out_refs, sems):
        me, sibling, chips, c, copy, own, first = self._plan(x_refs, out_refs, sems)
        n = len(self.ins)
        passed = []
        for a in range(n):
            for j, chip in enumerate(chips):
                copy(a, 1 + j, (*chip, c), me).wait_recv()
                fwd = copy(a, 4 + j, (*chip, c), sibling)
                fwd.start()
                passed.append(fwd)
        for a in range(n):
            copy(a, 0, sibling, me).wait_recv()
            for j, chip in enumerate(chips):
                copy(a, 4 + j, (*chip, 1 - c), me).wait_recv()
        for a in range(n):
            for cp in first(a):
                cp.wait_send()
        for cp in passed:
            cp.wait_send()
        for a in range(n):
            own(a).wait()


class _Exchange:
    def __init__(self, gs):
        self.ins = list(gs)
        self.outs = [jax.ShapeDtypeStruct(v.shape, v.dtype) for v in gs]
        self.scratch = _sem_scratch(len(gs))

    def _plan(self, g_refs, r_refs, sems):
        send_sems, recv_sems, local_sems = sems
        x, y, c = _mesh_pos()
        me_idx = 4 * x + 2 * y + c
        n = len(self.ins)

        def copy(a, k):
            peer = (_flip(x, k & 4), _flip(y, k & 2), _flip(c, k & 1))
            peer_idx = 4 * peer[0] + 2 * peer[1] + peer[2]
            return pltpu.make_async_remote_copy(
                src_ref=g_refs[a].at[peer_idx], dst_ref=r_refs[a].at[me_idx],
                send_sem=send_sems.at[a * 7 + k - 1], recv_sem=recv_sems.at[a * 7 + k - 1],
                device_id=peer, device_id_type=_MESH_ID)

        copies = [copy(a, k) for a in range(n) for k in range(1, _NDEV)]
        mine = [pltpu.make_async_copy(g_refs[a].at[me_idx], r_refs[a].at[me_idx], local_sems.at[a])
                for a in range(n)]
        return copies, mine

    def start(self, g_refs, r_refs, sems):
        copies, mine = self._plan(g_refs, r_refs, sems)
        for cp in copies + mine:
            cp.start()

    def finish(self, g_refs, r_refs, sems):
        copies, mine = self._plan(g_refs, r_refs, sems)
        for cp in copies:
            cp.wait_recv()
        for cp in copies:
            cp.wait_send()
        for cp in mine:
            cp.wait()


class _Both:
    def __init__(self, first, second):
        self.parts = (first, second)
        self.ins = first.ins + second.ins
        self.outs = first.outs + second.outs
        self.scratch = first.scratch + second.scratch

    def _split(self, ins, outs, sems):
        a, b = self.parts
        na, nb = len(a.ins), len(a.scratch)
        return (a, ins[:na], outs[:na], sems[:nb]), (b, ins[na:], outs[na:], sems[nb:])

    def start(self, ins, outs, sems):
        for part, i, o, s in self._split(ins, outs, sems):
            part.start(i, o, s)

    def finish(self, ins, outs, sems):
        for part, i, o, s in self._split(ins, outs, sems):
            part.finish(i, o, s)


def _comm_call(comm, name):
    def body(*refs):
        n_i, n_o = len(comm.ins), len(comm.outs)
        ins, outs, sems = refs[:n_i], refs[n_i:n_i + n_o], refs[n_i + n_o:]
        comm.start(ins, outs, sems)
        comm.finish(ins, outs, sems)

    return pl.pallas_call(
        body, out_shape=list(comm.outs), in_specs=[_ANY] * len(comm.ins), out_specs=[_ANY] * len(comm.outs),
        scratch_shapes=list(comm.scratch), name=name)(*comm.ins)


def _adamw_math(w, g, m, v):
    m = _ADAM_B1 * m + (1.0 - _ADAM_B1) * g
    v = _ADAM_B2 * v + (1.0 - _ADAM_B2) * (g * g)
    m_hat = m / (1.0 - _ADAM_B1 ** _ADAM_STEP)
    v_hat = v / (1.0 - _ADAM_B2 ** _ADAM_STEP)
    delta = -_ADAM_LR * (m_hat / (jnp.sqrt(v_hat) + _ADAM_EPS) + _ADAM_WD * w)
    return delta, m, v


def _sum_adamw(parts, w, m, v, name):
    r, c = w.shape
    tr = _tile(r, 128)

    def body(p_ref, w_ref, m_ref, v_ref, g_ref, d_ref, nm_ref, nv_ref):
        g = p_ref[0].astype(_F32)
        for j in range(1, _NDEV):
            g = g + p_ref[j].astype(_F32)
        delta, nm, nv = _adamw_math(w_ref[...], g, m_ref[...], v_ref[...])
        g_ref[...] = g
        d_ref[...] = delta
        nm_ref[...] = nm
        nv_ref[...] = nv

    blk = pl.BlockSpec((tr, c), lambda i: (i, 0))
    return pl.pallas_call(
        body, grid=(r // tr,),
        in_specs=[pl.BlockSpec((_NDEV, tr, c), lambda i: (0, i, 0)), blk, blk, blk],
        out_specs=[blk] * 4, out_shape=[jax.ShapeDtypeStruct((r, c), _F32)] * 4,
        compiler_params=_cparams("parallel"), name=name,
    )(parts, w, m, v)


def _sum8(parts, name):
    _, r, c = parts.shape

    def body(p_ref, o_ref):
        g = p_ref[0]
        for j in range(1, _NDEV):
            g = g + p_ref[j]
        o_ref[...] = g

    return pl.pallas_call(
        body, grid=(1,), in_specs=[pl.BlockSpec((_NDEV, r, c), lambda i: (0, 0, 0))],
        out_specs=pl.BlockSpec((r, c), lambda i: (0, 0)), out_shape=jax.ShapeDtypeStruct((r, c), _F32),
        compiler_params=_cparams("arbitrary"), name=name,
    )(parts)


def _adamw_many(gs, ws, ms, vs, name):
    n = len(ws)

    def body(*refs):
        g_refs, w_refs, m_refs, v_refs = (refs[k * n:(k + 1) * n] for k in range(4))
        d_refs, nm_refs, nv_refs = (refs[(4 + k) * n:(5 + k) * n] for k in range(3))
        for k in range(n):
            delta, nm, nv = _adamw_math(w_refs[k][...], g_refs[k][...], m_refs[k][...], v_refs[k][...])
            d_refs[k][...] = delta
            nm_refs[k][...] = nm
            nv_refs[k][...] = nv

    def whole(arr):
        return pl.BlockSpec(arr.shape, lambda i, nd=arr.ndim: (0,) * nd)

    specs = [whole(w) for w in ws]
    res = pl.pallas_call(
        body, grid=(1,), in_specs=specs * 4, out_specs=specs * 3,
        out_shape=[jax.ShapeDtypeStruct(w.shape, _F32) for w in ws] * 3,
        compiler_params=_cparams("arbitrary"), name=name,
    )(*gs, *ws, *ms, *vs)
    return res[:n], res[n:2 * n], res[2 * n:]


def _pack(arrs):
    flat = jnp.concatenate([a.reshape(-1).astype(_F32) for a in arrs])
    pad = (-flat.shape[0]) % 1024
    return jnp.pad(flat, (0, pad)).reshape(-1, 128)


def _unpack(flat2d, shapes):
    flat = flat2d.reshape(-1)
    out, off = [], 0
    for sh in shapes:
        size = 1
        for dim in sh:
            size *= dim
        out.append(flat[off:off + size].reshape(sh))
        off += size
    return out


def _block_diag(w):
    h, hd, _ = w.shape
    eye = jnp.eye(h, dtype=w.dtype)
    return (eye[:, None, :, None] * w[:, :, None, :]).reshape(h * hd, h * hd)


def _diag_blocks(full, h):
    hd = full.shape[0] // h
    return jnp.stack([full[i * hd:(i + 1) * hd, i * hd:(i + 1) * hd] for i in range(h)])


def kernel(x, mem, mix_norm_g, w_in, lru_conv_w, lru_conv_b, lru_w_a, lru_b_a, lru_w_x, lru_b_x, lru_lambda, conf_conv_w, conf_conv_b, conf_ln_g, conf_ln_b, w_out, xa_norm_g, mem_norm_g, w_q, w_kv, w_o, ffn_norm_g, w_up, ffn_conv_w, ffn_conv_b, w_down, final_norm_g, loss_target, m_mix_norm_g, m_w_in, m_lru_conv_w, m_lru_conv_b, m_lru_w_a, m_lru_b_a, m_lru_w_x, m_lru_b_x, m_lru_lambda, m_conf_conv_w, m_conf_conv_b, m_conf_ln_g, m_conf_ln_b, m_w_out, m_xa_norm_g, m_mem_norm_g, m_w_q, m_w_kv, m_w_o, m_ffn_norm_g, m_w_up, m_ffn_conv_w, m_ffn_conv_b, m_w_down, m_final_norm_g, v_mix_norm_g, v_w_in, v_lru_conv_w, v_lru_conv_b, v_lru_w_a, v_lru_b_a, v_lru_w_x, v_lru_b_x, v_lru_lambda, v_conf_conv_w, v_conf_conv_b, v_conf_ln_g, v_conf_ln_b, v_w_out, v_xa_norm_g, v_mem_norm_g, v_w_q, v_w_kv, v_w_o, v_ffn_norm_g, v_w_up, v_ffn_conv_w, v_ffn_conv_b, v_w_down, v_final_norm_g):
    names = ["mix_norm_g", "w_in", "lru_conv_w", "lru_conv_b", "lru_w_a", "lru_b_a", "lru_w_x", "lru_b_x",
             "lru_lambda", "conf_conv_w", "conf_conv_b", "conf_ln_g", "conf_ln_b", "w_out", "xa_norm_g",
             "mem_norm_g", "w_q", "w_kv", "w_o", "ffn_norm_g", "w_up", "ffn_conv_w", "ffn_conv_b", "w_down",
             "final_norm_g"]
    loc = locals()
    W = {n: loc[n] for n in names}
    M = {n: loc["m_" + n] for n in names}
    V = {n: loc["v_" + n] for n in names}
    big = ["w_in", "w_out", "w_q", "w_kv", "w_o", "w_up", "w_down"]
    conv_sharded = ["lru_conv_w", "conf_conv_w", "ffn_conv_w"]

    xs = x[0]
    mems = mem[0]
    tgt = loss_target[0]
    me = 4 * lax.axis_index("x") + 2 * lax.axis_index("y") + lax.axis_index("c")

    conv_shapes = [W[n].shape[1:] for n in conv_sharded]
    conv_pack = _pack([W[n][0] for n in conv_sharded])
    shard = {n: W[n][0].astype(_XFER) for n in big}
    h1, (g_in, g_out, g_conv) = _rms_fwd(
        xs, mix_norm_g, "rms1_fwd", comm=_Gather([shard["w_in"], shard["w_out"], conv_pack]))
    convs = [[] for _ in conv_sharded]
    for j in range(_NDEV):
        for idx, part in enumerate(_unpack(g_conv[j], conv_shapes)):
            convs[idx].append(part)
    lcw, ccw, fcw = [jnp.concatenate(parts, axis=-1) for parts in convs]

    wab = jnp.concatenate([_block_diag(lru_w_a[0]), _block_diag(lru_w_x[0])], axis=1).astype(_MXU)
    mixer_params = (lcw, lru_conv_b, wab, lru_b_a, lru_b_x, lru_lambda, ccw, conf_conv_b, conf_ln_g, conf_ln_b)

    w_out_f = g_out.reshape(-1, g_out.shape[-1])
    (z, ycat, hs, cc, x1, h2), (g_q, g_kv, g_o, g_up, g_down) = _mixer_fwd(
        xs, h1, g_in, w_out_f, xa_norm_g, *mixer_params, "mixer_fwd",
        comm=_Gather([shard[n] for n in ("w_q", "w_kv", "w_o", "w_up", "w_down")]))
    w_q_f = g_q.reshape(-1, g_q.shape[-1])
    w_o_f = g_o.reshape(-1, g_o.shape[-1])
    w_down_f = g_down.reshape(-1, g_down.shape[-1])
    row32, row16, vec32 = (_F32, "row"), (_MXU, "row"), (_F32, "vec")
    mn = _rms_fwd(mems, mem_norm_g, "rmsm_fwd")
    kv = _mm_nn_stacked(mn, g_kv, _MXU, "mm_kv_fwd")
    q, o, x2, h3 = _attn_fwd(h2, x1, w_q_f, kv, w_o_f, ffn_norm_g, "attn_fwd")

    gfin = final_norm_g.reshape(1, -1)
    g0, act, gelu_g, u_dgelu, dx3, dx3b, lvec, dg_final = _ffn_fused_fwd(
        h3, g_up, w_down_f, fcw, ffn_conv_b, x2, tgt, gfin, "ffn_fwd")

    def rows8(p):
        return p.reshape(_NDEV, p.shape[0] // _NDEV, p.shape[1])

    p_down = _mm_tn_nat(act, dx3b, _XFER, "mm_down_wgrad", ts=2048)
    (dgu, dx2, dx2b, dg_ffn, dfcw, dfcb), _ = _ffn_fused_bwd(
        dx3b, g0, gelu_g, u_dgelu, w_down_f, g_up, fcw, x2, ffn_norm_g, dx3, "ffn_bwd")
    p_up = _mm_tn_stacked(h3, dgu, _NDEV, _XFER, "mm_up_wgrad", slabs=1, ts=4096)

    (dx1, dx1b, dg_xa, dk, dv, p_o, p_q), _ = _attn_bwd(
        dx2b, dx2, q, o, h2, x1, kv, w_o_f, w_q_f, xa_norm_g, "attn_bwd")
    dkv = jnp.concatenate([dk, dv], axis=1).astype(_MXU)
    dmn = _mm_nt_stacked(dkv, g_kv, "mm_kv_dgrad", outs=[row32], slabs=_NDEV)
    p_kv = _mm_tn_stacked(mn, dkv, _NDEV, _XFER, "mm_kv_wgrad", slabs=_NDEV)
    _, _, dg_mem = _rms_bwd(dmn, mems, mem_norm_g, None, "rmsm_bwd")

    p_out = _mm_tn_nat(ycat, dx1b, _XFER, "mm_out_wgrad", ts=2048)
    ((dz, p_in, dlcw, dlcb, dwab, dba, dbx, dlam, dccw, dccb, dlng, dlnb),
     (r_down, r_up, r_o, r_q, r_kv, r_out)) = _mixer_bwd(
        dx1b, w_out_f, h1, z, hs, cc, *mixer_params, "mixer_bwd",
        comm=_Exchange([rows8(p_down), p_up, rows8(p_o), rows8(p_q), p_kv, rows8(p_out)]))

    c = _D_LRU
    heads = lru_w_a.shape[1]
    small_partial = {
        "lru_conv_w": dlcw, "lru_conv_b": dlcb,
        "lru_w_a": _diag_blocks(dwab[:, :c], heads), "lru_b_a": dba,
        "lru_w_x": _diag_blocks(dwab[:, c:], heads), "lru_b_x": dbx, "lru_lambda": dlam,
        "conf_conv_w": dccw, "conf_conv_b": dccb, "conf_ln_g": dlng, "conf_ln_b": dlnb,
        "xa_norm_g": dg_xa, "mem_norm_g": dg_mem, "ffn_norm_g": dg_ffn,
        "ffn_conv_w": dfcw, "ffn_conv_b": dfcb, "final_norm_g": dg_final,
    }
    early = list(small_partial)
    early_shapes = [small_partial[n].shape for n in early] + [lvec.shape]
    (grad_x, dg_mix), (r_in, early_all) = _mm_nt_stacked(
        dz, g_in, "mm_in_dgrad", epi=_epi_rms_bwd, extra=[(xs, "row"), (mix_norm_g, "vec"), (dx1, "row")],
        outs=[row32, vec32], tm=512, slabs=_NDEV,
        comm=_Both(_Exchange([p_in]), _Gather([_pack([small_partial[n] for n in early] + [lvec])])))
    (mix_all,) = _comm_call(_Gather([dg_mix]), "gather_mix_grad")
    small = early + ["mix_norm_g"]
    small_sum = _unpack(_sum8(early_all, "sum_small_grads"), early_shapes)
    loss = 0.5 * jnp.sum(small_sum.pop()) / xs.shape[1]
    small_sum.append(_sum8(mix_all.reshape(_NDEV, 8, -1), "sum_mix_grad").reshape(dg_mix.shape))
    received = {"w_in": r_in, "w_out": r_out, "w_q": r_q, "w_kv": r_kv, "w_o": r_o, "w_up": r_up,
                "w_down": r_down}

    grads, deltas, new_m, new_v = {}, {}, {}, {}
    for n, rec in ((n, received[n]) for n in big):
        shp = W[n].shape
        w2, m2, v2 = (t.reshape(shp[1:]) for t in (W[n], M[n], V[n]))
        outs = _sum_adamw(rec, w2, m2, v2, "adamw_" + n)
        grads[n], deltas[n], new_m[n], new_v[n] = (t.reshape(shp) for t in outs)

    small_g = []
    for n, g in zip(small, small_sum):
        if n in conv_sharded:
            width = W[n].shape[-1]
            g = lax.dynamic_slice_in_dim(g, me * width, width, axis=1)
        small_g.append(g.reshape(W[n].shape))

    def at_least_2d(a):
        return a.reshape(1, -1) if a.ndim == 1 else a

    sd, sm, sv = _adamw_many([at_least_2d(g) for g in small_g], [at_least_2d(W[n]) for n in small],
                             [at_least_2d(M[n]) for n in small], [at_least_2d(V[n]) for n in small], "adamw_small")
    for n, g, d_, m_, v_ in zip(small, small_g, sd, sm, sv):
        shp = W[n].shape
        grads[n], deltas[n], new_m[n], new_v[n] = g, d_.reshape(shp), m_.reshape(shp), v_.reshape(shp)

    return (loss, grad_x[None], *[grads[n] for n in names], *[deltas[n] for n in names],
            *[new_m[n] for n in names], *[new_v[n] for n in names])
```

```python
import functools

import jax
import jax.numpy as jnp
from jax import lax
from jax.experimental import pallas as pl
from jax.experimental.pallas import tpu as pltpu

_MXU = jnp.bfloat16
_XFER = jnp.bfloat16
_F32 = jnp.float32
_EPS = 1e-6
_NDEV = 8
_VMEM_LIMIT = 48 * 1024 * 1024

_D_LRU = 512
_XA_HEADS = 4
_RG_C = 8.0
_ADAM_LR, _ADAM_B1, _ADAM_B2, _ADAM_EPS, _ADAM_WD, _ADAM_STEP = 0.001, 0.9, 0.999, 1e-08, 0.01, 10

_MESH_ID = pl.DeviceIdType.MESH
_ANY = pl.BlockSpec(memory_space=pl.ANY)


def _cparams(*sem, vmem=_VMEM_LIMIT):
    return pltpu.CompilerParams(dimension_semantics=tuple(sem), vmem_limit_bytes=vmem)


def _pcall(body, *, args, grid, in_specs, out_specs, out_shape, sem, name, scratch_shapes=(), comm=None,
           vmem=_VMEM_LIMIT):
    outs_l = list(out_shape) if isinstance(out_shape, (list, tuple)) else [out_shape]
    ospecs_l = list(out_specs) if isinstance(out_specs, (list, tuple)) else [out_specs]
    n_in, n_out, n_scr = len(args), len(outs_l), len(scratch_shapes)
    if comm is None:
        res = pl.pallas_call(
            body, grid=grid, in_specs=list(in_specs), out_specs=ospecs_l, out_shape=outs_l,
            scratch_shapes=list(scratch_shapes), compiler_params=_cparams(*sem, vmem=vmem), name=name)(*args)
        return list(res), []
    n_ci, n_co = len(comm.ins), len(comm.outs)

    def wrapped(*refs):
        ins, cins = refs[:n_in], refs[n_in:n_in + n_ci]
        o = n_in + n_ci
        outs, couts = refs[o:o + n_out], refs[o + n_out:o + n_out + n_co]
        s = o + n_out + n_co
        scr, cscr = refs[s:s + n_scr], refs[s + n_scr:]
        first = pl.program_id(0) == 0
        last = pl.program_id(0) == grid[0] - 1
        for ax in range(1, len(grid)):
            first = jnp.logical_and(first, pl.program_id(ax) == 0)
            last = jnp.logical_and(last, pl.program_id(ax) == grid[ax] - 1)

        @pl.when(first)
        def _():
            comm.start(cins, couts, cscr)

        body(*ins, *outs, *scr)

        @pl.when(last)
        def _():
            comm.finish(cins, couts, cscr, outs)

    res = pl.pallas_call(
        wrapped, grid=grid, in_specs=list(in_specs) + [_ANY] * n_ci, out_specs=ospecs_l + [_ANY] * n_co,
        out_shape=outs_l + list(comm.outs), scratch_shapes=list(scratch_shapes) + list(comm.scratch),
        compiler_params=_cparams(*(("arbitrary",) * len(grid)), vmem=vmem), name=name)(*args, *comm.ins)
    return list(res[:n_out]), list(res[n_out:])


def _sigmoid(v):
    return 1.0 / (1.0 + jnp.exp(-v))


_GELU_C = 0.7978845608028654
_GELU_K = 0.044715


def _gelu(v):
    t = jnp.tanh(_GELU_C * (v + _GELU_K * v * v * v))
    return 0.5 * v * (1.0 + t)


def _gelu_and_grad(v):
    v2 = v * v
    s = 0.5 * jnp.tanh(v * (_GELU_C + (_GELU_C * _GELU_K) * v2)) + 0.5
    g = v * s
    dg = s + (g * (1.0 - s)) * ((2.0 * _GELU_C) + (6.0 * _GELU_C * _GELU_K) * v2)
    return g, dg


def _softplus(v):
    e = jnp.exp(-jnp.abs(v))
    log1p = jnp.where(e < 1e-2, e * (1.0 - e * (0.5 - e * (1.0 / 3.0))), jnp.log(1.0 + e))
    return jnp.maximum(v, 0.0) + log1p


def _neg_expm1(v):
    series = -v * (1.0 + v * (0.5 + v * ((1.0 / 6.0) + v * (1.0 / 24.0))))
    return jnp.where(v > -0.0625, series, 1.0 - jnp.exp(v))


def _dot(a, b, dims):
    return lax.dot_general(a.astype(_MXU), b.astype(_MXU), (dims, ((), ())), preferred_element_type=_F32)


_NN = ((1,), (0,))
_NT = ((1,), (1,))
_TN = ((0,), (0,))


def _scan_fwd(a, b, rows):
    n = a.shape[0]
    d = 1
    while d < n:
        keep = rows >= d
        b = jnp.where(keep, b + a * pltpu.roll(b, d, 0), b)
        a = jnp.where(keep, a * pltpu.roll(a, d, 0), a)
        d *= 2
    return a, b


def _scan_rev(a, b, rows):
    n = a.shape[0]
    d = 1
    while d < n:
        keep = rows < n - d
        b = jnp.where(keep, b + a * pltpu.roll(b, n - d, 0), b)
        a = jnp.where(keep, a * pltpu.roll(a, n - d, 0), a)
        d *= 2
    return a, b


def _colsum(v):
    return jnp.sum(v, axis=0, keepdims=True)


def _mm(a, b, *, dims, grid, a_spec, b_spec, outs, acc_shape, name, extra=(), epi=None, slabs=1, comm=None):
    nred = grid[-1]
    red_axis = len(grid) - 1
    n_ex, n_out = len(extra), len(outs)
    epi = _epi_store if epi is None else epi

    def body(*refs):
        a_ref, b_ref = refs[:2]
        ex, o_refs, acc_ref = refs[2:2 + n_ex], refs[2 + n_ex:2 + n_ex + n_out], refs[-1]
        if slabs == 1:
            p = _dot(a_ref[...], b_ref[...], dims)
        else:
            n = b_ref.shape[-1]
            p = _dot(a_ref[:, 0:n], b_ref[0], dims)
            for jj in range(1, slabs):
                p = p + _dot(a_ref[:, jj * n:(jj + 1) * n], b_ref[jj], dims)

        first_rows = pl.program_id(0) == 0
        if nred == 1:
            epi(p, ex, o_refs, first_rows)
        else:
            k = pl.program_id(red_axis)

            @pl.when(k == 0)
            def _():
                acc_ref[...] = p

            @pl.when(jnp.logical_and(k > 0, k < nred - 1))
            def _():
                acc_ref[...] += p

            @pl.when(k == nred - 1)
            def _():
                epi(acc_ref[...] + p, ex, o_refs, first_rows)

    sem = ("parallel",) * (len(grid) - 1) + ("arbitrary",)
    if any(o[0].shape[0] == 1 for o in outs):
        sem = ("arbitrary",) * len(grid)
    res, cres = _pcall(
        body, args=(a, b) + tuple(e[0] for e in extra), grid=grid,
        in_specs=[a_spec, b_spec] + [e[1] for e in extra],
        out_specs=[o[1] for o in outs], out_shape=[o[0] for o in outs],
        scratch_shapes=[pltpu.VMEM(acc_shape if nred > 1 else (8, 128), _F32)], sem=sem, name=name, comm=comm)
    res = res[0] if n_out == 1 else res
    return res if comm is None else (res, cres)


def _epi_store(total, ex, outs, first_rows):
    outs[0][...] = total.astype(outs[0].dtype)


def _epi_residual_rms(total, ex, outs, first_rows):
    res_ref, g_ref = ex
    xn = total + res_ref[...]
    outs[0][...] = xn
    r = lax.rsqrt(jnp.mean(xn * xn, axis=-1, keepdims=True) + _EPS)
    outs[1][...] = (xn * r * g_ref[...]).astype(outs[1].dtype)


def _epi_rms_bwd(total, ex, outs, first_rows):
    x_ref, g_ref, dres_ref = ex
    dg_ref = outs[-1]
    xv = x_ref[...]
    r = lax.rsqrt(jnp.mean(xv * xv, axis=-1, keepdims=True) + _EPS)
    xhat = xv * r
    dxh = total * g_ref[...]
    dx = dres_ref[...] + r * (dxh - xhat * jnp.mean(dxh * xhat, axis=-1, keepdims=True))
    for o_ref in outs[:-1]:
        o_ref[...] = dx.astype(o_ref.dtype)

    @pl.when(first_rows)
    def _():
        dg_ref[...] = jnp.zeros_like(dg_ref)

    dg_ref[...] += _colsum(total * xhat)


def _epi_final(total, ex, outs, first_rows):
    res_ref, t_ref, g_ref = ex
    dx_ref, dxb_ref, l_ref, dg_ref = outs
    xv = total + res_ref[...]
    gv = g_ref[...]
    d = xv.shape[-1]
    r = lax.rsqrt(jnp.mean(xv * xv, axis=-1, keepdims=True) + _EPS)
    xhat = xv * r
    err = xhat * gv - t_ref[...]
    dy = err * (1.0 / d)
    dxh = dy * gv
    dx = r * (dxh - xhat * jnp.mean(dxh * xhat, axis=-1, keepdims=True))
    dx_ref[...] = dx
    dxb_ref[...] = dx.astype(dxb_ref.dtype)

    @pl.when(first_rows)
    def _():
        l_ref[...] = jnp.zeros_like(l_ref)
        dg_ref[...] = jnp.zeros_like(dg_ref)

    l_ref[...] += _colsum(err * err)
    dg_ref[...] += _colsum(dy * xhat)


def _tile(m, cap):
    t = min(m, cap)
    assert m % t == 0
    return t


def _row_spec(tm, n):
    return pl.BlockSpec((tm, n), lambda i, *_: (i, 0))


def _vec_spec(n):
    return pl.BlockSpec((1, n), lambda *_: (0, 0))


def _row_io(m, n, tm, extra, outs):
    def spec(kind):
        return _row_spec(tm, n) if kind == "row" else _vec_spec(n)

    ex = [(arr, spec(kind)) for arr, kind in extra]
    os_ = [(jax.ShapeDtypeStruct((m, n) if kind == "row" else (1, n), dt), spec(kind)) for dt, kind in outs]
    return ex, os_


def _mm_nn_stacked(a, w, out_dtype, name, comm=None, tm=1024):
    m, k = a.shape
    j, _, n = w.shape
    tm = _tile(m, tm)
    return _mm(a, w, dims=_NN, grid=(m // tm, j, 1),
               a_spec=pl.BlockSpec((tm, k), lambda i, jj, r: (i, 0)),
               b_spec=pl.BlockSpec((None, k, n), lambda i, jj, r: (jj, 0, 0)),
               outs=[(jax.ShapeDtypeStruct((m, j * n), out_dtype), pl.BlockSpec((tm, n), lambda i, jj, r: (i, jj)))],
               acc_shape=(tm, n), name=name, comm=comm)


def _mm_nt_stacked(dc, w, name, *, outs, extra=(), epi=None, comm=None, tm=1024, slabs=1):
    m = dc.shape[0]
    j, k, n = w.shape
    tm = _tile(m, tm)
    assert j % slabs == 0
    ex, os_ = _row_io(m, k, tm, extra, outs)
    wblk = (None, k, n) if slabs == 1 else (slabs, k, n)
    return _mm(dc, w, dims=_NT, grid=(m // tm, j // slabs),
               a_spec=pl.BlockSpec((tm, slabs * n), lambda i, r: (i, r)),
               b_spec=pl.BlockSpec(wblk, lambda i, r: (r, 0, 0)),
               outs=os_, extra=ex, epi=epi, acc_shape=(tm, k), name=name, slabs=slabs, comm=comm)


def _mm_tn_stacked(a, dc, j, out_dtype, name, slabs=1, ts=1024):
    s, k = a.shape
    n = dc.shape[1] // j
    ts = _tile(s, ts)
    assert j % slabs == 0

    def epi(total, ex, outs, first_rows):
        for jj in range(slabs):
            outs[0][jj] = total[:, jj * n:(jj + 1) * n].astype(outs[0].dtype)

    return _mm(a, dc, dims=_TN, grid=(j // slabs, s // ts),
               a_spec=pl.BlockSpec((ts, k), lambda jj, r: (r, 0)),
               b_spec=pl.BlockSpec((ts, slabs * n), lambda jj, r: (r, jj)),
               outs=[(jax.ShapeDtypeStruct((j, k, n), out_dtype),
                      pl.BlockSpec((slabs, k, n), lambda jj, r: (jj, 0, 0)))],
               epi=epi, acc_shape=(k, slabs * n), name=name)


def _mm_tn_nat(a, dc, out_dtype, name, ts=1024):
    s, kt = a.shape
    n = dc.shape[1]
    ts = _tile(s, ts)
    tkb = _tile(kt, 1024)
    return _mm(a, dc, dims=_TN, grid=(kt // tkb, s // ts),
               a_spec=pl.BlockSpec((ts, tkb), lambda kb, r: (r, kb)),
               b_spec=pl.BlockSpec((ts, n), lambda kb, r: (r, 0)),
               outs=[(jax.ShapeDtypeStruct((kt, n), out_dtype), pl.BlockSpec((tkb, n), lambda kb, r: (kb, 0)))],
               acc_shape=(tkb, n), name=name)


def _rms_fwd(x, g, name, comm=None):
    s, d = x.shape
    t = _tile(s, 1024)

    def body(x_ref, g_ref, h_ref):
        xv = x_ref[...]
        r = lax.rsqrt(jnp.mean(xv * xv, axis=-1, keepdims=True) + _EPS)
        h_ref[...] = (xv * r * g_ref[...]).astype(h_ref.dtype)

    res, cres = _pcall(
        body, args=(x, g), grid=(s // t,),
        in_specs=[pl.BlockSpec((t, d), lambda i: (i, 0)), pl.BlockSpec((1, d), lambda i: (0, 0))],
        out_specs=pl.BlockSpec((t, d), lambda i: (i, 0)),
        out_shape=jax.ShapeDtypeStruct((s, d), _MXU), sem=("parallel",), name=name, comm=comm)
    return res[0] if comm is None else (res[0], cres)


def _rms_bwd(dh, x, g, dres, name):
    s, d = x.shape
    t = _tile(s, 256)
    has_res = dres is not None

    def body(*refs):
        if has_res:
            dh_ref, x_ref, g_ref, dres_ref, dx_ref, dxb_ref, dg_ref = refs
        else:
            dh_ref, x_ref, g_ref, dx_ref, dxb_ref, dg_ref = refs
        xv = x_ref[...]
        dhv = dh_ref[...]
        r = lax.rsqrt(jnp.mean(xv * xv, axis=-1, keepdims=True) + _EPS)
        xhat = xv * r
        dxh = dhv * g_ref[...]
        dx = r * (dxh - xhat * jnp.mean(dxh * xhat, axis=-1, keepdims=True))
        if has_res:
            dx = dx + dres_ref[...]
        dx_ref[...] = dx
        dxb_ref[...] = dx.astype(dxb_ref.dtype)

        @pl.when(pl.program_id(0) == 0)
        def _():
            dg_ref[...] = jnp.zeros_like(dg_ref)

        dg_ref[...] += _colsum(dhv * xhat)

    row = pl.BlockSpec((t, d), lambda i: (i, 0))
    vec = pl.BlockSpec((1, d), lambda i: (0, 0))
    in_specs = [row, row, vec] + ([row] if has_res else [])
    args = (dh, x, g) + ((dres,) if has_res else ())
    return pl.pallas_call(
        body, grid=(s // t,), in_specs=in_specs, out_specs=[row, row, vec],
        out_shape=[jax.ShapeDtypeStruct((s, d), _F32), jax.ShapeDtypeStruct((s, d), _MXU),
                   jax.ShapeDtypeStruct((1, d), _F32)],
        compiler_params=_cparams("arbitrary"), name=name,
    )(*args)


_LRU_K = 4
_CONF_K = 31
_LRU_HALO = 8
_CONF_HALO = 32
_MIX_T = 512


def _lru_gates(lx, wab_ref, ba_ref, bx_ref, lam_ref):
    c = _D_LRU
    pre = _dot(lx, wab_ref[...], _NN)
    r = _sigmoid(pre[:, :c] + ba_ref[...])
    ig = _sigmoid(pre[:, c:] + bx_ref[...])
    sp = _softplus(-lam_ref[...])
    log_a = (-_RG_C) * r * sp
    a = jnp.exp(log_a)
    mult = jnp.sqrt(_neg_expm1(2.0 * log_a))
    return r, ig, sp, a, mult


def _causal_conv(ext_ref, halo, w_ref, b_ref, taps, t):
    acc = b_ref[...] + w_ref[0:1, :] * ext_ref[pl.ds(halo - (taps - 1), t), :]
    for k in range(1, taps):
        acc = acc + w_ref[k:k + 1, :] * ext_ref[pl.ds(halo - (taps - 1) + k, t), :]
    return acc


class _Windows:
    def __init__(self, ext_ref, shifted_ref, t):
        self.ext_ref, self.shifted_ref, self.t = ext_ref, shifted_ref, t
        rows = t + 24
        for r in range(1, 8):
            shifted_ref[r - 1, 0:rows, :] = ext_ref[pl.ds(r, rows), :]

    def __call__(self, off):
        q, r = divmod(off, 8)
        if r == 0:
            return self.ext_ref[pl.ds(8 * q, self.t), :]
        return self.shifted_ref[r - 1, pl.ds(8 * q, self.t), :]


def _mixer_fwd(xs, h1, w_in, w_out, gnorm, lcw, lcb, wab, ba, bx, lam, ccw, ccb, lng, lnb, name, comm=None):
    s, d = xs.shape
    nblk, _, n = w_in.shape
    c = _D_LRU
    t = _tile(s, _MIX_T)
    nt = s // t

    def body(x_ref, h_ref, win_ref, wout_ref, gn_ref,
             lcw_ref, lcb_ref, wab_ref, ba_ref, bx_ref, lam_ref, ccw_ref, ccb_ref, lng_ref, lnb_ref,
             z_ref, ycat_ref, hs_ref, cc_ref, x1_ref, h2_ref,
             ext_ref, cge_ref, hc_ref, shifted_ref, zprev_ref):
        i = pl.program_id(0)
        first = i == 0
        rows = lax.broadcasted_iota(jnp.int32, (t, c), 0)

        @pl.when(first)
        def _():
            zprev_ref[...] = jnp.zeros_like(zprev_ref)

        hv = h_ref[...]
        for j in range(nblk):
            z_ref[:, j * n:(j + 1) * n] = _dot(hv, win_ref[j], _NN)
        lx0_ref, gate_ref = z_ref.at[:, 0:c], z_ref.at[:, c:2 * c]
        ca_ref, cb_ref = z_ref.at[:, 2 * c:3 * c], z_ref.at[:, 3 * c:4 * c]
        lx0h_ref = zprev_ref.at[_CONF_HALO - _LRU_HALO:_CONF_HALO, 0:c]
        cah_ref, cbh_ref = zprev_ref.at[:, 2 * c:3 * c], zprev_ref.at[:, 3 * c:4 * c]

        ext_ref[0:_LRU_HALO, :] = jnp.where(first, 0.0, lx0h_ref[...])
        ext_ref[_LRU_HALO:_LRU_HALO + t, :] = lx0_ref[...]
        lx = _causal_conv(ext_ref, _LRU_HALO, lcw_ref, lcb_ref, _LRU_K, t)
        r, ig, sp, a, mult = _lru_gates(lx, wab_ref, ba_ref, bx_ref, lam_ref)
        u = mult * (ig * lx)
        a_cum, h_loc = _scan_fwd(a, u, rows)

        @pl.when(first)
        def _():
            hc_ref[...] = jnp.zeros_like(hc_ref)

        h = h_loc + a_cum * hc_ref[7:8, :]
        hs_ref[...] = h
        hc_ref[...] = hs_ref[pl.ds(t - 8, 8), :]
        ycat_ref[:, 0:c] = (h * _gelu(gate_ref[...])).astype(ycat_ref.dtype)

        cge_ref[0:_CONF_HALO, :] = jnp.where(first, 0.0, cah_ref[...] * _sigmoid(cbh_ref[...]))
        cge_ref[_CONF_HALO:_CONF_HALO + t, :] = ca_ref[...] * _sigmoid(cb_ref[...])
        win = _Windows(cge_ref, shifted_ref, t)
        first_off = _CONF_HALO - (_CONF_K - 1)
        cc = ccb_ref[...] + ccw_ref[0:1, :] * win(first_off)
        for k in range(1, _CONF_K):
            cc = cc + ccw_ref[k:k + 1, :] * win(first_off + k)
        cc_ref[...] = cc
        xc = cc - jnp.mean(cc, axis=-1, keepdims=True)
        rstd = lax.rsqrt(jnp.mean(xc * xc, axis=-1, keepdims=True) + _EPS)
        ln = xc * rstd * lng_ref[...] + lnb_ref[...]
        ycat_ref[:, c:2 * c] = (ln * _sigmoid(ln)).astype(ycat_ref.dtype)

        zprev_ref[...] = z_ref[pl.ds(t - _CONF_HALO, _CONF_HALO), :]
        y = _dot(ycat_ref[...], wout_ref[...], _NN)
        _epi_residual_rms(y, (x_ref, gn_ref), (x1_ref, h2_ref), first)

    def const(arr):
        return pl.BlockSpec(arr.shape, lambda i: (0,) * arr.ndim, pipeline_mode=pl.Buffered(1))

    def rows_of(width):
        return pl.BlockSpec((t, width), lambda i: (i, 0))

    params = (lcw, lcb, wab, ba, bx, lam, ccw, ccb, lng, lnb)
    res, cres = _pcall(
        body, args=(xs, h1, w_in, w_out, gnorm, *params), grid=(nt,),
        in_specs=[rows_of(d), rows_of(d), const(w_in), const(w_out), const(gnorm)] + [const(p) for p in params],
        out_specs=[rows_of(nblk * n), rows_of(2 * c), rows_of(c), rows_of(c), rows_of(d), rows_of(d)],
        out_shape=[jax.ShapeDtypeStruct((s, nblk * n), _F32), jax.ShapeDtypeStruct((s, 2 * c), _MXU),
                   jax.ShapeDtypeStruct((s, c), _F32), jax.ShapeDtypeStruct((s, c), _F32),
                   jax.ShapeDtypeStruct((s, d), _F32), jax.ShapeDtypeStruct((s, d), _MXU)],
        scratch_shapes=[pltpu.VMEM((t + _LRU_HALO, c), _F32), pltpu.VMEM((t + _CONF_HALO, c), _F32),
                        pltpu.VMEM((8, c), _F32), pltpu.VMEM((7, t + _CONF_HALO, c), _F32),
                        pltpu.VMEM((_CONF_HALO, nblk * n), _F32)],
        sem=("arbitrary",), name=name, comm=comm)
    return res, cres


def _mixer_bwd(dx1b, w_out, h1, z, hs, cc, lcw, lcb, wab, ba, bx, lam, ccw, ccb, lng, lnb, name, comm=None):
    s = z.shape[0]
    d = h1.shape[1]
    c = _D_LRU
    t = _tile(s, _MIX_T)
    nt = s // t
    nblk = z.shape[1] // 256

    def body(dxb_ref, wout_ref, h1_ref, lx0_ref, lx0h_ref, gate_ref, ca_ref, cah_ref, cb_ref, cbh_ref,
             hs_ref, hsh_ref, cc_ref,
             lcw_ref, lcb_ref, wab_ref, ba_ref, bx_ref, lam_ref, ccw_ref, ccb_ref, lng_ref, lnb_ref,
             dz_ref, pin_ref, dlcw_ref, dlcb_ref, dwab_ref, dba_ref, dbx_ref, dlam_ref, dccw_ref, dccb_ref, dlng_ref,
             dlnb_ref,
             ext_ref, up_ref, cge_ref, dce_ref, xc_ref, dlxc_ref, dccc_ref, shifted_ref, dwin_ref):
        i = pl.program_id(0)
        first_tile = i == nt - 1
        last_tile = i == 0
        rows = lax.broadcasted_iota(jnp.int32, (t, c), 0)

        @pl.when(last_tile)
        def _():
            for ref in (dlcw_ref, dlcb_ref, dwab_ref, dba_ref, dbx_ref, dlam_ref, dccw_ref, dccb_ref, dlng_ref,
                        dlnb_ref, xc_ref, dlxc_ref, dccc_ref, dwin_ref):
                ref[...] = jnp.zeros_like(ref)

        dycat = _dot(dxb_ref[...], wout_ref[...], _NT)

        ext_ref[0:_LRU_HALO, :] = jnp.where(first_tile, 0.0, lx0h_ref[...])
        ext_ref[_LRU_HALO:_LRU_HALO + t, :] = lx0_ref[...]
        lx = _causal_conv(ext_ref, _LRU_HALO, lcw_ref, lcb_ref, _LRU_K, t)
        r, ig, sp, a, mult = _lru_gates(lx, wab_ref, ba_ref, bx_ref, lam_ref)
        h = hs_ref[...]
        gl, dgl = _gelu_and_grad(gate_ref[...])
        dyl = dycat[:, 0:c]
        dz_ref[:, c:2 * c] = (dyl * h * dgl).astype(dz_ref.dtype)
        dh = dyl * gl

        up_ref[0:t, :] = a
        up_ref[t:t + 8, :] = jnp.ones((8, c), _F32)
        a_up = up_ref[pl.ds(1, t), :]
        a_cum, g_loc = _scan_rev(a_up, dh, rows)
        gt = g_loc + a_cum * xc_ref[0:1, :]
        xc_ref[...] = (a * gt)[0:8, :]

        up_ref[0:8, :] = jnp.where(first_tile, 0.0, hsh_ref[...])
        up_ref[8:8 + t, :] = h
        hprev = up_ref[pl.ds(7, t), :]

        da = gt * hprev
        dmult = gt * ig * lx
        dig = gt * mult * lx
        dlx = gt * mult * ig
        dlog_a = da * a - dmult * a * a / mult
        dpre_r = dlog_a * (-_RG_C) * sp * r * (1.0 - r)
        dpre_i = dig * ig * (1.0 - ig)
        dlam_ref[...] += _colsum(dlog_a * r) * (_RG_C * _sigmoid(-lam_ref[...]))
        dba_ref[...] += _colsum(dpre_r)
        dbx_ref[...] += _colsum(dpre_i)
        dpre = jnp.concatenate([dpre_r, dpre_i], axis=1).astype(_MXU)
        dlx = dlx + _dot(dpre, wab_ref[...], _NT)
        dwab_ref[...] += _dot(lx, dpre, _TN)

        dlcb_ref[...] += _colsum(dlx)
        up_ref[0:t, :] = dlx
        up_ref[t:t + 8, :] = dlxc_ref[...]
        dlxc_ref[...] = dlx[0:8, :]
        acc = lcw_ref[0:1, :] * up_ref[pl.ds(_LRU_K - 1, t), :]
        for k in range(1, _LRU_K):
            acc = acc + lcw_ref[k:k + 1, :] * up_ref[pl.ds(_LRU_K - 1 - k, t), :]
        dz_ref[:, 0:c] = acc.astype(dz_ref.dtype)
        for k in range(_LRU_K):
            dlcw_ref[k:k + 1, :] += _colsum(dlx * ext_ref[pl.ds(_LRU_HALO - (_LRU_K - 1) + k, t), :])

        sig_b = _sigmoid(cb_ref[...])
        ca = ca_ref[...]
        cge_ref[0:_CONF_HALO, :] = jnp.where(first_tile, 0.0, cah_ref[...] * _sigmoid(cbh_ref[...]))
        cge_ref[_CONF_HALO:_CONF_HALO + t, :] = ca * sig_b
        ccv = cc_ref[...]
        xcen = ccv - jnp.mean(ccv, axis=-1, keepdims=True)
        rstd = lax.rsqrt(jnp.mean(xcen * xcen, axis=-1, keepdims=True) + _EPS)
        xn = xcen * rstd
        ln = xn * lng_ref[...] + lnb_ref[...]
        sg = _sigmoid(ln)
        dln = dycat[:, c:2 * c] * (sg * (1.0 + ln * (1.0 - sg)))
        dlng_ref[...] += _colsum(dln * xn)
        dlnb_ref[...] += _colsum(dln)
        dxn = dln * lng_ref[...]
        dcc = rstd * (dxn - jnp.mean(dxn, axis=-1, keepdims=True)
                      - xn * jnp.mean(dxn * xn, axis=-1, keepdims=True))
        dccb_ref[...] += _colsum(dcc)
        win = _Windows(cge_ref, shifted_ref, t)
        for k in range(_CONF_K):
            dccw_ref[k:k + 1, :] += _colsum(dcc * win(_CONF_HALO - (_CONF_K - 1) + k))
        dce_ref[0:t, :] = dcc
        dce_ref[t:t + _CONF_HALO, :] = dccc_ref[...]
        dccc_ref[...] = dcc[0:_CONF_HALO, :]
        win = _Windows(dce_ref, shifted_ref, t)
        dcg = ccw_ref[0:1, :] * win(_CONF_K - 1)
        for k in range(1, _CONF_K):
            dcg = dcg + ccw_ref[k:k + 1, :] * win(_CONF_K - 1 - k)
        dz_ref[:, 2 * c:3 * c] = (dcg * sig_b).astype(dz_ref.dtype)
        dz_ref[:, 3 * c:4 * c] = (dcg * ca * sig_b * (1.0 - sig_b)).astype(dz_ref.dtype)

        dwin_ref[...] += _dot(h1_ref[...], dz_ref[...], _TN)

        @pl.when(first_tile)
        def _():
            for j in range(nblk):
                pin_ref[j] = dwin_ref[:, j * 256:(j + 1) * 256].astype(pin_ref.dtype)

    def col(j):
        return pl.BlockSpec((t, c), lambda i: (nt - 1 - i, j))

    def halo(j, rows_):
        per = t // rows_
        return pl.BlockSpec((rows_, c), lambda i: (jnp.maximum((nt - 1 - i) * per - 1, 0), j))

    def full(shape):
        return pl.BlockSpec(shape, lambda i: (0,) * len(shape))

    params = (lcw, lcb, wab, ba, bx, lam, ccw, ccb, lng, lnb)
    small = [(_LRU_K, c), (1, c), (c, 2 * c), (1, c), (1, c), (1, c), (_CONF_K, c), (1, c), (1, c), (1, c)]
    wide = pl.BlockSpec((t, d), lambda i: (nt - 1 - i, 0))
    pin_shape = (nblk, d, 256)
    return _pcall(
        body, args=(dx1b, w_out, h1, z, z, z, z, z, z, z, hs, hs, cc, *params), grid=(nt,),
        in_specs=[wide, pl.BlockSpec(w_out.shape, lambda i: (0, 0), pipeline_mode=pl.Buffered(1)), wide,
                  col(0), halo(0, _LRU_HALO), col(1), col(2), halo(2, _CONF_HALO), col(3), halo(3, _CONF_HALO),
                  col(0), halo(0, 8), col(0)]
        + [full(p.shape) for p in params],
        out_specs=[pl.BlockSpec((t, 4 * c), lambda i: (nt - 1 - i, 0)), full(pin_shape)] + [full(sh) for sh in small],
        out_shape=[jax.ShapeDtypeStruct((s, 4 * c), _MXU), jax.ShapeDtypeStruct(pin_shape, _XFER)]
        + [jax.ShapeDtypeStruct(sh, _F32) for sh in small],
        scratch_shapes=[pltpu.VMEM((t + _LRU_HALO, c), _F32), pltpu.VMEM((t + 8, c), _F32),
                        pltpu.VMEM((t + _CONF_HALO, c), _F32), pltpu.VMEM((t + _CONF_HALO, c), _F32),
                        pltpu.VMEM((8, c), _F32), pltpu.VMEM((8, c), _F32), pltpu.VMEM((_CONF_HALO, c), _F32),
                        pltpu.VMEM((7, t + _CONF_HALO, c), _F32), pltpu.VMEM((d, 4 * c), _F32)],
        sem=("arbitrary",), name=name, comm=comm, vmem=_VMEM_LIMIT_FUSED)


_ATT_T = 512


def _attn_probs(qh, kh, scale):
    sc = _dot(qh, kh, _NT) * scale
    e = jnp.exp(sc - jnp.max(sc, axis=-1, keepdims=True))
    return e / jnp.sum(e, axis=-1, keepdims=True)


def _const_spec(arr):
    return pl.BlockSpec(arr.shape, lambda i: (0,) * arr.ndim, pipeline_mode=pl.Buffered(1))


def _attn_fwd(h2, x1, w_q, kv, w_o, gnorm, name):
    s, d = h2.shape
    nm = kv.shape[0]
    hd = d // _XA_HEADS
    t = _tile(s, _ATT_T)
    scale = hd ** -0.5

    def body(h_ref, x1_ref, wq_ref, k_ref, v_ref, wo_ref, g_ref, q_ref, o_ref, x2_ref, h3_ref):
        q_ref[...] = _dot(h_ref[...], wq_ref[...], _NN).astype(q_ref.dtype)
        for hh in range(_XA_HEADS):
            sl = slice(hh * hd, (hh + 1) * hd)
            p = _attn_probs(q_ref[:, sl], k_ref[:, sl], scale)
            o_ref[:, sl] = _dot(p, v_ref[:, sl], _NN).astype(o_ref.dtype)
        y = _dot(o_ref[...], wo_ref[...], _NN)
        _epi_residual_rms(y, (x1_ref, g_ref), (x2_ref, h3_ref), None)

    row = pl.BlockSpec((t, d), lambda i: (i, 0))
    half = pl.BlockSpec((nm, d), lambda i: (0, 0), pipeline_mode=pl.Buffered(1))
    half2 = pl.BlockSpec((nm, d), lambda i: (0, 1), pipeline_mode=pl.Buffered(1))
    return pl.pallas_call(
        body, grid=(s // t,),
        in_specs=[row, row, _const_spec(w_q), half, half2, _const_spec(w_o), _const_spec(gnorm)],
        out_specs=[row, row, row, row],
        out_shape=[jax.ShapeDtypeStruct((s, d), _MXU), jax.ShapeDtypeStruct((s, d), _MXU),
                   jax.ShapeDtypeStruct((s, d), _F32), jax.ShapeDtypeStruct((s, d), _MXU)],
        compiler_params=_cparams("parallel"), name=name,
    )(h2, x1, w_q, kv, kv, w_o, gnorm)


def _attn_bwd(dx2b, dx2, q, o, h2, x1, kv, w_o, w_q, gnorm, name, comm=None):
    s, d = q.shape
    nm = kv.shape[0]
    hd = d // _XA_HEADS
    t = _tile(s, _ATT_T)
    nt = s // t
    scale = hd ** -0.5

    def body(dxb_ref, dx2_ref, q_ref, o_ref, h_ref, x1_ref, k_ref, v_ref, wo_ref, wq_ref, g_ref,
             dx1_ref, dx1b_ref, dgn_ref, dk_ref, dv_ref, pwo_ref, pwq_ref, dq_ref, awo_ref, awq_ref):
        i = pl.program_id(0)
        first = i == 0

        @pl.when(first)
        def _():
            for ref in (dk_ref, dv_ref, awo_ref, awq_ref):
                ref[...] = jnp.zeros_like(ref)

        dxb = dxb_ref[...]
        do = _dot(dxb, wo_ref[...], _NT).astype(_MXU)
        awo_ref[...] += _dot(o_ref[...], dxb, _TN)
        for hh in range(_XA_HEADS):
            sl = slice(hh * hd, (hh + 1) * hd)
            qh = q_ref[:, sl]
            kh = k_ref[:, sl]
            doh = do[:, sl]
            p = _attn_probs(qh, kh, scale)
            dp = _dot(doh, v_ref[:, sl], _NT)
            dv_ref[:, sl] += _dot(p, doh, _TN)
            ds = (p * (dp - jnp.sum(dp * p, axis=-1, keepdims=True)) * scale).astype(_MXU)
            dq_ref[:, sl] = _dot(ds, kh, _NN).astype(dq_ref.dtype)
            dk_ref[:, sl] += _dot(ds, qh, _TN)
        dq = dq_ref[...]
        awq_ref[...] += _dot(h_ref[...], dq, _TN)
        dh = _dot(dq, wq_ref[...], _NT)
        _epi_rms_bwd(dh, (x1_ref, g_ref, dx2_ref), (dx1_ref, dx1b_ref, dgn_ref), first)

        @pl.when(i == nt - 1)
        def _():
            pwo_ref[...] = awo_ref[...].astype(pwo_ref.dtype)
            pwq_ref[...] = awq_ref[...].astype(pwq_ref.dtype)

    row = pl.BlockSpec((t, d), lambda i: (i, 0))
    vec = pl.BlockSpec((1, d), lambda i: (0, 0))
    mem_blk = pl.BlockSpec((nm, d), lambda i: (0, 0))
    sq = pl.BlockSpec((d, d), lambda i: (0, 0))
    half = pl.BlockSpec((nm, d), lambda i: (0, 0), pipeline_mode=pl.Buffered(1))
    half2 = pl.BlockSpec((nm, d), lambda i: (0, 1), pipeline_mode=pl.Buffered(1))
    return _pcall(
        body, args=(dx2b, dx2, q, o, h2, x1, kv, kv, w_o, w_q, gnorm), grid=(nt,),
        in_specs=[row, row, row, row, row, row, half, half2, _const_spec(w_o), _const_spec(w_q), _const_spec(gnorm)],
        out_specs=[row, row, vec, mem_blk, mem_blk, sq, sq],
        out_shape=[jax.ShapeDtypeStruct((s, d), _F32), jax.ShapeDtypeStruct((s, d), _MXU),
                   jax.ShapeDtypeStruct((1, d), _F32), jax.ShapeDtypeStruct((nm, d), _F32),
                   jax.ShapeDtypeStruct((nm, d), _F32), jax.ShapeDtypeStruct((d, d), _XFER),
                   jax.ShapeDtypeStruct((d, d), _XFER)],
        scratch_shapes=[pltpu.VMEM((t, d), _MXU), pltpu.VMEM((d, d), _F32), pltpu.VMEM((d, d), _F32)],
        sem=("arbitrary",), name=name, comm=comm, vmem=_VMEM_LIMIT_FUSED)


_FFN_K = 3
_FFN_FUSED_T = 256
_VMEM_LIMIT_FUSED = 58 * 1024 * 1024


def _ffn_fused_fwd(h3, w_up, w_down, fcw, fcb, x2, target, gfin, name):
    s, d = h3.shape
    nblk, _, n = w_up.shape
    half = nblk // 2
    f = half * n
    t = _tile(s, _FFN_FUSED_T)

    def body(h_ref, wup_ref, wdown_ref, w_ref, b_ref, x2_ref, t_ref, g_ref,
             g0_ref, act_ref, gl_ref, udgl_ref, dx_ref, dxb_ref, l_ref, dg_ref, ext0_ref, ext1_ref, halo_ref):
        i = pl.program_id(0)
        first = i == 0
        h = h_ref[...]
        total = None
        ahead = (_dot(h, wup_ref[0], _NN), _dot(h, wup_ref[half], _NN))
        for j in range(half):
            cs = slice(j * n, (j + 1) * n)
            ext_ref = ext0_ref if j % 2 == 0 else ext1_ref
            g0, u = ahead
            if j + 1 < half:
                ahead = (_dot(h, wup_ref[j + 1], _NN), _dot(h, wup_ref[half + j + 1], _NN))
            if j > 0:
                prev = slice((j - 1) * n, j * n)
                p = _dot(act_ref[:, prev], wdown_ref[prev, :], _NN)
                total = p if total is None else total + p
            g0_ref[:, cs] = g0.astype(g0_ref.dtype)
            ext_ref[0:8, :] = jnp.where(first, 0.0, halo_ref[:, cs])
            ext_ref[8:8 + t, :] = g0
            halo_ref[:, cs] = g0[t - 8:t, :]
            g = _causal_conv(ext_ref, 8, w_ref.at[:, cs], b_ref.at[:, cs], _FFN_K, t)
            gl, dgl = _gelu_and_grad(g)
            gl_ref[:, cs] = gl.astype(gl_ref.dtype)
            udgl_ref[:, cs] = (u * dgl).astype(udgl_ref.dtype)
            act_ref[:, cs] = (gl * u).astype(act_ref.dtype)
        last = slice((half - 1) * n, half * n)
        total = total + _dot(act_ref[:, last], wdown_ref[last, :], _NN)
        _epi_final(total, (x2_ref, t_ref, g_ref), (dx_ref, dxb_ref, l_ref, dg_ref), first)

    def const(shape):
        return pl.BlockSpec(shape, lambda i: (0,) * len(shape), pipeline_mode=pl.Buffered(1))

    row = pl.BlockSpec((t, d), lambda i: (i, 0))
    vec = pl.BlockSpec((1, d), lambda i: (0, 0))
    return pl.pallas_call(
        body, grid=(s // t,),
        in_specs=[row, const(w_up.shape), const(w_down.shape), const(fcw.shape), const(fcb.shape), row, row, vec],
        out_specs=[pl.BlockSpec((t, f), lambda i: (i, 0))] * 4 + [row, row, vec, vec],
        out_shape=[jax.ShapeDtypeStruct((s, f), _MXU)] * 4
        + [jax.ShapeDtypeStruct((s, d), _F32), jax.ShapeDtypeStruct((s, d), _MXU),
                   jax.ShapeDtypeStruct((1, d), _F32), jax.ShapeDtypeStruct((1, d), _F32)],
        scratch_shapes=[pltpu.VMEM((t + 8, n), _F32), pltpu.VMEM((t + 8, n), _F32), pltpu.VMEM((8, f), _F32)],
        compiler_params=_cparams("arbitrary", vmem=_VMEM_LIMIT_FUSED), name=name,
    )(h3, w_up, w_down, fcw, fcb, x2, target, gfin)


def _ffn_fused_bwd(dx3b, g0, gl, udgl, w_down, w_up, fcw, x2, gnorm, dx3, name, comm=None):
    s, d = x2.shape
    nblk, _, n = w_up.shape
    half = nblk // 2
    f = half * n
    t = _tile(s, _FFN_FUSED_T)
    nt = s // t
    hrows = 16

    def body(dxb_ref, g0_ref, g0h_ref, gl_ref, udgl_ref, wdown_ref, wup_ref, w_ref, x2_ref, g_ref, dx3_ref,
             dgu_ref, dx2_ref, dx2b_ref, dgn_ref, dw_ref, db_ref, ext0_ref, ext1_ref, up0_ref, up1_ref, car_ref):
        i = pl.program_id(0)
        first_tile = i == nt - 1
        last_tile = i == 0

        @pl.when(last_tile)
        def _():
            dw_ref[...] = jnp.zeros_like(dw_ref)
            db_ref[...] = jnp.zeros_like(db_ref)
            car_ref[...] = jnp.zeros_like(car_ref)

        dxb = dxb_ref[...]
        total = None
        for j in range(half):
            cs = slice(j * n, (j + 1) * n)
            us = slice(f + j * n, f + (j + 1) * n)
            ext_ref = ext0_ref if j % 2 == 0 else ext1_ref
            up_ref = up0_ref if j % 2 == 0 else up1_ref
            dact = _dot(dxb, wdown_ref[cs, :], _NT)
            ext_ref[0:8, :] = jnp.where(first_tile, 0.0, g0h_ref[:, cs].astype(_F32)[hrows - 8:hrows])
            ext_ref[8:8 + t, :] = g0_ref[:, cs].astype(_F32)
            du = (dact * gl_ref[:, cs].astype(_F32)).astype(dgu_ref.dtype)
            dgu_ref[:, us] = du
            dg = dact * udgl_ref[:, cs].astype(_F32)
            db_ref[:, cs] += _colsum(dg)
            for k in range(_FFN_K):
                dw_ref[k:k + 1, cs] += _colsum(dg * ext_ref[pl.ds(8 - (_FFN_K - 1) + k, t), :])
            up_ref[0:t, :] = dg
            up_ref[t:t + 8, :] = car_ref[:, cs]
            car_ref[:, cs] = dg[0:8, :]
            dg0 = w_ref[0:1, cs] * up_ref[pl.ds(_FFN_K - 1, t), :]
            for k in range(1, _FFN_K):
                dg0 = dg0 + w_ref[k:k + 1, cs] * up_ref[pl.ds(_FFN_K - 1 - k, t), :]
            dg0 = dg0.astype(dgu_ref.dtype)
            dgu_ref[:, cs] = dg0
            p = _dot(dg0, wup_ref[j], _NT) + _dot(du, wup_ref[half + j], _NT)
            total = p if total is None else total + p
        _epi_rms_bwd(total, (x2_ref, g_ref, dx3_ref), (dx2_ref, dx2b_ref, dgn_ref), last_tile)

    def const(shape):
        return pl.BlockSpec(shape, lambda i: (0,) * len(shape), pipeline_mode=pl.Buffered(1))

    row = pl.BlockSpec((t, d), lambda i: (nt - 1 - i, 0))
    vec = pl.BlockSpec((1, d), lambda i: (0, 0))
    per = t // hrows
    wide = pl.BlockSpec((t, f), lambda i: (nt - 1 - i, 0))
    return _pcall(
        body, args=(dx3b, g0, g0, gl, udgl, w_down, w_up, fcw, x2, gnorm, dx3), grid=(nt,),
        in_specs=[row, wide, pl.BlockSpec((hrows, f), lambda i: (jnp.maximum((nt - 1 - i) * per - 1, 0), 0)),
                  wide, wide, const(w_down.shape), const(w_up.shape), const(fcw.shape), row, vec, row],
        out_specs=[pl.BlockSpec((t, 2 * f), lambda i: (nt - 1 - i, 0)), row, row, vec,
                   pl.BlockSpec((_FFN_K, f), lambda i: (0, 0)), pl.BlockSpec((1, f), lambda i: (0, 0))],
        out_shape=[jax.ShapeDtypeStruct((s, 2 * f), _MXU), jax.ShapeDtypeStruct((s, d), _F32),
                   jax.ShapeDtypeStruct((s, d), _MXU), jax.ShapeDtypeStruct((1, d), _F32),
                   jax.ShapeDtypeStruct((_FFN_K, f), _F32), jax.ShapeDtypeStruct((1, f), _F32)],
        scratch_shapes=[pltpu.VMEM((t + 8, n), _F32), pltpu.VMEM((t + 8, n), _F32),
                        pltpu.VMEM((t + 8, n), _F32), pltpu.VMEM((t + 8, n), _F32), pltpu.VMEM((8, f), _F32)],
        sem=("arbitrary",), name=name, comm=comm, vmem=_VMEM_LIMIT_FUSED)


def _mesh_pos():
    return lax.axis_index("x"), lax.axis_index("y"), lax.axis_index("c")


def _flip(v, bit):
    return 1 - v if bit else v


def _sem_scratch(n):
    return [pltpu.SemaphoreType.DMA((7 * n,)), pltpu.SemaphoreType.DMA((7 * n,)), pltpu.SemaphoreType.DMA((n,))]


class _Gather:
    def __init__(self, xs):
        self.ins = list(xs)
        self.outs = [jax.ShapeDtypeStruct((_NDEV,) + v.shape, v.dtype) for v in xs]
        self.scratch = _sem_scratch(len(xs))

    def _plan(self, x_refs, out_refs, sems):
        send_sems, recv_sems, local_sems = sems
        x, y, c = _mesh_pos()
        me, sibling = (x, y, c), (x, y, 1 - c)
        chips = [(1 - x, y), (x, 1 - y), (1 - x, 1 - y)]

        def copy(a, k, block, to, src=None):
            slot = out_refs[a].at[4 * block[0] + 2 * block[1] + block[2]]
            return pltpu.make_async_remote_copy(
                src_ref=slot if src is None else src, dst_ref=slot,
                send_sem=send_sems.at[a * 7 + k], recv_sem=recv_sems.at[a * 7 + k],
                device_id=to, device_id_type=_MESH_ID)

        def own(a):
            return pltpu.make_async_copy(x_refs[a], out_refs[a].at[4 * x + 2 * y + c], local_sems.at[a])

        def first(a):
            return [copy(a, 0, me, sibling, src=x_refs[a])] + [
                copy(a, 1 + j, me, (*chip, c), src=x_refs[a]) for j, chip in enumerate(chips)]

        return me, sibling, chips, c, copy, own, first

    def start(self, x_refs, out_refs, sems):
        _, _, _, _, _, own, first = self._plan(x_refs, out_refs, sems)
        for a in range(len(self.ins)):
            own(a).start()
            for cp in first(a):
                cp.start()

    def finish(self, x_refs, out_refs, sems, results=None):
        me, sibling, chips, c, copy, own, first = self._plan(x_refs, out_refs, sems)
        n = len(self.ins)
        passed = []
        for a in range(n):
            for j, chip in enumerate(chips):
                copy(a, 1 + j, (*chip, c), me).wait_recv()
                fwd = copy(a, 4 + j, (*chip, c), sibling)
                fwd.start()
                passed.append(fwd)
        for a in range(n):
            copy(a, 0, sibling, me).wait_recv()
            for j, chip in enumerate(chips):
                copy(a, 4 + j, (*chip, 1 - c), me).wait_recv()
        for a in range(n):
            for cp in first(a):
                cp.wait_send()
        for cp in passed:
            cp.wait_send()
        for a in range(n):
            own(a).wait()


class _Exchange:
    def __init__(self, gs):
        self.ins = list(gs)
        self.outs = [jax.ShapeDtypeStruct(v.shape, v.dtype) for v in gs]
        self.scratch = _sem_scratch(len(gs))

    def _plan(self, g_refs, r_refs, sems):
        send_sems, recv_sems, local_sems = sems
        x, y, c = _mesh_pos()
        me_idx = 4 * x + 2 * y + c
        n = len(self.ins)

        def copy(a, k):
            peer = (_flip(x, k & 4), _flip(y, k & 2), _flip(c, k & 1))
            peer_idx = 4 * peer[0] + 2 * peer[1] + peer[2]
            return pltpu.make_async_remote_copy(
                src_ref=g_refs[a].at[peer_idx], dst_ref=r_refs[a].at[me_idx],
                send_sem=send_sems.at[a * 7 + k - 1], recv_sem=recv_sems.at[a * 7 + k - 1],
                device_id=peer, device_id_type=_MESH_ID)

        copies = [copy(a, k) for a in range(n) for k in range(1, _NDEV)]
        mine = [pltpu.make_async_copy(g_refs[a].at[me_idx], r_refs[a].at[me_idx], local_sems.at[a])
                for a in range(n)]
        return copies, mine

    def start(self, g_refs, r_refs, sems):
        copies, mine = self._plan(g_refs, r_refs, sems)
        for cp in copies + mine:
            cp.start()

    def finish(self, g_refs, r_refs, sems, results=None):
        copies, mine = self._plan(g_refs, r_refs, sems)
        for cp in copies:
            cp.wait_recv()
        for cp in copies:
            cp.wait_send()
        for cp in mine:
            cp.wait()


class _GatherOfResult:
    def __init__(self, index, shape, dtype):
        self.index = index
        self.ins = []
        self.outs = [jax.ShapeDtypeStruct((_NDEV,) + tuple(shape), dtype)]
        self.scratch = _sem_scratch(1)

    def start(self, ins, outs, sems):
        pass

    def finish(self, ins, outs, sems, results):
        send_sems, recv_sems, local_sems = sems
        src = results[self.index]
        x, y, c = _mesh_pos()
        me_idx = 4 * x + 2 * y + c
        mine = pltpu.make_async_copy(src, outs[0].at[me_idx], local_sems.at[0])
        copies = []
        for k in range(1, _NDEV):
            peer = (_flip(x, k & 4), _flip(y, k & 2), _flip(c, k & 1))
            copies.append(pltpu.make_async_remote_copy(
                src_ref=src, dst_ref=outs[0].at[me_idx], send_sem=send_sems.at[k - 1], recv_sem=recv_sems.at[k - 1],
                device_id=peer, device_id_type=_MESH_ID))
        for cp in copies + [mine]:
            cp.start()
        for cp in copies:
            cp.wait_recv()
        for cp in copies:
            cp.wait_send()
        mine.wait()


class _Several:
    def __init__(self, parts):
        self.parts = list(parts)
        self.ins = [a for p in self.parts for a in p.ins]
        self.outs = [a for p in self.parts for a in p.outs]
        self.scratch = [a for p in self.parts for a in p.scratch]

    def _split(self, ins, outs, sems):
        i = o = s = 0
        for p in self.parts:
            ni, no, ns = len(p.ins), len(p.outs), len(p.scratch)
            yield p, ins[i:i + ni], outs[o:o + no], sems[s:s + ns]
            i, o, s = i + ni, o + no, s + ns

    def start(self, ins, outs, sems):
        for part, i, o, s in self._split(ins, outs, sems):
            part.start(i, o, s)

    def finish(self, ins, outs, sems, results):
        for part, i, o, s in self._split(ins, outs, sems):
            part.finish(i, o, s, results)


def _adamw_math(w, g, m, v):
    m = _ADAM_B1 * m + (1.0 - _ADAM_B1) * g
    v = _ADAM_B2 * v + (1.0 - _ADAM_B2) * (g * g)
    m_hat = m / (1.0 - _ADAM_B1 ** _ADAM_STEP)
    v_hat = v / (1.0 - _ADAM_B2 ** _ADAM_STEP)
    delta = -_ADAM_LR * (m_hat / (jnp.sqrt(v_hat) + _ADAM_EPS) + _ADAM_WD * w)
    return delta, m, v


def _sum_adamw(parts, w, m, v, name):
    r, c = w.shape
    tr = _tile(r, 128)

    def body(p_ref, w_ref, m_ref, v_ref, g_ref, d_ref, nm_ref, nv_ref):
        g = p_ref[0].astype(_F32)
        for j in range(1, _NDEV):
            g = g + p_ref[j].astype(_F32)
        delta, nm, nv = _adamw_math(w_ref[...], g, m_ref[...], v_ref[...])
        g_ref[...] = g
        d_ref[...] = delta
        nm_ref[...] = nm
        nv_ref[...] = nv

    blk = pl.BlockSpec((tr, c), lambda i: (i, 0))
    return pl.pallas_call(
        body, grid=(r // tr,),
        in_specs=[pl.BlockSpec((_NDEV, tr, c), lambda i: (0, i, 0)), blk, blk, blk],
        out_specs=[blk] * 4, out_shape=[jax.ShapeDtypeStruct((r, c), _F32)] * 4,
        compiler_params=_cparams("parallel"), name=name,
    )(parts, w, m, v)


def _sum8(parts, name):
    _, r, c = parts.shape

    def body(p_ref, o_ref):
        g = p_ref[0]
        for j in range(1, _NDEV):
            g = g + p_ref[j]
        o_ref[...] = g

    return pl.pallas_call(
        body, grid=(1,), in_specs=[pl.BlockSpec((_NDEV, r, c), lambda i: (0, 0, 0))],
        out_specs=pl.BlockSpec((r, c), lambda i: (0, 0)), out_shape=jax.ShapeDtypeStruct((r, c), _F32),
        compiler_params=_cparams("arbitrary"), name=name,
    )(parts)


def _adamw_many(gs, ws, ms, vs, name):
    n = len(ws)

    def body(*refs):
        g_refs, w_refs, m_refs, v_refs = (refs[k * n:(k + 1) * n] for k in range(4))
        d_refs, nm_refs, nv_refs = (refs[(4 + k) * n:(5 + k) * n] for k in range(3))
        for k in range(n):
            delta, nm, nv = _adamw_math(w_refs[k][...], g_refs[k][...], m_refs[k][...], v_refs[k][...])
            d_refs[k][...] = delta
            nm_refs[k][...] = nm
            nv_refs[k][...] = nv

    def whole(arr):
        return pl.BlockSpec(arr.shape, lambda i, nd=arr.ndim: (0,) * nd)

    specs = [whole(w) for w in ws]
    res = pl.pallas_call(
        body, grid=(1,), in_specs=specs * 4, out_specs=specs * 3,
        out_shape=[jax.ShapeDtypeStruct(w.shape, _F32) for w in ws] * 3,
        compiler_params=_cparams("arbitrary"), name=name,
    )(*gs, *ws, *ms, *vs)
    return res[:n], res[n:2 * n], res[2 * n:]


def _pack(arrs):
    flat = jnp.concatenate([a.reshape(-1).astype(_F32) for a in arrs])
    pad = (-flat.shape[0]) % 1024
    return jnp.pad(flat, (0, pad)).reshape(-1, 128)


def _unpack(flat2d, shapes):
    flat = flat2d.reshape(-1)
    out, off = [], 0
    for sh in shapes:
        size = 1
        for dim in sh:
            size *= dim
        out.append(flat[off:off + size].reshape(sh))
        off += size
    return out


def _block_diag(w):
    h, hd, _ = w.shape
    eye = jnp.eye(h, dtype=w.dtype)
    return (eye[:, None, :, None] * w[:, :, None, :]).reshape(h * hd, h * hd)


def _diag_blocks(full, h):
    hd = full.shape[0] // h
    return jnp.stack([full[i * hd:(i + 1) * hd, i * hd:(i + 1) * hd] for i in range(h)])


def kernel(x, mem, mix_norm_g, w_in, lru_conv_w, lru_conv_b, lru_w_a, lru_b_a, lru_w_x, lru_b_x, lru_lambda, conf_conv_w, conf_conv_b, conf_ln_g, conf_ln_b, w_out, xa_norm_g, mem_norm_g, w_q, w_kv, w_o, ffn_norm_g, w_up, ffn_conv_w, ffn_conv_b, w_down, final_norm_g, loss_target, m_mix_norm_g, m_w_in, m_lru_conv_w, m_lru_conv_b, m_lru_w_a, m_lru_b_a, m_lru_w_x, m_lru_b_x, m_lru_lambda, m_conf_conv_w, m_conf_conv_b, m_conf_ln_g, m_conf_ln_b, m_w_out, m_xa_norm_g, m_mem_norm_g, m_w_q, m_w_kv, m_w_o, m_ffn_norm_g, m_w_up, m_ffn_conv_w, m_ffn_conv_b, m_w_down, m_final_norm_g, v_mix_norm_g, v_w_in, v_lru_conv_w, v_lru_conv_b, v_lru_w_a, v_lru_b_a, v_lru_w_x, v_lru_b_x, v_lru_lambda, v_conf_conv_w, v_conf_conv_b, v_conf_ln_g, v_conf_ln_b, v_w_out, v_xa_norm_g, v_mem_norm_g, v_w_q, v_w_kv, v_w_o, v_ffn_norm_g, v_w_up, v_ffn_conv_w, v_ffn_conv_b, v_w_down, v_final_norm_g):
    names = ["mix_norm_g", "w_in", "lru_conv_w", "lru_conv_b", "lru_w_a", "lru_b_a", "lru_w_x", "lru_b_x",
             "lru_lambda", "conf_conv_w", "conf_conv_b", "conf_ln_g", "conf_ln_b", "w_out", "xa_norm_g",
             "mem_norm_g", "w_q", "w_kv", "w_o", "ffn_norm_g", "w_up", "ffn_conv_w", "ffn_conv_b", "w_down",
             "final_norm_g"]
    loc = locals()
    W = {n: loc[n] for n in names}
    M = {n: loc["m_" + n] for n in names}
    V = {n: loc["v_" + n] for n in names}
    big = ["w_in", "w_out", "w_q", "w_kv", "w_o", "w_up", "w_down"]
    conv_sharded = ["lru_conv_w", "conf_conv_w", "ffn_conv_w"]

    xs = x[0]
    mems = mem[0]
    tgt = loss_target[0]
    me = 4 * lax.axis_index("x") + 2 * lax.axis_index("y") + lax.axis_index("c")

    conv_shapes = [W[n].shape[1:] for n in conv_sharded]
    conv_pack = _pack([W[n][0] for n in conv_sharded])
    shard = {n: W[n][0].astype(_XFER) for n in big}
    h1, (g_in, g_out, g_conv) = _rms_fwd(
        xs, mix_norm_g, "rms1_fwd", comm=_Gather([shard["w_in"], shard["w_out"], conv_pack]))
    convs = [[] for _ in conv_sharded]
    for j in range(_NDEV):
        for idx, part in enumerate(_unpack(g_conv[j], conv_shapes)):
            convs[idx].append(part)
    lcw, ccw, fcw = [jnp.concatenate(parts, axis=-1) for parts in convs]

    wab = jnp.concatenate([_block_diag(lru_w_a[0]), _block_diag(lru_w_x[0])], axis=1).astype(_MXU)
    mixer_params = (lcw, lru_conv_b, wab, lru_b_a, lru_b_x, lru_lambda, ccw, conf_conv_b, conf_ln_g, conf_ln_b)

    w_out_f = g_out.reshape(-1, g_out.shape[-1])
    (z, ycat, hs, cc, x1, h2), (g_q, g_kv, g_o, g_up, g_down) = _mixer_fwd(
        xs, h1, g_in, w_out_f, xa_norm_g, *mixer_params, "mixer_fwd",
        comm=_Gather([shard[n] for n in ("w_q", "w_kv", "w_o", "w_up", "w_down")]))
    w_q_f = g_q.reshape(-1, g_q.shape[-1])
    w_o_f = g_o.reshape(-1, g_o.shape[-1])
    w_down_f = g_down.reshape(-1, g_down.shape[-1])
    row32, row16, vec32 = (_F32, "row"), (_MXU, "row"), (_F32, "vec")
    mn = _rms_fwd(mems, mem_norm_g, "rmsm_fwd")
    kv = _mm_nn_stacked(mn, g_kv, _MXU, "mm_kv_fwd")
    q, o, x2, h3 = _attn_fwd(h2, x1, w_q_f, kv, w_o_f, ffn_norm_g, "attn_fwd")

    gfin = final_norm_g.reshape(1, -1)
    g0, act, gelu_g, u_dgelu, dx3, dx3b, lvec, dg_final = _ffn_fused_fwd(
        h3, g_up, w_down_f, fcw, ffn_conv_b, x2, tgt, gfin, "ffn_fwd")

    def rows8(p):
        return p.reshape(_NDEV, p.shape[0] // _NDEV, p.shape[1])

    p_down = _mm_tn_nat(act, dx3b, _XFER, "mm_down_wgrad", ts=2048)
    (dgu, dx2, dx2b, dg_ffn, dfcw, dfcb), _ = _ffn_fused_bwd(
        dx3b, g0, gelu_g, u_dgelu, w_down_f, g_up, fcw, x2, ffn_norm_g, dx3, "ffn_bwd")
    p_up = _mm_tn_stacked(h3, dgu, _NDEV, _XFER, "mm_up_wgrad", slabs=1, ts=4096)

    (dx1, dx1b, dg_xa, dk, dv, p_o, p_q), (r_up,) = _attn_bwd(
        dx2b, dx2, q, o, h2, x1, kv, w_o_f, w_q_f, xa_norm_g, "attn_bwd", comm=_Exchange([p_up]))
    dkv = jnp.concatenate([dk, dv], axis=1).astype(_MXU)
    dmn = _mm_nt_stacked(dkv, g_kv, "mm_kv_dgrad", outs=[row32], slabs=_NDEV)
    p_kv = _mm_tn_stacked(mn, dkv, _NDEV, _XFER, "mm_kv_wgrad", slabs=_NDEV)
    _, _, dg_mem = _rms_bwd(dmn, mems, mem_norm_g, None, "rmsm_bwd")

    p_out = _mm_tn_nat(ycat, dx1b, _XFER, "mm_out_wgrad", ts=2048)
    ((dz, p_in, dlcw, dlcb, dwab, dba, dbx, dlam, dccw, dccb, dlng, dlnb),
     (r_down, r_o, r_q, r_kv, r_out)) = _mixer_bwd(
        dx1b, w_out_f, h1, z, hs, cc, *mixer_params, "mixer_bwd",
        comm=_Exchange([rows8(p_down), rows8(p_o), rows8(p_q), p_kv, rows8(p_out)]))

    c = _D_LRU
    heads = lru_w_a.shape[1]
    small_partial = {
        "lru_conv_w": dlcw, "lru_conv_b": dlcb,
        "lru_w_a": _diag_blocks(dwab[:, :c], heads), "lru_b_a": dba,
        "lru_w_x": _diag_blocks(dwab[:, c:], heads), "lru_b_x": dbx, "lru_lambda": dlam,
        "conf_conv_w": dccw, "conf_conv_b": dccb, "conf_ln_g": dlng, "conf_ln_b": dlnb,
        "xa_norm_g": dg_xa, "mem_norm_g": dg_mem, "ffn_norm_g": dg_ffn,
        "ffn_conv_w": dfcw, "ffn_conv_b": dfcb, "final_norm_g": dg_final,
    }
    early = list(small_partial)
    early_shapes = [small_partial[n].shape for n in early] + [lvec.shape]
    (grad_x, dg_mix), (r_in, early_all, mix_all) = _mm_nt_stacked(
        dz, g_in, "mm_in_dgrad", epi=_epi_rms_bwd, extra=[(xs, "row"), (mix_norm_g, "vec"), (dx1, "row")],
        outs=[row32, vec32], tm=512, slabs=_NDEV,
        comm=_Several([_Exchange([p_in]), _Gather([_pack([small_partial[n] for n in early] + [lvec])]),
                       _GatherOfResult(1, mix_norm_g.shape, _F32)]))
    small = early + ["mix_norm_g"]
    small_sum = _unpack(_sum8(early_all, "sum_small_grads"), early_shapes)
    loss = 0.5 * jnp.sum(small_sum.pop()) / xs.shape[1]
    small_sum.append(_sum8(mix_all.reshape(_NDEV, 8, -1), "sum_mix_grad").reshape(dg_mix.shape))
    received = {"w_in": r_in, "w_out": r_out, "w_q": r_q, "w_kv": r_kv, "w_o": r_o, "w_up": r_up,
                "w_down": r_down}

    grads, deltas, new_m, new_v = {}, {}, {}, {}
    for n, rec in ((n, received[n]) for n in big):
        shp = W[n].shape
        w2, m2, v2 = (t.reshape(shp[1:]) for t in (W[n], M[n], V[n]))
        outs = _sum_adamw(rec, w2, m2, v2, "adamw_" + n)
        grads[n], deltas[n], new_m[n], new_v[n] = (t.reshape(shp) for t in outs)

    small_g = []
    for n, g in zip(small, small_sum):
        if n in conv_sharded:
            width = W[n].shape[-1]
            g = lax.dynamic_slice_in_dim(g, me * width, width, axis=1)
        small_g.append(g.reshape(W[n].shape))

    def at_least_2d(a):
        return a.reshape(1, -1) if a.ndim == 1 else a

    sd, sm, sv = _adamw_many([at_least_2d(g) for g in small_g], [at_least_2d(W[n]) for n in small],
                             [at_least_2d(M[n]) for n in small], [at_least_2d(V[n]) for n in small], "adamw_small")
    for n, g, d_, m_, v_ in zip(small, small_g, sd, sm, sv):
        shp = W[n].shape
        grads[n], deltas[n], new_m[n], new_v[n] = g, d_.reshape(shp), m_.reshape(shp), v_.reshape(shp)

    return (loss, grad_x[None], *[grads[n] for n in names], *[deltas[n] for n in names],
            *[new_m[n] for n in names], *[new_v[n] for n in names])
```

```python
import functools

import jax
import jax.numpy as jnp
from jax import lax
from jax.experimental import pallas as pl
from jax.experimental.pallas import tpu as pltpu

_MXU = jnp.bfloat16
_XFER = jnp.bfloat16
_F32 = jnp.float32
_EPS = 1e-6
_NDEV = 8
_VMEM_LIMIT = 48 * 1024 * 1024

_D_LRU = 512
_XA_HEADS = 4
_RG_C = 8.0
_ADAM_LR, _ADAM_B1, _ADAM_B2, _ADAM_EPS, _ADAM_WD, _ADAM_STEP = 0.001, 0.9, 0.999, 1e-08, 0.01, 10

_MESH_ID = pl.DeviceIdType.MESH
_ANY = pl.BlockSpec(memory_space=pl.ANY)


def _cparams(*sem, vmem=_VMEM_LIMIT):
    return pltpu.CompilerParams(dimension_semantics=tuple(sem), vmem_limit_bytes=vmem)


def _pcall(body, *, args, grid, in_specs, out_specs, out_shape, sem, name, scratch_shapes=(), comm=None,
           vmem=_VMEM_LIMIT):
    outs_l = list(out_shape) if isinstance(out_shape, (list, tuple)) else [out_shape]
    ospecs_l = list(out_specs) if isinstance(out_specs, (list, tuple)) else [out_specs]
    n_in, n_out, n_scr = len(args), len(outs_l), len(scratch_shapes)
    if comm is None:
        res = pl.pallas_call(
            body, grid=grid, in_specs=list(in_specs), out_specs=ospecs_l, out_shape=outs_l,
            scratch_shapes=list(scratch_shapes), compiler_params=_cparams(*sem, vmem=vmem), name=name)(*args)
        return list(res), []
    n_ci, n_co = len(comm.ins), len(comm.outs)

    def wrapped(*refs):
        ins, cins = refs[:n_in], refs[n_in:n_in + n_ci]
        o = n_in + n_ci
        outs, couts = refs[o:o + n_out], refs[o + n_out:o + n_out + n_co]
        s = o + n_out + n_co
        scr, cscr = refs[s:s + n_scr], refs[s + n_scr:]
        first = pl.program_id(0) == 0
        last = pl.program_id(0) == grid[0] - 1
        for ax in range(1, len(grid)):
            first = jnp.logical_and(first, pl.program_id(ax) == 0)
            last = jnp.logical_and(last, pl.program_id(ax) == grid[ax] - 1)

        @pl.when(first)
        def _():
            comm.start(cins, couts, cscr)

        body(*ins, *outs, *scr)

        @pl.when(last)
        def _():
            comm.finish(cins, couts, cscr, outs)

    res = pl.pallas_call(
        wrapped, grid=grid, in_specs=list(in_specs) + [_ANY] * n_ci, out_specs=ospecs_l + [_ANY] * n_co,
        out_shape=outs_l + list(comm.outs), scratch_shapes=list(scratch_shapes) + list(comm.scratch),
        compiler_params=_cparams(*(("arbitrary",) * len(grid)), vmem=vmem), name=name)(*args, *comm.ins)
    return list(res[:n_out]), list(res[n_out:])


def _sigmoid(v):
    return 1.0 / (1.0 + jnp.exp(-v))


_GELU_C = 0.7978845608028654
_GELU_K = 0.044715


def _gelu(v):
    t = jnp.tanh(_GELU_C * (v + _GELU_K * v * v * v))
    return 0.5 * v * (1.0 + t)


def _gelu_and_grad(v):
    v2 = v * v
    s = 0.5 * jnp.tanh(v * (_GELU_C + (_GELU_C * _GELU_K) * v2)) + 0.5
    g = v * s
    dg = s + (g * (1.0 - s)) * ((2.0 * _GELU_C) + (6.0 * _GELU_C * _GELU_K) * v2)
    return g, dg


def _softplus(v):
    e = jnp.exp(-jnp.abs(v))
    log1p = jnp.where(e < 1e-2, e * (1.0 - e * (0.5 - e * (1.0 / 3.0))), jnp.log(1.0 + e))
    return jnp.maximum(v, 0.0) + log1p


def _neg_expm1(v):
    series = -v * (1.0 + v * (0.5 + v * ((1.0 / 6.0) + v * (1.0 / 24.0))))
    return jnp.where(v > -0.0625, series, 1.0 - jnp.exp(v))


def _dot(a, b, dims):
    return lax.dot_general(a.astype(_MXU), b.astype(_MXU), (dims, ((), ())), preferred_element_type=_F32)


_NN = ((1,), (0,))
_NT = ((1,), (1,))
_TN = ((0,), (0,))


def _scan_fwd(a, b, rows):
    n = a.shape[0]
    d = 1
    while d < n:
        keep = rows >= d
        b = jnp.where(keep, b + a * pltpu.roll(b, d, 0), b)
        a = jnp.where(keep, a * pltpu.roll(a, d, 0), a)
        d *= 2
    return a, b


def _scan_rev(a, b, rows):
    n = a.shape[0]
    d = 1
    while d < n:
        keep = rows < n - d
        b = jnp.where(keep, b + a * pltpu.roll(b, n - d, 0), b)
        a = jnp.where(keep, a * pltpu.roll(a, n - d, 0), a)
        d *= 2
    return a, b


def _colsum(v):
    return jnp.sum(v, axis=0, keepdims=True)


def _mm(a, b, *, dims, grid, a_spec, b_spec, outs, acc_shape, name, extra=(), epi=None, slabs=1, comm=None):
    nred = grid[-1]
    red_axis = len(grid) - 1
    n_ex, n_out = len(extra), len(outs)
    epi = _epi_store if epi is None else epi

    def body(*refs):
        a_ref, b_ref = refs[:2]
        ex, o_refs, acc_ref = refs[2:2 + n_ex], refs[2 + n_ex:2 + n_ex + n_out], refs[-1]
        if slabs == 1:
            p = _dot(a_ref[...], b_ref[...], dims)
        else:
            n = b_ref.shape[-1]
            p = _dot(a_ref[:, 0:n], b_ref[0], dims)
            for jj in range(1, slabs):
                p = p + _dot(a_ref[:, jj * n:(jj + 1) * n], b_ref[jj], dims)

        first_rows = pl.program_id(0) == 0
        if nred == 1:
            epi(p, ex, o_refs, first_rows)
        else:
            k = pl.program_id(red_axis)

            @pl.when(k == 0)
            def _():
                acc_ref[...] = p

            @pl.when(jnp.logical_and(k > 0, k < nred - 1))
            def _():
                acc_ref[...] += p

            @pl.when(k == nred - 1)
            def _():
                epi(acc_ref[...] + p, ex, o_refs, first_rows)

    sem = ("parallel",) * (len(grid) - 1) + ("arbitrary",)
    if any(o[0].shape[0] == 1 for o in outs):
        sem = ("arbitrary",) * len(grid)
    res, cres = _pcall(
        body, args=(a, b) + tuple(e[0] for e in extra), grid=grid,
        in_specs=[a_spec, b_spec] + [e[1] for e in extra],
        out_specs=[o[1] for o in outs], out_shape=[o[0] for o in outs],
        scratch_shapes=[pltpu.VMEM(acc_shape if nred > 1 else (8, 128), _F32)], sem=sem, name=name, comm=comm)
    res = res[0] if n_out == 1 else res
    return res if comm is None else (res, cres)


def _epi_store(total, ex, outs, first_rows):
    outs[0][...] = total.astype(outs[0].dtype)


def _epi_residual_rms(total, ex, outs, first_rows):
    res_ref, g_ref = ex
    xn = total + res_ref[...]
    outs[0][...] = xn
    r = lax.rsqrt(jnp.mean(xn * xn, axis=-1, keepdims=True) + _EPS)
    outs[1][...] = (xn * r * g_ref[...]).astype(outs[1].dtype)


def _epi_rms_bwd(total, ex, outs, first_rows):
    x_ref, g_ref, dres_ref = ex
    dg_ref = outs[-1]
    xv = x_ref[...]
    r = lax.rsqrt(jnp.mean(xv * xv, axis=-1, keepdims=True) + _EPS)
    xhat = xv * r
    dxh = total * g_ref[...]
    dx = dres_ref[...] + r * (dxh - xhat * jnp.mean(dxh * xhat, axis=-1, keepdims=True))
    for o_ref in outs[:-1]:
        o_ref[...] = dx.astype(o_ref.dtype)

    @pl.when(first_rows)
    def _():
        dg_ref[...] = jnp.zeros_like(dg_ref)

    dg_ref[...] += _colsum(total * xhat)


def _epi_final(total, ex, outs, first_rows):
    res_ref, t_ref, g_ref = ex
    dx_ref, dxb_ref, l_ref, dg_ref = outs
    xv = total + res_ref[...]
    gv = g_ref[...]
    d = xv.shape[-1]
    r = lax.rsqrt(jnp.mean(xv * xv, axis=-1, keepdims=True) + _EPS)
    xhat = xv * r
    err = xhat * gv - t_ref[...]
    dy = err * (1.0 / d)
    dxh = dy * gv
    dx = r * (dxh - xhat * jnp.mean(dxh * xhat, axis=-1, keepdims=True))
    dx_ref[...] = dx
    dxb_ref[...] = dx.astype(dxb_ref.dtype)

    @pl.when(first_rows)
    def _():
        l_ref[...] = jnp.zeros_like(l_ref)
        dg_ref[...] = jnp.zeros_like(dg_ref)

    l_ref[...] += _colsum(err * err)
    dg_ref[...] += _colsum(dy * xhat)


def _tile(m, cap):
    t = min(m, cap)
    assert m % t == 0
    return t


def _row_spec(tm, n):
    return pl.BlockSpec((tm, n), lambda i, *_: (i, 0))


def _vec_spec(n):
    return pl.BlockSpec((1, n), lambda *_: (0, 0))


def _row_io(m, n, tm, extra, outs):
    def spec(kind):
        return _row_spec(tm, n) if kind == "row" else _vec_spec(n)

    ex = [(arr, spec(kind)) for arr, kind in extra]
    os_ = [(jax.ShapeDtypeStruct((m, n) if kind == "row" else (1, n), dt), spec(kind)) for dt, kind in outs]
    return ex, os_


def _mm_nn_stacked(a, w, out_dtype, name, comm=None, tm=1024):
    m, k = a.shape
    j, _, n = w.shape
    tm = _tile(m, tm)
    return _mm(a, w, dims=_NN, grid=(m // tm, j, 1),
               a_spec=pl.BlockSpec((tm, k), lambda i, jj, r: (i, 0)),
               b_spec=pl.BlockSpec((None, k, n), lambda i, jj, r: (jj, 0, 0)),
               outs=[(jax.ShapeDtypeStruct((m, j * n), out_dtype), pl.BlockSpec((tm, n), lambda i, jj, r: (i, jj)))],
               acc_shape=(tm, n), name=name, comm=comm)


def _mm_nt_stacked(dc, w, name, *, outs, extra=(), epi=None, comm=None, tm=1024, slabs=1):
    m = dc.shape[0]
    j, k, n = w.shape
    tm = _tile(m, tm)
    assert j % slabs == 0
    ex, os_ = _row_io(m, k, tm, extra, outs)
    wblk = (None, k, n) if slabs == 1 else (slabs, k, n)
    return _mm(dc, w, dims=_NT, grid=(m // tm, j // slabs),
               a_spec=pl.BlockSpec((tm, slabs * n), lambda i, r: (i, r)),
               b_spec=pl.BlockSpec(wblk, lambda i, r: (r, 0, 0)),
               outs=os_, extra=ex, epi=epi, acc_shape=(tm, k), name=name, slabs=slabs, comm=comm)


def _mm_tn_stacked(a, dc, j, out_dtype, name, slabs=1, ts=1024):
    s, k = a.shape
    n = dc.shape[1] // j
    ts = _tile(s, ts)
    assert j % slabs == 0

    def epi(total, ex, outs, first_rows):
        for jj in range(slabs):
            outs[0][jj] = total[:, jj * n:(jj + 1) * n].astype(outs[0].dtype)

    return _mm(a, dc, dims=_TN, grid=(j // slabs, s // ts),
               a_spec=pl.BlockSpec((ts, k), lambda jj, r: (r, 0)),
               b_spec=pl.BlockSpec((ts, slabs * n), lambda jj, r: (r, jj)),
               outs=[(jax.ShapeDtypeStruct((j, k, n), out_dtype),
                      pl.BlockSpec((slabs, k, n), lambda jj, r: (jj, 0, 0)))],
               epi=epi, acc_shape=(k, slabs * n), name=name)


def _mm_tn_nat(a, dc, out_dtype, name, ts=1024):
    s, kt = a.shape
    n = dc.shape[1]
    ts = _tile(s, ts)
    tkb = _tile(kt, 1024)
    return _mm(a, dc, dims=_TN, grid=(kt // tkb, s // ts),
               a_spec=pl.BlockSpec((ts, tkb), lambda kb, r: (r, kb)),
               b_spec=pl.BlockSpec((ts, n), lambda kb, r: (r, 0)),
               outs=[(jax.ShapeDtypeStruct((kt, n), out_dtype), pl.BlockSpec((tkb, n), lambda kb, r: (kb, 0)))],
               acc_shape=(tkb, n), name=name)


def _rms_fwd(x, g, name, comm=None):
    s, d = x.shape
    t = _tile(s, 1024)

    def body(x_ref, g_ref, h_ref):
        xv = x_ref[...]
        r = lax.rsqrt(jnp.mean(xv * xv, axis=-1, keepdims=True) + _EPS)
        h_ref[...] = (xv * r * g_ref[...]).astype(h_ref.dtype)

    res, cres = _pcall(
        body, args=(x, g), grid=(s // t,),
        in_specs=[pl.BlockSpec((t, d), lambda i: (i, 0)), pl.BlockSpec((1, d), lambda i: (0, 0))],
        out_specs=pl.BlockSpec((t, d), lambda i: (i, 0)),
        out_shape=jax.ShapeDtypeStruct((s, d), _MXU), sem=("parallel",), name=name, comm=comm)
    return res[0] if comm is None else (res[0], cres)


def _rms_bwd(dh, x, g, dres, name):
    s, d = x.shape
    t = _tile(s, 256)
    has_res = dres is not None

    def body(*refs):
        if has_res:
            dh_ref, x_ref, g_ref, dres_ref, dx_ref, dxb_ref, dg_ref = refs
        else:
            dh_ref, x_ref, g_ref, dx_ref, dxb_ref, dg_ref = refs
        xv = x_ref[...]
        dhv = dh_ref[...]
        r = lax.rsqrt(jnp.mean(xv * xv, axis=-1, keepdims=True) + _EPS)
        xhat = xv * r
        dxh = dhv * g_ref[...]
        dx = r * (dxh - xhat * jnp.mean(dxh * xhat, axis=-1, keepdims=True))
        if has_res:
            dx = dx + dres_ref[...]
        dx_ref[...] = dx
        dxb_ref[...] = dx.astype(dxb_ref.dtype)

        @pl.when(pl.program_id(0) == 0)
        def _():
            dg_ref[...] = jnp.zeros_like(dg_ref)

        dg_ref[...] += _colsum(dhv * xhat)

    row = pl.BlockSpec((t, d), lambda i: (i, 0))
    vec = pl.BlockSpec((1, d), lambda i: (0, 0))
    in_specs = [row, row, vec] + ([row] if has_res else [])
    args = (dh, x, g) + ((dres,) if has_res else ())
    return pl.pallas_call(
        body, grid=(s // t,), in_specs=in_specs, out_specs=[row, row, vec],
        out_shape=[jax.ShapeDtypeStruct((s, d), _F32), jax.ShapeDtypeStruct((s, d), _MXU),
                   jax.ShapeDtypeStruct((1, d), _F32)],
        compiler_params=_cparams("arbitrary"), name=name,
    )(*args)


_LRU_K = 4
_CONF_K = 31
_LRU_HALO = 8
_CONF_HALO = 32
_MIX_T = 512


def _lru_gates(lx, wab_ref, ba_ref, bx_ref, lam_ref):
    c = _D_LRU
    pre = _dot(lx, wab_ref[...], _NN)
    r = _sigmoid(pre[:, :c] + ba_ref[...])
    ig = _sigmoid(pre[:, c:] + bx_ref[...])
    sp = _softplus(-lam_ref[...])
    log_a = (-_RG_C) * r * sp
    a = jnp.exp(log_a)
    mult = jnp.sqrt(_neg_expm1(2.0 * log_a))
    return r, ig, sp, a, mult


def _causal_conv(ext_ref, halo, w_ref, b_ref, taps, t):
    acc = b_ref[...] + w_ref[0:1, :] * ext_ref[pl.ds(halo - (taps - 1), t), :]
    for k in range(1, taps):
        acc = acc + w_ref[k:k + 1, :] * ext_ref[pl.ds(halo - (taps - 1) + k, t), :]
    return acc


class _Windows:
    def __init__(self, ext_ref, shifted_ref, t):
        self.ext_ref, self.shifted_ref, self.t = ext_ref, shifted_ref, t
        rows = t + 24
        for r in range(1, 8):
            shifted_ref[r - 1, 0:rows, :] = ext_ref[pl.ds(r, rows), :]

    def __call__(self, off):
        q, r = divmod(off, 8)
        if r == 0:
            return self.ext_ref[pl.ds(8 * q, self.t), :]
        return self.shifted_ref[r - 1, pl.ds(8 * q, self.t), :]


def _mixer_fwd(xs, h1, w_in, w_out, gnorm, lcw, lcb, wab, ba, bx, lam, ccw, ccb, lng, lnb, name, comm=None):
    s, d = xs.shape
    nblk, _, n = w_in.shape
    c = _D_LRU
    t = _tile(s, _MIX_T)
    nt = s // t

    def body(x_ref, h_ref, win_ref, wout_ref, gn_ref,
             lcw_ref, lcb_ref, wab_ref, ba_ref, bx_ref, lam_ref, ccw_ref, ccb_ref, lng_ref, lnb_ref,
             z_ref, ycat_ref, hs_ref, cc_ref, x1_ref, h2_ref,
             ext_ref, cge_ref, hc_ref, shifted_ref, zprev_ref):
        i = pl.program_id(0)
        first = i == 0
        rows = lax.broadcasted_iota(jnp.int32, (t, c), 0)

        @pl.when(first)
        def _():
            zprev_ref[...] = jnp.zeros_like(zprev_ref)

        hv = h_ref[...]
        for j in range(nblk):
            z_ref[:, j * n:(j + 1) * n] = _dot(hv, win_ref[j], _NN)
        lx0_ref, gate_ref = z_ref.at[:, 0:c], z_ref.at[:, c:2 * c]
        ca_ref, cb_ref = z_ref.at[:, 2 * c:3 * c], z_ref.at[:, 3 * c:4 * c]
        lx0h_ref = zprev_ref.at[_CONF_HALO - _LRU_HALO:_CONF_HALO, 0:c]
        cah_ref, cbh_ref = zprev_ref.at[:, 2 * c:3 * c], zprev_ref.at[:, 3 * c:4 * c]

        ext_ref[0:_LRU_HALO, :] = jnp.where(first, 0.0, lx0h_ref[...])
        ext_ref[_LRU_HALO:_LRU_HALO + t, :] = lx0_ref[...]
        lx = _causal_conv(ext_ref, _LRU_HALO, lcw_ref, lcb_ref, _LRU_K, t)
        r, ig, sp, a, mult = _lru_gates(lx, wab_ref, ba_ref, bx_ref, lam_ref)
        u = mult * (ig * lx)
        a_cum, h_loc = _scan_fwd(a, u, rows)

        @pl.when(first)
        def _():
            hc_ref[...] = jnp.zeros_like(hc_ref)

        h = h_loc + a_cum * hc_ref[7:8, :]
        hs_ref[...] = h
        hc_ref[...] = hs_ref[pl.ds(t - 8, 8), :]
        ycat_ref[:, 0:c] = (h * _gelu(gate_ref[...])).astype(ycat_ref.dtype)

        cge_ref[0:_CONF_HALO, :] = jnp.where(first, 0.0, cah_ref[...] * _sigmoid(cbh_ref[...]))
        cge_ref[_CONF_HALO:_CONF_HALO + t, :] = ca_ref[...] * _sigmoid(cb_ref[...])
        win = _Windows(cge_ref, shifted_ref, t)
        first_off = _CONF_HALO - (_CONF_K - 1)
        cc = ccb_ref[...] + ccw_ref[0:1, :] * win(first_off)
        for k in range(1, _CONF_K):
            cc = cc + ccw_ref[k:k + 1, :] * win(first_off + k)
        cc_ref[...] = cc
        xc = cc - jnp.mean(cc, axis=-1, keepdims=True)
        rstd = lax.rsqrt(jnp.mean(xc * xc, axis=-1, keepdims=True) + _EPS)
        ln = xc * rstd * lng_ref[...] + lnb_ref[...]
        ycat_ref[:, c:2 * c] = (ln * _sigmoid(ln)).astype(ycat_ref.dtype)

        zprev_ref[...] = z_ref[pl.ds(t - _CONF_HALO, _CONF_HALO), :]
        y = _dot(ycat_ref[...], wout_ref[...], _NN)
        _epi_residual_rms(y, (x_ref, gn_ref), (x1_ref, h2_ref), first)

    def const(arr):
        return pl.BlockSpec(arr.shape, lambda i: (0,) * arr.ndim, pipeline_mode=pl.Buffered(1))

    def rows_of(width):
        return pl.BlockSpec((t, width), lambda i: (i, 0))

    params = (lcw, lcb, wab, ba, bx, lam, ccw, ccb, lng, lnb)
    res, cres = _pcall(
        body, args=(xs, h1, w_in, w_out, gnorm, *params), grid=(nt,),
        in_specs=[rows_of(d), rows_of(d), const(w_in), const(w_out), const(gnorm)] + [const(p) for p in params],
        out_specs=[rows_of(nblk * n), rows_of(2 * c), rows_of(c), rows_of(c), rows_of(d), rows_of(d)],
        out_shape=[jax.ShapeDtypeStruct((s, nblk * n), _F32), jax.ShapeDtypeStruct((s, 2 * c), _MXU),
                   jax.ShapeDtypeStruct((s, c), _F32), jax.ShapeDtypeStruct((s, c), _F32),
                   jax.ShapeDtypeStruct((s, d), _F32), jax.ShapeDtypeStruct((s, d), _MXU)],
        scratch_shapes=[pltpu.VMEM((t + _LRU_HALO, c), _F32), pltpu.VMEM((t + _CONF_HALO, c), _F32),
                        pltpu.VMEM((8, c), _F32), pltpu.VMEM((7, t + _CONF_HALO, c), _F32),
                        pltpu.VMEM((_CONF_HALO, nblk * n), _F32)],
        sem=("arbitrary",), name=name, comm=comm)
    return res, cres


def _mixer_bwd(dx1b, w_out, h1, z, hs, cc, lcw, lcb, wab, ba, bx, lam, ccw, ccb, lng, lnb, name, comm=None):
    s = z.shape[0]
    d = h1.shape[1]
    c = _D_LRU
    t = _tile(s, _MIX_T)
    nt = s // t
    nblk = z.shape[1] // 256

    def body(dxb_ref, wout_ref, h1_ref, lx0_ref, lx0h_ref, gate_ref, ca_ref, cah_ref, cb_ref, cbh_ref,
             hs_ref, hsh_ref, cc_ref,
             lcw_ref, lcb_ref, wab_ref, ba_ref, bx_ref, lam_ref, ccw_ref, ccb_ref, lng_ref, lnb_ref,
             dz_ref, pin_ref, dlcw_ref, dlcb_ref, dwab_ref, dba_ref, dbx_ref, dlam_ref, dccw_ref, dccb_ref, dlng_ref,
             dlnb_ref,
             ext_ref, up_ref, cge_ref, dce_ref, xc_ref, dlxc_ref, dccc_ref, shifted_ref, dwin_ref):
        i = pl.program_id(0)
        first_tile = i == nt - 1
        last_tile = i == 0
        rows = lax.broadcasted_iota(jnp.int32, (t, c), 0)

        @pl.when(last_tile)
        def _():
            for ref in (dlcw_ref, dlcb_ref, dwab_ref, dba_ref, dbx_ref, dlam_ref, dccw_ref, dccb_ref, dlng_ref,
                        dlnb_ref, xc_ref, dlxc_ref, dccc_ref, dwin_ref):
                ref[...] = jnp.zeros_like(ref)

        dycat = _dot(dxb_ref[...], wout_ref[...], _NT)

        ext_ref[0:_LRU_HALO, :] = jnp.where(first_tile, 0.0, lx0h_ref[...])
        ext_ref[_LRU_HALO:_LRU_HALO + t, :] = lx0_ref[...]
        lx = _causal_conv(ext_ref, _LRU_HALO, lcw_ref, lcb_ref, _LRU_K, t)
        r, ig, sp, a, mult = _lru_gates(lx, wab_ref, ba_ref, bx_ref, lam_ref)
        h = hs_ref[...]
        gl, dgl = _gelu_and_grad(gate_ref[...])
        dyl = dycat[:, 0:c]
        dz_ref[:, c:2 * c] = (dyl * h * dgl).astype(dz_ref.dtype)
        dh = dyl * gl

        up_ref[0:t, :] = a
        up_ref[t:t + 8, :] = jnp.ones((8, c), _F32)
        a_up = up_ref[pl.ds(1, t), :]
        a_cum, g_loc = _scan_rev(a_up, dh, rows)
        gt = g_loc + a_cum * xc_ref[0:1, :]
        xc_ref[...] = (a * gt)[0:8, :]

        up_ref[0:8, :] = jnp.where(first_tile, 0.0, hsh_ref[...])
        up_ref[8:8 + t, :] = h
        hprev = up_ref[pl.ds(7, t), :]

        da = gt * hprev
        dmult = gt * ig * lx
        dig = gt * mult * lx
        dlx = gt * mult * ig
        dlog_a = da * a - dmult * a * a / mult
        dpre_r = dlog_a * (-_RG_C) * sp * r * (1.0 - r)
        dpre_i = dig * ig * (1.0 - ig)
        dlam_ref[...] += _colsum(dlog_a * r) * (_RG_C * _sigmoid(-lam_ref[...]))
        dba_ref[...] += _colsum(dpre_r)
        dbx_ref[...] += _colsum(dpre_i)
        dpre = jnp.concatenate([dpre_r, dpre_i], axis=1).astype(_MXU)
        dlx = dlx + _dot(dpre, wab_ref[...], _NT)
        dwab_ref[...] += _dot(lx, dpre, _TN)

        dlcb_ref[...] += _colsum(dlx)
        up_ref[0:t, :] = dlx
        up_ref[t:t + 8, :] = dlxc_ref[...]
        dlxc_ref[...] = dlx[0:8, :]
        acc = lcw_ref[0:1, :] * up_ref[pl.ds(_LRU_K - 1, t), :]
        for k in range(1, _LRU_K):
            acc = acc + lcw_ref[k:k + 1, :] * up_ref[pl.ds(_LRU_K - 1 - k, t), :]
        dz_ref[:, 0:c] = acc.astype(dz_ref.dtype)
        for k in range(_LRU_K):
            dlcw_ref[k:k + 1, :] += _colsum(dlx * ext_ref[pl.ds(_LRU_HALO - (_LRU_K - 1) + k, t), :])

        sig_b = _sigmoid(cb_ref[...])
        ca = ca_ref[...]
        cge_ref[0:_CONF_HALO, :] = jnp.where(first_tile, 0.0, cah_ref[...] * _sigmoid(cbh_ref[...]))
        cge_ref[_CONF_HALO:_CONF_HALO + t, :] = ca * sig_b
        ccv = cc_ref[...]
        xcen = ccv - jnp.mean(ccv, axis=-1, keepdims=True)
        rstd = lax.rsqrt(jnp.mean(xcen * xcen, axis=-1, keepdims=True) + _EPS)
        xn = xcen * rstd
        ln = xn * lng_ref[...] + lnb_ref[...]
        sg = _sigmoid(ln)
        dln = dycat[:, c:2 * c] * (sg * (1.0 + ln * (1.0 - sg)))
        dlng_ref[...] += _colsum(dln * xn)
        dlnb_ref[...] += _colsum(dln)
        dxn = dln * lng_ref[...]
        dcc = rstd * (dxn - jnp.mean(dxn, axis=-1, keepdims=True)
                      - xn * jnp.mean(dxn * xn, axis=-1, keepdims=True))
        dccb_ref[...] += _colsum(dcc)
        win = _Windows(cge_ref, shifted_ref, t)
        for k in range(_CONF_K):
            dccw_ref[k:k + 1, :] += _colsum(dcc * win(_CONF_HALO - (_CONF_K - 1) + k))
        dce_ref[0:t, :] = dcc
        dce_ref[t:t + _CONF_HALO, :] = dccc_ref[...]
        dccc_ref[...] = dcc[0:_CONF_HALO, :]
        win = _Windows(dce_ref, shifted_ref, t)
        dcg = ccw_ref[0:1, :] * win(_CONF_K - 1)
        for k in range(1, _CONF_K):
            dcg = dcg + ccw_ref[k:k + 1, :] * win(_CONF_K - 1 - k)
        dz_ref[:, 2 * c:3 * c] = (dcg * sig_b).astype(dz_ref.dtype)
        dz_ref[:, 3 * c:4 * c] = (dcg * ca * sig_b * (1.0 - sig_b)).astype(dz_ref.dtype)

        dwin_ref[...] += _dot(h1_ref[...], dz_ref[...], _TN)

        @pl.when(first_tile)
        def _():
            for j in range(nblk):
                pin_ref[j] = dwin_ref[:, j * 256:(j + 1) * 256].astype(pin_ref.dtype)

    def col(j):
        return pl.BlockSpec((t, c), lambda i: (nt - 1 - i, j))

    def halo(j, rows_):
        per = t // rows_
        return pl.BlockSpec((rows_, c), lambda i: (jnp.maximum((nt - 1 - i) * per - 1, 0), j))

    def full(shape):
        return pl.BlockSpec(shape, lambda i: (0,) * len(shape))

    params = (lcw, lcb, wab, ba, bx, lam, ccw, ccb, lng, lnb)
    small = [(_LRU_K, c), (1, c), (c, 2 * c), (1, c), (1, c), (1, c), (_CONF_K, c), (1, c), (1, c), (1, c)]
    wide = pl.BlockSpec((t, d), lambda i: (nt - 1 - i, 0))
    pin_shape = (nblk, d, 256)
    return _pcall(
        body, args=(dx1b, w_out, h1, z, z, z, z, z, z, z, hs, hs, cc, *params), grid=(nt,),
        in_specs=[wide, pl.BlockSpec(w_out.shape, lambda i: (0, 0), pipeline_mode=pl.Buffered(1)), wide,
                  col(0), halo(0, _LRU_HALO), col(1), col(2), halo(2, _CONF_HALO), col(3), halo(3, _CONF_HALO),
                  col(0), halo(0, 8), col(0)]
        + [full(p.shape) for p in params],
        out_specs=[pl.BlockSpec((t, 4 * c), lambda i: (nt - 1 - i, 0)), full(pin_shape)] + [full(sh) for sh in small],
        out_shape=[jax.ShapeDtypeStruct((s, 4 * c), _MXU), jax.ShapeDtypeStruct(pin_shape, _XFER)]
        + [jax.ShapeDtypeStruct(sh, _F32) for sh in small],
        scratch_shapes=[pltpu.VMEM((t + _LRU_HALO, c), _F32), pltpu.VMEM((t + 8, c), _F32),
                        pltpu.VMEM((t + _CONF_HALO, c), _F32), pltpu.VMEM((t + _CONF_HALO, c), _F32),
                        pltpu.VMEM((8, c), _F32), pltpu.VMEM((8, c), _F32), pltpu.VMEM((_CONF_HALO, c), _F32),
                        pltpu.VMEM((7, t + _CONF_HALO, c), _F32), pltpu.VMEM((d, 4 * c), _F32)],
        sem=("arbitrary",), name=name, comm=comm, vmem=_VMEM_LIMIT_FUSED)


_ATT_T = 512


def _attn_probs(qh, kh, scale):
    sc = _dot(qh, kh, _NT) * scale
    e = jnp.exp(sc - jnp.max(sc, axis=-1, keepdims=True))
    return e / jnp.sum(e, axis=-1, keepdims=True)


def _const_spec(arr):
    return pl.BlockSpec(arr.shape, lambda i: (0,) * arr.ndim, pipeline_mode=pl.Buffered(1))


def _attn_fwd(h2, x1, w_q, kv, w_o, gnorm, name):
    s, d = h2.shape
    nm = kv.shape[0]
    hd = d // _XA_HEADS
    t = _tile(s, _ATT_T)
    scale = hd ** -0.5

    def body(h_ref, x1_ref, wq_ref, k_ref, v_ref, wo_ref, g_ref, q_ref, o_ref, x2_ref, h3_ref):
        q_ref[...] = _dot(h_ref[...], wq_ref[...], _NN).astype(q_ref.dtype)
        for hh in range(_XA_HEADS):
            sl = slice(hh * hd, (hh + 1) * hd)
            p = _attn_probs(q_ref[:, sl], k_ref[:, sl], scale)
            o_ref[:, sl] = _dot(p, v_ref[:, sl], _NN).astype(o_ref.dtype)
        y = _dot(o_ref[...], wo_ref[...], _NN)
        _epi_residual_rms(y, (x1_ref, g_ref), (x2_ref, h3_ref), None)

    row = pl.BlockSpec((t, d), lambda i: (i, 0))
    half = pl.BlockSpec((nm, d), lambda i: (0, 0), pipeline_mode=pl.Buffered(1))
    half2 = pl.BlockSpec((nm, d), lambda i: (0, 1), pipeline_mode=pl.Buffered(1))
    return pl.pallas_call(
        body, grid=(s // t,),
        in_specs=[row, row, _const_spec(w_q), half, half2, _const_spec(w_o), _const_spec(gnorm)],
        out_specs=[row, row, row, row],
        out_shape=[jax.ShapeDtypeStruct((s, d), _MXU), jax.ShapeDtypeStruct((s, d), _MXU),
                   jax.ShapeDtypeStruct((s, d), _F32), jax.ShapeDtypeStruct((s, d), _MXU)],
        compiler_params=_cparams("parallel"), name=name,
    )(h2, x1, w_q, kv, kv, w_o, gnorm)


def _attn_bwd(dx2b, dx2, q, o, h2, x1, kv, w_o, w_q, gnorm, name, comm=None):
    s, d = q.shape
    nm = kv.shape[0]
    hd = d // _XA_HEADS
    t = _tile(s, _ATT_T)
    nt = s // t
    scale = hd ** -0.5

    def body(dxb_ref, dx2_ref, q_ref, o_ref, h_ref, x1_ref, k_ref, v_ref, wo_ref, wq_ref, g_ref,
             dx1_ref, dx1b_ref, dgn_ref, dk_ref, dv_ref, pwo_ref, pwq_ref, dq_ref, awo_ref, awq_ref):
        i = pl.program_id(0)
        first = i == 0

        @pl.when(first)
        def _():
            for ref in (dk_ref, dv_ref, awo_ref, awq_ref):
                ref[...] = jnp.zeros_like(ref)

        dxb = dxb_ref[...]
        do = _dot(dxb, wo_ref[...], _NT).astype(_MXU)
        awo_ref[...] += _dot(o_ref[...], dxb, _TN)
        for hh in range(_XA_HEADS):
            sl = slice(hh * hd, (hh + 1) * hd)
            qh = q_ref[:, sl]
            kh = k_ref[:, sl]
            doh = do[:, sl]
            p = _attn_probs(qh, kh, scale)
            dp = _dot(doh, v_ref[:, sl], _NT)
            dv_ref[:, sl] += _dot(p, doh, _TN)
            ds = (p * (dp - jnp.sum(dp * p, axis=-1, keepdims=True)) * scale).astype(_MXU)
            dq_ref[:, sl] = _dot(ds, kh, _NN).astype(dq_ref.dtype)
            dk_ref[:, sl] += _dot(ds, qh, _TN)
        dq = dq_ref[...]
        awq_ref[...] += _dot(h_ref[...], dq, _TN)
        dh = _dot(dq, wq_ref[...], _NT)
        _epi_rms_bwd(dh, (x1_ref, g_ref, dx2_ref), (dx1_ref, dx1b_ref, dgn_ref), first)

        @pl.when(i == nt - 1)
        def _():
            pwo_ref[...] = awo_ref[...].astype(pwo_ref.dtype)
            pwq_ref[...] = awq_ref[...].astype(pwq_ref.dtype)

    row = pl.BlockSpec((t, d), lambda i: (i, 0))
    vec = pl.BlockSpec((1, d), lambda i: (0, 0))
    mem_blk = pl.BlockSpec((nm, d), lambda i: (0, 0))
    sq = pl.BlockSpec((d, d), lambda i: (0, 0))
    half = pl.BlockSpec((nm, d), lambda i: (0, 0), pipeline_mode=pl.Buffered(1))
    half2 = pl.BlockSpec((nm, d), lambda i: (0, 1), pipeline_mode=pl.Buffered(1))
    return _pcall(
        body, args=(dx2b, dx2, q, o, h2, x1, kv, kv, w_o, w_q, gnorm), grid=(nt,),
        in_specs=[row, row, row, row, row, row, half, half2, _const_spec(w_o), _const_spec(w_q), _const_spec(gnorm)],
        out_specs=[row, row, vec, mem_blk, mem_blk, sq, sq],
        out_shape=[jax.ShapeDtypeStruct((s, d), _F32), jax.ShapeDtypeStruct((s, d), _MXU),
                   jax.ShapeDtypeStruct((1, d), _F32), jax.ShapeDtypeStruct((nm, d), _F32),
                   jax.ShapeDtypeStruct((nm, d), _F32), jax.ShapeDtypeStruct((d, d), _XFER),
                   jax.ShapeDtypeStruct((d, d), _XFER)],
        scratch_shapes=[pltpu.VMEM((t, d), _MXU), pltpu.VMEM((d, d), _F32), pltpu.VMEM((d, d), _F32)],
        sem=("arbitrary",), name=name, comm=comm, vmem=_VMEM_LIMIT_FUSED)


_FFN_K = 3
_FFN_FUSED_T = 256
_VMEM_LIMIT_FUSED = 58 * 1024 * 1024


def _ffn_fused_fwd(h3, w_up, w_down, fcw, fcb, x2, target, gfin, name):
    s, d = h3.shape
    nblk, _, n = w_up.shape
    half = nblk // 2
    f = half * n
    t = _tile(s, _FFN_FUSED_T)

    def body(h_ref, wup_ref, wdown_ref, w_ref, b_ref, x2_ref, t_ref, g_ref,
             g0_ref, act_ref, gl_ref, udgl_ref, dx_ref, dxb_ref, l_ref, dg_ref, ext0_ref, ext1_ref, halo_ref):
        i = pl.program_id(0)
        first = i == 0
        h = h_ref[...]
        total = None
        ahead = (_dot(h, wup_ref[0], _NN), _dot(h, wup_ref[half], _NN))
        for j in range(half):
            cs = slice(j * n, (j + 1) * n)
            ext_ref = ext0_ref if j % 2 == 0 else ext1_ref
            g0, u = ahead
            if j + 1 < half:
                ahead = (_dot(h, wup_ref[j + 1], _NN), _dot(h, wup_ref[half + j + 1], _NN))
            if j > 0:
                prev = slice((j - 1) * n, j * n)
                p = _dot(act_ref[:, prev], wdown_ref[prev, :], _NN)
                total = p if total is None else total + p
            g0_ref[:, cs] = g0.astype(g0_ref.dtype)
            ext_ref[0:8, :] = jnp.where(first, 0.0, halo_ref[:, cs])
            ext_ref[8:8 + t, :] = g0
            halo_ref[:, cs] = g0[t - 8:t, :]
            g = _causal_conv(ext_ref, 8, w_ref.at[:, cs], b_ref.at[:, cs], _FFN_K, t)
            gl, dgl = _gelu_and_grad(g)
            gl_ref[:, cs] = gl.astype(gl_ref.dtype)
            udgl_ref[:, cs] = (u * dgl).astype(udgl_ref.dtype)
            act_ref[:, cs] = (gl * u).astype(act_ref.dtype)
        last = slice((half - 1) * n, half * n)
        total = total + _dot(act_ref[:, last], wdown_ref[last, :], _NN)
        _epi_final(total, (x2_ref, t_ref, g_ref), (dx_ref, dxb_ref, l_ref, dg_ref), first)

    def const(shape):
        return pl.BlockSpec(shape, lambda i: (0,) * len(shape), pipeline_mode=pl.Buffered(1))

    row = pl.BlockSpec((t, d), lambda i: (i, 0))
    vec = pl.BlockSpec((1, d), lambda i: (0, 0))
    return pl.pallas_call(
        body, grid=(s // t,),
        in_specs=[row, const(w_up.shape), const(w_down.shape), const(fcw.shape), const(fcb.shape), row, row, vec],
        out_specs=[pl.BlockSpec((t, f), lambda i: (i, 0))] * 4 + [row, row, vec, vec],
        out_shape=[jax.ShapeDtypeStruct((s, f), _MXU)] * 4
        + [jax.ShapeDtypeStruct((s, d), _F32), jax.ShapeDtypeStruct((s, d), _MXU),
                   jax.ShapeDtypeStruct((1, d), _F32), jax.ShapeDtypeStruct((1, d), _F32)],
        scratch_shapes=[pltpu.VMEM((t + 8, n), _F32), pltpu.VMEM((t + 8, n), _F32), pltpu.VMEM((8, f), _F32)],
        compiler_params=_cparams("arbitrary", vmem=_VMEM_LIMIT_FUSED), name=name,
    )(h3, w_up, w_down, fcw, fcb, x2, target, gfin)


def _ffn_fused_bwd(dx3b, g0, gl, udgl, w_down, w_up, fcw, x2, gnorm, dx3, name, comm=None):
    s, d = x2.shape
    nblk, _, n = w_up.shape
    half = nblk // 2
    f = half * n
    t = _tile(s, _FFN_FUSED_T)
    nt = s // t
    hrows = 16

    def body(dxb_ref, g0_ref, g0h_ref, gl_ref, udgl_ref, wdown_ref, wup_ref, w_ref, x2_ref, g_ref, dx3_ref,
             dgu_ref, dx2_ref, dx2b_ref, dgn_ref, dw_ref, db_ref, ext0_ref, ext1_ref, up0_ref, up1_ref, car_ref):
        i = pl.program_id(0)
        first_tile = i == nt - 1
        last_tile = i == 0

        @pl.when(last_tile)
        def _():
            dw_ref[...] = jnp.zeros_like(dw_ref)
            db_ref[...] = jnp.zeros_like(db_ref)
            car_ref[...] = jnp.zeros_like(car_ref)

        dxb = dxb_ref[...]
        total = None
        for j in range(half):
            cs = slice(j * n, (j + 1) * n)
            us = slice(f + j * n, f + (j + 1) * n)
            ext_ref = ext0_ref if j % 2 == 0 else ext1_ref
            up_ref = up0_ref if j % 2 == 0 else up1_ref
            dact = _dot(dxb, wdown_ref[cs, :], _NT)
            ext_ref[0:8, :] = jnp.where(first_tile, 0.0, g0h_ref[:, cs].astype(_F32)[hrows - 8:hrows])
            ext_ref[8:8 + t, :] = g0_ref[:, cs].astype(_F32)
            du = (dact * gl_ref[:, cs].astype(_F32)).astype(dgu_ref.dtype)
            dgu_ref[:, us] = du
            dg = dact * udgl_ref[:, cs].astype(_F32)
            db_ref[:, cs] += _colsum(dg)
            for k in range(_FFN_K):
                dw_ref[k:k + 1, cs] += _colsum(dg * ext_ref[pl.ds(8 - (_FFN_K - 1) + k, t), :])
            up_ref[0:t, :] = dg
            up_ref[t:t + 8, :] = car_ref[:, cs]
            car_ref[:, cs] = dg[0:8, :]
            dg0 = w_ref[0:1, cs] * up_ref[pl.ds(_FFN_K - 1, t), :]
            for k in range(1, _FFN_K):
                dg0 = dg0 + w_ref[k:k + 1, cs] * up_ref[pl.ds(_FFN_K - 1 - k, t), :]
            dg0 = dg0.astype(dgu_ref.dtype)
            dgu_ref[:, cs] = dg0
            p = _dot(dg0, wup_ref[j], _NT) + _dot(du, wup_ref[half + j], _NT)
            total = p if total is None else total + p
        _epi_rms_bwd(total, (x2_ref, g_ref, dx3_ref), (dx2_ref, dx2b_ref, dgn_ref), last_tile)

    def const(shape):
        return pl.BlockSpec(shape, lambda i: (0,) * len(shape), pipeline_mode=pl.Buffered(1))

    row = pl.BlockSpec((t, d), lambda i: (nt - 1 - i, 0))
    vec = pl.BlockSpec((1, d), lambda i: (0, 0))
    per = t // hrows
    wide = pl.BlockSpec((t, f), lambda i: (nt - 1 - i, 0))
    return _pcall(
        body, args=(dx3b, g0, g0, gl, udgl, w_down, w_up, fcw, x2, gnorm, dx3), grid=(nt,),
        in_specs=[row, wide, pl.BlockSpec((hrows, f), lambda i: (jnp.maximum((nt - 1 - i) * per - 1, 0), 0)),
                  wide, wide, const(w_down.shape), const(w_up.shape), const(fcw.shape), row, vec, row],
        out_specs=[pl.BlockSpec((t, 2 * f), lambda i: (nt - 1 - i, 0)), row, row, vec,
                   pl.BlockSpec((_FFN_K, f), lambda i: (0, 0)), pl.BlockSpec((1, f), lambda i: (0, 0))],
        out_shape=[jax.ShapeDtypeStruct((s, 2 * f), _MXU), jax.ShapeDtypeStruct((s, d), _F32),
                   jax.ShapeDtypeStruct((s, d), _MXU), jax.ShapeDtypeStruct((1, d), _F32),
                   jax.ShapeDtypeStruct((_FFN_K, f), _F32), jax.ShapeDtypeStruct((1, f), _F32)],
        scratch_shapes=[pltpu.VMEM((t + 8, n), _F32), pltpu.VMEM((t + 8, n), _F32),
                        pltpu.VMEM((t + 8, n), _F32), pltpu.VMEM((t + 8, n), _F32), pltpu.VMEM((8, f), _F32)],
        sem=("arbitrary",), name=name, comm=comm, vmem=_VMEM_LIMIT_FUSED)


def _mesh_pos():
    return lax.axis_index("x"), lax.axis_index("y"), lax.axis_index("c")


def _flip(v, bit):
    return 1 - v if bit else v


def _sem_scratch(n):
    return [pltpu.SemaphoreType.DMA((7 * n,)), pltpu.SemaphoreType.DMA((7 * n,)), pltpu.SemaphoreType.DMA((n,))]


class _Gather:
    def __init__(self, xs):
        self.ins = list(xs)
        self.outs = [jax.ShapeDtypeStruct((_NDEV,) + v.shape, v.dtype) for v in xs]
        self.scratch = _sem_scratch(len(xs))

    def _plan(self, x_refs, out_refs, sems):
        send_sems, recv_sems, local_sems = sems
        x, y, c = _mesh_pos()
        me, sibling = (x, y, c), (x, y, 1 - c)
        chips = [(1 - x, y), (x, 1 - y), (1 - x, 1 - y)]

        def copy(a, k, block, to, src=None):
            slot = out_refs[a].at[4 * block[0] + 2 * block[1] + block[2]]
            return pltpu.make_async_remote_copy(
                src_ref=slot if src is None else src, dst_ref=slot,
                send_sem=send_sems.at[a * 7 + k], recv_sem=recv_sems.at[a * 7 + k],
                device_id=to, device_id_type=_MESH_ID)

        def own(a):
            return pltpu.make_async_copy(x_refs[a], out_refs[a].at[4 * x + 2 * y + c], local_sems.at[a])

        def first(a):
            return [copy(a, 0, me, sibling, src=x_refs[a])] + [
                copy(a, 1 + j, me, (*chip, c), src=x_refs[a]) for j, chip in enumerate(chips)]

        return me, sibling, chips, c, copy, own, first

    def start(self, x_refs, out_refs, sems):
        _, _, _, _, _, own, first = self._plan(x_refs, out_refs, sems)
        for a in range(len(self.ins)):
            own(a).start()
            for cp in first(a):
                cp.start()

    def finish(self, x_refs, out_refs, sems, results=None):
        me, sibling, chips, c, copy, own, first = self._plan(x_refs, out_refs, sems)
        n = len(self.ins)
        passed = []
        for a in range(n):
            for j, chip in enumerate(chips):
                copy(a, 1 + j, (*chip, c), me).wait_recv()
                fwd = copy(a, 4 + j, (*chip, c), sibling)
                fwd.start()
                passed.append(fwd)
        for a in range(n):
            copy(a, 0, sibling, me).wait_recv()
            for j, chip in enumerate(chips):
                copy(a, 4 + j, (*chip, 1 - c), me).wait_recv()
        for a in range(n):
            for cp in first(a):
                cp.wait_send()
        for cp in passed:
            cp.wait_send()
        for a in range(n):
            own(a).wait()


class _Exchange:
    def __init__(self, gs):
        self.ins = list(gs)
        self.outs = [jax.ShapeDtypeStruct(v.shape, v.dtype) for v in gs]
        self.scratch = _sem_scratch(len(gs))

    def _plan(self, g_refs, r_refs, sems):
        send_sems, recv_sems, local_sems = sems
        x, y, c = _mesh_pos()
        me_idx = 4 * x + 2 * y + c
        n = len(self.ins)

        def copy(a, k):
            peer = (_flip(x, k & 4), _flip(y, k & 2), _flip(c, k & 1))
            peer_idx = 4 * peer[0] + 2 * peer[1] + peer[2]
            return pltpu.make_async_remote_copy(
                src_ref=g_refs[a].at[peer_idx], dst_ref=r_refs[a].at[me_idx],
                send_sem=send_sems.at[a * 7 + k - 1], recv_sem=recv_sems.at[a * 7 + k - 1],
                device_id=peer, device_id_type=_MESH_ID)

        copies = [copy(a, k) for a in range(n) for k in range(1, _NDEV)]
        mine = [pltpu.make_async_copy(g_refs[a].at[me_idx], r_refs[a].at[me_idx], local_sems.at[a])
                for a in range(n)]
        return copies, mine

    def start(self, g_refs, r_refs, sems):
        copies, mine = self._plan(g_refs, r_refs, sems)
        for cp in copies + mine:
            cp.start()

    def finish(self, g_refs, r_refs, sems, results=None):
        copies, mine = self._plan(g_refs, r_refs, sems)
        for cp in copies:
            cp.wait_recv()
        for cp in copies:
            cp.wait_send()
        for cp in mine:
            cp.wait()


class _GatherDirect:
    def __init__(self, xs):
        self.ins = list(xs)
        self.outs = [jax.ShapeDtypeStruct((_NDEV,) + v.shape, v.dtype) for v in xs]
        self.scratch = _sem_scratch(len(xs))

    def _plan(self, x_refs, out_refs, sems):
        send_sems, recv_sems, local_sems = sems
        x, y, c = _mesh_pos()
        me_idx = 4 * x + 2 * y + c
        copies, mine = [], []
        for a in range(len(self.ins)):
            mine.append(pltpu.make_async_copy(x_refs[a], out_refs[a].at[me_idx], local_sems.at[a]))
            for k in range(1, _NDEV):
                peer = (_flip(x, k & 4), _flip(y, k & 2), _flip(c, k & 1))
                copies.append(pltpu.make_async_remote_copy(
                    src_ref=x_refs[a], dst_ref=out_refs[a].at[me_idx],
                    send_sem=send_sems.at[a * 7 + k - 1], recv_sem=recv_sems.at[a * 7 + k - 1],
                    device_id=peer, device_id_type=_MESH_ID))
        return copies, mine

    def start(self, x_refs, out_refs, sems):
        copies, mine = self._plan(x_refs, out_refs, sems)
        for cp in copies + mine:
            cp.start()

    def finish(self, x_refs, out_refs, sems, results=None):
        copies, mine = self._plan(x_refs, out_refs, sems)
        for cp in copies:
            cp.wait_recv()
        for cp in copies:
            cp.wait_send()
        for cp in mine:
            cp.wait()


class _GatherOfResult:
    def __init__(self, index, shape, dtype):
        self.index = index
        self.ins = []
        self.outs = [jax.ShapeDtypeStruct((_NDEV,) + tuple(shape), dtype)]
        self.scratch = _sem_scratch(1)

    def start(self, ins, outs, sems):
        pass

    def finish(self, ins, outs, sems, results):
        send_sems, recv_sems, local_sems = sems
        src = results[self.index]
        x, y, c = _mesh_pos()
        me_idx = 4 * x + 2 * y + c
        mine = pltpu.make_async_copy(src, outs[0].at[me_idx], local_sems.at[0])
        copies = []
        for k in range(1, _NDEV):
            peer = (_flip(x, k & 4), _flip(y, k & 2), _flip(c, k & 1))
            copies.append(pltpu.make_async_remote_copy(
                src_ref=src, dst_ref=outs[0].at[me_idx], send_sem=send_sems.at[k - 1], recv_sem=recv_sems.at[k - 1],
                device_id=peer, device_id_type=_MESH_ID))
        for cp in copies + [mine]:
            cp.start()
        for cp in copies:
            cp.wait_recv()
        for cp in copies:
            cp.wait_send()
        mine.wait()


class _Several:
    def __init__(self, parts):
        self.parts = list(parts)
        self.ins = [a for p in self.parts for a in p.ins]
        self.outs = [a for p in self.parts for a in p.outs]
        self.scratch = [a for p in self.parts for a in p.scratch]

    def _split(self, ins, outs, sems):
        i = o = s = 0
        for p in self.parts:
            ni, no, ns = len(p.ins), len(p.outs), len(p.scratch)
            yield p, ins[i:i + ni], outs[o:o + no], sems[s:s + ns]
            i, o, s = i + ni, o + no, s + ns

    def start(self, ins, outs, sems):
        for part, i, o, s in self._split(ins, outs, sems):
            part.start(i, o, s)

    def finish(self, ins, outs, sems, results):
        for part, i, o, s in self._split(ins, outs, sems):
            part.finish(i, o, s, results)


def _adamw_math(w, g, m, v):
    m = _ADAM_B1 * m + (1.0 - _ADAM_B1) * g
    v = _ADAM_B2 * v + (1.0 - _ADAM_B2) * (g * g)
    m_hat = m / (1.0 - _ADAM_B1 ** _ADAM_STEP)
    v_hat = v / (1.0 - _ADAM_B2 ** _ADAM_STEP)
    delta = -_ADAM_LR * (m_hat / (jnp.sqrt(v_hat) + _ADAM_EPS) + _ADAM_WD * w)
    return delta, m, v


def _sum_adamw(parts, w, m, v, name):
    r, c = w.shape
    tr = _tile(r, 128)

    def body(p_ref, w_ref, m_ref, v_ref, g_ref, d_ref, nm_ref, nv_ref):
        g = p_ref[0].astype(_F32)
        for j in range(1, _NDEV):
            g = g + p_ref[j].astype(_F32)
        delta, nm, nv = _adamw_math(w_ref[...], g, m_ref[...], v_ref[...])
        g_ref[...] = g
        d_ref[...] = delta
        nm_ref[...] = nm
        nv_ref[...] = nv

    blk = pl.BlockSpec((tr, c), lambda i: (i, 0))
    return pl.pallas_call(
        body, grid=(r // tr,),
        in_specs=[pl.BlockSpec((_NDEV, tr, c), lambda i: (0, i, 0)), blk, blk, blk],
        out_specs=[blk] * 4, out_shape=[jax.ShapeDtypeStruct((r, c), _F32)] * 4,
        compiler_params=_cparams("parallel"), name=name,
    )(parts, w, m, v)


def _sum8(parts, name):
    _, r, c = parts.shape

    def body(p_ref, o_ref):
        g = p_ref[0]
        for j in range(1, _NDEV):
            g = g + p_ref[j]
        o_ref[...] = g

    return pl.pallas_call(
        body, grid=(1,), in_specs=[pl.BlockSpec((_NDEV, r, c), lambda i: (0, 0, 0))],
        out_specs=pl.BlockSpec((r, c), lambda i: (0, 0)), out_shape=jax.ShapeDtypeStruct((r, c), _F32),
        compiler_params=_cparams("arbitrary"), name=name,
    )(parts)


def _adamw_many(gs, ws, ms, vs, name):
    n = len(ws)

    def body(*refs):
        g_refs, w_refs, m_refs, v_refs = (refs[k * n:(k + 1) * n] for k in range(4))
        d_refs, nm_refs, nv_refs = (refs[(4 + k) * n:(5 + k) * n] for k in range(3))
        for k in range(n):
            delta, nm, nv = _adamw_math(w_refs[k][...], g_refs[k][...], m_refs[k][...], v_refs[k][...])
            d_refs[k][...] = delta
            nm_refs[k][...] = nm
            nv_refs[k][...] = nv

    def whole(arr):
        return pl.BlockSpec(arr.shape, lambda i, nd=arr.ndim: (0,) * nd)

    specs = [whole(w) for w in ws]
    res = pl.pallas_call(
        body, grid=(1,), in_specs=specs * 4, out_specs=specs * 3,
        out_shape=[jax.ShapeDtypeStruct(w.shape, _F32) for w in ws] * 3,
        compiler_params=_cparams("arbitrary"), name=name,
    )(*gs, *ws, *ms, *vs)
    return res[:n], res[n:2 * n], res[2 * n:]


def _pack(arrs):
    flat = jnp.concatenate([a.reshape(-1).astype(_F32) for a in arrs])
    pad = (-flat.shape[0]) % 1024
    return jnp.pad(flat, (0, pad)).reshape(-1, 128)


def _unpack(flat2d, shapes):
    flat = flat2d.reshape(-1)
    out, off = [], 0
    for sh in shapes:
        size = 1
        for dim in sh:
            size *= dim
        out.append(flat[off:off + size].reshape(sh))
        off += size
    return out


def _block_diag(w):
    h, hd, _ = w.shape
    eye = jnp.eye(h, dtype=w.dtype)
    return (eye[:, None, :, None] * w[:, :, None, :]).reshape(h * hd, h * hd)


def _diag_blocks(full, h):
    hd = full.shape[0] // h
    return jnp.stack([full[i * hd:(i + 1) * hd, i * hd:(i + 1) * hd] for i in range(h)])


def kernel(x, mem, mix_norm_g, w_in, lru_conv_w, lru_conv_b, lru_w_a, lru_b_a, lru_w_x, lru_b_x, lru_lambda, conf_conv_w, conf_conv_b, conf_ln_g, conf_ln_b, w_out, xa_norm_g, mem_norm_g, w_q, w_kv, w_o, ffn_norm_g, w_up, ffn_conv_w, ffn_conv_b, w_down, final_norm_g, loss_target, m_mix_norm_g, m_w_in, m_lru_conv_w, m_lru_conv_b, m_lru_w_a, m_lru_b_a, m_lru_w_x, m_lru_b_x, m_lru_lambda, m_conf_conv_w, m_conf_conv_b, m_conf_ln_g, m_conf_ln_b, m_w_out, m_xa_norm_g, m_mem_norm_g, m_w_q, m_w_kv, m_w_o, m_ffn_norm_g, m_w_up, m_ffn_conv_w, m_ffn_conv_b, m_w_down, m_final_norm_g, v_mix_norm_g, v_w_in, v_lru_conv_w, v_lru_conv_b, v_lru_w_a, v_lru_b_a, v_lru_w_x, v_lru_b_x, v_lru_lambda, v_conf_conv_w, v_conf_conv_b, v_conf_ln_g, v_conf_ln_b, v_w_out, v_xa_norm_g, v_mem_norm_g, v_w_q, v_w_kv, v_w_o, v_ffn_norm_g, v_w_up, v_ffn_conv_w, v_ffn_conv_b, v_w_down, v_final_norm_g):
    names = ["mix_norm_g", "w_in", "lru_conv_w", "lru_conv_b", "lru_w_a", "lru_b_a", "lru_w_x", "lru_b_x",
             "lru_lambda", "conf_conv_w", "conf_conv_b", "conf_ln_g", "conf_ln_b", "w_out", "xa_norm_g",
             "mem_norm_g", "w_q", "w_kv", "w_o", "ffn_norm_g", "w_up", "ffn_conv_w", "ffn_conv_b", "w_down",
             "final_norm_g"]
    loc = locals()
    W = {n: loc[n] for n in names}
    M = {n: loc["m_" + n] for n in names}
    V = {n: loc["v_" + n] for n in names}
    big = ["w_in", "w_out", "w_q", "w_kv", "w_o", "w_up", "w_down"]
    conv_sharded = ["lru_conv_w", "conf_conv_w", "ffn_conv_w"]

    xs = x[0]
    mems = mem[0]
    tgt = loss_target[0]
    me = 4 * lax.axis_index("x") + 2 * lax.axis_index("y") + lax.axis_index("c")

    conv_shapes = [W[n].shape[1:] for n in conv_sharded]
    conv_pack = _pack([W[n][0] for n in conv_sharded])
    shard = {n: W[n][0].astype(_XFER) for n in big}
    h1, (g_in, g_out, g_conv) = _rms_fwd(
        xs, mix_norm_g, "rms1_fwd", comm=_Gather([shard["w_in"], shard["w_out"], conv_pack]))
    convs = [[] for _ in conv_sharded]
    for j in range(_NDEV):
        for idx, part in enumerate(_unpack(g_conv[j], conv_shapes)):
            convs[idx].append(part)
    lcw, ccw, fcw = [jnp.concatenate(parts, axis=-1) for parts in convs]

    wab = jnp.concatenate([_block_diag(lru_w_a[0]), _block_diag(lru_w_x[0])], axis=1).astype(_MXU)
    mixer_params = (lcw, lru_conv_b, wab, lru_b_a, lru_b_x, lru_lambda, ccw, conf_conv_b, conf_ln_g, conf_ln_b)

    w_out_f = g_out.reshape(-1, g_out.shape[-1])
    (z, ycat, hs, cc, x1, h2), (g_q, g_kv, g_o, g_up, g_down) = _mixer_fwd(
        xs, h1, g_in, w_out_f, xa_norm_g, *mixer_params, "mixer_fwd",
        comm=_Gather([shard[n] for n in ("w_q", "w_kv", "w_o", "w_up", "w_down")]))
    w_q_f = g_q.reshape(-1, g_q.shape[-1])
    w_o_f = g_o.reshape(-1, g_o.shape[-1])
    w_down_f = g_down.reshape(-1, g_down.shape[-1])
    row32, row16, vec32 = (_F32, "row"), (_MXU, "row"), (_F32, "vec")
    mn = _rms_fwd(mems, mem_norm_g, "rmsm_fwd")
    kv = _mm_nn_stacked(mn, g_kv, _MXU, "mm_kv_fwd")
    q, o, x2, h3 = _attn_fwd(h2, x1, w_q_f, kv, w_o_f, ffn_norm_g, "attn_fwd")

    gfin = final_norm_g.reshape(1, -1)
    g0, act, gelu_g, u_dgelu, dx3, dx3b, lvec, dg_final = _ffn_fused_fwd(
        h3, g_up, w_down_f, fcw, ffn_conv_b, x2, tgt, gfin, "ffn_fwd")

    def rows8(p):
        return p.reshape(_NDEV, p.shape[0] // _NDEV, p.shape[1])

    p_down = _mm_tn_nat(act, dx3b, _XFER, "mm_down_wgrad", ts=2048)
    (dgu, dx2, dx2b, dg_ffn, dfcw, dfcb), _ = _ffn_fused_bwd(
        dx3b, g0, gelu_g, u_dgelu, w_down_f, g_up, fcw, x2, ffn_norm_g, dx3, "ffn_bwd")
    p_up = _mm_tn_stacked(h3, dgu, _NDEV, _XFER, "mm_up_wgrad", slabs=1, ts=4096)

    (dx1, dx1b, dg_xa, dk, dv, p_o, p_q), (r_up,) = _attn_bwd(
        dx2b, dx2, q, o, h2, x1, kv, w_o_f, w_q_f, xa_norm_g, "attn_bwd", comm=_Exchange([p_up]))
    dkv = jnp.concatenate([dk, dv], axis=1).astype(_MXU)
    dmn = _mm_nt_stacked(dkv, g_kv, "mm_kv_dgrad", outs=[row32], slabs=_NDEV)
    p_kv = _mm_tn_stacked(mn, dkv, _NDEV, _XFER, "mm_kv_wgrad", slabs=_NDEV)
    _, _, dg_mem = _rms_bwd(dmn, mems, mem_norm_g, None, "rmsm_bwd")

    p_out = _mm_tn_nat(ycat, dx1b, _XFER, "mm_out_wgrad", ts=2048)
    ((dz, p_in, dlcw, dlcb, dwab, dba, dbx, dlam, dccw, dccb, dlng, dlnb),
     (r_down, r_o, r_q, r_kv, r_out)) = _mixer_bwd(
        dx1b, w_out_f, h1, z, hs, cc, *mixer_params, "mixer_bwd",
        comm=_Exchange([rows8(p_down), rows8(p_o), rows8(p_q), p_kv, rows8(p_out)]))

    c = _D_LRU
    heads = lru_w_a.shape[1]
    small_partial = {
        "lru_conv_w": dlcw, "lru_conv_b": dlcb,
        "lru_w_a": _diag_blocks(dwab[:, :c], heads), "lru_b_a": dba,
        "lru_w_x": _diag_blocks(dwab[:, c:], heads), "lru_b_x": dbx, "lru_lambda": dlam,
        "conf_conv_w": dccw, "conf_conv_b": dccb, "conf_ln_g": dlng, "conf_ln_b": dlnb,
        "xa_norm_g": dg_xa, "mem_norm_g": dg_mem, "ffn_norm_g": dg_ffn,
        "ffn_conv_w": dfcw, "ffn_conv_b": dfcb, "final_norm_g": dg_final,
    }
    early = list(small_partial)
    early_shapes = [small_partial[n].shape for n in early] + [lvec.shape]
    (grad_x, dg_mix), (r_in, early_all, mix_all) = _mm_nt_stacked(
        dz, g_in, "mm_in_dgrad", epi=_epi_rms_bwd, extra=[(xs, "row"), (mix_norm_g, "vec"), (dx1, "row")],
        outs=[row32, vec32], tm=512, slabs=_NDEV,
        comm=_Several([_Exchange([p_in]), _GatherDirect([_pack([small_partial[n] for n in early] + [lvec])]),
                       _GatherOfResult(1, mix_norm_g.shape, _F32)]))
    small = early + ["mix_norm_g"]
    small_sum = _unpack(_sum8(early_all, "sum_small_grads"), early_shapes)
    loss = 0.5 * jnp.sum(small_sum.pop()) / xs.shape[1]
    small_sum.append(_sum8(mix_all.reshape(_NDEV, 8, -1), "sum_mix_grad").reshape(dg_mix.shape))
    received = {"w_in": r_in, "w_out": r_out, "w_q": r_q, "w_kv": r_kv, "w_o": r_o, "w_up": r_up,
                "w_down": r_down}

    grads, deltas, new_m, new_v = {}, {}, {}, {}
    for n, rec in ((n, received[n]) for n in big):
        shp = W[n].shape
        w2, m2, v2 = (t.reshape(shp[1:]) for t in (W[n], M[n], V[n]))
        outs = _sum_adamw(rec, w2, m2, v2, "adamw_" + n)
        grads[n], deltas[n], new_m[n], new_v[n] = (t.reshape(shp) for t in outs)

    small_g = []
    for n, g in zip(small, small_sum):
        if n in conv_sharded:
            width = W[n].shape[-1]
            g = lax.dynamic_slice_in_dim(g, me * width, width, axis=1)
        small_g.append(g.reshape(W[n].shape))

    def at_least_2d(a):
        return a.reshape(1, -1) if a.ndim == 1 else a

    sd, sm, sv = _adamw_many([at_least_2d(g) for g in small_g], [at_least_2d(W[n]) for n in small],
                             [at_least_2d(M[n]) for n in small], [at_least_2d(V[n]) for n in small], "adamw_small")
    for n, g, d_, m_, v_ in zip(small, small_g, sd, sm, sv):
        shp = W[n].shape
        grads[n], deltas[n], new_m[n], new_v[n] = g, d_.reshape(shp), m_.reshape(shp), v_.reshape(shp)

    return (loss, grad_x[None], *[grads[n] for n in names], *[deltas[n] for n in names],
            *[new_m[n] for n in names], *[new_v[n] for n in names])
```

```python
import functools

import jax
import jax.numpy as jnp
from jax import lax
from jax.experimental import pallas as pl
from jax.experimental.pallas import tpu as pltpu

_MXU = jnp.bfloat16
_XFER = jnp.bfloat16
_F32 = jnp.float32
_EPS = 1e-6
_NDEV = 8
_VMEM_LIMIT = 48 * 1024 * 1024

_D_LRU = 512
_XA_HEADS = 4
_RG_C = 8.0
_ADAM_LR, _ADAM_B1, _ADAM_B2, _ADAM_EPS, _ADAM_WD, _ADAM_STEP = 0.001, 0.9, 0.999, 1e-08, 0.01, 10

_MESH_ID = pl.DeviceIdType.MESH
_ANY = pl.BlockSpec(memory_space=pl.ANY)


def _cparams(*sem, vmem=_VMEM_LIMIT):
    return pltpu.CompilerParams(dimension_semantics=tuple(sem), vmem_limit_bytes=vmem)


def _pcall(body, *, args, grid, in_specs, out_specs, out_shape, sem, name, scratch_shapes=(), comm=None,
           vmem=_VMEM_LIMIT):
    outs_l = list(out_shape) if isinstance(out_shape, (list, tuple)) else [out_shape]
    ospecs_l = list(out_specs) if isinstance(out_specs, (list, tuple)) else [out_specs]
    n_in, n_out, n_scr = len(args), len(outs_l), len(scratch_shapes)
    if comm is None:
        res = pl.pallas_call(
            body, grid=grid, in_specs=list(in_specs), out_specs=ospecs_l, out_shape=outs_l,
            scratch_shapes=list(scratch_shapes), compiler_params=_cparams(*sem, vmem=vmem), name=name)(*args)
        return list(res), []
    n_ci, n_co = len(comm.ins), len(comm.outs)

    def wrapped(*refs):
        ins, cins = refs[:n_in], refs[n_in:n_in + n_ci]
        o = n_in + n_ci
        outs, couts = refs[o:o + n_out], refs[o + n_out:o + n_out + n_co]
        s = o + n_out + n_co
        scr, cscr = refs[s:s + n_scr], refs[s + n_scr:]
        first = pl.program_id(0) == 0
        last = pl.program_id(0) == grid[0] - 1
        for ax in range(1, len(grid)):
            first = jnp.logical_and(first, pl.program_id(ax) == 0)
            last = jnp.logical_and(last, pl.program_id(ax) == grid[ax] - 1)

        @pl.when(first)
        def _():
            comm.start(cins, couts, cscr)

        body(*ins, *outs, *scr)

        @pl.when(last)
        def _():
            comm.finish(cins, couts, cscr, outs)

    res = pl.pallas_call(
        wrapped, grid=grid, in_specs=list(in_specs) + [_ANY] * n_ci, out_specs=ospecs_l + [_ANY] * n_co,
        out_shape=outs_l + list(comm.outs), scratch_shapes=list(scratch_shapes) + list(comm.scratch),
        compiler_params=_cparams(*(("arbitrary",) * len(grid)), vmem=vmem), name=name)(*args, *comm.ins)
    return list(res[:n_out]), list(res[n_out:])


def _sigmoid(v):
    return 1.0 / (1.0 + jnp.exp(-v))


_GELU_C = 0.7978845608028654
_GELU_K = 0.044715


def _gelu(v):
    t = jnp.tanh(_GELU_C * (v + _GELU_K * v * v * v))
    return 0.5 * v * (1.0 + t)


def _gelu_and_grad(v):
    v2 = v * v
    s = 0.5 * jnp.tanh(v * (_GELU_C + (_GELU_C * _GELU_K) * v2)) + 0.5
    g = v * s
    dg = s + (g * (1.0 - s)) * ((2.0 * _GELU_C) + (6.0 * _GELU_C * _GELU_K) * v2)
    return g, dg


def _softplus(v):
    e = jnp.exp(-jnp.abs(v))
    log1p = jnp.where(e < 1e-2, e * (1.0 - e * (0.5 - e * (1.0 / 3.0))), jnp.log(1.0 + e))
    return jnp.maximum(v, 0.0) + log1p


def _neg_expm1(v):
    series = -v * (1.0 + v * (0.5 + v * ((1.0 / 6.0) + v * (1.0 / 24.0))))
    return jnp.where(v > -0.0625, series, 1.0 - jnp.exp(v))


def _dot(a, b, dims):
    return lax.dot_general(a.astype(_MXU), b.astype(_MXU), (dims, ((), ())), preferred_element_type=_F32)


_NN = ((1,), (0,))
_NT = ((1,), (1,))
_TN = ((0,), (0,))


def _scan_fwd(a, b, rows):
    n = a.shape[0]
    d = 1
    while d < n:
        keep = rows >= d
        b = jnp.where(keep, b + a * pltpu.roll(b, d, 0), b)
        a = jnp.where(keep, a * pltpu.roll(a, d, 0), a)
        d *= 2
    return a, b


def _scan_rev(a, b, rows):
    n = a.shape[0]
    d = 1
    while d < n:
        keep = rows < n - d
        b = jnp.where(keep, b + a * pltpu.roll(b, n - d, 0), b)
        a = jnp.where(keep, a * pltpu.roll(a, n - d, 0), a)
        d *= 2
    return a, b


def _colsum(v):
    return jnp.sum(v, axis=0, keepdims=True)


def _mm(a, b, *, dims, grid, a_spec, b_spec, outs, acc_shape, name, extra=(), epi=None, slabs=1, comm=None):
    nred = grid[-1]
    red_axis = len(grid) - 1
    n_ex, n_out = len(extra), len(outs)
    epi = _epi_store if epi is None else epi

    def body(*refs):
        a_ref, b_ref = refs[:2]
        ex, o_refs, acc_ref = refs[2:2 + n_ex], refs[2 + n_ex:2 + n_ex + n_out], refs[-1]
        if slabs == 1:
            p = _dot(a_ref[...], b_ref[...], dims)
        else:
            n = b_ref.shape[-1]
            p = _dot(a_ref[:, 0:n], b_ref[0], dims)
            for jj in range(1, slabs):
                p = p + _dot(a_ref[:, jj * n:(jj + 1) * n], b_ref[jj], dims)

        first_rows = pl.program_id(0) == 0
        if nred == 1:
            epi(p, ex, o_refs, first_rows)
        else:
            k = pl.program_id(red_axis)

            @pl.when(k == 0)
            def _():
                acc_ref[...] = p

            @pl.when(jnp.logical_and(k > 0, k < nred - 1))
            def _():
                acc_ref[...] += p

            @pl.when(k == nred - 1)
            def _():
                epi(acc_ref[...] + p, ex, o_refs, first_rows)

    sem = ("parallel",) * (len(grid) - 1) + ("arbitrary",)
    if any(o[0].shape[0] == 1 for o in outs):
        sem = ("arbitrary",) * len(grid)
    res, cres = _pcall(
        body, args=(a, b) + tuple(e[0] for e in extra), grid=grid,
        in_specs=[a_spec, b_spec] + [e[1] for e in extra],
        out_specs=[o[1] for o in outs], out_shape=[o[0] for o in outs],
        scratch_shapes=[pltpu.VMEM(acc_shape if nred > 1 else (8, 128), _F32)], sem=sem, name=name, comm=comm)
    res = res[0] if n_out == 1 else res
    return res if comm is None else (res, cres)


def _epi_store(total, ex, outs, first_rows):
    outs[0][...] = total.astype(outs[0].dtype)


def _epi_residual_rms(total, ex, outs, first_rows):
    res_ref, g_ref = ex
    xn = total + res_ref[...]
    outs[0][...] = xn
    r = lax.rsqrt(jnp.mean(xn * xn, axis=-1, keepdims=True) + _EPS)
    outs[1][...] = (xn * r * g_ref[...]).astype(outs[1].dtype)


def _epi_rms_bwd(total, ex, outs, first_rows):
    x_ref, g_ref, dres_ref = ex
    dg_ref = outs[-1]
    xv = x_ref[...]
    r = lax.rsqrt(jnp.mean(xv * xv, axis=-1, keepdims=True) + _EPS)
    xhat = xv * r
    dxh = total * g_ref[...]
    dx = dres_ref[...] + r * (dxh - xhat * jnp.mean(dxh * xhat, axis=-1, keepdims=True))
    for o_ref in outs[:-1]:
        o_ref[...] = dx.astype(o_ref.dtype)

    @pl.when(first_rows)
    def _():
        dg_ref[...] = jnp.zeros_like(dg_ref)

    dg_ref[...] += _colsum(total * xhat)


def _epi_final(total, ex, outs, first_rows):
    res_ref, t_ref, g_ref = ex
    dx_ref, dxb_ref, l_ref, dg_ref = outs
    xv = total + res_ref[...]
    gv = g_ref[...]
    d = xv.shape[-1]
    r = lax.rsqrt(jnp.mean(xv * xv, axis=-1, keepdims=True) + _EPS)
    xhat = xv * r
    err = xhat * gv - t_ref[...]
    dy = err * (1.0 / d)
    dxh = dy * gv
    dx = r * (dxh - xhat * jnp.mean(dxh * xhat, axis=-1, keepdims=True))
    dx_ref[...] = dx
    dxb_ref[...] = dx.astype(dxb_ref.dtype)

    @pl.when(first_rows)
    def _():
        l_ref[...] = jnp.zeros_like(l_ref)
        dg_ref[...] = jnp.zeros_like(dg_ref)

    l_ref[...] += _colsum(err * err)
    dg_ref[...] += _colsum(dy * xhat)


def _tile(m, cap):
    t = min(m, cap)
    assert m % t == 0
    return t


def _row_spec(tm, n):
    return pl.BlockSpec((tm, n), lambda i, *_: (i, 0))


def _vec_spec(n):
    return pl.BlockSpec((1, n), lambda *_: (0, 0))


def _row_io(m, n, tm, extra, outs):
    def spec(kind):
        return _row_spec(tm, n) if kind == "row" else _vec_spec(n)

    ex = [(arr, spec(kind)) for arr, kind in extra]
    os_ = [(jax.ShapeDtypeStruct((m, n) if kind == "row" else (1, n), dt), spec(kind)) for dt, kind in outs]
    return ex, os_


def _mm_nn_stacked(a, w, out_dtype, name, comm=None, tm=1024):
    m, k = a.shape
    j, _, n = w.shape
    tm = _tile(m, tm)
    return _mm(a, w, dims=_NN, grid=(m // tm, j, 1),
               a_spec=pl.BlockSpec((tm, k), lambda i, jj, r: (i, 0)),
               b_spec=pl.BlockSpec((None, k, n), lambda i, jj, r: (jj, 0, 0)),
               outs=[(jax.ShapeDtypeStruct((m, j * n), out_dtype), pl.BlockSpec((tm, n), lambda i, jj, r: (i, jj)))],
               acc_shape=(tm, n), name=name, comm=comm)


def _mm_nt_stacked(dc, w, name, *, outs, extra=(), epi=None, comm=None, tm=1024, slabs=1):
    m = dc.shape[0]
    j, k, n = w.shape
    tm = _tile(m, tm)
    assert j % slabs == 0
    ex, os_ = _row_io(m, k, tm, extra, outs)
    wblk = (None, k, n) if slabs == 1 else (slabs, k, n)
    return _mm(dc, w, dims=_NT, grid=(m // tm, j // slabs),
               a_spec=pl.BlockSpec((tm, slabs * n), lambda i, r: (i, r)),
               b_spec=pl.BlockSpec(wblk, lambda i, r: (r, 0, 0)),
               outs=os_, extra=ex, epi=epi, acc_shape=(tm, k), name=name, slabs=slabs, comm=comm)


def _mm_tn_stacked(a, dc, j, out_dtype, name, slabs=1, ts=1024):
    s, k = a.shape
    n = dc.shape[1] // j
    ts = _tile(s, ts)
    assert j % slabs == 0

    def epi(total, ex, outs, first_rows):
        for jj in range(slabs):
            outs[0][jj] = total[:, jj * n:(jj + 1) * n].astype(outs[0].dtype)

    return _mm(a, dc, dims=_TN, grid=(j // slabs, s // ts),
               a_spec=pl.BlockSpec((ts, k), lambda jj, r: (r, 0)),
               b_spec=pl.BlockSpec((ts, slabs * n), lambda jj, r: (r, jj)),
               outs=[(jax.ShapeDtypeStruct((j, k, n), out_dtype),
                      pl.BlockSpec((slabs, k, n), lambda jj, r: (jj, 0, 0)))],
               epi=epi, acc_shape=(k, slabs * n), name=name)


def _mm_tn_nat(a, dc, out_dtype, name, ts=1024):
    s, kt = a.shape
    n = dc.shape[1]
    ts = _tile(s, ts)
    tkb = _tile(kt, 1024)
    return _mm(a, dc, dims=_TN, grid=(kt // tkb, s // ts),
               a_spec=pl.BlockSpec((ts, tkb), lambda kb, r: (r, kb)),
               b_spec=pl.BlockSpec((ts, n), lambda kb, r: (r, 0)),
               outs=[(jax.ShapeDtypeStruct((kt, n), out_dtype), pl.BlockSpec((tkb, n), lambda kb, r: (kb, 0)))],
               acc_shape=(tkb, n), name=name)


def _rms_fwd(x, g, name, comm=None):
    s, d = x.shape
    t = _tile(s, 1024)

    def body(x_ref, g_ref, h_ref):
        xv = x_ref[...]
        r = lax.rsqrt(jnp.mean(xv * xv, axis=-1, keepdims=True) + _EPS)
        h_ref[...] = (xv * r * g_ref[...]).astype(h_ref.dtype)

    res, cres = _pcall(
        body, args=(x, g), grid=(s // t,),
        in_specs=[pl.BlockSpec((t, d), lambda i: (i, 0)), pl.BlockSpec((1, d), lambda i: (0, 0))],
        out_specs=pl.BlockSpec((t, d), lambda i: (i, 0)),
        out_shape=jax.ShapeDtypeStruct((s, d), _MXU), sem=("parallel",), name=name, comm=comm)
    return res[0] if comm is None else (res[0], cres)


def _rms_bwd(dh, x, g, dres, name):
    s, d = x.shape
    t = _tile(s, 256)
    has_res = dres is not None

    def body(*refs):
        if has_res:
            dh_ref, x_ref, g_ref, dres_ref, dx_ref, dxb_ref, dg_ref = refs
        else:
            dh_ref, x_ref, g_ref, dx_ref, dxb_ref, dg_ref = refs
        xv = x_ref[...]
        dhv = dh_ref[...]
        r = lax.rsqrt(jnp.mean(xv * xv, axis=-1, keepdims=True) + _EPS)
        xhat = xv * r
        dxh = dhv * g_ref[...]
        dx = r * (dxh - xhat * jnp.mean(dxh * xhat, axis=-1, keepdims=True))
        if has_res:
            dx = dx + dres_ref[...]
        dx_ref[...] = dx
        dxb_ref[...] = dx.astype(dxb_ref.dtype)

        @pl.when(pl.program_id(0) == 0)
        def _():
            dg_ref[...] = jnp.zeros_like(dg_ref)

        dg_ref[...] += _colsum(dhv * xhat)

    row = pl.BlockSpec((t, d), lambda i: (i, 0))
    vec = pl.BlockSpec((1, d), lambda i: (0, 0))
    in_specs = [row, row, vec] + ([row] if has_res else [])
    args = (dh, x, g) + ((dres,) if has_res else ())
    return pl.pallas_call(
        body, grid=(s // t,), in_specs=in_specs, out_specs=[row, row, vec],
        out_shape=[jax.ShapeDtypeStruct((s, d), _F32), jax.ShapeDtypeStruct((s, d), _MXU),
                   jax.ShapeDtypeStruct((1, d), _F32)],
        compiler_params=_cparams("arbitrary"), name=name,
    )(*args)


_LRU_K = 4
_CONF_K = 31
_LRU_HALO = 8
_CONF_HALO = 32
_MIX_T = 512


def _lru_gates(lx, wab_ref, ba_ref, bx_ref, lam_ref):
    c = _D_LRU
    pre = _dot(lx, wab_ref[...], _NN)
    r = _sigmoid(pre[:, :c] + ba_ref[...])
    ig = _sigmoid(pre[:, c:] + bx_ref[...])
    sp = _softplus(-lam_ref[...])
    log_a = (-_RG_C) * r * sp
    a = jnp.exp(log_a)
    mult = jnp.sqrt(_neg_expm1(2.0 * log_a))
    return r, ig, sp, a, mult


def _causal_conv(ext_ref, halo, w_ref, b_ref, taps, t):
    acc = b_ref[...] + w_ref[0:1, :] * ext_ref[pl.ds(halo - (taps - 1), t), :]
    for k in range(1, taps):
        acc = acc + w_ref[k:k + 1, :] * ext_ref[pl.ds(halo - (taps - 1) + k, t), :]
    return acc


class _Windows:
    def __init__(self, ext_ref, shifted_ref, t):
        self.ext_ref, self.shifted_ref, self.t = ext_ref, shifted_ref, t
        rows = t + 24
        for r in range(1, 8):
            shifted_ref[r - 1, 0:rows, :] = ext_ref[pl.ds(r, rows), :]

    def __call__(self, off):
        q, r = divmod(off, 8)
        if r == 0:
            return self.ext_ref[pl.ds(8 * q, self.t), :]
        return self.shifted_ref[r - 1, pl.ds(8 * q, self.t), :]


def _mixer_fwd(xs, h1, w_in, w_out, gnorm, lcw, lcb, wab, ba, bx, lam, ccw, ccb, lng, lnb, name, comm=None):
    s, d = xs.shape
    nblk, _, n = w_in.shape
    c = _D_LRU
    t = _tile(s, _MIX_T)
    nt = s // t

    def body(x_ref, h_ref, win_ref, wout_ref, gn_ref,
             lcw_ref, lcb_ref, wab_ref, ba_ref, bx_ref, lam_ref, ccw_ref, ccb_ref, lng_ref, lnb_ref,
             z_ref, ycat_ref, hs_ref, cc_ref, x1_ref, h2_ref,
             ext_ref, cge_ref, hc_ref, shifted_ref, zprev_ref):
        i = pl.program_id(0)
        first = i == 0
        rows = lax.broadcasted_iota(jnp.int32, (t, c), 0)

        @pl.when(first)
        def _():
            zprev_ref[...] = jnp.zeros_like(zprev_ref)

        hv = h_ref[...]
        for j in range(nblk):
            z_ref[:, j * n:(j + 1) * n] = _dot(hv, win_ref[j], _NN)
        lx0_ref, gate_ref = z_ref.at[:, 0:c], z_ref.at[:, c:2 * c]
        ca_ref, cb_ref = z_ref.at[:, 2 * c:3 * c], z_ref.at[:, 3 * c:4 * c]
        lx0h_ref = zprev_ref.at[_CONF_HALO - _LRU_HALO:_CONF_HALO, 0:c]
        cah_ref, cbh_ref = zprev_ref.at[:, 2 * c:3 * c], zprev_ref.at[:, 3 * c:4 * c]

        ext_ref[0:_LRU_HALO, :] = jnp.where(first, 0.0, lx0h_ref[...])
        ext_ref[_LRU_HALO:_LRU_HALO + t, :] = lx0_ref[...]
        lx = _causal_conv(ext_ref, _LRU_HALO, lcw_ref, lcb_ref, _LRU_K, t)
        r, ig, sp, a, mult = _lru_gates(lx, wab_ref, ba_ref, bx_ref, lam_ref)
        u = mult * (ig * lx)
        a_cum, h_loc = _scan_fwd(a, u, rows)

        @pl.when(first)
        def _():
            hc_ref[...] = jnp.zeros_like(hc_ref)

        h = h_loc + a_cum * hc_ref[7:8, :]
        hs_ref[...] = h
        hc_ref[...] = hs_ref[pl.ds(t - 8, 8), :]
        ycat_ref[:, 0:c] = (h * _gelu(gate_ref[...])).astype(ycat_ref.dtype)

        cge_ref[0:_CONF_HALO, :] = jnp.where(first, 0.0, cah_ref[...] * _sigmoid(cbh_ref[...]))
        cge_ref[_CONF_HALO:_CONF_HALO + t, :] = ca_ref[...] * _sigmoid(cb_ref[...])
        win = _Windows(cge_ref, shifted_ref, t)
        first_off = _CONF_HALO - (_CONF_K - 1)
        cc = ccb_ref[...] + ccw_ref[0:1, :] * win(first_off)
        for k in range(1, _CONF_K):
            cc = cc + ccw_ref[k:k + 1, :] * win(first_off + k)
        cc_ref[...] = cc
        xc = cc - jnp.mean(cc, axis=-1, keepdims=True)
        rstd = lax.rsqrt(jnp.mean(xc * xc, axis=-1, keepdims=True) + _EPS)
        ln = xc * rstd * lng_ref[...] + lnb_ref[...]
        ycat_ref[:, c:2 * c] = (ln * _sigmoid(ln)).astype(ycat_ref.dtype)

        zprev_ref[...] = z_ref[pl.ds(t - _CONF_HALO, _CONF_HALO), :]
        y = _dot(ycat_ref[...], wout_ref[...], _NN)
        _epi_residual_rms(y, (x_ref, gn_ref), (x1_ref, h2_ref), first)

    def const(arr):
        return pl.BlockSpec(arr.shape, lambda i: (0,) * arr.ndim, pipeline_mode=pl.Buffered(1))

    def rows_of(width):
        return pl.BlockSpec((t, width), lambda i: (i, 0))

    params = (lcw, lcb, wab, ba, bx, lam, ccw, ccb, lng, lnb)
    res, cres = _pcall(
        body, args=(xs, h1, w_in, w_out, gnorm, *params), grid=(nt,),
        in_specs=[rows_of(d), rows_of(d), const(w_in), const(w_out), const(gnorm)] + [const(p) for p in params],
        out_specs=[rows_of(nblk * n), rows_of(2 * c), rows_of(c), rows_of(c), rows_of(d), rows_of(d)],
        out_shape=[jax.ShapeDtypeStruct((s, nblk * n), _F32), jax.ShapeDtypeStruct((s, 2 * c), _MXU),
                   jax.ShapeDtypeStruct((s, c), _F32), jax.ShapeDtypeStruct((s, c), _F32),
                   jax.ShapeDtypeStruct((s, d), _F32), jax.ShapeDtypeStruct((s, d), _MXU)],
        scratch_shapes=[pltpu.VMEM((t + _LRU_HALO, c), _F32), pltpu.VMEM((t + _CONF_HALO, c), _F32),
                        pltpu.VMEM((8, c), _F32), pltpu.VMEM((7, t + _CONF_HALO, c), _F32),
                        pltpu.VMEM((_CONF_HALO, nblk * n), _F32)],
        sem=("arbitrary",), name=name, comm=comm)
    return res, cres


def _mixer_bwd(dx1b, w_out, h1, z, hs, cc, lcw, lcb, wab, ba, bx, lam, ccw, ccb, lng, lnb, name, comm=None):
    s = z.shape[0]
    d = h1.shape[1]
    c = _D_LRU
    t = _tile(s, _MIX_T)
    nt = s // t
    nblk = z.shape[1] // 256

    def body(dxb_ref, wout_ref, h1_ref, lx0_ref, lx0h_ref, gate_ref, ca_ref, cah_ref, cb_ref, cbh_ref,
             hs_ref, hsh_ref, cc_ref,
             lcw_ref, lcb_ref, wab_ref, ba_ref, bx_ref, lam_ref, ccw_ref, ccb_ref, lng_ref, lnb_ref,
             dz_ref, pin_ref, dlcw_ref, dlcb_ref, dwab_ref, dba_ref, dbx_ref, dlam_ref, dccw_ref, dccb_ref, dlng_ref,
             dlnb_ref,
             ext_ref, up_ref, cge_ref, dce_ref, xc_ref, dlxc_ref, dccc_ref, shifted_ref, dwin_ref):
        i = pl.program_id(0)
        first_tile = i == nt - 1
        last_tile = i == 0
        rows = lax.broadcasted_iota(jnp.int32, (t, c), 0)

        @pl.when(last_tile)
        def _():
            for ref in (dlcw_ref, dlcb_ref, dwab_ref, dba_ref, dbx_ref, dlam_ref, dccw_ref, dccb_ref, dlng_ref,
                        dlnb_ref, xc_ref, dlxc_ref, dccc_ref, dwin_ref):
                ref[...] = jnp.zeros_like(ref)

        dycat = _dot(dxb_ref[...], wout_ref[...], _NT)

        ext_ref[0:_LRU_HALO, :] = jnp.where(first_tile, 0.0, lx0h_ref[...])
        ext_ref[_LRU_HALO:_LRU_HALO + t, :] = lx0_ref[...]
        lx = _causal_conv(ext_ref, _LRU_HALO, lcw_ref, lcb_ref, _LRU_K, t)
        r, ig, sp, a, mult = _lru_gates(lx, wab_ref, ba_ref, bx_ref, lam_ref)
        h = hs_ref[...]
        gl, dgl = _gelu_and_grad(gate_ref[...])
        dyl = dycat[:, 0:c]
        dz_ref[:, c:2 * c] = (dyl * h * dgl).astype(dz_ref.dtype)
        dh = dyl * gl

        up_ref[0:t, :] = a
        up_ref[t:t + 8, :] = jnp.ones((8, c), _F32)
        a_up = up_ref[pl.ds(1, t), :]
        a_cum, g_loc = _scan_rev(a_up, dh, rows)
        gt = g_loc + a_cum * xc_ref[0:1, :]
        xc_ref[...] = (a * gt)[0:8, :]

        up_ref[0:8, :] = jnp.where(first_tile, 0.0, hsh_ref[...])
        up_ref[8:8 + t, :] = h
        hprev = up_ref[pl.ds(7, t), :]

        da = gt * hprev
        dmult = gt * ig * lx
        dig = gt * mult * lx
        dlx = gt * mult * ig
        dlog_a = da * a - dmult * a * a / mult
        dpre_r = dlog_a * (-_RG_C) * sp * r * (1.0 - r)
        dpre_i = dig * ig * (1.0 - ig)
        dlam_ref[...] += _colsum(dlog_a * r) * (_RG_C * _sigmoid(-lam_ref[...]))
        dba_ref[...] += _colsum(dpre_r)
        dbx_ref[...] += _colsum(dpre_i)
        dpre = jnp.concatenate([dpre_r, dpre_i], axis=1).astype(_MXU)
        dlx = dlx + _dot(dpre, wab_ref[...], _NT)
        dwab_ref[...] += _dot(lx, dpre, _TN)

        dlcb_ref[...] += _colsum(dlx)
        up_ref[0:t, :] = dlx
        up_ref[t:t + 8, :] = dlxc_ref[...]
        dlxc_ref[...] = dlx[0:8, :]
        acc = lcw_ref[0:1, :] * up_ref[pl.ds(_LRU_K - 1, t), :]
        for k in range(1, _LRU_K):
            acc = acc + lcw_ref[k:k + 1, :] * up_ref[pl.ds(_LRU_K - 1 - k, t), :]
        dz_ref[:, 0:c] = acc.astype(dz_ref.dtype)
        for k in range(_LRU_K):
            dlcw_ref[k:k + 1, :] += _colsum(dlx * ext_ref[pl.ds(_LRU_HALO - (_LRU_K - 1) + k, t), :])

        sig_b = _sigmoid(cb_ref[...])
        ca = ca_ref[...]
        cge_ref[0:_CONF_HALO, :] = jnp.where(first_tile, 0.0, cah_ref[...] * _sigmoid(cbh_ref[...]))
        cge_ref[_CONF_HALO:_CONF_HALO + t, :] = ca * sig_b
        ccv = cc_ref[...]
        xcen = ccv - jnp.mean(ccv, axis=-1, keepdims=True)
        rstd = lax.rsqrt(jnp.mean(xcen * xcen, axis=-1, keepdims=True) + _EPS)
        xn = xcen * rstd
        ln = xn * lng_ref[...] + lnb_ref[...]
        sg = _sigmoid(ln)
        dln = dycat[:, c:2 * c] * (sg * (1.0 + ln * (1.0 - sg)))
        dlng_ref[...] += _colsum(dln * xn)
        dlnb_ref[...] += _colsum(dln)
        dxn = dln * lng_ref[...]
        dcc = rstd * (dxn - jnp.mean(dxn, axis=-1, keepdims=True)
                      - xn * jnp.mean(dxn * xn, axis=-1, keepdims=True))
        dccb_ref[...] += _colsum(dcc)
        win = _Windows(cge_ref, shifted_ref, t)
        for k in range(_CONF_K):
            dccw_ref[k:k + 1, :] += _colsum(dcc * win(_CONF_HALO - (_CONF_K - 1) + k))
        dce_ref[0:t, :] = dcc
        dce_ref[t:t + _CONF_HALO, :] = dccc_ref[...]
        dccc_ref[...] = dcc[0:_CONF_HALO, :]
        win = _Windows(dce_ref, shifted_ref, t)
        dcg = ccw_ref[0:1, :] * win(_CONF_K - 1)
        for k in range(1, _CONF_K):
            dcg = dcg + ccw_ref[k:k + 1, :] * win(_CONF_K - 1 - k)
        dz_ref[:, 2 * c:3 * c] = (dcg * sig_b).astype(dz_ref.dtype)
        dz_ref[:, 3 * c:4 * c] = (dcg * ca * sig_b * (1.0 - sig_b)).astype(dz_ref.dtype)

        dwin_ref[...] += _dot(h1_ref[...], dz_ref[...], _TN)

        @pl.when(first_tile)
        def _():
            for j in range(nblk):
                pin_ref[j] = dwin_ref[:, j * 256:(j + 1) * 256].astype(pin_ref.dtype)

    def col(j):
        return pl.BlockSpec((t, c), lambda i: (nt - 1 - i, j))

    def halo(j, rows_):
        per = t // rows_
        return pl.BlockSpec((rows_, c), lambda i: (jnp.maximum((nt - 1 - i) * per - 1, 0), j))

    def full(shape):
        return pl.BlockSpec(shape, lambda i: (0,) * len(shape))

    params = (lcw, lcb, wab, ba, bx, lam, ccw, ccb, lng, lnb)
    small = [(_LRU_K, c), (1, c), (c, 2 * c), (1, c), (1, c), (1, c), (_CONF_K, c), (1, c), (1, c), (1, c)]
    wide = pl.BlockSpec((t, d), lambda i: (nt - 1 - i, 0))
    pin_shape = (nblk, d, 256)
    return _pcall(
        body, args=(dx1b, w_out, h1, z, z, z, z, z, z, z, hs, hs, cc, *params), grid=(nt,),
        in_specs=[wide, pl.BlockSpec(w_out.shape, lambda i: (0, 0), pipeline_mode=pl.Buffered(1)), wide,
                  col(0), halo(0, _LRU_HALO), col(1), col(2), halo(2, _CONF_HALO), col(3), halo(3, _CONF_HALO),
                  col(0), halo(0, 8), col(0)]
        + [full(p.shape) for p in params],
        out_specs=[pl.BlockSpec((t, 4 * c), lambda i: (nt - 1 - i, 0)), full(pin_shape)] + [full(sh) for sh in small],
        out_shape=[jax.ShapeDtypeStruct((s, 4 * c), _MXU), jax.ShapeDtypeStruct(pin_shape, _XFER)]
        + [jax.ShapeDtypeStruct(sh, _F32) for sh in small],
        scratch_shapes=[pltpu.VMEM((t + _LRU_HALO, c), _F32), pltpu.VMEM((t + 8, c), _F32),
                        pltpu.VMEM((t + _CONF_HALO, c), _F32), pltpu.VMEM((t + _CONF_HALO, c), _F32),
                        pltpu.VMEM((8, c), _F32), pltpu.VMEM((8, c), _F32), pltpu.VMEM((_CONF_HALO, c), _F32),
                        pltpu.VMEM((7, t + _CONF_HALO, c), _F32), pltpu.VMEM((d, 4 * c), _F32)],
        sem=("arbitrary",), name=name, comm=comm, vmem=_VMEM_LIMIT_FUSED)


_ATT_T = 512


def _attn_probs(qh, kh, scale):
    sc = _dot(qh, kh, _NT) * scale
    e = jnp.exp(sc - jnp.max(sc, axis=-1, keepdims=True))
    return e / jnp.sum(e, axis=-1, keepdims=True)


def _const_spec(arr):
    return pl.BlockSpec(arr.shape, lambda i: (0,) * arr.ndim, pipeline_mode=pl.Buffered(1))


def _attn_fwd(h2, x1, w_q, kv, w_o, gnorm, name):
    s, d = h2.shape
    nm = kv.shape[0]
    hd = d // _XA_HEADS
    t = _tile(s, 2 * _ATT_T)
    scale = hd ** -0.5

    def body(h_ref, x1_ref, wq_ref, k_ref, v_ref, wo_ref, g_ref, q_ref, o_ref, x2_ref, h3_ref):
        q_ref[...] = _dot(h_ref[...], wq_ref[...], _NN).astype(q_ref.dtype)
        for hh in range(_XA_HEADS):
            sl = slice(hh * hd, (hh + 1) * hd)
            p = _attn_probs(q_ref[:, sl], k_ref[:, sl], scale)
            o_ref[:, sl] = _dot(p, v_ref[:, sl], _NN).astype(o_ref.dtype)
        y = _dot(o_ref[...], wo_ref[...], _NN)
        _epi_residual_rms(y, (x1_ref, g_ref), (x2_ref, h3_ref), None)

    row = pl.BlockSpec((t, d), lambda i: (i, 0))
    half = pl.BlockSpec((nm, d), lambda i: (0, 0), pipeline_mode=pl.Buffered(1))
    half2 = pl.BlockSpec((nm, d), lambda i: (0, 1), pipeline_mode=pl.Buffered(1))
    return pl.pallas_call(
        body, grid=(s // t,),
        in_specs=[row, row, _const_spec(w_q), half, half2, _const_spec(w_o), _const_spec(gnorm)],
        out_specs=[row, row, row, row],
        out_shape=[jax.ShapeDtypeStruct((s, d), _MXU), jax.ShapeDtypeStruct((s, d), _MXU),
                   jax.ShapeDtypeStruct((s, d), _F32), jax.ShapeDtypeStruct((s, d), _MXU)],
        compiler_params=_cparams("parallel"), name=name,
    )(h2, x1, w_q, kv, kv, w_o, gnorm)


def _attn_bwd(dx2b, dx2, q, o, h2, x1, kv, w_o, w_q, gnorm, name, comm=None):
    s, d = q.shape
    nm = kv.shape[0]
    hd = d // _XA_HEADS
    t = _tile(s, _ATT_T)
    nt = s // t
    scale = hd ** -0.5

    def body(dxb_ref, dx2_ref, q_ref, o_ref, h_ref, x1_ref, k_ref, v_ref, wo_ref, wq_ref, g_ref,
             dx1_ref, dx1b_ref, dgn_ref, dk_ref, dv_ref, pwo_ref, pwq_ref, dq_ref, awo_ref, awq_ref):
        i = pl.program_id(0)
        first = i == 0

        @pl.when(first)
        def _():
            for ref in (dk_ref, dv_ref, awo_ref, awq_ref):
                ref[...] = jnp.zeros_like(ref)

        dxb = dxb_ref[...]
        do = _dot(dxb, wo_ref[...], _NT).astype(_MXU)
        awo_ref[...] += _dot(o_ref[...], dxb, _TN)
        for hh in range(_XA_HEADS):
            sl = slice(hh * hd, (hh + 1) * hd)
            qh = q_ref[:, sl]
            kh = k_ref[:, sl]
            doh = do[:, sl]
            p = _attn_probs(qh, kh, scale)
            dp = _dot(doh, v_ref[:, sl], _NT)
            dv_ref[:, sl] += _dot(p, doh, _TN)
            ds = (p * (dp - jnp.sum(dp * p, axis=-1, keepdims=True)) * scale).astype(_MXU)
            dq_ref[:, sl] = _dot(ds, kh, _NN).astype(dq_ref.dtype)
            dk_ref[:, sl] += _dot(ds, qh, _TN)
        dq = dq_ref[...]
        awq_ref[...] += _dot(h_ref[...], dq, _TN)
        dh = _dot(dq, wq_ref[...], _NT)
        _epi_rms_bwd(dh, (x1_ref, g_ref, dx2_ref), (dx1_ref, dx1b_ref, dgn_ref), first)

        @pl.when(i == nt - 1)
        def _():
            pwo_ref[...] = awo_ref[...].astype(pwo_ref.dtype)
            pwq_ref[...] = awq_ref[...].astype(pwq_ref.dtype)

    row = pl.BlockSpec((t, d), lambda i: (i, 0))
    vec = pl.BlockSpec((1, d), lambda i: (0, 0))
    mem_blk = pl.BlockSpec((nm, d), lambda i: (0, 0))
    sq = pl.BlockSpec((d, d), lambda i: (0, 0))
    half = pl.BlockSpec((nm, d), lambda i: (0, 0), pipeline_mode=pl.Buffered(1))
    half2 = pl.BlockSpec((nm, d), lambda i: (0, 1), pipeline_mode=pl.Buffered(1))
    return _pcall(
        body, args=(dx2b, dx2, q, o, h2, x1, kv, kv, w_o, w_q, gnorm), grid=(nt,),
        in_specs=[row, row, row, row, row, row, half, half2, _const_spec(w_o), _const_spec(w_q), _const_spec(gnorm)],
        out_specs=[row, row, vec, mem_blk, mem_blk, sq, sq],
        out_shape=[jax.ShapeDtypeStruct((s, d), _F32), jax.ShapeDtypeStruct((s, d), _MXU),
                   jax.ShapeDtypeStruct((1, d), _F32), jax.ShapeDtypeStruct((nm, d), _F32),
                   jax.ShapeDtypeStruct((nm, d), _F32), jax.ShapeDtypeStruct((d, d), _XFER),
                   jax.ShapeDtypeStruct((d, d), _XFER)],
        scratch_shapes=[pltpu.VMEM((t, d), _MXU), pltpu.VMEM((d, d), _F32), pltpu.VMEM((d, d), _F32)],
        sem=("arbitrary",), name=name, comm=comm, vmem=_VMEM_LIMIT_FUSED)


_FFN_K = 3
_FFN_FUSED_T = 256
_VMEM_LIMIT_FUSED = 58 * 1024 * 1024


def _ffn_fused_fwd(h3, w_up, w_down, fcw, fcb, x2, target, gfin, name):
    s, d = h3.shape
    nblk, _, n = w_up.shape
    half = nblk // 2
    f = half * n
    t = _tile(s, _FFN_FUSED_T)

    def body(h_ref, wup_ref, wdown_ref, w_ref, b_ref, x2_ref, t_ref, g_ref,
             g0_ref, act_ref, gl_ref, udgl_ref, dx_ref, dxb_ref, l_ref, dg_ref, ext0_ref, ext1_ref, halo_ref):
        i = pl.program_id(0)
        first = i == 0
        h = h_ref[...]
        total = None
        ahead = (_dot(h, wup_ref[0], _NN), _dot(h, wup_ref[half], _NN))
        for j in range(half):
            cs = slice(j * n, (j + 1) * n)
            ext_ref = ext0_ref if j % 2 == 0 else ext1_ref
            g0, u = ahead
            if j + 1 < half:
                ahead = (_dot(h, wup_ref[j + 1], _NN), _dot(h, wup_ref[half + j + 1], _NN))
            if j > 0:
                prev = slice((j - 1) * n, j * n)
                p = _dot(act_ref[:, prev], wdown_ref[prev, :], _NN)
                total = p if total is None else total + p
            g0_ref[:, cs] = g0.astype(g0_ref.dtype)
            ext_ref[0:8, :] = jnp.where(first, 0.0, halo_ref[:, cs])
            ext_ref[8:8 + t, :] = g0
            halo_ref[:, cs] = g0[t - 8:t, :]
            g = _causal_conv(ext_ref, 8, w_ref.at[:, cs], b_ref.at[:, cs], _FFN_K, t)
            gl, dgl = _gelu_and_grad(g)
            gl_ref[:, cs] = gl.astype(gl_ref.dtype)
            udgl_ref[:, cs] = (u * dgl).astype(udgl_ref.dtype)
            act_ref[:, cs] = (gl * u).astype(act_ref.dtype)
        last = slice((half - 1) * n, half * n)
        total = total + _dot(act_ref[:, last], wdown_ref[last, :], _NN)
        _epi_final(total, (x2_ref, t_ref, g_ref), (dx_ref, dxb_ref, l_ref, dg_ref), first)

    def const(shape):
        return pl.BlockSpec(shape, lambda i: (0,) * len(shape), pipeline_mode=pl.Buffered(1))

    row = pl.BlockSpec((t, d), lambda i: (i, 0))
    vec = pl.BlockSpec((1, d), lambda i: (0, 0))
    return pl.pallas_call(
        body, grid=(s // t,),
        in_specs=[row, const(w_up.shape), const(w_down.shape), const(fcw.shape), const(fcb.shape), row, row, vec],
        out_specs=[pl.BlockSpec((t, f), lambda i: (i, 0))] * 4 + [row, row, vec, vec],
        out_shape=[jax.ShapeDtypeStruct((s, f), _MXU)] * 4
        + [jax.ShapeDtypeStruct((s, d), _F32), jax.ShapeDtypeStruct((s, d), _MXU),
                   jax.ShapeDtypeStruct((1, d), _F32), jax.ShapeDtypeStruct((1, d), _F32)],
        scratch_shapes=[pltpu.VMEM((t + 8, n), _F32), pltpu.VMEM((t + 8, n), _F32), pltpu.VMEM((8, f), _F32)],
        compiler_params=_cparams("arbitrary", vmem=_VMEM_LIMIT_FUSED), name=name,
    )(h3, w_up, w_down, fcw, fcb, x2, target, gfin)


def _ffn_fused_bwd(dx3b, g0, gl, udgl, w_down, w_up, fcw, x2, gnorm, dx3, name, comm=None):
    s, d = x2.shape
    nblk, _, n = w_up.shape
    half = nblk // 2
    f = half * n
    t = _tile(s, _FFN_FUSED_T)
    nt = s // t
    hrows = 16

    def body(dxb_ref, g0_ref, g0h_ref, gl_ref, udgl_ref, wdown_ref, wup_ref, w_ref, x2_ref, g_ref, dx3_ref,
             dgu_ref, dx2_ref, dx2b_ref, dgn_ref, dw_ref, db_ref, ext0_ref, ext1_ref, up0_ref, up1_ref, car_ref):
        i = pl.program_id(0)
        first_tile = i == nt - 1
        last_tile = i == 0

        @pl.when(last_tile)
        def _():
            dw_ref[...] = jnp.zeros_like(dw_ref)
            db_ref[...] = jnp.zeros_like(db_ref)
            car_ref[...] = jnp.zeros_like(car_ref)

        dxb = dxb_ref[...]
        total = None
        for j in range(half):
            cs = slice(j * n, (j + 1) * n)
            us = slice(f + j * n, f + (j + 1) * n)
            ext_ref = ext0_ref if j % 2 == 0 else ext1_ref
            up_ref = up0_ref if j % 2 == 0 else up1_ref
            dact = _dot(dxb, wdown_ref[cs, :], _NT)
            ext_ref[0:8, :] = jnp.where(first_tile, 0.0, g0h_ref[:, cs].astype(_F32)[hrows - 8:hrows])
            ext_ref[8:8 + t, :] = g0_ref[:, cs].astype(_F32)
            du = (dact * gl_ref[:, cs].astype(_F32)).astype(dgu_ref.dtype)
            dgu_ref[:, us] = du
            dg = dact * udgl_ref[:, cs].astype(_F32)
            db_ref[:, cs] += _colsum(dg)
            for k in range(_FFN_K):
                dw_ref[k:k + 1, cs] += _colsum(dg * ext_ref[pl.ds(8 - (_FFN_K - 1) + k, t), :])
            up_ref[0:t, :] = dg
            up_ref[t:t + 8, :] = car_ref[:, cs]
            car_ref[:, cs] = dg[0:8, :]
            dg0 = w_ref[0:1, cs] * up_ref[pl.ds(_FFN_K - 1, t), :]
            for k in range(1, _FFN_K):
                dg0 = dg0 + w_ref[k:k + 1, cs] * up_ref[pl.ds(_FFN_K - 1 - k, t), :]
            dg0 = dg0.astype(dgu_ref.dtype)
            dgu_ref[:, cs] = dg0
            p = _dot(dg0, wup_ref[j], _NT) + _dot(du, wup_ref[half + j], _NT)
            total = p if total is None else total + p
        _epi_rms_bwd(total, (x2_ref, g_ref, dx3_ref), (dx2_ref, dx2b_ref, dgn_ref), last_tile)

    def const(shape):
        return pl.BlockSpec(shape, lambda i: (0,) * len(shape), pipeline_mode=pl.Buffered(1))

    row = pl.BlockSpec((t, d), lambda i: (nt - 1 - i, 0))
    vec = pl.BlockSpec((1, d), lambda i: (0, 0))
    per = t // hrows
    wide = pl.BlockSpec((t, f), lambda i: (nt - 1 - i, 0))
    return _pcall(
        body, args=(dx3b, g0, g0, gl, udgl, w_down, w_up, fcw, x2, gnorm, dx3), grid=(nt,),
        in_specs=[row, wide, pl.BlockSpec((hrows, f), lambda i: (jnp.maximum((nt - 1 - i) * per - 1, 0), 0)),
                  wide, wide, const(w_down.shape), const(w_up.shape), const(fcw.shape), row, vec, row],
        out_specs=[pl.BlockSpec((t, 2 * f), lambda i: (nt - 1 - i, 0)), row, row, vec,
                   pl.BlockSpec((_FFN_K, f), lambda i: (0, 0)), pl.BlockSpec((1, f), lambda i: (0, 0))],
        out_shape=[jax.ShapeDtypeStruct((s, 2 * f), _MXU), jax.ShapeDtypeStruct((s, d), _F32),
                   jax.ShapeDtypeStruct((s, d), _MXU), jax.ShapeDtypeStruct((1, d), _F32),
                   jax.ShapeDtypeStruct((_FFN_K, f), _F32), jax.ShapeDtypeStruct((1, f), _F32)],
        scratch_shapes=[pltpu.VMEM((t + 8, n), _F32), pltpu.VMEM((t + 8, n), _F32),
                        pltpu.VMEM((t + 8, n), _F32), pltpu.VMEM((t + 8, n), _F32), pltpu.VMEM((8, f), _F32)],
        sem=("arbitrary",), name=name, comm=comm, vmem=_VMEM_LIMIT_FUSED)


def _mesh_pos():
    return lax.axis_index("x"), lax.axis_index("y"), lax.axis_index("c")


def _flip(v, bit):
    return 1 - v if bit else v


def _sem_scratch(n):
    return [pltpu.SemaphoreType.DMA((7 * n,)), pltpu.SemaphoreType.DMA((7 * n,)), pltpu.SemaphoreType.DMA((n,))]


class _Gather:
    def __init__(self, xs):
        self.ins = list(xs)
        self.outs = [jax.ShapeDtypeStruct((_NDEV,) + v.shape, v.dtype) for v in xs]
        self.scratch = _sem_scratch(len(xs))

    def _plan(self, x_refs, out_refs, sems):
        send_sems, recv_sems, local_sems = sems
        x, y, c = _mesh_pos()
        me, sibling = (x, y, c), (x, y, 1 - c)
        chips = [(1 - x, y), (x, 1 - y), (1 - x, 1 - y)]

        def copy(a, k, block, to, src=None):
            slot = out_refs[a].at[4 * block[0] + 2 * block[1] + block[2]]
            return pltpu.make_async_remote_copy(
                src_ref=slot if src is None else src, dst_ref=slot,
                send_sem=send_sems.at[a * 7 + k], recv_sem=recv_sems.at[a * 7 + k],
                device_id=to, device_id_type=_MESH_ID)

        def own(a):
            return pltpu.make_async_copy(x_refs[a], out_refs[a].at[4 * x + 2 * y + c], local_sems.at[a])

        def first(a):
            return [copy(a, 0, me, sibling, src=x_refs[a])] + [
                copy(a, 1 + j, me, (*chip, c), src=x_refs[a]) for j, chip in enumerate(chips)]

        return me, sibling, chips, c, copy, own, first

    def start(self, x_refs, out_refs, sems):
        _, _, _, _, _, own, first = self._plan(x_refs, out_refs, sems)
        for a in range(len(self.ins)):
            own(a).start()
            for cp in first(a):
                cp.start()

    def finish(self, x_refs, out_refs, sems, results=None):
        me, sibling, chips, c, copy, own, first = self._plan(x_refs, out_refs, sems)
        n = len(self.ins)
        passed = []
        for a in range(n):
            for j, chip in enumerate(chips):
                copy(a, 1 + j, (*chip, c), me).wait_recv()
                fwd = copy(a, 4 + j, (*chip, c), sibling)
                fwd.start()
                passed.append(fwd)
        for a in range(n):
            copy(a, 0, sibling, me).wait_recv()
            for j, chip in enumerate(chips):
                copy(a, 4 + j, (*chip, 1 - c), me).wait_recv()
        for a in range(n):
            for cp in first(a):
                cp.wait_send()
        for cp in passed:
            cp.wait_send()
        for a in range(n):
            own(a).wait()


class _Exchange:
    def __init__(self, gs):
        self.ins = list(gs)
        self.outs = [jax.ShapeDtypeStruct(v.shape, v.dtype) for v in gs]
        self.scratch = _sem_scratch(len(gs))

    def _plan(self, g_refs, r_refs, sems):
        send_sems, recv_sems, local_sems = sems
        x, y, c = _mesh_pos()
        me_idx = 4 * x + 2 * y + c
        n = len(self.ins)

        def copy(a, k):
            peer = (_flip(x, k & 4), _flip(y, k & 2), _flip(c, k & 1))
            peer_idx = 4 * peer[0] + 2 * peer[1] + peer[2]
            return pltpu.make_async_remote_copy(
                src_ref=g_refs[a].at[peer_idx], dst_ref=r_refs[a].at[me_idx],
                send_sem=send_sems.at[a * 7 + k - 1], recv_sem=recv_sems.at[a * 7 + k - 1],
                device_id=peer, device_id_type=_MESH_ID)

        copies = [copy(a, k) for a in range(n) for k in range(1, _NDEV)]
        mine = [pltpu.make_async_copy(g_refs[a].at[me_idx], r_refs[a].at[me_idx], local_sems.at[a])
                for a in range(n)]
        return copies, mine

    def start(self, g_refs, r_refs, sems):
        copies, mine = self._plan(g_refs, r_refs, sems)
        for cp in copies + mine:
            cp.start()

    def finish(self, g_refs, r_refs, sems, results=None):
        copies, mine = self._plan(g_refs, r_refs, sems)
        for cp in copies:
            cp.wait_recv()
        for cp in copies:
            cp.wait_send()
        for cp in mine:
            cp.wait()


class _GatherOfResult:
    def __init__(self, index, shape, dtype):
        self.index = index
        self.ins = []
        self.outs = [jax.ShapeDtypeStruct((_NDEV,) + tuple(shape), dtype)]
        self.scratch = _sem_scratch(1)

    def start(self, ins, outs, sems):
        pass

    def finish(self, ins, outs, sems, results):
        send_sems, recv_sems, local_sems = sems
        src = results[self.index]
        x, y, c = _mesh_pos()
        me_idx = 4 * x + 2 * y + c
        mine = pltpu.make_async_copy(src, outs[0].at[me_idx], local_sems.at[0])
        copies = []
        for k in range(1, _NDEV):
            peer = (_flip(x, k & 4), _flip(y, k & 2), _flip(c, k & 1))
            copies.append(pltpu.make_async_remote_copy(
                src_ref=src, dst_ref=outs[0].at[me_idx], send_sem=send_sems.at[k - 1], recv_sem=recv_sems.at[k - 1],
                device_id=peer, device_id_type=_MESH_ID))
        for cp in copies + [mine]:
            cp.start()
        for cp in copies:
            cp.wait_recv()
        for cp in copies:
            cp.wait_send()
        mine.wait()


class _Several:
    def __init__(self, parts):
        self.parts = list(parts)
        self.ins = [a for p in self.parts for a in p.ins]
        self.outs = [a for p in self.parts for a in p.outs]
        self.scratch = [a for p in self.parts for a in p.scratch]

    def _split(self, ins, outs, sems):
        i = o = s = 0
        for p in self.parts:
            ni, no, ns = len(p.ins), len(p.outs), len(p.scratch)
            yield p, ins[i:i + ni], outs[o:o + no], sems[s:s + ns]
            i, o, s = i + ni, o + no, s + ns

    def start(self, ins, outs, sems):
        for part, i, o, s in self._split(ins, outs, sems):
            part.start(i, o, s)

    def finish(self, ins, outs, sems, results):
        for part, i, o, s in self._split(ins, outs, sems):
            part.finish(i, o, s, results)


def _adamw_math(w, g, m, v):
    m = _ADAM_B1 * m + (1.0 - _ADAM_B1) * g
    v = _ADAM_B2 * v + (1.0 - _ADAM_B2) * (g * g)
    m_hat = m / (1.0 - _ADAM_B1 ** _ADAM_STEP)
    v_hat = v / (1.0 - _ADAM_B2 ** _ADAM_STEP)
    delta = -_ADAM_LR * (m_hat / (jnp.sqrt(v_hat) + _ADAM_EPS) + _ADAM_WD * w)
    return delta, m, v


def _sum_adamw(parts, w, m, v, name):
    r, c = w.shape
    tr = _tile(r, 128)

    def body(p_ref, w_ref, m_ref, v_ref, g_ref, d_ref, nm_ref, nv_ref):
        g = p_ref[0].astype(_F32)
        for j in range(1, _NDEV):
            g = g + p_ref[j].astype(_F32)
        delta, nm, nv = _adamw_math(w_ref[...], g, m_ref[...], v_ref[...])
        g_ref[...] = g
        d_ref[...] = delta
        nm_ref[...] = nm
        nv_ref[...] = nv

    blk = pl.BlockSpec((tr, c), lambda i: (i, 0))
    return pl.pallas_call(
        body, grid=(r // tr,),
        in_specs=[pl.BlockSpec((_NDEV, tr, c), lambda i: (0, i, 0)), blk, blk, blk],
        out_specs=[blk] * 4, out_shape=[jax.ShapeDtypeStruct((r, c), _F32)] * 4,
        compiler_params=_cparams("parallel"), name=name,
    )(parts, w, m, v)


def _sum8(parts, name):
    _, r, c = parts.shape

    def body(p_ref, o_ref):
        g = p_ref[0]
        for j in range(1, _NDEV):
            g = g + p_ref[j]
        o_ref[...] = g

    return pl.pallas_call(
        body, grid=(1,), in_specs=[pl.BlockSpec((_NDEV, r, c), lambda i: (0, 0, 0))],
        out_specs=pl.BlockSpec((r, c), lambda i: (0, 0)), out_shape=jax.ShapeDtypeStruct((r, c), _F32),
        compiler_params=_cparams("arbitrary"), name=name,
    )(parts)


def _adamw_many(gs, ws, ms, vs, name):
    n = len(ws)

    def body(*refs):
        g_refs, w_refs, m_refs, v_refs = (refs[k * n:(k + 1) * n] for k in range(4))
        d_refs, nm_refs, nv_refs = (refs[(4 + k) * n:(5 + k) * n] for k in range(3))
        for k in range(n):
            delta, nm, nv = _adamw_math(w_refs[k][...], g_refs[k][...], m_refs[k][...], v_refs[k][...])
            d_refs[k][...] = delta
            nm_refs[k][...] = nm
            nv_refs[k][...] = nv

    def whole(arr):
        return pl.BlockSpec(arr.shape, lambda i, nd=arr.ndim: (0,) * nd)

    specs = [whole(w) for w in ws]
    res = pl.pallas_call(
        body, grid=(1,), in_specs=specs * 4, out_specs=specs * 3,
        out_shape=[jax.ShapeDtypeStruct(w.shape, _F32) for w in ws] * 3,
        compiler_params=_cparams("arbitrary"), name=name,
    )(*gs, *ws, *ms, *vs)
    return res[:n], res[n:2 * n], res[2 * n:]


def _pack(arrs):
    flat = jnp.concatenate([a.reshape(-1).astype(_F32) for a in arrs])
    pad = (-flat.shape[0]) % 1024
    return jnp.pad(flat, (0, pad)).reshape(-1, 128)


def _unpack(flat2d, shapes):
    flat = flat2d.reshape(-1)
    out, off = [], 0
    for sh in shapes:
        size = 1
        for dim in sh:
            size *= dim
        out.append(flat[off:off + size].reshape(sh))
        off += size
    return out


def _block_diag(w):
    h, hd, _ = w.shape
    eye = jnp.eye(h, dtype=w.dtype)
    return (eye[:, None, :, None] * w[:, :, None, :]).reshape(h * hd, h * hd)


def _diag_blocks(full, h):
    hd = full.shape[0] // h
    return jnp.stack([full[i * hd:(i + 1) * hd, i * hd:(i + 1) * hd] for i in range(h)])


def kernel(x, mem, mix_norm_g, w_in, lru_conv_w, lru_conv_b, lru_w_a, lru_b_a, lru_w_x, lru_b_x, lru_lambda, conf_conv_w, conf_conv_b, conf_ln_g, conf_ln_b, w_out, xa_norm_g, mem_norm_g, w_q, w_kv, w_o, ffn_norm_g, w_up, ffn_conv_w, ffn_conv_b, w_down, final_norm_g, loss_target, m_mix_norm_g, m_w_in, m_lru_conv_w, m_lru_conv_b, m_lru_w_a, m_lru_b_a, m_lru_w_x, m_lru_b_x, m_lru_lambda, m_conf_conv_w, m_conf_conv_b, m_conf_ln_g, m_conf_ln_b, m_w_out, m_xa_norm_g, m_mem_norm_g, m_w_q, m_w_kv, m_w_o, m_ffn_norm_g, m_w_up, m_ffn_conv_w, m_ffn_conv_b, m_w_down, m_final_norm_g, v_mix_norm_g, v_w_in, v_lru_conv_w, v_lru_conv_b, v_lru_w_a, v_lru_b_a, v_lru_w_x, v_lru_b_x, v_lru_lambda, v_conf_conv_w, v_conf_conv_b, v_conf_ln_g, v_conf_ln_b, v_w_out, v_xa_norm_g, v_mem_norm_g, v_w_q, v_w_kv, v_w_o, v_ffn_norm_g, v_w_up, v_ffn_conv_w, v_ffn_conv_b, v_w_down, v_final_norm_g):
    names = ["mix_norm_g", "w_in", "lru_conv_w", "lru_conv_b", "lru_w_a", "lru_b_a", "lru_w_x", "lru_b_x",
             "lru_lambda", "conf_conv_w", "conf_conv_b", "conf_ln_g", "conf_ln_b", "w_out", "xa_norm_g",
             "mem_norm_g", "w_q", "w_kv", "w_o", "ffn_norm_g", "w_up", "ffn_conv_w", "ffn_conv_b", "w_down",
             "final_norm_g"]
    loc = locals()
    W = {n: loc[n] for n in names}
    M = {n: loc["m_" + n] for n in names}
    V = {n: loc["v_" + n] for n in names}
    big = ["w_in", "w_out", "w_q", "w_kv", "w_o", "w_up", "w_down"]
    conv_sharded = ["lru_conv_w", "conf_conv_w", "ffn_conv_w"]

    xs = x[0]
    mems = mem[0]
    tgt = loss_target[0]
    me = 4 * lax.axis_index("x") + 2 * lax.axis_index("y") + lax.axis_index("c")

    conv_shapes = [W[n].shape[1:] for n in conv_sharded]
    conv_pack = _pack([W[n][0] for n in conv_sharded])
    shard = {n: W[n][0].astype(_XFER) for n in big}
    h1, (g_in, g_out, g_conv) = _rms_fwd(
        xs, mix_norm_g, "rms1_fwd", comm=_Gather([shard["w_in"], shard["w_out"], conv_pack]))
    convs = [[] for _ in conv_sharded]
    for j in range(_NDEV):
        for idx, part in enumerate(_unpack(g_conv[j], conv_shapes)):
            convs[idx].append(part)
    lcw, ccw, fcw = [jnp.concatenate(parts, axis=-1) for parts in convs]

    wab = jnp.concatenate([_block_diag(lru_w_a[0]), _block_diag(lru_w_x[0])], axis=1).astype(_MXU)
    mixer_params = (lcw, lru_conv_b, wab, lru_b_a, lru_b_x, lru_lambda, ccw, conf_conv_b, conf_ln_g, conf_ln_b)

    w_out_f = g_out.reshape(-1, g_out.shape[-1])
    (z, ycat, hs, cc, x1, h2), (g_q, g_kv, g_o, g_up, g_down) = _mixer_fwd(
        xs, h1, g_in, w_out_f, xa_norm_g, *mixer_params, "mixer_fwd",
        comm=_Gather([shard[n] for n in ("w_q", "w_kv", "w_o", "w_up", "w_down")]))
    w_q_f = g_q.reshape(-1, g_q.shape[-1])
    w_o_f = g_o.reshape(-1, g_o.shape[-1])
    w_down_f = g_down.reshape(-1, g_down.shape[-1])
    row32, row16, vec32 = (_F32, "row"), (_MXU, "row"), (_F32, "vec")
    mn = _rms_fwd(mems, mem_norm_g, "rmsm_fwd")
    kv = _mm_nn_stacked(mn, g_kv, _MXU, "mm_kv_fwd")
    q, o, x2, h3 = _attn_fwd(h2, x1, w_q_f, kv, w_o_f, ffn_norm_g, "attn_fwd")

    gfin = final_norm_g.reshape(1, -1)
    g0, act, gelu_g, u_dgelu, dx3, dx3b, lvec, dg_final = _ffn_fused_fwd(
        h3, g_up, w_down_f, fcw, ffn_conv_b, x2, tgt, gfin, "ffn_fwd")

    def rows8(p):
        return p.reshape(_NDEV, p.shape[0] // _NDEV, p.shape[1])

    p_down = _mm_tn_nat(act, dx3b, _XFER, "mm_down_wgrad", ts=2048)
    (dgu, dx2, dx2b, dg_ffn, dfcw, dfcb), _ = _ffn_fused_bwd(
        dx3b, g0, gelu_g, u_dgelu, w_down_f, g_up, fcw, x2, ffn_norm_g, dx3, "ffn_bwd")
    p_up = _mm_tn_stacked(h3, dgu, _NDEV, _XFER, "mm_up_wgrad", slabs=1, ts=4096)

    (dx1, dx1b, dg_xa, dk, dv, p_o, p_q), (r_up,) = _attn_bwd(
        dx2b, dx2, q, o, h2, x1, kv, w_o_f, w_q_f, xa_norm_g, "attn_bwd", comm=_Exchange([p_up]))
    dkv = jnp.concatenate([dk, dv], axis=1).astype(_MXU)
    dmn = _mm_nt_stacked(dkv, g_kv, "mm_kv_dgrad", outs=[row32], slabs=_NDEV)
    p_kv = _mm_tn_stacked(mn, dkv, _NDEV, _XFER, "mm_kv_wgrad", slabs=_NDEV)
    _, _, dg_mem = _rms_bwd(dmn, mems, mem_norm_g, None, "rmsm_bwd")

    p_out = _mm_tn_nat(ycat, dx1b, _XFER, "mm_out_wgrad", ts=2048)
    ((dz, p_in, dlcw, dlcb, dwab, dba, dbx, dlam, dccw, dccb, dlng, dlnb),
     (r_down, r_o, r_q, r_kv, r_out)) = _mixer_bwd(
        dx1b, w_out_f, h1, z, hs, cc, *mixer_params, "mixer_bwd",
        comm=_Exchange([rows8(p_down), rows8(p_o), rows8(p_q), p_kv, rows8(p_out)]))

    c = _D_LRU
    heads = lru_w_a.shape[1]
    small_partial = {
        "lru_conv_w": dlcw, "lru_conv_b": dlcb,
        "lru_w_a": _diag_blocks(dwab[:, :c], heads), "lru_b_a": dba,
        "lru_w_x": _diag_blocks(dwab[:, c:], heads), "lru_b_x": dbx, "lru_lambda": dlam,
        "conf_conv_w": dccw, "conf_conv_b": dccb, "conf_ln_g": dlng, "conf_ln_b": dlnb,
        "xa_norm_g": dg_xa, "mem_norm_g": dg_mem, "ffn_norm_g": dg_ffn,
        "ffn_conv_w": dfcw, "ffn_conv_b": dfcb, "final_norm_g": dg_final,
    }
    early = list(small_partial)
    early_shapes = [small_partial[n].shape for n in early] + [lvec.shape]
    (grad_x, dg_mix), (r_in, early_all, mix_all) = _mm_nt_stacked(
        dz, g_in, "mm_in_dgrad", epi=_epi_rms_bwd, extra=[(xs, "row"), (mix_norm_g, "vec"), (dx1, "row")],
        outs=[row32, vec32], tm=512, slabs=_NDEV,
        comm=_Several([_Exchange([p_in]), _Gather([_pack([small_partial[n] for n in early] + [lvec])]),
                       _GatherOfResult(1, mix_norm_g.shape, _F32)]))
    small = early + ["mix_norm_g"]
    small_sum = _unpack(_sum8(early_all, "sum_small_grads"), early_shapes)
    loss = 0.5 * jnp.sum(small_sum.pop()) / xs.shape[1]
    small_sum.append(_sum8(mix_all.reshape(_NDEV, 8, -1), "sum_mix_grad").reshape(dg_mix.shape))
    received = {"w_in": r_in, "w_out": r_out, "w_q": r_q, "w_kv": r_kv, "w_o": r_o, "w_up": r_up,
                "w_down": r_down}

    grads, deltas, new_m, new_v = {}, {}, {}, {}
    for n, rec in ((n, received[n]) for n in big):
        shp = W[n].shape
        w2, m2, v2 = (t.reshape(shp[1:]) for t in (W[n], M[n], V[n]))
        outs = _sum_adamw(rec, w2, m2, v2, "adamw_" + n)
        grads[n], deltas[n], new_m[n], new_v[n] = (t.reshape(shp) for t in outs)

    small_g = []
    for n, g in zip(small, small_sum):
        if n in conv_sharded:
            width = W[n].shape[-1]
            g = lax.dynamic_slice_in_dim(g, me * width, width, axis=1)
        small_g.append(g.reshape(W[n].shape))

    def at_least_2d(a):
        return a.reshape(1, -1) if a.ndim == 1 else a

    sd, sm, sv = _adamw_many([at_least_2d(g) for g in small_g], [at_least_2d(W[n]) for n in small],
                             [at_least_2d(M[n]) for n in small], [at_least_2d(V[n]) for n in small], "adamw_small")
    for n, g, d_, m_, v_ in zip(small, small_g, sd, sm, sv):
        shp = W[n].shape
        grads[n], deltas[n], new_m[n], new_v[n] = g, d_.reshape(shp), m_.reshape(shp), v_.reshape(shp)

    return (loss, grad_x[None], *[grads[n] for n in names], *[deltas[n] for n in names],
            *[new_m[n] for n in names], *[new_v[n] for n in names])
```
